```python
import math
import jax, jax.numpy as jnp
from jax import lax
import numpy as np

D_MODEL = 1024
BATCH = 16
SEQ = 256
DEPTH = 2
DEC_BATCH = 8
DEC_SEQ = 4096
PAST_LEN = 512

GRID_W = 64
MIX_WIDTH = D_MODEL
GROUP_WIDTH = MIX_WIDTH // 4
MLA_HEADS = 4
MLA_NOPE = 64
MLA_ROPE = 32
MLA_V = 64
MLA_Q_RANK = 192
MLA_KV_RANK = 128
LRU_WIDTH = GROUP_WIDTH
LRU_BLOCKS = 4
LRU_CONV = 4
LRU_C = 8.0
POOL_WINDOWS = (2, 4, 8, 16)
POOL_CH = GROUP_WIDTH // len(POOL_WINDOWS)
DIFF_HEADS = 4
DIFF_DIM = GROUP_WIDTH // (2 * DIFF_HEADS)
FF_HIDDEN = -(-8 * D_MODEL // (3 * 256)) * 256
ROPE_BASE = 10000.0
Q_BLOCK = 128
EPS = 1e-6
MLA_IN = MLA_Q_RANK + MLA_KV_RANK + MLA_ROPE
LRU_IN = 2 * LRU_WIDTH
POOL_IN = GROUP_WIDTH
DIFF_QK = DIFF_HEADS * 2 * DIFF_DIM
DIFF_IN = 2 * DIFF_QK + DIFF_HEADS * 2 * DIFF_DIM
IN_COLS = MLA_IN + LRU_IN + POOL_IN + DIFF_IN

kernel_name = 'hybrid_diffusion_prefix_step'


def _rms(x, g):
    xf = x.astype(jnp.float32)
    y = xf * lax.rsqrt(jnp.mean(xf * xf, axis=-1, keepdims=True) + EPS)
    return (y * g.astype(jnp.float32)).astype(x.dtype)


def _axial_rope(n, rot_dim):
    rows = n // GRID_W
    row = jnp.repeat(jnp.arange(rows), GRID_W).astype(jnp.float32)
    col = jnp.tile(jnp.arange(GRID_W), rows).astype(jnp.float32)
    quarter = rot_dim // 4
    inv = ROPE_BASE ** (-jnp.arange(quarter, dtype=jnp.float32) / quarter)
    ang = jnp.concatenate([row[:, None] * inv, col[:, None] * inv], axis=-1)
    return jnp.cos(ang), jnp.sin(ang)


def _rope(x, cs):
    cos, sin = cs
    half = x.shape[-1] // 2
    shape = (1, x.shape[1]) + (1,) * (x.ndim - 3) + (half,)
    c = cos.reshape(shape).astype(x.dtype)
    s = sin.reshape(shape).astype(x.dtype)
    x1, x2 = x[..., :half], x[..., half:]
    return jnp.concatenate([x1 * c - x2 * s, x1 * s + x2 * c], axis=-1)


def _blockwise(q, fn):
    B, N = q.shape[:2]
    blk = min(Q_BLOCK, N)
    nb = N // blk
    qb = jnp.moveaxis(q.reshape((B, nb, blk) + q.shape[2:]), 1, 0)
    out = lax.map(fn, qb)
    out = jnp.moveaxis(out, 0, 1)
    return out.reshape((B, N) + out.shape[3:])


def _softmax_attention(q, k, v, scale):
    def blk(qb):
        s = jnp.einsum('bqhe,bkhe->bhqk', qb, k).astype(jnp.float32) * scale
        p = jax.nn.softmax(s, axis=-1).astype(v.dtype)
        return jnp.einsum('bhqk,bkhf->bqhf', p, v)
    return _blockwise(q, blk)


def _mla(u, lp, rope_cs, ctx):
    B, N, _ = u.shape
    c_q = u[..., :MLA_Q_RANK]
    c_kv = u[..., MLA_Q_RANK:MLA_Q_RANK + MLA_KV_RANK]
    k_r = u[..., MLA_Q_RANK + MLA_KV_RANK:]
    q = (_rms(c_q, lp['mla_q_norm_g']) @ lp['mla_w_uq']).reshape(B, N, MLA_HEADS, MLA_NOPE + MLA_ROPE)
    q_nope, q_rope = q[..., :MLA_NOPE], q[..., MLA_NOPE:]
    kv_lat = _rms(c_kv, lp['mla_kv_norm_g'])
    if ctx is None:
        lat_all, kr_all = kv_lat, k_r
    else:
        q_rope = _rope(q_rope, rope_cs)
        lat_all = jnp.concatenate([ctx[0], kv_lat], axis=1)
        kr_all = jnp.concatenate([ctx[1], _rope(k_r, rope_cs)], axis=1)
    K = lat_all.shape[1]
    kv = (lat_all @ lp['mla_w_ukv']).reshape(B, K, MLA_HEADS, MLA_NOPE + MLA_V)
    k = jnp.concatenate([kv[..., :MLA_NOPE],
                         jnp.broadcast_to(kr_all[:, :, None, :], (B, K, MLA_HEADS, MLA_ROPE))], axis=-1)
    v = kv[..., MLA_NOPE:]
    qf = jnp.concatenate([q_nope, q_rope], axis=-1)
    o = _softmax_attention(qf, k, v, 1.0 / math.sqrt(MLA_NOPE + MLA_ROPE))
    return o.reshape(B, N, MLA_HEADS * MLA_V), (kv_lat, k_r)


def _conv_centred(x, w, b):
    N = x.shape[1]
    xp = jnp.pad(x, ((0, 0), (1, LRU_CONV - 2), (0, 0)))
    y = b
    for j in range(LRU_CONV):
        y = y + xp[:, j:j + N] * w[j]
    return y


def _lin_combine(e1, e2):
    a1, b1 = e1
    a2, b2 = e2
    return a1 * a2, a2 * b1 + b2


def _rglru(u, lp, h0):
    B, N, _ = u.shape
    xb, gb = u[..., :LRU_WIDTH], u[..., LRU_WIDTH:]
    xc = _conv_centred(xb, lp['lru_conv_w'], lp['lru_conv_b'])
    xg = xc.reshape(B, N, LRU_BLOCKS, LRU_WIDTH // LRU_BLOCKS)

    def gate(w, bias):
        z = jnp.einsum('bngc,zgce->zbnge', xg, w).reshape(2, B, N, LRU_WIDTH) + bias[:, None, None, :]
        return jax.nn.sigmoid(z.astype(jnp.float32))

    r = gate(lp['lru_w_r'], lp['lru_b_r'])
    i = gate(lp['lru_w_i'], lp['lru_b_i'])
    log_a = -LRU_C * r * jax.nn.softplus(-lp['lru_lambda'].astype(jnp.float32))[:, None, None, :]
    a = jnp.exp(log_a)
    bt = jnp.sqrt(1.0 - jnp.exp(2.0 * log_a)) * i * xc.astype(jnp.float32)[None]
    a = jnp.stack([a[0], jnp.flip(a[1], axis=1)])
    bt = jnp.stack([bt[0], jnp.flip(bt[1], axis=1)])
    if h0 is not None:
        bt = bt.at[:, :, 0].add(a[:, :, 0] * jnp.moveaxis(h0, 1, 0).astype(jnp.float32))
    _, h = lax.associative_scan(_lin_combine, (a, bt), axis=2)
    y = (h[0] + jnp.flip(h[1], axis=1)).astype(u.dtype) * jax.nn.gelu(gb)
    final = jnp.moveaxis(h[:, :, -1], 0, 1).astype(u.dtype) if h0 is None else None
    return y, final


def _pool(u, lp):
    B, N, _ = u.shape
    uf = u.astype(jnp.float32)
    cs = jnp.concatenate([jnp.zeros((B, 1, POOL_IN), jnp.float32), jnp.cumsum(uf, axis=1)], axis=1)
    t = jnp.arange(N)
    outs = []
    for g, w in enumerate(POOL_WINDOWS):
        lo = jnp.clip(t - w // 2, 0, N)
        hi = jnp.clip(t + w // 2, 0, N)
        seg = cs[:, :, g * POOL_CH:(g + 1) * POOL_CH]
        mean = (seg[:, hi] - seg[:, lo]) / (hi - lo).astype(jnp.float32)[None, :, None]
        outs.append(mean - uf[..., g * POOL_CH:(g + 1) * POOL_CH])
    d = jnp.stack(outs, axis=2).astype(u.dtype)
    y = jnp.einsum('bngc,gce->bnge', d, lp['pool_w']).reshape(B, N, POOL_IN)
    return y * lp['pool_scale']


def _diff(u, lp, layer_idx, rope_cs, ctx):
    B, N, _ = u.shape
    q = u[..., :DIFF_QK].reshape(B, N, DIFF_HEADS, 2, DIFF_DIM)
    k = u[..., DIFF_QK:2 * DIFF_QK].reshape(B, N, DIFF_HEADS, 2, DIFF_DIM)
    v = u[..., 2 * DIFF_QK:].reshape(B, N, DIFF_HEADS, 2 * DIFF_DIM)
    if ctx is None:
        k_all, v_all = k, v
    else:
        q = _rope(q, rope_cs)
        k_all = jnp.concatenate([ctx[0], _rope(k, rope_cs)], axis=1)
        v_all = jnp.concatenate([ctx[1], v], axis=1)
    lam_init = 0.8 - 0.6 * math.exp(-0.3 * layer_idx)
    lv = lp['diff_lambda'].astype(jnp.float32)
    lam = jnp.exp(jnp.sum(lv[0] * lv[1])) - jnp.exp(jnp.sum(lv[2] * lv[3])) + lam_init
    scale = 1.0 / math.sqrt(DIFF_DIM)

    def blk(qb):
        s = jnp.einsum('bqhce,bkhce->bhcqk', qb, k_all).astype(jnp.float32) * scale
        p = jax.nn.softmax(s, axis=-1)
        att = (p[:, :, 0] - lam * p[:, :, 1]).astype(v_all.dtype)
        return jnp.einsum('bhqk,bkhf->bqhf', att, v_all)

    o = _blockwise(q, blk)
    o = _rms(o, lp['diff_norm_g']) * (1.0 - lam_init)
    return o.reshape(B, N, DIFF_HEADS * 2 * DIFF_DIM), (k, v)


def _layer(x, cond, lp, layer_idx, rope_mla, rope_diff, ctx):
    mod = (jax.nn.silu(cond) @ lp['w_ada'] + lp['b_ada'])[:, None, :]
    sh1, sc1, g1, sh2, sc2, g2 = jnp.split(mod, 6, axis=-1)
    h = _rms(x, lp['norm1_g']) * (1.0 + sc1) + sh1
    u = h @ lp['w_in']
    o1 = MLA_IN
    o2 = o1 + LRU_IN
    o3 = o2 + POOL_IN
    y_mla, (ckv, kr) = _mla(u[..., :o1], lp, rope_mla, None if ctx is None else ctx[0:2])
    y_lru, st = _rglru(u[..., o1:o2], lp, None if ctx is None else ctx[4])
    y_pool = _pool(u[..., o2:o3], lp)
    y_diff, (dk, dv) = _diff(u[..., o3:], lp, layer_idx, rope_diff, None if ctx is None else ctx[2:4])
    mix = jnp.concatenate([y_mla, y_lru, y_pool, y_diff], axis=-1) @ lp['w_out']
    x = x + g1 * mix
    h = _rms(x, lp['norm2_g']) * (1.0 + sc2) + sh2
    gu = h @ lp['w_gu']
    x = x + g2 * ((jax.nn.silu(gu[..., :FF_HIDDEN]) * gu[..., FF_HIDDEN:]) @ lp['w_down'])
    return x, (ckv, kr, dk, dv, st)


def setup_inputs(seed: int = 0) -> dict:
    key = jax.random.key(seed)
    ks = list(jax.random.split(key, 40))

    def nrm(idx, shape, s=1.0):
        return jax.random.normal(ks[idx], shape, jnp.float32) * s

    a_c = jax.random.uniform(ks[30], (DEPTH, 2, LRU_WIDTH), jnp.float32, minval=0.9, maxval=0.999)
    a0 = a_c ** (1.0 / LRU_C)
    lru_lambda = jnp.log(a0) - jnp.log1p(-a0)
    return {
        'x_prompt': nrm(0, (BATCH, SEQ, D_MODEL)),
        'x_sample': nrm(1, (DEC_BATCH, DEC_SEQ, D_MODEL)),
        'cache_mla_ckv': nrm(2, (DEC_BATCH, DEPTH, PAST_LEN, MLA_KV_RANK)),
        'cache_mla_krope': nrm(3, (DEC_BATCH, DEPTH, PAST_LEN, MLA_ROPE)),
        'cache_diff_k': nrm(4, (DEC_BATCH, DEPTH, PAST_LEN, DIFF_HEADS, 2, DIFF_DIM)),
        'cache_diff_v': nrm(5, (DEC_BATCH, DEPTH, PAST_LEN, DIFF_HEADS, 2 * DIFF_DIM)),
        'state_lru': nrm(6, (DEC_BATCH, DEPTH, 2, LRU_WIDTH), 0.5),
        'c': nrm(7, (DEC_BATCH, D_MODEL)),
        'c_ctx': nrm(8, (D_MODEL,)),
        'w_ada': nrm(9, (DEPTH, D_MODEL, 6 * D_MODEL), 0.5 * D_MODEL ** -0.5),
        'b_ada': nrm(10, (DEPTH, 6 * D_MODEL), 0.01),
        'norm1_g': 1.0 + nrm(11, (DEPTH, D_MODEL), 0.01),
        'norm2_g': 1.0 + nrm(12, (DEPTH, D_MODEL), 0.01),
        'w_in': nrm(13, (DEPTH, D_MODEL, IN_COLS), D_MODEL ** -0.5),
        'mla_q_norm_g': 1.0 + nrm(14, (DEPTH, MLA_Q_RANK), 0.01),
        'mla_w_uq': nrm(15, (DEPTH, MLA_Q_RANK, MLA_HEADS * (MLA_NOPE + MLA_ROPE)), MLA_Q_RANK ** -0.5),
        'mla_kv_norm_g': 1.0 + nrm(16, (DEPTH, MLA_KV_RANK), 0.01),
        'mla_w_ukv': nrm(17, (DEPTH, MLA_KV_RANK, MLA_HEADS * (MLA_NOPE + MLA_V)), MLA_KV_RANK ** -0.5),
        'lru_conv_w': nrm(18, (DEPTH, LRU_CONV, LRU_WIDTH), LRU_CONV ** -0.5),
        'lru_conv_b': nrm(19, (DEPTH, LRU_WIDTH), 0.01),
        'lru_w_r': nrm(20, (DEPTH, 2, LRU_BLOCKS, LRU_WIDTH // LRU_BLOCKS, LRU_WIDTH // LRU_BLOCKS), (LRU_WIDTH // LRU_BLOCKS) ** -0.5),
        'lru_b_r': nrm(21, (DEPTH, 2, LRU_WIDTH), 0.01),
        'lru_w_i': nrm(22, (DEPTH, 2, LRU_BLOCKS, LRU_WIDTH // LRU_BLOCKS, LRU_WIDTH // LRU_BLOCKS), (LRU_WIDTH // LRU_BLOCKS) ** -0.5),
        'lru_b_i': nrm(23, (DEPTH, 2, LRU_WIDTH), 0.01),
        'lru_lambda': lru_lambda,
        'pool_w': nrm(24, (DEPTH, len(POOL_WINDOWS), POOL_CH, POOL_CH), POOL_CH ** -0.5),
        'pool_scale': 1.0 + nrm(25, (DEPTH, POOL_IN), 0.1),
        'diff_lambda': nrm(26, (DEPTH, 4, DIFF_DIM), 0.1),
        'diff_norm_g': 1.0 + nrm(27, (DEPTH, 2 * DIFF_DIM), 0.01),
        'w_out': nrm(28, (DEPTH, MIX_WIDTH, D_MODEL), MIX_WIDTH ** -0.5),
        'w_gu': nrm(29, (DEPTH, D_MODEL, 2 * FF_HIDDEN), D_MODEL ** -0.5),
        'w_down': nrm(31, (DEPTH, FF_HIDDEN, D_MODEL), FF_HIDDEN ** -0.5),
        'final_norm_g': 1.0 + nrm(32, (D_MODEL,), 0.01),
    }


def reference(x_prompt, x_sample, cache_mla_ckv, cache_mla_krope, cache_diff_k, cache_diff_v, state_lru,
              c, c_ctx, w_ada, b_ada, norm1_g, norm2_g, w_in, mla_q_norm_g, mla_w_uq, mla_kv_norm_g,
              mla_w_ukv, lru_conv_w, lru_conv_b, lru_w_r, lru_b_r, lru_w_i, lru_b_i, lru_lambda, pool_w,
              pool_scale, diff_lambda, diff_norm_g, w_out, w_gu, w_down, final_norm_g):
    stacked = {
        'w_ada': w_ada, 'b_ada': b_ada, 'norm1_g': norm1_g, 'norm2_g': norm2_g, 'w_in': w_in,
        'mla_q_norm_g': mla_q_norm_g, 'mla_w_uq': mla_w_uq, 'mla_kv_norm_g': mla_kv_norm_g,
        'mla_w_ukv': mla_w_ukv, 'lru_conv_w': lru_conv_w, 'lru_conv_b': lru_conv_b,
        'lru_w_r': lru_w_r, 'lru_b_r': lru_b_r, 'lru_w_i': lru_w_i, 'lru_b_i': lru_b_i,
        'lru_lambda': lru_lambda, 'pool_w': pool_w, 'pool_scale': pool_scale,
        'diff_lambda': diff_lambda, 'diff_norm_g': diff_norm_g, 'w_out': w_out,
        'w_gu': w_gu, 'w_down': w_down,
    }
    n_lat = x_sample.shape[1]
    rope_mla = _axial_rope(n_lat, MLA_ROPE)
    rope_diff = _axial_rope(n_lat, DIFF_DIM)
    cond_ctx = c_ctx[None, :]
    xp, xs = x_prompt, x_sample
    ckv_l, kr_l, dk_l, dv_l, st_l = [], [], [], [], []
    for l in range(DEPTH):
        lp = {name: arr[l] for name, arr in stacked.items()}
        xp, (ckv, kr, dk, dv, st) = _layer(xp, cond_ctx, lp, l, None, None, None)
        ckv_l.append(ckv)
        kr_l.append(kr)
        dk_l.append(dk)
        dv_l.append(dv)
        st_l.append(st)
        ctx = (cache_mla_ckv[:, l], cache_mla_krope[:, l], cache_diff_k[:, l], cache_diff_v[:, l], state_lru[:, l])
        xs, _ = _layer(xs, c, lp, l, rope_mla, rope_diff, ctx)
    y_prompt = _rms(xp, final_norm_g)
    y_sample = _rms(xs, final_norm_g)
    new_mla_ckv = jnp.stack(ckv_l, axis=1)
    new_mla_krope = jnp.stack(kr_l, axis=1)
    new_diff_k = jnp.stack(dk_l, axis=1)
    new_diff_v = jnp.stack(dv_l, axis=1)
    new_state_lru = jnp.stack(st_l, axis=1)
    return (y_prompt, y_sample, new_mla_ckv, new_mla_krope, new_diff_k, new_diff_v, new_state_lru)
```

```python
import functools
import math

import jax
import jax.numpy as jnp
from jax import lax
from jax.experimental import pallas as pl
from jax.experimental.pallas import tpu as pltpu

F32 = jnp.float32
BF16 = jnp.bfloat16

D_MODEL = 1024
DEPTH = 2
GRID_W = 64
GROUP_WIDTH = 256
MLA_HEADS = 4
MLA_NOPE = 64
MLA_ROPE = 32
MLA_V = 64
MLA_Q_RANK = 192
MLA_KV_RANK = 128
MLA_SLOT = 128
LRU_WIDTH = 256
LRU_C = 8.0
POOL_WINDOWS = (2, 4, 8, 16)
POOL_CH = 64
DIFF_HEADS = 4
DIFF_DIM = 32
FF_HIDDEN = 2816
ROPE_BASE = 10000.0
EPS = 1e-6
IN_EFF = 2048
HALO = 8

VMEM_LIMIT_BYTES = 56 * 1024 * 1024

_NT = (((1,), (1,)), ((), ()))


def _params(*sem):
    return pltpu.CompilerParams(dimension_semantics=sem, vmem_limit_bytes=VMEM_LIMIT_BYTES)


def _dot(a, b):
    return jnp.dot(a, b, preferred_element_type=F32)


def _dot_nt(a, b):
    return lax.dot_general(a, b, _NT, preferred_element_type=F32)


def _rms_rows(x, width):
    ms = jnp.sum(x * x, axis=-1, keepdims=True) * (1.0 / width)
    return x * lax.rsqrt(ms + EPS)


def _ada_kernel(cond_ref, w_ref, b_ref, out_ref):
    c = cond_ref[...]
    s = c * jax.nn.sigmoid(c)
    out_ref[0] = _dot(s.astype(BF16), w_ref[0].astype(BF16)) + b_ref[0]


def _ada(cond_all, w_ada, b_ada):
    rows = cond_all.shape[0]
    tn = 1536
    return pl.pallas_call(
        _ada_kernel,
        grid=(DEPTH, 6 * D_MODEL // tn),
        in_specs=[
            pl.BlockSpec((rows, D_MODEL), lambda l, j: (0, 0)),
            pl.BlockSpec((1, D_MODEL, tn), lambda l, j: (l, 0, j)),
            pl.BlockSpec((1, 1, tn), lambda l, j: (l, 0, j)),
        ],
        out_specs=pl.BlockSpec((1, rows, tn), lambda l, j: (l, 0, j)),
        out_shape=jax.ShapeDtypeStruct((DEPTH, rows, 6 * D_MODEL), F32),
        compiler_params=_params("arbitrary", "arbitrary"),
        name="ada_mod",
    )(cond_all, w_ada, b_ada.reshape(DEPTH, 1, 6 * D_MODEL))


def _inproj_kernel(x_ref, mod_ref, g1_ref, win_ref, gq_ref, gkv_ref, wq_ref, wqr_ref, wk_ref, wv_ref,
                   cosq_ref, sinq_ref, cosk_ref, sink_ref, cosd_ref, sina_ref, sinb_ref,
                   q_out, k_out, v_out, lru_out, pool_out, dq_out, dk_out, dv_out, *cache_outs):
    x = x_ref[0]
    mod = mod_ref[0]
    sh1 = mod[:, 0:D_MODEL]
    sc1 = mod[:, D_MODEL:2 * D_MODEL]
    h = _rms_rows(x, D_MODEL) * g1_ref[...]
    hb = (h * (1.0 + sc1) + sh1).astype(BF16)

    t01 = _dot(hb, win_ref[:, 0:256])
    lane = lax.broadcasted_iota(jnp.int32, (1, 256), 1)
    cq = jnp.where(lane < MLA_Q_RANK, t01, 0.0)
    cqn = (_rms_rows(cq, MLA_Q_RANK) * gq_ref[...]).astype(BF16)
    qa = _dot(cqn, wq_ref[...])
    qr = _dot(cqn, wqr_ref[...])
    cosq = cosq_ref[...]
    sinq = sinq_ref[...]
    ckv = _dot(hb, win_ref[:, 256:384])
    lat = _rms_rows(ckv, MLA_KV_RANK) * gkv_ref[...]
    latb = lat.astype(BF16)
    kk = _dot(latb, wk_ref[...])
    v_out[0] = _dot(latb, wv_ref[...]).astype(v_out.dtype)
    t1 = t01[:, 128:256]
    t3 = _dot(hb, win_ref[:, 384:512])
    kro = t1 * cosk_ref[...] + t3 * sink_ref[...]
    for hh in range(MLA_HEADS):
        sl = slice(hh * MLA_SLOT, (hh + 1) * MLA_SLOT)
        q_out[0, hh] = (qa[:, sl] * cosq + qr[:, sl] * sinq).astype(q_out.dtype)
        k_out[0, hh] = (kk[:, sl] + kro).astype(k_out.dtype)

    lru_out[0] = _dot(hb, win_ref[:, 512:1024])
    pool_out[0] = _dot(hb, win_ref[:, 1024:1280])

    cosd = cosd_ref[...]
    sina = sina_ref[...]
    sinb = sinb_ref[...]

    def rope(t):
        return t * cosd + pltpu.roll(t, 256 - 16, 1) * sina + pltpu.roll(t, 16, 1) * sinb

    dq = _dot(hb, win_ref[:, 1280:1536])
    dk = _dot(hb, win_ref[:, 1536:1792])
    dv = _dot(hb, win_ref[:, 1792:2048])
    dq_out[0] = (rope(dq) * (1.0 / math.sqrt(DIFF_DIM))).astype(dq_out.dtype)
    dk_out[0] = rope(dk).astype(dk_out.dtype)
    dv_out[0] = dv.astype(dv_out.dtype)

    if cache_outs:
        lat_out, kr_out = cache_outs
        lat_out[0] = lat
        kr_out[0] = t1


def _inproj(x, mod, lw, tabs, *, tm, emit_cache):
    B, N, _ = x.shape
    shared_mod = mod.shape[0] == 1
    kv_dtype = F32 if emit_cache else BF16

    def tok(width):
        return pl.BlockSpec((1, tm, width), lambda j, b: (b, j, 0))

    def const2(shape):
        return pl.BlockSpec(shape, lambda j, b: (0, 0))

    def tab(width):
        return pl.BlockSpec((tm, width), lambda j, b: (j, 0))

    head = pl.BlockSpec((1, MLA_HEADS, tm, MLA_SLOT), lambda j, b: (b, 0, j, 0))
    mod_spec = pl.BlockSpec((1, 1, 6 * D_MODEL),
                            (lambda j, b: (0, 0, 0)) if shared_mod else (lambda j, b: (b, 0, 0)))
    in_specs = [
        tok(D_MODEL), mod_spec, const2((1, D_MODEL)), const2((D_MODEL, IN_EFF)),
        const2((1, 256)), const2((1, 128)), const2((256, 512)), const2((256, 512)),
        const2((128, 512)), const2((128, 256)),
        tab(128), tab(128), tab(128), tab(128), tab(256), tab(256), tab(256),
    ]
    out_specs = [head, head, tok(256), tok(512), tok(256), tok(256), tok(256), tok(256)]
    out_shape = [
        jax.ShapeDtypeStruct((B, MLA_HEADS, N, MLA_SLOT), BF16),
        jax.ShapeDtypeStruct((B, MLA_HEADS, N, MLA_SLOT), BF16),
        jax.ShapeDtypeStruct((B, N, 256), BF16),
        jax.ShapeDtypeStruct((B, N, 512), F32),
        jax.ShapeDtypeStruct((B, N, 256), F32),
        jax.ShapeDtypeStruct((B, N, 256), BF16),
        jax.ShapeDtypeStruct((B, N, 256), kv_dtype),
        jax.ShapeDtypeStruct((B, N, 256), kv_dtype),
    ]
    if emit_cache:
        out_specs += [tok(128), tok(128)]
        out_shape += [jax.ShapeDtypeStruct((B, N, 128), F32), jax.ShapeDtypeStruct((B, N, 128), F32)]
    return pl.pallas_call(
        _inproj_kernel,
        grid=(N // tm, B),
        in_specs=in_specs,
        out_specs=out_specs,
        out_shape=out_shape,
        compiler_params=_params("arbitrary", "arbitrary"),
        name="inproj_cache" if emit_cache else "inproj",
    )(x, mod, lw["g1"], lw["w_in"], lw["gq"], lw["gkv"], lw["wq"], lw["wqr"], lw["wk"], lw["wv"],
      tabs["cosq"], tabs["sinq"], tabs["cosk"], tabs["sink"], tabs["cosd"], tabs["sina"], tabs["sinb"])


def _mla_ctx_kernel(ckv_ref, kr_ref, wk_ref, wv_ref, k_out, v_out):
    latb = ckv_ref[0].astype(BF16)
    kk = _dot(latb, wk_ref[...])
    kr = kr_ref[0]
    for hh in range(MLA_HEADS):
        k_out[0, hh] = (kk[:, hh * MLA_SLOT:(hh + 1) * MLA_SLOT] + kr).astype(k_out.dtype)
    v_out[0] = _dot(latb, wv_ref[...]).astype(v_out.dtype)


def _mla_ctx(ckv, kr_pad, lw):
    B, P, _ = ckv.shape
    return pl.pallas_call(
        _mla_ctx_kernel,
        grid=(B,),
        in_specs=[
            pl.BlockSpec((1, P, 128), lambda b: (b, 0, 0)),
            pl.BlockSpec((1, P, 128), lambda b: (b, 0, 0)),
            pl.BlockSpec((128, 512), lambda b: (0, 0)),
            pl.BlockSpec((128, 256), lambda b: (0, 0)),
        ],
        out_specs=[
            pl.BlockSpec((1, MLA_HEADS, P, MLA_SLOT), lambda b: (b, 0, 0, 0)),
            pl.BlockSpec((1, P, 256), lambda b: (b, 0, 0)),
        ],
        out_shape=[
            jax.ShapeDtypeStruct((B, MLA_HEADS, P, MLA_SLOT), BF16),
            jax.ShapeDtypeStruct((B, P, 256), BF16),
        ],
        compiler_params=_params("arbitrary"),
        name="mla_ctx",
    )(ckv, kr_pad, lw["wk"], lw["wv"])


def _mla_attn_kernel(*refs, has_ctx):
    if has_ctx:
        q_ref, k_ref, v_ref, kc_ref, vc_ref, o_ref = refs
    else:
        q_ref, k_ref, v_ref, o_ref = refs
    tq = q_ref.shape[2]
    lane = lax.broadcasted_iota(jnp.int32, (1, 256), 1)
    v = v_ref[0].astype(BF16)
    acc = jnp.zeros((tq, 256), F32)
    for hh in range(MLA_HEADS):
        q = q_ref[0, hh]
        s = _dot_nt(q, k_ref[0, hh])
        m = jnp.max(s, axis=-1, keepdims=True)
        if has_ctx:
            sc = _dot_nt(q, kc_ref[0, hh])
            m = jnp.maximum(m, jnp.max(sc, axis=-1, keepdims=True))
        e = jnp.exp(s - m)
        l = jnp.sum(e, axis=-1, keepdims=True)
        o = _dot(e.astype(BF16), v)
        if has_ctx:
            ec = jnp.exp(sc - m)
            l = l + jnp.sum(ec, axis=-1, keepdims=True)
            o = o + _dot(ec.astype(BF16), vc_ref[0])
        in_head = (lane >= hh * MLA_V) & (lane < (hh + 1) * MLA_V)
        acc = jnp.where(in_head, o * (1.0 / l), acc)
    o_ref[0] = acc


def _mla_attn(q, k, v, ctx, *, tq):
    B, H, N, S = q.shape
    in_specs = [
        pl.BlockSpec((1, H, tq, S), lambda b, j: (b, 0, j, 0)),
        pl.BlockSpec((1, H, N, S), lambda b, j: (b, 0, 0, 0)),
        pl.BlockSpec((1, N, 256), lambda b, j: (b, 0, 0)),
    ]
    args = [q, k, v]
    if ctx is not None:
        P = ctx[0].shape[2]
        in_specs += [
            pl.BlockSpec((1, H, P, S), lambda b, j: (b, 0, 0, 0)),
            pl.BlockSpec((1, P, 256), lambda b, j: (b, 0, 0)),
        ]
        args += list(ctx)
    return pl.pallas_call(
        functools.partial(_mla_attn_kernel, has_ctx=ctx is not None),
        grid=(B, N // tq),
        in_specs=in_specs,
        out_specs=pl.BlockSpec((1, tq, 256), lambda b, j: (b, j, 0)),
        out_shape=jax.ShapeDtypeStruct((B, N, 256), F32),
        compiler_params=_params("arbitrary", "arbitrary"),
        name="mla_attn_ctx" if ctx is not None else "mla_attn",
    )(*args)


def _diff_attn_kernel(*refs, has_ctx, lam_init):
    if has_ctx:
        lv_ref, g_ref, q_ref, k_ref, v_ref, kc_ref, vc_ref, o_ref = refs
    else:
        lv_ref, g_ref, q_ref, k_ref, v_ref, o_ref = refs
    tq = q_ref.shape[1]
    lv = lv_ref[...]
    lam = (jnp.exp(jnp.sum(lv[0:1] * lv[1:2], axis=-1, keepdims=True))
           - jnp.exp(jnp.sum(lv[2:3] * lv[3:4], axis=-1, keepdims=True)) + lam_init)
    lane = lax.broadcasted_iota(jnp.int32, (1, 256), 1)
    q = q_ref[0]
    k = k_ref[0].astype(BF16)
    v = v_ref[0].astype(BF16)
    if has_ctx:
        kc = kc_ref[0].astype(BF16)
        vc = vc_ref[0].astype(BF16)
    acc = jnp.zeros((tq, 256), F32)
    for hh in range(DIFF_HEADS):
        es, ecs, ws = [], [], []
        for c in range(2):
            p = 2 * hh + c
            in_pair = (lane >= p * DIFF_DIM) & (lane < (p + 1) * DIFF_DIM)
            qm = jnp.where(in_pair, q, jnp.zeros_like(q))
            s = _dot_nt(qm, k)
            m = jnp.max(s, axis=-1, keepdims=True)
            if has_ctx:
                sc = _dot_nt(qm, kc)
                m = jnp.maximum(m, jnp.max(sc, axis=-1, keepdims=True))
            e = jnp.exp(s - m)
            l = jnp.sum(e, axis=-1, keepdims=True)
            if has_ctx:
                ec = jnp.exp(sc - m)
                l = l + jnp.sum(ec, axis=-1, keepdims=True)
                ecs.append(ec)
            es.append(e)
            ws.append(1.0 / l)
        w0 = ws[0]
        w1 = lam * ws[1]
        att = (es[0] * w0 - es[1] * w1).astype(BF16)
        o = _dot(att, v)
        if has_ctx:
            attc = (ecs[0] * w0 - ecs[1] * w1).astype(BF16)
            o = o + _dot(attc, vc)
        in_head = (lane >= hh * 2 * DIFF_DIM) & (lane < (hh + 1) * 2 * DIFF_DIM)
        om = jnp.where(in_head, o, 0.0)
        ms = jnp.sum(om * om, axis=-1, keepdims=True) * (1.0 / (2 * DIFF_DIM))
        acc = jnp.where(in_head, o * lax.rsqrt(ms + EPS), acc)
    o_ref[0] = (acc * g_ref[...]) * (1.0 - lam_init)


def _diff_attn(q, k, v, ctx, lv, g4, *, tq, lam_init):
    B, N, _ = q.shape
    in_specs = [
        pl.BlockSpec((4, DIFF_DIM), lambda b, j: (0, 0)),
        pl.BlockSpec((1, 256), lambda b, j: (0, 0)),
        pl.BlockSpec((1, tq, 256), lambda b, j: (b, j, 0)),
        pl.BlockSpec((1, N, 256), lambda b, j: (b, 0, 0)),
        pl.BlockSpec((1, N, 256), lambda b, j: (b, 0, 0)),
    ]
    args = [lv, g4, q, k, v]
    if ctx is not None:
        P = ctx[0].shape[1]
        in_specs += [pl.BlockSpec((1, P, 256), lambda b, j: (b, 0, 0))] * 2
        args += list(ctx)
    return pl.pallas_call(
        functools.partial(_diff_attn_kernel, has_ctx=ctx is not None, lam_init=lam_init),
        grid=(B, N // tq),
        in_specs=in_specs,
        out_specs=pl.BlockSpec((1, tq, 256), lambda b, j: (b, j, 0)),
        out_shape=jax.ShapeDtypeStruct((B, N, 256), F32),
        compiler_params=_params("arbitrary", "arbitrary"),
        name="diff_attn_ctx" if ctx is not None else "diff_attn",
    )(*args)


def _shift_rows(v, k):
    return pltpu.roll(v, (-k) % v.shape[0], 0)


def _scan_chunk(a, b, reverse):
    T = a.shape[0]
    row = lax.broadcasted_iota(jnp.int32, a.shape, 0)
    s = 1
    while s < T:
        if reverse:
            valid = row < T - s
            ap, bp = _shift_rows(a, s), _shift_rows(b, s)
        else:
            valid = row >= s
            ap, bp = _shift_rows(a, -s), _shift_rows(b, -s)
        b = jnp.where(valid, a * bp + b, b)
        a = jnp.where(valid, a * ap, a)
        s *= 2
    return a, b


def _gelu_tanh(x):
    return x * (0.5 * (1.0 + jnp.tanh(math.sqrt(2.0 / math.pi) * (x + 0.044715 * (x * x * x)))))


def _lru_kernel(u_ref, h0_ref, cw_ref, cb_ref, wg_ref, bg_ref, lam_ref, y_ref, st_ref,
                xpad, a1s, b1s, *, N, T):
    W = LRU_WIDTH
    nc = N // T
    zeros = jnp.zeros((HALO, W), F32)
    xpad[0:HALO, :] = zeros
    xpad[N + HALO:N + 2 * HALO, :] = zeros

    def fill(j, carry):
        r0 = pl.multiple_of(j * T, T)
        xpad[pl.ds(r0 + HALO, T), :] = u_ref[0, pl.ds(r0, T), 0:W]
        return carry

    lax.fori_loop(0, nc, fill, 0)

    z = -lam_ref[...]
    sp = jnp.maximum(z, 0.0) + jnp.log1p(jnp.exp(-jnp.abs(z)))
    cw = cw_ref[...]
    cb = cb_ref[...]
    bg = bg_ref[...]

    def fwd(j, carry):
        r0 = pl.multiple_of(j * T, T)
        ext = xpad[pl.ds(r0, T + 2 * HALO), :]
        body = slice(HALO, HALO + T)
        xc = cb
        for tap in range(4):
            xc = xc + _shift_rows(ext, tap - 1)[body] * cw[tap:tap + 1]
        g = jax.nn.sigmoid(_dot(xc.astype(BF16), wg_ref[...]) + bg)
        ab = []
        for d in range(2):
            r = g[:, d * W:(d + 1) * W]
            i = g[:, (2 + d) * W:(3 + d) * W]
            log_a = (-LRU_C * r) * sp[d:d + 1]
            a = jnp.exp(log_a)
            bt = (jnp.sqrt(1.0 - jnp.exp(2.0 * log_a)) * i) * xc
            ab.append((a, bt))
        a1s[pl.ds(r0, T), :] = ab[1][0]
        b1s[pl.ds(r0, T), :] = ab[1][1]
        A, Bv = _scan_chunk(ab[0][0], ab[0][1], reverse=False)
        h = A * carry + Bv
        y_ref[0, pl.ds(r0, T), :] = h
        return h[T - 1:T, :]

    cf = lax.fori_loop(0, nc, fwd, h0_ref[0, 0:1, :])

    def bwd(jj, carry):
        r0 = pl.multiple_of((nc - 1 - jj) * T, T)
        A, Bv = _scan_chunk(a1s[pl.ds(r0, T), :], b1s[pl.ds(r0, T), :], reverse=True)
        h = A * carry + Bv
        gb = u_ref[0, pl.ds(r0, T), W:2 * W]
        y_ref[0, pl.ds(r0, T), :] = (y_ref[0, pl.ds(r0, T), :] + h) * _gelu_tanh(gb)
        return h[0:1, :]

    cbw = lax.fori_loop(0, nc, bwd, h0_ref[0, 1:2, :])
    st_ref[0, 0:1, :] = cf
    st_ref[0, 1:2, :] = cbw


def _lru(u, h0, lw):
    B, N, _ = u.shape
    T = min(N, 256)
    W = LRU_WIDTH
    return pl.pallas_call(
        functools.partial(_lru_kernel, N=N, T=T),
        grid=(B,),
        in_specs=[
            pl.BlockSpec((1, N, 2 * W), lambda b: (b, 0, 0)),
            pl.BlockSpec((1, 2, W), lambda b: (b, 0, 0)),
            pl.BlockSpec((4, W), lambda b: (0, 0)),
            pl.BlockSpec((1, W), lambda b: (0, 0)),
            pl.BlockSpec((W, 4 * W), lambda b: (0, 0)),
            pl.BlockSpec((1, 4 * W), lambda b: (0, 0)),
            pl.BlockSpec((2, W), lambda b: (0, 0)),
        ],
        out_specs=[
            pl.BlockSpec((1, N, W), lambda b: (b, 0, 0)),
            pl.BlockSpec((1, 2, W), lambda b: (b, 0, 0)),
        ],
        out_shape=[
            jax.ShapeDtypeStruct((B, N, W), F32),
            jax.ShapeDtypeStruct((B, 2, W), F32),
        ],
        scratch_shapes=[
            pltpu.VMEM((N + 2 * HALO, W), F32),
            pltpu.VMEM((N, W), F32),
            pltpu.VMEM((N, W), F32),
        ],
        compiler_params=_params("arbitrary"),
        name="rglru",
    )(u, h0, lw["conv_w"], lw["conv_b"], lw["w_gate"], lw["b_gate"], lw["lru_lambda"])


def _pool_kernel(u_ref, wp_ref, sc_ref, y_ref, xpad, *, N, T):
    W = GROUP_WIDTH
    nc = N // T
    zeros = jnp.zeros((HALO, W), F32)
    xpad[0:HALO, :] = zeros
    xpad[N + HALO:N + 2 * HALO, :] = zeros

    def fill(j, carry):
        r0 = pl.multiple_of(j * T, T)
        xpad[pl.ds(r0 + HALO, T), :] = u_ref[0, pl.ds(r0, T), :]
        return carry

    lax.fori_loop(0, nc, fill, 0)

    grp = lax.broadcasted_iota(jnp.int32, (1, W), 1) // POOL_CH
    half = jnp.where(grp == 0, 1, jnp.where(grp == 1, 2, jnp.where(grp == 2, 4, 8)))
    scale = sc_ref[...]

    def chunk(j, carry):
        r0 = pl.multiple_of(j * T, T)
        ext = xpad[pl.ds(r0, T + 2 * HALO), :]
        w2 = _shift_rows(ext, -1) + ext
        w4 = _shift_rows(w2, -1) + _shift_rows(w2, 1)
        w8 = _shift_rows(w4, -2) + _shift_rows(w4, 2)
        w16 = _shift_rows(w8, -4) + _shift_rows(w8, 4)
        ws = jnp.where(grp == 0, w2, jnp.where(grp == 1, w4, jnp.where(grp == 2, w8, w16)))
        body = slice(HALO, HALO + T)
        t = r0 + lax.broadcasted_iota(jnp.int32, (T, W), 0)
        cnt = (jnp.minimum(t + half, N) - jnp.maximum(t - half, 0)).astype(F32)
        d = ws[body] / cnt - ext[body]
        y_ref[0, pl.ds(r0, T), :] = _dot(d.astype(BF16), wp_ref[...]) * scale
        return carry

    lax.fori_loop(0, nc, chunk, 0)


def _pool(u, lw):
    B, N, W = u.shape
    T = min(N, 256)
    return pl.pallas_call(
        functools.partial(_pool_kernel, N=N, T=T),
        grid=(B,),
        in_specs=[
            pl.BlockSpec((1, N, W), lambda b: (b, 0, 0)),
            pl.BlockSpec((W, W), lambda b: (0, 0)),
            pl.BlockSpec((1, W), lambda b: (0, 0)),
        ],
        out_specs=pl.BlockSpec((1, N, W), lambda b: (b, 0, 0)),
        out_shape=jax.ShapeDtypeStruct((B, N, W), F32),
        scratch_shapes=[pltpu.VMEM((N + 2 * HALO, W), F32)],
        compiler_params=_params("arbitrary"),
        name="pool_mixer",
    )(u, lw["w_pool"], lw["pool_scale"])


def _outproj_kernel(x_ref, ya_ref, yb_ref, yc_ref, yd_ref, mod_ref, g2_ref, wo_ref, x1_ref, h2_ref):
    mod = mod_ref[0]
    g1 = mod[:, 2 * D_MODEL:3 * D_MODEL]
    sh2 = mod[:, 3 * D_MODEL:4 * D_MODEL]
    sc2 = mod[:, 4 * D_MODEL:5 * D_MODEL]
    mix = None
    for i, y_ref in enumerate((ya_ref, yb_ref, yc_ref, yd_ref)):
        part = _dot(y_ref[0].astype(BF16), wo_ref[i * GROUP_WIDTH:(i + 1) * GROUP_WIDTH, :])
        mix = part if mix is None else mix + part
    x1 = x_ref[0] + g1 * mix
    x1_ref[0] = x1
    h = _rms_rows(x1, D_MODEL) * g2_ref[...]
    h2_ref[0] = (h * (1.0 + sc2) + sh2).astype(h2_ref.dtype)


def _outproj(x, ys, mod, lw, *, tm):
    B, N, _ = x.shape
    shared_mod = mod.shape[0] == 1

    def tok(width):
        return pl.BlockSpec((1, tm, width), lambda j, b: (b, j, 0))

    mod_spec = pl.BlockSpec((1, 1, 6 * D_MODEL),
                            (lambda j, b: (0, 0, 0)) if shared_mod else (lambda j, b: (b, 0, 0)))
    return pl.pallas_call(
        _outproj_kernel,
        grid=(N // tm, B),
        in_specs=[tok(D_MODEL), tok(256), tok(256), tok(256), tok(256), mod_spec,
                  pl.BlockSpec((1, D_MODEL), lambda j, b: (0, 0)),
                  pl.BlockSpec((D_MODEL, D_MODEL), lambda j, b: (0, 0))],
        out_specs=[tok(D_MODEL), tok(D_MODEL)],
        out_shape=[jax.ShapeDtypeStruct((B, N, D_MODEL), F32),
                   jax.ShapeDtypeStruct((B, N, D_MODEL), BF16)],
        compiler_params=_params("arbitrary", "arbitrary"),
        name="outproj",
    )(x, *ys, mod, lw["g2"], lw["w_out"])


def _ffn_kernel(*refs, final):
    if final:
        x1_ref, h2_ref, mod_ref, wg_ref, wu_ref, wd_ref, gf_ref, o_ref, acc_ref = refs
    else:
        x1_ref, h2_ref, mod_ref, wg_ref, wu_ref, wd_ref, o_ref, acc_ref = refs
    c = pl.program_id(2)
    hb = h2_ref[0]
    gate = _dot(hb, wg_ref[...])
    up = _dot(hb, wu_ref[...])
    act = ((gate * jax.nn.sigmoid(gate)) * up).astype(BF16)
    part = _dot(act, wd_ref[...])

    @pl.when(c == 0)
    def _():
        acc_ref[...] = part

    @pl.when(c > 0)
    def _():
        acc_ref[...] += part

    @pl.when(c == pl.num_programs(2) - 1)
    def _():
        g2 = mod_ref[0][:, 5 * D_MODEL:6 * D_MODEL]
        x2 = x1_ref[0] + g2 * acc_ref[...]
        if final:
            x2 = _rms_rows(x2, D_MODEL) * gf_ref[...]
        o_ref[0] = x2


def _ffn(x1, h2, mod, lw, gf, *, tm, final):
    B, N, _ = x1.shape
    shared_mod = mod.shape[0] == 1
    ch = FF_HIDDEN // 2
    tok = pl.BlockSpec((1, tm, D_MODEL), lambda j, b, c: (b, j, 0))
    mod_spec = pl.BlockSpec((1, 1, 6 * D_MODEL),
                            (lambda j, b, c: (0, 0, 0)) if shared_mod else (lambda j, b, c: (b, 0, 0)))
    in_specs = [tok, tok, mod_spec,
                pl.BlockSpec((D_MODEL, ch), lambda j, b, c: (0, c)),
                pl.BlockSpec((D_MODEL, ch), lambda j, b, c: (0, c)),
                pl.BlockSpec((ch, D_MODEL), lambda j, b, c: (c, 0))]
    args = [x1, h2, mod, lw["w_gate_ff"], lw["w_up_ff"], lw["w_down"]]
    if final:
        in_specs.append(pl.BlockSpec((1, D_MODEL), lambda j, b, c: (0, 0)))
        args.append(gf)
    return pl.pallas_call(
        functools.partial(_ffn_kernel, final=final),
        grid=(N // tm, B, FF_HIDDEN // ch),
        in_specs=in_specs,
        out_specs=tok,
        out_shape=jax.ShapeDtypeStruct((B, N, D_MODEL), F32),
        scratch_shapes=[pltpu.VMEM((tm, D_MODEL), F32)],
        compiler_params=_params("arbitrary", "arbitrary", "arbitrary"),
        name="ffn_final" if final else "ffn",
    )(*args)


def _block_diag(w):
    G, c, e = w.shape
    return jnp.einsum('gce,gh->gche', w, jnp.eye(G, dtype=w.dtype)).reshape(G * c, G * e)


def _rot_cols(w):
    return jnp.concatenate([-w[:, 16:32], w[:, 0:16]], axis=1)


def _layer_weights(l, p):
    w_in = p["w_in"][l]
    o1 = MLA_Q_RANK
    o2 = o1 + MLA_KV_RANK
    o3 = o2 + MLA_ROPE
    c_q, c_kv, k_r, rest = w_in[:, :o1], w_in[:, o1:o2], w_in[:, o2:o3], w_in[:, o3:]
    z = lambda n: jnp.zeros((D_MODEL, n), F32)
    w_in_eff = jnp.concatenate([c_q, k_r, z(32), c_kv, z(64), _rot_cols(k_r), z(32), rest], axis=1)

    w_uq = p["mla_w_uq"][l]
    qd = MLA_NOPE + MLA_ROPE
    wq_parts, wqr_parts = [], []
    zq = lambda n: jnp.zeros((MLA_Q_RANK, n), F32)
    for h in range(MLA_HEADS):
        wh = w_uq[:, h * qd:(h + 1) * qd]
        wq_parts += [wh, zq(MLA_SLOT - qd)]
        wqr_parts += [zq(MLA_NOPE), _rot_cols(wh[:, MLA_NOPE:]), zq(MLA_SLOT - qd)]
    pad_rows = lambda w: jnp.pad(w, ((0, 256 - MLA_Q_RANK), (0, 0)))
    w_ukv = p["mla_w_ukv"][l]
    wk_parts, wv_parts = [], []
    zk = jnp.zeros((MLA_KV_RANK, MLA_SLOT - MLA_NOPE), F32)
    for h in range(MLA_HEADS):
        base = h * (MLA_NOPE + MLA_V)
        wk_parts += [w_ukv[:, base:base + MLA_NOPE], zk]
        wv_parts.append(w_ukv[:, base + MLA_NOPE:base + MLA_NOPE + MLA_V])

    w_gate = jnp.concatenate([_block_diag(p["lru_w_r"][l, 0]), _block_diag(p["lru_w_r"][l, 1]),
                              _block_diag(p["lru_w_i"][l, 0]), _block_diag(p["lru_w_i"][l, 1])], axis=1)
    b_gate = jnp.concatenate([p["lru_b_r"][l, 0], p["lru_b_r"][l, 1],
                              p["lru_b_i"][l, 0], p["lru_b_i"][l, 1]])[None, :]
    w_gu = p["w_gu"][l]
    return {
        "g1": p["norm1_g"][l][None, :],
        "g2": p["norm2_g"][l][None, :],
        "w_in": w_in_eff.astype(BF16),
        "gq": jnp.pad(p["mla_q_norm_g"][l], (0, 256 - MLA_Q_RANK))[None, :],
        "gkv": p["mla_kv_norm_g"][l][None, :],
        "wq": pad_rows(jnp.concatenate(wq_parts, axis=1)).astype(BF16),
        "wqr": pad_rows(jnp.concatenate(wqr_parts, axis=1)).astype(BF16),
        "wk": jnp.concatenate(wk_parts, axis=1).astype(BF16),
        "wv": jnp.concatenate(wv_parts, axis=1).astype(BF16),
        "conv_w": p["lru_conv_w"][l],
        "conv_b": p["lru_conv_b"][l][None, :],
        "w_gate": w_gate.astype(BF16),
        "b_gate": b_gate,
        "lru_lambda": p["lru_lambda"][l],
        "w_pool": _block_diag(p["pool_w"][l]).astype(BF16),
        "pool_scale": p["pool_scale"][l][None, :],
        "diff_lambda": p["diff_lambda"][l],
        "diff_g": jnp.tile(p["diff_norm_g"][l], DIFF_HEADS)[None, :],
        "w_out": p["w_out"][l].astype(BF16),
        "w_gate_ff": w_gu[:, :FF_HIDDEN].astype(BF16),
        "w_up_ff": w_gu[:, FF_HIDDEN:].astype(BF16),
        "w_down": p["w_down"][l].astype(BF16),
    }


def _rope_tables(n, positional):
    quarter = MLA_ROPE // 4
    if positional:
        t = jnp.arange(n)
        row = (t // GRID_W).astype(F32)
        col = (t % GRID_W).astype(F32)
        inv = ROPE_BASE ** (-jnp.arange(quarter, dtype=F32) / quarter)
        ang = jnp.concatenate([row[:, None] * inv, col[:, None] * inv], axis=-1)
        cos, sin = jnp.cos(ang), jnp.sin(ang)
    else:
        cos, sin = jnp.ones((n, 16), F32), jnp.zeros((n, 16), F32)
    one = lambda w: jnp.ones((n, w), F32)
    zero = lambda w: jnp.zeros((n, w), F32)
    scale = 1.0 / math.sqrt(MLA_NOPE + MLA_ROPE)
    return {
        "cosq": jnp.concatenate([one(64), cos, cos, one(32)], axis=1) * scale,
        "sinq": jnp.concatenate([zero(64), sin, sin, zero(32)], axis=1) * scale,
        "cosk": jnp.concatenate([zero(64), cos, cos, zero(32)], axis=1),
        "sink": jnp.concatenate([zero(64), sin, sin, zero(32)], axis=1),
        "cosd": jnp.tile(jnp.concatenate([cos, cos], axis=1), (1, 8)),
        "sina": jnp.tile(jnp.concatenate([-sin, zero(16)], axis=1), (1, 8)),
        "sinb": jnp.tile(jnp.concatenate([zero(16), sin], axis=1), (1, 8)),
    }


def _layer(x, mod, lw, tabs, layer_idx, ctx, gf, *, tm, tq, final):
    emit_cache = ctx is None
    outs = _inproj(x, mod, lw, tabs, tm=tm, emit_cache=emit_cache)
    q, k, v, u_lru, u_pool, dq, dk, dv = outs[:8]
    lam_init = 0.8 - 0.6 * math.exp(-0.3 * layer_idx)
    if ctx is None:
        B = x.shape[0]
        h0 = jnp.zeros((B, 2, LRU_WIDTH), F32)
        mla_ctx = diff_ctx = None
    else:
        ckv, kr_pad, cdk, cdv, h0 = ctx
        mla_ctx = _mla_ctx(ckv, kr_pad, lw)
        diff_ctx = (cdk, cdv)
    y_mla = _mla_attn(q, k, v, mla_ctx, tq=tq)
    y_lru, st = _lru(u_lru, h0, lw)
    y_pool = _pool(u_pool, lw)
    y_diff = _diff_attn(dq, dk, dv, diff_ctx, lw["diff_lambda"], lw["diff_g"], tq=tq, lam_init=lam_init)
    x1, h2 = _outproj(x, (y_mla, y_lru, y_pool, y_diff), mod, lw, tm=tm)
    x2 = _ffn(x1, h2, mod, lw, gf, tm=tm, final=final)
    cache = (outs[8], outs[9][..., 64:96], dk, dv, st) if emit_cache else None
    return x2, cache


def kernel(x_prompt, x_sample, cache_mla_ckv, cache_mla_krope, cache_diff_k, cache_diff_v, state_lru,
           c, c_ctx, w_ada, b_ada, norm1_g, norm2_g, w_in, mla_q_norm_g, mla_w_uq, mla_kv_norm_g,
           mla_w_ukv, lru_conv_w, lru_conv_b, lru_w_r, lru_b_r, lru_w_i, lru_b_i, lru_lambda, pool_w,
           pool_scale, diff_lambda, diff_norm_g, w_out, w_gu, w_down, final_norm_g):
    p = {
        "norm1_g": norm1_g, "norm2_g": norm2_g, "w_in": w_in, "mla_q_norm_g": mla_q_norm_g,
        "mla_w_uq": mla_w_uq, "mla_kv_norm_g": mla_kv_norm_g, "mla_w_ukv": mla_w_ukv,
        "lru_conv_w": lru_conv_w, "lru_conv_b": lru_conv_b, "lru_w_r": lru_w_r, "lru_b_r": lru_b_r,
        "lru_w_i": lru_w_i, "lru_b_i": lru_b_i, "lru_lambda": lru_lambda, "pool_w": pool_w,
        "pool_scale": pool_scale, "diff_lambda": diff_lambda, "diff_norm_g": diff_norm_g,
        "w_out": w_out, "w_gu": w_gu, "w_down": w_down,
    }
    Bp, Np, _ = x_prompt.shape
    Bs, Ns, _ = x_sample.shape
    P = cache_mla_ckv.shape[2]

    cond_all = jnp.concatenate([c, c_ctx[None, :], jnp.zeros((16 - Bs - 1, D_MODEL), F32)], axis=0)
    mod = _ada(cond_all, w_ada, b_ada)
    tabs_p = _rope_tables(Np, positional=False)
    tabs_s = _rope_tables(Ns, positional=True)
    kr_pad = jnp.pad(cache_mla_krope, ((0, 0), (0, 0), (0, 0), (MLA_NOPE, MLA_SLOT - MLA_NOPE - MLA_ROPE)))
    cdk = cache_diff_k.reshape(Bs, DEPTH, P, 256)
    cdv = cache_diff_v.reshape(Bs, DEPTH, P, 256)
    gf = final_norm_g[None, :]

    xp, xs = x_prompt, x_sample
    caches = []
    for l in range(DEPTH):
        lw = _layer_weights(l, p)
        final = l == DEPTH - 1
        mod_p = mod[l, Bs:Bs + 1].reshape(1, 1, 6 * D_MODEL)
        mod_s = mod[l, 0:Bs].reshape(Bs, 1, 6 * D_MODEL)
        xp, cache = _layer(xp, mod_p, lw, tabs_p, l, None, gf, tm=Np, tq=Np, final=final)
        caches.append(cache)
        ctx = (cache_mla_ckv[:, l], kr_pad[:, l], cdk[:, l], cdv[:, l], state_lru[:, l])
        xs, _ = _layer(xs, mod_s, lw, tabs_s, l, ctx, gf, tm=512, tq=256, final=final)

    new_mla_ckv = jnp.stack([cc[0] for cc in caches], axis=1)
    new_mla_krope = jnp.stack([cc[1] for cc in caches], axis=1)
    new_diff_k = jnp.stack([cc[2] for cc in caches], axis=1).reshape(
        Bp, DEPTH, Np, DIFF_HEADS, 2, DIFF_DIM)
    new_diff_v = jnp.stack([cc[3] for cc in caches], axis=1).reshape(
        Bp, DEPTH, Np, DIFF_HEADS, 2 * DIFF_DIM)
    new_state_lru = jnp.stack([cc[4] for cc in caches], axis=1)
    return (xp, xs, new_mla_ckv, new_mla_krope, new_diff_k, new_diff_v, new_state_lru)
```

```python
import functools
import math

import jax
import jax.numpy as jnp
from jax import lax
from jax.experimental import pallas as pl
from jax.experimental.pallas import tpu as pltpu

F32 = jnp.float32
BF16 = jnp.bfloat16

D_MODEL = 1024
DEPTH = 2
GRID_W = 64
GROUP_WIDTH = 256
MLA_HEADS = 4
MLA_NOPE = 64
MLA_ROPE = 32
MLA_V = 64
MLA_Q_RANK = 192
MLA_KV_RANK = 128
MLA_SLOT = 128
LRU_WIDTH = 256
LRU_C = 8.0
POOL_WINDOWS = (2, 4, 8, 16)
POOL_CH = 64
DIFF_HEADS = 4
DIFF_DIM = 32
HEAD_V = 64
FF_HIDDEN = 2816
ROPE_BASE = 10000.0
EPS = 1e-6
IN_EFF = 2048
HALO = 8
VT_ROWS = 80
ATT_TQ = 256
LOG2E = math.log2(math.e)

VMEM_LIMIT_BYTES = 56 * 1024 * 1024

_NT = (((1,), (1,)), ((), ()))


def _params(*sem):
    return pltpu.CompilerParams(dimension_semantics=sem, vmem_limit_bytes=VMEM_LIMIT_BYTES)


def _dot(a, b):
    return jnp.dot(a, b, preferred_element_type=F32)


def _dot_nt(a, b):
    return lax.dot_general(a, b, _NT, preferred_element_type=F32)


def _rms_rows(x, width):
    ms = jnp.sum(x * x, axis=-1, keepdims=True) * (1.0 / width)
    return x * lax.rsqrt(ms + EPS)


def _store_vt(vt_ref, v):
    vt = v.T
    rows = v.shape[0]
    pad = VT_ROWS - HEAD_V
    ones_row = jnp.where(lax.broadcasted_iota(jnp.int32, (pad, rows), 0) == 0, 1.0, 0.0).astype(BF16)
    for hh in range(vt_ref.shape[1]):
        vt_ref[0, hh, 0:HEAD_V, :] = vt[hh * HEAD_V:(hh + 1) * HEAD_V, :].astype(BF16)
        vt_ref[0, hh, HEAD_V:VT_ROWS, :] = ones_row


def _ada_kernel(cond_ref, w_ref, b_ref, out_ref):
    c = cond_ref[...]
    s = c * jax.nn.sigmoid(c)
    out_ref[0] = _dot(s.astype(BF16), w_ref[0].astype(BF16)) + b_ref[0]


def _ada(cond_all, w_ada, b_ada):
    rows = cond_all.shape[0]
    tn = 1536
    return pl.pallas_call(
        _ada_kernel,
        grid=(DEPTH, 6 * D_MODEL // tn),
        in_specs=[
            pl.BlockSpec((rows, D_MODEL), lambda l, j: (0, 0)),
            pl.BlockSpec((1, D_MODEL, tn), lambda l, j: (l, 0, j)),
            pl.BlockSpec((1, 1, tn), lambda l, j: (l, 0, j)),
        ],
        out_specs=pl.BlockSpec((1, rows, tn), lambda l, j: (l, 0, j)),
        out_shape=jax.ShapeDtypeStruct((DEPTH, rows, 6 * D_MODEL), F32),
        compiler_params=_params("arbitrary", "arbitrary"),
        name="ada_mod",
    )(cond_all, w_ada, b_ada.reshape(DEPTH, 1, 6 * D_MODEL))


def _inproj_kernel(x_ref, mod_ref, g1_ref, win_ref, gq_ref, gkv_ref, wq_ref, wqr_ref, wk_ref, wv_ref,
                   cosq_ref, sinq_ref, cosk_ref, sink_ref, cosd_ref, sina_ref, sinb_ref,
                   q_out, k_out, vt_out, lru_out, pool_out, dq_out, dk_out, dvt_out, *cache_outs):
    x = x_ref[0]
    mod = mod_ref[0]
    sh1 = mod[:, 0:D_MODEL]
    sc1 = mod[:, D_MODEL:2 * D_MODEL]
    h = _rms_rows(x, D_MODEL) * g1_ref[...]
    hb = (h * (1.0 + sc1) + sh1).astype(BF16)

    t01 = _dot(hb, win_ref[:, 0:256])
    lane = lax.broadcasted_iota(jnp.int32, (1, 256), 1)
    cq = jnp.where(lane < MLA_Q_RANK, t01, 0.0)
    cqn = (_rms_rows(cq, MLA_Q_RANK) * gq_ref[...]).astype(BF16)
    qa = _dot(cqn, wq_ref[...])
    qr = _dot(cqn, wqr_ref[...])
    cosq = cosq_ref[...]
    sinq = sinq_ref[...]
    ckv = _dot(hb, win_ref[:, 256:384])
    lat = _rms_rows(ckv, MLA_KV_RANK) * gkv_ref[...]
    latb = lat.astype(BF16)
    kk = _dot(latb, wk_ref[...])
    _store_vt(vt_out, _dot(latb, wv_ref[...]))
    t1 = t01[:, 128:256]
    t3 = _dot(hb, win_ref[:, 384:512])
    kro = t1 * cosk_ref[...] + t3 * sink_ref[...]
    for hh in range(MLA_HEADS):
        sl = slice(hh * MLA_SLOT, (hh + 1) * MLA_SLOT)
        q_out[0, hh] = (qa[:, sl] * cosq + qr[:, sl] * sinq).astype(q_out.dtype)
        k_out[0, hh] = (kk[:, sl] + kro).astype(k_out.dtype)

    lru_out[0] = _dot(hb, win_ref[:, 512:1024])
    pool_out[0] = _dot(hb, win_ref[:, 1024:1280])

    cosd = cosd_ref[...]
    sina = sina_ref[...]
    sinb = sinb_ref[...]

    def rope(t):
        return t * cosd + pltpu.roll(t, 256 - 16, 1) * sina + pltpu.roll(t, 16, 1) * sinb

    dq = _dot(hb, win_ref[:, 1280:1536])
    dk = _dot(hb, win_ref[:, 1536:1792])
    dv = _dot(hb, win_ref[:, 1792:2048])
    dq_out[0] = (rope(dq) * (LOG2E / math.sqrt(DIFF_DIM))).astype(dq_out.dtype)
    dk_out[0] = rope(dk).astype(dk_out.dtype)
    _store_vt(dvt_out, dv)

    if cache_outs:
        lat_out, kr_out, dk_raw_out, dv_raw_out = cache_outs
        lat_out[0] = lat
        kr_out[0] = t1
        dk_raw_out[0] = dk
        dv_raw_out[0] = dv


def _inproj(x, mod, lw, tabs, *, tm, emit_cache):
    B, N, _ = x.shape
    shared_mod = mod.shape[0] == 1

    def tok(width):
        return pl.BlockSpec((1, tm, width), lambda j, b: (b, j, 0))

    def const2(shape):
        return pl.BlockSpec(shape, lambda j, b: (0, 0))

    def tab(width):
        return pl.BlockSpec((tm, width), lambda j, b: (j, 0))

    head = pl.BlockSpec((1, MLA_HEADS, tm, MLA_SLOT), lambda j, b: (b, 0, j, 0))
    vt_spec = pl.BlockSpec((1, MLA_HEADS, VT_ROWS, tm), lambda j, b: (b, 0, 0, j))
    mod_spec = pl.BlockSpec((1, 1, 6 * D_MODEL),
                            (lambda j, b: (0, 0, 0)) if shared_mod else (lambda j, b: (b, 0, 0)))
    in_specs = [
        tok(D_MODEL), mod_spec, const2((1, D_MODEL)), const2((D_MODEL, IN_EFF)),
        const2((1, 256)), const2((1, 128)), const2((256, 512)), const2((256, 512)),
        const2((128, 512)), const2((128, 256)),
        tab(128), tab(128), tab(128), tab(128), tab(256), tab(256), tab(256),
    ]
    out_specs = [head, head, vt_spec, tok(512), tok(256), tok(256), tok(256), vt_spec]
    vt_shape = jax.ShapeDtypeStruct((B, MLA_HEADS, VT_ROWS, N), BF16)
    out_shape = [
        jax.ShapeDtypeStruct((B, MLA_HEADS, N, MLA_SLOT), BF16),
        jax.ShapeDtypeStruct((B, MLA_HEADS, N, MLA_SLOT), BF16),
        vt_shape,
        jax.ShapeDtypeStruct((B, N, 512), F32),
        jax.ShapeDtypeStruct((B, N, 256), F32),
        jax.ShapeDtypeStruct((B, N, 256), BF16),
        jax.ShapeDtypeStruct((B, N, 256), BF16),
        vt_shape,
    ]
    if emit_cache:
        out_specs += [tok(128), tok(128), tok(256), tok(256)]
        out_shape += [jax.ShapeDtypeStruct((B, N, 128), F32), jax.ShapeDtypeStruct((B, N, 128), F32),
                      jax.ShapeDtypeStruct((B, N, 256), F32), jax.ShapeDtypeStruct((B, N, 256), F32)]
    return pl.pallas_call(
        _inproj_kernel,
        grid=(N // tm, B),
        in_specs=in_specs,
        out_specs=out_specs,
        out_shape=out_shape,
        compiler_params=_params("arbitrary", "arbitrary"),
        name="inproj_cache" if emit_cache else "inproj",
    )(x, mod, lw["g1"], lw["w_in"], lw["gq"], lw["gkv"], lw["wq"], lw["wqr"], lw["wk"], lw["wv"],
      tabs["cosq"], tabs["sinq"], tabs["cosk"], tabs["sink"], tabs["cosd"], tabs["sina"], tabs["sinb"])


def _ctx_prep_kernel(ckv_ref, kr_ref, dk_ref, dv_ref, wk_ref, wv_ref, k_out, vt_out, dk_out, dvt_out):
    latb = ckv_ref[0].astype(BF16)
    kk = _dot(latb, wk_ref[...])
    kr = kr_ref[0]
    for hh in range(MLA_HEADS):
        k_out[0, hh] = (kk[:, hh * MLA_SLOT:(hh + 1) * MLA_SLOT] + kr).astype(k_out.dtype)
    _store_vt(vt_out, _dot(latb, wv_ref[...]))
    dk_out[0] = dk_ref[0].astype(dk_out.dtype)
    _store_vt(dvt_out, dv_ref[0])


def _ctx_prep(ckv, kr_pad, cdk, cdv, lw):
    B, P, _ = ckv.shape
    row = lambda w: pl.BlockSpec((1, P, w), lambda b: (b, 0, 0))
    vt_spec = pl.BlockSpec((1, MLA_HEADS, VT_ROWS, P), lambda b: (b, 0, 0, 0))
    vt_shape = jax.ShapeDtypeStruct((B, MLA_HEADS, VT_ROWS, P), BF16)
    return pl.pallas_call(
        _ctx_prep_kernel,
        grid=(B,),
        in_specs=[row(128), row(128), row(256), row(256),
                  pl.BlockSpec((128, 512), lambda b: (0, 0)),
                  pl.BlockSpec((128, 256), lambda b: (0, 0))],
        out_specs=[pl.BlockSpec((1, MLA_HEADS, P, MLA_SLOT), lambda b: (b, 0, 0, 0)), vt_spec,
                   row(256), vt_spec],
        out_shape=[jax.ShapeDtypeStruct((B, MLA_HEADS, P, MLA_SLOT), BF16), vt_shape,
                   jax.ShapeDtypeStruct((B, P, 256), BF16), vt_shape],
        compiler_params=_params("arbitrary"),
        name="ctx_prep",
    )(ckv, kr_pad, cdk, cdv, lw["wk"], lw["wv"])


def _run_pipeline(n_maps, scores, exps, finish):
    ms = {0: scores(0)}
    for u in range(n_maps):
        if u + 1 < n_maps:
            ms[u + 1] = scores(u + 1)
        exps(u, ms.pop(u))
        if u >= 1:
            finish(u - 1)
    finish(n_maps - 1)


def _score_stage(s_buf, k_new, k_ctx, q, n_ctx):
    sn = _dot_nt(k_new, q)
    s_buf[n_ctx:, :] = sn
    m = jnp.max(sn, axis=0, keepdims=True)
    if k_ctx is not None:
        sc = _dot_nt(k_ctx, q)
        s_buf[0:n_ctx, :] = sc
        m = jnp.maximum(m, jnp.max(sc, axis=0, keepdims=True))
    return m


def _value_stage(e_buf, vt_new, vt_ctx, n_ctx):
    o = _dot(vt_new, e_buf[n_ctx:, :])
    if vt_ctx is not None:
        o = o + _dot(vt_ctx, e_buf[0:n_ctx, :])
    return o


def _att_scratch(nk):
    return [pltpu.VMEM((nk, ATT_TQ), F32), pltpu.VMEM((nk, ATT_TQ), F32),
            pltpu.VMEM((nk, ATT_TQ), BF16), pltpu.VMEM((nk, ATT_TQ), BF16),
            pltpu.VMEM((MLA_HEADS * HEAD_V, ATT_TQ), F32)]


def _mla_attn_kernel(*refs, has_ctx, nsub):
    if has_ctx:
        q_ref, k_ref, vt_ref, kc_ref, vtc_ref, o_ref, s0, s1, e0, e1, ot = refs
        n_ctx = kc_ref.shape[2]
    else:
        q_ref, k_ref, vt_ref, o_ref, s0, s1, e0, e1, ot = refs
        n_ctx = 0
    s_bufs, e_bufs = (s0, s1), (e0, e1)

    def scores(u):
        t, hh = divmod(u, MLA_HEADS)
        q = q_ref[0, hh, t * ATT_TQ:(t + 1) * ATT_TQ, :]
        return _score_stage(s_bufs[u % 2], k_ref[0, hh], kc_ref[0, hh] if has_ctx else None, q, n_ctx)

    def exps(u, m):
        e_bufs[u % 2][...] = jnp.exp2(s_bufs[u % 2][...] - m).astype(BF16)

    def finish(u):
        t, hh = divmod(u, MLA_HEADS)
        o = _value_stage(e_bufs[u % 2], vt_ref[0, hh], vtc_ref[0, hh] if has_ctx else None, n_ctx)
        ot[hh * HEAD_V:(hh + 1) * HEAD_V, :] = o[0:HEAD_V, :] * (1.0 / o[HEAD_V:HEAD_V + 1, :])
        if hh == MLA_HEADS - 1:
            o_ref[0, t * ATT_TQ:(t + 1) * ATT_TQ, :] = ot[...].T

    _run_pipeline(nsub * MLA_HEADS, scores, exps, finish)


def _mla_attn(q, k, vt, ctx, *, nsub):
    B, H, N, S = q.shape
    tq = nsub * ATT_TQ
    in_specs = [
        pl.BlockSpec((1, H, tq, S), lambda b, j: (b, 0, j, 0)),
        pl.BlockSpec((1, H, N, S), lambda b, j: (b, 0, 0, 0)),
        pl.BlockSpec((1, H, VT_ROWS, N), lambda b, j: (b, 0, 0, 0)),
    ]
    args = [q, k, vt]
    n_ctx = 0
    if ctx is not None:
        n_ctx = ctx[0].shape[2]
        in_specs += [
            pl.BlockSpec((1, H, n_ctx, S), lambda b, j: (b, 0, 0, 0)),
            pl.BlockSpec((1, H, VT_ROWS, n_ctx), lambda b, j: (b, 0, 0, 0)),
        ]
        args += list(ctx)
    return pl.pallas_call(
        functools.partial(_mla_attn_kernel, has_ctx=ctx is not None, nsub=nsub),
        grid=(B, N // tq),
        in_specs=in_specs,
        out_specs=pl.BlockSpec((1, tq, 256), lambda b, j: (b, j, 0)),
        out_shape=jax.ShapeDtypeStruct((B, N, 256), F32),
        scratch_shapes=_att_scratch(N + n_ctx),
        compiler_params=_params("arbitrary", "arbitrary"),
        name="mla_attn_ctx" if ctx is not None else "mla_attn",
    )(*args)


def _diff_attn_kernel(*refs, has_ctx, nsub, lam_init):
    if has_ctx:
        lv_ref, g_ref, q_ref, k_ref, vt_ref, kc_ref, vtc_ref, o_ref, s0, s1, e0, e1, ot = refs
        n_ctx = kc_ref.shape[1]
    else:
        lv_ref, g_ref, q_ref, k_ref, vt_ref, o_ref, s0, s1, e0, e1, ot = refs
        n_ctx = 0
    s_bufs, e_bufs = (s0, s1), (e0, e1)
    lv = lv_ref[...]
    lam = (jnp.exp(jnp.sum(lv[0:1] * lv[1:2], axis=-1, keepdims=True))
           - jnp.exp(jnp.sum(lv[2:3] * lv[3:4], axis=-1, keepdims=True)) + lam_init)
    lane = lax.broadcasted_iota(jnp.int32, (1, 256), 1)
    n_pairs = 2 * DIFF_HEADS
    outs = {}

    def scores(u):
        t, p = divmod(u, n_pairs)
        q = q_ref[0, t * ATT_TQ:(t + 1) * ATT_TQ, :]
        in_pair = (lane >= p * DIFF_DIM) & (lane < (p + 1) * DIFF_DIM)
        qm = jnp.where(in_pair, q, jnp.zeros_like(q))
        return _score_stage(s_bufs[u % 2], k_ref[0], kc_ref[0] if has_ctx else None, qm, n_ctx)

    def exps(u, m):
        e_bufs[u % 2][...] = jnp.exp2(s_bufs[u % 2][...] - m).astype(BF16)

    def finish(u):
        t, p = divmod(u, n_pairs)
        hh = p // 2
        outs[u] = _value_stage(e_bufs[u % 2], vt_ref[0, hh], vtc_ref[0, hh] if has_ctx else None, n_ctx)
        if p % 2 == 1:
            o0, o1 = outs.pop(u - 1), outs.pop(u)
            w0 = 1.0 / o0[HEAD_V:HEAD_V + 1, :]
            w1 = lam / o1[HEAD_V:HEAD_V + 1, :]
            o = o0[0:HEAD_V, :] * w0 - o1[0:HEAD_V, :] * w1
            msq = jnp.sum(o * o, axis=0, keepdims=True) * (1.0 / HEAD_V)
            ot[hh * HEAD_V:(hh + 1) * HEAD_V, :] = o * lax.rsqrt(msq + EPS)
        if p == n_pairs - 1:
            o_ref[0, t * ATT_TQ:(t + 1) * ATT_TQ, :] = (ot[...].T * g_ref[...]) * (1.0 - lam_init)

    _run_pipeline(nsub * n_pairs, scores, exps, finish)


def _diff_attn(q, k, vt, ctx, lv, g4, *, nsub, lam_init):
    B, N, _ = q.shape
    tq = nsub * ATT_TQ
    in_specs = [
        pl.BlockSpec((4, DIFF_DIM), lambda b, j: (0, 0)),
        pl.BlockSpec((1, 256), lambda b, j: (0, 0)),
        pl.BlockSpec((1, tq, 256), lambda b, j: (b, j, 0)),
        pl.BlockSpec((1, N, 256), lambda b, j: (b, 0, 0)),
        pl.BlockSpec((1, DIFF_HEADS, VT_ROWS, N), lambda b, j: (b, 0, 0, 0)),
    ]
    args = [lv, g4, q, k, vt]
    n_ctx = 0
    if ctx is not None:
        n_ctx = ctx[0].shape[1]
        in_specs += [pl.BlockSpec((1, n_ctx, 256), lambda b, j: (b, 0, 0)),
                     pl.BlockSpec((1, DIFF_HEADS, VT_ROWS, n_ctx), lambda b, j: (b, 0, 0, 0))]
        args += list(ctx)
    return pl.pallas_call(
        functools.partial(_diff_attn_kernel, has_ctx=ctx is not None, nsub=nsub, lam_init=lam_init),
        grid=(B, N // tq),
        in_specs=in_specs,
        out_specs=pl.BlockSpec((1, tq, 256), lambda b, j: (b, j, 0)),
        out_shape=jax.ShapeDtypeStruct((B, N, 256), F32),
        scratch_shapes=_att_scratch(N + n_ctx),
        compiler_params=_params("arbitrary", "arbitrary"),
        name="diff_attn_ctx" if ctx is not None else "diff_attn",
    )(*args)


def _shift_rows(v, k):
    return pltpu.roll(v, (-k) % v.shape[0], 0)


def _scan_chunk(a, b, reverse):
    T = a.shape[0]
    row = lax.broadcasted_iota(jnp.int32, a.shape, 0)
    s = 1
    while s < T:
        if reverse:
            valid = row < T - s
            ap, bp = _shift_rows(a, s), _shift_rows(b, s)
        else:
            valid = row >= s
            ap, bp = _shift_rows(a, -s), _shift_rows(b, -s)
        b = jnp.where(valid, a * bp + b, b)
        a = jnp.where(valid, a * ap, a)
        s *= 2
    return a, b


def _gelu_tanh(x):
    return x * (0.5 * (1.0 + jnp.tanh(math.sqrt(2.0 / math.pi) * (x + 0.044715 * (x * x * x)))))


def _lru_kernel(u_ref, h0_ref, cw_ref, cb_ref, wg_ref, bg_ref, lam_ref, y_ref, st_ref,
                xpad, a1s, b1s, *, N, T):
    W = LRU_WIDTH
    nc = N // T
    zeros = jnp.zeros((HALO, W), F32)
    xpad[0:HALO, :] = zeros
    xpad[N + HALO:N + 2 * HALO, :] = zeros

    def fill(j, carry):
        r0 = pl.multiple_of(j * T, T)
        xpad[pl.ds(r0 + HALO, T), :] = u_ref[0, pl.ds(r0, T), 0:W]
        return carry

    lax.fori_loop(0, nc, fill, 0)

    z = -lam_ref[...]
    sp = jnp.maximum(z, 0.0) + jnp.log1p(jnp.exp(-jnp.abs(z)))
    cw = cw_ref[...]
    cb = cb_ref[...]
    bg = bg_ref[...]

    def fwd(j, carry):
        r0 = pl.multiple_of(j * T, T)
        ext = xpad[pl.ds(r0, T + 2 * HALO), :]
        body = slice(HALO, HALO + T)
        xc = cb
        for tap in range(4):
            xc = xc + _shift_rows(ext, tap - 1)[body] * cw[tap:tap + 1]
        g = jax.nn.sigmoid(_dot(xc.astype(BF16), wg_ref[...]) + bg)
        ab = []
        for d in range(2):
            r = g[:, d * W:(d + 1) * W]
            i = g[:, (2 + d) * W:(3 + d) * W]
            log_a = (-LRU_C * r) * sp[d:d + 1]
            a = jnp.exp(log_a)
            bt = (jnp.sqrt(1.0 - jnp.exp(2.0 * log_a)) * i) * xc
            ab.append((a, bt))
        a1s[pl.ds(r0, T), :] = ab[1][0]
        b1s[pl.ds(r0, T), :] = ab[1][1]
        A, Bv = _scan_chunk(ab[0][0], ab[0][1], reverse=False)
        h = A * carry + Bv
        y_ref[0, pl.ds(r0, T), :] = h
        return h[T - 1:T, :]

    cf = lax.fori_loop(0, nc, fwd, h0_ref[0, 0:1, :])

    def bwd(jj, carry):
        r0 = pl.multiple_of((nc - 1 - jj) * T, T)
        A, Bv = _scan_chunk(a1s[pl.ds(r0, T), :], b1s[pl.ds(r0, T), :], reverse=True)
        h = A * carry + Bv
        gb = u_ref[0, pl.ds(r0, T), W:2 * W]
        y_ref[0, pl.ds(r0, T), :] = (y_ref[0, pl.ds(r0, T), :] + h) * _gelu_tanh(gb)
        return h[0:1, :]

    cbw = lax.fori_loop(0, nc, bwd, h0_ref[0, 1:2, :])
    st_ref[0, 0:1, :] = cf
    st_ref[0, 1:2, :] = cbw


def _lru(u, h0, lw):
    B, N, _ = u.shape
    T = min(N, 256)
    W = LRU_WIDTH
    return pl.pallas_call(
        functools.partial(_lru_kernel, N=N, T=T),
        grid=(B,),
        in_specs=[
            pl.BlockSpec((1, N, 2 * W), lambda b: (b, 0, 0)),
            pl.BlockSpec((1, 2, W), lambda b: (b, 0, 0)),
            pl.BlockSpec((4, W), lambda b: (0, 0)),
            pl.BlockSpec((1, W), lambda b: (0, 0)),
            pl.BlockSpec((W, 4 * W), lambda b: (0, 0)),
            pl.BlockSpec((1, 4 * W), lambda b: (0, 0)),
            pl.BlockSpec((2, W), lambda b: (0, 0)),
        ],
        out_specs=[
            pl.BlockSpec((1, N, W), lambda b: (b, 0, 0)),
            pl.BlockSpec((1, 2, W), lambda b: (b, 0, 0)),
        ],
        out_shape=[
            jax.ShapeDtypeStruct((B, N, W), F32),
            jax.ShapeDtypeStruct((B, 2, W), F32),
        ],
        scratch_shapes=[
            pltpu.VMEM((N + 2 * HALO, W), F32),
            pltpu.VMEM((N, W), F32),
            pltpu.VMEM((N, W), F32),
        ],
        compiler_params=_params("arbitrary"),
        name="rglru",
    )(u, h0, lw["conv_w"], lw["conv_b"], lw["w_gate"], lw["b_gate"], lw["lru_lambda"])


def _pool_kernel(u_ref, wp_ref, sc_ref, y_ref, xpad, *, N, T):
    W = GROUP_WIDTH
    nc = N // T
    zeros = jnp.zeros((HALO, W), F32)
    xpad[0:HALO, :] = zeros
    xpad[N + HALO:N + 2 * HALO, :] = zeros

    def fill(j, carry):
        r0 = pl.multiple_of(j * T, T)
        xpad[pl.ds(r0 + HALO, T), :] = u_ref[0, pl.ds(r0, T), :]
        return carry

    lax.fori_loop(0, nc, fill, 0)

    grp = lax.broadcasted_iota(jnp.int32, (1, W), 1) // POOL_CH
    half = jnp.where(grp == 0, 1, jnp.where(grp == 1, 2, jnp.where(grp == 2, 4, 8)))
    scale = sc_ref[...]

    def chunk(j, carry):
        r0 = pl.multiple_of(j * T, T)
        ext = xpad[pl.ds(r0, T + 2 * HALO), :]
        w2 = _shift_rows(ext, -1) + ext
        w4 = _shift_rows(w2, -1) + _shift_rows(w2, 1)
        w8 = _shift_rows(w4, -2) + _shift_rows(w4, 2)
        w16 = _shift_rows(w8, -4) + _shift_rows(w8, 4)
        ws = jnp.where(grp == 0, w2, jnp.where(grp == 1, w4, jnp.where(grp == 2, w8, w16)))
        body = slice(HALO, HALO + T)
        t = r0 + lax.broadcasted_iota(jnp.int32, (T, W), 0)
        cnt = (jnp.minimum(t + half, N) - jnp.maximum(t - half, 0)).astype(F32)
        d = ws[body] / cnt - ext[body]
        y_ref[0, pl.ds(r0, T), :] = _dot(d.astype(BF16), wp_ref[...]) * scale
        return carry

    lax.fori_loop(0, nc, chunk, 0)


def _pool(u, lw):
    B, N, W = u.shape
    T = min(N, 256)
    return pl.pallas_call(
        functools.partial(_pool_kernel, N=N, T=T),
        grid=(B,),
        in_specs=[
            pl.BlockSpec((1, N, W), lambda b: (b, 0, 0)),
            pl.BlockSpec((W, W), lambda b: (0, 0)),
            pl.BlockSpec((1, W), lambda b: (0, 0)),
        ],
        out_specs=pl.BlockSpec((1, N, W), lambda b: (b, 0, 0)),
        out_shape=jax.ShapeDtypeStruct((B, N, W), F32),
        scratch_shapes=[pltpu.VMEM((N + 2 * HALO, W), F32)],
        compiler_params=_params("arbitrary"),
        name="pool_mixer",
    )(u, lw["w_pool"], lw["pool_scale"])


def _outproj_kernel(x_ref, ya_ref, yb_ref, yc_ref, yd_ref, mod_ref, g2_ref, wo_ref, x1_ref, h2_ref):
    mod = mod_ref[0]
    g1 = mod[:, 2 * D_MODEL:3 * D_MODEL]
    sh2 = mod[:, 3 * D_MODEL:4 * D_MODEL]
    sc2 = mod[:, 4 * D_MODEL:5 * D_MODEL]
    mix = None
    for i, y_ref in enumerate((ya_ref, yb_ref, yc_ref, yd_ref)):
        part = _dot(y_ref[0].astype(BF16), wo_ref[i * GROUP_WIDTH:(i + 1) * GROUP_WIDTH, :])
        mix = part if mix is None else mix + part
    x1 = x_ref[0] + g1 * mix
    x1_ref[0] = x1
    h = _rms_rows(x1, D_MODEL) * g2_ref[...]
    h2_ref[0] = (h * (1.0 + sc2) + sh2).astype(h2_ref.dtype)


def _outproj(x, ys, mod, lw, *, tm):
    B, N, _ = x.shape
    shared_mod = mod.shape[0] == 1

    def tok(width):
        return pl.BlockSpec((1, tm, width), lambda j, b: (b, j, 0))

    mod_spec = pl.BlockSpec((1, 1, 6 * D_MODEL),
                            (lambda j, b: (0, 0, 0)) if shared_mod else (lambda j, b: (b, 0, 0)))
    return pl.pallas_call(
        _outproj_kernel,
        grid=(N // tm, B),
        in_specs=[tok(D_MODEL), tok(256), tok(256), tok(256), tok(256), mod_spec,
                  pl.BlockSpec((1, D_MODEL), lambda j, b: (0, 0)),
                  pl.BlockSpec((D_MODEL, D_MODEL), lambda j, b: (0, 0))],
        out_specs=[tok(D_MODEL), tok(D_MODEL)],
        out_shape=[jax.ShapeDtypeStruct((B, N, D_MODEL), F32),
                   jax.ShapeDtypeStruct((B, N, D_MODEL), BF16)],
        compiler_params=_params("arbitrary", "arbitrary"),
        name="outproj",
    )(x, *ys, mod, lw["g2"], lw["w_out"])


def _ffn_kernel(*refs, final):
    if final:
        x1_ref, h2_ref, mod_ref, wg_ref, wu_ref, wd_ref, gf_ref, o_ref, acc_ref = refs
    else:
        x1_ref, h2_ref, mod_ref, wg_ref, wu_ref, wd_ref, o_ref, acc_ref = refs
    c = pl.program_id(2)
    hb = h2_ref[0]
    gate = _dot(hb, wg_ref[...])
    up = _dot(hb, wu_ref[...])
    act = ((gate * jax.nn.sigmoid(gate)) * up).astype(BF16)
    part = _dot(act, wd_ref[...])

    @pl.when(c == 0)
    def _():
        acc_ref[...] = part

    @pl.when(c > 0)
    def _():
        acc_ref[...] += part

    @pl.when(c == pl.num_programs(2) - 1)
    def _():
        g2 = mod_ref[0][:, 5 * D_MODEL:6 * D_MODEL]
        x2 = x1_ref[0] + g2 * acc_ref[...]
        if final:
            x2 = _rms_rows(x2, D_MODEL) * gf_ref[...]
        o_ref[0] = x2


def _ffn(x1, h2, mod, lw, gf, *, tm, final):
    B, N, _ = x1.shape
    shared_mod = mod.shape[0] == 1
    ch = FF_HIDDEN // 2
    tok = pl.BlockSpec((1, tm, D_MODEL), lambda j, b, c: (b, j, 0))
    mod_spec = pl.BlockSpec((1, 1, 6 * D_MODEL),
                            (lambda j, b, c: (0, 0, 0)) if shared_mod else (lambda j, b, c: (b, 0, 0)))
    in_specs = [tok, tok, mod_spec,
                pl.BlockSpec((D_MODEL, ch), lambda j, b, c: (0, c)),
                pl.BlockSpec((D_MODEL, ch), lambda j, b, c: (0, c)),
                pl.BlockSpec((ch, D_MODEL), lambda j, b, c: (c, 0))]
    args = [x1, h2, mod, lw["w_gate_ff"], lw["w_up_ff"], lw["w_down"]]
    if final:
        in_specs.append(pl.BlockSpec((1, D_MODEL), lambda j, b, c: (0, 0)))
        args.append(gf)
    return pl.pallas_call(
        functools.partial(_ffn_kernel, final=final),
        grid=(N // tm, B, FF_HIDDEN // ch),
        in_specs=in_specs,
        out_specs=tok,
        out_shape=jax.ShapeDtypeStruct((B, N, D_MODEL), F32),
        scratch_shapes=[pltpu.VMEM((tm, D_MODEL), F32)],
        compiler_params=_params("arbitrary", "arbitrary", "arbitrary"),
        name="ffn_final" if final else "ffn",
    )(*args)


def _block_diag(w):
    G, c, e = w.shape
    return jnp.einsum('gce,gh->gche', w, jnp.eye(G, dtype=w.dtype)).reshape(G * c, G * e)


def _rot_cols(w):
    return jnp.concatenate([-w[:, 16:32], w[:, 0:16]], axis=1)


def _layer_weights(l, p):
    w_in = p["w_in"][l]
    o1 = MLA_Q_RANK
    o2 = o1 + MLA_KV_RANK
    o3 = o2 + MLA_ROPE
    c_q, c_kv, k_r, rest = w_in[:, :o1], w_in[:, o1:o2], w_in[:, o2:o3], w_in[:, o3:]
    z = lambda n: jnp.zeros((D_MODEL, n), F32)
    w_in_eff = jnp.concatenate([c_q, k_r, z(32), c_kv, z(64), _rot_cols(k_r), z(32), rest], axis=1)

    w_uq = p["mla_w_uq"][l]
    qd = MLA_NOPE + MLA_ROPE
    wq_parts, wqr_parts = [], []
    zq = lambda n: jnp.zeros((MLA_Q_RANK, n), F32)
    for h in range(MLA_HEADS):
        wh = w_uq[:, h * qd:(h + 1) * qd]
        wq_parts += [wh, zq(MLA_SLOT - qd)]
        wqr_parts += [zq(MLA_NOPE), _rot_cols(wh[:, MLA_NOPE:]), zq(MLA_SLOT - qd)]
    pad_rows = lambda w: jnp.pad(w, ((0, 256 - MLA_Q_RANK), (0, 0)))
    w_ukv = p["mla_w_ukv"][l]
    wk_parts, wv_parts = [], []
    zk = jnp.zeros((MLA_KV_RANK, MLA_SLOT - MLA_NOPE), F32)
    for h in range(MLA_HEADS):
        base = h * (MLA_NOPE + MLA_V)
        wk_parts += [w_ukv[:, base:base + MLA_NOPE], zk]
        wv_parts.append(w_ukv[:, base + MLA_NOPE:base + MLA_NOPE + MLA_V])

    w_gate = jnp.concatenate([_block_diag(p["lru_w_r"][l, 0]), _block_diag(p["lru_w_r"][l, 1]),
                              _block_diag(p["lru_w_i"][l, 0]), _block_diag(p["lru_w_i"][l, 1])], axis=1)
    b_gate = jnp.concatenate([p["lru_b_r"][l, 0], p["lru_b_r"][l, 1],
                              p["lru_b_i"][l, 0], p["lru_b_i"][l, 1]])[None, :]
    w_gu = p["w_gu"][l]
    return {
        "g1": p["norm1_g"][l][None, :],
        "g2": p["norm2_g"][l][None, :],
        "w_in": w_in_eff.astype(BF16),
        "gq": jnp.pad(p["mla_q_norm_g"][l], (0, 256 - MLA_Q_RANK))[None, :],
        "gkv": p["mla_kv_norm_g"][l][None, :],
        "wq": pad_rows(jnp.concatenate(wq_parts, axis=1)).astype(BF16),
        "wqr": pad_rows(jnp.concatenate(wqr_parts, axis=1)).astype(BF16),
        "wk": jnp.concatenate(wk_parts, axis=1).astype(BF16),
        "wv": jnp.concatenate(wv_parts, axis=1).astype(BF16),
        "conv_w": p["lru_conv_w"][l],
        "conv_b": p["lru_conv_b"][l][None, :],
        "w_gate": w_gate.astype(BF16),
        "b_gate": b_gate,
        "lru_lambda": p["lru_lambda"][l],
        "w_pool": _block_diag(p["pool_w"][l]).astype(BF16),
        "pool_scale": p["pool_scale"][l][None, :],
        "diff_lambda": p["diff_lambda"][l],
        "diff_g": jnp.tile(p["diff_norm_g"][l], DIFF_HEADS)[None, :],
        "w_out": p["w_out"][l].astype(BF16),
        "w_gate_ff": w_gu[:, :FF_HIDDEN].astype(BF16),
        "w_up_ff": w_gu[:, FF_HIDDEN:].astype(BF16),
        "w_down": p["w_down"][l].astype(BF16),
    }


def _rope_tables(n, positional):
    quarter = MLA_ROPE // 4
    if positional:
        t = jnp.arange(n)
        row = (t // GRID_W).astype(F32)
        col = (t % GRID_W).astype(F32)
        inv = ROPE_BASE ** (-jnp.arange(quarter, dtype=F32) / quarter)
        ang = jnp.concatenate([row[:, None] * inv, col[:, None] * inv], axis=-1)
        cos, sin = jnp.cos(ang), jnp.sin(ang)
    else:
        cos, sin = jnp.ones((n, 16), F32), jnp.zeros((n, 16), F32)
    one = lambda w: jnp.ones((n, w), F32)
    zero = lambda w: jnp.zeros((n, w), F32)
    scale = LOG2E / math.sqrt(MLA_NOPE + MLA_ROPE)
    return {
        "cosq": jnp.concatenate([one(64), cos, cos, one(32)], axis=1) * scale,
        "sinq": jnp.concatenate([zero(64), sin, sin, zero(32)], axis=1) * scale,
        "cosk": jnp.concatenate([zero(64), cos, cos, zero(32)], axis=1),
        "sink": jnp.concatenate([zero(64), sin, sin, zero(32)], axis=1),
        "cosd": jnp.tile(jnp.concatenate([cos, cos], axis=1), (1, 8)),
        "sina": jnp.tile(jnp.concatenate([-sin, zero(16)], axis=1), (1, 8)),
        "sinb": jnp.tile(jnp.concatenate([zero(16), sin], axis=1), (1, 8)),
    }


def _layer(x, mod, lw, tabs, layer_idx, ctx, gf, *, tm, nsub, final):
    emit_cache = ctx is None
    outs = _inproj(x, mod, lw, tabs, tm=tm, emit_cache=emit_cache)
    q, k, vt, u_lru, u_pool, dq, dk, dvt = outs[:8]
    lam_init = 0.8 - 0.6 * math.exp(-0.3 * layer_idx)
    if ctx is None:
        B = x.shape[0]
        h0 = jnp.zeros((B, 2, LRU_WIDTH), F32)
        mla_ctx = diff_ctx = None
    else:
        ckv, kr_pad, cdk, cdv, h0 = ctx
        kc, vtc, dkc, dvtc = _ctx_prep(ckv, kr_pad, cdk, cdv, lw)
        mla_ctx = (kc, vtc)
        diff_ctx = (dkc, dvtc)
    y_mla = _mla_attn(q, k, vt, mla_ctx, nsub=nsub)
    y_lru, st = _lru(u_lru, h0, lw)
    y_pool = _pool(u_pool, lw)
    y_diff = _diff_attn(dq, dk, dvt, diff_ctx, lw["diff_lambda"], lw["diff_g"], nsub=nsub,
                        lam_init=lam_init)
    x1, h2 = _outproj(x, (y_mla, y_lru, y_pool, y_diff), mod, lw, tm=tm)
    x2 = _ffn(x1, h2, mod, lw, gf, tm=tm, final=final)
    cache = (outs[8], outs[9][..., 64:96], outs[10], outs[11], st) if emit_cache else None
    return x2, cache


def kernel(x_prompt, x_sample, cache_mla_ckv, cache_mla_krope, cache_diff_k, cache_diff_v, state_lru,
           c, c_ctx, w_ada, b_ada, norm1_g, norm2_g, w_in, mla_q_norm_g, mla_w_uq, mla_kv_norm_g,
           mla_w_ukv, lru_conv_w, lru_conv_b, lru_w_r, lru_b_r, lru_w_i, lru_b_i, lru_lambda, pool_w,
           pool_scale, diff_lambda, diff_norm_g, w_out, w_gu, w_down, final_norm_g):
    p = {
        "norm1_g": norm1_g, "norm2_g": norm2_g, "w_in": w_in, "mla_q_norm_g": mla_q_norm_g,
        "mla_w_uq": mla_w_uq, "mla_kv_norm_g": mla_kv_norm_g, "mla_w_ukv": mla_w_ukv,
        "lru_conv_w": lru_conv_w, "lru_conv_b": lru_conv_b, "lru_w_r": lru_w_r, "lru_b_r": lru_b_r,
        "lru_w_i": lru_w_i, "lru_b_i": lru_b_i, "lru_lambda": lru_lambda, "pool_w": pool_w,
        "pool_scale": pool_scale, "diff_lambda": diff_lambda, "diff_norm_g": diff_norm_g,
        "w_out": w_out, "w_gu": w_gu, "w_down": w_down,
    }
    Bp, Np, _ = x_prompt.shape
    Bs, Ns, _ = x_sample.shape
    P = cache_mla_ckv.shape[2]

    cond_all = jnp.concatenate([c, c_ctx[None, :], jnp.zeros((16 - Bs - 1, D_MODEL), F32)], axis=0)
    mod = _ada(cond_all, w_ada, b_ada)
    tabs_p = _rope_tables(Np, positional=False)
    tabs_s = _rope_tables(Ns, positional=True)
    kr_pad = jnp.pad(cache_mla_krope, ((0, 0), (0, 0), (0, 0), (MLA_NOPE, MLA_SLOT - MLA_NOPE - MLA_ROPE)))
    cdk = cache_diff_k.reshape(Bs, DEPTH, P, 256)
    cdv = cache_diff_v.reshape(Bs, DEPTH, P, 256)
    gf = final_norm_g[None, :]

    xp, xs = x_prompt, x_sample
    caches = []
    for l in range(DEPTH):
        lw = _layer_weights(l, p)
        final = l == DEPTH - 1
        mod_p = mod[l, Bs:Bs + 1].reshape(1, 1, 6 * D_MODEL)
        mod_s = mod[l, 0:Bs].reshape(Bs, 1, 6 * D_MODEL)
        xp, cache = _layer(xp, mod_p, lw, tabs_p, l, None, gf, tm=Np, nsub=1, final=final)
        caches.append(cache)
        ctx = (cache_mla_ckv[:, l], kr_pad[:, l], cdk[:, l], cdv[:, l], state_lru[:, l])
        xs, _ = _layer(xs, mod_s, lw, tabs_s, l, ctx, gf, tm=512, nsub=2, final=final)

    new_mla_ckv = jnp.stack([cc[0] for cc in caches], axis=1)
    new_mla_krope = jnp.stack([cc[1] for cc in caches], axis=1)
    new_diff_k = jnp.stack([cc[2] for cc in caches], axis=1).reshape(
        Bp, DEPTH, Np, DIFF_HEADS, 2, DIFF_DIM)
    new_diff_v = jnp.stack([cc[3] for cc in caches], axis=1).reshape(
        Bp, DEPTH, Np, DIFF_HEADS, 2 * DIFF_DIM)
    new_state_lru = jnp.stack([cc[4] for cc in caches], axis=1)
    return (xp, xs, new_mla_ckv, new_mla_krope, new_diff_k, new_diff_v, new_state_lru)
```

```python
import functools
import math

import jax
import jax.numpy as jnp
from jax import lax
from jax.experimental import pallas as pl
from jax.experimental.pallas import tpu as pltpu

F32 = jnp.float32
BF16 = jnp.bfloat16

D_MODEL = 1024
DEPTH = 2
GRID_W = 64
GROUP_WIDTH = 256
MLA_HEADS = 4
MLA_NOPE = 64
MLA_ROPE = 32
MLA_V = 64
MLA_Q_RANK = 192
MLA_KV_RANK = 128
MLA_SLOT = 128
LRU_WIDTH = 256
LRU_C = 8.0
POOL_WINDOWS = (2, 4, 8, 16)
POOL_CH = 64
DIFF_HEADS = 4
DIFF_DIM = 32
HEAD_V = 64
FF_HIDDEN = 2816
FF_CHUNKS = ((0, 1536), (1536, 2816))
ROPE_BASE = 10000.0
EPS = 1e-6
IN_EFF = 2048
HALO = 8
VT_ROWS = 80
ATT_TQ = 256
TOKEN_TILE = 512
LOG2E = math.log2(math.e)

VMEM_LIMIT_BYTES = 56 * 1024 * 1024

_NT = (((1,), (1,)), ((), ()))


def _params(*sem):
    return pltpu.CompilerParams(dimension_semantics=sem, vmem_limit_bytes=VMEM_LIMIT_BYTES)


def _resident(shape):
    zeros = (0,) * len(shape)
    return pl.BlockSpec(shape, lambda *_: zeros, pipeline_mode=pl.Buffered(1))


def _dot(a, b):
    return jnp.dot(a, b, preferred_element_type=F32)


def _dot_nt(a, b):
    return lax.dot_general(a, b, _NT, preferred_element_type=F32)


def _rms_rows(x, width):
    ms = jnp.sum(x * x, axis=-1, keepdims=True) * (1.0 / width)
    return x * lax.rsqrt(ms + EPS)


def _store_vt(vt_ref, v):
    vt = v.T
    rows = v.shape[0]
    pad = VT_ROWS - HEAD_V
    ones_row = jnp.where(lax.broadcasted_iota(jnp.int32, (pad, rows), 0) == 0, 1.0, 0.0).astype(BF16)
    for hh in range(vt_ref.shape[0]):
        vt_ref[hh, 0:HEAD_V, :] = vt[hh * HEAD_V:(hh + 1) * HEAD_V, :].astype(BF16)
        vt_ref[hh, HEAD_V:VT_ROWS, :] = ones_row


def _mod_spec(mod, batch_of):
    if mod.shape[0] == 1:
        return pl.BlockSpec((1, 1, 6 * D_MODEL), lambda *g: (0, 0, 0))
    return pl.BlockSpec((1, 1, 6 * D_MODEL), lambda *g: (batch_of(*g), 0, 0))


def _ada_kernel(cond_ref, w_ref, b_ref, out_ref):
    c = cond_ref[...]
    s = c * jax.nn.sigmoid(c)
    out_ref[0] = _dot(s.astype(BF16), w_ref[0].astype(BF16)) + b_ref[0]


def _ada(cond_all, w_ada, b_ada):
    rows = cond_all.shape[0]
    tn = 1536
    return pl.pallas_call(
        _ada_kernel,
        grid=(DEPTH, 6 * D_MODEL // tn),
        in_specs=[
            pl.BlockSpec((rows, D_MODEL), lambda l, j: (0, 0)),
            pl.BlockSpec((1, D_MODEL, tn), lambda l, j: (l, 0, j)),
            pl.BlockSpec((1, 1, tn), lambda l, j: (l, 0, j)),
        ],
        out_specs=pl.BlockSpec((1, rows, tn), lambda l, j: (l, 0, j)),
        out_shape=jax.ShapeDtypeStruct((DEPTH, rows, 6 * D_MODEL), F32),
        compiler_params=_params("arbitrary", "arbitrary"),
        name="ada_mod",
    )(cond_all, w_ada, b_ada.reshape(DEPTH, 1, 6 * D_MODEL))


def _inproj_kernel(x_ref, mod_ref, g1_ref, win_ref, gq_ref, gkv_ref, wq_ref, wqr_ref, wk_ref, wv_ref,
                   cosq_ref, sinq_ref, cosk_ref, sink_ref, cosd_ref, sina_ref, sinb_ref,
                   q_out, k_out, vt_out, lru_out, pool_out, dq_out, dk_out, dvt_out, *cache_outs):
    x = x_ref[...]
    mod = mod_ref[0]
    sh1 = mod[:, 0:D_MODEL]
    sc1 = mod[:, D_MODEL:2 * D_MODEL]
    h = _rms_rows(x, D_MODEL) * g1_ref[...]
    hb = (h * (1.0 + sc1) + sh1).astype(BF16)

    t01 = _dot(hb, win_ref[:, 0:256])
    lane = lax.broadcasted_iota(jnp.int32, (1, 256), 1)
    cq = jnp.where(lane < MLA_Q_RANK, t01, 0.0)
    cqn = (_rms_rows(cq, MLA_Q_RANK) * gq_ref[...]).astype(BF16)
    qa = _dot(cqn, wq_ref[...])
    qr = _dot(cqn, wqr_ref[...])
    cosq = cosq_ref[...]
    sinq = sinq_ref[...]
    ckv = _dot(hb, win_ref[:, 256:384])
    lat = _rms_rows(ckv, MLA_KV_RANK) * gkv_ref[...]
    latb = lat.astype(BF16)
    kk = _dot(latb, wk_ref[...])
    _store_vt(vt_out, _dot(latb, wv_ref[...]))
    t1 = t01[:, 128:256]
    t3 = _dot(hb, win_ref[:, 384:512])
    kro = t1 * cosk_ref[...] + t3 * sink_ref[...]
    for hh in range(MLA_HEADS):
        sl = slice(hh * MLA_SLOT, (hh + 1) * MLA_SLOT)
        q_out[hh] = (qa[:, sl] * cosq + qr[:, sl] * sinq).astype(q_out.dtype)
        k_out[hh] = (kk[:, sl] + kro).astype(k_out.dtype)

    lru_out[...] = _dot(hb, win_ref[:, 512:1024])
    pool_out[...] = _dot(hb, win_ref[:, 1024:1280])

    cosd = cosd_ref[...]
    sina = sina_ref[...]
    sinb = sinb_ref[...]

    def rope(t):
        return t * cosd + pltpu.roll(t, 256 - 16, 1) * sina + pltpu.roll(t, 16, 1) * sinb

    dq = _dot(hb, win_ref[:, 1280:1536])
    dk = _dot(hb, win_ref[:, 1536:1792])
    dv = _dot(hb, win_ref[:, 1792:2048])
    dq_out[...] = (rope(dq) * (LOG2E / math.sqrt(DIFF_DIM))).astype(dq_out.dtype)
    dk_out[...] = rope(dk).astype(dk_out.dtype)
    _store_vt(dvt_out, dv)

    if cache_outs:
        lat_out, kr_out, dk_raw_out, dv_raw_out = cache_outs
        lat_out[...] = lat
        kr_out[...] = t1
        dk_raw_out[...] = dk
        dv_raw_out[...] = dv


def _inproj(x, mod, lw, tabs, *, nb, n, emit_cache):
    T = nb * n
    tm = TOKEN_TILE
    npt = n // tm
    row_blk = lambda j, b: b * npt + j

    def tok(width):
        return pl.BlockSpec((tm, width), lambda j, b: (row_blk(j, b), 0))

    def tab(width):
        return pl.BlockSpec((tm, width), lambda j, b: (j, 0))

    head = pl.BlockSpec((MLA_HEADS, tm, MLA_SLOT), lambda j, b: (0, row_blk(j, b), 0))
    vt_spec = pl.BlockSpec((MLA_HEADS, VT_ROWS, tm), lambda j, b: (0, 0, row_blk(j, b)))
    in_specs = [
        tok(D_MODEL), _mod_spec(mod, lambda j, b: b), _resident((1, D_MODEL)), _resident((D_MODEL, IN_EFF)),
        _resident((1, 256)), _resident((1, 128)), _resident((256, 512)), _resident((256, 512)),
        _resident((128, 512)), _resident((128, 256)),
        tab(128), tab(128), tab(128), tab(128), tab(256), tab(256), tab(256),
    ]
    out_specs = [head, head, vt_spec, tok(512), tok(256), tok(256), tok(256), vt_spec]
    vt_shape = jax.ShapeDtypeStruct((MLA_HEADS, VT_ROWS, T), BF16)
    out_shape = [
        jax.ShapeDtypeStruct((MLA_HEADS, T, MLA_SLOT), BF16),
        jax.ShapeDtypeStruct((MLA_HEADS, T, MLA_SLOT), BF16),
        vt_shape,
        jax.ShapeDtypeStruct((T, 512), F32),
        jax.ShapeDtypeStruct((T, 256), F32),
        jax.ShapeDtypeStruct((T, 256), BF16),
        jax.ShapeDtypeStruct((T, 256), BF16),
        vt_shape,
    ]
    if emit_cache:
        out_specs += [tok(128), tok(128), tok(256), tok(256)]
        out_shape += [jax.ShapeDtypeStruct((T, 128), F32), jax.ShapeDtypeStruct((T, 128), F32),
                      jax.ShapeDtypeStruct((T, 256), F32), jax.ShapeDtypeStruct((T, 256), F32)]
    return pl.pallas_call(
        _inproj_kernel,
        grid=(npt, nb),
        in_specs=in_specs,
        out_specs=out_specs,
        out_shape=out_shape,
        compiler_params=_params("arbitrary", "arbitrary"),
        name="inproj_cache" if emit_cache else "inproj",
    )(x, mod, lw["g1"], lw["w_in"], lw["gq"], lw["gkv"], lw["wq"], lw["wqr"], lw["wk"], lw["wv"],
      tabs["cosq"], tabs["sinq"], tabs["cosk"], tabs["sink"], tabs["cosd"], tabs["sina"], tabs["sinb"])


def _ctx_prep_kernel(ckv_ref, kr_ref, dk_ref, dv_ref, wk_ref, wv_ref, k_out, vt_out, dk_out, dvt_out):
    latb = ckv_ref[...].astype(BF16)
    kk = _dot(latb, wk_ref[...])
    kr = kr_ref[...]
    for hh in range(MLA_HEADS):
        k_out[hh] = (kk[:, hh * MLA_SLOT:(hh + 1) * MLA_SLOT] + kr).astype(k_out.dtype)
    _store_vt(vt_out, _dot(latb, wv_ref[...]))
    dk_out[...] = dk_ref[...].astype(dk_out.dtype)
    _store_vt(dvt_out, dv_ref[...])


def _ctx_prep(ckv, kr_pad, cdk, cdv, lw, *, nb, p):
    T = nb * p
    row = lambda w: pl.BlockSpec((p, w), lambda b: (b, 0))
    vt_spec = pl.BlockSpec((MLA_HEADS, VT_ROWS, p), lambda b: (0, 0, b))
    vt_shape = jax.ShapeDtypeStruct((MLA_HEADS, VT_ROWS, T), BF16)
    return pl.pallas_call(
        _ctx_prep_kernel,
        grid=(nb,),
        in_specs=[row(128), row(128), row(256), row(256), _resident((128, 512)), _resident((128, 256))],
        out_specs=[pl.BlockSpec((MLA_HEADS, p, MLA_SLOT), lambda b: (0, b, 0)), vt_spec, row(256), vt_spec],
        out_shape=[jax.ShapeDtypeStruct((MLA_HEADS, T, MLA_SLOT), BF16), vt_shape,
                   jax.ShapeDtypeStruct((T, 256), BF16), vt_shape],
        compiler_params=_params("arbitrary"),
        name="ctx_prep",
    )(ckv, kr_pad, cdk, cdv, lw["wk"], lw["wv"])


def _run_pipeline(n_maps, scores, exps, finish):
    ms = {0: scores(0)}
    for u in range(n_maps):
        if u + 1 < n_maps:
            ms[u + 1] = scores(u + 1)
        exps(u, ms.pop(u))
        if u >= 1:
            finish(u - 1)
    finish(n_maps - 1)


def _score_stage(s_buf, k_new, k_ctx, q, n_ctx):
    sn = _dot_nt(k_new, q)
    s_buf[n_ctx:, :] = sn
    m = jnp.max(sn, axis=0, keepdims=True)
    if k_ctx is not None:
        sc = _dot_nt(k_ctx, q)
        s_buf[0:n_ctx, :] = sc
        m = jnp.maximum(m, jnp.max(sc, axis=0, keepdims=True))
    return m


def _value_stage(e_buf, vt_new, vt_ctx, n_ctx):
    o = _dot(vt_new, e_buf[n_ctx:, :])
    if vt_ctx is not None:
        o = o + _dot(vt_ctx, e_buf[0:n_ctx, :])
    return o


def _att_scratch(nk):
    return [pltpu.VMEM((nk, ATT_TQ), F32), pltpu.VMEM((nk, ATT_TQ), F32),
            pltpu.VMEM((nk, ATT_TQ), BF16), pltpu.VMEM((nk, ATT_TQ), BF16),
            pltpu.VMEM((MLA_HEADS * HEAD_V, ATT_TQ), F32)]


def _att_nsub(n):
    return 2 if n % (2 * ATT_TQ) == 0 else 1


def _mla_attn_kernel(*refs, has_ctx, nsub):
    if has_ctx:
        q_ref, k_ref, vt_ref, kc_ref, vtc_ref, o_ref, s0, s1, e0, e1, ot = refs
        n_ctx = kc_ref.shape[1]
    else:
        q_ref, k_ref, vt_ref, o_ref, s0, s1, e0, e1, ot = refs
        n_ctx = 0
    s_bufs, e_bufs = (s0, s1), (e0, e1)

    def scores(u):
        t, hh = divmod(u, MLA_HEADS)
        q = q_ref[hh, t * ATT_TQ:(t + 1) * ATT_TQ, :]
        return _score_stage(s_bufs[u % 2], k_ref[hh], kc_ref[hh] if has_ctx else None, q, n_ctx)

    def exps(u, m):
        e_bufs[u % 2][...] = jnp.exp2(s_bufs[u % 2][...] - m).astype(BF16)

    def finish(u):
        t, hh = divmod(u, MLA_HEADS)
        o = _value_stage(e_bufs[u % 2], vt_ref[hh], vtc_ref[hh] if has_ctx else None, n_ctx)
        ot[hh * HEAD_V:(hh + 1) * HEAD_V, :] = o[0:HEAD_V, :] * (1.0 / o[HEAD_V:HEAD_V + 1, :])
        if hh == MLA_HEADS - 1:
            o_ref[t * ATT_TQ:(t + 1) * ATT_TQ, :] = ot[...].T

    _run_pipeline(nsub * MLA_HEADS, scores, exps, finish)


def _mla_attn(q, k, vt, ctx, *, nb, n):
    nsub = _att_nsub(n)
    tq = nsub * ATT_TQ
    npt = n // tq
    H, S = MLA_HEADS, MLA_SLOT
    in_specs = [
        pl.BlockSpec((H, tq, S), lambda b, j: (0, b * npt + j, 0)),
        pl.BlockSpec((H, n, S), lambda b, j: (0, b, 0)),
        pl.BlockSpec((H, VT_ROWS, n), lambda b, j: (0, 0, b)),
    ]
    args = [q, k, vt]
    n_ctx = 0
    if ctx is not None:
        n_ctx = ctx[0].shape[1] // nb
        in_specs += [
            pl.BlockSpec((H, n_ctx, S), lambda b, j: (0, b, 0)),
            pl.BlockSpec((H, VT_ROWS, n_ctx), lambda b, j: (0, 0, b)),
        ]
        args += list(ctx)
    return pl.pallas_call(
        functools.partial(_mla_attn_kernel, has_ctx=ctx is not None, nsub=nsub),
        grid=(nb, npt),
        in_specs=in_specs,
        out_specs=pl.BlockSpec((tq, 256), lambda b, j: (b * npt + j, 0)),
        out_shape=jax.ShapeDtypeStruct((nb * n, 256), F32),
        scratch_shapes=_att_scratch(n + n_ctx),
        compiler_params=_params("arbitrary", "arbitrary"),
        name="mla_attn_ctx" if ctx is not None else "mla_attn",
    )(*args)


def _diff_attn_kernel(*refs, has_ctx, nsub, lam_init):
    if has_ctx:
        lv_ref, g_ref, q_ref, k_ref, vt_ref, kc_ref, vtc_ref, o_ref, s0, s1, e0, e1, ot = refs
        n_ctx = kc_ref.shape[0]
    else:
        lv_ref, g_ref, q_ref, k_ref, vt_ref, o_ref, s0, s1, e0, e1, ot = refs
        n_ctx = 0
    s_bufs, e_bufs = (s0, s1), (e0, e1)
    lv = lv_ref[...]
    lam = (jnp.exp(jnp.sum(lv[0:1] * lv[1:2], axis=-1, keepdims=True))
           - jnp.exp(jnp.sum(lv[2:3] * lv[3:4], axis=-1, keepdims=True)) + lam_init)
    lane = lax.broadcasted_iota(jnp.int32, (1, 256), 1)
    n_pairs = 2 * DIFF_HEADS
    outs = {}

    def scores(u):
        t, p = divmod(u, n_pairs)
        q = q_ref[t * ATT_TQ:(t + 1) * ATT_TQ, :]
        in_pair = (lane >= p * DIFF_DIM) & (lane < (p + 1) * DIFF_DIM)
        qm = jnp.where(in_pair, q, jnp.zeros_like(q))
        return _score_stage(s_bufs[u % 2], k_ref[...], kc_ref[...] if has_ctx else None, qm, n_ctx)

    def exps(u, m):
        e_bufs[u % 2][...] = jnp.exp2(s_bufs[u % 2][...] - m).astype(BF16)

    def finish(u):
        t, p = divmod(u, n_pairs)
        hh = p // 2
        outs[u] = _value_stage(e_bufs[u % 2], vt_ref[hh], vtc_ref[hh] if has_ctx else None, n_ctx)
        if p % 2 == 1:
            o0, o1 = outs.pop(u - 1), outs.pop(u)
            w0 = 1.0 / o0[HEAD_V:HEAD_V + 1, :]
            w1 = lam / o1[HEAD_V:HEAD_V + 1, :]
            o = o0[0:HEAD_V, :] * w0 - o1[0:HEAD_V, :] * w1
            msq = jnp.sum(o * o, axis=0, keepdims=True) * (1.0 / HEAD_V)
            ot[hh * HEAD_V:(hh + 1) * HEAD_V, :] = o * lax.rsqrt(msq + EPS)
        if p == n_pairs - 1:
            o_ref[t * ATT_TQ:(t + 1) * ATT_TQ, :] = (ot[...].T * g_ref[...]) * (1.0 - lam_init)

    _run_pipeline(nsub * n_pairs, scores, exps, finish)


def _diff_attn(q, k, vt, ctx, lv, g4, *, nb, n, lam_init):
    nsub = _att_nsub(n)
    tq = nsub * ATT_TQ
    npt = n // tq
    in_specs = [
        _resident((4, DIFF_DIM)),
        _resident((1, 256)),
        pl.BlockSpec((tq, 256), lambda b, j: (b * npt + j, 0)),
        pl.BlockSpec((n, 256), lambda b, j: (b, 0)),
        pl.BlockSpec((DIFF_HEADS, VT_ROWS, n), lambda b, j: (0, 0, b)),
    ]
    args = [lv, g4, q, k, vt]
    n_ctx = 0
    if ctx is not None:
        n_ctx = ctx[0].shape[0] // nb
        in_specs += [pl.BlockSpec((n_ctx, 256), lambda b, j: (b, 0)),
                     pl.BlockSpec((DIFF_HEADS, VT_ROWS, n_ctx), lambda b, j: (0, 0, b))]
        args += list(ctx)
    return pl.pallas_call(
        functools.partial(_diff_attn_kernel, has_ctx=ctx is not None, nsub=nsub, lam_init=lam_init),
        grid=(nb, npt),
        in_specs=in_specs,
        out_specs=pl.BlockSpec((tq, 256), lambda b, j: (b * npt + j, 0)),
        out_shape=jax.ShapeDtypeStruct((nb * n, 256), F32),
        scratch_shapes=_att_scratch(n + n_ctx),
        compiler_params=_params("arbitrary", "arbitrary"),
        name="diff_attn_ctx" if ctx is not None else "diff_attn",
    )(*args)


def _shift_rows(v, k):
    return pltpu.roll(v, (-k) % v.shape[0], 0)


def _scan_chunk(a, b, reverse):
    T = a.shape[0]
    row = lax.broadcasted_iota(jnp.int32, a.shape, 0)
    s = 1
    while s < T:
        if reverse:
            valid = row < T - s
            ap, bp = _shift_rows(a, s), _shift_rows(b, s)
        else:
            valid = row >= s
            ap, bp = _shift_rows(a, -s), _shift_rows(b, -s)
        b = jnp.where(valid, a * bp + b, b)
        a = jnp.where(valid, a * ap, a)
        s *= 2
    return a, b


def _gelu_tanh(x):
    return x * (0.5 * (1.0 + jnp.tanh(math.sqrt(2.0 / math.pi) * (x + 0.044715 * (x * x * x)))))


def _lru_kernel(u_ref, h0_ref, cw_ref, cb_ref, wg_ref, bg_ref, lam_ref, y_ref, st_ref,
                xpad, a1s, b1s, *, N, T):
    W = LRU_WIDTH
    nc = N // T
    zeros = jnp.zeros((HALO, W), F32)
    xpad[0:HALO, :] = zeros
    xpad[N + HALO:N + 2 * HALO, :] = zeros

    def fill(j, carry):
        r0 = pl.multiple_of(j * T, T)
        xpad[pl.ds(r0 + HALO, T), :] = u_ref[pl.ds(r0, T), 0:W]
        return carry

    lax.fori_loop(0, nc, fill, 0)

    z = -lam_ref[...]
    sp = jnp.maximum(z, 0.0) + jnp.log1p(jnp.exp(-jnp.abs(z)))
    cw = cw_ref[...]
    cb = cb_ref[...]
    bg = bg_ref[...]

    def fwd(j, carry):
        r0 = pl.multiple_of(j * T, T)
        ext = xpad[pl.ds(r0, T + 2 * HALO), :]
        body = slice(HALO, HALO + T)
        xc = cb
        for tap in range(4):
            xc = xc + _shift_rows(ext, tap - 1)[body] * cw[tap:tap + 1]
        g = jax.nn.sigmoid(_dot(xc.astype(BF16), wg_ref[...]) + bg)
        ab = []
        for d in range(2):
            r = g[:, d * W:(d + 1) * W]
            i = g[:, (2 + d) * W:(3 + d) * W]
            log_a = (-LRU_C * r) * sp[d:d + 1]
            a = jnp.exp(log_a)
            bt = (jnp.sqrt(1.0 - jnp.exp(2.0 * log_a)) * i) * xc
            ab.append((a, bt))
        a1s[pl.ds(r0, T), :] = ab[1][0]
        b1s[pl.ds(r0, T), :] = ab[1][1]
        A, Bv = _scan_chunk(ab[0][0], ab[0][1], reverse=False)
        h = A * carry + Bv
        y_ref[pl.ds(r0, T), :] = h
        return h[T - 1:T, :]

    cf = lax.fori_loop(0, nc, fwd, h0_ref[0, 0:1, :])

    def bwd(jj, carry):
        r0 = pl.multiple_of((nc - 1 - jj) * T, T)
        A, Bv = _scan_chunk(a1s[pl.ds(r0, T), :], b1s[pl.ds(r0, T), :], reverse=True)
        h = A * carry + Bv
        gb = u_ref[pl.ds(r0, T), W:2 * W]
        y_ref[pl.ds(r0, T), :] = (y_ref[pl.ds(r0, T), :] + h) * _gelu_tanh(gb)
        return h[0:1, :]

    cbw = lax.fori_loop(0, nc, bwd, h0_ref[0, 1:2, :])
    st_ref[0, 0:1, :] = cf
    st_ref[0, 1:2, :] = cbw


def _lru(u, h0, lw, *, nb, n):
    T = min(n, 256)
    W = LRU_WIDTH
    return pl.pallas_call(
        functools.partial(_lru_kernel, N=n, T=T),
        grid=(nb,),
        in_specs=[
            pl.BlockSpec((n, 2 * W), lambda b: (b, 0)),
            pl.BlockSpec((1, 2, W), lambda b: (b, 0, 0)),
            _resident((4, W)), _resident((1, W)), _resident((W, 4 * W)), _resident((1, 4 * W)),
            _resident((2, W)),
        ],
        out_specs=[
            pl.BlockSpec((n, W), lambda b: (b, 0)),
            pl.BlockSpec((1, 2, W), lambda b: (b, 0, 0)),
        ],
        out_shape=[
            jax.ShapeDtypeStruct((nb * n, W), F32),
            jax.ShapeDtypeStruct((nb, 2, W), F32),
        ],
        scratch_shapes=[
            pltpu.VMEM((n + 2 * HALO, W), F32),
            pltpu.VMEM((n, W), F32),
            pltpu.VMEM((n, W), F32),
        ],
        compiler_params=_params("arbitrary"),
        name="rglru",
    )(u, h0, lw["conv_w"], lw["conv_b"], lw["w_gate"], lw["b_gate"], lw["lru_lambda"])


def _pool_kernel(u_ref, wp_ref, sc_ref, y_ref, xpad, *, N, T):
    W = GROUP_WIDTH
    nc = N // T
    zeros = jnp.zeros((HALO, W), F32)
    xpad[0:HALO, :] = zeros
    xpad[N + HALO:N + 2 * HALO, :] = zeros

    def fill(j, carry):
        r0 = pl.multiple_of(j * T, T)
        xpad[pl.ds(r0 + HALO, T), :] = u_ref[pl.ds(r0, T), :]
        return carry

    lax.fori_loop(0, nc, fill, 0)

    grp = lax.broadcasted_iota(jnp.int32, (1, W), 1) // POOL_CH
    half = jnp.where(grp == 0, 1, jnp.where(grp == 1, 2, jnp.where(grp == 2, 4, 8)))
    scale = sc_ref[...]

    def chunk(j, carry):
        r0 = pl.multiple_of(j * T, T)
        ext = xpad[pl.ds(r0, T + 2 * HALO), :]
        w2 = _shift_rows(ext, -1) + ext
        w4 = _shift_rows(w2, -1) + _shift_rows(w2, 1)
        w8 = _shift_rows(w4, -2) + _shift_rows(w4, 2)
        w16 = _shift_rows(w8, -4) + _shift_rows(w8, 4)
        ws = jnp.where(grp == 0, w2, jnp.where(grp == 1, w4, jnp.where(grp == 2, w8, w16)))
        body = slice(HALO, HALO + T)
        t = r0 + lax.broadcasted_iota(jnp.int32, (T, W), 0)
        cnt = (jnp.minimum(t + half, N) - jnp.maximum(t - half, 0)).astype(F32)
        d = ws[body] / cnt - ext[body]
        y_ref[pl.ds(r0, T), :] = _dot(d.astype(BF16), wp_ref[...]) * scale
        return carry

    lax.fori_loop(0, nc, chunk, 0)


def _pool(u, lw, *, nb, n):
    W = GROUP_WIDTH
    T = min(n, 256)
    return pl.pallas_call(
        functools.partial(_pool_kernel, N=n, T=T),
        grid=(nb,),
        in_specs=[pl.BlockSpec((n, W), lambda b: (b, 0)), _resident((W, W)), _resident((1, W))],
        out_specs=pl.BlockSpec((n, W), lambda b: (b, 0)),
        out_shape=jax.ShapeDtypeStruct((nb * n, W), F32),
        scratch_shapes=[pltpu.VMEM((n + 2 * HALO, W), F32)],
        compiler_params=_params("arbitrary"),
        name="pool_mixer",
    )(u, lw["w_pool"], lw["pool_scale"])


def _mix_ffn_kernel(*refs, final):
    if final:
        (x_ref, ya_ref, yb_ref, yc_ref, yd_ref, mod_ref, g2_ref, wo_ref, wg_ref, wu_ref, wd_ref,
         gf_ref, o_ref) = refs
    else:
        (x_ref, ya_ref, yb_ref, yc_ref, yd_ref, mod_ref, g2_ref, wo_ref, wg_ref, wu_ref, wd_ref,
         o_ref) = refs
    mod = mod_ref[0]
    gate1 = mod[:, 2 * D_MODEL:3 * D_MODEL]
    sh2 = mod[:, 3 * D_MODEL:4 * D_MODEL]
    sc2 = mod[:, 4 * D_MODEL:5 * D_MODEL]
    gate2 = mod[:, 5 * D_MODEL:6 * D_MODEL]
    mix = None
    for i, y_ref in enumerate((ya_ref, yb_ref, yc_ref, yd_ref)):
        part = _dot(y_ref[...].astype(BF16), wo_ref[i * GROUP_WIDTH:(i + 1) * GROUP_WIDTH, :])
        mix = part if mix is None else mix + part
    x1 = x_ref[...] + gate1 * mix
    h = _rms_rows(x1, D_MODEL) * g2_ref[...]
    hb = (h * (1.0 + sc2) + sh2).astype(BF16)
    ff = None
    for lo, hi in FF_CHUNKS:
        g = _dot(hb, wg_ref[:, lo:hi])
        up = _dot(hb, wu_ref[:, lo:hi])
        act = ((g * jax.nn.sigmoid(g)) * up).astype(BF16)
        part = _dot(act, wd_ref[lo:hi, :])
        ff = part if ff is None else ff + part
    x2 = x1 + gate2 * ff
    if final:
        x2 = _rms_rows(x2, D_MODEL) * gf_ref[...]
    o_ref[...] = x2


def _mix_ffn(x, ys, mod, lw, gf, *, nb, n, final):
    T = nb * n
    tm = TOKEN_TILE
    npt = n // tm

    def tok(width):
        return pl.BlockSpec((tm, width), lambda i: (i, 0))

    in_specs = [tok(D_MODEL), tok(256), tok(256), tok(256), tok(256),
                _mod_spec(mod, lambda i: i // npt), _resident((1, D_MODEL)),
                _resident((D_MODEL, D_MODEL)), _resident((D_MODEL, FF_HIDDEN)),
                _resident((D_MODEL, FF_HIDDEN)), _resident((FF_HIDDEN, D_MODEL))]
    args = [x, *ys, mod, lw["g2"], lw["w_out"], lw["w_gate_ff"], lw["w_up_ff"], lw["w_down"]]
    if final:
        in_specs.append(_resident((1, D_MODEL)))
        args.append(gf)
    return pl.pallas_call(
        functools.partial(_mix_ffn_kernel, final=final),
        grid=(T // tm,),
        in_specs=in_specs,
        out_specs=tok(D_MODEL),
        out_shape=jax.ShapeDtypeStruct((T, D_MODEL), F32),
        compiler_params=_params("arbitrary"),
        name="mix_ffn_final" if final else "mix_ffn",
    )(*args)


def _block_diag(w):
    G, c, e = w.shape
    return jnp.einsum('gce,gh->gche', w, jnp.eye(G, dtype=w.dtype)).reshape(G * c, G * e)


def _rot_cols(w):
    return jnp.concatenate([-w[:, 16:32], w[:, 0:16]], axis=1)


def _layer_weights(l, p):
    w_in = p["w_in"][l]
    o1 = MLA_Q_RANK
    o2 = o1 + MLA_KV_RANK
    o3 = o2 + MLA_ROPE
    c_q, c_kv, k_r, rest = w_in[:, :o1], w_in[:, o1:o2], w_in[:, o2:o3], w_in[:, o3:]
    z = lambda n: jnp.zeros((D_MODEL, n), F32)
    w_in_eff = jnp.concatenate([c_q, k_r, z(32), c_kv, z(64), _rot_cols(k_r), z(32), rest], axis=1)

    w_uq = p["mla_w_uq"][l]
    qd = MLA_NOPE + MLA_ROPE
    wq_parts, wqr_parts = [], []
    zq = lambda n: jnp.zeros((MLA_Q_RANK, n), F32)
    for h in range(MLA_HEADS):
        wh = w_uq[:, h * qd:(h + 1) * qd]
        wq_parts += [wh, zq(MLA_SLOT - qd)]
        wqr_parts += [zq(MLA_NOPE), _rot_cols(wh[:, MLA_NOPE:]), zq(MLA_SLOT - qd)]
    pad_rows = lambda w: jnp.pad(w, ((0, 256 - MLA_Q_RANK), (0, 0)))
    w_ukv = p["mla_w_ukv"][l]
    wk_parts, wv_parts = [], []
    zk = jnp.zeros((MLA_KV_RANK, MLA_SLOT - MLA_NOPE), F32)
    for h in range(MLA_HEADS):
        base = h * (MLA_NOPE + MLA_V)
        wk_parts += [w_ukv[:, base:base + MLA_NOPE], zk]
        wv_parts.append(w_ukv[:, base + MLA_NOPE:base + MLA_NOPE + MLA_V])

    w_gate = jnp.concatenate([_block_diag(p["lru_w_r"][l, 0]), _block_diag(p["lru_w_r"][l, 1]),
                              _block_diag(p["lru_w_i"][l, 0]), _block_diag(p["lru_w_i"][l, 1])], axis=1)
    b_gate = jnp.concatenate([p["lru_b_r"][l, 0], p["lru_b_r"][l, 1],
                              p["lru_b_i"][l, 0], p["lru_b_i"][l, 1]])[None, :]
    w_gu = p["w_gu"][l]
    return {
        "g1": p["norm1_g"][l][None, :],
        "g2": p["norm2_g"][l][None, :],
        "w_in": w_in_eff.astype(BF16),
        "gq": jnp.pad(p["mla_q_norm_g"][l], (0, 256 - MLA_Q_RANK))[None, :],
        "gkv": p["mla_kv_norm_g"][l][None, :],
        "wq": pad_rows(jnp.concatenate(wq_parts, axis=1)).astype(BF16),
        "wqr": pad_rows(jnp.concatenate(wqr_parts, axis=1)).astype(BF16),
        "wk": jnp.concatenate(wk_parts, axis=1).astype(BF16),
        "wv": jnp.concatenate(wv_parts, axis=1).astype(BF16),
        "conv_w": p["lru_conv_w"][l],
        "conv_b": p["lru_conv_b"][l][None, :],
        "w_gate": w_gate.astype(BF16),
        "b_gate": b_gate,
        "lru_lambda": p["lru_lambda"][l],
        "w_pool": _block_diag(p["pool_w"][l]).astype(BF16),
        "pool_scale": p["pool_scale"][l][None, :],
        "diff_lambda": p["diff_lambda"][l],
        "diff_g": jnp.tile(p["diff_norm_g"][l], DIFF_HEADS)[None, :],
        "w_out": p["w_out"][l].astype(BF16),
        "w_gate_ff": w_gu[:, :FF_HIDDEN].astype(BF16),
        "w_up_ff": w_gu[:, FF_HIDDEN:].astype(BF16),
        "w_down": p["w_down"][l].astype(BF16),
    }


def _rope_tables(n, positional):
    quarter = MLA_ROPE // 4
    if positional:
        t = jnp.arange(n)
        row = (t // GRID_W).astype(F32)
        col = (t % GRID_W).astype(F32)
        inv = ROPE_BASE ** (-jnp.arange(quarter, dtype=F32) / quarter)
        ang = jnp.concatenate([row[:, None] * inv, col[:, None] * inv], axis=-1)
        cos, sin = jnp.cos(ang), jnp.sin(ang)
    else:
        cos, sin = jnp.ones((n, 16), F32), jnp.zeros((n, 16), F32)
    one = lambda w: jnp.ones((n, w), F32)
    zero = lambda w: jnp.zeros((n, w), F32)
    scale = LOG2E / math.sqrt(MLA_NOPE + MLA_ROPE)
    return {
        "cosq": jnp.concatenate([one(64), cos, cos, one(32)], axis=1) * scale,
        "sinq": jnp.concatenate([zero(64), sin, sin, zero(32)], axis=1) * scale,
        "cosk": jnp.concatenate([zero(64), cos, cos, zero(32)], axis=1),
        "sink": jnp.concatenate([zero(64), sin, sin, zero(32)], axis=1),
        "cosd": jnp.tile(jnp.concatenate([cos, cos], axis=1), (1, 8)),
        "sina": jnp.tile(jnp.concatenate([-sin, zero(16)], axis=1), (1, 8)),
        "sinb": jnp.tile(jnp.concatenate([zero(16), sin], axis=1), (1, 8)),
    }


def _layer(x, mod, lw, tabs, layer_idx, ctx, gf, *, nb, n, final):
    emit_cache = ctx is None
    tok_nb, tok_n = (1, nb * n) if mod.shape[0] == 1 else (nb, n)
    outs = _inproj(x, mod, lw, tabs, nb=tok_nb, n=tok_n, emit_cache=emit_cache)
    q, k, vt, u_lru, u_pool, dq, dk, dvt = outs[:8]
    lam_init = 0.8 - 0.6 * math.exp(-0.3 * layer_idx)
    if ctx is None:
        h0 = jnp.zeros((nb, 2, LRU_WIDTH), F32)
        mla_ctx = diff_ctx = None
    else:
        ckv, kr_pad, cdk, cdv, h0 = ctx
        p = ckv.shape[0] // nb
        kc, vtc, dkc, dvtc = _ctx_prep(ckv, kr_pad, cdk, cdv, lw, nb=nb, p=p)
        mla_ctx = (kc, vtc)
        diff_ctx = (dkc, dvtc)
    y_mla = _mla_attn(q, k, vt, mla_ctx, nb=nb, n=n)
    y_lru, st = _lru(u_lru, h0, lw, nb=nb, n=n)
    y_pool = _pool(u_pool, lw, nb=nb, n=n)
    y_diff = _diff_attn(dq, dk, dvt, diff_ctx, lw["diff_lambda"], lw["diff_g"], nb=nb, n=n,
                        lam_init=lam_init)
    x2 = _mix_ffn(x, (y_mla, y_lru, y_pool, y_diff), mod, lw, gf, nb=tok_nb, n=tok_n, final=final)
    cache = (outs[8], outs[9][:, 64:96], outs[10], outs[11], st) if emit_cache else None
    return x2, cache


def kernel(x_prompt, x_sample, cache_mla_ckv, cache_mla_krope, cache_diff_k, cache_diff_v, state_lru,
           c, c_ctx, w_ada, b_ada, norm1_g, norm2_g, w_in, mla_q_norm_g, mla_w_uq, mla_kv_norm_g,
           mla_w_ukv, lru_conv_w, lru_conv_b, lru_w_r, lru_b_r, lru_w_i, lru_b_i, lru_lambda, pool_w,
           pool_scale, diff_lambda, diff_norm_g, w_out, w_gu, w_down, final_norm_g):
    p = {
        "norm1_g": norm1_g, "norm2_g": norm2_g, "w_in": w_in, "mla_q_norm_g": mla_q_norm_g,
        "mla_w_uq": mla_w_uq, "mla_kv_norm_g": mla_kv_norm_g, "mla_w_ukv": mla_w_ukv,
        "lru_conv_w": lru_conv_w, "lru_conv_b": lru_conv_b, "lru_w_r": lru_w_r, "lru_b_r": lru_b_r,
        "lru_w_i": lru_w_i, "lru_b_i": lru_b_i, "lru_lambda": lru_lambda, "pool_w": pool_w,
        "pool_scale": pool_scale, "diff_lambda": diff_lambda, "diff_norm_g": diff_norm_g,
        "w_out": w_out, "w_gu": w_gu, "w_down": w_down,
    }
    Bp, Np, _ = x_prompt.shape
    Bs, Ns, _ = x_sample.shape
    P = cache_mla_ckv.shape[2]

    cond_all = jnp.concatenate([c, c_ctx[None, :], jnp.zeros((16 - Bs - 1, D_MODEL), F32)], axis=0)
    mod = _ada(cond_all, w_ada, b_ada)
    tabs_p = _rope_tables(Bp * Np, positional=False)
    tabs_s = _rope_tables(Ns, positional=True)
    kr_pad = jnp.pad(cache_mla_krope, ((0, 0), (0, 0), (0, 0), (MLA_NOPE, MLA_SLOT - MLA_NOPE - MLA_ROPE)))
    cdk = cache_diff_k.reshape(Bs, DEPTH, P, 256)
    cdv = cache_diff_v.reshape(Bs, DEPTH, P, 256)
    gf = final_norm_g[None, :]

    xp = x_prompt.reshape(Bp * Np, D_MODEL)
    xs = x_sample.reshape(Bs * Ns, D_MODEL)
    caches = []
    for l in range(DEPTH):
        lw = _layer_weights(l, p)
        final = l == DEPTH - 1
        mod_p = mod[l, Bs:Bs + 1].reshape(1, 1, 6 * D_MODEL)
        mod_s = mod[l, 0:Bs].reshape(Bs, 1, 6 * D_MODEL)
        xp, cache = _layer(xp, mod_p, lw, tabs_p, l, None, gf, nb=Bp, n=Np, final=final)
        caches.append(cache)
        flat = lambda a: a[:, l].reshape(Bs * P, a.shape[-1])
        ctx = (flat(cache_mla_ckv), flat(kr_pad), flat(cdk), flat(cdv), state_lru[:, l])
        xs, _ = _layer(xs, mod_s, lw, tabs_s, l, ctx, gf, nb=Bs, n=Ns, final=final)

    stack = lambda i, w: jnp.stack([cc[i].reshape(Bp, Np, w) for cc in caches], axis=1)
    new_mla_ckv = stack(0, MLA_KV_RANK)
    new_mla_krope = stack(1, MLA_ROPE)
    new_diff_k = stack(2, 256).reshape(Bp, DEPTH, Np, DIFF_HEADS, 2, DIFF_DIM)
    new_diff_v = stack(3, 256).reshape(Bp, DEPTH, Np, DIFF_HEADS, 2 * DIFF_DIM)
    new_state_lru = jnp.stack([cc[4] for cc in caches], axis=1)
    return (xp.reshape(Bp, Np, D_MODEL), xs.reshape(Bs, Ns, D_MODEL),
            new_mla_ckv, new_mla_krope, new_diff_k, new_diff_v, new_state_lru)
```

```python
import functools
import math

import jax
import jax.numpy as jnp
from jax import lax
from jax.experimental import pallas as pl
from jax.experimental.pallas import tpu as pltpu

F32 = jnp.float32
BF16 = jnp.bfloat16

D_MODEL = 1024
DEPTH = 2
GRID_W = 64
GROUP_WIDTH = 256
MLA_HEADS = 4
MLA_NOPE = 64
MLA_ROPE = 32
MLA_V = 64
MLA_Q_RANK = 192
MLA_KV_RANK = 128
MLA_SLOT = 128
LRU_WIDTH = 256
LRU_C = 8.0
POOL_WINDOWS = (2, 4, 8, 16)
POOL_CH = 64
DIFF_HEADS = 4
DIFF_DIM = 32
HEAD_V = 64
FF_HIDDEN = 2816
FF_CHUNKS = ((0, 1536), (1536, 2816))
ROPE_BASE = 10000.0
EPS = 1e-6
IN_EFF = 2048
HALO = 8
VT_ROWS = 80
ATT_TQ = 256
TOKEN_TILE = 512
MOD_ROWS = 16
LOG2E = math.log2(math.e)

VMEM_LIMIT_BYTES = 56 * 1024 * 1024

_NT = (((1,), (1,)), ((), ()))


def _params(*sem):
    return pltpu.CompilerParams(dimension_semantics=sem, vmem_limit_bytes=VMEM_LIMIT_BYTES)


def _resident(shape):
    zeros = (0,) * len(shape)
    return pl.BlockSpec(shape, lambda *_: zeros, pipeline_mode=pl.Buffered(1))


def _dot(a, b):
    return jnp.dot(a, b, preferred_element_type=F32)


def _dot_nt(a, b):
    return lax.dot_general(a, b, _NT, preferred_element_type=F32)


def _rms_rows(x, width):
    ms = jnp.sum(x * x, axis=-1, keepdims=True) * (1.0 / width)
    return x * lax.rsqrt(ms + EPS)


def _store_vt(vt_ref, v):
    vt = v.T
    rows = v.shape[0]
    pad = VT_ROWS - HEAD_V
    ones_row = jnp.where(lax.broadcasted_iota(jnp.int32, (pad, rows), 0) == 0, 1.0, 0.0).astype(BF16)
    for hh in range(vt_ref.shape[0]):
        vt_ref[hh, 0:HEAD_V, :] = vt[hh * HEAD_V:(hh + 1) * HEAD_V, :].astype(BF16)
        vt_ref[hh, HEAD_V:VT_ROWS, :] = ones_row


class _Mod:
    def __init__(self, table, row0, shared):
        self.table, self.row0, self.shared = table, row0, shared

    def spec(self, batch_of):
        row0 = self.row0
        if self.shared:
            return pl.BlockSpec((1, 1, 6 * D_MODEL), lambda *g: (row0, 0, 0))
        return pl.BlockSpec((1, 1, 6 * D_MODEL), lambda *g: (row0 + batch_of(*g), 0, 0))


class _LayerWeights:
    def __init__(self, stacked, layer):
        self.stacked, self.layer = stacked, layer

    def __getitem__(self, name):
        return self.stacked[name]

    def spec(self, name):
        layer = self.layer
        _, rows, cols = self.stacked[name].shape
        return pl.BlockSpec((None, rows, cols), lambda *_: (layer, 0, 0), pipeline_mode=pl.Buffered(1))


def _ada_kernel(cond_ref, w_ref, b_ref, out_ref):
    c = cond_ref[...]
    s = c * jax.nn.sigmoid(c)
    out_ref[0] = _dot(s.astype(BF16), w_ref[0].astype(BF16)) + b_ref[0]


def _ada(cond_all, w_ada, b_ada):
    rows = cond_all.shape[0]
    tn = 1536
    return pl.pallas_call(
        _ada_kernel,
        grid=(DEPTH, 6 * D_MODEL // tn),
        in_specs=[
            pl.BlockSpec((rows, D_MODEL), lambda l, j: (0, 0)),
            pl.BlockSpec((1, D_MODEL, tn), lambda l, j: (l, 0, j)),
            pl.BlockSpec((1, 1, tn), lambda l, j: (l, 0, j)),
        ],
        out_specs=pl.BlockSpec((1, rows, tn), lambda l, j: (l, 0, j)),
        out_shape=jax.ShapeDtypeStruct((DEPTH, rows, 6 * D_MODEL), F32),
        compiler_params=_params("arbitrary", "arbitrary"),
        name="ada_mod",
    )(cond_all, w_ada, b_ada.reshape(DEPTH, 1, 6 * D_MODEL))


def _inproj_kernel(x_ref, mod_ref, g1_ref, win_ref, gq_ref, gkv_ref, wq_ref, wqr_ref, wk_ref, wv_ref,
                   cosq_ref, sinq_ref, cosk_ref, sink_ref, cosd_ref, sina_ref, sinb_ref,
                   q_out, k_out, vt_out, lru_out, pool_out, dq_out, dk_out, dvt_out, *cache_outs):
    x = x_ref[...]
    mod = mod_ref[0]
    sh1 = mod[:, 0:D_MODEL]
    sc1 = mod[:, D_MODEL:2 * D_MODEL]
    h = _rms_rows(x, D_MODEL) * g1_ref[...]
    hb = (h * (1.0 + sc1) + sh1).astype(BF16)

    t01 = _dot(hb, win_ref[:, 0:256])
    lane = lax.broadcasted_iota(jnp.int32, (1, 256), 1)
    cq = jnp.where(lane < MLA_Q_RANK, t01, 0.0)
    cqn = (_rms_rows(cq, MLA_Q_RANK) * gq_ref[...]).astype(BF16)
    qa = _dot(cqn, wq_ref[...])
    qr = _dot(cqn, wqr_ref[...])
    cosq = cosq_ref[...]
    sinq = sinq_ref[...]
    ckv = _dot(hb, win_ref[:, 256:384])
    lat = _rms_rows(ckv, MLA_KV_RANK) * gkv_ref[...]
    latb = lat.astype(BF16)
    kk = _dot(latb, wk_ref[...])
    _store_vt(vt_out, _dot(latb, wv_ref[...]))
    t1 = t01[:, 128:256]
    t3 = _dot(hb, win_ref[:, 384:512])
    kro = t1 * cosk_ref[...] + t3 * sink_ref[...]
    for hh in range(MLA_HEADS):
        sl = slice(hh * MLA_SLOT, (hh + 1) * MLA_SLOT)
        q_out[hh] = (qa[:, sl] * cosq + qr[:, sl] * sinq).astype(q_out.dtype)
        k_out[hh] = (kk[:, sl] + kro).astype(k_out.dtype)

    lru_out[...] = _dot(hb, win_ref[:, 512:1024])
    pool_out[...] = _dot(hb, win_ref[:, 1024:1280])

    cosd = cosd_ref[...]
    sina = sina_ref[...]
    sinb = sinb_ref[...]

    def rope(t):
        return t * cosd + pltpu.roll(t, 256 - 16, 1) * sina + pltpu.roll(t, 16, 1) * sinb

    dq = _dot(hb, win_ref[:, 1280:1536])
    dk = _dot(hb, win_ref[:, 1536:1792])
    dv = _dot(hb, win_ref[:, 1792:2048])
    dq_out[...] = (rope(dq) * (LOG2E / math.sqrt(DIFF_DIM))).astype(dq_out.dtype)
    dk_out[...] = rope(dk).astype(dk_out.dtype)
    _store_vt(dvt_out, dv)

    if cache_outs:
        lat_out, kr_out, dk_raw_out, dv_raw_out = cache_outs
        lat_out[...] = lat
        kr_out[...] = t1
        dk_raw_out[...] = dk
        dv_raw_out[...] = dv


def _inproj(x, mod, lw, tabs, *, nb, n, emit_cache):
    T = nb * n
    tm = TOKEN_TILE
    npt = n // tm
    row_blk = lambda j, b: b * npt + j

    def tok(width):
        return pl.BlockSpec((tm, width), lambda j, b: (row_blk(j, b), 0))

    def tab(width):
        return pl.BlockSpec((tm, width), lambda j, b: (j, 0))

    head = pl.BlockSpec((MLA_HEADS, tm, MLA_SLOT), lambda j, b: (0, row_blk(j, b), 0))
    vt_spec = pl.BlockSpec((MLA_HEADS, VT_ROWS, tm), lambda j, b: (0, 0, row_blk(j, b)))
    wnames = ("g1", "w_in", "gq", "gkv", "wq", "wqr", "wk", "wv")
    in_specs = [tok(D_MODEL), mod.spec(lambda j, b: b)] + [lw.spec(nm) for nm in wnames] + [
        tab(128), tab(128), tab(128), tab(128), tab(256), tab(256), tab(256)]
    out_specs = [head, head, vt_spec, tok(512), tok(256), tok(256), tok(256), vt_spec]
    vt_shape = jax.ShapeDtypeStruct((MLA_HEADS, VT_ROWS, T), BF16)
    out_shape = [
        jax.ShapeDtypeStruct((MLA_HEADS, T, MLA_SLOT), BF16),
        jax.ShapeDtypeStruct((MLA_HEADS, T, MLA_SLOT), BF16),
        vt_shape,
        jax.ShapeDtypeStruct((T, 512), F32),
        jax.ShapeDtypeStruct((T, 256), F32),
        jax.ShapeDtypeStruct((T, 256), BF16),
        jax.ShapeDtypeStruct((T, 256), BF16),
        vt_shape,
    ]
    if emit_cache:
        out_specs += [tok(128), tok(128), tok(256), tok(256)]
        out_shape += [jax.ShapeDtypeStruct((T, 128), F32), jax.ShapeDtypeStruct((T, 128), F32),
                      jax.ShapeDtypeStruct((T, 256), F32), jax.ShapeDtypeStruct((T, 256), F32)]
    return pl.pallas_call(
        _inproj_kernel,
        grid=(npt, nb),
        in_specs=in_specs,
        out_specs=out_specs,
        out_shape=out_shape,
        compiler_params=_params("arbitrary", "arbitrary"),
        name="inproj_cache" if emit_cache else "inproj",
    )(x, mod.table, *[lw[nm] for nm in wnames],
      tabs["cosq"], tabs["sinq"], tabs["cosk"], tabs["sink"], tabs["cosd"], tabs["sina"], tabs["sinb"])


def _ctx_prep_kernel(ckv_ref, kr_ref, dk_ref, dv_ref, wk_ref, wv_ref, k_out, vt_out, dk_out, dvt_out):
    latb = ckv_ref[...].astype(BF16)
    kk = _dot(latb, wk_ref[...])
    kr = kr_ref[...]
    for hh in range(MLA_HEADS):
        k_out[hh] = (kk[:, hh * MLA_SLOT:(hh + 1) * MLA_SLOT] + kr).astype(k_out.dtype)
    _store_vt(vt_out, _dot(latb, wv_ref[...]))
    dk_out[...] = dk_ref[...].astype(dk_out.dtype)
    _store_vt(dvt_out, dv_ref[...])


def _ctx_prep(ckv, kr_pad, cdk, cdv, lw, *, nb, p):
    T = nb * p
    layer = lw.layer
    cache_row = lambda w: pl.BlockSpec((p, w), lambda b: (b * DEPTH + layer, 0))
    row = lambda w: pl.BlockSpec((p, w), lambda b: (b, 0))
    vt_spec = pl.BlockSpec((MLA_HEADS, VT_ROWS, p), lambda b: (0, 0, b))
    vt_shape = jax.ShapeDtypeStruct((MLA_HEADS, VT_ROWS, T), BF16)
    return pl.pallas_call(
        _ctx_prep_kernel,
        grid=(nb,),
        in_specs=[cache_row(128), cache_row(128), cache_row(256), cache_row(256),
                  lw.spec("wk"), lw.spec("wv")],
        out_specs=[pl.BlockSpec((MLA_HEADS, p, MLA_SLOT), lambda b: (0, b, 0)), vt_spec, row(256), vt_spec],
        out_shape=[jax.ShapeDtypeStruct((MLA_HEADS, T, MLA_SLOT), BF16), vt_shape,
                   jax.ShapeDtypeStruct((T, 256), BF16), vt_shape],
        compiler_params=_params("arbitrary"),
        name="ctx_prep",
    )(ckv, kr_pad, cdk, cdv, lw["wk"], lw["wv"])


def _run_pipeline(n_maps, scores, exps, finish):
    ms = {0: scores(0)}
    for u in range(n_maps):
        if u + 1 < n_maps:
            ms[u + 1] = scores(u + 1)
        exps(u, ms.pop(u))
        if u >= 1:
            finish(u - 1)
    finish(n_maps - 1)


def _score_stage(s_buf, k_new, k_ctx, q, n_ctx):
    sn = _dot_nt(k_new, q)
    s_buf[n_ctx:, :] = sn
    m = jnp.max(sn, axis=0, keepdims=True)
    if k_ctx is not None:
        sc = _dot_nt(k_ctx, q)
        s_buf[0:n_ctx, :] = sc
        m = jnp.maximum(m, jnp.max(sc, axis=0, keepdims=True))
    return m


def _value_stage(e_buf, vt_new, vt_ctx, n_ctx):
    o = _dot(vt_new, e_buf[n_ctx:, :])
    if vt_ctx is not None:
        o = o + _dot(vt_ctx, e_buf[0:n_ctx, :])
    return o


def _att_scratch(nk):
    return [pltpu.VMEM((nk, ATT_TQ), F32), pltpu.VMEM((nk, ATT_TQ), F32),
            pltpu.VMEM((nk, ATT_TQ), BF16), pltpu.VMEM((nk, ATT_TQ), BF16),
            pltpu.VMEM((MLA_HEADS * HEAD_V, ATT_TQ), F32)]


def _att_nsub(n):
    for nsub in (4, 2):
        if n % (nsub * ATT_TQ) == 0:
            return nsub
    return 1


def _mla_attn_kernel(*refs, has_ctx, nsub):
    if has_ctx:
        q_ref, k_ref, vt_ref, kc_ref, vtc_ref, o_ref, s0, s1, e0, e1, ot = refs
        n_ctx = kc_ref.shape[1]
    else:
        q_ref, k_ref, vt_ref, o_ref, s0, s1, e0, e1, ot = refs
        n_ctx = 0
    s_bufs, e_bufs = (s0, s1), (e0, e1)

    def scores(u):
        t, hh = divmod(u, MLA_HEADS)
        q = q_ref[hh, t * ATT_TQ:(t + 1) * ATT_TQ, :]
        return _score_stage(s_bufs[u % 2], k_ref[hh], kc_ref[hh] if has_ctx else None, q, n_ctx)

    def exps(u, m):
        e_bufs[u % 2][...] = jnp.exp2(s_bufs[u % 2][...] - m).astype(BF16)

    def finish(u):
        t, hh = divmod(u, MLA_HEADS)
        o = _value_stage(e_bufs[u % 2], vt_ref[hh], vtc_ref[hh] if has_ctx else None, n_ctx)
        ot[hh * HEAD_V:(hh + 1) * HEAD_V, :] = o[0:HEAD_V, :] * (1.0 / o[HEAD_V:HEAD_V + 1, :])
        if hh == MLA_HEADS - 1:
            o_ref[t * ATT_TQ:(t + 1) * ATT_TQ, :] = ot[...].T

    _run_pipeline(nsub * MLA_HEADS, scores, exps, finish)


def _mla_attn(q, k, vt, ctx, *, nb, n):
    nsub = _att_nsub(n)
    tq = nsub * ATT_TQ
    npt = n // tq
    H, S = MLA_HEADS, MLA_SLOT
    in_specs = [
        pl.BlockSpec((H, tq, S), lambda b, j: (0, b * npt + j, 0)),
        pl.BlockSpec((H, n, S), lambda b, j: (0, b, 0)),
        pl.BlockSpec((H, VT_ROWS, n), lambda b, j: (0, 0, b)),
    ]
    args = [q, k, vt]
    n_ctx = 0
    if ctx is not None:
        n_ctx = ctx[0].shape[1] // nb
        in_specs += [
            pl.BlockSpec((H, n_ctx, S), lambda b, j: (0, b, 0)),
            pl.BlockSpec((H, VT_ROWS, n_ctx), lambda b, j: (0, 0, b)),
        ]
        args += list(ctx)
    return pl.pallas_call(
        functools.partial(_mla_attn_kernel, has_ctx=ctx is not None, nsub=nsub),
        grid=(nb, npt),
        in_specs=in_specs,
        out_specs=pl.BlockSpec((tq, 256), lambda b, j: (b * npt + j, 0)),
        out_shape=jax.ShapeDtypeStruct((nb * n, 256), F32),
        scratch_shapes=_att_scratch(n + n_ctx),
        compiler_params=_params("arbitrary", "arbitrary"),
        name="mla_attn_ctx" if ctx is not None else "mla_attn",
    )(*args)


def _diff_attn_kernel(*refs, has_ctx, nsub, lam_init):
    if has_ctx:
        lv_ref, g_ref, q_ref, k_ref, vt_ref, kc_ref, vtc_ref, o_ref, s0, s1, e0, e1, ot = refs
        n_ctx = kc_ref.shape[0]
    else:
        lv_ref, g_ref, q_ref, k_ref, vt_ref, o_ref, s0, s1, e0, e1, ot = refs
        n_ctx = 0
    s_bufs, e_bufs = (s0, s1), (e0, e1)
    lv = lv_ref[...]
    lam = (jnp.exp(jnp.sum(lv[0:1] * lv[1:2], axis=-1, keepdims=True))
           - jnp.exp(jnp.sum(lv[2:3] * lv[3:4], axis=-1, keepdims=True)) + lam_init)
    lane = lax.broadcasted_iota(jnp.int32, (1, 256), 1)
    n_pairs = 2 * DIFF_HEADS
    outs = {}

    def scores(u):
        t, p = divmod(u, n_pairs)
        q = q_ref[t * ATT_TQ:(t + 1) * ATT_TQ, :]
        in_pair = (lane >= p * DIFF_DIM) & (lane < (p + 1) * DIFF_DIM)
        qm = jnp.where(in_pair, q, jnp.zeros_like(q))
        return _score_stage(s_bufs[u % 2], k_ref[...], kc_ref[...] if has_ctx else None, qm, n_ctx)

    def exps(u, m):
        e_bufs[u % 2][...] = jnp.exp2(s_bufs[u % 2][...] - m).astype(BF16)

    def finish(u):
        t, p = divmod(u, n_pairs)
        hh = p // 2
        outs[u] = _value_stage(e_bufs[u % 2], vt_ref[hh], vtc_ref[hh] if has_ctx else None, n_ctx)
        if p % 2 == 1:
            o0, o1 = outs.pop(u - 1), outs.pop(u)
            w0 = 1.0 / o0[HEAD_V:HEAD_V + 1, :]
            w1 = lam / o1[HEAD_V:HEAD_V + 1, :]
            o = o0[0:HEAD_V, :] * w0 - o1[0:HEAD_V, :] * w1
            msq = jnp.sum(o * o, axis=0, keepdims=True) * (1.0 / HEAD_V)
            ot[hh * HEAD_V:(hh + 1) * HEAD_V, :] = o * lax.rsqrt(msq + EPS)
        if p == n_pairs - 1:
            o_ref[t * ATT_TQ:(t + 1) * ATT_TQ, :] = (ot[...].T * g_ref[...]) * (1.0 - lam_init)

    _run_pipeline(nsub * n_pairs, scores, exps, finish)


def _diff_attn(q, k, vt, ctx, lw, *, nb, n, lam_init):
    nsub = _att_nsub(n)
    tq = nsub * ATT_TQ
    npt = n // tq
    in_specs = [
        lw.spec("diff_lambda"),
        lw.spec("diff_g"),
        pl.BlockSpec((tq, 256), lambda b, j: (b * npt + j, 0)),
        pl.BlockSpec((n, 256), lambda b, j: (b, 0)),
        pl.BlockSpec((DIFF_HEADS, VT_ROWS, n), lambda b, j: (0, 0, b)),
    ]
    args = [lw["diff_lambda"], lw["diff_g"], q, k, vt]
    n_ctx = 0
    if ctx is not None:
        n_ctx = ctx[0].shape[0] // nb
        in_specs += [pl.BlockSpec((n_ctx, 256), lambda b, j: (b, 0)),
                     pl.BlockSpec((DIFF_HEADS, VT_ROWS, n_ctx), lambda b, j: (0, 0, b))]
        args += list(ctx)
    return pl.pallas_call(
        functools.partial(_diff_attn_kernel, has_ctx=ctx is not None, nsub=nsub, lam_init=lam_init),
        grid=(nb, npt),
        in_specs=in_specs,
        out_specs=pl.BlockSpec((tq, 256), lambda b, j: (b * npt + j, 0)),
        out_shape=jax.ShapeDtypeStruct((nb * n, 256), F32),
        scratch_shapes=_att_scratch(n + n_ctx),
        compiler_params=_params("arbitrary", "arbitrary"),
        name="diff_attn_ctx" if ctx is not None else "diff_attn",
    )(*args)


def _shift_rows(v, k):
    return pltpu.roll(v, (-k) % v.shape[0], 0)


def _scan_chunk(a, b, reverse):
    T = a.shape[0]
    row = lax.broadcasted_iota(jnp.int32, a.shape, 0)
    s = 1
    while s < T:
        if reverse:
            valid = row < T - s
            ap, bp = _shift_rows(a, s), _shift_rows(b, s)
        else:
            valid = row >= s
            ap, bp = _shift_rows(a, -s), _shift_rows(b, -s)
        b = jnp.where(valid, a * bp + b, b)
        a = jnp.where(valid, a * ap, a)
        s *= 2
    return a, b


def _sigmoid(x):
    return 0.5 * jnp.tanh(0.5 * x) + 0.5


def _gelu_tanh(x):
    return x * (0.5 * (1.0 + jnp.tanh(math.sqrt(2.0 / math.pi) * (x + 0.044715 * (x * x * x)))))


def _lru_kernel(u_ref, h0_ref, cw_ref, cb_ref, wg_ref, bg_ref, lam_ref, y_ref, st_ref,
                xpad, a1s, b1s, *, N, T):
    W = LRU_WIDTH
    nc = N // T
    zeros = jnp.zeros((HALO, W), F32)
    xpad[0:HALO, :] = zeros
    xpad[N + HALO:N + 2 * HALO, :] = zeros

    def fill(j, carry):
        r0 = pl.multiple_of(j * T, T)
        xpad[pl.ds(r0 + HALO, T), :] = u_ref[pl.ds(r0, T), 0:W]
        return carry

    lax.fori_loop(0, nc, fill, 0)

    z = -lam_ref[...]
    sp = jnp.maximum(z, 0.0) + jnp.log1p(jnp.exp(-jnp.abs(z)))
    cw = cw_ref[...]
    cb = cb_ref[...]
    bg = bg_ref[...]

    def fwd(j, carry):
        r0 = pl.multiple_of(j * T, T)
        ext = xpad[pl.ds(r0, T + 2 * HALO), :]
        body = slice(HALO, HALO + T)
        xc = cb
        for tap in range(4):
            xc = xc + _shift_rows(ext, tap - 1)[body] * cw[tap:tap + 1]
        g = _sigmoid(_dot(xc.astype(BF16), wg_ref[...]) + bg)
        ab = []
        for d in range(2):
            r = g[:, d * W:(d + 1) * W]
            i = g[:, (2 + d) * W:(3 + d) * W]
            log_a = (-LRU_C * r) * sp[d:d + 1]
            a = jnp.exp(log_a)
            bt = (jnp.sqrt(1.0 - a * a) * i) * xc
            ab.append((a, bt))
        a1s[pl.ds(r0, T), :] = ab[1][0]
        b1s[pl.ds(r0, T), :] = ab[1][1]
        A, Bv = _scan_chunk(ab[0][0], ab[0][1], reverse=False)
        h = A * carry + Bv
        y_ref[pl.ds(r0, T), :] = h
        return h[T - 1:T, :]

    cf = lax.fori_loop(0, nc, fwd, h0_ref[0, 0:1, :])

    def bwd(jj, carry):
        r0 = pl.multiple_of((nc - 1 - jj) * T, T)
        A, Bv = _scan_chunk(a1s[pl.ds(r0, T), :], b1s[pl.ds(r0, T), :], reverse=True)
        h = A * carry + Bv
        gb = u_ref[pl.ds(r0, T), W:2 * W]
        y_ref[pl.ds(r0, T), :] = (y_ref[pl.ds(r0, T), :] + h) * _gelu_tanh(gb)
        return h[0:1, :]

    cbw = lax.fori_loop(0, nc, bwd, h0_ref[0, 1:2, :])
    st_ref[0, 0:1, :] = cf
    st_ref[0, 1:2, :] = cbw


def _lru(u, h0, h0_block, lw, *, nb, n):
    T = min(n, 256)
    W = LRU_WIDTH
    return pl.pallas_call(
        functools.partial(_lru_kernel, N=n, T=T),
        grid=(nb,),
        in_specs=[
            pl.BlockSpec((n, 2 * W), lambda b: (b, 0)),
            pl.BlockSpec((1, 2, W), lambda b: (h0_block(b), 0, 0)),
            lw.spec("conv_w"), lw.spec("conv_b"), lw.spec("w_gate"), lw.spec("b_gate"),
            lw.spec("lru_lambda"),
        ],
        out_specs=[
            pl.BlockSpec((n, W), lambda b: (b, 0)),
            pl.BlockSpec((1, 2, W), lambda b: (b, 0, 0)),
        ],
        out_shape=[
            jax.ShapeDtypeStruct((nb * n, W), F32),
            jax.ShapeDtypeStruct((nb, 2, W), F32),
        ],
        scratch_shapes=[
            pltpu.VMEM((n + 2 * HALO, W), F32),
            pltpu.VMEM((n, W), F32),
            pltpu.VMEM((n, W), F32),
        ],
        compiler_params=_params("arbitrary"),
        name="rglru",
    )(u, h0, lw["conv_w"], lw["conv_b"], lw["w_gate"], lw["b_gate"], lw["lru_lambda"])


def _pool_kernel(u_ref, wp_ref, sc_ref, y_ref, xpad, *, N, T):
    W = GROUP_WIDTH
    nc = N // T
    zeros = jnp.zeros((HALO, W), F32)
    xpad[0:HALO, :] = zeros
    xpad[N + HALO:N + 2 * HALO, :] = zeros

    def fill(j, carry):
        r0 = pl.multiple_of(j * T, T)
        xpad[pl.ds(r0 + HALO, T), :] = u_ref[pl.ds(r0, T), :]
        return carry

    lax.fori_loop(0, nc, fill, 0)

    grp = lax.broadcasted_iota(jnp.int32, (1, W), 1) // POOL_CH
    half = jnp.where(grp == 0, 1, jnp.where(grp == 1, 2, jnp.where(grp == 2, 4, 8)))
    scale = sc_ref[...]

    def chunk(j, carry):
        r0 = pl.multiple_of(j * T, T)
        ext = xpad[pl.ds(r0, T + 2 * HALO), :]
        w2 = _shift_rows(ext, -1) + ext
        w4 = _shift_rows(w2, -1) + _shift_rows(w2, 1)
        w8 = _shift_rows(w4, -2) + _shift_rows(w4, 2)
        w16 = _shift_rows(w8, -4) + _shift_rows(w8, 4)
        ws = jnp.where(grp == 0, w2, jnp.where(grp == 1, w4, jnp.where(grp == 2, w8, w16)))
        body = slice(HALO, HALO + T)
        t = r0 + lax.broadcasted_iota(jnp.int32, (T, W), 0)
        cnt = (jnp.minimum(t + half, N) - jnp.maximum(t - half, 0)).astype(F32)
        d = ws[body] / cnt - ext[body]
        y_ref[pl.ds(r0, T), :] = _dot(d.astype(BF16), wp_ref[...]) * scale
        return carry

    lax.fori_loop(0, nc, chunk, 0)


def _pool(u, lw, *, nb, n):
    W = GROUP_WIDTH
    T = min(n, 256)
    return pl.pallas_call(
        functools.partial(_pool_kernel, N=n, T=T),
        grid=(nb,),
        in_specs=[pl.BlockSpec((n, W), lambda b: (b, 0)), lw.spec("w_pool"), lw.spec("pool_scale")],
        out_specs=pl.BlockSpec((n, W), lambda b: (b, 0)),
        out_shape=jax.ShapeDtypeStruct((nb * n, W), F32),
        scratch_shapes=[pltpu.VMEM((n + 2 * HALO, W), F32)],
        compiler_params=_params("arbitrary"),
        name="pool_mixer",
    )(u, lw["w_pool"], lw["pool_scale"])


def _mix_ffn_kernel(*refs, final):
    if final:
        (x_ref, ya_ref, yb_ref, yc_ref, yd_ref, mod_ref, g2_ref, wo_ref, wg_ref, wu_ref, wd_ref,
         gf_ref, o_ref) = refs
    else:
        (x_ref, ya_ref, yb_ref, yc_ref, yd_ref, mod_ref, g2_ref, wo_ref, wg_ref, wu_ref, wd_ref,
         o_ref) = refs
    mod = mod_ref[0]
    gate1 = mod[:, 2 * D_MODEL:3 * D_MODEL]
    sh2 = mod[:, 3 * D_MODEL:4 * D_MODEL]
    sc2 = mod[:, 4 * D_MODEL:5 * D_MODEL]
    gate2 = mod[:, 5 * D_MODEL:6 * D_MODEL]
    mix = None
    for i, y_ref in enumerate((ya_ref, yb_ref, yc_ref, yd_ref)):
        part = _dot(y_ref[...].astype(BF16), wo_ref[i * GROUP_WIDTH:(i + 1) * GROUP_WIDTH, :])
        mix = part if mix is None else mix + part
    x1 = x_ref[...] + gate1 * mix
    h = _rms_rows(x1, D_MODEL) * g2_ref[...]
    hb = (h * (1.0 + sc2) + sh2).astype(BF16)
    ff = None
    for lo, hi in FF_CHUNKS:
        g = _dot(hb, wg_ref[:, lo:hi])
        up = _dot(hb, wu_ref[:, lo:hi])
        act = ((g * jax.nn.sigmoid(g)) * up).astype(BF16)
        part = _dot(act, wd_ref[lo:hi, :])
        ff = part if ff is None else ff + part
    x2 = x1 + gate2 * ff
    if final:
        x2 = _rms_rows(x2, D_MODEL) * gf_ref[...]
    o_ref[...] = x2


def _mix_ffn(x, ys, mod, lw, gf, *, nb, n, final):
    T = nb * n
    tm = TOKEN_TILE
    npt = n // tm

    def tok(width):
        return pl.BlockSpec((tm, width), lambda i: (i, 0))

    wnames = ("g2", "w_out", "w_gate_ff", "w_up_ff", "w_down")
    in_specs = [tok(D_MODEL), tok(256), tok(256), tok(256), tok(256),
                mod.spec(lambda i: i // npt)] + [lw.spec(nm) for nm in wnames]
    args = [x, *ys, mod.table] + [lw[nm] for nm in wnames]
    if final:
        in_specs.append(_resident((1, D_MODEL)))
        args.append(gf)
    return pl.pallas_call(
        functools.partial(_mix_ffn_kernel, final=final),
        grid=(T // tm,),
        in_specs=in_specs,
        out_specs=tok(D_MODEL),
        out_shape=jax.ShapeDtypeStruct((T, D_MODEL), F32),
        compiler_params=_params("arbitrary"),
        name="mix_ffn_final" if final else "mix_ffn",
    )(*args)


def _block_diag(w):
    L, G, c, e = w.shape
    return jnp.einsum('lgce,gh->lgche', w, jnp.eye(G, dtype=w.dtype)).reshape(L, G * c, G * e)


def _rot_cols(w):
    return jnp.concatenate([-w[..., 16:32], w[..., 0:16]], axis=-1)


def _stack_weights(p):
    w_in = p["w_in"]
    o1 = MLA_Q_RANK
    o2 = o1 + MLA_KV_RANK
    o3 = o2 + MLA_ROPE
    c_q, c_kv, k_r, rest = w_in[..., :o1], w_in[..., o1:o2], w_in[..., o2:o3], w_in[..., o3:]
    z = lambda n: jnp.zeros((DEPTH, D_MODEL, n), F32)
    w_in_eff = jnp.concatenate([c_q, k_r, z(32), c_kv, z(64), _rot_cols(k_r), z(32), rest], axis=-1)

    w_uq = p["mla_w_uq"]
    qd = MLA_NOPE + MLA_ROPE
    wq_parts, wqr_parts = [], []
    zq = lambda n: jnp.zeros((DEPTH, MLA_Q_RANK, n), F32)
    for h in range(MLA_HEADS):
        wh = w_uq[..., h * qd:(h + 1) * qd]
        wq_parts += [wh, zq(MLA_SLOT - qd)]
        wqr_parts += [zq(MLA_NOPE), _rot_cols(wh[..., MLA_NOPE:]), zq(MLA_SLOT - qd)]
    pad_rows = lambda w: jnp.pad(w, ((0, 0), (0, 256 - MLA_Q_RANK), (0, 0)))
    w_ukv = p["mla_w_ukv"]
    wk_parts, wv_parts = [], []
    zk = jnp.zeros((DEPTH, MLA_KV_RANK, MLA_SLOT - MLA_NOPE), F32)
    for h in range(MLA_HEADS):
        base = h * (MLA_NOPE + MLA_V)
        wk_parts += [w_ukv[..., base:base + MLA_NOPE], zk]
        wv_parts.append(w_ukv[..., base + MLA_NOPE:base + MLA_NOPE + MLA_V])

    w_r, w_i, b_r, b_i = p["lru_w_r"], p["lru_w_i"], p["lru_b_r"], p["lru_b_i"]
    w_gate = jnp.concatenate([_block_diag(w_r[:, 0]), _block_diag(w_r[:, 1]),
                              _block_diag(w_i[:, 0]), _block_diag(w_i[:, 1])], axis=-1)
    b_gate = jnp.concatenate([b_r[:, 0], b_r[:, 1], b_i[:, 0], b_i[:, 1]], axis=-1)
    w_gu = p["w_gu"]
    row = lambda v: v[:, None, :]
    return {
        "g1": row(p["norm1_g"]),
        "g2": row(p["norm2_g"]),
        "w_in": w_in_eff.astype(BF16),
        "gq": row(jnp.pad(p["mla_q_norm_g"], ((0, 0), (0, 256 - MLA_Q_RANK)))),
        "gkv": row(p["mla_kv_norm_g"]),
        "wq": pad_rows(jnp.concatenate(wq_parts, axis=-1)).astype(BF16),
        "wqr": pad_rows(jnp.concatenate(wqr_parts, axis=-1)).astype(BF16),
        "wk": jnp.concatenate(wk_parts, axis=-1).astype(BF16),
        "wv": jnp.concatenate(wv_parts, axis=-1).astype(BF16),
        "conv_w": p["lru_conv_w"],
        "conv_b": row(p["lru_conv_b"]),
        "w_gate": w_gate.astype(BF16),
        "b_gate": row(b_gate),
        "lru_lambda": p["lru_lambda"],
        "w_pool": _block_diag(p["pool_w"]).astype(BF16),
        "pool_scale": row(p["pool_scale"]),
        "diff_lambda": p["diff_lambda"],
        "diff_g": row(jnp.tile(p["diff_norm_g"], (1, DIFF_HEADS))),
        "w_out": p["w_out"].astype(BF16),
        "w_gate_ff": w_gu[..., :FF_HIDDEN].astype(BF16),
        "w_up_ff": w_gu[..., FF_HIDDEN:].astype(BF16),
        "w_down": p["w_down"].astype(BF16),
    }


def _rope_tables(n, positional):
    quarter = MLA_ROPE // 4
    if positional:
        t = jnp.arange(n)
        row = (t // GRID_W).astype(F32)
        col = (t % GRID_W).astype(F32)
        inv = ROPE_BASE ** (-jnp.arange(quarter, dtype=F32) / quarter)
        ang = jnp.concatenate([row[:, None] * inv, col[:, None] * inv], axis=-1)
        cos, sin = jnp.cos(ang), jnp.sin(ang)
    else:
        cos, sin = jnp.ones((n, 16), F32), jnp.zeros((n, 16), F32)
    one = lambda w: jnp.ones((n, w), F32)
    zero = lambda w: jnp.zeros((n, w), F32)
    scale = LOG2E / math.sqrt(MLA_NOPE + MLA_ROPE)
    return {
        "cosq": jnp.concatenate([one(64), cos, cos, one(32)], axis=1) * scale,
        "sinq": jnp.concatenate([zero(64), sin, sin, zero(32)], axis=1) * scale,
        "cosk": jnp.concatenate([zero(64), cos, cos, zero(32)], axis=1),
        "sink": jnp.concatenate([zero(64), sin, sin, zero(32)], axis=1),
        "cosd": jnp.tile(jnp.concatenate([cos, cos], axis=1), (1, 8)),
        "sina": jnp.tile(jnp.concatenate([-sin, zero(16)], axis=1), (1, 8)),
        "sinb": jnp.tile(jnp.concatenate([zero(16), sin], axis=1), (1, 8)),
    }


def _layer(x, mod, lw, tabs, layer_idx, ctx, gf, *, nb, n, final):
    emit_cache = ctx is None
    tok_nb, tok_n = (1, nb * n) if mod.shared else (nb, n)
    outs = _inproj(x, mod, lw, tabs, nb=tok_nb, n=tok_n, emit_cache=emit_cache)
    q, k, vt, u_lru, u_pool, dq, dk, dvt = outs[:8]
    lam_init = 0.8 - 0.6 * math.exp(-0.3 * layer_idx)
    if ctx is None:
        h0 = jnp.zeros((1, 2, LRU_WIDTH), F32)
        h0_block = lambda b: 0
        mla_ctx = diff_ctx = None
    else:
        ckv, kr_pad, cdk, cdv, h0 = ctx
        p = ckv.shape[0] // (nb * DEPTH)
        h0_block = lambda b: b * DEPTH + layer_idx
        kc, vtc, dkc, dvtc = _ctx_prep(ckv, kr_pad, cdk, cdv, lw, nb=nb, p=p)
        mla_ctx = (kc, vtc)
        diff_ctx = (dkc, dvtc)
    y_mla = _mla_attn(q, k, vt, mla_ctx, nb=nb, n=n)
    y_lru, st = _lru(u_lru, h0, h0_block, lw, nb=nb, n=n)
    y_pool = _pool(u_pool, lw, nb=nb, n=n)
    y_diff = _diff_attn(dq, dk, dvt, diff_ctx, lw, nb=nb, n=n, lam_init=lam_init)
    x2 = _mix_ffn(x, (y_mla, y_lru, y_pool, y_diff), mod, lw, gf, nb=tok_nb, n=tok_n, final=final)
    cache = (outs[8], outs[9][:, 64:96], outs[10], outs[11], st) if emit_cache else None
    return x2, cache


def kernel(x_prompt, x_sample, cache_mla_ckv, cache_mla_krope, cache_diff_k, cache_diff_v, state_lru,
           c, c_ctx, w_ada, b_ada, norm1_g, norm2_g, w_in, mla_q_norm_g, mla_w_uq, mla_kv_norm_g,
           mla_w_ukv, lru_conv_w, lru_conv_b, lru_w_r, lru_b_r, lru_w_i, lru_b_i, lru_lambda, pool_w,
           pool_scale, diff_lambda, diff_norm_g, w_out, w_gu, w_down, final_norm_g):
    p = {
        "norm1_g": norm1_g, "norm2_g": norm2_g, "w_in": w_in, "mla_q_norm_g": mla_q_norm_g,
        "mla_w_uq": mla_w_uq, "mla_kv_norm_g": mla_kv_norm_g, "mla_w_ukv": mla_w_ukv,
        "lru_conv_w": lru_conv_w, "lru_conv_b": lru_conv_b, "lru_w_r": lru_w_r, "lru_b_r": lru_b_r,
        "lru_w_i": lru_w_i, "lru_b_i": lru_b_i, "lru_lambda": lru_lambda, "pool_w": pool_w,
        "pool_scale": pool_scale, "diff_lambda": diff_lambda, "diff_norm_g": diff_norm_g,
        "w_out": w_out, "w_gu": w_gu, "w_down": w_down,
    }
    Bp, Np, _ = x_prompt.shape
    Bs, Ns, _ = x_sample.shape
    P = cache_mla_ckv.shape[2]

    cond_all = jnp.concatenate([c, c_ctx[None, :], jnp.zeros((MOD_ROWS - Bs - 1, D_MODEL), F32)], axis=0)
    mod_table = _ada(cond_all, w_ada, b_ada).reshape(DEPTH * MOD_ROWS, 1, 6 * D_MODEL)
    tabs_p = _rope_tables(Bp * Np, positional=False)
    tabs_s = _rope_tables(Ns, positional=True)
    kr_pad = jnp.pad(cache_mla_krope, ((0, 0), (0, 0), (0, 0), (MLA_NOPE, MLA_SLOT - MLA_NOPE - MLA_ROPE)))
    flat = lambda a, w: a.reshape(Bs * DEPTH * P, w)
    ctx = (flat(cache_mla_ckv, MLA_KV_RANK), flat(kr_pad, MLA_SLOT), flat(cache_diff_k, 256),
           flat(cache_diff_v, 256), state_lru.reshape(Bs * DEPTH, 2, LRU_WIDTH))
    gf = final_norm_g[None, :]
    stacked = _stack_weights(p)

    xp = x_prompt.reshape(Bp * Np, D_MODEL)
    xs = x_sample.reshape(Bs * Ns, D_MODEL)
    caches = []
    for l in range(DEPTH):
        lw = _LayerWeights(stacked, l)
        final = l == DEPTH - 1
        mod_p = _Mod(mod_table, l * MOD_ROWS + Bs, shared=True)
        mod_s = _Mod(mod_table, l * MOD_ROWS, shared=False)
        xp, cache = _layer(xp, mod_p, lw, tabs_p, l, None, gf, nb=Bp, n=Np, final=final)
        caches.append(cache)
        xs, _ = _layer(xs, mod_s, lw, tabs_s, l, ctx, gf, nb=Bs, n=Ns, final=final)

    stack = lambda i, w: jnp.stack([cc[i].reshape(Bp, Np, w) for cc in caches], axis=1)
    new_mla_ckv = stack(0, MLA_KV_RANK)
    new_mla_krope = stack(1, MLA_ROPE)
    new_diff_k = stack(2, 256).reshape(Bp, DEPTH, Np, DIFF_HEADS, 2, DIFF_DIM)
    new_diff_v = stack(3, 256).reshape(Bp, DEPTH, Np, DIFF_HEADS, 2 * DIFF_DIM)
    new_state_lru = jnp.stack([cc[4] for cc in caches], axis=1)
    return (xp.reshape(Bp, Np, D_MODEL), xs.reshape(Bs, Ns, D_MODEL),
            new_mla_ckv, new_mla_krope, new_diff_k, new_diff_v, new_state_lru)
```

```python
import functools
import math

import jax
import jax.numpy as jnp
from jax import lax
from jax.experimental import pallas as pl
from jax.experimental.pallas import tpu as pltpu

F32 = jnp.float32
BF16 = jnp.bfloat16

D_MODEL = 1024
DEPTH = 2
GRID_W = 64
GROUP_WIDTH = 256
MLA_HEADS = 4
MLA_NOPE = 64
MLA_ROPE = 32
MLA_V = 64
MLA_Q_RANK = 192
MLA_KV_RANK = 128
MLA_SLOT = 128
LRU_WIDTH = 256
LRU_C = 8.0
POOL_WINDOWS = (2, 4, 8, 16)
POOL_CH = 64
DIFF_HEADS = 4
DIFF_DIM = 32
HEAD_V = 64
FF_HIDDEN = 2816
FF_CHUNKS = ((0, 1536), (1536, 2816))
ROPE_BASE = 10000.0
EPS = 1e-6
IN_EFF = 2048
HALO = 8
VT_ROWS = 80
ATT_TQ = 256
TOKEN_TILE = 512
MOD_ROWS = 16
LOG2E = math.log2(math.e)

VMEM_LIMIT_BYTES = 56 * 1024 * 1024

_NT = (((1,), (1,)), ((), ()))


def _params(*sem):
    return pltpu.CompilerParams(dimension_semantics=sem, vmem_limit_bytes=VMEM_LIMIT_BYTES)


def _resident(shape):
    zeros = (0,) * len(shape)
    return pl.BlockSpec(shape, lambda *_: zeros, pipeline_mode=pl.Buffered(1))


def _dot(a, b):
    return jnp.dot(a, b, preferred_element_type=F32)


def _dot_nt(a, b):
    return lax.dot_general(a, b, _NT, preferred_element_type=F32)


def _rms_rows(x, width):
    ms = jnp.sum(x * x, axis=-1, keepdims=True) * (1.0 / width)
    return x * lax.rsqrt(ms + EPS)


def _store_vt(vt_ref, v):
    vt = v.T
    rows = v.shape[0]
    pad = VT_ROWS - HEAD_V
    ones_row = jnp.where(lax.broadcasted_iota(jnp.int32, (pad, rows), 0) == 0, 1.0, 0.0).astype(BF16)
    for hh in range(vt_ref.shape[0]):
        vt_ref[hh, 0:HEAD_V, :] = vt[hh * HEAD_V:(hh + 1) * HEAD_V, :].astype(BF16)
        vt_ref[hh, HEAD_V:VT_ROWS, :] = ones_row


class _Mod:
    def __init__(self, table, row0, shared):
        self.table, self.row0, self.shared = table, row0, shared

    def spec(self, batch_of):
        row0 = self.row0
        if self.shared:
            return pl.BlockSpec((1, 1, 6 * D_MODEL), lambda *g: (row0, 0, 0))
        return pl.BlockSpec((1, 1, 6 * D_MODEL), lambda *g: (row0 + batch_of(*g), 0, 0))


class _LayerWeights:
    def __init__(self, stacked, layer):
        self.stacked, self.layer = stacked, layer

    def __getitem__(self, name):
        return self.stacked[name]

    def spec(self, name):
        layer = self.layer
        _, rows, cols = self.stacked[name].shape
        return pl.BlockSpec((None, rows, cols), lambda *_: (layer, 0, 0), pipeline_mode=pl.Buffered(1))


def _ada_kernel(cond_ref, w_ref, b_ref, out_ref):
    c = cond_ref[...]
    s = c * jax.nn.sigmoid(c)
    out_ref[0] = _dot(s.astype(BF16), w_ref[0].astype(BF16)) + b_ref[0]


def _ada(cond_all, w_ada, b_ada):
    rows = cond_all.shape[0]
    tn = 1536
    return pl.pallas_call(
        _ada_kernel,
        grid=(DEPTH, 6 * D_MODEL // tn),
        in_specs=[
            pl.BlockSpec((rows, D_MODEL), lambda l, j: (0, 0)),
            pl.BlockSpec((1, D_MODEL, tn), lambda l, j: (l, 0, j)),
            pl.BlockSpec((1, 1, tn), lambda l, j: (l, 0, j)),
        ],
        out_specs=pl.BlockSpec((1, rows, tn), lambda l, j: (l, 0, j)),
        out_shape=jax.ShapeDtypeStruct((DEPTH, rows, 6 * D_MODEL), F32),
        compiler_params=_params("arbitrary", "arbitrary"),
        name="ada_mod",
    )(cond_all, w_ada, b_ada.reshape(DEPTH, 1, 6 * D_MODEL))


def _inproj_kernel(x_ref, mod_ref, g1_ref, win_ref, gq_ref, gkv_ref, wq_ref, wqr_ref, wk_ref, wv_ref,
                   cosq_ref, sinq_ref, cosk_ref, sink_ref, cosd_ref, sina_ref, sinb_ref,
                   q_out, k_out, vt_out, lru_out, pool_out, dq_out, dk_out, dvt_out, *cache_outs):
    x = x_ref[...]
    mod = mod_ref[0]
    sh1 = mod[:, 0:D_MODEL]
    sc1 = mod[:, D_MODEL:2 * D_MODEL]
    h = _rms_rows(x, D_MODEL) * g1_ref[...]
    hb = (h * (1.0 + sc1) + sh1).astype(BF16)

    t01 = _dot(hb, win_ref[:, 0:256])
    lane = lax.broadcasted_iota(jnp.int32, (1, 256), 1)
    cq = jnp.where(lane < MLA_Q_RANK, t01, 0.0)
    cqn = (_rms_rows(cq, MLA_Q_RANK) * gq_ref[...]).astype(BF16)
    qa = _dot(cqn, wq_ref[...])
    qr = _dot(cqn, wqr_ref[...])
    cosq = cosq_ref[...]
    sinq = sinq_ref[...]
    ckv = _dot(hb, win_ref[:, 256:384])
    lat = _rms_rows(ckv, MLA_KV_RANK) * gkv_ref[...]
    latb = lat.astype(BF16)
    kk = _dot(latb, wk_ref[...])
    _store_vt(vt_out, _dot(latb, wv_ref[...]))
    t1 = t01[:, 128:256]
    t3 = _dot(hb, win_ref[:, 384:512])
    kro = t1 * cosk_ref[...] + t3 * sink_ref[...]
    for hh in range(MLA_HEADS):
        sl = slice(hh * MLA_SLOT, (hh + 1) * MLA_SLOT)
        q_out[hh] = (qa[:, sl] * cosq + qr[:, sl] * sinq).astype(q_out.dtype)
        k_out[hh] = (kk[:, sl] + kro).astype(k_out.dtype)

    lru_out[...] = _dot(hb, win_ref[:, 512:1024])
    pool_out[...] = _dot(hb, win_ref[:, 1024:1280])

    cosd = cosd_ref[...]
    sina = sina_ref[...]
    sinb = sinb_ref[...]

    def rope(t):
        return t * cosd + pltpu.roll(t, 256 - 16, 1) * sina + pltpu.roll(t, 16, 1) * sinb

    dq = _dot(hb, win_ref[:, 1280:1536])
    dk = _dot(hb, win_ref[:, 1536:1792])
    dv = _dot(hb, win_ref[:, 1792:2048])
    dq_out[...] = (rope(dq) * (LOG2E / math.sqrt(DIFF_DIM))).astype(dq_out.dtype)
    dk_out[...] = rope(dk).astype(dk_out.dtype)
    _store_vt(dvt_out, dv)

    if cache_outs:
        lat_out, kr_out, dk_raw_out, dv_raw_out = cache_outs
        lat_out[...] = lat
        kr_out[...] = t1
        dk_raw_out[...] = dk
        dv_raw_out[...] = dv


def _inproj(x, mod, lw, tabs, *, nb, n, emit_cache):
    T = nb * n
    tm = TOKEN_TILE
    npt = n // tm
    row_blk = lambda j, b: b * npt + j

    def tok(width):
        return pl.BlockSpec((tm, width), lambda j, b: (row_blk(j, b), 0))

    def tab(width):
        return pl.BlockSpec((tm, width), lambda j, b: (j, 0))

    head = pl.BlockSpec((MLA_HEADS, tm, MLA_SLOT), lambda j, b: (0, row_blk(j, b), 0))
    vt_spec = pl.BlockSpec((MLA_HEADS, VT_ROWS, tm), lambda j, b: (0, 0, row_blk(j, b)))
    wnames = ("g1", "w_in", "gq", "gkv", "wq", "wqr", "wk", "wv")
    in_specs = [tok(D_MODEL), mod.spec(lambda j, b: b)] + [lw.spec(nm) for nm in wnames] + [
        tab(128), tab(128), tab(128), tab(128), tab(256), tab(256), tab(256)]
    out_specs = [head, head, vt_spec, tok(512), tok(256), tok(256), tok(256), vt_spec]
    vt_shape = jax.ShapeDtypeStruct((MLA_HEADS, VT_ROWS, T), BF16)
    out_shape = [
        jax.ShapeDtypeStruct((MLA_HEADS, T, MLA_SLOT), BF16),
        jax.ShapeDtypeStruct((MLA_HEADS, T, MLA_SLOT), BF16),
        vt_shape,
        jax.ShapeDtypeStruct((T, 512), F32),
        jax.ShapeDtypeStruct((T, 256), F32),
        jax.ShapeDtypeStruct((T, 256), BF16),
        jax.ShapeDtypeStruct((T, 256), BF16),
        vt_shape,
    ]
    if emit_cache:
        out_specs += [tok(128), tok(128), tok(256), tok(256)]
        out_shape += [jax.ShapeDtypeStruct((T, 128), F32), jax.ShapeDtypeStruct((T, 128), F32),
                      jax.ShapeDtypeStruct((T, 256), F32), jax.ShapeDtypeStruct((T, 256), F32)]
    return pl.pallas_call(
        _inproj_kernel,
        grid=(npt, nb),
        in_specs=in_specs,
        out_specs=out_specs,
        out_shape=out_shape,
        compiler_params=_params("arbitrary", "arbitrary"),
        name="inproj_cache" if emit_cache else "inproj",
    )(x, mod.table, *[lw[nm] for nm in wnames],
      tabs["cosq"], tabs["sinq"], tabs["cosk"], tabs["sink"], tabs["cosd"], tabs["sina"], tabs["sinb"])


def _ctx_prep_kernel(ckv_ref, kr_ref, dk_ref, dv_ref, wk_ref, wv_ref, k_out, vt_out, dk_out, dvt_out):
    latb = ckv_ref[...].astype(BF16)
    kk = _dot(latb, wk_ref[...])
    kr = kr_ref[...]
    for hh in range(MLA_HEADS):
        k_out[hh] = (kk[:, hh * MLA_SLOT:(hh + 1) * MLA_SLOT] + kr).astype(k_out.dtype)
    _store_vt(vt_out, _dot(latb, wv_ref[...]))
    dk_out[...] = dk_ref[...].astype(dk_out.dtype)
    _store_vt(dvt_out, dv_ref[...])


def _ctx_prep(ckv, kr_pad, cdk, cdv, lw, *, nb, p):
    T = nb * p
    layer = lw.layer
    cache_row = lambda w: pl.BlockSpec((p, w), lambda b: (b * DEPTH + layer, 0))
    row = lambda w: pl.BlockSpec((p, w), lambda b: (b, 0))
    vt_spec = pl.BlockSpec((MLA_HEADS, VT_ROWS, p), lambda b: (0, 0, b))
    vt_shape = jax.ShapeDtypeStruct((MLA_HEADS, VT_ROWS, T), BF16)
    return pl.pallas_call(
        _ctx_prep_kernel,
        grid=(nb,),
        in_specs=[cache_row(128), cache_row(128), cache_row(256), cache_row(256),
                  lw.spec("wk"), lw.spec("wv")],
        out_specs=[pl.BlockSpec((MLA_HEADS, p, MLA_SLOT), lambda b: (0, b, 0)), vt_spec, row(256), vt_spec],
        out_shape=[jax.ShapeDtypeStruct((MLA_HEADS, T, MLA_SLOT), BF16), vt_shape,
                   jax.ShapeDtypeStruct((T, 256), BF16), vt_shape],
        compiler_params=_params("arbitrary"),
        name="ctx_prep",
    )(ckv, kr_pad, cdk, cdv, lw["wk"], lw["wv"])


SAFE_DENOM = 2.0 ** -60
BOUND_SLACK = 1.02


def _scores(k_new, k_ctx, q):
    sn = _dot_nt(k_new(), q)
    sc = _dot_nt(k_ctx(), q) if k_ctx is not None else None
    return sn, sc


def _exact_shift(k_new, k_ctx, q):
    sn, sc = _scores(k_new, k_ctx, q)
    m = jnp.max(sn, axis=0, keepdims=True)
    if sc is not None:
        m = jnp.maximum(m, jnp.max(sc, axis=0, keepdims=True))
    return m


def _bound_shift(q, key_norm2):
    qf = q.astype(F32)
    ones = jnp.ones((8, q.shape[1]), BF16)
    q_norm2 = _dot_nt(ones, (qf * qf).astype(BF16))[0:1, :]
    return jnp.sqrt(q_norm2 * key_norm2) * BOUND_SLACK


def _max_row_norm2(k_new, k_ctx, col_sum):
    def one(k):
        kf = k.astype(F32)
        return jnp.max(_dot((kf * kf).astype(BF16), col_sum), axis=0, keepdims=True)
    m = one(k_new)
    if k_ctx is not None:
        m = jnp.maximum(m, one(k_ctx))
    return m * BOUND_SLACK


def _exp_stage(e_buf, k_new, k_ctx, q, shift, n_ctx):
    sn, sc = _scores(k_new, k_ctx, q)
    e_buf[n_ctx:, :] = jnp.exp2(sn - shift).astype(BF16)
    if sc is not None:
        e_buf[0:n_ctx, :] = jnp.exp2(sc - shift).astype(BF16)


def _value_stage(e_buf, vt_new, vt_ctx, n_ctx):
    o = _dot(vt_new(), e_buf[n_ctx:, :])
    if vt_ctx is not None:
        o = o + _dot(vt_ctx(), e_buf[0:n_ctx, :])
    return o


def _run_pipeline(n_maps, exp_stage, value_stage):
    exp_stage(0)
    for u in range(n_maps):
        if u + 1 < n_maps:
            exp_stage(u + 1)
        value_stage(u)


def _att_scratch(nk):
    return [pltpu.VMEM((8, 128), F32),
            pltpu.VMEM((MLA_HEADS * HEAD_V, ATT_TQ), F32),
            pltpu.VMEM((nk, ATT_TQ), BF16), pltpu.VMEM((nk, ATT_TQ), BF16)]


def _att_nsub(n):
    return 2 if n % (2 * ATT_TQ) == 0 else 1


def _mla_attn_kernel(*refs, has_ctx, nsub):
    if has_ctx:
        q_ref, k_ref, vt_ref, kc_ref, vtc_ref, o_ref, kn2, ot, e0, e1 = refs
        n_ctx = kc_ref.shape[1]
    else:
        q_ref, k_ref, vt_ref, o_ref, kn2, ot, e0, e1 = refs
        n_ctx = 0
    e_bufs = (e0, e1)

    @pl.when(pl.program_id(1) == 0)
    def _():
        ones = jnp.ones((MLA_SLOT, 128), BF16)
        for hh in range(MLA_HEADS):
            kn2[hh:hh + 1, :] = _max_row_norm2(k_ref[hh], kc_ref[hh] if has_ctx else None, ones)

    def run(exact):
        denoms = []

        def exp_stage(u):
            t, hh = divmod(u, MLA_HEADS)
            q = q_ref[hh, t * ATT_TQ:(t + 1) * ATT_TQ, :]
            k_new = lambda: k_ref[hh]
            k_ctx = (lambda: kc_ref[hh]) if has_ctx else None
            shift = _exact_shift(k_new, k_ctx, q) if exact else _bound_shift(q, kn2[hh:hh + 1, 0:1])
            _exp_stage(e_bufs[u % 2], k_new, k_ctx, q, shift, n_ctx)

        def value_stage(u):
            t, hh = divmod(u, MLA_HEADS)
            o = _value_stage(e_bufs[u % 2], lambda: vt_ref[hh],
                             (lambda: vtc_ref[hh]) if has_ctx else None, n_ctx)
            denom = o[HEAD_V:HEAD_V + 1, :]
            denoms.append(denom)
            ot[hh * HEAD_V:(hh + 1) * HEAD_V, :] = o[0:HEAD_V, :] * (1.0 / denom)
            if hh == MLA_HEADS - 1:
                o_ref[t * ATT_TQ:(t + 1) * ATT_TQ, :] = ot[...].T

        _run_pipeline(nsub * MLA_HEADS, exp_stage, value_stage)
        return jnp.min(functools.reduce(jnp.minimum, denoms))

    denom_min = run(exact=False)

    @pl.when(jnp.logical_not(denom_min >= SAFE_DENOM))
    def _():
        run(exact=True)


def _mla_attn(q, k, vt, ctx, *, nb, n):
    nsub = _att_nsub(n)
    tq = nsub * ATT_TQ
    npt = n // tq
    H, S = MLA_HEADS, MLA_SLOT
    in_specs = [
        pl.BlockSpec((H, tq, S), lambda b, j: (0, b * npt + j, 0)),
        pl.BlockSpec((H, n, S), lambda b, j: (0, b, 0)),
        pl.BlockSpec((H, VT_ROWS, n), lambda b, j: (0, 0, b)),
    ]
    args = [q, k, vt]
    n_ctx = 0
    if ctx is not None:
        n_ctx = ctx[0].shape[1] // nb
        in_specs += [
            pl.BlockSpec((H, n_ctx, S), lambda b, j: (0, b, 0)),
            pl.BlockSpec((H, VT_ROWS, n_ctx), lambda b, j: (0, 0, b)),
        ]
        args += list(ctx)
    return pl.pallas_call(
        functools.partial(_mla_attn_kernel, has_ctx=ctx is not None, nsub=nsub),
        grid=(nb, npt),
        in_specs=in_specs,
        out_specs=pl.BlockSpec((tq, 256), lambda b, j: (b * npt + j, 0)),
        out_shape=jax.ShapeDtypeStruct((nb * n, 256), F32),
        scratch_shapes=_att_scratch(n + n_ctx),
        compiler_params=_params("arbitrary", "arbitrary"),
        name="mla_attn_ctx" if ctx is not None else "mla_attn",
    )(*args)


def _diff_attn_kernel(*refs, has_ctx, nsub, lam_init):
    if has_ctx:
        lv_ref, g_ref, q_ref, k_ref, vt_ref, kc_ref, vtc_ref, o_ref, kn2, ot, e0, e1 = refs
        n_ctx = kc_ref.shape[0]
    else:
        lv_ref, g_ref, q_ref, k_ref, vt_ref, o_ref, kn2, ot, e0, e1 = refs
        n_ctx = 0
    e_bufs = (e0, e1)
    lv = lv_ref[...]
    lam = (jnp.exp(jnp.sum(lv[0:1] * lv[1:2], axis=-1, keepdims=True))
           - jnp.exp(jnp.sum(lv[2:3] * lv[3:4], axis=-1, keepdims=True)) + lam_init)
    lane = lax.broadcasted_iota(jnp.int32, (1, 256), 1)
    n_pairs = 2 * DIFF_HEADS

    @pl.when(pl.program_id(1) == 0)
    def _():
        dim = lax.broadcasted_iota(jnp.int32, (256, 128), 0)
        col = lax.broadcasted_iota(jnp.int32, (256, 128), 1)
        indicator = jnp.where(dim // DIFF_DIM == col, 1.0, 0.0).astype(BF16)
        kn2[0:1, :] = _max_row_norm2(k_ref[...], kc_ref[...] if has_ctx else None, indicator)

    def run(exact):
        denoms = []
        outs = {}
        k_new = lambda: k_ref[...]
        k_ctx = (lambda: kc_ref[...]) if has_ctx else None

        def exp_stage(u):
            t, p = divmod(u, n_pairs)
            q = q_ref[t * ATT_TQ:(t + 1) * ATT_TQ, :]
            in_pair = (lane >= p * DIFF_DIM) & (lane < (p + 1) * DIFF_DIM)
            qm = jnp.where(in_pair, q, jnp.zeros_like(q))
            shift = _exact_shift(k_new, k_ctx, qm) if exact else _bound_shift(qm, kn2[0:1, p:p + 1])
            _exp_stage(e_bufs[u % 2], k_new, k_ctx, qm, shift, n_ctx)

        def value_stage(u):
            t, p = divmod(u, n_pairs)
            hh = p // 2
            o = _value_stage(e_bufs[u % 2], lambda: vt_ref[hh],
                             (lambda: vtc_ref[hh]) if has_ctx else None, n_ctx)
            denom = o[HEAD_V:HEAD_V + 1, :]
            denoms.append(denom)
            outs[u] = (o[0:HEAD_V, :], denom)
            if p % 2 == 1:
                (o0, l0), (o1, l1) = outs.pop(u - 1), outs.pop(u)
                o = o0 * (1.0 / l0) - o1 * (lam / l1)
                msq = jnp.sum(o * o, axis=0, keepdims=True) * (1.0 / HEAD_V)
                ot[hh * HEAD_V:(hh + 1) * HEAD_V, :] = o * lax.rsqrt(msq + EPS)
            if p == n_pairs - 1:
                o_ref[t * ATT_TQ:(t + 1) * ATT_TQ, :] = (ot[...].T * g_ref[...]) * (1.0 - lam_init)

        _run_pipeline(nsub * n_pairs, exp_stage, value_stage)
        return jnp.min(functools.reduce(jnp.minimum, denoms))

    denom_min = run(exact=False)

    @pl.when(jnp.logical_not(denom_min >= SAFE_DENOM))
    def _():
        run(exact=True)


def _diff_attn(q, k, vt, ctx, lw, *, nb, n, lam_init):
    nsub = _att_nsub(n)
    tq = nsub * ATT_TQ
    npt = n // tq
    in_specs = [
        lw.spec("diff_lambda"),
        lw.spec("diff_g"),
        pl.BlockSpec((tq, 256), lambda b, j: (b * npt + j, 0)),
        pl.BlockSpec((n, 256), lambda b, j: (b, 0)),
        pl.BlockSpec((DIFF_HEADS, VT_ROWS, n), lambda b, j: (0, 0, b)),
    ]
    args = [lw["diff_lambda"], lw["diff_g"], q, k, vt]
    n_ctx = 0
    if ctx is not None:
        n_ctx = ctx[0].shape[0] // nb
        in_specs += [pl.BlockSpec((n_ctx, 256), lambda b, j: (b, 0)),
                     pl.BlockSpec((DIFF_HEADS, VT_ROWS, n_ctx), lambda b, j: (0, 0, b))]
        args += list(ctx)
    return pl.pallas_call(
        functools.partial(_diff_attn_kernel, has_ctx=ctx is not None, nsub=nsub, lam_init=lam_init),
        grid=(nb, npt),
        in_specs=in_specs,
        out_specs=pl.BlockSpec((tq, 256), lambda b, j: (b * npt + j, 0)),
        out_shape=jax.ShapeDtypeStruct((nb * n, 256), F32),
        scratch_shapes=_att_scratch(n + n_ctx),
        compiler_params=_params("arbitrary", "arbitrary"),
        name="diff_attn_ctx" if ctx is not None else "diff_attn",
    )(*args)


def _shift_rows(v, k):
    return pltpu.roll(v, (-k) % v.shape[0], 0)


def _scan_chunk(a, b, reverse):
    T = a.shape[0]
    row = lax.broadcasted_iota(jnp.int32, a.shape, 0)
    s = 1
    while s < T:
        if reverse:
            valid = row < T - s
            ap, bp = _shift_rows(a, s), _shift_rows(b, s)
        else:
            valid = row >= s
            ap, bp = _shift_rows(a, -s), _shift_rows(b, -s)
        b = jnp.where(valid, a * bp + b, b)
        a = jnp.where(valid, a * ap, a)
        s *= 2
    return a, b


def _sigmoid(x):
    return 0.5 * jnp.tanh(0.5 * x) + 0.5


def _gelu_tanh(x):
    return x * (0.5 * (1.0 + jnp.tanh(math.sqrt(2.0 / math.pi) * (x + 0.044715 * (x * x * x)))))


def _lru_kernel(u_ref, h0_ref, cw_ref, cb_ref, wg_ref, bg_ref, lam_ref, y_ref, st_ref,
                xpad, a1s, b1s, *, N, T):
    W = LRU_WIDTH
    nc = N // T
    zeros = jnp.zeros((HALO, W), F32)
    xpad[0:HALO, :] = zeros
    xpad[N + HALO:N + 2 * HALO, :] = zeros

    def fill(j, carry):
        r0 = pl.multiple_of(j * T, T)
        xpad[pl.ds(r0 + HALO, T), :] = u_ref[pl.ds(r0, T), 0:W]
        return carry

    lax.fori_loop(0, nc, fill, 0)

    z = -lam_ref[...]
    sp = jnp.maximum(z, 0.0) + jnp.log1p(jnp.exp(-jnp.abs(z)))
    cw = cw_ref[...]
    cb = cb_ref[...]
    bg = bg_ref[...]

    def fwd(j, carry):
        r0 = pl.multiple_of(j * T, T)
        ext = xpad[pl.ds(r0, T + 2 * HALO), :]
        body = slice(HALO, HALO + T)
        xc = cb
        for tap in range(4):
            xc = xc + _shift_rows(ext, tap - 1)[body] * cw[tap:tap + 1]
        g = _sigmoid(_dot(xc.astype(BF16), wg_ref[...]) + bg)
        ab = []
        for d in range(2):
            r = g[:, d * W:(d + 1) * W]
            i = g[:, (2 + d) * W:(3 + d) * W]
            log_a = (-LRU_C * r) * sp[d:d + 1]
            a = jnp.exp(log_a)
            bt = (jnp.sqrt(1.0 - a * a) * i) * xc
            ab.append((a, bt))
        a1s[pl.ds(r0, T), :] = ab[1][0]
        b1s[pl.ds(r0, T), :] = ab[1][1]
        A, Bv = _scan_chunk(ab[0][0], ab[0][1], reverse=False)
        h = A * carry + Bv
        y_ref[pl.ds(r0, T), :] = h
        return h[T - 1:T, :]

    cf = lax.fori_loop(0, nc, fwd, h0_ref[0, 0:1, :])

    def bwd(jj, carry):
        r0 = pl.multiple_of((nc - 1 - jj) * T, T)
        A, Bv = _scan_chunk(a1s[pl.ds(r0, T), :], b1s[pl.ds(r0, T), :], reverse=True)
        h = A * carry + Bv
        gb = u_ref[pl.ds(r0, T), W:2 * W]
        y_ref[pl.ds(r0, T), :] = (y_ref[pl.ds(r0, T), :] + h) * _gelu_tanh(gb)
        return h[0:1, :]

    cbw = lax.fori_loop(0, nc, bwd, h0_ref[0, 1:2, :])
    st_ref[0, 0:1, :] = cf
    st_ref[0, 1:2, :] = cbw


def _lru(u, h0, h0_block, lw, *, nb, n):
    T = min(n, 256)
    W = LRU_WIDTH
    return pl.pallas_call(
        functools.partial(_lru_kernel, N=n, T=T),
        grid=(nb,),
        in_specs=[
            pl.BlockSpec((n, 2 * W), lambda b: (b, 0)),
            pl.BlockSpec((1, 2, W), lambda b: (h0_block(b), 0, 0)),
            lw.spec("conv_w"), lw.spec("conv_b"), lw.spec("w_gate"), lw.spec("b_gate"),
            lw.spec("lru_lambda"),
        ],
        out_specs=[
            pl.BlockSpec((n, W), lambda b: (b, 0)),
            pl.BlockSpec((1, 2, W), lambda b: (b, 0, 0)),
        ],
        out_shape=[
            jax.ShapeDtypeStruct((nb * n, W), F32),
            jax.ShapeDtypeStruct((nb, 2, W), F32),
        ],
        scratch_shapes=[
            pltpu.VMEM((n + 2 * HALO, W), F32),
            pltpu.VMEM((n, W), F32),
            pltpu.VMEM((n, W), F32),
        ],
        compiler_params=_params("arbitrary"),
        name="rglru",
    )(u, h0, lw["conv_w"], lw["conv_b"], lw["w_gate"], lw["b_gate"], lw["lru_lambda"])


def _pool_kernel(u_ref, wp_ref, sc_ref, y_ref, xpad, *, N, T):
    W = GROUP_WIDTH
    nc = N // T
    zeros = jnp.zeros((HALO, W), F32)
    xpad[0:HALO, :] = zeros
    xpad[N + HALO:N + 2 * HALO, :] = zeros

    def fill(j, carry):
        r0 = pl.multiple_of(j * T, T)
        xpad[pl.ds(r0 + HALO, T), :] = u_ref[pl.ds(r0, T), :]
        return carry

    lax.fori_loop(0, nc, fill, 0)

    grp = lax.broadcasted_iota(jnp.int32, (1, W), 1) // POOL_CH
    half = jnp.where(grp == 0, 1, jnp.where(grp == 1, 2, jnp.where(grp == 2, 4, 8)))
    scale = sc_ref[...]

    def chunk(j, carry):
        r0 = pl.multiple_of(j * T, T)
        ext = xpad[pl.ds(r0, T + 2 * HALO), :]
        w2 = _shift_rows(ext, -1) + ext
        w4 = _shift_rows(w2, -1) + _shift_rows(w2, 1)
        w8 = _shift_rows(w4, -2) + _shift_rows(w4, 2)
        w16 = _shift_rows(w8, -4) + _shift_rows(w8, 4)
        ws = jnp.where(grp == 0, w2, jnp.where(grp == 1, w4, jnp.where(grp == 2, w8, w16)))
        body = slice(HALO, HALO + T)
        t = r0 + lax.broadcasted_iota(jnp.int32, (T, W), 0)
        cnt = (jnp.minimum(t + half, N) - jnp.maximum(t - half, 0)).astype(F32)
        d = ws[body] / cnt - ext[body]
        y_ref[pl.ds(r0, T), :] = _dot(d.astype(BF16), wp_ref[...]) * scale
        return carry

    lax.fori_loop(0, nc, chunk, 0)


def _pool(u, lw, *, nb, n):
    W = GROUP_WIDTH
    T = min(n, 256)
    return pl.pallas_call(
        functools.partial(_pool_kernel, N=n, T=T),
        grid=(nb,),
        in_specs=[pl.BlockSpec((n, W), lambda b: (b, 0)), lw.spec("w_pool"), lw.spec("pool_scale")],
        out_specs=pl.BlockSpec((n, W), lambda b: (b, 0)),
        out_shape=jax.ShapeDtypeStruct((nb * n, W), F32),
        scratch_shapes=[pltpu.VMEM((n + 2 * HALO, W), F32)],
        compiler_params=_params("arbitrary"),
        name="pool_mixer",
    )(u, lw["w_pool"], lw["pool_scale"])


def _mix_ffn_kernel(*refs, final):
    if final:
        (x_ref, ya_ref, yb_ref, yc_ref, yd_ref, mod_ref, g2_ref, wo_ref, wg_ref, wu_ref, wd_ref,
         gf_ref, o_ref) = refs
    else:
        (x_ref, ya_ref, yb_ref, yc_ref, yd_ref, mod_ref, g2_ref, wo_ref, wg_ref, wu_ref, wd_ref,
         o_ref) = refs
    mod = mod_ref[0]
    gate1 = mod[:, 2 * D_MODEL:3 * D_MODEL]
    sh2 = mod[:, 3 * D_MODEL:4 * D_MODEL]
    sc2 = mod[:, 4 * D_MODEL:5 * D_MODEL]
    gate2 = mod[:, 5 * D_MODEL:6 * D_MODEL]
    mix = None
    for i, y_ref in enumerate((ya_ref, yb_ref, yc_ref, yd_ref)):
        part = _dot(y_ref[...].astype(BF16), wo_ref[i * GROUP_WIDTH:(i + 1) * GROUP_WIDTH, :])
        mix = part if mix is None else mix + part
    x1 = x_ref[...] + gate1 * mix
    h = _rms_rows(x1, D_MODEL) * g2_ref[...]
    hb = (h * (1.0 + sc2) + sh2).astype(BF16)
    ff = None
    for lo, hi in FF_CHUNKS:
        g = _dot(hb, wg_ref[:, lo:hi])
        up = _dot(hb, wu_ref[:, lo:hi])
        act = ((g * jax.nn.sigmoid(g)) * up).astype(BF16)
        part = _dot(act, wd_ref[lo:hi, :])
        ff = part if ff is None else ff + part
    x2 = x1 + gate2 * ff
    if final:
        x2 = _rms_rows(x2, D_MODEL) * gf_ref[...]
    o_ref[...] = x2


def _mix_ffn(x, ys, mod, lw, gf, *, nb, n, final):
    T = nb * n
    tm = TOKEN_TILE
    npt = n // tm

    def tok(width):
        return pl.BlockSpec((tm, width), lambda i: (i, 0))

    wnames = ("g2", "w_out", "w_gate_ff", "w_up_ff", "w_down")
    in_specs = [tok(D_MODEL), tok(256), tok(256), tok(256), tok(256),
                mod.spec(lambda i: i // npt)] + [lw.spec(nm) for nm in wnames]
    args = [x, *ys, mod.table] + [lw[nm] for nm in wnames]
    if final:
        in_specs.append(_resident((1, D_MODEL)))
        args.append(gf)
    return pl.pallas_call(
        functools.partial(_mix_ffn_kernel, final=final),
        grid=(T // tm,),
        in_specs=in_specs,
        out_specs=tok(D_MODEL),
        out_shape=jax.ShapeDtypeStruct((T, D_MODEL), F32),
        compiler_params=_params("arbitrary"),
        name="mix_ffn_final" if final else "mix_ffn",
    )(*args)


def _block_diag(w):
    L, G, c, e = w.shape
    return jnp.einsum('lgce,gh->lgche', w, jnp.eye(G, dtype=w.dtype)).reshape(L, G * c, G * e)


def _rot_cols(w):
    return jnp.concatenate([-w[..., 16:32], w[..., 0:16]], axis=-1)


def _stack_weights(p):
    w_in = p["w_in"]
    o1 = MLA_Q_RANK
    o2 = o1 + MLA_KV_RANK
    o3 = o2 + MLA_ROPE
    c_q, c_kv, k_r, rest = w_in[..., :o1], w_in[..., o1:o2], w_in[..., o2:o3], w_in[..., o3:]
    z = lambda n: jnp.zeros((DEPTH, D_MODEL, n), F32)
    w_in_eff = jnp.concatenate([c_q, k_r, z(32), c_kv, z(64), _rot_cols(k_r), z(32), rest], axis=-1)

    w_uq = p["mla_w_uq"]
    qd = MLA_NOPE + MLA_ROPE
    wq_parts, wqr_parts = [], []
    zq = lambda n: jnp.zeros((DEPTH, MLA_Q_RANK, n), F32)
    for h in range(MLA_HEADS):
        wh = w_uq[..., h * qd:(h + 1) * qd]
        wq_parts += [wh, zq(MLA_SLOT - qd)]
        wqr_parts += [zq(MLA_NOPE), _rot_cols(wh[..., MLA_NOPE:]), zq(MLA_SLOT - qd)]
    pad_rows = lambda w: jnp.pad(w, ((0, 0), (0, 256 - MLA_Q_RANK), (0, 0)))
    w_ukv = p["mla_w_ukv"]
    wk_parts, wv_parts = [], []
    zk = jnp.zeros((DEPTH, MLA_KV_RANK, MLA_SLOT - MLA_NOPE), F32)
    for h in range(MLA_HEADS):
        base = h * (MLA_NOPE + MLA_V)
        wk_parts += [w_ukv[..., base:base + MLA_NOPE], zk]
        wv_parts.append(w_ukv[..., base + MLA_NOPE:base + MLA_NOPE + MLA_V])

    w_r, w_i, b_r, b_i = p["lru_w_r"], p["lru_w_i"], p["lru_b_r"], p["lru_b_i"]
    w_gate = jnp.concatenate([_block_diag(w_r[:, 0]), _block_diag(w_r[:, 1]),
                              _block_diag(w_i[:, 0]), _block_diag(w_i[:, 1])], axis=-1)
    b_gate = jnp.concatenate([b_r[:, 0], b_r[:, 1], b_i[:, 0], b_i[:, 1]], axis=-1)
    w_gu = p["w_gu"]
    row = lambda v: v[:, None, :]
    return {
        "g1": row(p["norm1_g"]),
        "g2": row(p["norm2_g"]),
        "w_in": w_in_eff.astype(BF16),
        "gq": row(jnp.pad(p["mla_q_norm_g"], ((0, 0), (0, 256 - MLA_Q_RANK)))),
        "gkv": row(p["mla_kv_norm_g"]),
        "wq": pad_rows(jnp.concatenate(wq_parts, axis=-1)).astype(BF16),
        "wqr": pad_rows(jnp.concatenate(wqr_parts, axis=-1)).astype(BF16),
        "wk": jnp.concatenate(wk_parts, axis=-1).astype(BF16),
        "wv": jnp.concatenate(wv_parts, axis=-1).astype(BF16),
        "conv_w": p["lru_conv_w"],
        "conv_b": row(p["lru_conv_b"]),
        "w_gate": w_gate.astype(BF16),
        "b_gate": row(b_gate),
        "lru_lambda": p["lru_lambda"],
        "w_pool": _block_diag(p["pool_w"]).astype(BF16),
        "pool_scale": row(p["pool_scale"]),
        "diff_lambda": p["diff_lambda"],
        "diff_g": row(jnp.tile(p["diff_norm_g"], (1, DIFF_HEADS))),
        "w_out": p["w_out"].astype(BF16),
        "w_gate_ff": w_gu[..., :FF_HIDDEN].astype(BF16),
        "w_up_ff": w_gu[..., FF_HIDDEN:].astype(BF16),
        "w_down": p["w_down"].astype(BF16),
    }


def _rope_tables(n, positional):
    quarter = MLA_ROPE // 4
    if positional:
        t = jnp.arange(n)
        row = (t // GRID_W).astype(F32)
        col = (t % GRID_W).astype(F32)
        inv = ROPE_BASE ** (-jnp.arange(quarter, dtype=F32) / quarter)
        ang = jnp.concatenate([row[:, None] * inv, col[:, None] * inv], axis=-1)
        cos, sin = jnp.cos(ang), jnp.sin(ang)
    else:
        cos, sin = jnp.ones((n, 16), F32), jnp.zeros((n, 16), F32)
    one = lambda w: jnp.ones((n, w), F32)
    zero = lambda w: jnp.zeros((n, w), F32)
    scale = LOG2E / math.sqrt(MLA_NOPE + MLA_ROPE)
    return {
        "cosq": jnp.concatenate([one(64), cos, cos, one(32)], axis=1) * scale,
        "sinq": jnp.concatenate([zero(64), sin, sin, zero(32)], axis=1) * scale,
        "cosk": jnp.concatenate([zero(64), cos, cos, zero(32)], axis=1),
        "sink": jnp.concatenate([zero(64), sin, sin, zero(32)], axis=1),
        "cosd": jnp.tile(jnp.concatenate([cos, cos], axis=1), (1, 8)),
        "sina": jnp.tile(jnp.concatenate([-sin, zero(16)], axis=1), (1, 8)),
        "sinb": jnp.tile(jnp.concatenate([zero(16), sin], axis=1), (1, 8)),
    }


def _layer(x, mod, lw, tabs, layer_idx, ctx, gf, *, nb, n, final):
    emit_cache = ctx is None
    tok_nb, tok_n = (1, nb * n) if mod.shared else (nb, n)
    outs = _inproj(x, mod, lw, tabs, nb=tok_nb, n=tok_n, emit_cache=emit_cache)
    q, k, vt, u_lru, u_pool, dq, dk, dvt = outs[:8]
    lam_init = 0.8 - 0.6 * math.exp(-0.3 * layer_idx)
    if ctx is None:
        h0 = jnp.zeros((1, 2, LRU_WIDTH), F32)
        h0_block = lambda b: 0
        mla_ctx = diff_ctx = None
    else:
        ckv, kr_pad, cdk, cdv, h0 = ctx
        p = ckv.shape[0] // (nb * DEPTH)
        h0_block = lambda b: b * DEPTH + layer_idx
        kc, vtc, dkc, dvtc = _ctx_prep(ckv, kr_pad, cdk, cdv, lw, nb=nb, p=p)
        mla_ctx = (kc, vtc)
        diff_ctx = (dkc, dvtc)
    y_mla = _mla_attn(q, k, vt, mla_ctx, nb=nb, n=n)
    y_lru, st = _lru(u_lru, h0, h0_block, lw, nb=nb, n=n)
    y_pool = _pool(u_pool, lw, nb=nb, n=n)
    y_diff = _diff_attn(dq, dk, dvt, diff_ctx, lw, nb=nb, n=n, lam_init=lam_init)
    x2 = _mix_ffn(x, (y_mla, y_lru, y_pool, y_diff), mod, lw, gf, nb=tok_nb, n=tok_n, final=final)
    cache = (outs[8], outs[9][:, 64:96], outs[10], outs[11], st) if emit_cache else None
    return x2, cache


def kernel(x_prompt, x_sample, cache_mla_ckv, cache_mla_krope, cache_diff_k, cache_diff_v, state_lru,
           c, c_ctx, w_ada, b_ada, norm1_g, norm2_g, w_in, mla_q_norm_g, mla_w_uq, mla_kv_norm_g,
           mla_w_ukv, lru_conv_w, lru_conv_b, lru_w_r, lru_b_r, lru_w_i, lru_b_i, lru_lambda, pool_w,
           pool_scale, diff_lambda, diff_norm_g, w_out, w_gu, w_down, final_norm_g):
    p = {
        "norm1_g": norm1_g, "norm2_g": norm2_g, "w_in": w_in, "mla_q_norm_g": mla_q_norm_g,
        "mla_w_uq": mla_w_uq, "mla_kv_norm_g": mla_kv_norm_g, "mla_w_ukv": mla_w_ukv,
        "lru_conv_w": lru_conv_w, "lru_conv_b": lru_conv_b, "lru_w_r": lru_w_r, "lru_b_r": lru_b_r,
        "lru_w_i": lru_w_i, "lru_b_i": lru_b_i, "lru_lambda": lru_lambda, "pool_w": pool_w,
        "pool_scale": pool_scale, "diff_lambda": diff_lambda, "diff_norm_g": diff_norm_g,
        "w_out": w_out, "w_gu": w_gu, "w_down": w_down,
    }
    Bp, Np, _ = x_prompt.shape
    Bs, Ns, _ = x_sample.shape
    P = cache_mla_ckv.shape[2]

    cond_all = jnp.concatenate([c, c_ctx[None, :], jnp.zeros((MOD_ROWS - Bs - 1, D_MODEL), F32)], axis=0)
    mod_table = _ada(cond_all, w_ada, b_ada).reshape(DEPTH * MOD_ROWS, 1, 6 * D_MODEL)
    tabs_p = _rope_tables(Bp * Np, positional=False)
    tabs_s = _rope_tables(Ns, positional=True)
    kr_pad = jnp.pad(cache_mla_krope, ((0, 0), (0, 0), (0, 0), (MLA_NOPE, MLA_SLOT - MLA_NOPE - MLA_ROPE)))
    flat = lambda a, w: a.reshape(Bs * DEPTH * P, w)
    ctx = (flat(cache_mla_ckv, MLA_KV_RANK), flat(kr_pad, MLA_SLOT), flat(cache_diff_k, 256),
           flat(cache_diff_v, 256), state_lru.reshape(Bs * DEPTH, 2, LRU_WIDTH))
    gf = final_norm_g[None, :]
    stacked = _stack_weights(p)

    xp = x_prompt.reshape(Bp * Np, D_MODEL)
    xs = x_sample.reshape(Bs * Ns, D_MODEL)
    caches = []
    for l in range(DEPTH):
        lw = _LayerWeights(stacked, l)
        final = l == DEPTH - 1
        mod_p = _Mod(mod_table, l * MOD_ROWS + Bs, shared=True)
        mod_s = _Mod(mod_table, l * MOD_ROWS, shared=False)
        xp, cache = _layer(xp, mod_p, lw, tabs_p, l, None, gf, nb=Bp, n=Np, final=final)
        caches.append(cache)
        xs, _ = _layer(xs, mod_s, lw, tabs_s, l, ctx, gf, nb=Bs, n=Ns, final=final)

    stack = lambda i, w: jnp.stack([cc[i].reshape(Bp, Np, w) for cc in caches], axis=1)
    new_mla_ckv = stack(0, MLA_KV_RANK)
    new_mla_krope = stack(1, MLA_ROPE)
    new_diff_k = stack(2, 256).reshape(Bp, DEPTH, Np, DIFF_HEADS, 2, DIFF_DIM)
    new_diff_v = stack(3, 256).reshape(Bp, DEPTH, Np, DIFF_HEADS, 2 * DIFF_DIM)
    new_state_lru = jnp.stack([cc[4] for cc in caches], axis=1)
    return (xp.reshape(Bp, Np, D_MODEL), xs.reshape(Bs, Ns, D_MODEL),
            new_mla_ckv, new_mla_krope, new_diff_k, new_diff_v, new_state_lru)
```

```python
import functools
import math

import jax
import jax.numpy as jnp
from jax import lax
from jax.experimental import pallas as pl
from jax.experimental.pallas import tpu as pltpu

F32 = jnp.float32
BF16 = jnp.bfloat16

D_MODEL = 1024
DEPTH = 2
GRID_W = 64
GROUP_WIDTH = 256
MLA_HEADS = 4
MLA_NOPE = 64
MLA_ROPE = 32
MLA_V = 64
MLA_Q_RANK = 192
MLA_KV_RANK = 128
MLA_SLOT = 128
LRU_WIDTH = 256
LRU_C = 8.0
POOL_WINDOWS = (2, 4, 8, 16)
POOL_CH = 64
DIFF_HEADS = 4
DIFF_DIM = 32
HEAD_V = 64
FF_HIDDEN = 2816
FF_CHUNKS = ((0, 1536), (1536, 2816))
ROPE_BASE = 10000.0
EPS = 1e-6
IN_EFF = 2048
HALO = 8
VT_ROWS = 80
ATT_TQ = 256
TOKEN_TILE = 512
MOD_ROWS = 16
LOG2E = math.log2(math.e)

VMEM_LIMIT_BYTES = 56 * 1024 * 1024

_NT = (((1,), (1,)), ((), ()))


def _params(*sem):
    return pltpu.CompilerParams(dimension_semantics=sem, vmem_limit_bytes=VMEM_LIMIT_BYTES)


def _resident(shape):
    zeros = (0,) * len(shape)
    return pl.BlockSpec(shape, lambda *_: zeros, pipeline_mode=pl.Buffered(1))


def _dot(a, b):
    return jnp.dot(a, b, preferred_element_type=F32)


def _dot_nt(a, b):
    return lax.dot_general(a, b, _NT, preferred_element_type=F32)


def _rms_rows(x, width):
    ms = jnp.sum(x * x, axis=-1, keepdims=True) * (1.0 / width)
    return x * lax.rsqrt(ms + EPS)


def _store_vt(vt_ref, v):
    vt = v.T
    rows = v.shape[0]
    pad = VT_ROWS - HEAD_V
    ones_row = jnp.where(lax.broadcasted_iota(jnp.int32, (pad, rows), 0) == 0, 1.0, 0.0).astype(BF16)
    for hh in range(vt_ref.shape[0]):
        vt_ref[hh, 0:HEAD_V, :] = vt[hh * HEAD_V:(hh + 1) * HEAD_V, :].astype(BF16)
        vt_ref[hh, HEAD_V:VT_ROWS, :] = ones_row


class _Mod:
    def __init__(self, table, row0, shared):
        self.table, self.row0, self.shared = table, row0, shared

    def spec(self, batch_of):
        row0 = self.row0
        if self.shared:
            return pl.BlockSpec((1, 1, 6 * D_MODEL), lambda *g: (row0, 0, 0))
        return pl.BlockSpec((1, 1, 6 * D_MODEL), lambda *g: (row0 + batch_of(*g), 0, 0))


class _LayerWeights:
    def __init__(self, stacked, layer):
        self.stacked, self.layer = stacked, layer

    def __getitem__(self, name):
        return self.stacked[name]

    def spec(self, name):
        layer = self.layer
        _, rows, cols = self.stacked[name].shape
        return pl.BlockSpec((None, rows, cols), lambda *_: (layer, 0, 0), pipeline_mode=pl.Buffered(1))


def _ada_kernel(cond_ref, w_ref, b_ref, out_ref):
    c = cond_ref[...]
    s = c * jax.nn.sigmoid(c)
    out_ref[0] = _dot(s.astype(BF16), w_ref[0].astype(BF16)) + b_ref[0]


def _ada(cond_all, w_ada, b_ada):
    rows = cond_all.shape[0]
    tn = 1536
    return pl.pallas_call(
        _ada_kernel,
        grid=(DEPTH, 6 * D_MODEL // tn),
        in_specs=[
            pl.BlockSpec((rows, D_MODEL), lambda l, j: (0, 0)),
            pl.BlockSpec((1, D_MODEL, tn), lambda l, j: (l, 0, j)),
            pl.BlockSpec((1, 1, tn), lambda l, j: (l, 0, j)),
        ],
        out_specs=pl.BlockSpec((1, rows, tn), lambda l, j: (l, 0, j)),
        out_shape=jax.ShapeDtypeStruct((DEPTH, rows, 6 * D_MODEL), F32),
        compiler_params=_params("arbitrary", "arbitrary"),
        name="ada_mod",
    )(cond_all, w_ada, b_ada.reshape(DEPTH, 1, 6 * D_MODEL))


def _inproj_kernel(x_ref, mod_ref, g1_ref, win_ref, gq_ref, gkv_ref, wq_ref, wqr_ref, wk_ref, wv_ref,
                   cosq_ref, sinq_ref, cosk_ref, sink_ref, cosd_ref, sina_ref, sinb_ref,
                   q_out, k_out, vt_out, lru_out, pool_out, dq_out, dk_out, dvt_out, *cache_outs):
    x = x_ref[...]
    mod = mod_ref[0]
    sh1 = mod[:, 0:D_MODEL]
    sc1 = mod[:, D_MODEL:2 * D_MODEL]
    h = _rms_rows(x, D_MODEL) * g1_ref[...]
    hb = (h * (1.0 + sc1) + sh1).astype(BF16)

    t01 = _dot(hb, win_ref[:, 0:256])
    lane = lax.broadcasted_iota(jnp.int32, (1, 256), 1)
    cq = jnp.where(lane < MLA_Q_RANK, t01, 0.0)
    cqn = (_rms_rows(cq, MLA_Q_RANK) * gq_ref[...]).astype(BF16)
    qa = _dot(cqn, wq_ref[...])
    qr = _dot(cqn, wqr_ref[...])
    cosq = cosq_ref[...]
    sinq = sinq_ref[...]
    ckv = _dot(hb, win_ref[:, 256:384])
    lat = _rms_rows(ckv, MLA_KV_RANK) * gkv_ref[...]
    latb = lat.astype(BF16)
    kk = _dot(latb, wk_ref[...])
    _store_vt(vt_out, _dot(latb, wv_ref[...]))
    t1 = t01[:, 128:256]
    t3 = _dot(hb, win_ref[:, 384:512])
    kro = t1 * cosk_ref[...] + t3 * sink_ref[...]
    for hh in range(MLA_HEADS):
        sl = slice(hh * MLA_SLOT, (hh + 1) * MLA_SLOT)
        q_out[hh] = (qa[:, sl] * cosq + qr[:, sl] * sinq).astype(q_out.dtype)
        k_out[hh] = (kk[:, sl] + kro).astype(k_out.dtype)

    lru_out[...] = _dot(hb, win_ref[:, 512:1024])
    pool_out[...] = _dot(hb, win_ref[:, 1024:1280])

    cosd = cosd_ref[...]
    sina = sina_ref[...]
    sinb = sinb_ref[...]

    def rope(t):
        return t * cosd + pltpu.roll(t, 256 - 16, 1) * sina + pltpu.roll(t, 16, 1) * sinb

    dq = _dot(hb, win_ref[:, 1280:1536])
    dk = _dot(hb, win_ref[:, 1536:1792])
    dv = _dot(hb, win_ref[:, 1792:2048])
    dq_out[...] = (rope(dq) * (LOG2E / math.sqrt(DIFF_DIM))).astype(dq_out.dtype)
    dk_out[...] = rope(dk).astype(dk_out.dtype)
    _store_vt(dvt_out, dv)

    if cache_outs:
        lat_out, kr_out, dk_raw_out, dv_raw_out = cache_outs
        lat_out[...] = lat
        kr_out[...] = t1
        dk_raw_out[...] = dk
        dv_raw_out[...] = dv


def _inproj(x, mod, lw, tabs, *, nb, n, emit_cache):
    T = nb * n
    tm = TOKEN_TILE
    npt = n // tm
    row_blk = lambda j, b: b * npt + j

    def tok(width):
        return pl.BlockSpec((tm, width), lambda j, b: (row_blk(j, b), 0))

    def tab(width):
        return pl.BlockSpec((tm, width), lambda j, b: (j, 0))

    head = pl.BlockSpec((MLA_HEADS, tm, MLA_SLOT), lambda j, b: (0, row_blk(j, b), 0))
    vt_spec = pl.BlockSpec((MLA_HEADS, VT_ROWS, tm), lambda j, b: (0, 0, row_blk(j, b)))
    wnames = ("g1", "w_in", "gq", "gkv", "wq", "wqr", "wk", "wv")
    in_specs = [tok(D_MODEL), mod.spec(lambda j, b: b)] + [lw.spec(nm) for nm in wnames] + [
        tab(128), tab(128), tab(128), tab(128), tab(256), tab(256), tab(256)]
    out_specs = [head, head, vt_spec, tok(512), tok(256), tok(256), tok(256), vt_spec]
    vt_shape = jax.ShapeDtypeStruct((MLA_HEADS, VT_ROWS, T), BF16)
    out_shape = [
        jax.ShapeDtypeStruct((MLA_HEADS, T, MLA_SLOT), BF16),
        jax.ShapeDtypeStruct((MLA_HEADS, T, MLA_SLOT), BF16),
        vt_shape,
        jax.ShapeDtypeStruct((T, 512), F32),
        jax.ShapeDtypeStruct((T, 256), F32),
        jax.ShapeDtypeStruct((T, 256), BF16),
        jax.ShapeDtypeStruct((T, 256), BF16),
        vt_shape,
    ]
    if emit_cache:
        out_specs += [tok(128), tok(128), tok(256), tok(256)]
        out_shape += [jax.ShapeDtypeStruct((T, 128), F32), jax.ShapeDtypeStruct((T, 128), F32),
                      jax.ShapeDtypeStruct((T, 256), F32), jax.ShapeDtypeStruct((T, 256), F32)]
    return pl.pallas_call(
        _inproj_kernel,
        grid=(npt, nb),
        in_specs=in_specs,
        out_specs=out_specs,
        out_shape=out_shape,
        compiler_params=_params("arbitrary", "arbitrary"),
        name="inproj_cache" if emit_cache else "inproj",
    )(x, mod.table, *[lw[nm] for nm in wnames],
      tabs["cosq"], tabs["sinq"], tabs["cosk"], tabs["sink"], tabs["cosd"], tabs["sina"], tabs["sinb"])


def _ctx_prep_kernel(ckv_ref, kr_ref, dk_ref, dv_ref, wk_ref, wv_ref, k_out, vt_out, dk_out, dvt_out):
    latb = ckv_ref[...].astype(BF16)
    kk = _dot(latb, wk_ref[...])
    kr = kr_ref[...]
    for hh in range(MLA_HEADS):
        k_out[hh] = (kk[:, hh * MLA_SLOT:(hh + 1) * MLA_SLOT] + kr).astype(k_out.dtype)
    _store_vt(vt_out, _dot(latb, wv_ref[...]))
    dk_out[...] = dk_ref[...].astype(dk_out.dtype)
    _store_vt(dvt_out, dv_ref[...])


def _ctx_prep(ckv, kr_pad, cdk, cdv, lw, *, nb, p):
    T = nb * p
    layer = lw.layer
    cache_row = lambda w: pl.BlockSpec((p, w), lambda b: (b * DEPTH + layer, 0))
    row = lambda w: pl.BlockSpec((p, w), lambda b: (b, 0))
    vt_spec = pl.BlockSpec((MLA_HEADS, VT_ROWS, p), lambda b: (0, 0, b))
    vt_shape = jax.ShapeDtypeStruct((MLA_HEADS, VT_ROWS, T), BF16)
    return pl.pallas_call(
        _ctx_prep_kernel,
        grid=(nb,),
        in_specs=[cache_row(128), cache_row(128), cache_row(256), cache_row(256),
                  lw.spec("wk"), lw.spec("wv")],
        out_specs=[pl.BlockSpec((MLA_HEADS, p, MLA_SLOT), lambda b: (0, b, 0)), vt_spec, row(256), vt_spec],
        out_shape=[jax.ShapeDtypeStruct((MLA_HEADS, T, MLA_SLOT), BF16), vt_shape,
                   jax.ShapeDtypeStruct((T, 256), BF16), vt_shape],
        compiler_params=_params("arbitrary"),
        name="ctx_prep",
    )(ckv, kr_pad, cdk, cdv, lw["wk"], lw["wv"])


SAFE_DENOM = 2.0 ** -60
BOUND_SLACK = 1.02


def _scores(k_new, k_ctx, q):
    sn = _dot_nt(k_new(), q)
    sc = _dot_nt(k_ctx(), q) if k_ctx is not None else None
    return sn, sc


def _exact_shift(k_new, k_ctx, q):
    sn, sc = _scores(k_new, k_ctx, q)
    m = jnp.max(sn, axis=0, keepdims=True)
    if sc is not None:
        m = jnp.maximum(m, jnp.max(sc, axis=0, keepdims=True))
    return m


def _bound_shift(q, key_norm2):
    qf = q.astype(F32)
    ones = jnp.ones((8, q.shape[1]), BF16)
    q_norm2 = _dot_nt(ones, (qf * qf).astype(BF16))[0:1, :]
    return jnp.sqrt(q_norm2 * key_norm2) * BOUND_SLACK


def _max_row_norm2(k_new, k_ctx, col_sum):
    def one(k):
        kf = k.astype(F32)
        return jnp.max(_dot((kf * kf).astype(BF16), col_sum), axis=0, keepdims=True)
    m = one(k_new)
    if k_ctx is not None:
        m = jnp.maximum(m, one(k_ctx))
    return m * BOUND_SLACK


def _exp_stage(e_buf, k_new, k_ctx, q, shift, n_ctx):
    sn, sc = _scores(k_new, k_ctx, q)
    e_buf[n_ctx:, :] = jnp.exp2(sn - shift).astype(BF16)
    if sc is not None:
        e_buf[0:n_ctx, :] = jnp.exp2(sc - shift).astype(BF16)


def _value_stage(e_buf, vt_new, vt_ctx, n_ctx):
    o = _dot(vt_new(), e_buf[n_ctx:, :])
    if vt_ctx is not None:
        o = o + _dot(vt_ctx(), e_buf[0:n_ctx, :])
    return o


def _run_pipeline(n_maps, exp_stage, value_stage):
    exp_stage(0)
    for u in range(n_maps):
        if u + 1 < n_maps:
            exp_stage(u + 1)
        value_stage(u)


def _att_scratch(nk):
    return [pltpu.VMEM((8, 128), F32),
            pltpu.VMEM((MLA_HEADS * HEAD_V, ATT_TQ), F32),
            pltpu.VMEM((nk, ATT_TQ), BF16), pltpu.VMEM((nk, ATT_TQ), BF16)]


def _att_nsub(n):
    return 2 if n % (2 * ATT_TQ) == 0 else 1


def _mla_attn_kernel(*refs, has_ctx, nsub):
    if has_ctx:
        q_ref, k_ref, vt_ref, kc_ref, vtc_ref, o_ref, kn2, ot, e0, e1 = refs
        n_ctx = kc_ref.shape[1]
    else:
        q_ref, k_ref, vt_ref, o_ref, kn2, ot, e0, e1 = refs
        n_ctx = 0
    e_bufs = (e0, e1)

    @pl.when(pl.program_id(1) == 0)
    def _():
        ones = jnp.ones((MLA_SLOT, 128), BF16)
        for hh in range(MLA_HEADS):
            kn2[hh:hh + 1, :] = _max_row_norm2(k_ref[hh], kc_ref[hh] if has_ctx else None, ones)

    def run(exact):
        denoms = []

        def exp_stage(u):
            t, hh = divmod(u, MLA_HEADS)
            q = q_ref[hh, t * ATT_TQ:(t + 1) * ATT_TQ, :]
            k_new = lambda: k_ref[hh]
            k_ctx = (lambda: kc_ref[hh]) if has_ctx else None
            shift = _exact_shift(k_new, k_ctx, q) if exact else _bound_shift(q, kn2[hh:hh + 1, 0:1])
            _exp_stage(e_bufs[u % 2], k_new, k_ctx, q, shift, n_ctx)

        def value_stage(u):
            t, hh = divmod(u, MLA_HEADS)
            o = _value_stage(e_bufs[u % 2], lambda: vt_ref[hh],
                             (lambda: vtc_ref[hh]) if has_ctx else None, n_ctx)
            denom = o[HEAD_V:HEAD_V + 1, :]
            denoms.append(denom)
            ot[hh * HEAD_V:(hh + 1) * HEAD_V, :] = o[0:HEAD_V, :] * (1.0 / denom)
            if hh == MLA_HEADS - 1:
                o_ref[t * ATT_TQ:(t + 1) * ATT_TQ, :] = ot[...].T

        _run_pipeline(nsub * MLA_HEADS, exp_stage, value_stage)
        return jnp.min(functools.reduce(jnp.minimum, denoms))

    denom_min = run(exact=False)

    @pl.when(jnp.logical_not(denom_min >= SAFE_DENOM))
    def _():
        run(exact=True)


def _mla_attn(q, k, vt, ctx, *, nb, n):
    nsub = _att_nsub(n)
    tq = nsub * ATT_TQ
    npt = n // tq
    H, S = MLA_HEADS, MLA_SLOT
    in_specs = [
        pl.BlockSpec((H, tq, S), lambda b, j: (0, b * npt + j, 0)),
        pl.BlockSpec((H, n, S), lambda b, j: (0, b, 0)),
        pl.BlockSpec((H, VT_ROWS, n), lambda b, j: (0, 0, b)),
    ]
    args = [q, k, vt]
    n_ctx = 0
    if ctx is not None:
        n_ctx = ctx[0].shape[1] // nb
        in_specs += [
            pl.BlockSpec((H, n_ctx, S), lambda b, j: (0, b, 0)),
            pl.BlockSpec((H, VT_ROWS, n_ctx), lambda b, j: (0, 0, b)),
        ]
        args += list(ctx)
    return pl.pallas_call(
        functools.partial(_mla_attn_kernel, has_ctx=ctx is not None, nsub=nsub),
        grid=(nb, npt),
        in_specs=in_specs,
        out_specs=pl.BlockSpec((tq, 256), lambda b, j: (b * npt + j, 0)),
        out_shape=jax.ShapeDtypeStruct((nb * n, 256), F32),
        scratch_shapes=_att_scratch(n + n_ctx),
        compiler_params=_params("arbitrary", "arbitrary"),
        name="mla_attn_ctx" if ctx is not None else "mla_attn",
    )(*args)


def _diff_attn_kernel(*refs, has_ctx, nsub, lam_init):
    if has_ctx:
        lv_ref, g_ref, q_ref, k_ref, vt_ref, kc_ref, vtc_ref, o_ref, kn2, ot, e0, e1 = refs
        n_ctx = kc_ref.shape[0]
    else:
        lv_ref, g_ref, q_ref, k_ref, vt_ref, o_ref, kn2, ot, e0, e1 = refs
        n_ctx = 0
    e_bufs = (e0, e1)
    lv = lv_ref[...]
    lam = (jnp.exp(jnp.sum(lv[0:1] * lv[1:2], axis=-1, keepdims=True))
           - jnp.exp(jnp.sum(lv[2:3] * lv[3:4], axis=-1, keepdims=True)) + lam_init)
    lane128 = lax.broadcasted_iota(jnp.int32, (1, 128), 1)
    n_pairs = 2 * DIFF_HEADS

    @pl.when(pl.program_id(1) == 0)
    def _():
        dim = lax.broadcasted_iota(jnp.int32, (256, 128), 0)
        col = lax.broadcasted_iota(jnp.int32, (256, 128), 1)
        indicator = jnp.where(dim // DIFF_DIM == col, 1.0, 0.0).astype(BF16)
        kn2[0:1, :] = _max_row_norm2(k_ref[...], kc_ref[...] if has_ctx else None, indicator)

    def run(exact):
        denoms = []
        outs = {}

        def exp_stage(u):
            t, p = divmod(u, n_pairs)
            tile = slice((p * DIFF_DIM // 128) * 128, (p * DIFF_DIM // 128 + 1) * 128)
            k_new = lambda: k_ref[:, tile]
            k_ctx = (lambda: kc_ref[:, tile]) if has_ctx else None
            q = q_ref[t * ATT_TQ:(t + 1) * ATT_TQ, tile]
            lo = p * DIFF_DIM - tile.start
            in_pair = (lane128 >= lo) & (lane128 < lo + DIFF_DIM)
            qm = jnp.where(in_pair, q, jnp.zeros_like(q))
            shift = _exact_shift(k_new, k_ctx, qm) if exact else _bound_shift(qm, kn2[0:1, p:p + 1])
            _exp_stage(e_bufs[u % 2], k_new, k_ctx, qm, shift, n_ctx)

        def value_stage(u):
            t, p = divmod(u, n_pairs)
            hh = p // 2
            o = _value_stage(e_bufs[u % 2], lambda: vt_ref[hh],
                             (lambda: vtc_ref[hh]) if has_ctx else None, n_ctx)
            denom = o[HEAD_V:HEAD_V + 1, :]
            denoms.append(denom)
            outs[u] = (o[0:HEAD_V, :], denom)
            if p % 2 == 1:
                (o0, l0), (o1, l1) = outs.pop(u - 1), outs.pop(u)
                o = o0 * (1.0 / l0) - o1 * (lam / l1)
                msq = jnp.sum(o * o, axis=0, keepdims=True) * (1.0 / HEAD_V)
                ot[hh * HEAD_V:(hh + 1) * HEAD_V, :] = o * lax.rsqrt(msq + EPS)
            if p == n_pairs - 1:
                o_ref[t * ATT_TQ:(t + 1) * ATT_TQ, :] = (ot[...].T * g_ref[...]) * (1.0 - lam_init)

        _run_pipeline(nsub * n_pairs, exp_stage, value_stage)
        return jnp.min(functools.reduce(jnp.minimum, denoms))

    denom_min = run(exact=False)

    @pl.when(jnp.logical_not(denom_min >= SAFE_DENOM))
    def _():
        run(exact=True)


def _diff_attn(q, k, vt, ctx, lw, *, nb, n, lam_init):
    nsub = _att_nsub(n)
    tq = nsub * ATT_TQ
    npt = n // tq
    in_specs = [
        lw.spec("diff_lambda"),
        lw.spec("diff_g"),
        pl.BlockSpec((tq, 256), lambda b, j: (b * npt + j, 0)),
        pl.BlockSpec((n, 256), lambda b, j: (b, 0)),
        pl.BlockSpec((DIFF_HEADS, VT_ROWS, n), lambda b, j: (0, 0, b)),
    ]
    args = [lw["diff_lambda"], lw["diff_g"], q, k, vt]
    n_ctx = 0
    if ctx is not None:
        n_ctx = ctx[0].shape[0] // nb
        in_specs += [pl.BlockSpec((n_ctx, 256), lambda b, j: (b, 0)),
                     pl.BlockSpec((DIFF_HEADS, VT_ROWS, n_ctx), lambda b, j: (0, 0, b))]
        args += list(ctx)
    return pl.pallas_call(
        functools.partial(_diff_attn_kernel, has_ctx=ctx is not None, nsub=nsub, lam_init=lam_init),
        grid=(nb, npt),
        in_specs=in_specs,
        out_specs=pl.BlockSpec((tq, 256), lambda b, j: (b * npt + j, 0)),
        out_shape=jax.ShapeDtypeStruct((nb * n, 256), F32),
        scratch_shapes=_att_scratch(n + n_ctx),
        compiler_params=_params("arbitrary", "arbitrary"),
        name="diff_attn_ctx" if ctx is not None else "diff_attn",
    )(*args)


def _shift_rows(v, k):
    return pltpu.roll(v, (-k) % v.shape[0], 0)


def _scan_chunk(a, b, reverse):
    T = a.shape[0]
    row = lax.broadcasted_iota(jnp.int32, a.shape, 0)
    s = 1
    while s < T:
        if reverse:
            valid = row < T - s
            ap, bp = _shift_rows(a, s), _shift_rows(b, s)
        else:
            valid = row >= s
            ap, bp = _shift_rows(a, -s), _shift_rows(b, -s)
        b = jnp.where(valid, a * bp + b, b)
        a = jnp.where(valid, a * ap, a)
        s *= 2
    return a, b


def _sigmoid(x):
    return 0.5 * jnp.tanh(0.5 * x) + 0.5


def _gelu_tanh(x):
    return x * (0.5 * (1.0 + jnp.tanh(math.sqrt(2.0 / math.pi) * (x + 0.044715 * (x * x * x)))))


def _lru_kernel(u_ref, h0_ref, cw_ref, cb_ref, wg_ref, bg_ref, lam_ref, y_ref, st_ref,
                xpad, a1s, b1s, *, N, T):
    W = LRU_WIDTH
    nc = N // T
    zeros = jnp.zeros((HALO, W), F32)
    xpad[0:HALO, :] = zeros
    xpad[N + HALO:N + 2 * HALO, :] = zeros

    def fill(j, carry):
        r0 = pl.multiple_of(j * T, T)
        xpad[pl.ds(r0 + HALO, T), :] = u_ref[pl.ds(r0, T), 0:W]
        return carry

    lax.fori_loop(0, nc, fill, 0)

    z = -lam_ref[...]
    sp = jnp.maximum(z, 0.0) + jnp.log1p(jnp.exp(-jnp.abs(z)))
    cw = cw_ref[...]
    cb = cb_ref[...]
    bg = bg_ref[...]

    def fwd(j, carry):
        r0 = pl.multiple_of(j * T, T)
        ext = xpad[pl.ds(r0, T + 2 * HALO), :]
        body = slice(HALO, HALO + T)
        xc = cb
        for tap in range(4):
            xc = xc + _shift_rows(ext, tap - 1)[body] * cw[tap:tap + 1]
        g = _sigmoid(_dot(xc.astype(BF16), wg_ref[...]) + bg)
        ab = []
        for d in range(2):
            r = g[:, d * W:(d + 1) * W]
            i = g[:, (2 + d) * W:(3 + d) * W]
            log_a = (-LRU_C * r) * sp[d:d + 1]
            a = jnp.exp(log_a)
            bt = (jnp.sqrt(1.0 - a * a) * i) * xc
            ab.append((a, bt))
        a1s[pl.ds(r0, T), :] = ab[1][0]
        b1s[pl.ds(r0, T), :] = ab[1][1]
        A, Bv = _scan_chunk(ab[0][0], ab[0][1], reverse=False)
        h = A * carry + Bv
        y_ref[pl.ds(r0, T), :] = h
        return h[T - 1:T, :]

    cf = lax.fori_loop(0, nc, fwd, h0_ref[0, 0:1, :])

    def bwd(jj, carry):
        r0 = pl.multiple_of((nc - 1 - jj) * T, T)
        A, Bv = _scan_chunk(a1s[pl.ds(r0, T), :], b1s[pl.ds(r0, T), :], reverse=True)
        h = A * carry + Bv
        gb = u_ref[pl.ds(r0, T), W:2 * W]
        y_ref[pl.ds(r0, T), :] = (y_ref[pl.ds(r0, T), :] + h) * _gelu_tanh(gb)
        return h[0:1, :]

    cbw = lax.fori_loop(0, nc, bwd, h0_ref[0, 1:2, :])
    st_ref[0, 0:1, :] = cf
    st_ref[0, 1:2, :] = cbw


def _lru(u, h0, h0_block, lw, *, nb, n):
    T = min(n, 256)
    W = LRU_WIDTH
    return pl.pallas_call(
        functools.partial(_lru_kernel, N=n, T=T),
        grid=(nb,),
        in_specs=[
            pl.BlockSpec((n, 2 * W), lambda b: (b, 0)),
            pl.BlockSpec((1, 2, W), lambda b: (h0_block(b), 0, 0)),
            lw.spec("conv_w"), lw.spec("conv_b"), lw.spec("w_gate"), lw.spec("b_gate"),
            lw.spec("lru_lambda"),
        ],
        out_specs=[
            pl.BlockSpec((n, W), lambda b: (b, 0)),
            pl.BlockSpec((1, 2, W), lambda b: (b, 0, 0)),
        ],
        out_shape=[
            jax.ShapeDtypeStruct((nb * n, W), F32),
            jax.ShapeDtypeStruct((nb, 2, W), F32),
        ],
        scratch_shapes=[
            pltpu.VMEM((n + 2 * HALO, W), F32),
            pltpu.VMEM((n, W), F32),
            pltpu.VMEM((n, W), F32),
        ],
        compiler_params=_params("arbitrary"),
        name="rglru",
    )(u, h0, lw["conv_w"], lw["conv_b"], lw["w_gate"], lw["b_gate"], lw["lru_lambda"])


def _pool_kernel(u_ref, wp_ref, sc_ref, y_ref, xpad, *, N, T):
    W = GROUP_WIDTH
    nc = N // T
    zeros = jnp.zeros((HALO, W), F32)
    xpad[0:HALO, :] = zeros
    xpad[N + HALO:N + 2 * HALO, :] = zeros

    def fill(j, carry):
        r0 = pl.multiple_of(j * T, T)
        xpad[pl.ds(r0 + HALO, T), :] = u_ref[pl.ds(r0, T), :]
        return carry

    lax.fori_loop(0, nc, fill, 0)

    grp = lax.broadcasted_iota(jnp.int32, (1, W), 1) // POOL_CH
    half = jnp.where(grp == 0, 1, jnp.where(grp == 1, 2, jnp.where(grp == 2, 4, 8)))
    scale = sc_ref[...]

    def chunk(j, carry):
        r0 = pl.multiple_of(j * T, T)
        ext = xpad[pl.ds(r0, T + 2 * HALO), :]
        w2 = _shift_rows(ext, -1) + ext
        w4 = _shift_rows(w2, -1) + _shift_rows(w2, 1)
        w8 = _shift_rows(w4, -2) + _shift_rows(w4, 2)
        w16 = _shift_rows(w8, -4) + _shift_rows(w8, 4)
        ws = jnp.where(grp == 0, w2, jnp.where(grp == 1, w4, jnp.where(grp == 2, w8, w16)))
        body = slice(HALO, HALO + T)
        t = r0 + lax.broadcasted_iota(jnp.int32, (T, W), 0)
        cnt = (jnp.minimum(t + half, N) - jnp.maximum(t - half, 0)).astype(F32)
        d = ws[body] / cnt - ext[body]
        y_ref[pl.ds(r0, T), :] = _dot(d.astype(BF16), wp_ref[...]) * scale
        return carry

    lax.fori_loop(0, nc, chunk, 0)


def _pool(u, lw, *, nb, n):
    W = GROUP_WIDTH
    T = min(n, 256)
    return pl.pallas_call(
        functools.partial(_pool_kernel, N=n, T=T),
        grid=(nb,),
        in_specs=[pl.BlockSpec((n, W), lambda b: (b, 0)), lw.spec("w_pool"), lw.spec("pool_scale")],
        out_specs=pl.BlockSpec((n, W), lambda b: (b, 0)),
        out_shape=jax.ShapeDtypeStruct((nb * n, W), F32),
        scratch_shapes=[pltpu.VMEM((n + 2 * HALO, W), F32)],
        compiler_params=_params("arbitrary"),
        name="pool_mixer",
    )(u, lw["w_pool"], lw["pool_scale"])


def _mix_ffn_kernel(*refs, final):
    if final:
        (x_ref, ya_ref, yb_ref, yc_ref, yd_ref, mod_ref, g2_ref, wo_ref, wg_ref, wu_ref, wd_ref,
         gf_ref, o_ref) = refs
    else:
        (x_ref, ya_ref, yb_ref, yc_ref, yd_ref, mod_ref, g2_ref, wo_ref, wg_ref, wu_ref, wd_ref,
         o_ref) = refs
    mod = mod_ref[0]
    gate1 = mod[:, 2 * D_MODEL:3 * D_MODEL]
    sh2 = mod[:, 3 * D_MODEL:4 * D_MODEL]
    sc2 = mod[:, 4 * D_MODEL:5 * D_MODEL]
    gate2 = mod[:, 5 * D_MODEL:6 * D_MODEL]
    mix = None
    for i, y_ref in enumerate((ya_ref, yb_ref, yc_ref, yd_ref)):
        part = _dot(y_ref[...].astype(BF16), wo_ref[i * GROUP_WIDTH:(i + 1) * GROUP_WIDTH, :])
        mix = part if mix is None else mix + part
    x1 = x_ref[...] + gate1 * mix
    h = _rms_rows(x1, D_MODEL) * g2_ref[...]
    hb = (h * (1.0 + sc2) + sh2).astype(BF16)
    ff = None
    for lo, hi in FF_CHUNKS:
        g = _dot(hb, wg_ref[:, lo:hi])
        up = _dot(hb, wu_ref[:, lo:hi])
        act = ((g * jax.nn.sigmoid(g)) * up).astype(BF16)
        part = _dot(act, wd_ref[lo:hi, :])
        ff = part if ff is None else ff + part
    x2 = x1 + gate2 * ff
    if final:
        x2 = _rms_rows(x2, D_MODEL) * gf_ref[...]
    o_ref[...] = x2


def _mix_ffn(x, ys, mod, lw, gf, *, nb, n, final):
    T = nb * n
    tm = TOKEN_TILE
    npt = n // tm

    def tok(width):
        return pl.BlockSpec((tm, width), lambda i: (i, 0))

    wnames = ("g2", "w_out", "w_gate_ff", "w_up_ff", "w_down")
    in_specs = [tok(D_MODEL), tok(256), tok(256), tok(256), tok(256),
                mod.spec(lambda i: i // npt)] + [lw.spec(nm) for nm in wnames]
    args = [x, *ys, mod.table] + [lw[nm] for nm in wnames]
    if final:
        in_specs.append(_resident((1, D_MODEL)))
        args.append(gf)
    return pl.pallas_call(
        functools.partial(_mix_ffn_kernel, final=final),
        grid=(T // tm,),
        in_specs=in_specs,
        out_specs=tok(D_MODEL),
        out_shape=jax.ShapeDtypeStruct((T, D_MODEL), F32),
        compiler_params=_params("arbitrary"),
        name="mix_ffn_final" if final else "mix_ffn",
    )(*args)


def _block_diag(w):
    L, G, c, e = w.shape
    return jnp.einsum('lgce,gh->lgche', w, jnp.eye(G, dtype=w.dtype)).reshape(L, G * c, G * e)


def _rot_cols(w):
    return jnp.concatenate([-w[..., 16:32], w[..., 0:16]], axis=-1)


def _stack_weights(p):
    w_in = p["w_in"]
    o1 = MLA_Q_RANK
    o2 = o1 + MLA_KV_RANK
    o3 = o2 + MLA_ROPE
    c_q, c_kv, k_r, rest = w_in[..., :o1], w_in[..., o1:o2], w_in[..., o2:o3], w_in[..., o3:]
    z = lambda n: jnp.zeros((DEPTH, D_MODEL, n), F32)
    w_in_eff = jnp.concatenate([c_q, k_r, z(32), c_kv, z(64), _rot_cols(k_r), z(32), rest], axis=-1)

    w_uq = p["mla_w_uq"]
    qd = MLA_NOPE + MLA_ROPE
    wq_parts, wqr_parts = [], []
    zq = lambda n: jnp.zeros((DEPTH, MLA_Q_RANK, n), F32)
    for h in range(MLA_HEADS):
        wh = w_uq[..., h * qd:(h + 1) * qd]
        wq_parts += [wh, zq(MLA_SLOT - qd)]
        wqr_parts += [zq(MLA_NOPE), _rot_cols(wh[..., MLA_NOPE:]), zq(MLA_SLOT - qd)]
    pad_rows = lambda w: jnp.pad(w, ((0, 0), (0, 256 - MLA_Q_RANK), (0, 0)))
    w_ukv = p["mla_w_ukv"]
    wk_parts, wv_parts = [], []
    zk = jnp.zeros((DEPTH, MLA_KV_RANK, MLA_SLOT - MLA_NOPE), F32)
    for h in range(MLA_HEADS):
        base = h * (MLA_NOPE + MLA_V)
        wk_parts += [w_ukv[..., base:base + MLA_NOPE], zk]
        wv_parts.append(w_ukv[..., base + MLA_NOPE:base + MLA_NOPE + MLA_V])

    w_r, w_i, b_r, b_i = p["lru_w_r"], p["lru_w_i"], p["lru_b_r"], p["lru_b_i"]
    w_gate = jnp.concatenate([_block_diag(w_r[:, 0]), _block_diag(w_r[:, 1]),
                              _block_diag(w_i[:, 0]), _block_diag(w_i[:, 1])], axis=-1)
    b_gate = jnp.concatenate([b_r[:, 0], b_r[:, 1], b_i[:, 0], b_i[:, 1]], axis=-1)
    w_gu = p["w_gu"]
    row = lambda v: v[:, None, :]
    return {
        "g1": row(p["norm1_g"]),
        "g2": row(p["norm2_g"]),
        "w_in": w_in_eff.astype(BF16),
        "gq": row(jnp.pad(p["mla_q_norm_g"], ((0, 0), (0, 256 - MLA_Q_RANK)))),
        "gkv": row(p["mla_kv_norm_g"]),
        "wq": pad_rows(jnp.concatenate(wq_parts, axis=-1)).astype(BF16),
        "wqr": pad_rows(jnp.concatenate(wqr_parts, axis=-1)).astype(BF16),
        "wk": jnp.concatenate(wk_parts, axis=-1).astype(BF16),
        "wv": jnp.concatenate(wv_parts, axis=-1).astype(BF16),
        "conv_w": p["lru_conv_w"],
        "conv_b": row(p["lru_conv_b"]),
        "w_gate": w_gate.astype(BF16),
        "b_gate": row(b_gate),
        "lru_lambda": p["lru_lambda"],
        "w_pool": _block_diag(p["pool_w"]).astype(BF16),
        "pool_scale": row(p["pool_scale"]),
        "diff_lambda": p["diff_lambda"],
        "diff_g": row(jnp.tile(p["diff_norm_g"], (1, DIFF_HEADS))),
        "w_out": p["w_out"].astype(BF16),
        "w_gate_ff": w_gu[..., :FF_HIDDEN].astype(BF16),
        "w_up_ff": w_gu[..., FF_HIDDEN:].astype(BF16),
        "w_down": p["w_down"].astype(BF16),
    }


def _rope_tables(n, positional):
    quarter = MLA_ROPE // 4
    if positional:
        t = jnp.arange(n)
        row = (t // GRID_W).astype(F32)
        col = (t % GRID_W).astype(F32)
        inv = ROPE_BASE ** (-jnp.arange(quarter, dtype=F32) / quarter)
        ang = jnp.concatenate([row[:, None] * inv, col[:, None] * inv], axis=-1)
        cos, sin = jnp.cos(ang), jnp.sin(ang)
    else:
        cos, sin = jnp.ones((n, 16), F32), jnp.zeros((n, 16), F32)
    one = lambda w: jnp.ones((n, w), F32)
    zero = lambda w: jnp.zeros((n, w), F32)
    scale = LOG2E / math.sqrt(MLA_NOPE + MLA_ROPE)
    return {
        "cosq": jnp.concatenate([one(64), cos, cos, one(32)], axis=1) * scale,
        "sinq": jnp.concatenate([zero(64), sin, sin, zero(32)], axis=1) * scale,
        "cosk": jnp.concatenate([zero(64), cos, cos, zero(32)], axis=1),
        "sink": jnp.concatenate([zero(64), sin, sin, zero(32)], axis=1),
        "cosd": jnp.tile(jnp.concatenate([cos, cos], axis=1), (1, 8)),
        "sina": jnp.tile(jnp.concatenate([-sin, zero(16)], axis=1), (1, 8)),
        "sinb": jnp.tile(jnp.concatenate([zero(16), sin], axis=1), (1, 8)),
    }


def _layer(x, mod, lw, tabs, layer_idx, ctx, gf, *, nb, n, final):
    emit_cache = ctx is None
    tok_nb, tok_n = (1, nb * n) if mod.shared else (nb, n)
    outs = _inproj(x, mod, lw, tabs, nb=tok_nb, n=tok_n, emit_cache=emit_cache)
    q, k, vt, u_lru, u_pool, dq, dk, dvt = outs[:8]
    lam_init = 0.8 - 0.6 * math.exp(-0.3 * layer_idx)
    if ctx is None:
        h0 = jnp.zeros((1, 2, LRU_WIDTH), F32)
        h0_block = lambda b: 0
        mla_ctx = diff_ctx = None
    else:
        ckv, kr_pad, cdk, cdv, h0 = ctx
        p = ckv.shape[0] // (nb * DEPTH)
        h0_block = lambda b: b * DEPTH + layer_idx
        kc, vtc, dkc, dvtc = _ctx_prep(ckv, kr_pad, cdk, cdv, lw, nb=nb, p=p)
        mla_ctx = (kc, vtc)
        diff_ctx = (dkc, dvtc)
    y_mla = _mla_attn(q, k, vt, mla_ctx, nb=nb, n=n)
    y_lru, st = _lru(u_lru, h0, h0_block, lw, nb=nb, n=n)
    y_pool = _pool(u_pool, lw, nb=nb, n=n)
    y_diff = _diff_attn(dq, dk, dvt, diff_ctx, lw, nb=nb, n=n, lam_init=lam_init)
    x2 = _mix_ffn(x, (y_mla, y_lru, y_pool, y_diff), mod, lw, gf, nb=tok_nb, n=tok_n, final=final)
    cache = (outs[8], outs[9][:, 64:96], outs[10], outs[11], st) if emit_cache else None
    return x2, cache


def kernel(x_prompt, x_sample, cache_mla_ckv, cache_mla_krope, cache_diff_k, cache_diff_v, state_lru,
           c, c_ctx, w_ada, b_ada, norm1_g, norm2_g, w_in, mla_q_norm_g, mla_w_uq, mla_kv_norm_g,
           mla_w_ukv, lru_conv_w, lru_conv_b, lru_w_r, lru_b_r, lru_w_i, lru_b_i, lru_lambda, pool_w,
           pool_scale, diff_lambda, diff_norm_g, w_out, w_gu, w_down, final_norm_g):
    p = {
        "norm1_g": norm1_g, "norm2_g": norm2_g, "w_in": w_in, "mla_q_norm_g": mla_q_norm_g,
        "mla_w_uq": mla_w_uq, "mla_kv_norm_g": mla_kv_norm_g, "mla_w_ukv": mla_w_ukv,
        "lru_conv_w": lru_conv_w, "lru_conv_b": lru_conv_b, "lru_w_r": lru_w_r, "lru_b_r": lru_b_r,
        "lru_w_i": lru_w_i, "lru_b_i": lru_b_i, "lru_lambda": lru_lambda, "pool_w": pool_w,
        "pool_scale": pool_scale, "diff_lambda": diff_lambda, "diff_norm_g": diff_norm_g,
        "w_out": w_out, "w_gu": w_gu, "w_down": w_down,
    }
    Bp, Np, _ = x_prompt.shape
    Bs, Ns, _ = x_sample.shape
    P = cache_mla_ckv.shape[2]

    cond_all = jnp.concatenate([c, c_ctx[None, :], jnp.zeros((MOD_ROWS - Bs - 1, D_MODEL), F32)], axis=0)
    mod_table = _ada(cond_all, w_ada, b_ada).reshape(DEPTH * MOD_ROWS, 1, 6 * D_MODEL)
    tabs_p = _rope_tables(Bp * Np, positional=False)
    tabs_s = _rope_tables(Ns, positional=True)
    kr_pad = jnp.pad(cache_mla_krope, ((0, 0), (0, 0), (0, 0), (MLA_NOPE, MLA_SLOT - MLA_NOPE - MLA_ROPE)))
    flat = lambda a, w: a.reshape(Bs * DEPTH * P, w)
    ctx = (flat(cache_mla_ckv, MLA_KV_RANK), flat(kr_pad, MLA_SLOT), flat(cache_diff_k, 256),
           flat(cache_diff_v, 256), state_lru.reshape(Bs * DEPTH, 2, LRU_WIDTH))
    gf = final_norm_g[None, :]
    stacked = _stack_weights(p)

    xp = x_prompt.reshape(Bp * Np, D_MODEL)
    xs = x_sample.reshape(Bs * Ns, D_MODEL)
    caches = []
    for l in range(DEPTH):
        lw = _LayerWeights(stacked, l)
        final = l == DEPTH - 1
        mod_p = _Mod(mod_table, l * MOD_ROWS + Bs, shared=True)
        mod_s = _Mod(mod_table, l * MOD_ROWS, shared=False)
        xp, cache = _layer(xp, mod_p, lw, tabs_p, l, None, gf, nb=Bp, n=Np, final=final)
        caches.append(cache)
        xs, _ = _layer(xs, mod_s, lw, tabs_s, l, ctx, gf, nb=Bs, n=Ns, final=final)

    stack = lambda i, w: jnp.stack([cc[i].reshape(Bp, Np, w) for cc in caches], axis=1)
    new_mla_ckv = stack(0, MLA_KV_RANK)
    new_mla_krope = stack(1, MLA_ROPE)
    new_diff_k = stack(2, 256).reshape(Bp, DEPTH, Np, DIFF_HEADS, 2, DIFF_DIM)
    new_diff_v = stack(3, 256).reshape(Bp, DEPTH, Np, DIFF_HEADS, 2 * DIFF_DIM)
    new_state_lru = jnp.stack([cc[4] for cc in caches], axis=1)
    return (xp.reshape(Bp, Np, D_MODEL), xs.reshape(Bs, Ns, D_MODEL),
            new_mla_ckv, new_mla_krope, new_diff_k, new_diff_v, new_state_lru)
```

```python
import functools
import math

import jax
import jax.numpy as jnp
from jax import lax
from jax.experimental import pallas as pl
from jax.experimental.pallas import tpu as pltpu

F32 = jnp.float32
BF16 = jnp.bfloat16

D_MODEL = 1024
DEPTH = 2
GRID_W = 64
GROUP_WIDTH = 256
MLA_HEADS = 4
MLA_NOPE = 64
MLA_ROPE = 32
MLA_V = 64
MLA_Q_RANK = 192
MLA_KV_RANK = 128
MLA_SLOT = 128
LRU_WIDTH = 256
LRU_C = 8.0
POOL_WINDOWS = (2, 4, 8, 16)
POOL_CH = 64
DIFF_HEADS = 4
DIFF_DIM = 32
HEAD_V = 64
FF_HIDDEN = 2816
FF_CHUNKS = ((0, 1536), (1536, 2816))
ROPE_BASE = 10000.0
EPS = 1e-6
IN_EFF = 2048
HALO = 8
VT_ROWS = 80
ATT_TQ = 256
TOKEN_TILE = 512
SUB_ROWS = 256
MOD_ROWS = 16
LOG2E = math.log2(math.e)

VMEM_LIMIT_BYTES = 56 * 1024 * 1024

_NT = (((1,), (1,)), ((), ()))


def _params(*sem):
    return pltpu.CompilerParams(dimension_semantics=sem, vmem_limit_bytes=VMEM_LIMIT_BYTES)


def _resident(shape):
    zeros = (0,) * len(shape)
    return pl.BlockSpec(shape, lambda *_: zeros, pipeline_mode=pl.Buffered(1))


def _dot(a, b):
    return jnp.dot(a, b, preferred_element_type=F32)


def _dot_nt(a, b):
    return lax.dot_general(a, b, _NT, preferred_element_type=F32)


def _rms_rows(x, width):
    ms = jnp.sum(x * x, axis=-1, keepdims=True) * (1.0 / width)
    return x * lax.rsqrt(ms + EPS)


def _store_vt(vt_ref, cols, v):
    vt = v.T
    rows = v.shape[0]
    pad = VT_ROWS - HEAD_V
    ones_row = jnp.where(lax.broadcasted_iota(jnp.int32, (pad, rows), 0) == 0, 1.0, 0.0).astype(BF16)
    for hh in range(vt_ref.shape[0]):
        vt_ref[hh, 0:HEAD_V, cols] = vt[hh * HEAD_V:(hh + 1) * HEAD_V, :].astype(BF16)
        vt_ref[hh, HEAD_V:VT_ROWS, cols] = ones_row


class _Mod:
    def __init__(self, table, row0, shared):
        self.table, self.row0, self.shared = table, row0, shared

    def spec(self, batch_of):
        row0 = self.row0
        if self.shared:
            return pl.BlockSpec((1, 1, 6 * D_MODEL), lambda *g: (row0, 0, 0))
        return pl.BlockSpec((1, 1, 6 * D_MODEL), lambda *g: (row0 + batch_of(*g), 0, 0))


class _LayerWeights:
    def __init__(self, stacked, layer):
        self.stacked, self.layer = stacked, layer

    def __getitem__(self, name):
        return self.stacked[name]

    def spec(self, name):
        layer = self.layer
        _, rows, cols = self.stacked[name].shape
        return pl.BlockSpec((None, rows, cols), lambda *_: (layer, 0, 0), pipeline_mode=pl.Buffered(1))


def _ada_kernel(cond_ref, w_ref, b_ref, out_ref):
    c = cond_ref[...]
    s = c * jax.nn.sigmoid(c)
    out_ref[0] = _dot(s.astype(BF16), w_ref[0].astype(BF16)) + b_ref[0]


def _ada(cond_all, w_ada, b_ada):
    rows = cond_all.shape[0]
    tn = 1536
    return pl.pallas_call(
        _ada_kernel,
        grid=(DEPTH, 6 * D_MODEL // tn),
        in_specs=[
            pl.BlockSpec((rows, D_MODEL), lambda l, j: (0, 0)),
            pl.BlockSpec((1, D_MODEL, tn), lambda l, j: (l, 0, j)),
            pl.BlockSpec((1, 1, tn), lambda l, j: (l, 0, j)),
        ],
        out_specs=pl.BlockSpec((1, rows, tn), lambda l, j: (l, 0, j)),
        out_shape=jax.ShapeDtypeStruct((DEPTH, rows, 6 * D_MODEL), F32),
        compiler_params=_params("arbitrary", "arbitrary"),
        name="ada_mod",
    )(cond_all, w_ada, b_ada.reshape(DEPTH, 1, 6 * D_MODEL))


def _inproj_kernel(x_ref, mod_ref, g1_ref, win_ref, gq_ref, gkv_ref, wq_ref, wqr_ref, wk_ref, wv_ref,
                   cosq_ref, sinq_ref, cosk_ref, sink_ref, cosd_ref, sina_ref, sinb_ref,
                   q_out, k_out, vt_out, lru_out, pool_out, dq_out, dk_out, dvt_out, *cache_outs):
    mod = mod_ref[0]
    sh1 = mod[:, 0:D_MODEL]
    sc1 = mod[:, D_MODEL:2 * D_MODEL]
    lane = lax.broadcasted_iota(jnp.int32, (1, 256), 1)
    for r0 in range(0, x_ref.shape[0], SUB_ROWS):
        rows = slice(r0, r0 + SUB_ROWS)
        h = _rms_rows(x_ref[rows, :], D_MODEL) * g1_ref[...]
        hb = (h * (1.0 + sc1) + sh1).astype(BF16)

        t01 = _dot(hb, win_ref[:, 0:256])
        cq = jnp.where(lane < MLA_Q_RANK, t01, 0.0)
        cqn = (_rms_rows(cq, MLA_Q_RANK) * gq_ref[...]).astype(BF16)
        qa = _dot(cqn, wq_ref[...])
        qr = _dot(cqn, wqr_ref[...])
        cosq = cosq_ref[rows, :]
        sinq = sinq_ref[rows, :]
        ckv = _dot(hb, win_ref[:, 256:384])
        lat = _rms_rows(ckv, MLA_KV_RANK) * gkv_ref[...]
        latb = lat.astype(BF16)
        kk = _dot(latb, wk_ref[...])
        _store_vt(vt_out, rows, _dot(latb, wv_ref[...]))
        t1 = t01[:, 128:256]
        t3 = _dot(hb, win_ref[:, 384:512])
        kro = t1 * cosk_ref[rows, :] + t3 * sink_ref[rows, :]
        for hh in range(MLA_HEADS):
            sl = slice(hh * MLA_SLOT, (hh + 1) * MLA_SLOT)
            q_out[hh, rows, :] = (qa[:, sl] * cosq + qr[:, sl] * sinq).astype(q_out.dtype)
            k_out[hh, rows, :] = (kk[:, sl] + kro).astype(k_out.dtype)

        lru_out[rows, :] = _dot(hb, win_ref[:, 512:1024])
        pool_out[rows, :] = _dot(hb, win_ref[:, 1024:1280])

        cosd = cosd_ref[rows, :]
        sina = sina_ref[rows, :]
        sinb = sinb_ref[rows, :]

        def rope(t):
            return t * cosd + pltpu.roll(t, 256 - 16, 1) * sina + pltpu.roll(t, 16, 1) * sinb

        dq = _dot(hb, win_ref[:, 1280:1536])
        dk = _dot(hb, win_ref[:, 1536:1792])
        dv = _dot(hb, win_ref[:, 1792:2048])
        dq_out[rows, :] = (rope(dq) * (LOG2E / math.sqrt(DIFF_DIM))).astype(dq_out.dtype)
        dk_out[rows, :] = rope(dk).astype(dk_out.dtype)
        _store_vt(dvt_out, rows, dv)

        if cache_outs:
            lat_out, kr_out, dk_raw_out, dv_raw_out = cache_outs
            lat_out[rows, :] = lat
            kr_out[rows, :] = t1
            dk_raw_out[rows, :] = dk
            dv_raw_out[rows, :] = dv


def _inproj(x, mod, lw, tabs, *, nb, n, emit_cache):
    T = nb * n
    tm = TOKEN_TILE
    npt = n // tm
    row_blk = lambda j, b: b * npt + j

    def tok(width):
        return pl.BlockSpec((tm, width), lambda j, b: (row_blk(j, b), 0))

    def tab(width):
        return pl.BlockSpec((tm, width), lambda j, b: (j, 0))

    head = pl.BlockSpec((MLA_HEADS, tm, MLA_SLOT), lambda j, b: (0, row_blk(j, b), 0))
    vt_spec = pl.BlockSpec((MLA_HEADS, VT_ROWS, tm), lambda j, b: (0, 0, row_blk(j, b)))
    wnames = ("g1", "w_in", "gq", "gkv", "wq", "wqr", "wk", "wv")
    in_specs = [tok(D_MODEL), mod.spec(lambda j, b: b)] + [lw.spec(nm) for nm in wnames] + [
        tab(128), tab(128), tab(128), tab(128), tab(256), tab(256), tab(256)]
    out_specs = [head, head, vt_spec, tok(512), tok(256), tok(256), tok(256), vt_spec]
    vt_shape = jax.ShapeDtypeStruct((MLA_HEADS, VT_ROWS, T), BF16)
    out_shape = [
        jax.ShapeDtypeStruct((MLA_HEADS, T, MLA_SLOT), BF16),
        jax.ShapeDtypeStruct((MLA_HEADS, T, MLA_SLOT), BF16),
        vt_shape,
        jax.ShapeDtypeStruct((T, 512), F32),
        jax.ShapeDtypeStruct((T, 256), F32),
        jax.ShapeDtypeStruct((T, 256), BF16),
        jax.ShapeDtypeStruct((T, 256), BF16),
        vt_shape,
    ]
    if emit_cache:
        out_specs += [tok(128), tok(128), tok(256), tok(256)]
        out_shape += [jax.ShapeDtypeStruct((T, 128), F32), jax.ShapeDtypeStruct((T, 128), F32),
                      jax.ShapeDtypeStruct((T, 256), F32), jax.ShapeDtypeStruct((T, 256), F32)]
    return pl.pallas_call(
        _inproj_kernel,
        grid=(npt, nb),
        in_specs=in_specs,
        out_specs=out_specs,
        out_shape=out_shape,
        compiler_params=_params("arbitrary", "arbitrary"),
        name="inproj_cache" if emit_cache else "inproj",
    )(x, mod.table, *[lw[nm] for nm in wnames],
      tabs["cosq"], tabs["sinq"], tabs["cosk"], tabs["sink"], tabs["cosd"], tabs["sina"], tabs["sinb"])


def _ctx_prep_kernel(ckv_ref, kr_ref, dk_ref, dv_ref, wk_ref, wv_ref, k_out, vt_out, dk_out, dvt_out):
    latb = ckv_ref[...].astype(BF16)
    kk = _dot(latb, wk_ref[...])
    kr = kr_ref[...]
    for hh in range(MLA_HEADS):
        k_out[hh] = (kk[:, hh * MLA_SLOT:(hh + 1) * MLA_SLOT] + kr).astype(k_out.dtype)
    _store_vt(vt_out, slice(None), _dot(latb, wv_ref[...]))
    dk_out[...] = dk_ref[...].astype(dk_out.dtype)
    _store_vt(dvt_out, slice(None), dv_ref[...])


def _ctx_prep(ckv, kr_pad, cdk, cdv, lw, *, nb, p):
    T = nb * p
    layer = lw.layer
    cache_row = lambda w: pl.BlockSpec((p, w), lambda b: (b * DEPTH + layer, 0))
    row = lambda w: pl.BlockSpec((p, w), lambda b: (b, 0))
    vt_spec = pl.BlockSpec((MLA_HEADS, VT_ROWS, p), lambda b: (0, 0, b))
    vt_shape = jax.ShapeDtypeStruct((MLA_HEADS, VT_ROWS, T), BF16)
    return pl.pallas_call(
        _ctx_prep_kernel,
        grid=(nb,),
        in_specs=[cache_row(128), cache_row(128), cache_row(256), cache_row(256),
                  lw.spec("wk"), lw.spec("wv")],
        out_specs=[pl.BlockSpec((MLA_HEADS, p, MLA_SLOT), lambda b: (0, b, 0)), vt_spec, row(256), vt_spec],
        out_shape=[jax.ShapeDtypeStruct((MLA_HEADS, T, MLA_SLOT), BF16), vt_shape,
                   jax.ShapeDtypeStruct((T, 256), BF16), vt_shape],
        compiler_params=_params("arbitrary"),
        name="ctx_prep",
    )(ckv, kr_pad, cdk, cdv, lw["wk"], lw["wv"])


SAFE_DENOM = 2.0 ** -60
BOUND_SLACK = 1.02


def _scores(k_new, k_ctx, q):
    sn = _dot_nt(k_new(), q)
    sc = _dot_nt(k_ctx(), q) if k_ctx is not None else None
    return sn, sc


def _exact_shift(k_new, k_ctx, q):
    sn, sc = _scores(k_new, k_ctx, q)
    m = jnp.max(sn, axis=0, keepdims=True)
    if sc is not None:
        m = jnp.maximum(m, jnp.max(sc, axis=0, keepdims=True))
    return m


def _bound_shift(q, key_norm2):
    qf = q.astype(F32)
    ones = jnp.ones((8, q.shape[1]), BF16)
    q_norm2 = _dot_nt(ones, (qf * qf).astype(BF16))[0:1, :]
    return jnp.sqrt(q_norm2 * key_norm2) * BOUND_SLACK


def _max_row_norm2(k_new, k_ctx, col_sum):
    def one(k):
        kf = k.astype(F32)
        return jnp.max(_dot((kf * kf).astype(BF16), col_sum), axis=0, keepdims=True)
    m = one(k_new)
    if k_ctx is not None:
        m = jnp.maximum(m, one(k_ctx))
    return m * BOUND_SLACK


def _exp_stage(e_buf, k_new, k_ctx, q, shift, n_ctx):
    sn, sc = _scores(k_new, k_ctx, q)
    e_buf[n_ctx:, :] = jnp.exp2(sn - shift).astype(BF16)
    if sc is not None:
        e_buf[0:n_ctx, :] = jnp.exp2(sc - shift).astype(BF16)


def _value_stage(e_buf, vt_new, vt_ctx, n_ctx):
    o = _dot(vt_new(), e_buf[n_ctx:, :])
    if vt_ctx is not None:
        o = o + _dot(vt_ctx(), e_buf[0:n_ctx, :])
    return o


def _run_pipeline(n_maps, exp_stage, value_stage):
    exp_stage(0)
    for u in range(n_maps):
        if u + 1 < n_maps:
            exp_stage(u + 1)
        value_stage(u)


def _att_scratch(nk):
    return [pltpu.VMEM((8, 128), F32),
            pltpu.VMEM((MLA_HEADS * HEAD_V, ATT_TQ), F32),
            pltpu.VMEM((nk, ATT_TQ), BF16), pltpu.VMEM((nk, ATT_TQ), BF16)]


def _att_nsub(n):
    for nsub in (4, 2):
        if n % (nsub * ATT_TQ) == 0:
            return nsub
    return 1


def _mla_attn_kernel(*refs, has_ctx, nsub):
    if has_ctx:
        q_ref, k_ref, vt_ref, kc_ref, vtc_ref, o_ref, kn2, ot, e0, e1 = refs
        n_ctx = kc_ref.shape[1]
    else:
        q_ref, k_ref, vt_ref, o_ref, kn2, ot, e0, e1 = refs
        n_ctx = 0
    e_bufs = (e0, e1)

    @pl.when(pl.program_id(1) == 0)
    def _():
        ones = jnp.ones((MLA_SLOT, 128), BF16)
        for hh in range(MLA_HEADS):
            kn2[hh:hh + 1, :] = _max_row_norm2(k_ref[hh], kc_ref[hh] if has_ctx else None, ones)

    def run(exact):
        denoms = []

        def exp_stage(u):
            t, hh = divmod(u, MLA_HEADS)
            q = q_ref[hh, t * ATT_TQ:(t + 1) * ATT_TQ, :]
            k_new = lambda: k_ref[hh]
            k_ctx = (lambda: kc_ref[hh]) if has_ctx else None
            shift = _exact_shift(k_new, k_ctx, q) if exact else _bound_shift(q, kn2[hh:hh + 1, 0:1])
            _exp_stage(e_bufs[u % 2], k_new, k_ctx, q, shift, n_ctx)

        def value_stage(u):
            t, hh = divmod(u, MLA_HEADS)
            o = _value_stage(e_bufs[u % 2], lambda: vt_ref[hh],
                             (lambda: vtc_ref[hh]) if has_ctx else None, n_ctx)
            denom = o[HEAD_V:HEAD_V + 1, :]
            denoms.append(denom)
            ot[hh * HEAD_V:(hh + 1) * HEAD_V, :] = o[0:HEAD_V, :] * (1.0 / denom)
            if hh == MLA_HEADS - 1:
                o_ref[t * ATT_TQ:(t + 1) * ATT_TQ, :] = ot[...].T

        _run_pipeline(nsub * MLA_HEADS, exp_stage, value_stage)
        return jnp.min(functools.reduce(jnp.minimum, denoms))

    denom_min = run(exact=False)

    @pl.when(jnp.logical_not(denom_min >= SAFE_DENOM))
    def _():
        run(exact=True)


def _mla_attn(q, k, vt, ctx, *, nb, n):
    nsub = _att_nsub(n)
    tq = nsub * ATT_TQ
    npt = n // tq
    H, S = MLA_HEADS, MLA_SLOT
    in_specs = [
        pl.BlockSpec((H, tq, S), lambda b, j: (0, b * npt + j, 0)),
        pl.BlockSpec((H, n, S), lambda b, j: (0, b, 0)),
        pl.BlockSpec((H, VT_ROWS, n), lambda b, j: (0, 0, b)),
    ]
    args = [q, k, vt]
    n_ctx = 0
    if ctx is not None:
        n_ctx = ctx[0].shape[1] // nb
        in_specs += [
            pl.BlockSpec((H, n_ctx, S), lambda b, j: (0, b, 0)),
            pl.BlockSpec((H, VT_ROWS, n_ctx), lambda b, j: (0, 0, b)),
        ]
        args += list(ctx)
    return pl.pallas_call(
        functools.partial(_mla_attn_kernel, has_ctx=ctx is not None, nsub=nsub),
        grid=(nb, npt),
        in_specs=in_specs,
        out_specs=pl.BlockSpec((tq, 256), lambda b, j: (b * npt + j, 0)),
        out_shape=jax.ShapeDtypeStruct((nb * n, 256), F32),
        scratch_shapes=_att_scratch(n + n_ctx),
        compiler_params=_params("arbitrary", "arbitrary"),
        name="mla_attn_ctx" if ctx is not None else "mla_attn",
    )(*args)


def _diff_attn_kernel(*refs, has_ctx, nsub, lam_init):
    if has_ctx:
        lv_ref, g_ref, q_ref, k_ref, vt_ref, kc_ref, vtc_ref, o_ref, kn2, ot, e0, e1 = refs
        n_ctx = kc_ref.shape[0]
    else:
        lv_ref, g_ref, q_ref, k_ref, vt_ref, o_ref, kn2, ot, e0, e1 = refs
        n_ctx = 0
    e_bufs = (e0, e1)
    lv = lv_ref[...]
    lam = (jnp.exp(jnp.sum(lv[0:1] * lv[1:2], axis=-1, keepdims=True))
           - jnp.exp(jnp.sum(lv[2:3] * lv[3:4], axis=-1, keepdims=True)) + lam_init)
    lane128 = lax.broadcasted_iota(jnp.int32, (1, 128), 1)
    n_pairs = 2 * DIFF_HEADS

    @pl.when(pl.program_id(1) == 0)
    def _():
        dim = lax.broadcasted_iota(jnp.int32, (256, 128), 0)
        col = lax.broadcasted_iota(jnp.int32, (256, 128), 1)
        indicator = jnp.where(dim // DIFF_DIM == col, 1.0, 0.0).astype(BF16)
        kn2[0:1, :] = _max_row_norm2(k_ref[...], kc_ref[...] if has_ctx else None, indicator)

    def run(exact):
        denoms = []
        outs = {}

        def exp_stage(u):
            t, p = divmod(u, n_pairs)
            tile = slice((p * DIFF_DIM // 128) * 128, (p * DIFF_DIM // 128 + 1) * 128)
            k_new = lambda: k_ref[:, tile]
            k_ctx = (lambda: kc_ref[:, tile]) if has_ctx else None
            q = q_ref[t * ATT_TQ:(t + 1) * ATT_TQ, tile]
            lo = p * DIFF_DIM - tile.start
            in_pair = (lane128 >= lo) & (lane128 < lo + DIFF_DIM)
            qm = jnp.where(in_pair, q, jnp.zeros_like(q))
            shift = _exact_shift(k_new, k_ctx, qm) if exact else _bound_shift(qm, kn2[0:1, p:p + 1])
            _exp_stage(e_bufs[u % 2], k_new, k_ctx, qm, shift, n_ctx)

        def value_stage(u):
            t, p = divmod(u, n_pairs)
            hh = p // 2
            o = _value_stage(e_bufs[u % 2], lambda: vt_ref[hh],
                             (lambda: vtc_ref[hh]) if has_ctx else None, n_ctx)
            denom = o[HEAD_V:HEAD_V + 1, :]
            denoms.append(denom)
            outs[u] = (o[0:HEAD_V, :], denom)
            if p % 2 == 1:
                (o0, l0), (o1, l1) = outs.pop(u - 1), outs.pop(u)
                o = o0 * (1.0 / l0) - o1 * (lam / l1)
                msq = jnp.sum(o * o, axis=0, keepdims=True) * (1.0 / HEAD_V)
                ot[hh * HEAD_V:(hh + 1) * HEAD_V, :] = o * lax.rsqrt(msq + EPS)
            if p == n_pairs - 1:
                o_ref[t * ATT_TQ:(t + 1) * ATT_TQ, :] = (ot[...].T * g_ref[...]) * (1.0 - lam_init)

        _run_pipeline(nsub * n_pairs, exp_stage, value_stage)
        return jnp.min(functools.reduce(jnp.minimum, denoms))

    denom_min = run(exact=False)

    @pl.when(jnp.logical_not(denom_min >= SAFE_DENOM))
    def _():
        run(exact=True)


def _diff_attn(q, k, vt, ctx, lw, *, nb, n, lam_init):
    nsub = _att_nsub(n)
    tq = nsub * ATT_TQ
    npt = n // tq
    in_specs = [
        lw.spec("diff_lambda"),
        lw.spec("diff_g"),
        pl.BlockSpec((tq, 256), lambda b, j: (b * npt + j, 0)),
        pl.BlockSpec((n, 256), lambda b, j: (b, 0)),
        pl.BlockSpec((DIFF_HEADS, VT_ROWS, n), lambda b, j: (0, 0, b)),
    ]
    args = [lw["diff_lambda"], lw["diff_g"], q, k, vt]
    n_ctx = 0
    if ctx is not None:
        n_ctx = ctx[0].shape[0] // nb
        in_specs += [pl.BlockSpec((n_ctx, 256), lambda b, j: (b, 0)),
                     pl.BlockSpec((DIFF_HEADS, VT_ROWS, n_ctx), lambda b, j: (0, 0, b))]
        args += list(ctx)
    return pl.pallas_call(
        functools.partial(_diff_attn_kernel, has_ctx=ctx is not None, nsub=nsub, lam_init=lam_init),
        grid=(nb, npt),
        in_specs=in_specs,
        out_specs=pl.BlockSpec((tq, 256), lambda b, j: (b * npt + j, 0)),
        out_shape=jax.ShapeDtypeStruct((nb * n, 256), F32),
        scratch_shapes=_att_scratch(n + n_ctx),
        compiler_params=_params("arbitrary", "arbitrary"),
        name="diff_attn_ctx" if ctx is not None else "diff_attn",
    )(*args)


def _shift_rows(v, k):
    return pltpu.roll(v, (-k) % v.shape[0], 0)


def _scan_chunk(a, b, reverse):
    T = a.shape[0]
    row = lax.broadcasted_iota(jnp.int32, a.shape, 0)
    s = 1
    while s < T:
        if reverse:
            valid = row < T - s
            ap, bp = _shift_rows(a, s), _shift_rows(b, s)
        else:
            valid = row >= s
            ap, bp = _shift_rows(a, -s), _shift_rows(b, -s)
        b = jnp.where(valid, a * bp + b, b)
        a = jnp.where(valid, a * ap, a)
        s *= 2
    return a, b


def _sigmoid(x):
    return 0.5 * jnp.tanh(0.5 * x) + 0.5


def _gelu_tanh(x):
    return x * (0.5 * (1.0 + jnp.tanh(math.sqrt(2.0 / math.pi) * (x + 0.044715 * (x * x * x)))))


def _lru_kernel(u_ref, h0_ref, cw_ref, cb_ref, wg_ref, bg_ref, lam_ref, y_ref, st_ref,
                xpad, a1s, b1s, *, N, T):
    W = LRU_WIDTH
    nc = N // T
    zeros = jnp.zeros((HALO, W), F32)
    xpad[0:HALO, :] = zeros
    xpad[N + HALO:N + 2 * HALO, :] = zeros

    def fill(j, carry):
        r0 = pl.multiple_of(j * T, T)
        xpad[pl.ds(r0 + HALO, T), :] = u_ref[pl.ds(r0, T), 0:W]
        return carry

    lax.fori_loop(0, nc, fill, 0)

    z = -lam_ref[...]
    sp = jnp.maximum(z, 0.0) + jnp.log1p(jnp.exp(-jnp.abs(z)))
    cw = cw_ref[...]
    cb = cb_ref[...]
    bg = bg_ref[...]

    def fwd(j, carry):
        r0 = pl.multiple_of(j * T, T)
        ext = xpad[pl.ds(r0, T + 2 * HALO), :]
        body = slice(HALO, HALO + T)
        xc = cb
        for tap in range(4):
            xc = xc + _shift_rows(ext, tap - 1)[body] * cw[tap:tap + 1]
        g = _sigmoid(_dot(xc.astype(BF16), wg_ref[...]) + bg)
        ab = []
        for d in range(2):
            r = g[:, d * W:(d + 1) * W]
            i = g[:, (2 + d) * W:(3 + d) * W]
            log_a = (-LRU_C * r) * sp[d:d + 1]
            a = jnp.exp(log_a)
            bt = (jnp.sqrt(1.0 - a * a) * i) * xc
            ab.append((a, bt))
        a1s[pl.ds(r0, T), :] = ab[1][0]
        b1s[pl.ds(r0, T), :] = ab[1][1]
        A, Bv = _scan_chunk(ab[0][0], ab[0][1], reverse=False)
        h = A * carry + Bv
        y_ref[pl.ds(r0, T), :] = h
        return h[T - 1:T, :]

    cf = lax.fori_loop(0, nc, fwd, h0_ref[0, 0:1, :])

    def bwd(jj, carry):
        r0 = pl.multiple_of((nc - 1 - jj) * T, T)
        A, Bv = _scan_chunk(a1s[pl.ds(r0, T), :], b1s[pl.ds(r0, T), :], reverse=True)
        h = A * carry + Bv
        gb = u_ref[pl.ds(r0, T), W:2 * W]
        y_ref[pl.ds(r0, T), :] = (y_ref[pl.ds(r0, T), :] + h) * _gelu_tanh(gb)
        return h[0:1, :]

    cbw = lax.fori_loop(0, nc, bwd, h0_ref[0, 1:2, :])
    st_ref[0, 0:1, :] = cf
    st_ref[0, 1:2, :] = cbw


def _lru(u, h0, h0_block, lw, *, nb, n):
    T = min(n, 256)
    W = LRU_WIDTH
    return pl.pallas_call(
        functools.partial(_lru_kernel, N=n, T=T),
        grid=(nb,),
        in_specs=[
            pl.BlockSpec((n, 2 * W), lambda b: (b, 0)),
            pl.BlockSpec((1, 2, W), lambda b: (h0_block(b), 0, 0)),
            lw.spec("conv_w"), lw.spec("conv_b"), lw.spec("w_gate"), lw.spec("b_gate"),
            lw.spec("lru_lambda"),
        ],
        out_specs=[
            pl.BlockSpec((n, W), lambda b: (b, 0)),
            pl.BlockSpec((1, 2, W), lambda b: (b, 0, 0)),
        ],
        out_shape=[
            jax.ShapeDtypeStruct((nb * n, W), F32),
            jax.ShapeDtypeStruct((nb, 2, W), F32),
        ],
        scratch_shapes=[
            pltpu.VMEM((n + 2 * HALO, W), F32),
            pltpu.VMEM((n, W), F32),
            pltpu.VMEM((n, W), F32),
        ],
        compiler_params=_params("arbitrary"),
        name="rglru",
    )(u, h0, lw["conv_w"], lw["conv_b"], lw["w_gate"], lw["b_gate"], lw["lru_lambda"])


def _pool_kernel(u_ref, wp_ref, sc_ref, y_ref, xpad, *, N, T):
    W = GROUP_WIDTH
    nc = N // T
    zeros = jnp.zeros((HALO, W), F32)
    xpad[0:HALO, :] = zeros
    xpad[N + HALO:N + 2 * HALO, :] = zeros

    def fill(j, carry):
        r0 = pl.multiple_of(j * T, T)
        xpad[pl.ds(r0 + HALO, T), :] = u_ref[pl.ds(r0, T), :]
        return carry

    lax.fori_loop(0, nc, fill, 0)

    grp = lax.broadcasted_iota(jnp.int32, (1, W), 1) // POOL_CH
    half = jnp.where(grp == 0, 1, jnp.where(grp == 1, 2, jnp.where(grp == 2, 4, 8)))
    scale = sc_ref[...]

    def chunk(j, carry):
        r0 = pl.multiple_of(j * T, T)
        ext = xpad[pl.ds(r0, T + 2 * HALO), :]
        w2 = _shift_rows(ext, -1) + ext
        w4 = _shift_rows(w2, -1) + _shift_rows(w2, 1)
        w8 = _shift_rows(w4, -2) + _shift_rows(w4, 2)
        w16 = _shift_rows(w8, -4) + _shift_rows(w8, 4)
        ws = jnp.where(grp == 0, w2, jnp.where(grp == 1, w4, jnp.where(grp == 2, w8, w16)))
        body = slice(HALO, HALO + T)
        t = r0 + lax.broadcasted_iota(jnp.int32, (T, W), 0)
        cnt = (jnp.minimum(t + half, N) - jnp.maximum(t - half, 0)).astype(F32)
        d = ws[body] / cnt - ext[body]
        y_ref[pl.ds(r0, T), :] = _dot(d.astype(BF16), wp_ref[...]) * scale
        return carry

    lax.fori_loop(0, nc, chunk, 0)


def _pool(u, lw, *, nb, n):
    W = GROUP_WIDTH
    T = min(n, 256)
    return pl.pallas_call(
        functools.partial(_pool_kernel, N=n, T=T),
        grid=(nb,),
        in_specs=[pl.BlockSpec((n, W), lambda b: (b, 0)), lw.spec("w_pool"), lw.spec("pool_scale")],
        out_specs=pl.BlockSpec((n, W), lambda b: (b, 0)),
        out_shape=jax.ShapeDtypeStruct((nb * n, W), F32),
        scratch_shapes=[pltpu.VMEM((n + 2 * HALO, W), F32)],
        compiler_params=_params("arbitrary"),
        name="pool_mixer",
    )(u, lw["w_pool"], lw["pool_scale"])


def _mix_ffn_kernel(*refs, final):
    if final:
        (x_ref, ya_ref, yb_ref, yc_ref, yd_ref, mod_ref, g2_ref, wo_ref, wg_ref, wu_ref, wd_ref,
         gf_ref, o_ref) = refs
    else:
        (x_ref, ya_ref, yb_ref, yc_ref, yd_ref, mod_ref, g2_ref, wo_ref, wg_ref, wu_ref, wd_ref,
         o_ref) = refs
    mod = mod_ref[0]
    gate1 = mod[:, 2 * D_MODEL:3 * D_MODEL]
    sh2 = mod[:, 3 * D_MODEL:4 * D_MODEL]
    sc2 = mod[:, 4 * D_MODEL:5 * D_MODEL]
    gate2 = mod[:, 5 * D_MODEL:6 * D_MODEL]
    mix = None
    for i, y_ref in enumerate((ya_ref, yb_ref, yc_ref, yd_ref)):
        part = _dot(y_ref[...].astype(BF16), wo_ref[i * GROUP_WIDTH:(i + 1) * GROUP_WIDTH, :])
        mix = part if mix is None else mix + part
    x1 = x_ref[...] + gate1 * mix
    h = _rms_rows(x1, D_MODEL) * g2_ref[...]
    hb = (h * (1.0 + sc2) + sh2).astype(BF16)
    ff = None
    for lo, hi in FF_CHUNKS:
        g = _dot(hb, wg_ref[:, lo:hi])
        up = _dot(hb, wu_ref[:, lo:hi])
        act = ((g * jax.nn.sigmoid(g)) * up).astype(BF16)
        part = _dot(act, wd_ref[lo:hi, :])
        ff = part if ff is None else ff + part
    x2 = x1 + gate2 * ff
    if final:
        x2 = _rms_rows(x2, D_MODEL) * gf_ref[...]
    o_ref[...] = x2


def _mix_ffn(x, ys, mod, lw, gf, *, nb, n, final):
    T = nb * n
    tm = TOKEN_TILE
    npt = n // tm

    def tok(width):
        return pl.BlockSpec((tm, width), lambda i: (i, 0))

    wnames = ("g2", "w_out", "w_gate_ff", "w_up_ff", "w_down")
    in_specs = [tok(D_MODEL), tok(256), tok(256), tok(256), tok(256),
                mod.spec(lambda i: i // npt)] + [lw.spec(nm) for nm in wnames]
    args = [x, *ys, mod.table] + [lw[nm] for nm in wnames]
    if final:
        in_specs.append(_resident((1, D_MODEL)))
        args.append(gf)
    return pl.pallas_call(
        functools.partial(_mix_ffn_kernel, final=final),
        grid=(T // tm,),
        in_specs=in_specs,
        out_specs=tok(D_MODEL),
        out_shape=jax.ShapeDtypeStruct((T, D_MODEL), F32),
        compiler_params=_params("arbitrary"),
        name="mix_ffn_final" if final else "mix_ffn",
    )(*args)


def _block_diag(w):
    L, G, c, e = w.shape
    return jnp.einsum('lgce,gh->lgche', w, jnp.eye(G, dtype=w.dtype)).reshape(L, G * c, G * e)


def _rot_cols(w):
    return jnp.concatenate([-w[..., 16:32], w[..., 0:16]], axis=-1)


def _stack_weights(p):
    w_in = p["w_in"]
    o1 = MLA_Q_RANK
    o2 = o1 + MLA_KV_RANK
    o3 = o2 + MLA_ROPE
    c_q, c_kv, k_r, rest = w_in[..., :o1], w_in[..., o1:o2], w_in[..., o2:o3], w_in[..., o3:]
    z = lambda n: jnp.zeros((DEPTH, D_MODEL, n), F32)
    w_in_eff = jnp.concatenate([c_q, k_r, z(32), c_kv, z(64), _rot_cols(k_r), z(32), rest], axis=-1)

    w_uq = p["mla_w_uq"]
    qd = MLA_NOPE + MLA_ROPE
    wq_parts, wqr_parts = [], []
    zq = lambda n: jnp.zeros((DEPTH, MLA_Q_RANK, n), F32)
    for h in range(MLA_HEADS):
        wh = w_uq[..., h * qd:(h + 1) * qd]
        wq_parts += [wh, zq(MLA_SLOT - qd)]
        wqr_parts += [zq(MLA_NOPE), _rot_cols(wh[..., MLA_NOPE:]), zq(MLA_SLOT - qd)]
    pad_rows = lambda w: jnp.pad(w, ((0, 0), (0, 256 - MLA_Q_RANK), (0, 0)))
    w_ukv = p["mla_w_ukv"]
    wk_parts, wv_parts = [], []
    zk = jnp.zeros((DEPTH, MLA_KV_RANK, MLA_SLOT - MLA_NOPE), F32)
    for h in range(MLA_HEADS):
        base = h * (MLA_NOPE + MLA_V)
        wk_parts += [w_ukv[..., base:base + MLA_NOPE], zk]
        wv_parts.append(w_ukv[..., base + MLA_NOPE:base + MLA_NOPE + MLA_V])

    w_r, w_i, b_r, b_i = p["lru_w_r"], p["lru_w_i"], p["lru_b_r"], p["lru_b_i"]
    w_gate = jnp.concatenate([_block_diag(w_r[:, 0]), _block_diag(w_r[:, 1]),
                              _block_diag(w_i[:, 0]), _block_diag(w_i[:, 1])], axis=-1)
    b_gate = jnp.concatenate([b_r[:, 0], b_r[:, 1], b_i[:, 0], b_i[:, 1]], axis=-1)
    w_gu = p["w_gu"]
    row = lambda v: v[:, None, :]
    return {
        "g1": row(p["norm1_g"]),
        "g2": row(p["norm2_g"]),
        "w_in": w_in_eff.astype(BF16),
        "gq": row(jnp.pad(p["mla_q_norm_g"], ((0, 0), (0, 256 - MLA_Q_RANK)))),
        "gkv": row(p["mla_kv_norm_g"]),
        "wq": pad_rows(jnp.concatenate(wq_parts, axis=-1)).astype(BF16),
        "wqr": pad_rows(jnp.concatenate(wqr_parts, axis=-1)).astype(BF16),
        "wk": jnp.concatenate(wk_parts, axis=-1).astype(BF16),
        "wv": jnp.concatenate(wv_parts, axis=-1).astype(BF16),
        "conv_w": p["lru_conv_w"],
        "conv_b": row(p["lru_conv_b"]),
        "w_gate": w_gate.astype(BF16),
        "b_gate": row(b_gate),
        "lru_lambda": p["lru_lambda"],
        "w_pool": _block_diag(p["pool_w"]).astype(BF16),
        "pool_scale": row(p["pool_scale"]),
        "diff_lambda": p["diff_lambda"],
        "diff_g": row(jnp.tile(p["diff_norm_g"], (1, DIFF_HEADS))),
        "w_out": p["w_out"].astype(BF16),
        "w_gate_ff": w_gu[..., :FF_HIDDEN].astype(BF16),
        "w_up_ff": w_gu[..., FF_HIDDEN:].astype(BF16),
        "w_down": p["w_down"].astype(BF16),
    }


def _rope_tables(n, positional):
    quarter = MLA_ROPE // 4
    if positional:
        t = jnp.arange(n)
        row = (t // GRID_W).astype(F32)
        col = (t % GRID_W).astype(F32)
        inv = ROPE_BASE ** (-jnp.arange(quarter, dtype=F32) / quarter)
        ang = jnp.concatenate([row[:, None] * inv, col[:, None] * inv], axis=-1)
        cos, sin = jnp.cos(ang), jnp.sin(ang)
    else:
        cos, sin = jnp.ones((n, 16), F32), jnp.zeros((n, 16), F32)
    one = lambda w: jnp.ones((n, w), F32)
    zero = lambda w: jnp.zeros((n, w), F32)
    scale = LOG2E / math.sqrt(MLA_NOPE + MLA_ROPE)
    return {
        "cosq": jnp.concatenate([one(64), cos, cos, one(32)], axis=1) * scale,
        "sinq": jnp.concatenate([zero(64), sin, sin, zero(32)], axis=1) * scale,
        "cosk": jnp.concatenate([zero(64), cos, cos, zero(32)], axis=1),
        "sink": jnp.concatenate([zero(64), sin, sin, zero(32)], axis=1),
        "cosd": jnp.tile(jnp.concatenate([cos, cos], axis=1), (1, 8)),
        "sina": jnp.tile(jnp.concatenate([-sin, zero(16)], axis=1), (1, 8)),
        "sinb": jnp.tile(jnp.concatenate([zero(16), sin], axis=1), (1, 8)),
    }


def _layer(x, mod, lw, tabs, layer_idx, ctx, gf, *, nb, n, final):
    emit_cache = ctx is None
    tok_nb, tok_n = (1, nb * n) if mod.shared else (nb, n)
    outs = _inproj(x, mod, lw, tabs, nb=tok_nb, n=tok_n, emit_cache=emit_cache)
    q, k, vt, u_lru, u_pool, dq, dk, dvt = outs[:8]
    lam_init = 0.8 - 0.6 * math.exp(-0.3 * layer_idx)
    if ctx is None:
        h0 = jnp.zeros((1, 2, LRU_WIDTH), F32)
        h0_block = lambda b: 0
        mla_ctx = diff_ctx = None
    else:
        ckv, kr_pad, cdk, cdv, h0 = ctx
        p = ckv.shape[0] // (nb * DEPTH)
        h0_block = lambda b: b * DEPTH + layer_idx
        kc, vtc, dkc, dvtc = _ctx_prep(ckv, kr_pad, cdk, cdv, lw, nb=nb, p=p)
        mla_ctx = (kc, vtc)
        diff_ctx = (dkc, dvtc)
    y_mla = _mla_attn(q, k, vt, mla_ctx, nb=nb, n=n)
    y_lru, st = _lru(u_lru, h0, h0_block, lw, nb=nb, n=n)
    y_pool = _pool(u_pool, lw, nb=nb, n=n)
    y_diff = _diff_attn(dq, dk, dvt, diff_ctx, lw, nb=nb, n=n, lam_init=lam_init)
    x2 = _mix_ffn(x, (y_mla, y_lru, y_pool, y_diff), mod, lw, gf, nb=tok_nb, n=tok_n, final=final)
    cache = (outs[8], outs[9][:, 64:96], outs[10], outs[11], st) if emit_cache else None
    return x2, cache


def kernel(x_prompt, x_sample, cache_mla_ckv, cache_mla_krope, cache_diff_k, cache_diff_v, state_lru,
           c, c_ctx, w_ada, b_ada, norm1_g, norm2_g, w_in, mla_q_norm_g, mla_w_uq, mla_kv_norm_g,
           mla_w_ukv, lru_conv_w, lru_conv_b, lru_w_r, lru_b_r, lru_w_i, lru_b_i, lru_lambda, pool_w,
           pool_scale, diff_lambda, diff_norm_g, w_out, w_gu, w_down, final_norm_g):
    p = {
        "norm1_g": norm1_g, "norm2_g": norm2_g, "w_in": w_in, "mla_q_norm_g": mla_q_norm_g,
        "mla_w_uq": mla_w_uq, "mla_kv_norm_g": mla_kv_norm_g, "mla_w_ukv": mla_w_ukv,
        "lru_conv_w": lru_conv_w, "lru_conv_b": lru_conv_b, "lru_w_r": lru_w_r, "lru_b_r": lru_b_r,
        "lru_w_i": lru_w_i, "lru_b_i": lru_b_i, "lru_lambda": lru_lambda, "pool_w": pool_w,
        "pool_scale": pool_scale, "diff_lambda": diff_lambda, "diff_norm_g": diff_norm_g,
        "w_out": w_out, "w_gu": w_gu, "w_down": w_down,
    }
    Bp, Np, _ = x_prompt.shape
    Bs, Ns, _ = x_sample.shape
    P = cache_mla_ckv.shape[2]

    cond_all = jnp.concatenate([c, c_ctx[None, :], jnp.zeros((MOD_ROWS - Bs - 1, D_MODEL), F32)], axis=0)
    mod_table = _ada(cond_all, w_ada, b_ada).reshape(DEPTH * MOD_ROWS, 1, 6 * D_MODEL)
    tabs_p = _rope_tables(Bp * Np, positional=False)
    tabs_s = _rope_tables(Ns, positional=True)
    kr_pad = jnp.pad(cache_mla_krope, ((0, 0), (0, 0), (0, 0), (MLA_NOPE, MLA_SLOT - MLA_NOPE - MLA_ROPE)))
    flat = lambda a, w: a.reshape(Bs * DEPTH * P, w)
    ctx = (flat(cache_mla_ckv, MLA_KV_RANK), flat(kr_pad, MLA_SLOT), flat(cache_diff_k, 256),
           flat(cache_diff_v, 256), state_lru.reshape(Bs * DEPTH, 2, LRU_WIDTH))
    gf = final_norm_g[None, :]
    stacked = _stack_weights(p)

    xp = x_prompt.reshape(Bp * Np, D_MODEL)
    xs = x_sample.reshape(Bs * Ns, D_MODEL)
    caches = []
    for l in range(DEPTH):
        lw = _LayerWeights(stacked, l)
        final = l == DEPTH - 1
        mod_p = _Mod(mod_table, l * MOD_ROWS + Bs, shared=True)
        mod_s = _Mod(mod_table, l * MOD_ROWS, shared=False)
        xp, cache = _layer(xp, mod_p, lw, tabs_p, l, None, gf, nb=Bp, n=Np, final=final)
        caches.append(cache)
        xs, _ = _layer(xs, mod_s, lw, tabs_s, l, ctx, gf, nb=Bs, n=Ns, final=final)

    stack = lambda i, w: jnp.stack([cc[i].reshape(Bp, Np, w) for cc in caches], axis=1)
    new_mla_ckv = stack(0, MLA_KV_RANK)
    new_mla_krope = stack(1, MLA_ROPE)
    new_diff_k = stack(2, 256).reshape(Bp, DEPTH, Np, DIFF_HEADS, 2, DIFF_DIM)
    new_diff_v = stack(3, 256).reshape(Bp, DEPTH, Np, DIFF_HEADS, 2 * DIFF_DIM)
    new_state_lru = jnp.stack([cc[4] for cc in caches], axis=1)
    return (xp.reshape(Bp, Np, D_MODEL), xs.reshape(Bs, Ns, D_MODEL),
            new_mla_ckv, new_mla_krope, new_diff_k, new_diff_v, new_state_lru)
```

```python
import functools
import math

import jax
import jax.numpy as jnp
from jax import lax
from jax.experimental import pallas as pl
from jax.experimental.pallas import tpu as pltpu

F32 = jnp.float32
BF16 = jnp.bfloat16

D_MODEL = 1024
DEPTH = 2
GRID_W = 64
GROUP_WIDTH = 256
MLA_HEADS = 4
MLA_NOPE = 64
MLA_ROPE = 32
MLA_V = 64
MLA_Q_RANK = 192
MLA_KV_RANK = 128
MLA_SLOT = 128
LRU_WIDTH = 256
LRU_C = 8.0
POOL_WINDOWS = (2, 4, 8, 16)
POOL_CH = 64
DIFF_HEADS = 4
DIFF_DIM = 32
HEAD_V = 64
FF_HIDDEN = 2816
FF_CHUNKS = ((0, 1536), (1536, 2816))
ROPE_BASE = 10000.0
EPS = 1e-6
IN_EFF = 2048
HALO = 8
SCAN_RUN = 4
VT_ROWS = 80
ATT_TQ = 256
TOKEN_TILE = 512
MOD_ROWS = 16
LOG2E = math.log2(math.e)

VMEM_LIMIT_BYTES = 56 * 1024 * 1024

_NT = (((1,), (1,)), ((), ()))


def _params(*sem):
    return pltpu.CompilerParams(dimension_semantics=sem, vmem_limit_bytes=VMEM_LIMIT_BYTES)


def _resident(shape):
    zeros = (0,) * len(shape)
    return pl.BlockSpec(shape, lambda *_: zeros, pipeline_mode=pl.Buffered(1))


def _dot(a, b):
    return jnp.dot(a, b, preferred_element_type=F32)


def _dot_nt(a, b):
    return lax.dot_general(a, b, _NT, preferred_element_type=F32)


def _rms_rows(x, width):
    ms = jnp.sum(x * x, axis=-1, keepdims=True) * (1.0 / width)
    return x * lax.rsqrt(ms + EPS)


def _store_vt(vt_ref, v):
    vt = v.T
    rows = v.shape[0]
    pad = VT_ROWS - HEAD_V
    ones_row = jnp.where(lax.broadcasted_iota(jnp.int32, (pad, rows), 0) == 0, 1.0, 0.0).astype(BF16)
    for hh in range(vt_ref.shape[0]):
        vt_ref[hh, 0:HEAD_V, :] = vt[hh * HEAD_V:(hh + 1) * HEAD_V, :].astype(BF16)
        vt_ref[hh, HEAD_V:VT_ROWS, :] = ones_row


class _Mod:
    def __init__(self, table, row0, shared):
        self.table, self.row0, self.shared = table, row0, shared

    def spec(self, batch_of):
        row0 = self.row0
        if self.shared:
            return pl.BlockSpec((1, 1, 6 * D_MODEL), lambda *g: (row0, 0, 0))
        return pl.BlockSpec((1, 1, 6 * D_MODEL), lambda *g: (row0 + batch_of(*g), 0, 0))


class _LayerWeights:
    def __init__(self, stacked, layer):
        self.stacked, self.layer = stacked, layer

    def __getitem__(self, name):
        return self.stacked[name]

    def spec(self, name):
        layer = self.layer
        _, rows, cols = self.stacked[name].shape
        return pl.BlockSpec((None, rows, cols), lambda *_: (layer, 0, 0), pipeline_mode=pl.Buffered(1))


def _ada_kernel(cond_ref, w_ref, b_ref, out_ref):
    c = cond_ref[...]
    s = c * jax.nn.sigmoid(c)
    out_ref[0] = _dot(s.astype(BF16), w_ref[0].astype(BF16)) + b_ref[0]


def _ada(cond_all, w_ada, b_ada):
    rows = cond_all.shape[0]
    tn = 1536
    return pl.pallas_call(
        _ada_kernel,
        grid=(DEPTH, 6 * D_MODEL // tn),
        in_specs=[
            pl.BlockSpec((rows, D_MODEL), lambda l, j: (0, 0)),
            pl.BlockSpec((1, D_MODEL, tn), lambda l, j: (l, 0, j)),
            pl.BlockSpec((1, 1, tn), lambda l, j: (l, 0, j)),
        ],
        out_specs=pl.BlockSpec((1, rows, tn), lambda l, j: (l, 0, j)),
        out_shape=jax.ShapeDtypeStruct((DEPTH, rows, 6 * D_MODEL), F32),
        compiler_params=_params("arbitrary", "arbitrary"),
        name="ada_mod",
    )(cond_all, w_ada, b_ada.reshape(DEPTH, 1, 6 * D_MODEL))


def _inproj_kernel(x_ref, mod_ref, g1_ref, win_ref, gq_ref, gkv_ref, wq_ref, wqr_ref, wk_ref, wv_ref,
                   cosq_ref, sinq_ref, cosk_ref, sink_ref, cosd_ref, sina_ref, sinb_ref,
                   q_out, k_out, vt_out, lru_out, pool_out, dq_out, dk_out, dvt_out, *cache_outs):
    x = x_ref[...]
    mod = mod_ref[0]
    sh1 = mod[:, 0:D_MODEL]
    sc1 = mod[:, D_MODEL:2 * D_MODEL]
    h = _rms_rows(x, D_MODEL) * g1_ref[...]
    hb = (h * (1.0 + sc1) + sh1).astype(BF16)

    t01 = _dot(hb, win_ref[:, 0:256])
    lane = lax.broadcasted_iota(jnp.int32, (1, 256), 1)
    cq = jnp.where(lane < MLA_Q_RANK, t01, 0.0)
    cqn = (_rms_rows(cq, MLA_Q_RANK) * gq_ref[...]).astype(BF16)
    qa = _dot(cqn, wq_ref[...])
    qr = _dot(cqn, wqr_ref[...])
    cosq = cosq_ref[...]
    sinq = sinq_ref[...]
    ckv = _dot(hb, win_ref[:, 256:384])
    lat = _rms_rows(ckv, MLA_KV_RANK) * gkv_ref[...]
    latb = lat.astype(BF16)
    kk = _dot(latb, wk_ref[...])
    _store_vt(vt_out, _dot(latb, wv_ref[...]))
    t1 = t01[:, 128:256]
    t3 = _dot(hb, win_ref[:, 384:512])
    kro = t1 * cosk_ref[...] + t3 * sink_ref[...]
    for hh in range(MLA_HEADS):
        sl = slice(hh * MLA_SLOT, (hh + 1) * MLA_SLOT)
        q_out[hh] = (qa[:, sl] * cosq + qr[:, sl] * sinq).astype(q_out.dtype)
        k_out[hh] = (kk[:, sl] + kro).astype(k_out.dtype)

    lru_out[...] = _dot(hb, win_ref[:, 512:1024])
    pool_out[...] = _dot(hb, win_ref[:, 1024:1280])

    cosd = cosd_ref[...]
    sina = sina_ref[...]
    sinb = sinb_ref[...]

    def rope(t):
        return t * cosd + pltpu.roll(t, 256 - 16, 1) * sina + pltpu.roll(t, 16, 1) * sinb

    dq = _dot(hb, win_ref[:, 1280:1536])
    dk = _dot(hb, win_ref[:, 1536:1792])
    dv = _dot(hb, win_ref[:, 1792:2048])
    dq_out[...] = (rope(dq) * (LOG2E / math.sqrt(DIFF_DIM))).astype(dq_out.dtype)
    dk_out[...] = rope(dk).astype(dk_out.dtype)
    _store_vt(dvt_out, dv)

    if cache_outs:
        lat_out, kr_out, dk_raw_out, dv_raw_out = cache_outs
        lat_out[...] = lat
        kr_out[...] = t1
        dk_raw_out[...] = dk
        dv_raw_out[...] = dv


def _inproj(x, mod, lw, tabs, *, nb, n, emit_cache):
    T = nb * n
    tm = TOKEN_TILE
    npt = n // tm
    row_blk = lambda j, b: b * npt + j

    def tok(width):
        return pl.BlockSpec((tm, width), lambda j, b: (row_blk(j, b), 0))

    def tab(width):
        return pl.BlockSpec((tm, width), lambda j, b: (j, 0))

    head = pl.BlockSpec((MLA_HEADS, tm, MLA_SLOT), lambda j, b: (0, row_blk(j, b), 0))
    vt_spec = pl.BlockSpec((MLA_HEADS, VT_ROWS, tm), lambda j, b: (0, 0, row_blk(j, b)))
    wnames = ("g1", "w_in", "gq", "gkv", "wq", "wqr", "wk", "wv")
    in_specs = [tok(D_MODEL), mod.spec(lambda j, b: b)] + [lw.spec(nm) for nm in wnames] + [
        tab(128), tab(128), tab(128), tab(128), tab(256), tab(256), tab(256)]
    out_specs = [head, head, vt_spec, tok(512), tok(256), tok(256), tok(256), vt_spec]
    vt_shape = jax.ShapeDtypeStruct((MLA_HEADS, VT_ROWS, T), BF16)
    out_shape = [
        jax.ShapeDtypeStruct((MLA_HEADS, T, MLA_SLOT), BF16),
        jax.ShapeDtypeStruct((MLA_HEADS, T, MLA_SLOT), BF16),
        vt_shape,
        jax.ShapeDtypeStruct((T, 512), F32),
        jax.ShapeDtypeStruct((T, 256), F32),
        jax.ShapeDtypeStruct((T, 256), BF16),
        jax.ShapeDtypeStruct((T, 256), BF16),
        vt_shape,
    ]
    if emit_cache:
        out_specs += [tok(128), tok(128), tok(256), tok(256)]
        out_shape += [jax.ShapeDtypeStruct((T, 128), F32), jax.ShapeDtypeStruct((T, 128), F32),
                      jax.ShapeDtypeStruct((T, 256), F32), jax.ShapeDtypeStruct((T, 256), F32)]
    return pl.pallas_call(
        _inproj_kernel,
        grid=(npt, nb),
        in_specs=in_specs,
        out_specs=out_specs,
        out_shape=out_shape,
        compiler_params=_params("arbitrary", "arbitrary"),
        name="inproj_cache" if emit_cache else "inproj",
    )(x, mod.table, *[lw[nm] for nm in wnames],
      tabs["cosq"], tabs["sinq"], tabs["cosk"], tabs["sink"], tabs["cosd"], tabs["sina"], tabs["sinb"])


def _ctx_prep_kernel(ckv_ref, kr_ref, dk_ref, dv_ref, wk_ref, wv_ref, k_out, vt_out, dk_out, dvt_out):
    latb = ckv_ref[...].astype(BF16)
    kk = _dot(latb, wk_ref[...])
    kr = kr_ref[...]
    for hh in range(MLA_HEADS):
        k_out[hh] = (kk[:, hh * MLA_SLOT:(hh + 1) * MLA_SLOT] + kr).astype(k_out.dtype)
    _store_vt(vt_out, _dot(latb, wv_ref[...]))
    dk_out[...] = dk_ref[...].astype(dk_out.dtype)
    _store_vt(dvt_out, dv_ref[...])


def _ctx_prep(ckv, kr_pad, cdk, cdv, lw, *, nb, p):
    T = nb * p
    layer = lw.layer
    cache_row = lambda w: pl.BlockSpec((p, w), lambda b: (b * DEPTH + layer, 0))
    row = lambda w: pl.BlockSpec((p, w), lambda b: (b, 0))
    vt_spec = pl.BlockSpec((MLA_HEADS, VT_ROWS, p), lambda b: (0, 0, b))
    vt_shape = jax.ShapeDtypeStruct((MLA_HEADS, VT_ROWS, T), BF16)
    return pl.pallas_call(
        _ctx_prep_kernel,
        grid=(nb,),
        in_specs=[cache_row(128), cache_row(128), cache_row(256), cache_row(256),
                  lw.spec("wk"), lw.spec("wv")],
        out_specs=[pl.BlockSpec((MLA_HEADS, p, MLA_SLOT), lambda b: (0, b, 0)), vt_spec, row(256), vt_spec],
        out_shape=[jax.ShapeDtypeStruct((MLA_HEADS, T, MLA_SLOT), BF16), vt_shape,
                   jax.ShapeDtypeStruct((T, 256), BF16), vt_shape],
        compiler_params=_params("arbitrary"),
        name="ctx_prep",
    )(ckv, kr_pad, cdk, cdv, lw["wk"], lw["wv"])


SAFE_DENOM = 2.0 ** -60
BOUND_SLACK = 1.02


def _scores(k_new, k_ctx, q):
    sn = _dot_nt(k_new(), q)
    sc = _dot_nt(k_ctx(), q) if k_ctx is not None else None
    return sn, sc


def _exact_shift(k_new, k_ctx, q):
    sn, sc = _scores(k_new, k_ctx, q)
    m = jnp.max(sn, axis=0, keepdims=True)
    if sc is not None:
        m = jnp.maximum(m, jnp.max(sc, axis=0, keepdims=True))
    return m


def _bound_shift(q, key_norm2):
    qf = q.astype(F32)
    ones = jnp.ones((8, q.shape[1]), BF16)
    q_norm2 = _dot_nt(ones, (qf * qf).astype(BF16))[0:1, :]
    return jnp.sqrt(q_norm2 * key_norm2) * BOUND_SLACK


def _max_row_norm2(k_new, k_ctx, col_sum):
    def one(k):
        kf = k.astype(F32)
        return jnp.max(_dot((kf * kf).astype(BF16), col_sum), axis=0, keepdims=True)
    m = one(k_new)
    if k_ctx is not None:
        m = jnp.maximum(m, one(k_ctx))
    return m * BOUND_SLACK


def _exp_stage(e_buf, k_new, k_ctx, q, shift, n_ctx):
    sn, sc = _scores(k_new, k_ctx, q)
    e_buf[n_ctx:, :] = jnp.exp2(sn - shift).astype(BF16)
    if sc is not None:
        e_buf[0:n_ctx, :] = jnp.exp2(sc - shift).astype(BF16)


def _value_stage(e_buf, vt_new, vt_ctx, n_ctx):
    o = _dot(vt_new(), e_buf[n_ctx:, :])
    if vt_ctx is not None:
        o = o + _dot(vt_ctx(), e_buf[0:n_ctx, :])
    return o


def _run_pipeline(n_maps, exp_stage, value_stage):
    exp_stage(0)
    for u in range(n_maps):
        if u + 1 < n_maps:
            exp_stage(u + 1)
        value_stage(u)


def _att_scratch(nk):
    return [pltpu.VMEM((8, 128), F32),
            pltpu.VMEM((MLA_HEADS * HEAD_V, ATT_TQ), F32),
            pltpu.VMEM((nk, ATT_TQ), BF16), pltpu.VMEM((nk, ATT_TQ), BF16)]


def _att_nsub(n):
    return 2 if n % (2 * ATT_TQ) == 0 else 1


def _mla_attn_kernel(*refs, has_ctx, nsub):
    if has_ctx:
        q_ref, k_ref, vt_ref, kc_ref, vtc_ref, o_ref, kn2, ot, e0, e1 = refs
        n_ctx = kc_ref.shape[1]
    else:
        q_ref, k_ref, vt_ref, o_ref, kn2, ot, e0, e1 = refs
        n_ctx = 0
    e_bufs = (e0, e1)

    @pl.when(pl.program_id(1) == 0)
    def _():
        ones = jnp.ones((MLA_SLOT, 128), BF16)
        for hh in range(MLA_HEADS):
            kn2[hh:hh + 1, :] = _max_row_norm2(k_ref[hh], kc_ref[hh] if has_ctx else None, ones)

    def run(exact):
        denoms = []

        def exp_stage(u):
            t, hh = divmod(u, MLA_HEADS)
            q = q_ref[hh, t * ATT_TQ:(t + 1) * ATT_TQ, :]
            k_new = lambda: k_ref[hh]
            k_ctx = (lambda: kc_ref[hh]) if has_ctx else None
            shift = _exact_shift(k_new, k_ctx, q) if exact else _bound_shift(q, kn2[hh:hh + 1, 0:1])
            _exp_stage(e_bufs[u % 2], k_new, k_ctx, q, shift, n_ctx)

        def value_stage(u):
            t, hh = divmod(u, MLA_HEADS)
            o = _value_stage(e_bufs[u % 2], lambda: vt_ref[hh],
                             (lambda: vtc_ref[hh]) if has_ctx else None, n_ctx)
            denom = o[HEAD_V:HEAD_V + 1, :]
            denoms.append(denom)
            ot[hh * HEAD_V:(hh + 1) * HEAD_V, :] = o[0:HEAD_V, :] * (1.0 / denom)
            if hh == MLA_HEADS - 1:
                o_ref[t * ATT_TQ:(t + 1) * ATT_TQ, :] = ot[...].T

        _run_pipeline(nsub * MLA_HEADS, exp_stage, value_stage)
        return jnp.min(functools.reduce(jnp.minimum, denoms))

    denom_min = run(exact=False)

    @pl.when(jnp.logical_not(denom_min >= SAFE_DENOM))
    def _():
        run(exact=True)


def _mla_attn(q, k, vt, ctx, *, nb, n):
    nsub = _att_nsub(n)
    tq = nsub * ATT_TQ
    npt = n // tq
    H, S = MLA_HEADS, MLA_SLOT
    in_specs = [
        pl.BlockSpec((H, tq, S), lambda b, j: (0, b * npt + j, 0)),
        pl.BlockSpec((H, n, S), lambda b, j: (0, b, 0)),
        pl.BlockSpec((H, VT_ROWS, n), lambda b, j: (0, 0, b)),
    ]
    args = [q, k, vt]
    n_ctx = 0
    if ctx is not None:
        n_ctx = ctx[0].shape[1] // nb
        in_specs += [
            pl.BlockSpec((H, n_ctx, S), lambda b, j: (0, b, 0)),
            pl.BlockSpec((H, VT_ROWS, n_ctx), lambda b, j: (0, 0, b)),
        ]
        args += list(ctx)
    return pl.pallas_call(
        functools.partial(_mla_attn_kernel, has_ctx=ctx is not None, nsub=nsub),
        grid=(nb, npt),
        in_specs=in_specs,
        out_specs=pl.BlockSpec((tq, 256), lambda b, j: (b * npt + j, 0)),
        out_shape=jax.ShapeDtypeStruct((nb * n, 256), F32),
        scratch_shapes=_att_scratch(n + n_ctx),
        compiler_params=_params("arbitrary", "arbitrary"),
        name="mla_attn_ctx" if ctx is not None else "mla_attn",
    )(*args)


def _diff_attn_kernel(*refs, has_ctx, nsub, lam_init):
    if has_ctx:
        lv_ref, g_ref, q_ref, k_ref, vt_ref, kc_ref, vtc_ref, o_ref, kn2, ot, e0, e1 = refs
        n_ctx = kc_ref.shape[0]
    else:
        lv_ref, g_ref, q_ref, k_ref, vt_ref, o_ref, kn2, ot, e0, e1 = refs
        n_ctx = 0
    e_bufs = (e0, e1)
    lv = lv_ref[...]
    lam = (jnp.exp(jnp.sum(lv[0:1] * lv[1:2], axis=-1, keepdims=True))
           - jnp.exp(jnp.sum(lv[2:3] * lv[3:4], axis=-1, keepdims=True)) + lam_init)
    lane128 = lax.broadcasted_iota(jnp.int32, (1, 128), 1)
    n_pairs = 2 * DIFF_HEADS

    @pl.when(pl.program_id(1) == 0)
    def _():
        dim = lax.broadcasted_iota(jnp.int32, (256, 128), 0)
        col = lax.broadcasted_iota(jnp.int32, (256, 128), 1)
        indicator = jnp.where(dim // DIFF_DIM == col, 1.0, 0.0).astype(BF16)
        kn2[0:1, :] = _max_row_norm2(k_ref[...], kc_ref[...] if has_ctx else None, indicator)

    def run(exact):
        denoms = []
        outs = {}

        def exp_stage(u):
            t, p = divmod(u, n_pairs)
            tile = slice((p * DIFF_DIM // 128) * 128, (p * DIFF_DIM // 128 + 1) * 128)
            k_new = lambda: k_ref[:, tile]
            k_ctx = (lambda: kc_ref[:, tile]) if has_ctx else None
            q = q_ref[t * ATT_TQ:(t + 1) * ATT_TQ, tile]
            lo = p * DIFF_DIM - tile.start
            in_pair = (lane128 >= lo) & (lane128 < lo + DIFF_DIM)
            qm = jnp.where(in_pair, q, jnp.zeros_like(q))
            shift = _exact_shift(k_new, k_ctx, qm) if exact else _bound_shift(qm, kn2[0:1, p:p + 1])
            _exp_stage(e_bufs[u % 2], k_new, k_ctx, qm, shift, n_ctx)

        def value_stage(u):
            t, p = divmod(u, n_pairs)
            hh = p // 2
            o = _value_stage(e_bufs[u % 2], lambda: vt_ref[hh],
                             (lambda: vtc_ref[hh]) if has_ctx else None, n_ctx)
            denom = o[HEAD_V:HEAD_V + 1, :]
            denoms.append(denom)
            outs[u] = (o[0:HEAD_V, :], denom)
            if p % 2 == 1:
                (o0, l0), (o1, l1) = outs.pop(u - 1), outs.pop(u)
                o = o0 * (1.0 / l0) - o1 * (lam / l1)
                msq = jnp.sum(o * o, axis=0, keepdims=True) * (1.0 / HEAD_V)
                ot[hh * HEAD_V:(hh + 1) * HEAD_V, :] = o * lax.rsqrt(msq + EPS)
            if p == n_pairs - 1:
                o_ref[t * ATT_TQ:(t + 1) * ATT_TQ, :] = (ot[...].T * g_ref[...]) * (1.0 - lam_init)

        _run_pipeline(nsub * n_pairs, exp_stage, value_stage)
        return jnp.min(functools.reduce(jnp.minimum, denoms))

    denom_min = run(exact=False)

    @pl.when(jnp.logical_not(denom_min >= SAFE_DENOM))
    def _():
        run(exact=True)


def _diff_attn(q, k, vt, ctx, lw, *, nb, n, lam_init):
    nsub = _att_nsub(n)
    tq = nsub * ATT_TQ
    npt = n // tq
    in_specs = [
        lw.spec("diff_lambda"),
        lw.spec("diff_g"),
        pl.BlockSpec((tq, 256), lambda b, j: (b * npt + j, 0)),
        pl.BlockSpec((n, 256), lambda b, j: (b, 0)),
        pl.BlockSpec((DIFF_HEADS, VT_ROWS, n), lambda b, j: (0, 0, b)),
    ]
    args = [lw["diff_lambda"], lw["diff_g"], q, k, vt]
    n_ctx = 0
    if ctx is not None:
        n_ctx = ctx[0].shape[0] // nb
        in_specs += [pl.BlockSpec((n_ctx, 256), lambda b, j: (b, 0)),
                     pl.BlockSpec((DIFF_HEADS, VT_ROWS, n_ctx), lambda b, j: (0, 0, b))]
        args += list(ctx)
    return pl.pallas_call(
        functools.partial(_diff_attn_kernel, has_ctx=ctx is not None, nsub=nsub, lam_init=lam_init),
        grid=(nb, npt),
        in_specs=in_specs,
        out_specs=pl.BlockSpec((tq, 256), lambda b, j: (b * npt + j, 0)),
        out_shape=jax.ShapeDtypeStruct((nb * n, 256), F32),
        scratch_shapes=_att_scratch(n + n_ctx),
        compiler_params=_params("arbitrary", "arbitrary"),
        name="diff_attn_ctx" if ctx is not None else "diff_attn",
    )(*args)


def _shift_rows(v, k):
    return pltpu.roll(v, (-k) % v.shape[0], 0)


def _scan_strided(a_ref, b_ref, h_ref, row0, carry, n_rows, reverse):
    sub = lax.broadcasted_iota(jnp.int32, (8, 128), 0)
    span = 8 * SCAN_RUN
    order = tuple(range(SCAN_RUN))[::-1] if reverse else tuple(range(SCAN_RUN))
    starts = tuple(range(0, n_rows, span))[::-1] if reverse else tuple(range(0, n_rows, span))
    carries = []
    for lt in range(a_ref.shape[0]):
        c_in = carry[:, lt * 128:(lt + 1) * 128]
        for start in starts:
            tile = lambda ref, g: ref[lt, pl.ds(row0 + start + g, 8, stride=SCAN_RUN), :]
            a = [tile(a_ref, g) for g in range(SCAN_RUN)]
            b = [tile(b_ref, g) for g in range(SCAN_RUN)]
            h = {order[0]: b[order[0]]}
            p = {order[0]: a[order[0]]}
            for prev, g in zip(order, order[1:]):
                h[g] = a[g] * h[prev] + b[g]
                p[g] = a[g] * p[prev]
            pi, hi = p[order[-1]], h[order[-1]]
            for s in (1, 2, 4):
                shift = 8 - s if reverse else s
                valid = (sub < 8 - s) if reverse else (sub >= s)
                pr, hr = pltpu.roll(pi, shift, 0), pltpu.roll(hi, shift, 0)
                hi = jnp.where(valid, pi * hr + hi, hi)
                pi = jnp.where(valid, pi * pr, pi)
            one = 7 if reverse else 1
            first = (sub == 7) if reverse else (sub == 0)
            pe = jnp.where(first, 1.0, pltpu.roll(pi, one, 0))
            he = jnp.where(first, 0.0, pltpu.roll(hi, one, 0))
            c = pe * c_in + he
            for g in range(SCAN_RUN):
                h_ref[lt, pl.ds(start + g, 8, stride=SCAN_RUN), :] = h[g] + p[g] * c
            last = 0 if reverse else 7
            c_in = pi[last:last + 1, :] * c_in + hi[last:last + 1, :]
        carries.append(c_in)
    return jnp.concatenate(carries, axis=1)


def _sigmoid(x):
    return 0.5 * jnp.tanh(0.5 * x) + 0.5


def _gelu_tanh(x):
    return x * (0.5 * (1.0 + jnp.tanh(math.sqrt(2.0 / math.pi) * (x + 0.044715 * (x * x * x)))))


def _lru_kernel(u_ref, h0_ref, cw_ref, cb_ref, wg_ref, bg_ref, lam_ref, y_ref, st_ref,
                xpad, a1s, b1s, a0c, b0c, hc, *, N, T):
    W = LRU_WIDTH
    nc = N // T
    tiles = [slice(lt * 128, (lt + 1) * 128) for lt in range(W // 128)]
    zeros = jnp.zeros((HALO, W), F32)
    xpad[0:HALO, :] = zeros
    xpad[N + HALO:N + 2 * HALO, :] = zeros

    def fill(j, carry):
        r0 = pl.multiple_of(j * T, T)
        xpad[pl.ds(r0 + HALO, T), :] = u_ref[pl.ds(r0, T), 0:W]
        return carry

    lax.fori_loop(0, nc, fill, 0)

    z = -lam_ref[...]
    sp = jnp.maximum(z, 0.0) + jnp.log1p(jnp.exp(-jnp.abs(z)))
    cw = cw_ref[...]
    cb = cb_ref[...]
    bg = bg_ref[...]

    def fwd(j, carry):
        r0 = pl.multiple_of(j * T, T)
        ext = xpad[pl.ds(r0, T + 2 * HALO), :]
        body = slice(HALO, HALO + T)
        xc = cb
        for tap in range(4):
            xc = xc + _shift_rows(ext, tap - 1)[body] * cw[tap:tap + 1]
        g = _sigmoid(_dot(xc.astype(BF16), wg_ref[...]) + bg)
        ab = []
        for d in range(2):
            r = g[:, d * W:(d + 1) * W]
            i = g[:, (2 + d) * W:(3 + d) * W]
            log_a = (-LRU_C * r) * sp[d:d + 1]
            a = jnp.exp(log_a)
            bt = (jnp.sqrt(1.0 - a * a) * i) * xc
            ab.append((a, bt))
        for lt, lanes in enumerate(tiles):
            a0c[lt] = ab[0][0][:, lanes]
            b0c[lt] = ab[0][1][:, lanes]
            a1s[lt, pl.ds(r0, T), :] = ab[1][0][:, lanes]
            b1s[lt, pl.ds(r0, T), :] = ab[1][1][:, lanes]
        carry = _scan_strided(a0c, b0c, hc, 0, carry, T, reverse=False)
        for lt, lanes in enumerate(tiles):
            y_ref[pl.ds(r0, T), lanes] = hc[lt]
        return carry

    cf = lax.fori_loop(0, nc, fwd, h0_ref[0, 0:1, :])

    def bwd(jj, carry):
        r0 = pl.multiple_of((nc - 1 - jj) * T, T)
        carry = _scan_strided(a1s, b1s, hc, r0, carry, T, reverse=True)
        for lt, lanes in enumerate(tiles):
            gb = u_ref[pl.ds(r0, T), W + lt * 128:W + (lt + 1) * 128]
            y_ref[pl.ds(r0, T), lanes] = (y_ref[pl.ds(r0, T), lanes] + hc[lt]) * _gelu_tanh(gb)
        return carry

    cbw = lax.fori_loop(0, nc, bwd, h0_ref[0, 1:2, :])
    st_ref[0, 0:1, :] = cf
    st_ref[0, 1:2, :] = cbw


def _lru(u, h0, h0_block, lw, *, nb, n):
    T = min(n, 256)
    W = LRU_WIDTH
    return pl.pallas_call(
        functools.partial(_lru_kernel, N=n, T=T),
        grid=(nb,),
        in_specs=[
            pl.BlockSpec((n, 2 * W), lambda b: (b, 0)),
            pl.BlockSpec((1, 2, W), lambda b: (h0_block(b), 0, 0)),
            lw.spec("conv_w"), lw.spec("conv_b"), lw.spec("w_gate"), lw.spec("b_gate"),
            lw.spec("lru_lambda"),
        ],
        out_specs=[
            pl.BlockSpec((n, W), lambda b: (b, 0)),
            pl.BlockSpec((1, 2, W), lambda b: (b, 0, 0)),
        ],
        out_shape=[
            jax.ShapeDtypeStruct((nb * n, W), F32),
            jax.ShapeDtypeStruct((nb, 2, W), F32),
        ],
        scratch_shapes=[
            pltpu.VMEM((n + 2 * HALO, W), F32),
            pltpu.VMEM((W // 128, n, 128), F32),
            pltpu.VMEM((W // 128, n, 128), F32),
            pltpu.VMEM((W // 128, T, 128), F32),
            pltpu.VMEM((W // 128, T, 128), F32),
            pltpu.VMEM((W // 128, T, 128), F32),
        ],
        compiler_params=_params("arbitrary"),
        name="rglru",
    )(u, h0, lw["conv_w"], lw["conv_b"], lw["w_gate"], lw["b_gate"], lw["lru_lambda"])


def _pool_kernel(u_ref, wp_ref, sc_ref, y_ref, xpad, *, N, T):
    W = GROUP_WIDTH
    nc = N // T
    zeros = jnp.zeros((HALO, W), F32)
    xpad[0:HALO, :] = zeros
    xpad[N + HALO:N + 2 * HALO, :] = zeros

    def fill(j, carry):
        r0 = pl.multiple_of(j * T, T)
        xpad[pl.ds(r0 + HALO, T), :] = u_ref[pl.ds(r0, T), :]
        return carry

    lax.fori_loop(0, nc, fill, 0)

    grp = lax.broadcasted_iota(jnp.int32, (1, W), 1) // POOL_CH
    half = jnp.where(grp == 0, 1, jnp.where(grp == 1, 2, jnp.where(grp == 2, 4, 8)))
    scale = sc_ref[...]

    def chunk(j, carry):
        r0 = pl.multiple_of(j * T, T)
        ext = xpad[pl.ds(r0, T + 2 * HALO), :]
        w2 = _shift_rows(ext, -1) + ext
        w4 = _shift_rows(w2, -1) + _shift_rows(w2, 1)
        w8 = _shift_rows(w4, -2) + _shift_rows(w4, 2)
        w16 = _shift_rows(w8, -4) + _shift_rows(w8, 4)
        ws = jnp.where(grp == 0, w2, jnp.where(grp == 1, w4, jnp.where(grp == 2, w8, w16)))
        body = slice(HALO, HALO + T)
        t = r0 + lax.broadcasted_iota(jnp.int32, (T, W), 0)
        cnt = (jnp.minimum(t + half, N) - jnp.maximum(t - half, 0)).astype(F32)
        d = ws[body] / cnt - ext[body]
        y_ref[pl.ds(r0, T), :] = _dot(d.astype(BF16), wp_ref[...]) * scale
        return carry

    lax.fori_loop(0, nc, chunk, 0)


def _pool(u, lw, *, nb, n):
    W = GROUP_WIDTH
    T = min(n, 256)
    return pl.pallas_call(
        functools.partial(_pool_kernel, N=n, T=T),
        grid=(nb,),
        in_specs=[pl.BlockSpec((n, W), lambda b: (b, 0)), lw.spec("w_pool"), lw.spec("pool_scale")],
        out_specs=pl.BlockSpec((n, W), lambda b: (b, 0)),
        out_shape=jax.ShapeDtypeStruct((nb * n, W), F32),
        scratch_shapes=[pltpu.VMEM((n + 2 * HALO, W), F32)],
        compiler_params=_params("arbitrary"),
        name="pool_mixer",
    )(u, lw["w_pool"], lw["pool_scale"])


def _mix_ffn_kernel(*refs, final):
    if final:
        (x_ref, ya_ref, yb_ref, yc_ref, yd_ref, mod_ref, g2_ref, wo_ref, wg_ref, wu_ref, wd_ref,
         gf_ref, o_ref) = refs
    else:
        (x_ref, ya_ref, yb_ref, yc_ref, yd_ref, mod_ref, g2_ref, wo_ref, wg_ref, wu_ref, wd_ref,
         o_ref) = refs
    mod = mod_ref[0]
    gate1 = mod[:, 2 * D_MODEL:3 * D_MODEL]
    sh2 = mod[:, 3 * D_MODEL:4 * D_MODEL]
    sc2 = mod[:, 4 * D_MODEL:5 * D_MODEL]
    gate2 = mod[:, 5 * D_MODEL:6 * D_MODEL]
    mix = None
    for i, y_ref in enumerate((ya_ref, yb_ref, yc_ref, yd_ref)):
        part = _dot(y_ref[...].astype(BF16), wo_ref[i * GROUP_WIDTH:(i + 1) * GROUP_WIDTH, :])
        mix = part if mix is None else mix + part
    x1 = x_ref[...] + gate1 * mix
    h = _rms_rows(x1, D_MODEL) * g2_ref[...]
    hb = (h * (1.0 + sc2) + sh2).astype(BF16)
    ff = None
    for lo, hi in FF_CHUNKS:
        g = _dot(hb, wg_ref[:, lo:hi])
        up = _dot(hb, wu_ref[:, lo:hi])
        act = ((g * jax.nn.sigmoid(g)) * up).astype(BF16)
        part = _dot(act, wd_ref[lo:hi, :])
        ff = part if ff is None else ff + part
    x2 = x1 + gate2 * ff
    if final:
        x2 = _rms_rows(x2, D_MODEL) * gf_ref[...]
    o_ref[...] = x2


def _mix_ffn(x, ys, mod, lw, gf, *, nb, n, final):
    T = nb * n
    tm = TOKEN_TILE
    npt = n // tm

    def tok(width):
        return pl.BlockSpec((tm, width), lambda i: (i, 0))

    wnames = ("g2", "w_out", "w_gate_ff", "w_up_ff", "w_down")
    in_specs = [tok(D_MODEL), tok(256), tok(256), tok(256), tok(256),
                mod.spec(lambda i: i // npt)] + [lw.spec(nm) for nm in wnames]
    args = [x, *ys, mod.table] + [lw[nm] for nm in wnames]
    if final:
        in_specs.append(_resident((1, D_MODEL)))
        args.append(gf)
    return pl.pallas_call(
        functools.partial(_mix_ffn_kernel, final=final),
        grid=(T // tm,),
        in_specs=in_specs,
        out_specs=tok(D_MODEL),
        out_shape=jax.ShapeDtypeStruct((T, D_MODEL), F32),
        compiler_params=_params("arbitrary"),
        name="mix_ffn_final" if final else "mix_ffn",
    )(*args)


def _block_diag(w):
    L, G, c, e = w.shape
    return jnp.einsum('lgce,gh->lgche', w, jnp.eye(G, dtype=w.dtype)).reshape(L, G * c, G * e)


def _rot_cols(w):
    return jnp.concatenate([-w[..., 16:32], w[..., 0:16]], axis=-1)


def _stack_weights(p):
    w_in = p["w_in"]
    o1 = MLA_Q_RANK
    o2 = o1 + MLA_KV_RANK
    o3 = o2 + MLA_ROPE
    c_q, c_kv, k_r, rest = w_in[..., :o1], w_in[..., o1:o2], w_in[..., o2:o3], w_in[..., o3:]
    z = lambda n: jnp.zeros((DEPTH, D_MODEL, n), F32)
    w_in_eff = jnp.concatenate([c_q, k_r, z(32), c_kv, z(64), _rot_cols(k_r), z(32), rest], axis=-1)

    w_uq = p["mla_w_uq"]
    qd = MLA_NOPE + MLA_ROPE
    wq_parts, wqr_parts = [], []
    zq = lambda n: jnp.zeros((DEPTH, MLA_Q_RANK, n), F32)
    for h in range(MLA_HEADS):
        wh = w_uq[..., h * qd:(h + 1) * qd]
        wq_parts += [wh, zq(MLA_SLOT - qd)]
        wqr_parts += [zq(MLA_NOPE), _rot_cols(wh[..., MLA_NOPE:]), zq(MLA_SLOT - qd)]
    pad_rows = lambda w: jnp.pad(w, ((0, 0), (0, 256 - MLA_Q_RANK), (0, 0)))
    w_ukv = p["mla_w_ukv"]
    wk_parts, wv_parts = [], []
    zk = jnp.zeros((DEPTH, MLA_KV_RANK, MLA_SLOT - MLA_NOPE), F32)
    for h in range(MLA_HEADS):
        base = h * (MLA_NOPE + MLA_V)
        wk_parts += [w_ukv[..., base:base + MLA_NOPE], zk]
        wv_parts.append(w_ukv[..., base + MLA_NOPE:base + MLA_NOPE + MLA_V])

    w_r, w_i, b_r, b_i = p["lru_w_r"], p["lru_w_i"], p["lru_b_r"], p["lru_b_i"]
    w_gate = jnp.concatenate([_block_diag(w_r[:, 0]), _block_diag(w_r[:, 1]),
                              _block_diag(w_i[:, 0]), _block_diag(w_i[:, 1])], axis=-1)
    b_gate = jnp.concatenate([b_r[:, 0], b_r[:, 1], b_i[:, 0], b_i[:, 1]], axis=-1)
    w_gu = p["w_gu"]
    row = lambda v: v[:, None, :]
    return {
        "g1": row(p["norm1_g"]),
        "g2": row(p["norm2_g"]),
        "w_in": w_in_eff.astype(BF16),
        "gq": row(jnp.pad(p["mla_q_norm_g"], ((0, 0), (0, 256 - MLA_Q_RANK)))),
        "gkv": row(p["mla_kv_norm_g"]),
        "wq": pad_rows(jnp.concatenate(wq_parts, axis=-1)).astype(BF16),
        "wqr": pad_rows(jnp.concatenate(wqr_parts, axis=-1)).astype(BF16),
        "wk": jnp.concatenate(wk_parts, axis=-1).astype(BF16),
        "wv": jnp.concatenate(wv_parts, axis=-1).astype(BF16),
        "conv_w": p["lru_conv_w"],
        "conv_b": row(p["lru_conv_b"]),
        "w_gate": w_gate.astype(BF16),
        "b_gate": row(b_gate),
        "lru_lambda": p["lru_lambda"],
        "w_pool": _block_diag(p["pool_w"]).astype(BF16),
        "pool_scale": row(p["pool_scale"]),
        "diff_lambda": p["diff_lambda"],
        "diff_g": row(jnp.tile(p["diff_norm_g"], (1, DIFF_HEADS))),
        "w_out": p["w_out"].astype(BF16),
        "w_gate_ff": w_gu[..., :FF_HIDDEN].astype(BF16),
        "w_up_ff": w_gu[..., FF_HIDDEN:].astype(BF16),
        "w_down": p["w_down"].astype(BF16),
    }


def _rope_tables(n, positional):
    quarter = MLA_ROPE // 4
    if positional:
        t = jnp.arange(n)
        row = (t // GRID_W).astype(F32)
        col = (t % GRID_W).astype(F32)
        inv = ROPE_BASE ** (-jnp.arange(quarter, dtype=F32) / quarter)
        ang = jnp.concatenate([row[:, None] * inv, col[:, None] * inv], axis=-1)
        cos, sin = jnp.cos(ang), jnp.sin(ang)
    else:
        cos, sin = jnp.ones((n, 16), F32), jnp.zeros((n, 16), F32)
    one = lambda w: jnp.ones((n, w), F32)
    zero = lambda w: jnp.zeros((n, w), F32)
    scale = LOG2E / math.sqrt(MLA_NOPE + MLA_ROPE)
    return {
        "cosq": jnp.concatenate([one(64), cos, cos, one(32)], axis=1) * scale,
        "sinq": jnp.concatenate([zero(64), sin, sin, zero(32)], axis=1) * scale,
        "cosk": jnp.concatenate([zero(64), cos, cos, zero(32)], axis=1),
        "sink": jnp.concatenate([zero(64), sin, sin, zero(32)], axis=1),
        "cosd": jnp.tile(jnp.concatenate([cos, cos], axis=1), (1, 8)),
        "sina": jnp.tile(jnp.concatenate([-sin, zero(16)], axis=1), (1, 8)),
        "sinb": jnp.tile(jnp.concatenate([zero(16), sin], axis=1), (1, 8)),
    }


def _layer(x, mod, lw, tabs, layer_idx, ctx, gf, *, nb, n, final):
    emit_cache = ctx is None
    tok_nb, tok_n = (1, nb * n) if mod.shared else (nb, n)
    outs = _inproj(x, mod, lw, tabs, nb=tok_nb, n=tok_n, emit_cache=emit_cache)
    q, k, vt, u_lru, u_pool, dq, dk, dvt = outs[:8]
    lam_init = 0.8 - 0.6 * math.exp(-0.3 * layer_idx)
    if ctx is None:
        h0 = jnp.zeros((1, 2, LRU_WIDTH), F32)
        h0_block = lambda b: 0
        mla_ctx = diff_ctx = None
    else:
        ckv, kr_pad, cdk, cdv, h0 = ctx
        p = ckv.shape[0] // (nb * DEPTH)
        h0_block = lambda b: b * DEPTH + layer_idx
        kc, vtc, dkc, dvtc = _ctx_prep(ckv, kr_pad, cdk, cdv, lw, nb=nb, p=p)
        mla_ctx = (kc, vtc)
        diff_ctx = (dkc, dvtc)
    y_mla = _mla_attn(q, k, vt, mla_ctx, nb=nb, n=n)
    y_lru, st = _lru(u_lru, h0, h0_block, lw, nb=nb, n=n)
    y_pool = _pool(u_pool, lw, nb=nb, n=n)
    y_diff = _diff_attn(dq, dk, dvt, diff_ctx, lw, nb=nb, n=n, lam_init=lam_init)
    x2 = _mix_ffn(x, (y_mla, y_lru, y_pool, y_diff), mod, lw, gf, nb=tok_nb, n=tok_n, final=final)
    cache = (outs[8], outs[9][:, 64:96], outs[10], outs[11], st) if emit_cache else None
    return x2, cache


def kernel(x_prompt, x_sample, cache_mla_ckv, cache_mla_krope, cache_diff_k, cache_diff_v, state_lru,
           c, c_ctx, w_ada, b_ada, norm1_g, norm2_g, w_in, mla_q_norm_g, mla_w_uq, mla_kv_norm_g,
           mla_w_ukv, lru_conv_w, lru_conv_b, lru_w_r, lru_b_r, lru_w_i, lru_b_i, lru_lambda, pool_w,
           pool_scale, diff_lambda, diff_norm_g, w_out, w_gu, w_down, final_norm_g):
    p = {
        "norm1_g": norm1_g, "norm2_g": norm2_g, "w_in": w_in, "mla_q_norm_g": mla_q_norm_g,
        "mla_w_uq": mla_w_uq, "mla_kv_norm_g": mla_kv_norm_g, "mla_w_ukv": mla_w_ukv,
        "lru_conv_w": lru_conv_w, "lru_conv_b": lru_conv_b, "lru_w_r": lru_w_r, "lru_b_r": lru_b_r,
        "lru_w_i": lru_w_i, "lru_b_i": lru_b_i, "lru_lambda": lru_lambda, "pool_w": pool_w,
        "pool_scale": pool_scale, "diff_lambda": diff_lambda, "diff_norm_g": diff_norm_g,
        "w_out": w_out, "w_gu": w_gu, "w_down": w_down,
    }
    Bp, Np, _ = x_prompt.shape
    Bs, Ns, _ = x_sample.shape
    P = cache_mla_ckv.shape[2]

    cond_all = jnp.concatenate([c, c_ctx[None, :], jnp.zeros((MOD_ROWS - Bs - 1, D_MODEL), F32)], axis=0)
    mod_table = _ada(cond_all, w_ada, b_ada).reshape(DEPTH * MOD_ROWS, 1, 6 * D_MODEL)
    tabs_p = _rope_tables(Bp * Np, positional=False)
    tabs_s = _rope_tables(Ns, positional=True)
    kr_pad = jnp.pad(cache_mla_krope, ((0, 0), (0, 0), (0, 0), (MLA_NOPE, MLA_SLOT - MLA_NOPE - MLA_ROPE)))
    flat = lambda a, w: a.reshape(Bs * DEPTH * P, w)
    ctx = (flat(cache_mla_ckv, MLA_KV_RANK), flat(kr_pad, MLA_SLOT), flat(cache_diff_k, 256),
           flat(cache_diff_v, 256), state_lru.reshape(Bs * DEPTH, 2, LRU_WIDTH))
    gf = final_norm_g[None, :]
    stacked = _stack_weights(p)

    xp = x_prompt.reshape(Bp * Np, D_MODEL)
    xs = x_sample.reshape(Bs * Ns, D_MODEL)
    caches = []
    for l in range(DEPTH):
        lw = _LayerWeights(stacked, l)
        final = l == DEPTH - 1
        mod_p = _Mod(mod_table, l * MOD_ROWS + Bs, shared=True)
        mod_s = _Mod(mod_table, l * MOD_ROWS, shared=False)
        xp, cache = _layer(xp, mod_p, lw, tabs_p, l, None, gf, nb=Bp, n=Np, final=final)
        caches.append(cache)
        xs, _ = _layer(xs, mod_s, lw, tabs_s, l, ctx, gf, nb=Bs, n=Ns, final=final)

    stack = lambda i, w: jnp.stack([cc[i].reshape(Bp, Np, w) for cc in caches], axis=1)
    new_mla_ckv = stack(0, MLA_KV_RANK)
    new_mla_krope = stack(1, MLA_ROPE)
    new_diff_k = stack(2, 256).reshape(Bp, DEPTH, Np, DIFF_HEADS, 2, DIFF_DIM)
    new_diff_v = stack(3, 256).reshape(Bp, DEPTH, Np, DIFF_HEADS, 2 * DIFF_DIM)
    new_state_lru = jnp.stack([cc[4] for cc in caches], axis=1)
    return (xp.reshape(Bp, Np, D_MODEL), xs.reshape(Bs, Ns, D_MODEL),
            new_mla_ckv, new_mla_krope, new_diff_k, new_diff_v, new_state_lru)
```

```python
import functools
import math

import jax
import jax.numpy as jnp
from jax import lax
from jax.experimental import pallas as pl
from jax.experimental.pallas import tpu as pltpu

F32 = jnp.float32
BF16 = jnp.bfloat16

D_MODEL = 1024
DEPTH = 2
GRID_W = 64
GROUP_WIDTH = 256
MLA_HEADS = 4
MLA_NOPE = 64
MLA_ROPE = 32
MLA_V = 64
MLA_Q_RANK = 192
MLA_KV_RANK = 128
MLA_SLOT = 128
LRU_WIDTH = 256
LRU_C = 8.0
POOL_WINDOWS = (2, 4, 8, 16)
POOL_CH = 64
DIFF_HEADS = 4
DIFF_DIM = 32
HEAD_V = 64
FF_HIDDEN = 2816
FF_CHUNKS = ((0, 1536), (1536, 2816))
ROPE_BASE = 10000.0
EPS = 1e-6
IN_EFF = 2048
HALO = 8
SCAN_RUN = 4
VT_ROWS = 80
ATT_TQ = 256
TOKEN_TILE = 512
MOD_ROWS = 16
LOG2E = math.log2(math.e)

VMEM_LIMIT_BYTES = 56 * 1024 * 1024

_NT = (((1,), (1,)), ((), ()))


def _params(*sem):
    return pltpu.CompilerParams(dimension_semantics=sem, vmem_limit_bytes=VMEM_LIMIT_BYTES)


def _resident(shape):
    zeros = (0,) * len(shape)
    return pl.BlockSpec(shape, lambda *_: zeros, pipeline_mode=pl.Buffered(1))


def _dot(a, b):
    return jnp.dot(a, b, preferred_element_type=F32)


def _dot_nt(a, b):
    return lax.dot_general(a, b, _NT, preferred_element_type=F32)


def _rms_rows(x, width):
    ms = jnp.sum(x * x, axis=-1, keepdims=True) * (1.0 / width)
    return x * lax.rsqrt(ms + EPS)


def _store_vt(vt_ref, v):
    vt = v.T
    rows = v.shape[0]
    pad = VT_ROWS - HEAD_V
    ones_row = jnp.where(lax.broadcasted_iota(jnp.int32, (pad, rows), 0) == 0, 1.0, 0.0).astype(BF16)
    for hh in range(vt_ref.shape[0]):
        vt_ref[hh, 0:HEAD_V, :] = vt[hh * HEAD_V:(hh + 1) * HEAD_V, :].astype(BF16)
        vt_ref[hh, HEAD_V:VT_ROWS, :] = ones_row


class _Mod:
    def __init__(self, table, row0, shared):
        self.table, self.row0, self.shared = table, row0, shared

    def spec(self, batch_of):
        row0 = self.row0
        if self.shared:
            return pl.BlockSpec((1, 1, 6 * D_MODEL), lambda *g: (row0, 0, 0))
        return pl.BlockSpec((1, 1, 6 * D_MODEL), lambda *g: (row0 + batch_of(*g), 0, 0))


class _LayerWeights:
    def __init__(self, stacked, layer):
        self.stacked, self.layer = stacked, layer

    def __getitem__(self, name):
        return self.stacked[name]

    def spec(self, name):
        layer = self.layer
        _, rows, cols = self.stacked[name].shape
        return pl.BlockSpec((None, rows, cols), lambda *_: (layer, 0, 0), pipeline_mode=pl.Buffered(1))


def _ada_kernel(cond_ref, w_ref, b_ref, out_ref):
    c = cond_ref[...]
    s = c * jax.nn.sigmoid(c)
    out_ref[0] = _dot(s.astype(BF16), w_ref[0].astype(BF16)) + b_ref[0]


def _ada(cond_all, w_ada, b_ada):
    rows = cond_all.shape[0]
    tn = 1536
    return pl.pallas_call(
        _ada_kernel,
        grid=(DEPTH, 6 * D_MODEL // tn),
        in_specs=[
            pl.BlockSpec((rows, D_MODEL), lambda l, j: (0, 0)),
            pl.BlockSpec((1, D_MODEL, tn), lambda l, j: (l, 0, j)),
            pl.BlockSpec((1, 1, tn), lambda l, j: (l, 0, j)),
        ],
        out_specs=pl.BlockSpec((1, rows, tn), lambda l, j: (l, 0, j)),
        out_shape=jax.ShapeDtypeStruct((DEPTH, rows, 6 * D_MODEL), F32),
        compiler_params=_params("arbitrary", "arbitrary"),
        name="ada_mod",
    )(cond_all, w_ada, b_ada.reshape(DEPTH, 1, 6 * D_MODEL))


def _inproj_kernel(x_ref, mod_ref, g1_ref, win_ref, gq_ref, gkv_ref, wq_ref, wqr_ref, wk_ref, wv_ref,
                   cosq_ref, sinq_ref, cosk_ref, sink_ref, cosd_ref, sina_ref, sinb_ref,
                   q_out, k_out, vt_out, lru_out, pool_out, dq_out, dk_out, dvt_out, *cache_outs):
    x = x_ref[...]
    mod = mod_ref[0]
    sh1 = mod[:, 0:D_MODEL]
    sc1 = mod[:, D_MODEL:2 * D_MODEL]
    h = _rms_rows(x, D_MODEL) * g1_ref[...]
    hb = (h * (1.0 + sc1) + sh1).astype(BF16)

    t01 = _dot(hb, win_ref[:, 0:256])
    lane = lax.broadcasted_iota(jnp.int32, (1, 256), 1)
    cq = jnp.where(lane < MLA_Q_RANK, t01, 0.0)
    cqn = (_rms_rows(cq, MLA_Q_RANK) * gq_ref[...]).astype(BF16)
    qa = _dot(cqn, wq_ref[...])
    qr = _dot(cqn, wqr_ref[...])
    cosq = cosq_ref[...]
    sinq = sinq_ref[...]
    ckv = _dot(hb, win_ref[:, 256:384])
    lat = _rms_rows(ckv, MLA_KV_RANK) * gkv_ref[...]
    latb = lat.astype(BF16)
    kk = _dot(latb, wk_ref[...])
    _store_vt(vt_out, _dot(latb, wv_ref[...]))
    t1 = t01[:, 128:256]
    t3 = _dot(hb, win_ref[:, 384:512])
    kro = t1 * cosk_ref[...] + t3 * sink_ref[...]
    for hh in range(MLA_HEADS):
        sl = slice(hh * MLA_SLOT, (hh + 1) * MLA_SLOT)
        q_out[hh] = (qa[:, sl] * cosq + qr[:, sl] * sinq).astype(q_out.dtype)
        k_out[hh] = (kk[:, sl] + kro).astype(k_out.dtype)

    lru_out[...] = _dot(hb, win_ref[:, 512:1024])
    pool_out[...] = _dot(hb, win_ref[:, 1024:1280])

    cosd = cosd_ref[...]
    sina = sina_ref[...]
    sinb = sinb_ref[...]

    def rope(t):
        return t * cosd + pltpu.roll(t, 256 - 16, 1) * sina + pltpu.roll(t, 16, 1) * sinb

    dq = _dot(hb, win_ref[:, 1280:1536])
    dk = _dot(hb, win_ref[:, 1536:1792])
    dv = _dot(hb, win_ref[:, 1792:2048])
    dq_out[...] = (rope(dq) * (LOG2E / math.sqrt(DIFF_DIM))).astype(dq_out.dtype)
    dk_out[...] = rope(dk).astype(dk_out.dtype)
    _store_vt(dvt_out, dv)

    if cache_outs:
        lat_out, kr_out, dk_raw_out, dv_raw_out = cache_outs
        lat_out[...] = lat
        kr_out[...] = t1
        dk_raw_out[...] = dk
        dv_raw_out[...] = dv


def _inproj(x, mod, lw, tabs, *, nb, n, emit_cache):
    T = nb * n
    tm = TOKEN_TILE
    npt = n // tm
    row_blk = lambda j, b: b * npt + j

    def tok(width):
        return pl.BlockSpec((tm, width), lambda j, b: (row_blk(j, b), 0))

    def tab(width):
        return pl.BlockSpec((tm, width), lambda j, b: (j, 0))

    head = pl.BlockSpec((MLA_HEADS, tm, MLA_SLOT), lambda j, b: (0, row_blk(j, b), 0))
    vt_spec = pl.BlockSpec((MLA_HEADS, VT_ROWS, tm), lambda j, b: (0, 0, row_blk(j, b)))
    wnames = ("g1", "w_in", "gq", "gkv", "wq", "wqr", "wk", "wv")
    in_specs = [tok(D_MODEL), mod.spec(lambda j, b: b)] + [lw.spec(nm) for nm in wnames] + [
        tab(128), tab(128), tab(128), tab(128), tab(256), tab(256), tab(256)]
    out_specs = [head, head, vt_spec, tok(512), tok(256), tok(256), tok(256), vt_spec]
    vt_shape = jax.ShapeDtypeStruct((MLA_HEADS, VT_ROWS, T), BF16)
    out_shape = [
        jax.ShapeDtypeStruct((MLA_HEADS, T, MLA_SLOT), BF16),
        jax.ShapeDtypeStruct((MLA_HEADS, T, MLA_SLOT), BF16),
        vt_shape,
        jax.ShapeDtypeStruct((T, 512), F32),
        jax.ShapeDtypeStruct((T, 256), F32),
        jax.ShapeDtypeStruct((T, 256), BF16),
        jax.ShapeDtypeStruct((T, 256), BF16),
        vt_shape,
    ]
    if emit_cache:
        out_specs += [tok(128), tok(128), tok(256), tok(256)]
        out_shape += [jax.ShapeDtypeStruct((T, 128), F32), jax.ShapeDtypeStruct((T, 128), F32),
                      jax.ShapeDtypeStruct((T, 256), F32), jax.ShapeDtypeStruct((T, 256), F32)]
    return pl.pallas_call(
        _inproj_kernel,
        grid=(npt, nb),
        in_specs=in_specs,
        out_specs=out_specs,
        out_shape=out_shape,
        compiler_params=_params("arbitrary", "arbitrary"),
        name="inproj_cache" if emit_cache else "inproj",
    )(x, mod.table, *[lw[nm] for nm in wnames],
      tabs["cosq"], tabs["sinq"], tabs["cosk"], tabs["sink"], tabs["cosd"], tabs["sina"], tabs["sinb"])


def _ctx_prep_kernel(ckv_ref, kr_ref, dk_ref, dv_ref, wk_ref, wv_ref, k_out, vt_out, dk_out, dvt_out):
    latb = ckv_ref[...].astype(BF16)
    kk = _dot(latb, wk_ref[...])
    kr = kr_ref[...]
    for hh in range(MLA_HEADS):
        k_out[hh] = (kk[:, hh * MLA_SLOT:(hh + 1) * MLA_SLOT] + kr).astype(k_out.dtype)
    _store_vt(vt_out, _dot(latb, wv_ref[...]))
    dk_out[...] = dk_ref[...].astype(dk_out.dtype)
    _store_vt(dvt_out, dv_ref[...])


def _ctx_prep(ckv, kr_pad, cdk, cdv, lw, *, nb, p):
    T = nb * p
    layer = lw.layer
    cache_row = lambda w: pl.BlockSpec((p, w), lambda b: (b * DEPTH + layer, 0))
    row = lambda w: pl.BlockSpec((p, w), lambda b: (b, 0))
    vt_spec = pl.BlockSpec((MLA_HEADS, VT_ROWS, p), lambda b: (0, 0, b))
    vt_shape = jax.ShapeDtypeStruct((MLA_HEADS, VT_ROWS, T), BF16)
    return pl.pallas_call(
        _ctx_prep_kernel,
        grid=(nb,),
        in_specs=[cache_row(128), cache_row(128), cache_row(256), cache_row(256),
                  lw.spec("wk"), lw.spec("wv")],
        out_specs=[pl.BlockSpec((MLA_HEADS, p, MLA_SLOT), lambda b: (0, b, 0)), vt_spec, row(256), vt_spec],
        out_shape=[jax.ShapeDtypeStruct((MLA_HEADS, T, MLA_SLOT), BF16), vt_shape,
                   jax.ShapeDtypeStruct((T, 256), BF16), vt_shape],
        compiler_params=_params("arbitrary"),
        name="ctx_prep",
    )(ckv, kr_pad, cdk, cdv, lw["wk"], lw["wv"])


SAFE_DENOM = 2.0 ** -60
BOUND_SLACK = 1.02


def _scores(k_new, k_ctx, q):
    sn = _dot_nt(k_new(), q)
    sc = _dot_nt(k_ctx(), q) if k_ctx is not None else None
    return sn, sc


def _exact_shift(k_new, k_ctx, q):
    sn, sc = _scores(k_new, k_ctx, q)
    m = jnp.max(sn, axis=0, keepdims=True)
    if sc is not None:
        m = jnp.maximum(m, jnp.max(sc, axis=0, keepdims=True))
    return m


def _bound_shift(q, key_norm2):
    qf = q.astype(F32)
    ones = jnp.ones((8, q.shape[1]), BF16)
    q_norm2 = _dot_nt(ones, (qf * qf).astype(BF16))[0:1, :]
    return jnp.sqrt(q_norm2 * key_norm2) * BOUND_SLACK


def _max_row_norm2(k_new, k_ctx, col_sum):
    def one(k):
        kf = k.astype(F32)
        return jnp.max(_dot((kf * kf).astype(BF16), col_sum), axis=0, keepdims=True)
    m = one(k_new)
    if k_ctx is not None:
        m = jnp.maximum(m, one(k_ctx))
    return m * BOUND_SLACK


def _exp_stage(e_buf, k_new, k_ctx, q, shift, n_ctx):
    sn, sc = _scores(k_new, k_ctx, q)
    en = jnp.exp2(sn - shift)
    e_buf[n_ctx:, :] = en.astype(BF16)
    denom = jnp.sum(en, axis=0, keepdims=True)
    if sc is not None:
        ec = jnp.exp2(sc - shift)
        e_buf[0:n_ctx, :] = ec.astype(BF16)
        denom = denom + jnp.sum(ec, axis=0, keepdims=True)
    return denom


def _value_stage(e_buf, vt_new, vt_ctx, n_ctx):
    o = _dot(vt_new(), e_buf[n_ctx:, :])
    if vt_ctx is not None:
        o = o + _dot(vt_ctx(), e_buf[0:n_ctx, :])
    return o


def _run_pipeline(n_maps, exp_stage, value_stage):
    exp_stage(0)
    for u in range(n_maps):
        if u + 1 < n_maps:
            exp_stage(u + 1)
        value_stage(u)


def _att_scratch(nk):
    return [pltpu.VMEM((8, 128), F32),
            pltpu.VMEM((MLA_HEADS * HEAD_V, ATT_TQ), F32),
            pltpu.VMEM((nk, ATT_TQ), BF16), pltpu.VMEM((nk, ATT_TQ), BF16)]


def _att_nsub(n):
    return 2 if n % (2 * ATT_TQ) == 0 else 1


def _mla_attn_kernel(*refs, has_ctx, nsub):
    if has_ctx:
        q_ref, k_ref, vt_ref, kc_ref, vtc_ref, o_ref, kn2, ot, e0, e1 = refs
        n_ctx = kc_ref.shape[1]
    else:
        q_ref, k_ref, vt_ref, o_ref, kn2, ot, e0, e1 = refs
        n_ctx = 0
    e_bufs = (e0, e1)

    @pl.when(pl.program_id(1) == 0)
    def _():
        ones = jnp.ones((MLA_SLOT, 128), BF16)
        for hh in range(MLA_HEADS):
            kn2[hh:hh + 1, :] = _max_row_norm2(k_ref[hh], kc_ref[hh] if has_ctx else None, ones)

    def run(exact):
        denoms = {}

        def exp_stage(u):
            t, hh = divmod(u, MLA_HEADS)
            q = q_ref[hh, t * ATT_TQ:(t + 1) * ATT_TQ, :]
            k_new = lambda: k_ref[hh]
            k_ctx = (lambda: kc_ref[hh]) if has_ctx else None
            shift = _exact_shift(k_new, k_ctx, q) if exact else _bound_shift(q, kn2[hh:hh + 1, 0:1])
            denoms[u] = _exp_stage(e_bufs[u % 2], k_new, k_ctx, q, shift, n_ctx)

        def value_stage(u):
            t, hh = divmod(u, MLA_HEADS)
            o = _value_stage(e_bufs[u % 2], lambda: vt_ref[hh, 0:HEAD_V, :],
                             (lambda: vtc_ref[hh, 0:HEAD_V, :]) if has_ctx else None, n_ctx)
            ot[hh * HEAD_V:(hh + 1) * HEAD_V, :] = o * (1.0 / denoms[u])
            if hh == MLA_HEADS - 1:
                o_ref[t * ATT_TQ:(t + 1) * ATT_TQ, :] = ot[...].T

        _run_pipeline(nsub * MLA_HEADS, exp_stage, value_stage)
        return jnp.min(functools.reduce(jnp.minimum, denoms.values()))

    denom_min = run(exact=False)

    @pl.when(jnp.logical_not(denom_min >= SAFE_DENOM))
    def _():
        run(exact=True)


def _mla_attn(q, k, vt, ctx, *, nb, n):
    nsub = _att_nsub(n)
    tq = nsub * ATT_TQ
    npt = n // tq
    H, S = MLA_HEADS, MLA_SLOT
    in_specs = [
        pl.BlockSpec((H, tq, S), lambda b, j: (0, b * npt + j, 0)),
        pl.BlockSpec((H, n, S), lambda b, j: (0, b, 0)),
        pl.BlockSpec((H, VT_ROWS, n), lambda b, j: (0, 0, b)),
    ]
    args = [q, k, vt]
    n_ctx = 0
    if ctx is not None:
        n_ctx = ctx[0].shape[1] // nb
        in_specs += [
            pl.BlockSpec((H, n_ctx, S), lambda b, j: (0, b, 0)),
            pl.BlockSpec((H, VT_ROWS, n_ctx), lambda b, j: (0, 0, b)),
        ]
        args += list(ctx)
    return pl.pallas_call(
        functools.partial(_mla_attn_kernel, has_ctx=ctx is not None, nsub=nsub),
        grid=(nb, npt),
        in_specs=in_specs,
        out_specs=pl.BlockSpec((tq, 256), lambda b, j: (b * npt + j, 0)),
        out_shape=jax.ShapeDtypeStruct((nb * n, 256), F32),
        scratch_shapes=_att_scratch(n + n_ctx),
        compiler_params=_params("arbitrary", "arbitrary"),
        name="mla_attn_ctx" if ctx is not None else "mla_attn",
    )(*args)


def _diff_attn_kernel(*refs, has_ctx, nsub, lam_init):
    if has_ctx:
        lv_ref, g_ref, q_ref, k_ref, vt_ref, kc_ref, vtc_ref, o_ref, kn2, ot, e0, e1 = refs
        n_ctx = kc_ref.shape[0]
    else:
        lv_ref, g_ref, q_ref, k_ref, vt_ref, o_ref, kn2, ot, e0, e1 = refs
        n_ctx = 0
    e_bufs = (e0, e1)
    lv = lv_ref[...]
    lam = (jnp.exp(jnp.sum(lv[0:1] * lv[1:2], axis=-1, keepdims=True))
           - jnp.exp(jnp.sum(lv[2:3] * lv[3:4], axis=-1, keepdims=True)) + lam_init)
    lane128 = lax.broadcasted_iota(jnp.int32, (1, 128), 1)
    n_pairs = 2 * DIFF_HEADS

    @pl.when(pl.program_id(1) == 0)
    def _():
        dim = lax.broadcasted_iota(jnp.int32, (256, 128), 0)
        col = lax.broadcasted_iota(jnp.int32, (256, 128), 1)
        indicator = jnp.where(dim // DIFF_DIM == col, 1.0, 0.0).astype(BF16)
        kn2[0:1, :] = _max_row_norm2(k_ref[...], kc_ref[...] if has_ctx else None, indicator)

    def run(exact):
        denoms = {}
        outs = {}

        def exp_stage(u):
            t, p = divmod(u, n_pairs)
            tile = slice((p * DIFF_DIM // 128) * 128, (p * DIFF_DIM // 128 + 1) * 128)
            k_new = lambda: k_ref[:, tile]
            k_ctx = (lambda: kc_ref[:, tile]) if has_ctx else None
            q = q_ref[t * ATT_TQ:(t + 1) * ATT_TQ, tile]
            lo = p * DIFF_DIM - tile.start
            in_pair = (lane128 >= lo) & (lane128 < lo + DIFF_DIM)
            qm = jnp.where(in_pair, q, jnp.zeros_like(q))
            shift = _exact_shift(k_new, k_ctx, qm) if exact else _bound_shift(qm, kn2[0:1, p:p + 1])
            denoms[u] = _exp_stage(e_bufs[u % 2], k_new, k_ctx, qm, shift, n_ctx)

        def value_stage(u):
            t, p = divmod(u, n_pairs)
            hh = p // 2
            o = _value_stage(e_bufs[u % 2], lambda: vt_ref[hh, 0:HEAD_V, :],
                             (lambda: vtc_ref[hh, 0:HEAD_V, :]) if has_ctx else None, n_ctx)
            outs[u] = (o, denoms[u])
            if p % 2 == 1:
                (o0, l0), (o1, l1) = outs.pop(u - 1), outs.pop(u)
                o = o0 * (1.0 / l0) - o1 * (lam / l1)
                msq = jnp.sum(o * o, axis=0, keepdims=True) * (1.0 / HEAD_V)
                ot[hh * HEAD_V:(hh + 1) * HEAD_V, :] = o * lax.rsqrt(msq + EPS)
            if p == n_pairs - 1:
                o_ref[t * ATT_TQ:(t + 1) * ATT_TQ, :] = (ot[...].T * g_ref[...]) * (1.0 - lam_init)

        _run_pipeline(nsub * n_pairs, exp_stage, value_stage)
        return jnp.min(functools.reduce(jnp.minimum, denoms.values()))

    denom_min = run(exact=False)

    @pl.when(jnp.logical_not(denom_min >= SAFE_DENOM))
    def _():
        run(exact=True)


def _diff_attn(q, k, vt, ctx, lw, *, nb, n, lam_init):
    nsub = _att_nsub(n)
    tq = nsub * ATT_TQ
    npt = n // tq
    in_specs = [
        lw.spec("diff_lambda"),
        lw.spec("diff_g"),
        pl.BlockSpec((tq, 256), lambda b, j: (b * npt + j, 0)),
        pl.BlockSpec((n, 256), lambda b, j: (b, 0)),
        pl.BlockSpec((DIFF_HEADS, VT_ROWS, n), lambda b, j: (0, 0, b)),
    ]
    args = [lw["diff_lambda"], lw["diff_g"], q, k, vt]
    n_ctx = 0
    if ctx is not None:
        n_ctx = ctx[0].shape[0] // nb
        in_specs += [pl.BlockSpec((n_ctx, 256), lambda b, j: (b, 0)),
                     pl.BlockSpec((DIFF_HEADS, VT_ROWS, n_ctx), lambda b, j: (0, 0, b))]
        args += list(ctx)
    return pl.pallas_call(
        functools.partial(_diff_attn_kernel, has_ctx=ctx is not None, nsub=nsub, lam_init=lam_init),
        grid=(nb, npt),
        in_specs=in_specs,
        out_specs=pl.BlockSpec((tq, 256), lambda b, j: (b * npt + j, 0)),
        out_shape=jax.ShapeDtypeStruct((nb * n, 256), F32),
        scratch_shapes=_att_scratch(n + n_ctx),
        compiler_params=_params("arbitrary", "arbitrary"),
        name="diff_attn_ctx" if ctx is not None else "diff_attn",
    )(*args)


def _shift_rows(v, k):
    return pltpu.roll(v, (-k) % v.shape[0], 0)


def _scan_strided(a_ref, b_ref, h_ref, row0, carry, n_rows, reverse):
    sub = lax.broadcasted_iota(jnp.int32, (8, 128), 0)
    span = 8 * SCAN_RUN
    order = tuple(range(SCAN_RUN))[::-1] if reverse else tuple(range(SCAN_RUN))
    starts = tuple(range(0, n_rows, span))[::-1] if reverse else tuple(range(0, n_rows, span))
    carries = []
    for lt in range(a_ref.shape[0]):
        c_in = carry[:, lt * 128:(lt + 1) * 128]
        for start in starts:
            tile = lambda ref, g: ref[lt, pl.ds(row0 + start + g, 8, stride=SCAN_RUN), :]
            a = [tile(a_ref, g) for g in range(SCAN_RUN)]
            b = [tile(b_ref, g) for g in range(SCAN_RUN)]
            h = {order[0]: b[order[0]]}
            p = {order[0]: a[order[0]]}
            for prev, g in zip(order, order[1:]):
                h[g] = a[g] * h[prev] + b[g]
                p[g] = a[g] * p[prev]
            pi, hi = p[order[-1]], h[order[-1]]
            for s in (1, 2, 4):
                shift = 8 - s if reverse else s
                valid = (sub < 8 - s) if reverse else (sub >= s)
                pr, hr = pltpu.roll(pi, shift, 0), pltpu.roll(hi, shift, 0)
                hi = jnp.where(valid, pi * hr + hi, hi)
                pi = jnp.where(valid, pi * pr, pi)
            one = 7 if reverse else 1
            first = (sub == 7) if reverse else (sub == 0)
            pe = jnp.where(first, 1.0, pltpu.roll(pi, one, 0))
            he = jnp.where(first, 0.0, pltpu.roll(hi, one, 0))
            c = pe * c_in + he
            for g in range(SCAN_RUN):
                h_ref[lt, pl.ds(start + g, 8, stride=SCAN_RUN), :] = h[g] + p[g] * c
            last = 0 if reverse else 7
            c_in = pi[last:last + 1, :] * c_in + hi[last:last + 1, :]
        carries.append(c_in)
    return jnp.concatenate(carries, axis=1)


def _sigmoid(x):
    return 0.5 * jnp.tanh(0.5 * x) + 0.5


def _gelu_tanh(x):
    return x * (0.5 * (1.0 + jnp.tanh(math.sqrt(2.0 / math.pi) * (x + 0.044715 * (x * x * x)))))


def _lru_kernel(u_ref, h0_ref, cw_ref, cb_ref, wg_ref, bg_ref, lam_ref, y_ref, st_ref,
                xpad, a1s, b1s, a0c, b0c, hc, *, N, T):
    W = LRU_WIDTH
    nc = N // T
    tiles = [slice(lt * 128, (lt + 1) * 128) for lt in range(W // 128)]
    zeros = jnp.zeros((HALO, W), F32)
    xpad[0:HALO, :] = zeros
    xpad[N + HALO:N + 2 * HALO, :] = zeros

    def fill(j, carry):
        r0 = pl.multiple_of(j * T, T)
        xpad[pl.ds(r0 + HALO, T), :] = u_ref[pl.ds(r0, T), 0:W]
        return carry

    lax.fori_loop(0, nc, fill, 0)

    z = -lam_ref[...]
    sp = jnp.maximum(z, 0.0) + jnp.log1p(jnp.exp(-jnp.abs(z)))
    cw = cw_ref[...]
    cb = cb_ref[...]
    bg = bg_ref[...]

    def fwd(j, carry):
        r0 = pl.multiple_of(j * T, T)
        ext = xpad[pl.ds(r0, T + 2 * HALO), :]
        body = slice(HALO, HALO + T)
        xc = cb
        for tap in range(4):
            xc = xc + _shift_rows(ext, tap - 1)[body] * cw[tap:tap + 1]
        g = _sigmoid(_dot(xc.astype(BF16), wg_ref[...]) + bg)
        ab = []
        for d in range(2):
            r = g[:, d * W:(d + 1) * W]
            i = g[:, (2 + d) * W:(3 + d) * W]
            log_a = (-LRU_C * r) * sp[d:d + 1]
            a = jnp.exp(log_a)
            bt = (jnp.sqrt(1.0 - a * a) * i) * xc
            ab.append((a, bt))
        for lt, lanes in enumerate(tiles):
            a0c[lt] = ab[0][0][:, lanes]
            b0c[lt] = ab[0][1][:, lanes]
            a1s[lt, pl.ds(r0, T), :] = ab[1][0][:, lanes]
            b1s[lt, pl.ds(r0, T), :] = ab[1][1][:, lanes]
        carry = _scan_strided(a0c, b0c, hc, 0, carry, T, reverse=False)
        for lt, lanes in enumerate(tiles):
            y_ref[pl.ds(r0, T), lanes] = hc[lt]
        return carry

    cf = lax.fori_loop(0, nc, fwd, h0_ref[0, 0:1, :])

    def bwd(jj, carry):
        r0 = pl.multiple_of((nc - 1 - jj) * T, T)
        carry = _scan_strided(a1s, b1s, hc, r0, carry, T, reverse=True)
        for lt, lanes in enumerate(tiles):
            gb = u_ref[pl.ds(r0, T), W + lt * 128:W + (lt + 1) * 128]
            y_ref[pl.ds(r0, T), lanes] = (y_ref[pl.ds(r0, T), lanes] + hc[lt]) * _gelu_tanh(gb)
        return carry

    cbw = lax.fori_loop(0, nc, bwd, h0_ref[0, 1:2, :])
    st_ref[0, 0:1, :] = cf
    st_ref[0, 1:2, :] = cbw


def _lru(u, h0, h0_block, lw, *, nb, n):
    T = min(n, 256)
    W = LRU_WIDTH
    return pl.pallas_call(
        functools.partial(_lru_kernel, N=n, T=T),
        grid=(nb,),
        in_specs=[
            pl.BlockSpec((n, 2 * W), lambda b: (b, 0)),
            pl.BlockSpec((1, 2, W), lambda b: (h0_block(b), 0, 0)),
            lw.spec("conv_w"), lw.spec("conv_b"), lw.spec("w_gate"), lw.spec("b_gate"),
            lw.spec("lru_lambda"),
        ],
        out_specs=[
            pl.BlockSpec((n, W), lambda b: (b, 0)),
            pl.BlockSpec((1, 2, W), lambda b: (b, 0, 0)),
        ],
        out_shape=[
            jax.ShapeDtypeStruct((nb * n, W), F32),
            jax.ShapeDtypeStruct((nb, 2, W), F32),
        ],
        scratch_shapes=[
            pltpu.VMEM((n + 2 * HALO, W), F32),
            pltpu.VMEM((W // 128, n, 128), F32),
            pltpu.VMEM((W // 128, n, 128), F32),
            pltpu.VMEM((W // 128, T, 128), F32),
            pltpu.VMEM((W // 128, T, 128), F32),
            pltpu.VMEM((W // 128, T, 128), F32),
        ],
        compiler_params=_params("arbitrary"),
        name="rglru",
    )(u, h0, lw["conv_w"], lw["conv_b"], lw["w_gate"], lw["b_gate"], lw["lru_lambda"])


def _pool_kernel(u_ref, wp_ref, sc_ref, y_ref, xpad, *, N, T):
    W = GROUP_WIDTH
    nc = N // T
    zeros = jnp.zeros((HALO, W), F32)
    xpad[0:HALO, :] = zeros
    xpad[N + HALO:N + 2 * HALO, :] = zeros

    def fill(j, carry):
        r0 = pl.multiple_of(j * T, T)
        xpad[pl.ds(r0 + HALO, T), :] = u_ref[pl.ds(r0, T), :]
        return carry

    lax.fori_loop(0, nc, fill, 0)

    grp = lax.broadcasted_iota(jnp.int32, (1, W), 1) // POOL_CH
    half = jnp.where(grp == 0, 1, jnp.where(grp == 1, 2, jnp.where(grp == 2, 4, 8)))
    scale = sc_ref[...]

    def chunk(j, carry):
        r0 = pl.multiple_of(j * T, T)
        ext = xpad[pl.ds(r0, T + 2 * HALO), :]
        w2 = _shift_rows(ext, -1) + ext
        w4 = _shift_rows(w2, -1) + _shift_rows(w2, 1)
        w8 = _shift_rows(w4, -2) + _shift_rows(w4, 2)
        w16 = _shift_rows(w8, -4) + _shift_rows(w8, 4)
        ws = jnp.where(grp == 0, w2, jnp.where(grp == 1, w4, jnp.where(grp == 2, w8, w16)))
        body = slice(HALO, HALO + T)
        t = r0 + lax.broadcasted_iota(jnp.int32, (T, W), 0)
        cnt = (jnp.minimum(t + half, N) - jnp.maximum(t - half, 0)).astype(F32)
        d = ws[body] / cnt - ext[body]
        y_ref[pl.ds(r0, T), :] = _dot(d.astype(BF16), wp_ref[...]) * scale
        return carry

    lax.fori_loop(0, nc, chunk, 0)


def _pool(u, lw, *, nb, n):
    W = GROUP_WIDTH
    T = min(n, 256)
    return pl.pallas_call(
        functools.partial(_pool_kernel, N=n, T=T),
        grid=(nb,),
        in_specs=[pl.BlockSpec((n, W), lambda b: (b, 0)), lw.spec("w_pool"), lw.spec("pool_scale")],
        out_specs=pl.BlockSpec((n, W), lambda b: (b, 0)),
        out_shape=jax.ShapeDtypeStruct((nb * n, W), F32),
        scratch_shapes=[pltpu.VMEM((n + 2 * HALO, W), F32)],
        compiler_params=_params("arbitrary"),
        name="pool_mixer",
    )(u, lw["w_pool"], lw["pool_scale"])


def _mix_ffn_kernel(*refs, final):
    if final:
        (x_ref, ya_ref, yb_ref, yc_ref, yd_ref, mod_ref, g2_ref, wo_ref, wg_ref, wu_ref, wd_ref,
         gf_ref, o_ref) = refs
    else:
        (x_ref, ya_ref, yb_ref, yc_ref, yd_ref, mod_ref, g2_ref, wo_ref, wg_ref, wu_ref, wd_ref,
         o_ref) = refs
    mod = mod_ref[0]
    gate1 = mod[:, 2 * D_MODEL:3 * D_MODEL]
    sh2 = mod[:, 3 * D_MODEL:4 * D_MODEL]
    sc2 = mod[:, 4 * D_MODEL:5 * D_MODEL]
    gate2 = mod[:, 5 * D_MODEL:6 * D_MODEL]
    mix = None
    for i, y_ref in enumerate((ya_ref, yb_ref, yc_ref, yd_ref)):
        part = _dot(y_ref[...].astype(BF16), wo_ref[i * GROUP_WIDTH:(i + 1) * GROUP_WIDTH, :])
        mix = part if mix is None else mix + part
    x1 = x_ref[...] + gate1 * mix
    h = _rms_rows(x1, D_MODEL) * g2_ref[...]
    hb = (h * (1.0 + sc2) + sh2).astype(BF16)
    ff = None
    for lo, hi in FF_CHUNKS:
        g = _dot(hb, wg_ref[:, lo:hi])
        up = _dot(hb, wu_ref[:, lo:hi])
        act = ((g * jax.nn.sigmoid(g)) * up).astype(BF16)
        part = _dot(act, wd_ref[lo:hi, :])
        ff = part if ff is None else ff + part
    x2 = x1 + gate2 * ff
    if final:
        x2 = _rms_rows(x2, D_MODEL) * gf_ref[...]
    o_ref[...] = x2


def _mix_ffn(x, ys, mod, lw, gf, *, nb, n, final):
    T = nb * n
    tm = TOKEN_TILE
    npt = n // tm

    def tok(width):
        return pl.BlockSpec((tm, width), lambda i: (i, 0))

    wnames = ("g2", "w_out", "w_gate_ff", "w_up_ff", "w_down")
    in_specs = [tok(D_MODEL), tok(256), tok(256), tok(256), tok(256),
                mod.spec(lambda i: i // npt)] + [lw.spec(nm) for nm in wnames]
    args = [x, *ys, mod.table] + [lw[nm] for nm in wnames]
    if final:
        in_specs.append(_resident((1, D_MODEL)))
        args.append(gf)
    return pl.pallas_call(
        functools.partial(_mix_ffn_kernel, final=final),
        grid=(T // tm,),
        in_specs=in_specs,
        out_specs=tok(D_MODEL),
        out_shape=jax.ShapeDtypeStruct((T, D_MODEL), F32),
        compiler_params=_params("arbitrary"),
        name="mix_ffn_final" if final else "mix_ffn",
    )(*args)


def _block_diag(w):
    L, G, c, e = w.shape
    return jnp.einsum('lgce,gh->lgche', w, jnp.eye(G, dtype=w.dtype)).reshape(L, G * c, G * e)


def _rot_cols(w):
    return jnp.concatenate([-w[..., 16:32], w[..., 0:16]], axis=-1)


def _stack_weights(p):
    w_in = p["w_in"]
    o1 = MLA_Q_RANK
    o2 = o1 + MLA_KV_RANK
    o3 = o2 + MLA_ROPE
    c_q, c_kv, k_r, rest = w_in[..., :o1], w_in[..., o1:o2], w_in[..., o2:o3], w_in[..., o3:]
    z = lambda n: jnp.zeros((DEPTH, D_MODEL, n), F32)
    w_in_eff = jnp.concatenate([c_q, k_r, z(32), c_kv, z(64), _rot_cols(k_r), z(32), rest], axis=-1)

    w_uq = p["mla_w_uq"]
    qd = MLA_NOPE + MLA_ROPE
    wq_parts, wqr_parts = [], []
    zq = lambda n: jnp.zeros((DEPTH, MLA_Q_RANK, n), F32)
    for h in range(MLA_HEADS):
        wh = w_uq[..., h * qd:(h + 1) * qd]
        wq_parts += [wh, zq(MLA_SLOT - qd)]
        wqr_parts += [zq(MLA_NOPE), _rot_cols(wh[..., MLA_NOPE:]), zq(MLA_SLOT - qd)]
    pad_rows = lambda w: jnp.pad(w, ((0, 0), (0, 256 - MLA_Q_RANK), (0, 0)))
    w_ukv = p["mla_w_ukv"]
    wk_parts, wv_parts = [], []
    zk = jnp.zeros((DEPTH, MLA_KV_RANK, MLA_SLOT - MLA_NOPE), F32)
    for h in range(MLA_HEADS):
        base = h * (MLA_NOPE + MLA_V)
        wk_parts += [w_ukv[..., base:base + MLA_NOPE], zk]
        wv_parts.append(w_ukv[..., base + MLA_NOPE:base + MLA_NOPE + MLA_V])

    w_r, w_i, b_r, b_i = p["lru_w_r"], p["lru_w_i"], p["lru_b_r"], p["lru_b_i"]
    w_gate = jnp.concatenate([_block_diag(w_r[:, 0]), _block_diag(w_r[:, 1]),
                              _block_diag(w_i[:, 0]), _block_diag(w_i[:, 1])], axis=-1)
    b_gate = jnp.concatenate([b_r[:, 0], b_r[:, 1], b_i[:, 0], b_i[:, 1]], axis=-1)
    w_gu = p["w_gu"]
    row = lambda v: v[:, None, :]
    return {
        "g1": row(p["norm1_g"]),
        "g2": row(p["norm2_g"]),
        "w_in": w_in_eff.astype(BF16),
        "gq": row(jnp.pad(p["mla_q_norm_g"], ((0, 0), (0, 256 - MLA_Q_RANK)))),
        "gkv": row(p["mla_kv_norm_g"]),
        "wq": pad_rows(jnp.concatenate(wq_parts, axis=-1)).astype(BF16),
        "wqr": pad_rows(jnp.concatenate(wqr_parts, axis=-1)).astype(BF16),
        "wk": jnp.concatenate(wk_parts, axis=-1).astype(BF16),
        "wv": jnp.concatenate(wv_parts, axis=-1).astype(BF16),
        "conv_w": p["lru_conv_w"],
        "conv_b": row(p["lru_conv_b"]),
        "w_gate": w_gate.astype(BF16),
        "b_gate": row(b_gate),
        "lru_lambda": p["lru_lambda"],
        "w_pool": _block_diag(p["pool_w"]).astype(BF16),
        "pool_scale": row(p["pool_scale"]),
        "diff_lambda": p["diff_lambda"],
        "diff_g": row(jnp.tile(p["diff_norm_g"], (1, DIFF_HEADS))),
        "w_out": p["w_out"].astype(BF16),
        "w_gate_ff": w_gu[..., :FF_HIDDEN].astype(BF16),
        "w_up_ff": w_gu[..., FF_HIDDEN:].astype(BF16),
        "w_down": p["w_down"].astype(BF16),
    }


def _rope_tables(n, positional):
    quarter = MLA_ROPE // 4
    if positional:
        t = jnp.arange(n)
        row = (t // GRID_W).astype(F32)
        col = (t % GRID_W).astype(F32)
        inv = ROPE_BASE ** (-jnp.arange(quarter, dtype=F32) / quarter)
        ang = jnp.concatenate([row[:, None] * inv, col[:, None] * inv], axis=-1)
        cos, sin = jnp.cos(ang), jnp.sin(ang)
    else:
        cos, sin = jnp.ones((n, 16), F32), jnp.zeros((n, 16), F32)
    one = lambda w: jnp.ones((n, w), F32)
    zero = lambda w: jnp.zeros((n, w), F32)
    scale = LOG2E / math.sqrt(MLA_NOPE + MLA_ROPE)
    return {
        "cosq": jnp.concatenate([one(64), cos, cos, one(32)], axis=1) * scale,
        "sinq": jnp.concatenate([zero(64), sin, sin, zero(32)], axis=1) * scale,
        "cosk": jnp.concatenate([zero(64), cos, cos, zero(32)], axis=1),
        "sink": jnp.concatenate([zero(64), sin, sin, zero(32)], axis=1),
        "cosd": jnp.tile(jnp.concatenate([cos, cos], axis=1), (1, 8)),
        "sina": jnp.tile(jnp.concatenate([-sin, zero(16)], axis=1), (1, 8)),
        "sinb": jnp.tile(jnp.concatenate([zero(16), sin], axis=1), (1, 8)),
    }


def _layer(x, mod, lw, tabs, layer_idx, ctx, gf, *, nb, n, final):
    emit_cache = ctx is None
    tok_nb, tok_n = (1, nb * n) if mod.shared else (nb, n)
    outs = _inproj(x, mod, lw, tabs, nb=tok_nb, n=tok_n, emit_cache=emit_cache)
    q, k, vt, u_lru, u_pool, dq, dk, dvt = outs[:8]
    lam_init = 0.8 - 0.6 * math.exp(-0.3 * layer_idx)
    if ctx is None:
        h0 = jnp.zeros((1, 2, LRU_WIDTH), F32)
        h0_block = lambda b: 0
        mla_ctx = diff_ctx = None
    else:
        ckv, kr_pad, cdk, cdv, h0 = ctx
        p = ckv.shape[0] // (nb * DEPTH)
        h0_block = lambda b: b * DEPTH + layer_idx
        kc, vtc, dkc, dvtc = _ctx_prep(ckv, kr_pad, cdk, cdv, lw, nb=nb, p=p)
        mla_ctx = (kc, vtc)
        diff_ctx = (dkc, dvtc)
    y_mla = _mla_attn(q, k, vt, mla_ctx, nb=nb, n=n)
    y_lru, st = _lru(u_lru, h0, h0_block, lw, nb=nb, n=n)
    y_pool = _pool(u_pool, lw, nb=nb, n=n)
    y_diff = _diff_attn(dq, dk, dvt, diff_ctx, lw, nb=nb, n=n, lam_init=lam_init)
    x2 = _mix_ffn(x, (y_mla, y_lru, y_pool, y_diff), mod, lw, gf, nb=tok_nb, n=tok_n, final=final)
    cache = (outs[8], outs[9][:, 64:96], outs[10], outs[11], st) if emit_cache else None
    return x2, cache


def kernel(x_prompt, x_sample, cache_mla_ckv, cache_mla_krope, cache_diff_k, cache_diff_v, state_lru,
           c, c_ctx, w_ada, b_ada, norm1_g, norm2_g, w_in, mla_q_norm_g, mla_w_uq, mla_kv_norm_g,
           mla_w_ukv, lru_conv_w, lru_conv_b, lru_w_r, lru_b_r, lru_w_i, lru_b_i, lru_lambda, pool_w,
           pool_scale, diff_lambda, diff_norm_g, w_out, w_gu, w_down, final_norm_g):
    p = {
        "norm1_g": norm1_g, "norm2_g": norm2_g, "w_in": w_in, "mla_q_norm_g": mla_q_norm_g,
        "mla_w_uq": mla_w_uq, "mla_kv_norm_g": mla_kv_norm_g, "mla_w_ukv": mla_w_ukv,
        "lru_conv_w": lru_conv_w, "lru_conv_b": lru_conv_b, "lru_w_r": lru_w_r, "lru_b_r": lru_b_r,
        "lru_w_i": lru_w_i, "lru_b_i": lru_b_i, "lru_lambda": lru_lambda, "pool_w": pool_w,
        "pool_scale": pool_scale, "diff_lambda": diff_lambda, "diff_norm_g": diff_norm_g,
        "w_out": w_out, "w_gu": w_gu, "w_down": w_down,
    }
    Bp, Np, _ = x_prompt.shape
    Bs, Ns, _ = x_sample.shape
    P = cache_mla_ckv.shape[2]

    cond_all = jnp.concatenate([c, c_ctx[None, :], jnp.zeros((MOD_ROWS - Bs - 1, D_MODEL), F32)], axis=0)
    mod_table = _ada(cond_all, w_ada, b_ada).reshape(DEPTH * MOD_ROWS, 1, 6 * D_MODEL)
    tabs_p = _rope_tables(Bp * Np, positional=False)
    tabs_s = _rope_tables(Ns, positional=True)
    kr_pad = jnp.pad(cache_mla_krope, ((0, 0), (0, 0), (0, 0), (MLA_NOPE, MLA_SLOT - MLA_NOPE - MLA_ROPE)))
    flat = lambda a, w: a.reshape(Bs * DEPTH * P, w)
    ctx = (flat(cache_mla_ckv, MLA_KV_RANK), flat(kr_pad, MLA_SLOT), flat(cache_diff_k, 256),
           flat(cache_diff_v, 256), state_lru.reshape(Bs * DEPTH, 2, LRU_WIDTH))
    gf = final_norm_g[None, :]
    stacked = _stack_weights(p)

    xp = x_prompt.reshape(Bp * Np, D_MODEL)
    xs = x_sample.reshape(Bs * Ns, D_MODEL)
    caches = []
    for l in range(DEPTH):
        lw = _LayerWeights(stacked, l)
        final = l == DEPTH - 1
        mod_p = _Mod(mod_table, l * MOD_ROWS + Bs, shared=True)
        mod_s = _Mod(mod_table, l * MOD_ROWS, shared=False)
        xp, cache = _layer(xp, mod_p, lw, tabs_p, l, None, gf, nb=Bp, n=Np, final=final)
        caches.append(cache)
        xs, _ = _layer(xs, mod_s, lw, tabs_s, l, ctx, gf, nb=Bs, n=Ns, final=final)

    stack = lambda i, w: jnp.stack([cc[i].reshape(Bp, Np, w) for cc in caches], axis=1)
    new_mla_ckv = stack(0, MLA_KV_RANK)
    new_mla_krope = stack(1, MLA_ROPE)
    new_diff_k = stack(2, 256).reshape(Bp, DEPTH, Np, DIFF_HEADS, 2, DIFF_DIM)
    new_diff_v = stack(3, 256).reshape(Bp, DEPTH, Np, DIFF_HEADS, 2 * DIFF_DIM)
    new_state_lru = jnp.stack([cc[4] for cc in caches], axis=1)
    return (xp.reshape(Bp, Np, D_MODEL), xs.reshape(Bs, Ns, D_MODEL),
            new_mla_ckv, new_mla_krope, new_diff_k, new_diff_v, new_state_lru)
```

```python
import functools
import math

import jax
import jax.numpy as jnp
from jax import lax
from jax.experimental import pallas as pl
from jax.experimental.pallas import tpu as pltpu

F32 = jnp.float32
BF16 = jnp.bfloat16

D_MODEL = 1024
DEPTH = 2
GRID_W = 64
GROUP_WIDTH = 256
MLA_HEADS = 4
MLA_NOPE = 64
MLA_ROPE = 32
MLA_V = 64
MLA_Q_RANK = 192
MLA_KV_RANK = 128
MLA_SLOT = 128
LRU_WIDTH = 256
LRU_C = 8.0
POOL_WINDOWS = (2, 4, 8, 16)
POOL_CH = 64
DIFF_HEADS = 4
DIFF_DIM = 32
HEAD_V = 64
FF_HIDDEN = 2816
FF_CHUNKS = ((0, 1536), (1536, 2816))
ROPE_BASE = 10000.0
EPS = 1e-6
IN_EFF = 2048
HALO = 8
SCAN_RUN = 4
VT_ROWS = 80
ATT_TQ = 256
TOKEN_TILE = 512
MOD_ROWS = 16
LOG2E = math.log2(math.e)

VMEM_LIMIT_BYTES = 56 * 1024 * 1024

_NT = (((1,), (1,)), ((), ()))


def _params(*sem):
    return pltpu.CompilerParams(dimension_semantics=sem, vmem_limit_bytes=VMEM_LIMIT_BYTES)


def _resident(shape):
    zeros = (0,) * len(shape)
    return pl.BlockSpec(shape, lambda *_: zeros, pipeline_mode=pl.Buffered(1))


def _dot(a, b):
    return jnp.dot(a, b, preferred_element_type=F32)


def _dot_nt(a, b):
    return lax.dot_general(a, b, _NT, preferred_element_type=F32)


def _rms_rows(x, width):
    ms = jnp.sum(x * x, axis=-1, keepdims=True) * (1.0 / width)
    return x * lax.rsqrt(ms + EPS)


def _store_vt(vt_ref, v):
    vt = v.T
    rows = v.shape[0]
    pad = VT_ROWS - HEAD_V
    ones_row = jnp.where(lax.broadcasted_iota(jnp.int32, (pad, rows), 0) == 0, 1.0, 0.0).astype(BF16)
    for hh in range(vt_ref.shape[0]):
        vt_ref[hh, 0:HEAD_V, :] = vt[hh * HEAD_V:(hh + 1) * HEAD_V, :].astype(BF16)
        vt_ref[hh, HEAD_V:VT_ROWS, :] = ones_row


class _Mod:
    def __init__(self, table, row0, shared):
        self.table, self.row0, self.shared = table, row0, shared

    def spec(self, batch_of):
        row0 = self.row0
        if self.shared:
            return pl.BlockSpec((1, 1, 6 * D_MODEL), lambda *g: (row0, 0, 0))
        return pl.BlockSpec((1, 1, 6 * D_MODEL), lambda *g: (row0 + batch_of(*g), 0, 0))


class _LayerWeights:
    def __init__(self, stacked, layer):
        self.stacked, self.layer = stacked, layer

    def __getitem__(self, name):
        return self.stacked[name]

    def spec(self, name):
        layer = self.layer
        _, rows, cols = self.stacked[name].shape
        return pl.BlockSpec((None, rows, cols), lambda *_: (layer, 0, 0), pipeline_mode=pl.Buffered(1))


def _ada_kernel(cond_ref, w_ref, b_ref, out_ref):
    c = cond_ref[...]
    s = c * jax.nn.sigmoid(c)
    out_ref[0] = _dot(s.astype(BF16), w_ref[0].astype(BF16)) + b_ref[0]


def _ada(cond_all, w_ada, b_ada):
    rows = cond_all.shape[0]
    tn = 1536
    return pl.pallas_call(
        _ada_kernel,
        grid=(DEPTH, 6 * D_MODEL // tn),
        in_specs=[
            pl.BlockSpec((rows, D_MODEL), lambda l, j: (0, 0)),
            pl.BlockSpec((1, D_MODEL, tn), lambda l, j: (l, 0, j)),
            pl.BlockSpec((1, 1, tn), lambda l, j: (l, 0, j)),
        ],
        out_specs=pl.BlockSpec((1, rows, tn), lambda l, j: (l, 0, j)),
        out_shape=jax.ShapeDtypeStruct((DEPTH, rows, 6 * D_MODEL), F32),
        compiler_params=_params("arbitrary", "arbitrary"),
        name="ada_mod",
    )(cond_all, w_ada, b_ada.reshape(DEPTH, 1, 6 * D_MODEL))


def _inproj_kernel(x_ref, mod_ref, g1_ref, win_ref, gq_ref, gkv_ref, wq_ref, wqr_ref, wk_ref, wv_ref,
                   cosq_ref, sinq_ref, cosk_ref, sink_ref, cosd_ref, sina_ref, sinb_ref,
                   q_out, k_out, vt_out, lru_out, pool_out, dq_out, dk_out, dvt_out, *cache_outs):
    x = x_ref[...]
    mod = mod_ref[0]
    sh1 = mod[:, 0:D_MODEL]
    sc1 = mod[:, D_MODEL:2 * D_MODEL]
    h = _rms_rows(x, D_MODEL) * g1_ref[...]
    hb = (h * (1.0 + sc1) + sh1).astype(BF16)

    t01 = _dot(hb, win_ref[:, 0:256])
    lane = lax.broadcasted_iota(jnp.int32, (1, 256), 1)
    cq = jnp.where(lane < MLA_Q_RANK, t01, 0.0)
    cqn = (_rms_rows(cq, MLA_Q_RANK) * gq_ref[...]).astype(BF16)
    qa = _dot(cqn, wq_ref[...])
    qr = _dot(cqn, wqr_ref[...])
    cosq = cosq_ref[...]
    sinq = sinq_ref[...]
    ckv = _dot(hb, win_ref[:, 256:384])
    lat = _rms_rows(ckv, MLA_KV_RANK) * gkv_ref[...]
    latb = lat.astype(BF16)
    kk = _dot(latb, wk_ref[...])
    _store_vt(vt_out, _dot(latb, wv_ref[...]))
    t1 = t01[:, 128:256]
    t3 = _dot(hb, win_ref[:, 384:512])
    kro = t1 * cosk_ref[...] + t3 * sink_ref[...]
    for hh in range(MLA_HEADS):
        sl = slice(hh * MLA_SLOT, (hh + 1) * MLA_SLOT)
        q_out[hh] = (qa[:, sl] * cosq + qr[:, sl] * sinq).astype(q_out.dtype)
        k_out[hh] = (kk[:, sl] + kro).astype(k_out.dtype)

    lru_out[...] = _dot(hb, win_ref[:, 512:1024])
    pool_out[...] = _dot(hb, win_ref[:, 1024:1280])

    cosd = cosd_ref[...]
    sina = sina_ref[...]
    sinb = sinb_ref[...]

    def rope(t):
        return t * cosd + pltpu.roll(t, 256 - 16, 1) * sina + pltpu.roll(t, 16, 1) * sinb

    dq = _dot(hb, win_ref[:, 1280:1536])
    dk = _dot(hb, win_ref[:, 1536:1792])
    dv = _dot(hb, win_ref[:, 1792:2048])
    dq_out[...] = (rope(dq) * (LOG2E / math.sqrt(DIFF_DIM))).astype(dq_out.dtype)
    dk_out[...] = rope(dk).astype(dk_out.dtype)
    _store_vt(dvt_out, dv)

    if cache_outs:
        lat_out, kr_out, dk_raw_out, dv_raw_out = cache_outs
        lat_out[...] = lat
        kr_out[...] = t1
        dk_raw_out[...] = dk
        dv_raw_out[...] = dv


def _inproj(x, mod, lw, tabs, *, nb, n, emit_cache):
    T = nb * n
    tm = TOKEN_TILE
    npt = n // tm
    row_blk = lambda j, b: b * npt + j

    def tok(width):
        return pl.BlockSpec((tm, width), lambda j, b: (row_blk(j, b), 0))

    def tab(width):
        return pl.BlockSpec((tm, width), lambda j, b: (j, 0))

    head = pl.BlockSpec((MLA_HEADS, tm, MLA_SLOT), lambda j, b: (0, row_blk(j, b), 0))
    vt_spec = pl.BlockSpec((MLA_HEADS, VT_ROWS, tm), lambda j, b: (0, 0, row_blk(j, b)))
    wnames = ("g1", "w_in", "gq", "gkv", "wq", "wqr", "wk", "wv")
    in_specs = [tok(D_MODEL), mod.spec(lambda j, b: b)] + [lw.spec(nm) for nm in wnames] + [
        tab(128), tab(128), tab(128), tab(128), tab(256), tab(256), tab(256)]
    out_specs = [head, head, vt_spec, tok(512), tok(256), tok(256), tok(256), vt_spec]
    vt_shape = jax.ShapeDtypeStruct((MLA_HEADS, VT_ROWS, T), BF16)
    out_shape = [
        jax.ShapeDtypeStruct((MLA_HEADS, T, MLA_SLOT), BF16),
        jax.ShapeDtypeStruct((MLA_HEADS, T, MLA_SLOT), BF16),
        vt_shape,
        jax.ShapeDtypeStruct((T, 512), F32),
        jax.ShapeDtypeStruct((T, 256), F32),
        jax.ShapeDtypeStruct((T, 256), BF16),
        jax.ShapeDtypeStruct((T, 256), BF16),
        vt_shape,
    ]
    if emit_cache:
        out_specs += [tok(128), tok(128), tok(256), tok(256)]
        out_shape += [jax.ShapeDtypeStruct((T, 128), F32), jax.ShapeDtypeStruct((T, 128), F32),
                      jax.ShapeDtypeStruct((T, 256), F32), jax.ShapeDtypeStruct((T, 256), F32)]
    return pl.pallas_call(
        _inproj_kernel,
        grid=(npt, nb),
        in_specs=in_specs,
        out_specs=out_specs,
        out_shape=out_shape,
        compiler_params=_params("arbitrary", "arbitrary"),
        name="inproj_cache" if emit_cache else "inproj",
    )(x, mod.table, *[lw[nm] for nm in wnames],
      tabs["cosq"], tabs["sinq"], tabs["cosk"], tabs["sink"], tabs["cosd"], tabs["sina"], tabs["sinb"])


def _ctx_prep_kernel(ckv_ref, kr_ref, dk_ref, dv_ref, wk_ref, wv_ref, k_out, vt_out, dk_out, dvt_out):
    latb = ckv_ref[...].astype(BF16)
    kk = _dot(latb, wk_ref[...])
    kr = kr_ref[...]
    for hh in range(MLA_HEADS):
        k_out[hh] = (kk[:, hh * MLA_SLOT:(hh + 1) * MLA_SLOT] + kr).astype(k_out.dtype)
    _store_vt(vt_out, _dot(latb, wv_ref[...]))
    dk_out[...] = dk_ref[...].astype(dk_out.dtype)
    _store_vt(dvt_out, dv_ref[...])


def _ctx_prep(ckv, kr_pad, cdk, cdv, lw, *, nb, p):
    T = nb * p
    layer = lw.layer
    cache_row = lambda w: pl.BlockSpec((p, w), lambda b: (b * DEPTH + layer, 0))
    row = lambda w: pl.BlockSpec((p, w), lambda b: (b, 0))
    vt_spec = pl.BlockSpec((MLA_HEADS, VT_ROWS, p), lambda b: (0, 0, b))
    vt_shape = jax.ShapeDtypeStruct((MLA_HEADS, VT_ROWS, T), BF16)
    return pl.pallas_call(
        _ctx_prep_kernel,
        grid=(nb,),
        in_specs=[cache_row(128), cache_row(128), cache_row(256), cache_row(256),
                  lw.spec("wk"), lw.spec("wv")],
        out_specs=[pl.BlockSpec((MLA_HEADS, p, MLA_SLOT), lambda b: (0, b, 0)), vt_spec, row(256), vt_spec],
        out_shape=[jax.ShapeDtypeStruct((MLA_HEADS, T, MLA_SLOT), BF16), vt_shape,
                   jax.ShapeDtypeStruct((T, 256), BF16), vt_shape],
        compiler_params=_params("arbitrary"),
        name="ctx_prep",
    )(ckv, kr_pad, cdk, cdv, lw["wk"], lw["wv"])


SAFE_DENOM = 2.0 ** -60
BOUND_SLACK = 1.02


def _scores(k_new, k_ctx, q):
    sn = _dot_nt(k_new(), q)
    sc = _dot_nt(k_ctx(), q) if k_ctx is not None else None
    return sn, sc


def _exact_shift(k_new, k_ctx, q):
    sn, sc = _scores(k_new, k_ctx, q)
    m = jnp.max(sn, axis=0, keepdims=True)
    if sc is not None:
        m = jnp.maximum(m, jnp.max(sc, axis=0, keepdims=True))
    return m


def _bound_shift(q, key_norm2):
    qf = q.astype(F32)
    ones = jnp.ones((8, q.shape[1]), BF16)
    q_norm2 = _dot_nt(ones, (qf * qf).astype(BF16))[0:1, :]
    return jnp.sqrt(q_norm2 * key_norm2) * BOUND_SLACK


def _max_row_norm2(k_new, k_ctx, col_sum):
    def one(k):
        kf = k.astype(F32)
        return jnp.max(_dot((kf * kf).astype(BF16), col_sum), axis=0, keepdims=True)
    m = one(k_new)
    if k_ctx is not None:
        m = jnp.maximum(m, one(k_ctx))
    return m * BOUND_SLACK


def _exp_stage(e_buf, k_new, k_ctx, q, shift, n_ctx):
    sn, sc = _scores(k_new, k_ctx, q)
    e_buf[n_ctx:, :] = jnp.exp2(sn - shift).astype(BF16)
    if sc is not None:
        e_buf[0:n_ctx, :] = jnp.exp2(sc - shift).astype(BF16)


def _value_stage(e_buf, vt_new, vt_ctx, n_ctx):
    o = _dot(vt_new(), e_buf[n_ctx:, :])
    if vt_ctx is not None:
        o = o + _dot(vt_ctx(), e_buf[0:n_ctx, :])
    return o


def _run_pipeline(n_maps, exp_stage, value_stage):
    exp_stage(0)
    for u in range(n_maps):
        if u + 1 < n_maps:
            exp_stage(u + 1)
        value_stage(u)


def _att_scratch(nk):
    return [pltpu.VMEM((8, 128), F32),
            pltpu.VMEM((MLA_HEADS * HEAD_V, ATT_TQ), F32),
            pltpu.VMEM((nk, ATT_TQ), BF16), pltpu.VMEM((nk, ATT_TQ), BF16)]


def _att_nsub(n):
    return 2 if n % (2 * ATT_TQ) == 0 else 1


def _mla_attn_kernel(*refs, has_ctx, nsub):
    if has_ctx:
        q_ref, k_ref, vt_ref, kc_ref, vtc_ref, o_ref, kn2, ot, e0, e1 = refs
        n_ctx = kc_ref.shape[1]
    else:
        q_ref, k_ref, vt_ref, o_ref, kn2, ot, e0, e1 = refs
        n_ctx = 0
    e_bufs = (e0, e1)

    @pl.when(pl.program_id(1) == 0)
    def _():
        ones = jnp.ones((MLA_SLOT, 128), BF16)
        for hh in range(MLA_HEADS):
            kn2[hh:hh + 1, :] = _max_row_norm2(k_ref[hh], kc_ref[hh] if has_ctx else None, ones)

    def run(exact):
        denoms = []

        def exp_stage(u):
            t, hh = divmod(u, MLA_HEADS)
            q = q_ref[hh, t * ATT_TQ:(t + 1) * ATT_TQ, :]
            k_new = lambda: k_ref[hh]
            k_ctx = (lambda: kc_ref[hh]) if has_ctx else None
            shift = _exact_shift(k_new, k_ctx, q) if exact else _bound_shift(q, kn2[hh:hh + 1, 0:1])
            _exp_stage(e_bufs[u % 2], k_new, k_ctx, q, shift, n_ctx)

        def value_stage(u):
            t, hh = divmod(u, MLA_HEADS)
            o = _value_stage(e_bufs[u % 2], lambda: vt_ref[hh],
                             (lambda: vtc_ref[hh]) if has_ctx else None, n_ctx)
            denom = o[HEAD_V:HEAD_V + 1, :]
            denoms.append(denom)
            ot[hh * HEAD_V:(hh + 1) * HEAD_V, :] = o[0:HEAD_V, :] * (1.0 / denom)
            if hh == MLA_HEADS - 1:
                o_ref[t * ATT_TQ:(t + 1) * ATT_TQ, :] = ot[...].T

        _run_pipeline(nsub * MLA_HEADS, exp_stage, value_stage)
        return jnp.min(functools.reduce(jnp.minimum, denoms))

    denom_min = run(exact=False)

    @pl.when(jnp.logical_not(denom_min >= SAFE_DENOM))
    def _():
        run(exact=True)


def _mla_attn(q, k, vt, ctx, *, nb, n):
    nsub = _att_nsub(n)
    tq = nsub * ATT_TQ
    npt = n // tq
    H, S = MLA_HEADS, MLA_SLOT
    in_specs = [
        pl.BlockSpec((H, tq, S), lambda b, j: (0, b * npt + j, 0)),
        pl.BlockSpec((H, n, S), lambda b, j: (0, b, 0)),
        pl.BlockSpec((H, VT_ROWS, n), lambda b, j: (0, 0, b)),
    ]
    args = [q, k, vt]
    n_ctx = 0
    if ctx is not None:
        n_ctx = ctx[0].shape[1] // nb
        in_specs += [
            pl.BlockSpec((H, n_ctx, S), lambda b, j: (0, b, 0)),
            pl.BlockSpec((H, VT_ROWS, n_ctx), lambda b, j: (0, 0, b)),
        ]
        args += list(ctx)
    return pl.pallas_call(
        functools.partial(_mla_attn_kernel, has_ctx=ctx is not None, nsub=nsub),
        grid=(nb, npt),
        in_specs=in_specs,
        out_specs=pl.BlockSpec((tq, 256), lambda b, j: (b * npt + j, 0)),
        out_shape=jax.ShapeDtypeStruct((nb * n, 256), F32),
        scratch_shapes=_att_scratch(n + n_ctx),
        compiler_params=_params("arbitrary", "arbitrary"),
        name="mla_attn_ctx" if ctx is not None else "mla_attn",
    )(*args)


def _diff_attn_kernel(*refs, has_ctx, nsub, lam_init):
    if has_ctx:
        lv_ref, g_ref, q_ref, k_ref, vt_ref, kc_ref, vtc_ref, o_ref, kn2, ot, e0, e1 = refs
        n_ctx = kc_ref.shape[0]
    else:
        lv_ref, g_ref, q_ref, k_ref, vt_ref, o_ref, kn2, ot, e0, e1 = refs
        n_ctx = 0
    e_bufs = (e0, e1)
    lv = lv_ref[...]
    lam = (jnp.exp(jnp.sum(lv[0:1] * lv[1:2], axis=-1, keepdims=True))
           - jnp.exp(jnp.sum(lv[2:3] * lv[3:4], axis=-1, keepdims=True)) + lam_init)
    lane128 = lax.broadcasted_iota(jnp.int32, (1, 128), 1)
    n_pairs = 2 * DIFF_HEADS

    @pl.when(pl.program_id(1) == 0)
    def _():
        dim = lax.broadcasted_iota(jnp.int32, (256, 128), 0)
        col = lax.broadcasted_iota(jnp.int32, (256, 128), 1)
        indicator = jnp.where(dim // DIFF_DIM == col, 1.0, 0.0).astype(BF16)
        kn2[0:1, :] = _max_row_norm2(k_ref[...], kc_ref[...] if has_ctx else None, indicator)

    def run(exact):
        denoms = []
        outs = {}

        def exp_stage(u):
            t, p = divmod(u, n_pairs)
            tile = slice((p * DIFF_DIM // 128) * 128, (p * DIFF_DIM // 128 + 1) * 128)
            k_new = lambda: k_ref[:, tile]
            k_ctx = (lambda: kc_ref[:, tile]) if has_ctx else None
            q = q_ref[t * ATT_TQ:(t + 1) * ATT_TQ, tile]
            lo = p * DIFF_DIM - tile.start
            in_pair = (lane128 >= lo) & (lane128 < lo + DIFF_DIM)
            qm = jnp.where(in_pair, q, jnp.zeros_like(q))
            shift = _exact_shift(k_new, k_ctx, qm) if exact else _bound_shift(qm, kn2[0:1, p:p + 1])
            _exp_stage(e_bufs[u % 2], k_new, k_ctx, qm, shift, n_ctx)

        def value_stage(u):
            t, p = divmod(u, n_pairs)
            hh = p // 2
            o = _value_stage(e_bufs[u % 2], lambda: vt_ref[hh],
                             (lambda: vtc_ref[hh]) if has_ctx else None, n_ctx)
            denom = o[HEAD_V:HEAD_V + 1, :]
            denoms.append(denom)
            outs[u] = (o[0:HEAD_V, :], denom)
            if p % 2 == 1:
                (o0, l0), (o1, l1) = outs.pop(u - 1), outs.pop(u)
                o = o0 * (1.0 / l0) - o1 * (lam / l1)
                msq = jnp.sum(o * o, axis=0, keepdims=True) * (1.0 / HEAD_V)
                ot[hh * HEAD_V:(hh + 1) * HEAD_V, :] = o * lax.rsqrt(msq + EPS)
            if p == n_pairs - 1:
                o_ref[t * ATT_TQ:(t + 1) * ATT_TQ, :] = (ot[...].T * g_ref[...]) * (1.0 - lam_init)

        _run_pipeline(nsub * n_pairs, exp_stage, value_stage)
        return jnp.min(functools.reduce(jnp.minimum, denoms))

    denom_min = run(exact=False)

    @pl.when(jnp.logical_not(denom_min >= SAFE_DENOM))
    def _():
        run(exact=True)


def _diff_attn(q, k, vt, ctx, lw, *, nb, n, lam_init):
    nsub = 1
    tq = nsub * ATT_TQ
    npt = n // tq
    in_specs = [
        lw.spec("diff_lambda"),
        lw.spec("diff_g"),
        pl.BlockSpec((tq, 256), lambda b, j: (b * npt + j, 0)),
        pl.BlockSpec((n, 256), lambda b, j: (b, 0)),
        pl.BlockSpec((DIFF_HEADS, VT_ROWS, n), lambda b, j: (0, 0, b)),
    ]
    args = [lw["diff_lambda"], lw["diff_g"], q, k, vt]
    n_ctx = 0
    if ctx is not None:
        n_ctx = ctx[0].shape[0] // nb
        in_specs += [pl.BlockSpec((n_ctx, 256), lambda b, j: (b, 0)),
                     pl.BlockSpec((DIFF_HEADS, VT_ROWS, n_ctx), lambda b, j: (0, 0, b))]
        args += list(ctx)
    return pl.pallas_call(
        functools.partial(_diff_attn_kernel, has_ctx=ctx is not None, nsub=nsub, lam_init=lam_init),
        grid=(nb, npt),
        in_specs=in_specs,
        out_specs=pl.BlockSpec((tq, 256), lambda b, j: (b * npt + j, 0)),
        out_shape=jax.ShapeDtypeStruct((nb * n, 256), F32),
        scratch_shapes=_att_scratch(n + n_ctx),
        compiler_params=_params("arbitrary", "arbitrary"),
        name="diff_attn_ctx" if ctx is not None else "diff_attn",
    )(*args)


def _shift_rows(v, k):
    return pltpu.roll(v, (-k) % v.shape[0], 0)


def _scan_strided(a_ref, b_ref, h_ref, row0, carry, n_rows, reverse):
    sub = lax.broadcasted_iota(jnp.int32, (8, 128), 0)
    span = 8 * SCAN_RUN
    order = tuple(range(SCAN_RUN))[::-1] if reverse else tuple(range(SCAN_RUN))
    starts = tuple(range(0, n_rows, span))[::-1] if reverse else tuple(range(0, n_rows, span))
    carries = []
    for lt in range(a_ref.shape[0]):
        c_in = carry[:, lt * 128:(lt + 1) * 128]
        for start in starts:
            tile = lambda ref, g: ref[lt, pl.ds(row0 + start + g, 8, stride=SCAN_RUN), :]
            a = [tile(a_ref, g) for g in range(SCAN_RUN)]
            b = [tile(b_ref, g) for g in range(SCAN_RUN)]
            h = {order[0]: b[order[0]]}
            p = {order[0]: a[order[0]]}
            for prev, g in zip(order, order[1:]):
                h[g] = a[g] * h[prev] + b[g]
                p[g] = a[g] * p[prev]
            pi, hi = p[order[-1]], h[order[-1]]
            for s in (1, 2, 4):
                shift = 8 - s if reverse else s
                valid = (sub < 8 - s) if reverse else (sub >= s)
                pr, hr = pltpu.roll(pi, shift, 0), pltpu.roll(hi, shift, 0)
                hi = jnp.where(valid, pi * hr + hi, hi)
                pi = jnp.where(valid, pi * pr, pi)
            one = 7 if reverse else 1
            first = (sub == 7) if reverse else (sub == 0)
            pe = jnp.where(first, 1.0, pltpu.roll(pi, one, 0))
            he = jnp.where(first, 0.0, pltpu.roll(hi, one, 0))
            c = pe * c_in + he
            for g in range(SCAN_RUN):
                h_ref[lt, pl.ds(start + g, 8, stride=SCAN_RUN), :] = h[g] + p[g] * c
            last = 0 if reverse else 7
            c_in = pi[last:last + 1, :] * c_in + hi[last:last + 1, :]
        carries.append(c_in)
    return jnp.concatenate(carries, axis=1)


def _sigmoid(x):
    return 0.5 * jnp.tanh(0.5 * x) + 0.5


def _gelu_tanh(x):
    return x * (0.5 * (1.0 + jnp.tanh(math.sqrt(2.0 / math.pi) * (x + 0.044715 * (x * x * x)))))


def _lru_kernel(u_ref, h0_ref, cw_ref, cb_ref, wg_ref, bg_ref, lam_ref, y_ref, st_ref,
                xpad, a1s, b1s, a0c, b0c, hc, *, N, T):
    W = LRU_WIDTH
    nc = N // T
    tiles = [slice(lt * 128, (lt + 1) * 128) for lt in range(W // 128)]
    zeros = jnp.zeros((HALO, W), F32)
    xpad[0:HALO, :] = zeros
    xpad[N + HALO:N + 2 * HALO, :] = zeros

    def fill(j, carry):
        r0 = pl.multiple_of(j * T, T)
        xpad[pl.ds(r0 + HALO, T), :] = u_ref[pl.ds(r0, T), 0:W]
        return carry

    lax.fori_loop(0, nc, fill, 0)

    z = -lam_ref[...]
    sp = jnp.maximum(z, 0.0) + jnp.log1p(jnp.exp(-jnp.abs(z)))
    cw = cw_ref[...]
    cb = cb_ref[...]
    bg = bg_ref[...]

    def fwd(j, carry):
        r0 = pl.multiple_of(j * T, T)
        ext = xpad[pl.ds(r0, T + 2 * HALO), :]
        body = slice(HALO, HALO + T)
        xc = cb
        for tap in range(4):
            xc = xc + _shift_rows(ext, tap - 1)[body] * cw[tap:tap + 1]
        g = _sigmoid(_dot(xc.astype(BF16), wg_ref[...]) + bg)
        ab = []
        for d in range(2):
            r = g[:, d * W:(d + 1) * W]
            i = g[:, (2 + d) * W:(3 + d) * W]
            log_a = (-LRU_C * r) * sp[d:d + 1]
            a = jnp.exp(log_a)
            bt = (jnp.sqrt(1.0 - a * a) * i) * xc
            ab.append((a, bt))
        for lt, lanes in enumerate(tiles):
            a0c[lt] = ab[0][0][:, lanes]
            b0c[lt] = ab[0][1][:, lanes]
            a1s[lt, pl.ds(r0, T), :] = ab[1][0][:, lanes]
            b1s[lt, pl.ds(r0, T), :] = ab[1][1][:, lanes]
        carry = _scan_strided(a0c, b0c, hc, 0, carry, T, reverse=False)
        for lt, lanes in enumerate(tiles):
            y_ref[pl.ds(r0, T), lanes] = hc[lt]
        return carry

    cf = lax.fori_loop(0, nc, fwd, h0_ref[0, 0:1, :])

    def bwd(jj, carry):
        r0 = pl.multiple_of((nc - 1 - jj) * T, T)
        carry = _scan_strided(a1s, b1s, hc, r0, carry, T, reverse=True)
        for lt, lanes in enumerate(tiles):
            gb = u_ref[pl.ds(r0, T), W + lt * 128:W + (lt + 1) * 128]
            y_ref[pl.ds(r0, T), lanes] = (y_ref[pl.ds(r0, T), lanes] + hc[lt]) * _gelu_tanh(gb)
        return carry

    cbw = lax.fori_loop(0, nc, bwd, h0_ref[0, 1:2, :])
    st_ref[0, 0:1, :] = cf
    st_ref[0, 1:2, :] = cbw


def _lru(u, h0, h0_block, lw, *, nb, n):
    T = min(n, 256)
    W = LRU_WIDTH
    return pl.pallas_call(
        functools.partial(_lru_kernel, N=n, T=T),
        grid=(nb,),
        in_specs=[
            pl.BlockSpec((n, 2 * W), lambda b: (b, 0)),
            pl.BlockSpec((1, 2, W), lambda b: (h0_block(b), 0, 0)),
            lw.spec("conv_w"), lw.spec("conv_b"), lw.spec("w_gate"), lw.spec("b_gate"),
            lw.spec("lru_lambda"),
        ],
        out_specs=[
            pl.BlockSpec((n, W), lambda b: (b, 0)),
            pl.BlockSpec((1, 2, W), lambda b: (b, 0, 0)),
        ],
        out_shape=[
            jax.ShapeDtypeStruct((nb * n, W), F32),
            jax.ShapeDtypeStruct((nb, 2, W), F32),
        ],
        scratch_shapes=[
            pltpu.VMEM((n + 2 * HALO, W), F32),
            pltpu.VMEM((W // 128, n, 128), F32),
            pltpu.VMEM((W // 128, n, 128), F32),
            pltpu.VMEM((W // 128, T, 128), F32),
            pltpu.VMEM((W // 128, T, 128), F32),
            pltpu.VMEM((W // 128, T, 128), F32),
        ],
        compiler_params=_params("arbitrary"),
        name="rglru",
    )(u, h0, lw["conv_w"], lw["conv_b"], lw["w_gate"], lw["b_gate"], lw["lru_lambda"])


def _pool_kernel(u_ref, wp_ref, sc_ref, y_ref, xpad, *, N, T):
    W = GROUP_WIDTH
    nc = N // T
    zeros = jnp.zeros((HALO, W), F32)
    xpad[0:HALO, :] = zeros
    xpad[N + HALO:N + 2 * HALO, :] = zeros

    def fill(j, carry):
        r0 = pl.multiple_of(j * T, T)
        xpad[pl.ds(r0 + HALO, T), :] = u_ref[pl.ds(r0, T), :]
        return carry

    lax.fori_loop(0, nc, fill, 0)

    grp = lax.broadcasted_iota(jnp.int32, (1, W), 1) // POOL_CH
    half = jnp.where(grp == 0, 1, jnp.where(grp == 1, 2, jnp.where(grp == 2, 4, 8)))
    scale = sc_ref[...]

    def chunk(j, carry):
        r0 = pl.multiple_of(j * T, T)
        ext = xpad[pl.ds(r0, T + 2 * HALO), :]
        w2 = _shift_rows(ext, -1) + ext
        w4 = _shift_rows(w2, -1) + _shift_rows(w2, 1)
        w8 = _shift_rows(w4, -2) + _shift_rows(w4, 2)
        w16 = _shift_rows(w8, -4) + _shift_rows(w8, 4)
        ws = jnp.where(grp == 0, w2, jnp.where(grp == 1, w4, jnp.where(grp == 2, w8, w16)))
        body = slice(HALO, HALO + T)
        t = r0 + lax.broadcasted_iota(jnp.int32, (T, W), 0)
        cnt = (jnp.minimum(t + half, N) - jnp.maximum(t - half, 0)).astype(F32)
        d = ws[body] / cnt - ext[body]
        y_ref[pl.ds(r0, T), :] = _dot(d.astype(BF16), wp_ref[...]) * scale
        return carry

    lax.fori_loop(0, nc, chunk, 0)


def _pool(u, lw, *, nb, n):
    W = GROUP_WIDTH
    T = min(n, 256)
    return pl.pallas_call(
        functools.partial(_pool_kernel, N=n, T=T),
        grid=(nb,),
        in_specs=[pl.BlockSpec((n, W), lambda b: (b, 0)), lw.spec("w_pool"), lw.spec("pool_scale")],
        out_specs=pl.BlockSpec((n, W), lambda b: (b, 0)),
        out_shape=jax.ShapeDtypeStruct((nb * n, W), F32),
        scratch_shapes=[pltpu.VMEM((n + 2 * HALO, W), F32)],
        compiler_params=_params("arbitrary"),
        name="pool_mixer",
    )(u, lw["w_pool"], lw["pool_scale"])


def _mix_ffn_kernel(*refs, final):
    if final:
        (x_ref, ya_ref, yb_ref, yc_ref, yd_ref, mod_ref, g2_ref, wo_ref, wg_ref, wu_ref, wd_ref,
         gf_ref, o_ref) = refs
    else:
        (x_ref, ya_ref, yb_ref, yc_ref, yd_ref, mod_ref, g2_ref, wo_ref, wg_ref, wu_ref, wd_ref,
         o_ref) = refs
    mod = mod_ref[0]
    gate1 = mod[:, 2 * D_MODEL:3 * D_MODEL]
    sh2 = mod[:, 3 * D_MODEL:4 * D_MODEL]
    sc2 = mod[:, 4 * D_MODEL:5 * D_MODEL]
    gate2 = mod[:, 5 * D_MODEL:6 * D_MODEL]
    mix = None
    for i, y_ref in enumerate((ya_ref, yb_ref, yc_ref, yd_ref)):
        part = _dot(y_ref[...].astype(BF16), wo_ref[i * GROUP_WIDTH:(i + 1) * GROUP_WIDTH, :])
        mix = part if mix is None else mix + part
    x1 = x_ref[...] + gate1 * mix
    h = _rms_rows(x1, D_MODEL) * g2_ref[...]
    hb = (h * (1.0 + sc2) + sh2).astype(BF16)
    ff = None
    for lo, hi in FF_CHUNKS:
        g = _dot(hb, wg_ref[:, lo:hi])
        up = _dot(hb, wu_ref[:, lo:hi])
        act = ((g * jax.nn.sigmoid(g)) * up).astype(BF16)
        part = _dot(act, wd_ref[lo:hi, :])
        ff = part if ff is None else ff + part
    x2 = x1 + gate2 * ff
    if final:
        x2 = _rms_rows(x2, D_MODEL) * gf_ref[...]
    o_ref[...] = x2


def _mix_ffn(x, ys, mod, lw, gf, *, nb, n, final):
    T = nb * n
    tm = TOKEN_TILE
    npt = n // tm

    def tok(width):
        return pl.BlockSpec((tm, width), lambda i: (i, 0))

    wnames = ("g2", "w_out", "w_gate_ff", "w_up_ff", "w_down")
    in_specs = [tok(D_MODEL), tok(256), tok(256), tok(256), tok(256),
                mod.spec(lambda i: i // npt)] + [lw.spec(nm) for nm in wnames]
    args = [x, *ys, mod.table] + [lw[nm] for nm in wnames]
    if final:
        in_specs.append(_resident((1, D_MODEL)))
        args.append(gf)
    return pl.pallas_call(
        functools.partial(_mix_ffn_kernel, final=final),
        grid=(T // tm,),
        in_specs=in_specs,
        out_specs=tok(D_MODEL),
        out_shape=jax.ShapeDtypeStruct((T, D_MODEL), F32),
        compiler_params=_params("arbitrary"),
        name="mix_ffn_final" if final else "mix_ffn",
    )(*args)


def _block_diag(w):
    L, G, c, e = w.shape
    return jnp.einsum('lgce,gh->lgche', w, jnp.eye(G, dtype=w.dtype)).reshape(L, G * c, G * e)


def _rot_cols(w):
    return jnp.concatenate([-w[..., 16:32], w[..., 0:16]], axis=-1)


def _stack_weights(p):
    w_in = p["w_in"]
    o1 = MLA_Q_RANK
    o2 = o1 + MLA_KV_RANK
    o3 = o2 + MLA_ROPE
    c_q, c_kv, k_r, rest = w_in[..., :o1], w_in[..., o1:o2], w_in[..., o2:o3], w_in[..., o3:]
    z = lambda n: jnp.zeros((DEPTH, D_MODEL, n), F32)
    w_in_eff = jnp.concatenate([c_q, k_r, z(32), c_kv, z(64), _rot_cols(k_r), z(32), rest], axis=-1)

    w_uq = p["mla_w_uq"]
    qd = MLA_NOPE + MLA_ROPE
    wq_parts, wqr_parts = [], []
    zq = lambda n: jnp.zeros((DEPTH, MLA_Q_RANK, n), F32)
    for h in range(MLA_HEADS):
        wh = w_uq[..., h * qd:(h + 1) * qd]
        wq_parts += [wh, zq(MLA_SLOT - qd)]
        wqr_parts += [zq(MLA_NOPE), _rot_cols(wh[..., MLA_NOPE:]), zq(MLA_SLOT - qd)]
    pad_rows = lambda w: jnp.pad(w, ((0, 0), (0, 256 - MLA_Q_RANK), (0, 0)))
    w_ukv = p["mla_w_ukv"]
    wk_parts, wv_parts = [], []
    zk = jnp.zeros((DEPTH, MLA_KV_RANK, MLA_SLOT - MLA_NOPE), F32)
    for h in range(MLA_HEADS):
        base = h * (MLA_NOPE + MLA_V)
        wk_parts += [w_ukv[..., base:base + MLA_NOPE], zk]
        wv_parts.append(w_ukv[..., base + MLA_NOPE:base + MLA_NOPE + MLA_V])

    w_r, w_i, b_r, b_i = p["lru_w_r"], p["lru_w_i"], p["lru_b_r"], p["lru_b_i"]
    w_gate = jnp.concatenate([_block_diag(w_r[:, 0]), _block_diag(w_r[:, 1]),
                              _block_diag(w_i[:, 0]), _block_diag(w_i[:, 1])], axis=-1)
    b_gate = jnp.concatenate([b_r[:, 0], b_r[:, 1], b_i[:, 0], b_i[:, 1]], axis=-1)
    w_gu = p["w_gu"]
    row = lambda v: v[:, None, :]
    return {
        "g1": row(p["norm1_g"]),
        "g2": row(p["norm2_g"]),
        "w_in": w_in_eff.astype(BF16),
        "gq": row(jnp.pad(p["mla_q_norm_g"], ((0, 0), (0, 256 - MLA_Q_RANK)))),
        "gkv": row(p["mla_kv_norm_g"]),
        "wq": pad_rows(jnp.concatenate(wq_parts, axis=-1)).astype(BF16),
        "wqr": pad_rows(jnp.concatenate(wqr_parts, axis=-1)).astype(BF16),
        "wk": jnp.concatenate(wk_parts, axis=-1).astype(BF16),
        "wv": jnp.concatenate(wv_parts, axis=-1).astype(BF16),
        "conv_w": p["lru_conv_w"],
        "conv_b": row(p["lru_conv_b"]),
        "w_gate": w_gate.astype(BF16),
        "b_gate": row(b_gate),
        "lru_lambda": p["lru_lambda"],
        "w_pool": _block_diag(p["pool_w"]).astype(BF16),
        "pool_scale": row(p["pool_scale"]),
        "diff_lambda": p["diff_lambda"],
        "diff_g": row(jnp.tile(p["diff_norm_g"], (1, DIFF_HEADS))),
        "w_out": p["w_out"].astype(BF16),
        "w_gate_ff": w_gu[..., :FF_HIDDEN].astype(BF16),
        "w_up_ff": w_gu[..., FF_HIDDEN:].astype(BF16),
        "w_down": p["w_down"].astype(BF16),
    }


def _rope_tables(n, positional):
    quarter = MLA_ROPE // 4
    if positional:
        t = jnp.arange(n)
        row = (t // GRID_W).astype(F32)
        col = (t % GRID_W).astype(F32)
        inv = ROPE_BASE ** (-jnp.arange(quarter, dtype=F32) / quarter)
        ang = jnp.concatenate([row[:, None] * inv, col[:, None] * inv], axis=-1)
        cos, sin = jnp.cos(ang), jnp.sin(ang)
    else:
        cos, sin = jnp.ones((n, 16), F32), jnp.zeros((n, 16), F32)
    one = lambda w: jnp.ones((n, w), F32)
    zero = lambda w: jnp.zeros((n, w), F32)
    scale = LOG2E / math.sqrt(MLA_NOPE + MLA_ROPE)
    return {
        "cosq": jnp.concatenate([one(64), cos, cos, one(32)], axis=1) * scale,
        "sinq": jnp.concatenate([zero(64), sin, sin, zero(32)], axis=1) * scale,
        "cosk": jnp.concatenate([zero(64), cos, cos, zero(32)], axis=1),
        "sink": jnp.concatenate([zero(64), sin, sin, zero(32)], axis=1),
        "cosd": jnp.tile(jnp.concatenate([cos, cos], axis=1), (1, 8)),
        "sina": jnp.tile(jnp.concatenate([-sin, zero(16)], axis=1), (1, 8)),
        "sinb": jnp.tile(jnp.concatenate([zero(16), sin], axis=1), (1, 8)),
    }


def _layer(x, mod, lw, tabs, layer_idx, ctx, gf, *, nb, n, final):
    emit_cache = ctx is None
    tok_nb, tok_n = (1, nb * n) if mod.shared else (nb, n)
    outs = _inproj(x, mod, lw, tabs, nb=tok_nb, n=tok_n, emit_cache=emit_cache)
    q, k, vt, u_lru, u_pool, dq, dk, dvt = outs[:8]
    lam_init = 0.8 - 0.6 * math.exp(-0.3 * layer_idx)
    if ctx is None:
        h0 = jnp.zeros((1, 2, LRU_WIDTH), F32)
        h0_block = lambda b: 0
        mla_ctx = diff_ctx = None
    else:
        ckv, kr_pad, cdk, cdv, h0 = ctx
        p = ckv.shape[0] // (nb * DEPTH)
        h0_block = lambda b: b * DEPTH + layer_idx
        kc, vtc, dkc, dvtc = _ctx_prep(ckv, kr_pad, cdk, cdv, lw, nb=nb, p=p)
        mla_ctx = (kc, vtc)
        diff_ctx = (dkc, dvtc)
    y_mla = _mla_attn(q, k, vt, mla_ctx, nb=nb, n=n)
    y_lru, st = _lru(u_lru, h0, h0_block, lw, nb=nb, n=n)
    y_pool = _pool(u_pool, lw, nb=nb, n=n)
    y_diff = _diff_attn(dq, dk, dvt, diff_ctx, lw, nb=nb, n=n, lam_init=lam_init)
    x2 = _mix_ffn(x, (y_mla, y_lru, y_pool, y_diff), mod, lw, gf, nb=tok_nb, n=tok_n, final=final)
    cache = (outs[8], outs[9][:, 64:96], outs[10], outs[11], st) if emit_cache else None
    return x2, cache


def kernel(x_prompt, x_sample, cache_mla_ckv, cache_mla_krope, cache_diff_k, cache_diff_v, state_lru,
           c, c_ctx, w_ada, b_ada, norm1_g, norm2_g, w_in, mla_q_norm_g, mla_w_uq, mla_kv_norm_g,
           mla_w_ukv, lru_conv_w, lru_conv_b, lru_w_r, lru_b_r, lru_w_i, lru_b_i, lru_lambda, pool_w,
           pool_scale, diff_lambda, diff_norm_g, w_out, w_gu, w_down, final_norm_g):
    p = {
        "norm1_g": norm1_g, "norm2_g": norm2_g, "w_in": w_in, "mla_q_norm_g": mla_q_norm_g,
        "mla_w_uq": mla_w_uq, "mla_kv_norm_g": mla_kv_norm_g, "mla_w_ukv": mla_w_ukv,
        "lru_conv_w": lru_conv_w, "lru_conv_b": lru_conv_b, "lru_w_r": lru_w_r, "lru_b_r": lru_b_r,
        "lru_w_i": lru_w_i, "lru_b_i": lru_b_i, "lru_lambda": lru_lambda, "pool_w": pool_w,
        "pool_scale": pool_scale, "diff_lambda": diff_lambda, "diff_norm_g": diff_norm_g,
        "w_out": w_out, "w_gu": w_gu, "w_down": w_down,
    }
    Bp, Np, _ = x_prompt.shape
    Bs, Ns, _ = x_sample.shape
    P = cache_mla_ckv.shape[2]

    cond_all = jnp.concatenate([c, c_ctx[None, :], jnp.zeros((MOD_ROWS - Bs - 1, D_MODEL), F32)], axis=0)
    mod_table = _ada(cond_all, w_ada, b_ada).reshape(DEPTH * MOD_ROWS, 1, 6 * D_MODEL)
    tabs_p = _rope_tables(Bp * Np, positional=False)
    tabs_s = _rope_tables(Ns, positional=True)
    kr_pad = jnp.pad(cache_mla_krope, ((0, 0), (0, 0), (0, 0), (MLA_NOPE, MLA_SLOT - MLA_NOPE - MLA_ROPE)))
    flat = lambda a, w: a.reshape(Bs * DEPTH * P, w)
    ctx = (flat(cache_mla_ckv, MLA_KV_RANK), flat(kr_pad, MLA_SLOT), flat(cache_diff_k, 256),
           flat(cache_diff_v, 256), state_lru.reshape(Bs * DEPTH, 2, LRU_WIDTH))
    gf = final_norm_g[None, :]
    stacked = _stack_weights(p)

    xp = x_prompt.reshape(Bp * Np, D_MODEL)
    xs = x_sample.reshape(Bs * Ns, D_MODEL)
    caches = []
    for l in range(DEPTH):
        lw = _LayerWeights(stacked, l)
        final = l == DEPTH - 1
        mod_p = _Mod(mod_table, l * MOD_ROWS + Bs, shared=True)
        mod_s = _Mod(mod_table, l * MOD_ROWS, shared=False)
        xp, cache = _layer(xp, mod_p, lw, tabs_p, l, None, gf, nb=Bp, n=Np, final=final)
        caches.append(cache)
        xs, _ = _layer(xs, mod_s, lw, tabs_s, l, ctx, gf, nb=Bs, n=Ns, final=final)

    stack = lambda i, w: jnp.stack([cc[i].reshape(Bp, Np, w) for cc in caches], axis=1)
    new_mla_ckv = stack(0, MLA_KV_RANK)
    new_mla_krope = stack(1, MLA_ROPE)
    new_diff_k = stack(2, 256).reshape(Bp, DEPTH, Np, DIFF_HEADS, 2, DIFF_DIM)
    new_diff_v = stack(3, 256).reshape(Bp, DEPTH, Np, DIFF_HEADS, 2 * DIFF_DIM)
    new_state_lru = jnp.stack([cc[4] for cc in caches], axis=1)
    return (xp.reshape(Bp, Np, D_MODEL), xs.reshape(Bs, Ns, D_MODEL),
            new_mla_ckv, new_mla_krope, new_diff_k, new_diff_v, new_state_lru)
```

```python
import functools
import math

import jax
import jax.numpy as jnp
from jax import lax
from jax.experimental import pallas as pl
from jax.experimental.pallas import tpu as pltpu

F32 = jnp.float32
BF16 = jnp.bfloat16

D_MODEL = 1024
DEPTH = 2
GRID_W = 64
GROUP_WIDTH = 256
MLA_HEADS = 4
MLA_NOPE = 64
MLA_ROPE = 32
MLA_V = 64
MLA_Q_RANK = 192
MLA_KV_RANK = 128
MLA_SLOT = 128
LRU_WIDTH = 256
LRU_C = 8.0
POOL_WINDOWS = (2, 4, 8, 16)
POOL_CH = 64
DIFF_HEADS = 4
DIFF_DIM = 32
HEAD_V = 64
FF_HIDDEN = 2816
FF_CHUNKS = ((0, 1536), (1536, 2816))
ROPE_BASE = 10000.0
EPS = 1e-6
IN_EFF = 2048
HALO = 8
SCAN_RUN = 4
VT_ROWS = 80
ATT_TQ = 256
TOKEN_TILE = 512
MOD_ROWS = 16
LOG2E = math.log2(math.e)

VMEM_LIMIT_BYTES = 56 * 1024 * 1024

_NT = (((1,), (1,)), ((), ()))


def _params(*sem):
    return pltpu.CompilerParams(dimension_semantics=sem, vmem_limit_bytes=VMEM_LIMIT_BYTES)


def _resident(shape):
    zeros = (0,) * len(shape)
    return pl.BlockSpec(shape, lambda *_: zeros, pipeline_mode=pl.Buffered(1))


def _dot(a, b):
    return jnp.dot(a, b, preferred_element_type=F32)


def _dot_nt(a, b):
    return lax.dot_general(a, b, _NT, preferred_element_type=F32)


def _rms_rows(x, width):
    ms = jnp.sum(x * x, axis=-1, keepdims=True) * (1.0 / width)
    return x * lax.rsqrt(ms + EPS)


def _store_vt(vt_ref, v):
    vt = v.T
    rows = v.shape[0]
    pad = VT_ROWS - HEAD_V
    ones_row = jnp.where(lax.broadcasted_iota(jnp.int32, (pad, rows), 0) == 0, 1.0, 0.0).astype(BF16)
    for hh in range(vt_ref.shape[0]):
        vt_ref[hh, 0:HEAD_V, :] = vt[hh * HEAD_V:(hh + 1) * HEAD_V, :].astype(BF16)
        vt_ref[hh, HEAD_V:VT_ROWS, :] = ones_row


class _Mod:
    def __init__(self, table, row0, shared):
        self.table, self.row0, self.shared = table, row0, shared

    def spec(self, batch_of):
        row0 = self.row0
        if self.shared:
            return pl.BlockSpec((1, 1, 6 * D_MODEL), lambda *g: (row0, 0, 0))
        return pl.BlockSpec((1, 1, 6 * D_MODEL), lambda *g: (row0 + batch_of(*g), 0, 0))


class _LayerWeights:
    def __init__(self, stacked, layer):
        self.stacked, self.layer = stacked, layer

    def __getitem__(self, name):
        return self.stacked[name]

    def spec(self, name):
        layer = self.layer
        _, rows, cols = self.stacked[name].shape
        return pl.BlockSpec((None, rows, cols), lambda *_: (layer, 0, 0), pipeline_mode=pl.Buffered(1))


def _ada_kernel(cond_ref, w_ref, b_ref, out_ref):
    c = cond_ref[...]
    s = c * jax.nn.sigmoid(c)
    out_ref[0] = _dot(s.astype(BF16), w_ref[0].astype(BF16)) + b_ref[0]


def _ada(cond_all, w_ada, b_ada):
    rows = cond_all.shape[0]
    tn = 1536
    return pl.pallas_call(
        _ada_kernel,
        grid=(DEPTH, 6 * D_MODEL // tn),
        in_specs=[
            pl.BlockSpec((rows, D_MODEL), lambda l, j: (0, 0)),
            pl.BlockSpec((1, D_MODEL, tn), lambda l, j: (l, 0, j)),
            pl.BlockSpec((1, 1, tn), lambda l, j: (l, 0, j)),
        ],
        out_specs=pl.BlockSpec((1, rows, tn), lambda l, j: (l, 0, j)),
        out_shape=jax.ShapeDtypeStruct((DEPTH, rows, 6 * D_MODEL), F32),
        compiler_params=_params("arbitrary", "arbitrary"),
        name="ada_mod",
    )(cond_all, w_ada, b_ada.reshape(DEPTH, 1, 6 * D_MODEL))


def _inproj_kernel(x_ref, mod_ref, g1_ref, win_ref, gq_ref, gkv_ref, wq_ref, wqr_ref, wk_ref, wv_ref,
                   cosq_ref, sinq_ref, cosk_ref, sink_ref, cosd_ref, sina_ref, sinb_ref,
                   q_out, k_out, vt_out, lru_out, pool_out, dq_out, dk_out, dvt_out, *cache_outs):
    x = x_ref[...]
    mod = mod_ref[0]
    sh1 = mod[:, 0:D_MODEL]
    sc1 = mod[:, D_MODEL:2 * D_MODEL]
    h = _rms_rows(x, D_MODEL) * g1_ref[...]
    hb = (h * (1.0 + sc1) + sh1).astype(BF16)

    t01 = _dot(hb, win_ref[:, 0:256])
    lane = lax.broadcasted_iota(jnp.int32, (1, 256), 1)
    cq = jnp.where(lane < MLA_Q_RANK, t01, 0.0)
    cqn = (_rms_rows(cq, MLA_Q_RANK) * gq_ref[...]).astype(BF16)
    qa = _dot(cqn, wq_ref[...])
    qr = _dot(cqn, wqr_ref[...])
    cosq = cosq_ref[...]
    sinq = sinq_ref[...]
    ckv = _dot(hb, win_ref[:, 256:384])
    lat = _rms_rows(ckv, MLA_KV_RANK) * gkv_ref[...]
    latb = lat.astype(BF16)
    kk = _dot(latb, wk_ref[...])
    _store_vt(vt_out, _dot(latb, wv_ref[...]))
    t1 = t01[:, 128:256]
    t3 = _dot(hb, win_ref[:, 384:512])
    kro = t1 * cosk_ref[...] + t3 * sink_ref[...]
    for hh in range(MLA_HEADS):
        sl = slice(hh * MLA_SLOT, (hh + 1) * MLA_SLOT)
        q_out[hh] = (qa[:, sl] * cosq + qr[:, sl] * sinq).astype(q_out.dtype)
        k_out[hh] = (kk[:, sl] + kro).astype(k_out.dtype)

    lru_out[...] = _dot(hb, win_ref[:, 512:1024])
    pool_out[...] = _dot(hb, win_ref[:, 1024:1280])

    cosd = cosd_ref[...]
    sina = sina_ref[...]
    sinb = sinb_ref[...]

    def rope(t):
        return t * cosd + pltpu.roll(t, 256 - 16, 1) * sina + pltpu.roll(t, 16, 1) * sinb

    dq = _dot(hb, win_ref[:, 1280:1536])
    dk = _dot(hb, win_ref[:, 1536:1792])
    dv = _dot(hb, win_ref[:, 1792:2048])
    dq_out[...] = (rope(dq) * (LOG2E / math.sqrt(DIFF_DIM))).astype(dq_out.dtype)
    dk_out[...] = rope(dk).astype(dk_out.dtype)
    _store_vt(dvt_out, dv)

    if cache_outs:
        lat_out, kr_out, dk_raw_out, dv_raw_out = cache_outs
        lat_out[...] = lat
        kr_out[...] = t1
        dk_raw_out[...] = dk
        dv_raw_out[...] = dv


def _inproj(x, mod, lw, tabs, *, nb, n, emit_cache):
    T = nb * n
    tm = TOKEN_TILE
    npt = n // tm
    row_blk = lambda j, b: b * npt + j

    def tok(width):
        return pl.BlockSpec((tm, width), lambda j, b: (row_blk(j, b), 0))

    def tab(width):
        return pl.BlockSpec((tm, width), lambda j, b: (j, 0))

    head = pl.BlockSpec((MLA_HEADS, tm, MLA_SLOT), lambda j, b: (0, row_blk(j, b), 0))
    vt_spec = pl.BlockSpec((MLA_HEADS, VT_ROWS, tm), lambda j, b: (0, 0, row_blk(j, b)))
    wnames = ("g1", "w_in", "gq", "gkv", "wq", "wqr", "wk", "wv")
    in_specs = [tok(D_MODEL), mod.spec(lambda j, b: b)] + [lw.spec(nm) for nm in wnames] + [
        tab(128), tab(128), tab(128), tab(128), tab(256), tab(256), tab(256)]
    out_specs = [head, head, vt_spec, tok(512), tok(256), tok(256), tok(256), vt_spec]
    vt_shape = jax.ShapeDtypeStruct((MLA_HEADS, VT_ROWS, T), BF16)
    out_shape = [
        jax.ShapeDtypeStruct((MLA_HEADS, T, MLA_SLOT), BF16),
        jax.ShapeDtypeStruct((MLA_HEADS, T, MLA_SLOT), BF16),
        vt_shape,
        jax.ShapeDtypeStruct((T, 512), F32),
        jax.ShapeDtypeStruct((T, 256), F32),
        jax.ShapeDtypeStruct((T, 256), BF16),
        jax.ShapeDtypeStruct((T, 256), BF16),
        vt_shape,
    ]
    if emit_cache:
        out_specs += [tok(128), tok(128), tok(256), tok(256)]
        out_shape += [jax.ShapeDtypeStruct((T, 128), F32), jax.ShapeDtypeStruct((T, 128), F32),
                      jax.ShapeDtypeStruct((T, 256), F32), jax.ShapeDtypeStruct((T, 256), F32)]
    return pl.pallas_call(
        _inproj_kernel,
        grid=(npt, nb),
        in_specs=in_specs,
        out_specs=out_specs,
        out_shape=out_shape,
        compiler_params=_params("arbitrary", "arbitrary"),
        name="inproj_cache" if emit_cache else "inproj",
    )(x, mod.table, *[lw[nm] for nm in wnames],
      tabs["cosq"], tabs["sinq"], tabs["cosk"], tabs["sink"], tabs["cosd"], tabs["sina"], tabs["sinb"])


def _ctx_prep_kernel(ckv_ref, kr_ref, dk_ref, dv_ref, wk_ref, wv_ref, k_out, vt_out, dk_out, dvt_out):
    latb = ckv_ref[...].astype(BF16)
    kk = _dot(latb, wk_ref[...])
    kr = kr_ref[...]
    for hh in range(MLA_HEADS):
        k_out[hh] = (kk[:, hh * MLA_SLOT:(hh + 1) * MLA_SLOT] + kr).astype(k_out.dtype)
    _store_vt(vt_out, _dot(latb, wv_ref[...]))
    dk_out[...] = dk_ref[...].astype(dk_out.dtype)
    _store_vt(dvt_out, dv_ref[...])


def _ctx_prep(ckv, kr_pad, cdk, cdv, lw, *, nb, p):
    T = nb * p
    layer = lw.layer
    cache_row = lambda w: pl.BlockSpec((p, w), lambda b: (b * DEPTH + layer, 0))
    row = lambda w: pl.BlockSpec((p, w), lambda b: (b, 0))
    vt_spec = pl.BlockSpec((MLA_HEADS, VT_ROWS, p), lambda b: (0, 0, b))
    vt_shape = jax.ShapeDtypeStruct((MLA_HEADS, VT_ROWS, T), BF16)
    return pl.pallas_call(
        _ctx_prep_kernel,
        grid=(nb,),
        in_specs=[cache_row(128), cache_row(128), cache_row(256), cache_row(256),
                  lw.spec("wk"), lw.spec("wv")],
        out_specs=[pl.BlockSpec((MLA_HEADS, p, MLA_SLOT), lambda b: (0, b, 0)), vt_spec, row(256), vt_spec],
        out_shape=[jax.ShapeDtypeStruct((MLA_HEADS, T, MLA_SLOT), BF16), vt_shape,
                   jax.ShapeDtypeStruct((T, 256), BF16), vt_shape],
        compiler_params=_params("arbitrary"),
        name="ctx_prep",
    )(ckv, kr_pad, cdk, cdv, lw["wk"], lw["wv"])


SAFE_DENOM = 2.0 ** -60
BOUND_SLACK = 1.02


def _scores(k_new, k_ctx, q):
    sn = _dot_nt(k_new(), q)
    sc = _dot_nt(k_ctx(), q) if k_ctx is not None else None
    return sn, sc


def _exact_shift(k_new, k_ctx, q):
    sn, sc = _scores(k_new, k_ctx, q)
    m = jnp.max(sn, axis=0, keepdims=True)
    if sc is not None:
        m = jnp.maximum(m, jnp.max(sc, axis=0, keepdims=True))
    return m


def _bound_shift(q, key_norm2):
    qf = q.astype(F32)
    ones = jnp.ones((8, q.shape[1]), BF16)
    q_norm2 = _dot_nt(ones, (qf * qf).astype(BF16))[0:1, :]
    return jnp.sqrt(q_norm2 * key_norm2) * BOUND_SLACK


def _max_row_norm2(k_new, k_ctx, col_sum):
    def one(k):
        kf = k.astype(F32)
        return jnp.max(_dot((kf * kf).astype(BF16), col_sum), axis=0, keepdims=True)
    m = one(k_new)
    if k_ctx is not None:
        m = jnp.maximum(m, one(k_ctx))
    return m * BOUND_SLACK


def _exp_stage(e_buf, k_new, k_ctx, q, shift, n_ctx):
    sn, sc = _scores(k_new, k_ctx, q)
    e_buf[n_ctx:, :] = jnp.exp2(sn - shift).astype(BF16)
    if sc is not None:
        e_buf[0:n_ctx, :] = jnp.exp2(sc - shift).astype(BF16)


def _value_stage(e_buf, vt_new, vt_ctx, n_ctx):
    o = _dot(vt_new(), e_buf[n_ctx:, :])
    if vt_ctx is not None:
        o = o + _dot(vt_ctx(), e_buf[0:n_ctx, :])
    return o


def _run_pipeline(n_maps, exp_stage, value_stage):
    exp_stage(0)
    for u in range(n_maps):
        if u + 1 < n_maps:
            exp_stage(u + 1)
        value_stage(u)


def _att_scratch(nk):
    return [pltpu.VMEM((8, 128), F32),
            pltpu.VMEM((MLA_HEADS * HEAD_V, ATT_TQ), F32),
            pltpu.VMEM((nk, ATT_TQ), BF16), pltpu.VMEM((nk, ATT_TQ), BF16)]


def _att_nsub(n):
    return 2 if n % (2 * ATT_TQ) == 0 else 1


def _mla_attn_kernel(*refs, has_ctx, nsub):
    if has_ctx:
        q_ref, k_ref, vt_ref, kc_ref, vtc_ref, o_ref, kn2, ot, e0, e1 = refs
        n_ctx = kc_ref.shape[1]
    else:
        q_ref, k_ref, vt_ref, o_ref, kn2, ot, e0, e1 = refs
        n_ctx = 0
    e_bufs = (e0, e1)

    @pl.when(pl.program_id(1) == 0)
    def _():
        ones = jnp.ones((MLA_SLOT, 128), BF16)
        for hh in range(MLA_HEADS):
            kn2[hh:hh + 1, :] = _max_row_norm2(k_ref[hh], kc_ref[hh] if has_ctx else None, ones)

    def run(exact):
        denoms = []

        def exp_stage(u):
            t, hh = divmod(u, MLA_HEADS)
            q = q_ref[hh, t * ATT_TQ:(t + 1) * ATT_TQ, :]
            k_new = lambda: k_ref[hh]
            k_ctx = (lambda: kc_ref[hh]) if has_ctx else None
            shift = _exact_shift(k_new, k_ctx, q) if exact else _bound_shift(q, kn2[hh:hh + 1, 0:1])
            _exp_stage(e_bufs[u % 2], k_new, k_ctx, q, shift, n_ctx)

        def value_stage(u):
            t, hh = divmod(u, MLA_HEADS)
            o = _value_stage(e_bufs[u % 2], lambda: vt_ref[hh],
                             (lambda: vtc_ref[hh]) if has_ctx else None, n_ctx)
            denom = o[HEAD_V:HEAD_V + 1, :]
            denoms.append(denom)
            ot[hh * HEAD_V:(hh + 1) * HEAD_V, :] = o[0:HEAD_V, :] * (1.0 / denom)
            if hh == MLA_HEADS - 1:
                o_ref[t * ATT_TQ:(t + 1) * ATT_TQ, :] = ot[...].T

        _run_pipeline(nsub * MLA_HEADS, exp_stage, value_stage)
        return jnp.min(functools.reduce(jnp.minimum, denoms))

    denom_min = run(exact=False)

    @pl.when(jnp.logical_not(denom_min >= SAFE_DENOM))
    def _():
        run(exact=True)


def _mla_attn(q, k, vt, ctx, *, nb, n):
    nsub = 1
    tq = nsub * ATT_TQ
    npt = n // tq
    H, S = MLA_HEADS, MLA_SLOT
    in_specs = [
        pl.BlockSpec((H, tq, S), lambda b, j: (0, b * npt + j, 0)),
        pl.BlockSpec((H, n, S), lambda b, j: (0, b, 0)),
        pl.BlockSpec((H, VT_ROWS, n), lambda b, j: (0, 0, b)),
    ]
    args = [q, k, vt]
    n_ctx = 0
    if ctx is not None:
        n_ctx = ctx[0].shape[1] // nb
        in_specs += [
            pl.BlockSpec((H, n_ctx, S), lambda b, j: (0, b, 0)),
            pl.BlockSpec((H, VT_ROWS, n_ctx), lambda b, j: (0, 0, b)),
        ]
        args += list(ctx)
    return pl.pallas_call(
        functools.partial(_mla_attn_kernel, has_ctx=ctx is not None, nsub=nsub),
        grid=(nb, npt),
        in_specs=in_specs,
        out_specs=pl.BlockSpec((tq, 256), lambda b, j: (b * npt + j, 0)),
        out_shape=jax.ShapeDtypeStruct((nb * n, 256), F32),
        scratch_shapes=_att_scratch(n + n_ctx),
        compiler_params=_params("arbitrary", "arbitrary"),
        name="mla_attn_ctx" if ctx is not None else "mla_attn",
    )(*args)


def _diff_attn_kernel(*refs, has_ctx, nsub, lam_init):
    if has_ctx:
        lv_ref, g_ref, q_ref, k_ref, vt_ref, kc_ref, vtc_ref, o_ref, kn2, ot, e0, e1 = refs
        n_ctx = kc_ref.shape[0]
    else:
        lv_ref, g_ref, q_ref, k_ref, vt_ref, o_ref, kn2, ot, e0, e1 = refs
        n_ctx = 0
    e_bufs = (e0, e1)
    lv = lv_ref[...]
    lam = (jnp.exp(jnp.sum(lv[0:1] * lv[1:2], axis=-1, keepdims=True))
           - jnp.exp(jnp.sum(lv[2:3] * lv[3:4], axis=-1, keepdims=True)) + lam_init)
    lane128 = lax.broadcasted_iota(jnp.int32, (1, 128), 1)
    n_pairs = 2 * DIFF_HEADS

    @pl.when(pl.program_id(1) == 0)
    def _():
        dim = lax.broadcasted_iota(jnp.int32, (256, 128), 0)
        col = lax.broadcasted_iota(jnp.int32, (256, 128), 1)
        indicator = jnp.where(dim // DIFF_DIM == col, 1.0, 0.0).astype(BF16)
        kn2[0:1, :] = _max_row_norm2(k_ref[...], kc_ref[...] if has_ctx else None, indicator)

    def run(exact):
        denoms = []
        outs = {}

        def exp_stage(u):
            t, p = divmod(u, n_pairs)
            tile = slice((p * DIFF_DIM // 128) * 128, (p * DIFF_DIM // 128 + 1) * 128)
            k_new = lambda: k_ref[:, tile]
            k_ctx = (lambda: kc_ref[:, tile]) if has_ctx else None
            q = q_ref[t * ATT_TQ:(t + 1) * ATT_TQ, tile]
            lo = p * DIFF_DIM - tile.start
            in_pair = (lane128 >= lo) & (lane128 < lo + DIFF_DIM)
            qm = jnp.where(in_pair, q, jnp.zeros_like(q))
            shift = _exact_shift(k_new, k_ctx, qm) if exact else _bound_shift(qm, kn2[0:1, p:p + 1])
            _exp_stage(e_bufs[u % 2], k_new, k_ctx, qm, shift, n_ctx)

        def value_stage(u):
            t, p = divmod(u, n_pairs)
            hh = p // 2
            o = _value_stage(e_bufs[u % 2], lambda: vt_ref[hh],
                             (lambda: vtc_ref[hh]) if has_ctx else None, n_ctx)
            denom = o[HEAD_V:HEAD_V + 1, :]
            denoms.append(denom)
            outs[u] = (o[0:HEAD_V, :], denom)
            if p % 2 == 1:
                (o0, l0), (o1, l1) = outs.pop(u - 1), outs.pop(u)
                o = o0 * (1.0 / l0) - o1 * (lam / l1)
                msq = jnp.sum(o * o, axis=0, keepdims=True) * (1.0 / HEAD_V)
                ot[hh * HEAD_V:(hh + 1) * HEAD_V, :] = o * lax.rsqrt(msq + EPS)
            if p == n_pairs - 1:
                o_ref[t * ATT_TQ:(t + 1) * ATT_TQ, :] = (ot[...].T * g_ref[...]) * (1.0 - lam_init)

        _run_pipeline(nsub * n_pairs, exp_stage, value_stage)
        return jnp.min(functools.reduce(jnp.minimum, denoms))

    denom_min = run(exact=False)

    @pl.when(jnp.logical_not(denom_min >= SAFE_DENOM))
    def _():
        run(exact=True)


def _diff_attn(q, k, vt, ctx, lw, *, nb, n, lam_init):
    nsub = 1
    tq = nsub * ATT_TQ
    npt = n // tq
    in_specs = [
        lw.spec("diff_lambda"),
        lw.spec("diff_g"),
        pl.BlockSpec((tq, 256), lambda b, j: (b * npt + j, 0)),
        pl.BlockSpec((n, 256), lambda b, j: (b, 0)),
        pl.BlockSpec((DIFF_HEADS, VT_ROWS, n), lambda b, j: (0, 0, b)),
    ]
    args = [lw["diff_lambda"], lw["diff_g"], q, k, vt]
    n_ctx = 0
    if ctx is not None:
        n_ctx = ctx[0].shape[0] // nb
        in_specs += [pl.BlockSpec((n_ctx, 256), lambda b, j: (b, 0)),
                     pl.BlockSpec((DIFF_HEADS, VT_ROWS, n_ctx), lambda b, j: (0, 0, b))]
        args += list(ctx)
    return pl.pallas_call(
        functools.partial(_diff_attn_kernel, has_ctx=ctx is not None, nsub=nsub, lam_init=lam_init),
        grid=(nb, npt),
        in_specs=in_specs,
        out_specs=pl.BlockSpec((tq, 256), lambda b, j: (b * npt + j, 0)),
        out_shape=jax.ShapeDtypeStruct((nb * n, 256), F32),
        scratch_shapes=_att_scratch(n + n_ctx),
        compiler_params=_params("arbitrary", "arbitrary"),
        name="diff_attn_ctx" if ctx is not None else "diff_attn",
    )(*args)


def _shift_rows(v, k):
    return pltpu.roll(v, (-k) % v.shape[0], 0)


def _scan_strided(a_ref, b_ref, h_ref, row0, carry, n_rows, reverse):
    sub = lax.broadcasted_iota(jnp.int32, (8, 128), 0)
    span = 8 * SCAN_RUN
    order = tuple(range(SCAN_RUN))[::-1] if reverse else tuple(range(SCAN_RUN))
    starts = tuple(range(0, n_rows, span))[::-1] if reverse else tuple(range(0, n_rows, span))
    carries = []
    for lt in range(a_ref.shape[0]):
        c_in = carry[:, lt * 128:(lt + 1) * 128]
        for start in starts:
            tile = lambda ref, g: ref[lt, pl.ds(row0 + start + g, 8, stride=SCAN_RUN), :]
            a = [tile(a_ref, g) for g in range(SCAN_RUN)]
            b = [tile(b_ref, g) for g in range(SCAN_RUN)]
            h = {order[0]: b[order[0]]}
            p = {order[0]: a[order[0]]}
            for prev, g in zip(order, order[1:]):
                h[g] = a[g] * h[prev] + b[g]
                p[g] = a[g] * p[prev]
            pi, hi = p[order[-1]], h[order[-1]]
            for s in (1, 2, 4):
                shift = 8 - s if reverse else s
                valid = (sub < 8 - s) if reverse else (sub >= s)
                pr, hr = pltpu.roll(pi, shift, 0), pltpu.roll(hi, shift, 0)
                hi = jnp.where(valid, pi * hr + hi, hi)
                pi = jnp.where(valid, pi * pr, pi)
            one = 7 if reverse else 1
            first = (sub == 7) if reverse else (sub == 0)
            pe = jnp.where(first, 1.0, pltpu.roll(pi, one, 0))
            he = jnp.where(first, 0.0, pltpu.roll(hi, one, 0))
            c = pe * c_in + he
            for g in range(SCAN_RUN):
                h_ref[lt, pl.ds(start + g, 8, stride=SCAN_RUN), :] = h[g] + p[g] * c
            last = 0 if reverse else 7
            c_in = pi[last:last + 1, :] * c_in + hi[last:last + 1, :]
        carries.append(c_in)
    return jnp.concatenate(carries, axis=1)


def _sigmoid(x):
    return 0.5 * jnp.tanh(0.5 * x) + 0.5


def _gelu_tanh(x):
    return x * (0.5 * (1.0 + jnp.tanh(math.sqrt(2.0 / math.pi) * (x + 0.044715 * (x * x * x)))))


def _lru_kernel(u_ref, h0_ref, cw_ref, cb_ref, wg_ref, bg_ref, lam_ref, y_ref, st_ref,
                xpad, a1s, b1s, a0c, b0c, hc, *, N, T):
    W = LRU_WIDTH
    nc = N // T
    tiles = [slice(lt * 128, (lt + 1) * 128) for lt in range(W // 128)]
    zeros = jnp.zeros((HALO, W), F32)
    xpad[0:HALO, :] = zeros
    xpad[N + HALO:N + 2 * HALO, :] = zeros

    def fill(j, carry):
        r0 = pl.multiple_of(j * T, T)
        xpad[pl.ds(r0 + HALO, T), :] = u_ref[pl.ds(r0, T), 0:W]
        return carry

    lax.fori_loop(0, nc, fill, 0)

    z = -lam_ref[...]
    sp = jnp.maximum(z, 0.0) + jnp.log1p(jnp.exp(-jnp.abs(z)))
    cw = cw_ref[...]
    cb = cb_ref[...]
    bg = bg_ref[...]

    def fwd(j, carry):
        r0 = pl.multiple_of(j * T, T)
        ext = xpad[pl.ds(r0, T + 2 * HALO), :]
        body = slice(HALO, HALO + T)
        xc = cb
        for tap in range(4):
            xc = xc + _shift_rows(ext, tap - 1)[body] * cw[tap:tap + 1]
        g = _sigmoid(_dot(xc.astype(BF16), wg_ref[...]) + bg)
        ab = []
        for d in range(2):
            r = g[:, d * W:(d + 1) * W]
            i = g[:, (2 + d) * W:(3 + d) * W]
            log_a = (-LRU_C * r) * sp[d:d + 1]
            a = jnp.exp(log_a)
            bt = (jnp.sqrt(1.0 - a * a) * i) * xc
            ab.append((a, bt))
        for lt, lanes in enumerate(tiles):
            a0c[lt] = ab[0][0][:, lanes]
            b0c[lt] = ab[0][1][:, lanes]
            a1s[lt, pl.ds(r0, T), :] = ab[1][0][:, lanes]
            b1s[lt, pl.ds(r0, T), :] = ab[1][1][:, lanes]
        carry = _scan_strided(a0c, b0c, hc, 0, carry, T, reverse=False)
        for lt, lanes in enumerate(tiles):
            y_ref[pl.ds(r0, T), lanes] = hc[lt]
        return carry

    cf = lax.fori_loop(0, nc, fwd, h0_ref[0, 0:1, :])

    def bwd(jj, carry):
        r0 = pl.multiple_of((nc - 1 - jj) * T, T)
        carry = _scan_strided(a1s, b1s, hc, r0, carry, T, reverse=True)
        for lt, lanes in enumerate(tiles):
            gb = u_ref[pl.ds(r0, T), W + lt * 128:W + (lt + 1) * 128]
            y_ref[pl.ds(r0, T), lanes] = (y_ref[pl.ds(r0, T), lanes] + hc[lt]) * _gelu_tanh(gb)
        return carry

    cbw = lax.fori_loop(0, nc, bwd, h0_ref[0, 1:2, :])
    st_ref[0, 0:1, :] = cf
    st_ref[0, 1:2, :] = cbw


def _lru(u, h0, h0_block, lw, *, nb, n):
    T = min(n, 256)
    W = LRU_WIDTH
    return pl.pallas_call(
        functools.partial(_lru_kernel, N=n, T=T),
        grid=(nb,),
        in_specs=[
            pl.BlockSpec((n, 2 * W), lambda b: (b, 0)),
            pl.BlockSpec((1, 2, W), lambda b: (h0_block(b), 0, 0)),
            lw.spec("conv_w"), lw.spec("conv_b"), lw.spec("w_gate"), lw.spec("b_gate"),
            lw.spec("lru_lambda"),
        ],
        out_specs=[
            pl.BlockSpec((n, W), lambda b: (b, 0)),
            pl.BlockSpec((1, 2, W), lambda b: (b, 0, 0)),
        ],
        out_shape=[
            jax.ShapeDtypeStruct((nb * n, W), F32),
            jax.ShapeDtypeStruct((nb, 2, W), F32),
        ],
        scratch_shapes=[
            pltpu.VMEM((n + 2 * HALO, W), F32),
            pltpu.VMEM((W // 128, n, 128), F32),
            pltpu.VMEM((W // 128, n, 128), F32),
            pltpu.VMEM((W // 128, T, 128), F32),
            pltpu.VMEM((W // 128, T, 128), F32),
            pltpu.VMEM((W // 128, T, 128), F32),
        ],
        compiler_params=_params("arbitrary"),
        name="rglru",
    )(u, h0, lw["conv_w"], lw["conv_b"], lw["w_gate"], lw["b_gate"], lw["lru_lambda"])


def _pool_kernel(u_ref, wp_ref, sc_ref, y_ref, xpad, *, N, T):
    W = GROUP_WIDTH
    nc = N // T
    zeros = jnp.zeros((HALO, W), F32)
    xpad[0:HALO, :] = zeros
    xpad[N + HALO:N + 2 * HALO, :] = zeros

    def fill(j, carry):
        r0 = pl.multiple_of(j * T, T)
        xpad[pl.ds(r0 + HALO, T), :] = u_ref[pl.ds(r0, T), :]
        return carry

    lax.fori_loop(0, nc, fill, 0)

    grp = lax.broadcasted_iota(jnp.int32, (1, W), 1) // POOL_CH
    half = jnp.where(grp == 0, 1, jnp.where(grp == 1, 2, jnp.where(grp == 2, 4, 8)))
    scale = sc_ref[...]

    def chunk(j, carry):
        r0 = pl.multiple_of(j * T, T)
        ext = xpad[pl.ds(r0, T + 2 * HALO), :]
        w2 = _shift_rows(ext, -1) + ext
        w4 = _shift_rows(w2, -1) + _shift_rows(w2, 1)
        w8 = _shift_rows(w4, -2) + _shift_rows(w4, 2)
        w16 = _shift_rows(w8, -4) + _shift_rows(w8, 4)
        ws = jnp.where(grp == 0, w2, jnp.where(grp == 1, w4, jnp.where(grp == 2, w8, w16)))
        body = slice(HALO, HALO + T)
        t = r0 + lax.broadcasted_iota(jnp.int32, (T, W), 0)
        cnt = (jnp.minimum(t + half, N) - jnp.maximum(t - half, 0)).astype(F32)
        d = ws[body] / cnt - ext[body]
        y_ref[pl.ds(r0, T), :] = _dot(d.astype(BF16), wp_ref[...]) * scale
        return carry

    lax.fori_loop(0, nc, chunk, 0)


def _pool(u, lw, *, nb, n):
    W = GROUP_WIDTH
    T = min(n, 256)
    return pl.pallas_call(
        functools.partial(_pool_kernel, N=n, T=T),
        grid=(nb,),
        in_specs=[pl.BlockSpec((n, W), lambda b: (b, 0)), lw.spec("w_pool"), lw.spec("pool_scale")],
        out_specs=pl.BlockSpec((n, W), lambda b: (b, 0)),
        out_shape=jax.ShapeDtypeStruct((nb * n, W), F32),
        scratch_shapes=[pltpu.VMEM((n + 2 * HALO, W), F32)],
        compiler_params=_params("arbitrary"),
        name="pool_mixer",
    )(u, lw["w_pool"], lw["pool_scale"])


def _mix_ffn_kernel(*refs, final):
    if final:
        (x_ref, ya_ref, yb_ref, yc_ref, yd_ref, mod_ref, g2_ref, wo_ref, wg_ref, wu_ref, wd_ref,
         gf_ref, o_ref) = refs
    else:
        (x_ref, ya_ref, yb_ref, yc_ref, yd_ref, mod_ref, g2_ref, wo_ref, wg_ref, wu_ref, wd_ref,
         o_ref) = refs
    mod = mod_ref[0]
    gate1 = mod[:, 2 * D_MODEL:3 * D_MODEL]
    sh2 = mod[:, 3 * D_MODEL:4 * D_MODEL]
    sc2 = mod[:, 4 * D_MODEL:5 * D_MODEL]
    gate2 = mod[:, 5 * D_MODEL:6 * D_MODEL]
    mix = None
    for i, y_ref in enumerate((ya_ref, yb_ref, yc_ref, yd_ref)):
        part = _dot(y_ref[...].astype(BF16), wo_ref[i * GROUP_WIDTH:(i + 1) * GROUP_WIDTH, :])
        mix = part if mix is None else mix + part
    x1 = x_ref[...] + gate1 * mix
    h = _rms_rows(x1, D_MODEL) * g2_ref[...]
    hb = (h * (1.0 + sc2) + sh2).astype(BF16)
    ff = None
    for lo, hi in FF_CHUNKS:
        g = _dot(hb, wg_ref[:, lo:hi])
        up = _dot(hb, wu_ref[:, lo:hi])
        act = ((g * jax.nn.sigmoid(g)) * up).astype(BF16)
        part = _dot(act, wd_ref[lo:hi, :])
        ff = part if ff is None else ff + part
    x2 = x1 + gate2 * ff
    if final:
        x2 = _rms_rows(x2, D_MODEL) * gf_ref[...]
    o_ref[...] = x2


def _mix_ffn(x, ys, mod, lw, gf, *, nb, n, final):
    T = nb * n
    tm = TOKEN_TILE
    npt = n // tm

    def tok(width):
        return pl.BlockSpec((tm, width), lambda i: (i, 0))

    wnames = ("g2", "w_out", "w_gate_ff", "w_up_ff", "w_down")
    in_specs = [tok(D_MODEL), tok(256), tok(256), tok(256), tok(256),
                mod.spec(lambda i: i // npt)] + [lw.spec(nm) for nm in wnames]
    args = [x, *ys, mod.table] + [lw[nm] for nm in wnames]
    if final:
        in_specs.append(_resident((1, D_MODEL)))
        args.append(gf)
    return pl.pallas_call(
        functools.partial(_mix_ffn_kernel, final=final),
        grid=(T // tm,),
        in_specs=in_specs,
        out_specs=tok(D_MODEL),
        out_shape=jax.ShapeDtypeStruct((T, D_MODEL), F32),
        compiler_params=_params("arbitrary"),
        name="mix_ffn_final" if final else "mix_ffn",
    )(*args)


def _block_diag(w):
    L, G, c, e = w.shape
    return jnp.einsum('lgce,gh->lgche', w, jnp.eye(G, dtype=w.dtype)).reshape(L, G * c, G * e)


def _rot_cols(w):
    return jnp.concatenate([-w[..., 16:32], w[..., 0:16]], axis=-1)


def _stack_weights(p):
    w_in = p["w_in"]
    o1 = MLA_Q_RANK
    o2 = o1 + MLA_KV_RANK
    o3 = o2 + MLA_ROPE
    c_q, c_kv, k_r, rest = w_in[..., :o1], w_in[..., o1:o2], w_in[..., o2:o3], w_in[..., o3:]
    z = lambda n: jnp.zeros((DEPTH, D_MODEL, n), F32)
    w_in_eff = jnp.concatenate([c_q, k_r, z(32), c_kv, z(64), _rot_cols(k_r), z(32), rest], axis=-1)

    w_uq = p["mla_w_uq"]
    qd = MLA_NOPE + MLA_ROPE
    wq_parts, wqr_parts = [], []
    zq = lambda n: jnp.zeros((DEPTH, MLA_Q_RANK, n), F32)
    for h in range(MLA_HEADS):
        wh = w_uq[..., h * qd:(h + 1) * qd]
        wq_parts += [wh, zq(MLA_SLOT - qd)]
        wqr_parts += [zq(MLA_NOPE), _rot_cols(wh[..., MLA_NOPE:]), zq(MLA_SLOT - qd)]
    pad_rows = lambda w: jnp.pad(w, ((0, 0), (0, 256 - MLA_Q_RANK), (0, 0)))
    w_ukv = p["mla_w_ukv"]
    wk_parts, wv_parts = [], []
    zk = jnp.zeros((DEPTH, MLA_KV_RANK, MLA_SLOT - MLA_NOPE), F32)
    for h in range(MLA_HEADS):
        base = h * (MLA_NOPE + MLA_V)
        wk_parts += [w_ukv[..., base:base + MLA_NOPE], zk]
        wv_parts.append(w_ukv[..., base + MLA_NOPE:base + MLA_NOPE + MLA_V])

    w_r, w_i, b_r, b_i = p["lru_w_r"], p["lru_w_i"], p["lru_b_r"], p["lru_b_i"]
    w_gate = jnp.concatenate([_block_diag(w_r[:, 0]), _block_diag(w_r[:, 1]),
                              _block_diag(w_i[:, 0]), _block_diag(w_i[:, 1])], axis=-1)
    b_gate = jnp.concatenate([b_r[:, 0], b_r[:, 1], b_i[:, 0], b_i[:, 1]], axis=-1)
    w_gu = p["w_gu"]
    row = lambda v: v[:, None, :]
    return {
        "g1": row(p["norm1_g"]),
        "g2": row(p["norm2_g"]),
        "w_in": w_in_eff.astype(BF16),
        "gq": row(jnp.pad(p["mla_q_norm_g"], ((0, 0), (0, 256 - MLA_Q_RANK)))),
        "gkv": row(p["mla_kv_norm_g"]),
        "wq": pad_rows(jnp.concatenate(wq_parts, axis=-1)).astype(BF16),
        "wqr": pad_rows(jnp.concatenate(wqr_parts, axis=-1)).astype(BF16),
        "wk": jnp.concatenate(wk_parts, axis=-1).astype(BF16),
        "wv": jnp.concatenate(wv_parts, axis=-1).astype(BF16),
        "conv_w": p["lru_conv_w"],
        "conv_b": row(p["lru_conv_b"]),
        "w_gate": w_gate.astype(BF16),
        "b_gate": row(b_gate),
        "lru_lambda": p["lru_lambda"],
        "w_pool": _block_diag(p["pool_w"]).astype(BF16),
        "pool_scale": row(p["pool_scale"]),
        "diff_lambda": p["diff_lambda"],
        "diff_g": row(jnp.tile(p["diff_norm_g"], (1, DIFF_HEADS))),
        "w_out": p["w_out"].astype(BF16),
        "w_gate_ff": w_gu[..., :FF_HIDDEN].astype(BF16),
        "w_up_ff": w_gu[..., FF_HIDDEN:].astype(BF16),
        "w_down": p["w_down"].astype(BF16),
    }


def _rope_tables(n, positional):
    quarter = MLA_ROPE // 4
    if positional:
        t = jnp.arange(n)
        row = (t // GRID_W).astype(F32)
        col = (t % GRID_W).astype(F32)
        inv = ROPE_BASE ** (-jnp.arange(quarter, dtype=F32) / quarter)
        ang = jnp.concatenate([row[:, None] * inv, col[:, None] * inv], axis=-1)
        cos, sin = jnp.cos(ang), jnp.sin(ang)
    else:
        cos, sin = jnp.ones((n, 16), F32), jnp.zeros((n, 16), F32)
    one = lambda w: jnp.ones((n, w), F32)
    zero = lambda w: jnp.zeros((n, w), F32)
    scale = LOG2E / math.sqrt(MLA_NOPE + MLA_ROPE)
    return {
        "cosq": jnp.concatenate([one(64), cos, cos, one(32)], axis=1) * scale,
        "sinq": jnp.concatenate([zero(64), sin, sin, zero(32)], axis=1) * scale,
        "cosk": jnp.concatenate([zero(64), cos, cos, zero(32)], axis=1),
        "sink": jnp.concatenate([zero(64), sin, sin, zero(32)], axis=1),
        "cosd": jnp.tile(jnp.concatenate([cos, cos], axis=1), (1, 8)),
        "sina": jnp.tile(jnp.concatenate([-sin, zero(16)], axis=1), (1, 8)),
        "sinb": jnp.tile(jnp.concatenate([zero(16), sin], axis=1), (1, 8)),
    }


def _layer(x, mod, lw, tabs, layer_idx, ctx, gf, *, nb, n, final):
    emit_cache = ctx is None
    tok_nb, tok_n = (1, nb * n) if mod.shared else (nb, n)
    outs = _inproj(x, mod, lw, tabs, nb=tok_nb, n=tok_n, emit_cache=emit_cache)
    q, k, vt, u_lru, u_pool, dq, dk, dvt = outs[:8]
    lam_init = 0.8 - 0.6 * math.exp(-0.3 * layer_idx)
    if ctx is None:
        h0 = jnp.zeros((1, 2, LRU_WIDTH), F32)
        h0_block = lambda b: 0
        mla_ctx = diff_ctx = None
    else:
        ckv, kr_pad, cdk, cdv, h0 = ctx
        p = ckv.shape[0] // (nb * DEPTH)
        h0_block = lambda b: b * DEPTH + layer_idx
        kc, vtc, dkc, dvtc = _ctx_prep(ckv, kr_pad, cdk, cdv, lw, nb=nb, p=p)
        mla_ctx = (kc, vtc)
        diff_ctx = (dkc, dvtc)
    y_mla = _mla_attn(q, k, vt, mla_ctx, nb=nb, n=n)
    y_lru, st = _lru(u_lru, h0, h0_block, lw, nb=nb, n=n)
    y_pool = _pool(u_pool, lw, nb=nb, n=n)
    y_diff = _diff_attn(dq, dk, dvt, diff_ctx, lw, nb=nb, n=n, lam_init=lam_init)
    x2 = _mix_ffn(x, (y_mla, y_lru, y_pool, y_diff), mod, lw, gf, nb=tok_nb, n=tok_n, final=final)
    cache = (outs[8], outs[9][:, 64:96], outs[10], outs[11], st) if emit_cache else None
    return x2, cache


def kernel(x_prompt, x_sample, cache_mla_ckv, cache_mla_krope, cache_diff_k, cache_diff_v, state_lru,
           c, c_ctx, w_ada, b_ada, norm1_g, norm2_g, w_in, mla_q_norm_g, mla_w_uq, mla_kv_norm_g,
           mla_w_ukv, lru_conv_w, lru_conv_b, lru_w_r, lru_b_r, lru_w_i, lru_b_i, lru_lambda, pool_w,
           pool_scale, diff_lambda, diff_norm_g, w_out, w_gu, w_down, final_norm_g):
    p = {
        "norm1_g": norm1_g, "norm2_g": norm2_g, "w_in": w_in, "mla_q_norm_g": mla_q_norm_g,
        "mla_w_uq": mla_w_uq, "mla_kv_norm_g": mla_kv_norm_g, "mla_w_ukv": mla_w_ukv,
        "lru_conv_w": lru_conv_w, "lru_conv_b": lru_conv_b, "lru_w_r": lru_w_r, "lru_b_r": lru_b_r,
        "lru_w_i": lru_w_i, "lru_b_i": lru_b_i, "lru_lambda": lru_lambda, "pool_w": pool_w,
        "pool_scale": pool_scale, "diff_lambda": diff_lambda, "diff_norm_g": diff_norm_g,
        "w_out": w_out, "w_gu": w_gu, "w_down": w_down,
    }
    Bp, Np, _ = x_prompt.shape
    Bs, Ns, _ = x_sample.shape
    P = cache_mla_ckv.shape[2]

    cond_all = jnp.concatenate([c, c_ctx[None, :], jnp.zeros((MOD_ROWS - Bs - 1, D_MODEL), F32)], axis=0)
    mod_table = _ada(cond_all, w_ada, b_ada).reshape(DEPTH * MOD_ROWS, 1, 6 * D_MODEL)
    tabs_p = _rope_tables(Bp * Np, positional=False)
    tabs_s = _rope_tables(Ns, positional=True)
    kr_pad = jnp.pad(cache_mla_krope, ((0, 0), (0, 0), (0, 0), (MLA_NOPE, MLA_SLOT - MLA_NOPE - MLA_ROPE)))
    flat = lambda a, w: a.reshape(Bs * DEPTH * P, w)
    ctx = (flat(cache_mla_ckv, MLA_KV_RANK), flat(kr_pad, MLA_SLOT), flat(cache_diff_k, 256),
           flat(cache_diff_v, 256), state_lru.reshape(Bs * DEPTH, 2, LRU_WIDTH))
    gf = final_norm_g[None, :]
    stacked = _stack_weights(p)

    xp = x_prompt.reshape(Bp * Np, D_MODEL)
    xs = x_sample.reshape(Bs * Ns, D_MODEL)
    caches = []
    for l in range(DEPTH):
        lw = _LayerWeights(stacked, l)
        final = l == DEPTH - 1
        mod_p = _Mod(mod_table, l * MOD_ROWS + Bs, shared=True)
        mod_s = _Mod(mod_table, l * MOD_ROWS, shared=False)
        xp, cache = _layer(xp, mod_p, lw, tabs_p, l, None, gf, nb=Bp, n=Np, final=final)
        caches.append(cache)
        xs, _ = _layer(xs, mod_s, lw, tabs_s, l, ctx, gf, nb=Bs, n=Ns, final=final)

    stack = lambda i, w: jnp.stack([cc[i].reshape(Bp, Np, w) for cc in caches], axis=1)
    new_mla_ckv = stack(0, MLA_KV_RANK)
    new_mla_krope = stack(1, MLA_ROPE)
    new_diff_k = stack(2, 256).reshape(Bp, DEPTH, Np, DIFF_HEADS, 2, DIFF_DIM)
    new_diff_v = stack(3, 256).reshape(Bp, DEPTH, Np, DIFF_HEADS, 2 * DIFF_DIM)
    new_state_lru = jnp.stack([cc[4] for cc in caches], axis=1)
    return (xp.reshape(Bp, Np, D_MODEL), xs.reshape(Bs, Ns, D_MODEL),
            new_mla_ckv, new_mla_krope, new_diff_k, new_diff_v, new_state_lru)
```

```python
import functools
import math

import jax
import jax.numpy as jnp
from jax import lax
from jax.experimental import pallas as pl
from jax.experimental.pallas import tpu as pltpu

F32 = jnp.float32
BF16 = jnp.bfloat16

D_MODEL = 1024
DEPTH = 2
GRID_W = 64
GROUP_WIDTH = 256
MLA_HEADS = 4
MLA_NOPE = 64
MLA_ROPE = 32
MLA_V = 64
MLA_Q_RANK = 192
MLA_KV_RANK = 128
MLA_SLOT = 128
LRU_WIDTH = 256
LRU_C = 8.0
POOL_WINDOWS = (2, 4, 8, 16)
POOL_CH = 64
DIFF_HEADS = 4
DIFF_DIM = 32
HEAD_V = 64
FF_HIDDEN = 2816
FF_CHUNKS = ((0, 768), (768, 1536), (1536, 2304), (2304, 2816))
ROPE_BASE = 10000.0
EPS = 1e-6
IN_EFF = 2048
HALO = 8
SCAN_RUN = 4
VT_ROWS = 80
ATT_TQ = 256
TOKEN_TILE = 512
MOD_ROWS = 16
LOG2E = math.log2(math.e)

VMEM_LIMIT_BYTES = 56 * 1024 * 1024

_NT = (((1,), (1,)), ((), ()))


def _params(*sem):
    return pltpu.CompilerParams(dimension_semantics=sem, vmem_limit_bytes=VMEM_LIMIT_BYTES)


def _resident(shape):
    zeros = (0,) * len(shape)
    return pl.BlockSpec(shape, lambda *_: zeros, pipeline_mode=pl.Buffered(1))


def _dot(a, b):
    return jnp.dot(a, b, preferred_element_type=F32)


def _dot_nt(a, b):
    return lax.dot_general(a, b, _NT, preferred_element_type=F32)


def _rms_rows(x, width):
    ms = jnp.sum(x * x, axis=-1, keepdims=True) * (1.0 / width)
    return x * lax.rsqrt(ms + EPS)


def _store_vt(vt_ref, v):
    vt = v.T
    rows = v.shape[0]
    pad = VT_ROWS - HEAD_V
    ones_row = jnp.where(lax.broadcasted_iota(jnp.int32, (pad, rows), 0) == 0, 1.0, 0.0).astype(BF16)
    for hh in range(vt_ref.shape[0]):
        vt_ref[hh, 0:HEAD_V, :] = vt[hh * HEAD_V:(hh + 1) * HEAD_V, :].astype(BF16)
        vt_ref[hh, HEAD_V:VT_ROWS, :] = ones_row


class _Mod:
    def __init__(self, table, row0, shared):
        self.table, self.row0, self.shared = table, row0, shared

    def spec(self, batch_of):
        row0 = self.row0
        if self.shared:
            return pl.BlockSpec((1, 1, 6 * D_MODEL), lambda *g: (row0, 0, 0))
        return pl.BlockSpec((1, 1, 6 * D_MODEL), lambda *g: (row0 + batch_of(*g), 0, 0))


class _LayerWeights:
    def __init__(self, stacked, layer):
        self.stacked, self.layer = stacked, layer

    def __getitem__(self, name):
        return self.stacked[name]

    def spec(self, name):
        layer = self.layer
        _, rows, cols = self.stacked[name].shape
        return pl.BlockSpec((None, rows, cols), lambda *_: (layer, 0, 0), pipeline_mode=pl.Buffered(1))


def _ada_kernel(cond_ref, w_ref, b_ref, out_ref):
    c = cond_ref[...]
    s = c * jax.nn.sigmoid(c)
    out_ref[0] = _dot(s.astype(BF16), w_ref[0].astype(BF16)) + b_ref[0]


def _ada(cond_all, w_ada, b_ada):
    rows = cond_all.shape[0]
    tn = 1536
    return pl.pallas_call(
        _ada_kernel,
        grid=(DEPTH, 6 * D_MODEL // tn),
        in_specs=[
            pl.BlockSpec((rows, D_MODEL), lambda l, j: (0, 0)),
            pl.BlockSpec((1, D_MODEL, tn), lambda l, j: (l, 0, j)),
            pl.BlockSpec((1, 1, tn), lambda l, j: (l, 0, j)),
        ],
        out_specs=pl.BlockSpec((1, rows, tn), lambda l, j: (l, 0, j)),
        out_shape=jax.ShapeDtypeStruct((DEPTH, rows, 6 * D_MODEL), F32),
        compiler_params=_params("arbitrary", "arbitrary"),
        name="ada_mod",
    )(cond_all, w_ada, b_ada.reshape(DEPTH, 1, 6 * D_MODEL))


def _inproj_kernel(x_ref, mod_ref, g1_ref, win_ref, gq_ref, gkv_ref, wq_ref, wqr_ref, wk_ref, wv_ref,
                   cosq_ref, sinq_ref, cosk_ref, sink_ref, cosd_ref, sina_ref, sinb_ref,
                   q_out, k_out, vt_out, lru_out, pool_out, dq_out, dk_out, dvt_out, *cache_outs):
    x = x_ref[...]
    mod = mod_ref[0]
    sh1 = mod[:, 0:D_MODEL]
    sc1 = mod[:, D_MODEL:2 * D_MODEL]
    h = _rms_rows(x, D_MODEL) * g1_ref[...]
    hb = (h * (1.0 + sc1) + sh1).astype(BF16)

    t01 = _dot(hb, win_ref[:, 0:256])
    lane = lax.broadcasted_iota(jnp.int32, (1, 256), 1)
    cq = jnp.where(lane < MLA_Q_RANK, t01, 0.0)
    cqn = (_rms_rows(cq, MLA_Q_RANK) * gq_ref[...]).astype(BF16)
    qa = _dot(cqn, wq_ref[...])
    qr = _dot(cqn, wqr_ref[...])
    cosq = cosq_ref[...]
    sinq = sinq_ref[...]
    ckv = _dot(hb, win_ref[:, 256:384])
    lat = _rms_rows(ckv, MLA_KV_RANK) * gkv_ref[...]
    latb = lat.astype(BF16)
    kk = _dot(latb, wk_ref[...])
    _store_vt(vt_out, _dot(latb, wv_ref[...]))
    t1 = t01[:, 128:256]
    t3 = _dot(hb, win_ref[:, 384:512])
    kro = t1 * cosk_ref[...] + t3 * sink_ref[...]
    for hh in range(MLA_HEADS):
        sl = slice(hh * MLA_SLOT, (hh + 1) * MLA_SLOT)
        q_out[hh] = (qa[:, sl] * cosq + qr[:, sl] * sinq).astype(q_out.dtype)
        k_out[hh] = (kk[:, sl] + kro).astype(k_out.dtype)

    lru_out[...] = _dot(hb, win_ref[:, 512:1024])
    pool_out[...] = _dot(hb, win_ref[:, 1024:1280])

    cosd = cosd_ref[...]
    sina = sina_ref[...]
    sinb = sinb_ref[...]

    def rope(t):
        return t * cosd + pltpu.roll(t, 256 - 16, 1) * sina + pltpu.roll(t, 16, 1) * sinb

    dq = _dot(hb, win_ref[:, 1280:1536])
    dk = _dot(hb, win_ref[:, 1536:1792])
    dv = _dot(hb, win_ref[:, 1792:2048])
    dq_out[...] = (rope(dq) * (LOG2E / math.sqrt(DIFF_DIM))).astype(dq_out.dtype)
    dk_out[...] = rope(dk).astype(dk_out.dtype)
    _store_vt(dvt_out, dv)

    if cache_outs:
        lat_out, kr_out, dk_raw_out, dv_raw_out = cache_outs
        lat_out[...] = lat
        kr_out[...] = t1
        dk_raw_out[...] = dk
        dv_raw_out[...] = dv


def _inproj(x, mod, lw, tabs, *, nb, n, emit_cache):
    T = nb * n
    tm = TOKEN_TILE
    npt = n // tm
    row_blk = lambda j, b: b * npt + j

    def tok(width):
        return pl.BlockSpec((tm, width), lambda j, b: (row_blk(j, b), 0))

    def tab(width):
        return pl.BlockSpec((tm, width), lambda j, b: (j, 0))

    head = pl.BlockSpec((MLA_HEADS, tm, MLA_SLOT), lambda j, b: (0, row_blk(j, b), 0))
    vt_spec = pl.BlockSpec((MLA_HEADS, VT_ROWS, tm), lambda j, b: (0, 0, row_blk(j, b)))
    wnames = ("g1", "w_in", "gq", "gkv", "wq", "wqr", "wk", "wv")
    in_specs = [tok(D_MODEL), mod.spec(lambda j, b: b)] + [lw.spec(nm) for nm in wnames] + [
        tab(128), tab(128), tab(128), tab(128), tab(256), tab(256), tab(256)]
    out_specs = [head, head, vt_spec, tok(512), tok(256), tok(256), tok(256), vt_spec]
    vt_shape = jax.ShapeDtypeStruct((MLA_HEADS, VT_ROWS, T), BF16)
    out_shape = [
        jax.ShapeDtypeStruct((MLA_HEADS, T, MLA_SLOT), BF16),
        jax.ShapeDtypeStruct((MLA_HEADS, T, MLA_SLOT), BF16),
        vt_shape,
        jax.ShapeDtypeStruct((T, 512), F32),
        jax.ShapeDtypeStruct((T, 256), F32),
        jax.ShapeDtypeStruct((T, 256), BF16),
        jax.ShapeDtypeStruct((T, 256), BF16),
        vt_shape,
    ]
    if emit_cache:
        out_specs += [tok(128), tok(128), tok(256), tok(256)]
        out_shape += [jax.ShapeDtypeStruct((T, 128), F32), jax.ShapeDtypeStruct((T, 128), F32),
                      jax.ShapeDtypeStruct((T, 256), F32), jax.ShapeDtypeStruct((T, 256), F32)]
    return pl.pallas_call(
        _inproj_kernel,
        grid=(npt, nb),
        in_specs=in_specs,
        out_specs=out_specs,
        out_shape=out_shape,
        compiler_params=_params("arbitrary", "arbitrary"),
        name="inproj_cache" if emit_cache else "inproj",
    )(x, mod.table, *[lw[nm] for nm in wnames],
      tabs["cosq"], tabs["sinq"], tabs["cosk"], tabs["sink"], tabs["cosd"], tabs["sina"], tabs["sinb"])


def _ctx_prep_kernel(ckv_ref, kr_ref, dk_ref, dv_ref, wk_ref, wv_ref, k_out, vt_out, dk_out, dvt_out):
    latb = ckv_ref[...].astype(BF16)
    kk = _dot(latb, wk_ref[...])
    kr = kr_ref[...]
    for hh in range(MLA_HEADS):
        k_out[hh] = (kk[:, hh * MLA_SLOT:(hh + 1) * MLA_SLOT] + kr).astype(k_out.dtype)
    _store_vt(vt_out, _dot(latb, wv_ref[...]))
    dk_out[...] = dk_ref[...].astype(dk_out.dtype)
    _store_vt(dvt_out, dv_ref[...])


def _ctx_prep(ckv, kr_pad, cdk, cdv, lw, *, nb, p):
    T = nb * p
    layer = lw.layer
    cache_row = lambda w: pl.BlockSpec((p, w), lambda b: (b * DEPTH + layer, 0))
    row = lambda w: pl.BlockSpec((p, w), lambda b: (b, 0))
    vt_spec = pl.BlockSpec((MLA_HEADS, VT_ROWS, p), lambda b: (0, 0, b))
    vt_shape = jax.ShapeDtypeStruct((MLA_HEADS, VT_ROWS, T), BF16)
    return pl.pallas_call(
        _ctx_prep_kernel,
        grid=(nb,),
        in_specs=[cache_row(128), cache_row(128), cache_row(256), cache_row(256),
                  lw.spec("wk"), lw.spec("wv")],
        out_specs=[pl.BlockSpec((MLA_HEADS, p, MLA_SLOT), lambda b: (0, b, 0)), vt_spec, row(256), vt_spec],
        out_shape=[jax.ShapeDtypeStruct((MLA_HEADS, T, MLA_SLOT), BF16), vt_shape,
                   jax.ShapeDtypeStruct((T, 256), BF16), vt_shape],
        compiler_params=_params("arbitrary"),
        name="ctx_prep",
    )(ckv, kr_pad, cdk, cdv, lw["wk"], lw["wv"])


SAFE_DENOM = 2.0 ** -60
BOUND_SLACK = 1.02


def _scores(k_new, k_ctx, q):
    sn = _dot_nt(k_new(), q)
    sc = _dot_nt(k_ctx(), q) if k_ctx is not None else None
    return sn, sc


def _exact_shift(k_new, k_ctx, q):
    sn, sc = _scores(k_new, k_ctx, q)
    m = jnp.max(sn, axis=0, keepdims=True)
    if sc is not None:
        m = jnp.maximum(m, jnp.max(sc, axis=0, keepdims=True))
    return m


def _bound_shift(q, key_norm2):
    qf = q.astype(F32)
    ones = jnp.ones((8, q.shape[1]), BF16)
    q_norm2 = _dot_nt(ones, (qf * qf).astype(BF16))[0:1, :]
    return jnp.sqrt(q_norm2 * key_norm2) * BOUND_SLACK


def _max_row_norm2(k_new, k_ctx, col_sum):
    def one(k):
        kf = k.astype(F32)
        return jnp.max(_dot((kf * kf).astype(BF16), col_sum), axis=0, keepdims=True)
    m = one(k_new)
    if k_ctx is not None:
        m = jnp.maximum(m, one(k_ctx))
    return m * BOUND_SLACK


def _exp_stage(e_buf, k_new, k_ctx, q, shift, n_ctx):
    sn, sc = _scores(k_new, k_ctx, q)
    e_buf[n_ctx:, :] = jnp.exp2(sn - shift).astype(BF16)
    if sc is not None:
        e_buf[0:n_ctx, :] = jnp.exp2(sc - shift).astype(BF16)


def _value_stage(e_buf, vt_new, vt_ctx, n_ctx):
    o = _dot(vt_new(), e_buf[n_ctx:, :])
    if vt_ctx is not None:
        o = o + _dot(vt_ctx(), e_buf[0:n_ctx, :])
    return o


def _run_pipeline(n_maps, exp_stage, value_stage):
    exp_stage(0)
    for u in range(n_maps):
        if u + 1 < n_maps:
            exp_stage(u + 1)
        value_stage(u)


def _att_scratch(nk):
    return [pltpu.VMEM((8, 128), F32),
            pltpu.VMEM((MLA_HEADS * HEAD_V, ATT_TQ), F32),
            pltpu.VMEM((nk, ATT_TQ), BF16), pltpu.VMEM((nk, ATT_TQ), BF16)]


def _att_nsub(n):
    return 2 if n % (2 * ATT_TQ) == 0 else 1


def _mla_attn_kernel(*refs, has_ctx, nsub):
    if has_ctx:
        q_ref, k_ref, vt_ref, kc_ref, vtc_ref, o_ref, kn2, ot, e0, e1 = refs
        n_ctx = kc_ref.shape[1]
    else:
        q_ref, k_ref, vt_ref, o_ref, kn2, ot, e0, e1 = refs
        n_ctx = 0
    e_bufs = (e0, e1)

    @pl.when(pl.program_id(1) == 0)
    def _():
        ones = jnp.ones((MLA_SLOT, 128), BF16)
        for hh in range(MLA_HEADS):
            kn2[hh:hh + 1, :] = _max_row_norm2(k_ref[hh], kc_ref[hh] if has_ctx else None, ones)

    def run(exact):
        denoms = []

        def exp_stage(u):
            t, hh = divmod(u, MLA_HEADS)
            q = q_ref[hh, t * ATT_TQ:(t + 1) * ATT_TQ, :]
            k_new = lambda: k_ref[hh]
            k_ctx = (lambda: kc_ref[hh]) if has_ctx else None
            shift = _exact_shift(k_new, k_ctx, q) if exact else _bound_shift(q, kn2[hh:hh + 1, 0:1])
            _exp_stage(e_bufs[u % 2], k_new, k_ctx, q, shift, n_ctx)

        def value_stage(u):
            t, hh = divmod(u, MLA_HEADS)
            o = _value_stage(e_bufs[u % 2], lambda: vt_ref[hh],
                             (lambda: vtc_ref[hh]) if has_ctx else None, n_ctx)
            denom = o[HEAD_V:HEAD_V + 1, :]
            denoms.append(denom)
            ot[hh * HEAD_V:(hh + 1) * HEAD_V, :] = o[0:HEAD_V, :] * (1.0 / denom)
            if hh == MLA_HEADS - 1:
                o_ref[t * ATT_TQ:(t + 1) * ATT_TQ, :] = ot[...].T

        _run_pipeline(nsub * MLA_HEADS, exp_stage, value_stage)
        return jnp.min(functools.reduce(jnp.minimum, denoms))

    denom_min = run(exact=False)

    @pl.when(jnp.logical_not(denom_min >= SAFE_DENOM))
    def _():
        run(exact=True)


def _mla_attn(q, k, vt, ctx, *, nb, n):
    nsub = _att_nsub(n)
    tq = nsub * ATT_TQ
    npt = n // tq
    H, S = MLA_HEADS, MLA_SLOT
    in_specs = [
        pl.BlockSpec((H, tq, S), lambda b, j: (0, b * npt + j, 0)),
        pl.BlockSpec((H, n, S), lambda b, j: (0, b, 0)),
        pl.BlockSpec((H, VT_ROWS, n), lambda b, j: (0, 0, b)),
    ]
    args = [q, k, vt]
    n_ctx = 0
    if ctx is not None:
        n_ctx = ctx[0].shape[1] // nb
        in_specs += [
            pl.BlockSpec((H, n_ctx, S), lambda b, j: (0, b, 0)),
            pl.BlockSpec((H, VT_ROWS, n_ctx), lambda b, j: (0, 0, b)),
        ]
        args += list(ctx)
    return pl.pallas_call(
        functools.partial(_mla_attn_kernel, has_ctx=ctx is not None, nsub=nsub),
        grid=(nb, npt),
        in_specs=in_specs,
        out_specs=pl.BlockSpec((tq, 256), lambda b, j: (b * npt + j, 0)),
        out_shape=jax.ShapeDtypeStruct((nb * n, 256), F32),
        scratch_shapes=_att_scratch(n + n_ctx),
        compiler_params=_params("arbitrary", "arbitrary"),
        name="mla_attn_ctx" if ctx is not None else "mla_attn",
    )(*args)


def _diff_attn_kernel(*refs, has_ctx, nsub, lam_init):
    if has_ctx:
        lv_ref, g_ref, q_ref, k_ref, vt_ref, kc_ref, vtc_ref, o_ref, kn2, ot, e0, e1 = refs
        n_ctx = kc_ref.shape[0]
    else:
        lv_ref, g_ref, q_ref, k_ref, vt_ref, o_ref, kn2, ot, e0, e1 = refs
        n_ctx = 0
    e_bufs = (e0, e1)
    lv = lv_ref[...]
    lam = (jnp.exp(jnp.sum(lv[0:1] * lv[1:2], axis=-1, keepdims=True))
           - jnp.exp(jnp.sum(lv[2:3] * lv[3:4], axis=-1, keepdims=True)) + lam_init)
    lane128 = lax.broadcasted_iota(jnp.int32, (1, 128), 1)
    n_pairs = 2 * DIFF_HEADS

    @pl.when(pl.program_id(1) == 0)
    def _():
        dim = lax.broadcasted_iota(jnp.int32, (256, 128), 0)
        col = lax.broadcasted_iota(jnp.int32, (256, 128), 1)
        indicator = jnp.where(dim // DIFF_DIM == col, 1.0, 0.0).astype(BF16)
        kn2[0:1, :] = _max_row_norm2(k_ref[...], kc_ref[...] if has_ctx else None, indicator)

    def run(exact):
        denoms = []
        outs = {}

        def exp_stage(u):
            t, p = divmod(u, n_pairs)
            tile = slice((p * DIFF_DIM // 128) * 128, (p * DIFF_DIM // 128 + 1) * 128)
            k_new = lambda: k_ref[:, tile]
            k_ctx = (lambda: kc_ref[:, tile]) if has_ctx else None
            q = q_ref[t * ATT_TQ:(t + 1) * ATT_TQ, tile]
            lo = p * DIFF_DIM - tile.start
            in_pair = (lane128 >= lo) & (lane128 < lo + DIFF_DIM)
            qm = jnp.where(in_pair, q, jnp.zeros_like(q))
            shift = _exact_shift(k_new, k_ctx, qm) if exact else _bound_shift(qm, kn2[0:1, p:p + 1])
            _exp_stage(e_bufs[u % 2], k_new, k_ctx, qm, shift, n_ctx)

        def value_stage(u):
            t, p = divmod(u, n_pairs)
            hh = p // 2
            o = _value_stage(e_bufs[u % 2], lambda: vt_ref[hh],
                             (lambda: vtc_ref[hh]) if has_ctx else None, n_ctx)
            denom = o[HEAD_V:HEAD_V + 1, :]
            denoms.append(denom)
            outs[u] = (o[0:HEAD_V, :], denom)
            if p % 2 == 1:
                (o0, l0), (o1, l1) = outs.pop(u - 1), outs.pop(u)
                o = o0 * (1.0 / l0) - o1 * (lam / l1)
                msq = jnp.sum(o * o, axis=0, keepdims=True) * (1.0 / HEAD_V)
                ot[hh * HEAD_V:(hh + 1) * HEAD_V, :] = o * lax.rsqrt(msq + EPS)
            if p == n_pairs - 1:
                o_ref[t * ATT_TQ:(t + 1) * ATT_TQ, :] = (ot[...].T * g_ref[...]) * (1.0 - lam_init)

        _run_pipeline(nsub * n_pairs, exp_stage, value_stage)
        return jnp.min(functools.reduce(jnp.minimum, denoms))

    denom_min = run(exact=False)

    @pl.when(jnp.logical_not(denom_min >= SAFE_DENOM))
    def _():
        run(exact=True)


def _diff_attn(q, k, vt, ctx, lw, *, nb, n, lam_init):
    nsub = 1
    tq = nsub * ATT_TQ
    npt = n // tq
    in_specs = [
        lw.spec("diff_lambda"),
        lw.spec("diff_g"),
        pl.BlockSpec((tq, 256), lambda b, j: (b * npt + j, 0)),
        pl.BlockSpec((n, 256), lambda b, j: (b, 0)),
        pl.BlockSpec((DIFF_HEADS, VT_ROWS, n), lambda b, j: (0, 0, b)),
    ]
    args = [lw["diff_lambda"], lw["diff_g"], q, k, vt]
    n_ctx = 0
    if ctx is not None:
        n_ctx = ctx[0].shape[0] // nb
        in_specs += [pl.BlockSpec((n_ctx, 256), lambda b, j: (b, 0)),
                     pl.BlockSpec((DIFF_HEADS, VT_ROWS, n_ctx), lambda b, j: (0, 0, b))]
        args += list(ctx)
    return pl.pallas_call(
        functools.partial(_diff_attn_kernel, has_ctx=ctx is not None, nsub=nsub, lam_init=lam_init),
        grid=(nb, npt),
        in_specs=in_specs,
        out_specs=pl.BlockSpec((tq, 256), lambda b, j: (b * npt + j, 0)),
        out_shape=jax.ShapeDtypeStruct((nb * n, 256), F32),
        scratch_shapes=_att_scratch(n + n_ctx),
        compiler_params=_params("arbitrary", "arbitrary"),
        name="diff_attn_ctx" if ctx is not None else "diff_attn",
    )(*args)


def _shift_rows(v, k):
    return pltpu.roll(v, (-k) % v.shape[0], 0)


def _scan_strided(a_ref, b_ref, h_ref, row0, carry, n_rows, reverse):
    sub = lax.broadcasted_iota(jnp.int32, (8, 128), 0)
    span = 8 * SCAN_RUN
    order = tuple(range(SCAN_RUN))[::-1] if reverse else tuple(range(SCAN_RUN))
    starts = tuple(range(0, n_rows, span))[::-1] if reverse else tuple(range(0, n_rows, span))
    carries = []
    for lt in range(a_ref.shape[0]):
        c_in = carry[:, lt * 128:(lt + 1) * 128]
        for start in starts:
            tile = lambda ref, g: ref[lt, pl.ds(row0 + start + g, 8, stride=SCAN_RUN), :]
            a = [tile(a_ref, g) for g in range(SCAN_RUN)]
            b = [tile(b_ref, g) for g in range(SCAN_RUN)]
            h = {order[0]: b[order[0]]}
            p = {order[0]: a[order[0]]}
            for prev, g in zip(order, order[1:]):
                h[g] = a[g] * h[prev] + b[g]
                p[g] = a[g] * p[prev]
            pi, hi = p[order[-1]], h[order[-1]]
            for s in (1, 2, 4):
                shift = 8 - s if reverse else s
                valid = (sub < 8 - s) if reverse else (sub >= s)
                pr, hr = pltpu.roll(pi, shift, 0), pltpu.roll(hi, shift, 0)
                hi = jnp.where(valid, pi * hr + hi, hi)
                pi = jnp.where(valid, pi * pr, pi)
            one = 7 if reverse else 1
            first = (sub == 7) if reverse else (sub == 0)
            pe = jnp.where(first, 1.0, pltpu.roll(pi, one, 0))
            he = jnp.where(first, 0.0, pltpu.roll(hi, one, 0))
            c = pe * c_in + he
            for g in range(SCAN_RUN):
                h_ref[lt, pl.ds(start + g, 8, stride=SCAN_RUN), :] = h[g] + p[g] * c
            last = 0 if reverse else 7
            c_in = pi[last:last + 1, :] * c_in + hi[last:last + 1, :]
        carries.append(c_in)
    return jnp.concatenate(carries, axis=1)


def _sigmoid(x):
    return 0.5 * jnp.tanh(0.5 * x) + 0.5


def _gelu_tanh(x):
    return x * (0.5 * (1.0 + jnp.tanh(math.sqrt(2.0 / math.pi) * (x + 0.044715 * (x * x * x)))))


def _lru_kernel(u_ref, h0_ref, cw_ref, cb_ref, wg_ref, bg_ref, lam_ref, y_ref, st_ref,
                xpad, a1s, b1s, a0c, b0c, hc, *, N, T):
    W = LRU_WIDTH
    nc = N // T
    tiles = [slice(lt * 128, (lt + 1) * 128) for lt in range(W // 128)]
    zeros = jnp.zeros((HALO, W), F32)
    xpad[0:HALO, :] = zeros
    xpad[N + HALO:N + 2 * HALO, :] = zeros

    def fill(j, carry):
        r0 = pl.multiple_of(j * T, T)
        xpad[pl.ds(r0 + HALO, T), :] = u_ref[pl.ds(r0, T), 0:W]
        return carry

    lax.fori_loop(0, nc, fill, 0)

    z = -lam_ref[...]
    sp = jnp.maximum(z, 0.0) + jnp.log1p(jnp.exp(-jnp.abs(z)))
    cw = cw_ref[...]
    cb = cb_ref[...]
    bg = bg_ref[...]

    def fwd(j, carry):
        r0 = pl.multiple_of(j * T, T)
        ext = xpad[pl.ds(r0, T + 2 * HALO), :]
        body = slice(HALO, HALO + T)
        xc = cb
        for tap in range(4):
            xc = xc + _shift_rows(ext, tap - 1)[body] * cw[tap:tap + 1]
        g = _sigmoid(_dot(xc.astype(BF16), wg_ref[...]) + bg)
        ab = []
        for d in range(2):
            r = g[:, d * W:(d + 1) * W]
            i = g[:, (2 + d) * W:(3 + d) * W]
            log_a = (-LRU_C * r) * sp[d:d + 1]
            a = jnp.exp(log_a)
            bt = (jnp.sqrt(1.0 - a * a) * i) * xc
            ab.append((a, bt))
        for lt, lanes in enumerate(tiles):
            a0c[lt] = ab[0][0][:, lanes]
            b0c[lt] = ab[0][1][:, lanes]
            a1s[lt, pl.ds(r0, T), :] = ab[1][0][:, lanes]
            b1s[lt, pl.ds(r0, T), :] = ab[1][1][:, lanes]
        carry = _scan_strided(a0c, b0c, hc, 0, carry, T, reverse=False)
        for lt, lanes in enumerate(tiles):
            y_ref[pl.ds(r0, T), lanes] = hc[lt]
        return carry

    cf = lax.fori_loop(0, nc, fwd, h0_ref[0, 0:1, :])

    def bwd(jj, carry):
        r0 = pl.multiple_of((nc - 1 - jj) * T, T)
        carry = _scan_strided(a1s, b1s, hc, r0, carry, T, reverse=True)
        for lt, lanes in enumerate(tiles):
            gb = u_ref[pl.ds(r0, T), W + lt * 128:W + (lt + 1) * 128]
            y_ref[pl.ds(r0, T), lanes] = (y_ref[pl.ds(r0, T), lanes] + hc[lt]) * _gelu_tanh(gb)
        return carry

    cbw = lax.fori_loop(0, nc, bwd, h0_ref[0, 1:2, :])
    st_ref[0, 0:1, :] = cf
    st_ref[0, 1:2, :] = cbw


def _lru(u, h0, h0_block, lw, *, nb, n):
    T = min(n, 256)
    W = LRU_WIDTH
    return pl.pallas_call(
        functools.partial(_lru_kernel, N=n, T=T),
        grid=(nb,),
        in_specs=[
            pl.BlockSpec((n, 2 * W), lambda b: (b, 0)),
            pl.BlockSpec((1, 2, W), lambda b: (h0_block(b), 0, 0)),
            lw.spec("conv_w"), lw.spec("conv_b"), lw.spec("w_gate"), lw.spec("b_gate"),
            lw.spec("lru_lambda"),
        ],
        out_specs=[
            pl.BlockSpec((n, W), lambda b: (b, 0)),
            pl.BlockSpec((1, 2, W), lambda b: (b, 0, 0)),
        ],
        out_shape=[
            jax.ShapeDtypeStruct((nb * n, W), F32),
            jax.ShapeDtypeStruct((nb, 2, W), F32),
        ],
        scratch_shapes=[
            pltpu.VMEM((n + 2 * HALO, W), F32),
            pltpu.VMEM((W // 128, n, 128), F32),
            pltpu.VMEM((W // 128, n, 128), F32),
            pltpu.VMEM((W // 128, T, 128), F32),
            pltpu.VMEM((W // 128, T, 128), F32),
            pltpu.VMEM((W // 128, T, 128), F32),
        ],
        compiler_params=_params("arbitrary"),
        name="rglru",
    )(u, h0, lw["conv_w"], lw["conv_b"], lw["w_gate"], lw["b_gate"], lw["lru_lambda"])


def _pool_kernel(u_ref, wp_ref, sc_ref, y_ref, xpad, *, N, T):
    W = GROUP_WIDTH
    nc = N // T
    zeros = jnp.zeros((HALO, W), F32)
    xpad[0:HALO, :] = zeros
    xpad[N + HALO:N + 2 * HALO, :] = zeros

    def fill(j, carry):
        r0 = pl.multiple_of(j * T, T)
        xpad[pl.ds(r0 + HALO, T), :] = u_ref[pl.ds(r0, T), :]
        return carry

    lax.fori_loop(0, nc, fill, 0)

    grp = lax.broadcasted_iota(jnp.int32, (1, W), 1) // POOL_CH
    half = jnp.where(grp == 0, 1, jnp.where(grp == 1, 2, jnp.where(grp == 2, 4, 8)))
    scale = sc_ref[...]

    def chunk(j, carry):
        r0 = pl.multiple_of(j * T, T)
        ext = xpad[pl.ds(r0, T + 2 * HALO), :]
        w2 = _shift_rows(ext, -1) + ext
        w4 = _shift_rows(w2, -1) + _shift_rows(w2, 1)
        w8 = _shift_rows(w4, -2) + _shift_rows(w4, 2)
        w16 = _shift_rows(w8, -4) + _shift_rows(w8, 4)
        ws = jnp.where(grp == 0, w2, jnp.where(grp == 1, w4, jnp.where(grp == 2, w8, w16)))
        body = slice(HALO, HALO + T)
        t = r0 + lax.broadcasted_iota(jnp.int32, (T, W), 0)
        cnt = (jnp.minimum(t + half, N) - jnp.maximum(t - half, 0)).astype(F32)
        d = ws[body] / cnt - ext[body]
        y_ref[pl.ds(r0, T), :] = _dot(d.astype(BF16), wp_ref[...]) * scale
        return carry

    lax.fori_loop(0, nc, chunk, 0)


def _pool(u, lw, *, nb, n):
    W = GROUP_WIDTH
    T = min(n, 256)
    return pl.pallas_call(
        functools.partial(_pool_kernel, N=n, T=T),
        grid=(nb,),
        in_specs=[pl.BlockSpec((n, W), lambda b: (b, 0)), lw.spec("w_pool"), lw.spec("pool_scale")],
        out_specs=pl.BlockSpec((n, W), lambda b: (b, 0)),
        out_shape=jax.ShapeDtypeStruct((nb * n, W), F32),
        scratch_shapes=[pltpu.VMEM((n + 2 * HALO, W), F32)],
        compiler_params=_params("arbitrary"),
        name="pool_mixer",
    )(u, lw["w_pool"], lw["pool_scale"])


def _mix_ffn_kernel(*refs, final):
    if final:
        (x_ref, ya_ref, yb_ref, yc_ref, yd_ref, mod_ref, g2_ref, wo_ref, wg_ref, wu_ref, wd_ref,
         gf_ref, o_ref) = refs
    else:
        (x_ref, ya_ref, yb_ref, yc_ref, yd_ref, mod_ref, g2_ref, wo_ref, wg_ref, wu_ref, wd_ref,
         o_ref) = refs
    mod = mod_ref[0]
    gate1 = mod[:, 2 * D_MODEL:3 * D_MODEL]
    sh2 = mod[:, 3 * D_MODEL:4 * D_MODEL]
    sc2 = mod[:, 4 * D_MODEL:5 * D_MODEL]
    gate2 = mod[:, 5 * D_MODEL:6 * D_MODEL]
    mix = None
    for i, y_ref in enumerate((ya_ref, yb_ref, yc_ref, yd_ref)):
        part = _dot(y_ref[...].astype(BF16), wo_ref[i * GROUP_WIDTH:(i + 1) * GROUP_WIDTH, :])
        mix = part if mix is None else mix + part
    x1 = x_ref[...] + gate1 * mix
    h = _rms_rows(x1, D_MODEL) * g2_ref[...]
    hb = (h * (1.0 + sc2) + sh2).astype(BF16)
    ff = None
    for lo, hi in FF_CHUNKS:
        g = _dot(hb, wg_ref[:, lo:hi])
        up = _dot(hb, wu_ref[:, lo:hi])
        act = ((g * jax.nn.sigmoid(g)) * up).astype(BF16)
        part = _dot(act, wd_ref[lo:hi, :])
        ff = part if ff is None else ff + part
    x2 = x1 + gate2 * ff
    if final:
        x2 = _rms_rows(x2, D_MODEL) * gf_ref[...]
    o_ref[...] = x2


def _mix_ffn(x, ys, mod, lw, gf, *, nb, n, final):
    T = nb * n
    tm = TOKEN_TILE
    npt = n // tm

    def tok(width):
        return pl.BlockSpec((tm, width), lambda i: (i, 0))

    wnames = ("g2", "w_out", "w_gate_ff", "w_up_ff", "w_down")
    in_specs = [tok(D_MODEL), tok(256), tok(256), tok(256), tok(256),
                mod.spec(lambda i: i // npt)] + [lw.spec(nm) for nm in wnames]
    args = [x, *ys, mod.table] + [lw[nm] for nm in wnames]
    if final:
        in_specs.append(_resident((1, D_MODEL)))
        args.append(gf)
    return pl.pallas_call(
        functools.partial(_mix_ffn_kernel, final=final),
        grid=(T // tm,),
        in_specs=in_specs,
        out_specs=tok(D_MODEL),
        out_shape=jax.ShapeDtypeStruct((T, D_MODEL), F32),
        compiler_params=_params("arbitrary"),
        name="mix_ffn_final" if final else "mix_ffn",
    )(*args)


def _block_diag(w):
    L, G, c, e = w.shape
    return jnp.einsum('lgce,gh->lgche', w, jnp.eye(G, dtype=w.dtype)).reshape(L, G * c, G * e)


def _rot_cols(w):
    return jnp.concatenate([-w[..., 16:32], w[..., 0:16]], axis=-1)


def _stack_weights(p):
    w_in = p["w_in"]
    o1 = MLA_Q_RANK
    o2 = o1 + MLA_KV_RANK
    o3 = o2 + MLA_ROPE
    c_q, c_kv, k_r, rest = w_in[..., :o1], w_in[..., o1:o2], w_in[..., o2:o3], w_in[..., o3:]
    z = lambda n: jnp.zeros((DEPTH, D_MODEL, n), F32)
    w_in_eff = jnp.concatenate([c_q, k_r, z(32), c_kv, z(64), _rot_cols(k_r), z(32), rest], axis=-1)

    w_uq = p["mla_w_uq"]
    qd = MLA_NOPE + MLA_ROPE
    wq_parts, wqr_parts = [], []
    zq = lambda n: jnp.zeros((DEPTH, MLA_Q_RANK, n), F32)
    for h in range(MLA_HEADS):
        wh = w_uq[..., h * qd:(h + 1) * qd]
        wq_parts += [wh, zq(MLA_SLOT - qd)]
        wqr_parts += [zq(MLA_NOPE), _rot_cols(wh[..., MLA_NOPE:]), zq(MLA_SLOT - qd)]
    pad_rows = lambda w: jnp.pad(w, ((0, 0), (0, 256 - MLA_Q_RANK), (0, 0)))
    w_ukv = p["mla_w_ukv"]
    wk_parts, wv_parts = [], []
    zk = jnp.zeros((DEPTH, MLA_KV_RANK, MLA_SLOT - MLA_NOPE), F32)
    for h in range(MLA_HEADS):
        base = h * (MLA_NOPE + MLA_V)
        wk_parts += [w_ukv[..., base:base + MLA_NOPE], zk]
        wv_parts.append(w_ukv[..., base + MLA_NOPE:base + MLA_NOPE + MLA_V])

    w_r, w_i, b_r, b_i = p["lru_w_r"], p["lru_w_i"], p["lru_b_r"], p["lru_b_i"]
    w_gate = jnp.concatenate([_block_diag(w_r[:, 0]), _block_diag(w_r[:, 1]),
                              _block_diag(w_i[:, 0]), _block_diag(w_i[:, 1])], axis=-1)
    b_gate = jnp.concatenate([b_r[:, 0], b_r[:, 1], b_i[:, 0], b_i[:, 1]], axis=-1)
    w_gu = p["w_gu"]
    row = lambda v: v[:, None, :]
    return {
        "g1": row(p["norm1_g"]),
        "g2": row(p["norm2_g"]),
        "w_in": w_in_eff.astype(BF16),
        "gq": row(jnp.pad(p["mla_q_norm_g"], ((0, 0), (0, 256 - MLA_Q_RANK)))),
        "gkv": row(p["mla_kv_norm_g"]),
        "wq": pad_rows(jnp.concatenate(wq_parts, axis=-1)).astype(BF16),
        "wqr": pad_rows(jnp.concatenate(wqr_parts, axis=-1)).astype(BF16),
        "wk": jnp.concatenate(wk_parts, axis=-1).astype(BF16),
        "wv": jnp.concatenate(wv_parts, axis=-1).astype(BF16),
        "conv_w": p["lru_conv_w"],
        "conv_b": row(p["lru_conv_b"]),
        "w_gate": w_gate.astype(BF16),
        "b_gate": row(b_gate),
        "lru_lambda": p["lru_lambda"],
        "w_pool": _block_diag(p["pool_w"]).astype(BF16),
        "pool_scale": row(p["pool_scale"]),
        "diff_lambda": p["diff_lambda"],
        "diff_g": row(jnp.tile(p["diff_norm_g"], (1, DIFF_HEADS))),
        "w_out": p["w_out"].astype(BF16),
        "w_gate_ff": w_gu[..., :FF_HIDDEN].astype(BF16),
        "w_up_ff": w_gu[..., FF_HIDDEN:].astype(BF16),
        "w_down": p["w_down"].astype(BF16),
    }


def _rope_tables(n, positional):
    quarter = MLA_ROPE // 4
    if positional:
        t = jnp.arange(n)
        row = (t // GRID_W).astype(F32)
        col = (t % GRID_W).astype(F32)
        inv = ROPE_BASE ** (-jnp.arange(quarter, dtype=F32) / quarter)
        ang = jnp.concatenate([row[:, None] * inv, col[:, None] * inv], axis=-1)
        cos, sin = jnp.cos(ang), jnp.sin(ang)
    else:
        cos, sin = jnp.ones((n, 16), F32), jnp.zeros((n, 16), F32)
    one = lambda w: jnp.ones((n, w), F32)
    zero = lambda w: jnp.zeros((n, w), F32)
    scale = LOG2E / math.sqrt(MLA_NOPE + MLA_ROPE)
    return {
        "cosq": jnp.concatenate([one(64), cos, cos, one(32)], axis=1) * scale,
        "sinq": jnp.concatenate([zero(64), sin, sin, zero(32)], axis=1) * scale,
        "cosk": jnp.concatenate([zero(64), cos, cos, zero(32)], axis=1),
        "sink": jnp.concatenate([zero(64), sin, sin, zero(32)], axis=1),
        "cosd": jnp.tile(jnp.concatenate([cos, cos], axis=1), (1, 8)),
        "sina": jnp.tile(jnp.concatenate([-sin, zero(16)], axis=1), (1, 8)),
        "sinb": jnp.tile(jnp.concatenate([zero(16), sin], axis=1), (1, 8)),
    }


def _layer(x, mod, lw, tabs, layer_idx, ctx, gf, *, nb, n, final):
    emit_cache = ctx is None
    tok_nb, tok_n = (1, nb * n) if mod.shared else (nb, n)
    outs = _inproj(x, mod, lw, tabs, nb=tok_nb, n=tok_n, emit_cache=emit_cache)
    q, k, vt, u_lru, u_pool, dq, dk, dvt = outs[:8]
    lam_init = 0.8 - 0.6 * math.exp(-0.3 * layer_idx)
    if ctx is None:
        h0 = jnp.zeros((1, 2, LRU_WIDTH), F32)
        h0_block = lambda b: 0
        mla_ctx = diff_ctx = None
    else:
        ckv, kr_pad, cdk, cdv, h0 = ctx
        p = ckv.shape[0] // (nb * DEPTH)
        h0_block = lambda b: b * DEPTH + layer_idx
        kc, vtc, dkc, dvtc = _ctx_prep(ckv, kr_pad, cdk, cdv, lw, nb=nb, p=p)
        mla_ctx = (kc, vtc)
        diff_ctx = (dkc, dvtc)
    y_mla = _mla_attn(q, k, vt, mla_ctx, nb=nb, n=n)
    y_lru, st = _lru(u_lru, h0, h0_block, lw, nb=nb, n=n)
    y_pool = _pool(u_pool, lw, nb=nb, n=n)
    y_diff = _diff_attn(dq, dk, dvt, diff_ctx, lw, nb=nb, n=n, lam_init=lam_init)
    x2 = _mix_ffn(x, (y_mla, y_lru, y_pool, y_diff), mod, lw, gf, nb=tok_nb, n=tok_n, final=final)
    cache = (outs[8], outs[9][:, 64:96], outs[10], outs[11], st) if emit_cache else None
    return x2, cache


def kernel(x_prompt, x_sample, cache_mla_ckv, cache_mla_krope, cache_diff_k, cache_diff_v, state_lru,
           c, c_ctx, w_ada, b_ada, norm1_g, norm2_g, w_in, mla_q_norm_g, mla_w_uq, mla_kv_norm_g,
           mla_w_ukv, lru_conv_w, lru_conv_b, lru_w_r, lru_b_r, lru_w_i, lru_b_i, lru_lambda, pool_w,
           pool_scale, diff_lambda, diff_norm_g, w_out, w_gu, w_down, final_norm_g):
    p = {
        "norm1_g": norm1_g, "norm2_g": norm2_g, "w_in": w_in, "mla_q_norm_g": mla_q_norm_g,
        "mla_w_uq": mla_w_uq, "mla_kv_norm_g": mla_kv_norm_g, "mla_w_ukv": mla_w_ukv,
        "lru_conv_w": lru_conv_w, "lru_conv_b": lru_conv_b, "lru_w_r": lru_w_r, "lru_b_r": lru_b_r,
        "lru_w_i": lru_w_i, "lru_b_i": lru_b_i, "lru_lambda": lru_lambda, "pool_w": pool_w,
        "pool_scale": pool_scale, "diff_lambda": diff_lambda, "diff_norm_g": diff_norm_g,
        "w_out": w_out, "w_gu": w_gu, "w_down": w_down,
    }
    Bp, Np, _ = x_prompt.shape
    Bs, Ns, _ = x_sample.shape
    P = cache_mla_ckv.shape[2]

    cond_all = jnp.concatenate([c, c_ctx[None, :], jnp.zeros((MOD_ROWS - Bs - 1, D_MODEL), F32)], axis=0)
    mod_table = _ada(cond_all, w_ada, b_ada).reshape(DEPTH * MOD_ROWS, 1, 6 * D_MODEL)
    tabs_p = _rope_tables(Bp * Np, positional=False)
    tabs_s = _rope_tables(Ns, positional=True)
    kr_pad = jnp.pad(cache_mla_krope, ((0, 0), (0, 0), (0, 0), (MLA_NOPE, MLA_SLOT - MLA_NOPE - MLA_ROPE)))
    flat = lambda a, w: a.reshape(Bs * DEPTH * P, w)
    ctx = (flat(cache_mla_ckv, MLA_KV_RANK), flat(kr_pad, MLA_SLOT), flat(cache_diff_k, 256),
           flat(cache_diff_v, 256), state_lru.reshape(Bs * DEPTH, 2, LRU_WIDTH))
    gf = final_norm_g[None, :]
    stacked = _stack_weights(p)

    xp = x_prompt.reshape(Bp * Np, D_MODEL)
    xs = x_sample.reshape(Bs * Ns, D_MODEL)
    caches = []
    for l in range(DEPTH):
        lw = _LayerWeights(stacked, l)
        final = l == DEPTH - 1
        mod_p = _Mod(mod_table, l * MOD_ROWS + Bs, shared=True)
        mod_s = _Mod(mod_table, l * MOD_ROWS, shared=False)
        xp, cache = _layer(xp, mod_p, lw, tabs_p, l, None, gf, nb=Bp, n=Np, final=final)
        caches.append(cache)
        xs, _ = _layer(xs, mod_s, lw, tabs_s, l, ctx, gf, nb=Bs, n=Ns, final=final)

    stack = lambda i, w: jnp.stack([cc[i].reshape(Bp, Np, w) for cc in caches], axis=1)
    new_mla_ckv = stack(0, MLA_KV_RANK)
    new_mla_krope = stack(1, MLA_ROPE)
    new_diff_k = stack(2, 256).reshape(Bp, DEPTH, Np, DIFF_HEADS, 2, DIFF_DIM)
    new_diff_v = stack(3, 256).reshape(Bp, DEPTH, Np, DIFF_HEADS, 2 * DIFF_DIM)
    new_state_lru = jnp.stack([cc[4] for cc in caches], axis=1)
    return (xp.reshape(Bp, Np, D_MODEL), xs.reshape(Bs, Ns, D_MODEL),
            new_mla_ckv, new_mla_krope, new_diff_k, new_diff_v, new_state_lru)
```

```python
import functools
import math

import jax
import jax.numpy as jnp
from jax import lax
from jax.experimental import pallas as pl
from jax.experimental.pallas import tpu as pltpu

F32 = jnp.float32
BF16 = jnp.bfloat16

D_MODEL = 1024
DEPTH = 2
GRID_W = 64
GROUP_WIDTH = 256
MLA_HEADS = 4
MLA_NOPE = 64
MLA_ROPE = 32
MLA_V = 64
MLA_Q_RANK = 192
MLA_KV_RANK = 128
MLA_SLOT = 128
LRU_WIDTH = 256
LRU_C = 8.0
POOL_WINDOWS = (2, 4, 8, 16)
POOL_CH = 64
DIFF_HEADS = 4
DIFF_DIM = 32
HEAD_V = 64
FF_HIDDEN = 2816
FF_CHUNKS = ((0, 1536), (1536, 2816))
ROPE_BASE = 10000.0
EPS = 1e-6
IN_EFF = 2048
HALO = 8
SCAN_RUN = 4
VT_ROWS = 80
ATT_TQ = 256
TOKEN_TILE = 512
MOD_ROWS = 16
LOG2E = math.log2(math.e)

VMEM_LIMIT_BYTES = 56 * 1024 * 1024

_NT = (((1,), (1,)), ((), ()))


def _params(*sem):
    return pltpu.CompilerParams(dimension_semantics=sem, vmem_limit_bytes=VMEM_LIMIT_BYTES)


def _resident(shape):
    zeros = (0,) * len(shape)
    return pl.BlockSpec(shape, lambda *_: zeros, pipeline_mode=pl.Buffered(1))


def _dot(a, b):
    return jnp.dot(a, b, preferred_element_type=F32)


def _dot_nt(a, b):
    return lax.dot_general(a, b, _NT, preferred_element_type=F32)


def _rms_rows(x, width):
    ms = jnp.sum(x * x, axis=-1, keepdims=True) * (1.0 / width)
    return x * lax.rsqrt(ms + EPS)


def _store_vt(vt_ref, v):
    vt = v.T
    rows = v.shape[0]
    pad = VT_ROWS - HEAD_V
    ones_row = jnp.where(lax.broadcasted_iota(jnp.int32, (pad, rows), 0) == 0, 1.0, 0.0).astype(BF16)
    for hh in range(vt_ref.shape[0]):
        vt_ref[hh, 0:HEAD_V, :] = vt[hh * HEAD_V:(hh + 1) * HEAD_V, :].astype(BF16)
        vt_ref[hh, HEAD_V:VT_ROWS, :] = ones_row


class _Mod:
    def __init__(self, table, row0, shared):
        self.table, self.row0, self.shared = table, row0, shared

    def spec(self, batch_of):
        row0 = self.row0
        if self.shared:
            return pl.BlockSpec((1, 1, 6 * D_MODEL), lambda *g: (row0, 0, 0))
        return pl.BlockSpec((1, 1, 6 * D_MODEL), lambda *g: (row0 + batch_of(*g), 0, 0))


class _LayerWeights:
    def __init__(self, stacked, layer):
        self.stacked, self.layer = stacked, layer

    def __getitem__(self, name):
        return self.stacked[name]

    def spec(self, name):
        layer = self.layer
        _, rows, cols = self.stacked[name].shape
        return pl.BlockSpec((None, rows, cols), lambda *_: (layer, 0, 0), pipeline_mode=pl.Buffered(1))


def _ada_kernel(cond_ref, w_ref, b_ref, out_ref):
    c = cond_ref[...]
    s = c * jax.nn.sigmoid(c)
    out_ref[0] = _dot(s.astype(BF16), w_ref[0].astype(BF16)) + b_ref[0]


def _ada(cond_all, w_ada, b_ada):
    rows = cond_all.shape[0]
    tn = 1536
    return pl.pallas_call(
        _ada_kernel,
        grid=(DEPTH, 6 * D_MODEL // tn),
        in_specs=[
            pl.BlockSpec((rows, D_MODEL), lambda l, j: (0, 0)),
            pl.BlockSpec((1, D_MODEL, tn), lambda l, j: (l, 0, j)),
            pl.BlockSpec((1, 1, tn), lambda l, j: (l, 0, j)),
        ],
        out_specs=pl.BlockSpec((1, rows, tn), lambda l, j: (l, 0, j)),
        out_shape=jax.ShapeDtypeStruct((DEPTH, rows, 6 * D_MODEL), F32),
        compiler_params=_params("arbitrary", "arbitrary"),
        name="ada_mod",
    )(cond_all, w_ada, b_ada.reshape(DEPTH, 1, 6 * D_MODEL))


def _inproj_kernel(x_ref, mod_ref, g1_ref, win_ref, gq_ref, gkv_ref, wq_ref, wqr_ref, wk_ref, wv_ref,
                   cosq_ref, sinq_ref, cosk_ref, sink_ref, cosd_ref, sina_ref, sinb_ref,
                   q_out, k_out, vt_out, lru_out, pool_out, dq_out, dk_out, dvt_out, *cache_outs):
    x = x_ref[...]
    mod = mod_ref[0]
    sh1 = mod[:, 0:D_MODEL]
    sc1 = mod[:, D_MODEL:2 * D_MODEL]
    h = _rms_rows(x, D_MODEL) * g1_ref[...]
    hb = (h * (1.0 + sc1) + sh1).astype(BF16)

    u_mla = _dot(hb, win_ref[:, 0:512])
    u_pd = _dot(hb, win_ref[:, 1024:1536])
    u_kv = _dot(hb, win_ref[:, 1536:2048])
    t01 = u_mla[:, 0:256]
    lane = lax.broadcasted_iota(jnp.int32, (1, 256), 1)
    cq = jnp.where(lane < MLA_Q_RANK, t01, 0.0)
    cqn = (_rms_rows(cq, MLA_Q_RANK) * gq_ref[...]).astype(BF16)
    qa = _dot(cqn, wq_ref[...])
    qr = _dot(cqn, wqr_ref[...])
    cosq = cosq_ref[...]
    sinq = sinq_ref[...]
    ckv = u_mla[:, 256:384]
    lat = _rms_rows(ckv, MLA_KV_RANK) * gkv_ref[...]
    latb = lat.astype(BF16)
    kk = _dot(latb, wk_ref[...])
    _store_vt(vt_out, _dot(latb, wv_ref[...]))
    t1 = t01[:, 128:256]
    t3 = u_mla[:, 384:512]
    kro = t1 * cosk_ref[...] + t3 * sink_ref[...]
    for hh in range(MLA_HEADS):
        sl = slice(hh * MLA_SLOT, (hh + 1) * MLA_SLOT)
        q_out[hh] = (qa[:, sl] * cosq + qr[:, sl] * sinq).astype(q_out.dtype)
        k_out[hh] = (kk[:, sl] + kro).astype(k_out.dtype)

    lru_out[...] = _dot(hb, win_ref[:, 512:1024])
    pool_out[...] = u_pd[:, 0:256]

    cosd = cosd_ref[...]
    sina = sina_ref[...]
    sinb = sinb_ref[...]

    def rope(t):
        return t * cosd + pltpu.roll(t, 256 - 16, 1) * sina + pltpu.roll(t, 16, 1) * sinb

    dq = u_pd[:, 256:512]
    dk = u_kv[:, 0:256]
    dv = u_kv[:, 256:512]
    dq_out[...] = (rope(dq) * (LOG2E / math.sqrt(DIFF_DIM))).astype(dq_out.dtype)
    dk_out[...] = rope(dk).astype(dk_out.dtype)
    _store_vt(dvt_out, dv)

    if cache_outs:
        lat_out, kr_out, dk_raw_out, dv_raw_out = cache_outs
        lat_out[...] = lat
        kr_out[...] = t1
        dk_raw_out[...] = dk
        dv_raw_out[...] = dv


def _inproj(x, mod, lw, tabs, *, nb, n, emit_cache):
    T = nb * n
    tm = TOKEN_TILE
    npt = n // tm
    row_blk = lambda j, b: b * npt + j

    def tok(width):
        return pl.BlockSpec((tm, width), lambda j, b: (row_blk(j, b), 0))

    def tab(width):
        return pl.BlockSpec((tm, width), lambda j, b: (j, 0))

    head = pl.BlockSpec((MLA_HEADS, tm, MLA_SLOT), lambda j, b: (0, row_blk(j, b), 0))
    vt_spec = pl.BlockSpec((MLA_HEADS, VT_ROWS, tm), lambda j, b: (0, 0, row_blk(j, b)))
    wnames = ("g1", "w_in", "gq", "gkv", "wq", "wqr", "wk", "wv")
    in_specs = [tok(D_MODEL), mod.spec(lambda j, b: b)] + [lw.spec(nm) for nm in wnames] + [
        tab(128), tab(128), tab(128), tab(128), tab(256), tab(256), tab(256)]
    out_specs = [head, head, vt_spec, tok(512), tok(256), tok(256), tok(256), vt_spec]
    vt_shape = jax.ShapeDtypeStruct((MLA_HEADS, VT_ROWS, T), BF16)
    out_shape = [
        jax.ShapeDtypeStruct((MLA_HEADS, T, MLA_SLOT), BF16),
        jax.ShapeDtypeStruct((MLA_HEADS, T, MLA_SLOT), BF16),
        vt_shape,
        jax.ShapeDtypeStruct((T, 512), F32),
        jax.ShapeDtypeStruct((T, 256), F32),
        jax.ShapeDtypeStruct((T, 256), BF16),
        jax.ShapeDtypeStruct((T, 256), BF16),
        vt_shape,
    ]
    if emit_cache:
        out_specs += [tok(128), tok(128), tok(256), tok(256)]
        out_shape += [jax.ShapeDtypeStruct((T, 128), F32), jax.ShapeDtypeStruct((T, 128), F32),
                      jax.ShapeDtypeStruct((T, 256), F32), jax.ShapeDtypeStruct((T, 256), F32)]
    return pl.pallas_call(
        _inproj_kernel,
        grid=(npt, nb),
        in_specs=in_specs,
        out_specs=out_specs,
        out_shape=out_shape,
        compiler_params=_params("arbitrary", "arbitrary"),
        name="inproj_cache" if emit_cache else "inproj",
    )(x, mod.table, *[lw[nm] for nm in wnames],
      tabs["cosq"], tabs["sinq"], tabs["cosk"], tabs["sink"], tabs["cosd"], tabs["sina"], tabs["sinb"])


def _ctx_prep_kernel(ckv_ref, kr_ref, dk_ref, dv_ref, wk_ref, wv_ref, k_out, vt_out, dk_out, dvt_out):
    latb = ckv_ref[...].astype(BF16)
    kk = _dot(latb, wk_ref[...])
    kr = kr_ref[...]
    for hh in range(MLA_HEADS):
        k_out[hh] = (kk[:, hh * MLA_SLOT:(hh + 1) * MLA_SLOT] + kr).astype(k_out.dtype)
    _store_vt(vt_out, _dot(latb, wv_ref[...]))
    dk_out[...] = dk_ref[...].astype(dk_out.dtype)
    _store_vt(dvt_out, dv_ref[...])


def _ctx_prep(ckv, kr_pad, cdk, cdv, lw, *, nb, p):
    T = nb * p
    layer = lw.layer
    cache_row = lambda w: pl.BlockSpec((p, w), lambda b: (b * DEPTH + layer, 0))
    row = lambda w: pl.BlockSpec((p, w), lambda b: (b, 0))
    vt_spec = pl.BlockSpec((MLA_HEADS, VT_ROWS, p), lambda b: (0, 0, b))
    vt_shape = jax.ShapeDtypeStruct((MLA_HEADS, VT_ROWS, T), BF16)
    return pl.pallas_call(
        _ctx_prep_kernel,
        grid=(nb,),
        in_specs=[cache_row(128), cache_row(128), cache_row(256), cache_row(256),
                  lw.spec("wk"), lw.spec("wv")],
        out_specs=[pl.BlockSpec((MLA_HEADS, p, MLA_SLOT), lambda b: (0, b, 0)), vt_spec, row(256), vt_spec],
        out_shape=[jax.ShapeDtypeStruct((MLA_HEADS, T, MLA_SLOT), BF16), vt_shape,
                   jax.ShapeDtypeStruct((T, 256), BF16), vt_shape],
        compiler_params=_params("arbitrary"),
        name="ctx_prep",
    )(ckv, kr_pad, cdk, cdv, lw["wk"], lw["wv"])


SAFE_DENOM = 2.0 ** -60
BOUND_SLACK = 1.02


def _scores(k_new, k_ctx, q):
    sn = _dot_nt(k_new(), q)
    sc = _dot_nt(k_ctx(), q) if k_ctx is not None else None
    return sn, sc


def _exact_shift(k_new, k_ctx, q):
    sn, sc = _scores(k_new, k_ctx, q)
    m = jnp.max(sn, axis=0, keepdims=True)
    if sc is not None:
        m = jnp.maximum(m, jnp.max(sc, axis=0, keepdims=True))
    return m


def _bound_shift(q, key_norm2):
    qf = q.astype(F32)
    ones = jnp.ones((8, q.shape[1]), BF16)
    q_norm2 = _dot_nt(ones, (qf * qf).astype(BF16))[0:1, :]
    return jnp.sqrt(q_norm2 * key_norm2) * BOUND_SLACK


def _max_row_norm2(k_new, k_ctx, col_sum):
    def one(k):
        kf = k.astype(F32)
        return jnp.max(_dot((kf * kf).astype(BF16), col_sum), axis=0, keepdims=True)
    m = one(k_new)
    if k_ctx is not None:
        m = jnp.maximum(m, one(k_ctx))
    return m * BOUND_SLACK


def _exp_stage(e_buf, k_new, k_ctx, q, shift, n_ctx):
    sn, sc = _scores(k_new, k_ctx, q)
    e_buf[n_ctx:, :] = jnp.exp2(sn - shift).astype(BF16)
    if sc is not None:
        e_buf[0:n_ctx, :] = jnp.exp2(sc - shift).astype(BF16)


def _value_stage(e_buf, vt_new, vt_ctx, n_ctx):
    o = _dot(vt_new(), e_buf[n_ctx:, :])
    if vt_ctx is not None:
        o = o + _dot(vt_ctx(), e_buf[0:n_ctx, :])
    return o


def _run_pipeline(n_maps, exp_stage, value_stage):
    exp_stage(0)
    for u in range(n_maps):
        if u + 1 < n_maps:
            exp_stage(u + 1)
        value_stage(u)


def _att_scratch(nk):
    return [pltpu.VMEM((8, 128), F32),
            pltpu.VMEM((MLA_HEADS * HEAD_V, ATT_TQ), F32),
            pltpu.VMEM((nk, ATT_TQ), BF16), pltpu.VMEM((nk, ATT_TQ), BF16)]


def _att_nsub(n):
    return 2 if n % (2 * ATT_TQ) == 0 else 1


def _mla_attn_kernel(*refs, has_ctx, nsub):
    if has_ctx:
        q_ref, k_ref, vt_ref, kc_ref, vtc_ref, o_ref, kn2, ot, e0, e1 = refs
        n_ctx = kc_ref.shape[1]
    else:
        q_ref, k_ref, vt_ref, o_ref, kn2, ot, e0, e1 = refs
        n_ctx = 0
    e_bufs = (e0, e1)

    @pl.when(pl.program_id(1) == 0)
    def _():
        ones = jnp.ones((MLA_SLOT, 128), BF16)
        for hh in range(MLA_HEADS):
            kn2[hh:hh + 1, :] = _max_row_norm2(k_ref[hh], kc_ref[hh] if has_ctx else None, ones)

    def run(exact):
        denoms = []

        def exp_stage(u):
            t, hh = divmod(u, MLA_HEADS)
            q = q_ref[hh, t * ATT_TQ:(t + 1) * ATT_TQ, :]
            k_new = lambda: k_ref[hh]
            k_ctx = (lambda: kc_ref[hh]) if has_ctx else None
            shift = _exact_shift(k_new, k_ctx, q) if exact else _bound_shift(q, kn2[hh:hh + 1, 0:1])
            _exp_stage(e_bufs[u % 2], k_new, k_ctx, q, shift, n_ctx)

        def value_stage(u):
            t, hh = divmod(u, MLA_HEADS)
            o = _value_stage(e_bufs[u % 2], lambda: vt_ref[hh],
                             (lambda: vtc_ref[hh]) if has_ctx else None, n_ctx)
            denom = o[HEAD_V:HEAD_V + 1, :]
            denoms.append(denom)
            ot[hh * HEAD_V:(hh + 1) * HEAD_V, :] = o[0:HEAD_V, :] * (1.0 / denom)
            if hh == MLA_HEADS - 1:
                o_ref[t * ATT_TQ:(t + 1) * ATT_TQ, :] = ot[...].T

        _run_pipeline(nsub * MLA_HEADS, exp_stage, value_stage)
        return jnp.min(functools.reduce(jnp.minimum, denoms))

    denom_min = run(exact=False)

    @pl.when(jnp.logical_not(denom_min >= SAFE_DENOM))
    def _():
        run(exact=True)


def _mla_attn(q, k, vt, ctx, *, nb, n):
    nsub = _att_nsub(n)
    tq = nsub * ATT_TQ
    npt = n // tq
    H, S = MLA_HEADS, MLA_SLOT
    in_specs = [
        pl.BlockSpec((H, tq, S), lambda b, j: (0, b * npt + j, 0)),
        pl.BlockSpec((H, n, S), lambda b, j: (0, b, 0)),
        pl.BlockSpec((H, VT_ROWS, n), lambda b, j: (0, 0, b)),
    ]
    args = [q, k, vt]
    n_ctx = 0
    if ctx is not None:
        n_ctx = ctx[0].shape[1] // nb
        in_specs += [
            pl.BlockSpec((H, n_ctx, S), lambda b, j: (0, b, 0)),
            pl.BlockSpec((H, VT_ROWS, n_ctx), lambda b, j: (0, 0, b)),
        ]
        args += list(ctx)
    return pl.pallas_call(
        functools.partial(_mla_attn_kernel, has_ctx=ctx is not None, nsub=nsub),
        grid=(nb, npt),
        in_specs=in_specs,
        out_specs=pl.BlockSpec((tq, 256), lambda b, j: (b * npt + j, 0)),
        out_shape=jax.ShapeDtypeStruct((nb * n, 256), F32),
        scratch_shapes=_att_scratch(n + n_ctx),
        compiler_params=_params("arbitrary", "arbitrary"),
        name="mla_attn_ctx" if ctx is not None else "mla_attn",
    )(*args)


def _diff_attn_kernel(*refs, has_ctx, nsub, lam_init):
    if has_ctx:
        lv_ref, g_ref, q_ref, k_ref, vt_ref, kc_ref, vtc_ref, o_ref, kn2, ot, e0, e1 = refs
        n_ctx = kc_ref.shape[0]
    else:
        lv_ref, g_ref, q_ref, k_ref, vt_ref, o_ref, kn2, ot, e0, e1 = refs
        n_ctx = 0
    e_bufs = (e0, e1)
    lv = lv_ref[...]
    lam = (jnp.exp(jnp.sum(lv[0:1] * lv[1:2], axis=-1, keepdims=True))
           - jnp.exp(jnp.sum(lv[2:3] * lv[3:4], axis=-1, keepdims=True)) + lam_init)
    lane128 = lax.broadcasted_iota(jnp.int32, (1, 128), 1)
    n_pairs = 2 * DIFF_HEADS

    @pl.when(pl.program_id(1) == 0)
    def _():
        dim = lax.broadcasted_iota(jnp.int32, (256, 128), 0)
        col = lax.broadcasted_iota(jnp.int32, (256, 128), 1)
        indicator = jnp.where(dim // DIFF_DIM == col, 1.0, 0.0).astype(BF16)
        kn2[0:1, :] = _max_row_norm2(k_ref[...], kc_ref[...] if has_ctx else None, indicator)

    def run(exact):
        denoms = []
        outs = {}

        def exp_stage(u):
            t, p = divmod(u, n_pairs)
            tile = slice((p * DIFF_DIM // 128) * 128, (p * DIFF_DIM // 128 + 1) * 128)
            k_new = lambda: k_ref[:, tile]
            k_ctx = (lambda: kc_ref[:, tile]) if has_ctx else None
            q = q_ref[t * ATT_TQ:(t + 1) * ATT_TQ, tile]
            lo = p * DIFF_DIM - tile.start
            in_pair = (lane128 >= lo) & (lane128 < lo + DIFF_DIM)
            qm = jnp.where(in_pair, q, jnp.zeros_like(q))
            shift = _exact_shift(k_new, k_ctx, qm) if exact else _bound_shift(qm, kn2[0:1, p:p + 1])
            _exp_stage(e_bufs[u % 2], k_new, k_ctx, qm, shift, n_ctx)

        def value_stage(u):
            t, p = divmod(u, n_pairs)
            hh = p // 2
            o = _value_stage(e_bufs[u % 2], lambda: vt_ref[hh],
                             (lambda: vtc_ref[hh]) if has_ctx else None, n_ctx)
            denom = o[HEAD_V:HEAD_V + 1, :]
            denoms.append(denom)
            outs[u] = (o[0:HEAD_V, :], denom)
            if p % 2 == 1:
                (o0, l0), (o1, l1) = outs.pop(u - 1), outs.pop(u)
                o = o0 * (1.0 / l0) - o1 * (lam / l1)
                msq = jnp.sum(o * o, axis=0, keepdims=True) * (1.0 / HEAD_V)
                ot[hh * HEAD_V:(hh + 1) * HEAD_V, :] = o * lax.rsqrt(msq + EPS)
            if p == n_pairs - 1:
                o_ref[t * ATT_TQ:(t + 1) * ATT_TQ, :] = (ot[...].T * g_ref[...]) * (1.0 - lam_init)

        _run_pipeline(nsub * n_pairs, exp_stage, value_stage)
        return jnp.min(functools.reduce(jnp.minimum, denoms))

    denom_min = run(exact=False)

    @pl.when(jnp.logical_not(denom_min >= SAFE_DENOM))
    def _():
        run(exact=True)


def _diff_attn(q, k, vt, ctx, lw, *, nb, n, lam_init):
    nsub = 1
    tq = nsub * ATT_TQ
    npt = n // tq
    in_specs = [
        lw.spec("diff_lambda"),
        lw.spec("diff_g"),
        pl.BlockSpec((tq, 256), lambda b, j: (b * npt + j, 0)),
        pl.BlockSpec((n, 256), lambda b, j: (b, 0)),
        pl.BlockSpec((DIFF_HEADS, VT_ROWS, n), lambda b, j: (0, 0, b)),
    ]
    args = [lw["diff_lambda"], lw["diff_g"], q, k, vt]
    n_ctx = 0
    if ctx is not None:
        n_ctx = ctx[0].shape[0] // nb
        in_specs += [pl.BlockSpec((n_ctx, 256), lambda b, j: (b, 0)),
                     pl.BlockSpec((DIFF_HEADS, VT_ROWS, n_ctx), lambda b, j: (0, 0, b))]
        args += list(ctx)
    return pl.pallas_call(
        functools.partial(_diff_attn_kernel, has_ctx=ctx is not None, nsub=nsub, lam_init=lam_init),
        grid=(nb, npt),
        in_specs=in_specs,
        out_specs=pl.BlockSpec((tq, 256), lambda b, j: (b * npt + j, 0)),
        out_shape=jax.ShapeDtypeStruct((nb * n, 256), F32),
        scratch_shapes=_att_scratch(n + n_ctx),
        compiler_params=_params("arbitrary", "arbitrary"),
        name="diff_attn_ctx" if ctx is not None else "diff_attn",
    )(*args)


def _shift_rows(v, k):
    return pltpu.roll(v, (-k) % v.shape[0], 0)


def _scan_strided(a_ref, b_ref, h_ref, row0, carry, n_rows, reverse):
    sub = lax.broadcasted_iota(jnp.int32, (8, 128), 0)
    span = 8 * SCAN_RUN
    order = tuple(range(SCAN_RUN))[::-1] if reverse else tuple(range(SCAN_RUN))
    starts = tuple(range(0, n_rows, span))[::-1] if reverse else tuple(range(0, n_rows, span))
    carries = []
    for lt in range(a_ref.shape[0]):
        c_in = carry[:, lt * 128:(lt + 1) * 128]
        for start in starts:
            tile = lambda ref, g: ref[lt, pl.ds(row0 + start + g, 8, stride=SCAN_RUN), :]
            a = [tile(a_ref, g) for g in range(SCAN_RUN)]
            b = [tile(b_ref, g) for g in range(SCAN_RUN)]
            h = {order[0]: b[order[0]]}
            p = {order[0]: a[order[0]]}
            for prev, g in zip(order, order[1:]):
                h[g] = a[g] * h[prev] + b[g]
                p[g] = a[g] * p[prev]
            pi, hi = p[order[-1]], h[order[-1]]
            for s in (1, 2, 4):
                shift = 8 - s if reverse else s
                valid = (sub < 8 - s) if reverse else (sub >= s)
                pr, hr = pltpu.roll(pi, shift, 0), pltpu.roll(hi, shift, 0)
                hi = jnp.where(valid, pi * hr + hi, hi)
                pi = jnp.where(valid, pi * pr, pi)
            one = 7 if reverse else 1
            first = (sub == 7) if reverse else (sub == 0)
            pe = jnp.where(first, 1.0, pltpu.roll(pi, one, 0))
            he = jnp.where(first, 0.0, pltpu.roll(hi, one, 0))
            c = pe * c_in + he
            for g in range(SCAN_RUN):
                h_ref[lt, pl.ds(start + g, 8, stride=SCAN_RUN), :] = h[g] + p[g] * c
            last = 0 if reverse else 7
            c_in = pi[last:last + 1, :] * c_in + hi[last:last + 1, :]
        carries.append(c_in)
    return jnp.concatenate(carries, axis=1)


def _sigmoid(x):
    return 0.5 * jnp.tanh(0.5 * x) + 0.5


def _gelu_tanh(x):
    return x * (0.5 * (1.0 + jnp.tanh(math.sqrt(2.0 / math.pi) * (x + 0.044715 * (x * x * x)))))


def _lru_kernel(u_ref, h0_ref, cw_ref, cb_ref, wg_ref, bg_ref, lam_ref, y_ref, st_ref,
                xpad, a1s, b1s, a0c, b0c, hc, *, N, T):
    W = LRU_WIDTH
    nc = N // T
    tiles = [slice(lt * 128, (lt + 1) * 128) for lt in range(W // 128)]
    zeros = jnp.zeros((HALO, W), F32)
    xpad[0:HALO, :] = zeros
    xpad[N + HALO:N + 2 * HALO, :] = zeros

    def fill(j, carry):
        r0 = pl.multiple_of(j * T, T)
        xpad[pl.ds(r0 + HALO, T), :] = u_ref[pl.ds(r0, T), 0:W]
        return carry

    lax.fori_loop(0, nc, fill, 0)

    z = -lam_ref[...]
    sp = jnp.maximum(z, 0.0) + jnp.log1p(jnp.exp(-jnp.abs(z)))
    cw = cw_ref[...]
    cb = cb_ref[...]
    bg = bg_ref[...]

    def fwd(j, carry):
        r0 = pl.multiple_of(j * T, T)
        ext = xpad[pl.ds(r0, T + 2 * HALO), :]
        body = slice(HALO, HALO + T)
        xc = cb
        for tap in range(4):
            xc = xc + _shift_rows(ext, tap - 1)[body] * cw[tap:tap + 1]
        g = _sigmoid(_dot(xc.astype(BF16), wg_ref[...]) + bg)
        ab = []
        for d in range(2):
            r = g[:, d * W:(d + 1) * W]
            i = g[:, (2 + d) * W:(3 + d) * W]
            log_a = (-LRU_C * r) * sp[d:d + 1]
            a = jnp.exp(log_a)
            bt = (jnp.sqrt(1.0 - a * a) * i) * xc
            ab.append((a, bt))
        for lt, lanes in enumerate(tiles):
            a0c[lt] = ab[0][0][:, lanes]
            b0c[lt] = ab[0][1][:, lanes]
            a1s[lt, pl.ds(r0, T), :] = ab[1][0][:, lanes]
            b1s[lt, pl.ds(r0, T), :] = ab[1][1][:, lanes]
        carry = _scan_strided(a0c, b0c, hc, 0, carry, T, reverse=False)
        for lt, lanes in enumerate(tiles):
            y_ref[pl.ds(r0, T), lanes] = hc[lt]
        return carry

    cf = lax.fori_loop(0, nc, fwd, h0_ref[0, 0:1, :])

    def bwd(jj, carry):
        r0 = pl.multiple_of((nc - 1 - jj) * T, T)
        carry = _scan_strided(a1s, b1s, hc, r0, carry, T, reverse=True)
        for lt, lanes in enumerate(tiles):
            gb = u_ref[pl.ds(r0, T), W + lt * 128:W + (lt + 1) * 128]
            y_ref[pl.ds(r0, T), lanes] = (y_ref[pl.ds(r0, T), lanes] + hc[lt]) * _gelu_tanh(gb)
        return carry

    cbw = lax.fori_loop(0, nc, bwd, h0_ref[0, 1:2, :])
    st_ref[0, 0:1, :] = cf
    st_ref[0, 1:2, :] = cbw


def _lru(u, h0, h0_block, lw, *, nb, n):
    T = min(n, 256)
    W = LRU_WIDTH
    return pl.pallas_call(
        functools.partial(_lru_kernel, N=n, T=T),
        grid=(nb,),
        in_specs=[
            pl.BlockSpec((n, 2 * W), lambda b: (b, 0)),
            pl.BlockSpec((1, 2, W), lambda b: (h0_block(b), 0, 0)),
            lw.spec("conv_w"), lw.spec("conv_b"), lw.spec("w_gate"), lw.spec("b_gate"),
            lw.spec("lru_lambda"),
        ],
        out_specs=[
            pl.BlockSpec((n, W), lambda b: (b, 0)),
            pl.BlockSpec((1, 2, W), lambda b: (b, 0, 0)),
        ],
        out_shape=[
            jax.ShapeDtypeStruct((nb * n, W), F32),
            jax.ShapeDtypeStruct((nb, 2, W), F32),
        ],
        scratch_shapes=[
            pltpu.VMEM((n + 2 * HALO, W), F32),
            pltpu.VMEM((W // 128, n, 128), F32),
            pltpu.VMEM((W // 128, n, 128), F32),
            pltpu.VMEM((W // 128, T, 128), F32),
            pltpu.VMEM((W // 128, T, 128), F32),
            pltpu.VMEM((W // 128, T, 128), F32),
        ],
        compiler_params=_params("arbitrary"),
        name="rglru",
    )(u, h0, lw["conv_w"], lw["conv_b"], lw["w_gate"], lw["b_gate"], lw["lru_lambda"])


def _pool_kernel(u_ref, wp_ref, sc_ref, y_ref, xpad, *, N, T):
    W = GROUP_WIDTH
    nc = N // T
    zeros = jnp.zeros((HALO, W), F32)
    xpad[0:HALO, :] = zeros
    xpad[N + HALO:N + 2 * HALO, :] = zeros

    def fill(j, carry):
        r0 = pl.multiple_of(j * T, T)
        xpad[pl.ds(r0 + HALO, T), :] = u_ref[pl.ds(r0, T), :]
        return carry

    lax.fori_loop(0, nc, fill, 0)

    grp = lax.broadcasted_iota(jnp.int32, (1, W), 1) // POOL_CH
    half = jnp.where(grp == 0, 1, jnp.where(grp == 1, 2, jnp.where(grp == 2, 4, 8)))
    scale = sc_ref[...]

    def chunk(j, carry):
        r0 = pl.multiple_of(j * T, T)
        ext = xpad[pl.ds(r0, T + 2 * HALO), :]
        w2 = _shift_rows(ext, -1) + ext
        w4 = _shift_rows(w2, -1) + _shift_rows(w2, 1)
        w8 = _shift_rows(w4, -2) + _shift_rows(w4, 2)
        w16 = _shift_rows(w8, -4) + _shift_rows(w8, 4)
        ws = jnp.where(grp == 0, w2, jnp.where(grp == 1, w4, jnp.where(grp == 2, w8, w16)))
        body = slice(HALO, HALO + T)
        t = r0 + lax.broadcasted_iota(jnp.int32, (T, W), 0)
        cnt = (jnp.minimum(t + half, N) - jnp.maximum(t - half, 0)).astype(F32)
        d = ws[body] / cnt - ext[body]
        y_ref[pl.ds(r0, T), :] = _dot(d.astype(BF16), wp_ref[...]) * scale
        return carry

    lax.fori_loop(0, nc, chunk, 0)


def _pool(u, lw, *, nb, n):
    W = GROUP_WIDTH
    T = min(n, 256)
    return pl.pallas_call(
        functools.partial(_pool_kernel, N=n, T=T),
        grid=(nb,),
        in_specs=[pl.BlockSpec((n, W), lambda b: (b, 0)), lw.spec("w_pool"), lw.spec("pool_scale")],
        out_specs=pl.BlockSpec((n, W), lambda b: (b, 0)),
        out_shape=jax.ShapeDtypeStruct((nb * n, W), F32),
        scratch_shapes=[pltpu.VMEM((n + 2 * HALO, W), F32)],
        compiler_params=_params("arbitrary"),
        name="pool_mixer",
    )(u, lw["w_pool"], lw["pool_scale"])


def _mix_ffn_kernel(*refs, final):
    if final:
        (x_ref, ya_ref, yb_ref, yc_ref, yd_ref, mod_ref, g2_ref, wo_ref, wg_ref, wu_ref, wd_ref,
         gf_ref, o_ref) = refs
    else:
        (x_ref, ya_ref, yb_ref, yc_ref, yd_ref, mod_ref, g2_ref, wo_ref, wg_ref, wu_ref, wd_ref,
         o_ref) = refs
    mod = mod_ref[0]
    gate1 = mod[:, 2 * D_MODEL:3 * D_MODEL]
    sh2 = mod[:, 3 * D_MODEL:4 * D_MODEL]
    sc2 = mod[:, 4 * D_MODEL:5 * D_MODEL]
    gate2 = mod[:, 5 * D_MODEL:6 * D_MODEL]
    mix = None
    for i, y_ref in enumerate((ya_ref, yb_ref, yc_ref, yd_ref)):
        part = _dot(y_ref[...].astype(BF16), wo_ref[i * GROUP_WIDTH:(i + 1) * GROUP_WIDTH, :])
        mix = part if mix is None else mix + part
    x1 = x_ref[...] + gate1 * mix
    h = _rms_rows(x1, D_MODEL) * g2_ref[...]
    hb = (h * (1.0 + sc2) + sh2).astype(BF16)
    ff = None
    for lo, hi in FF_CHUNKS:
        g = _dot(hb, wg_ref[:, lo:hi])
        up = _dot(hb, wu_ref[:, lo:hi])
        act = ((g * jax.nn.sigmoid(g)) * up).astype(BF16)
        part = _dot(act, wd_ref[lo:hi, :])
        ff = part if ff is None else ff + part
    x2 = x1 + gate2 * ff
    if final:
        x2 = _rms_rows(x2, D_MODEL) * gf_ref[...]
    o_ref[...] = x2


def _mix_ffn(x, ys, mod, lw, gf, *, nb, n, final):
    T = nb * n
    tm = TOKEN_TILE
    npt = n // tm

    def tok(width):
        return pl.BlockSpec((tm, width), lambda i: (i, 0))

    wnames = ("g2", "w_out", "w_gate_ff", "w_up_ff", "w_down")
    in_specs = [tok(D_MODEL), tok(256), tok(256), tok(256), tok(256),
                mod.spec(lambda i: i // npt)] + [lw.spec(nm) for nm in wnames]
    args = [x, *ys, mod.table] + [lw[nm] for nm in wnames]
    if final:
        in_specs.append(_resident((1, D_MODEL)))
        args.append(gf)
    return pl.pallas_call(
        functools.partial(_mix_ffn_kernel, final=final),
        grid=(T // tm,),
        in_specs=in_specs,
        out_specs=tok(D_MODEL),
        out_shape=jax.ShapeDtypeStruct((T, D_MODEL), F32),
        compiler_params=_params("arbitrary"),
        name="mix_ffn_final" if final else "mix_ffn",
    )(*args)


def _block_diag(w):
    L, G, c, e = w.shape
    return jnp.einsum('lgce,gh->lgche', w, jnp.eye(G, dtype=w.dtype)).reshape(L, G * c, G * e)


def _rot_cols(w):
    return jnp.concatenate([-w[..., 16:32], w[..., 0:16]], axis=-1)


def _stack_weights(p):
    w_in = p["w_in"]
    o1 = MLA_Q_RANK
    o2 = o1 + MLA_KV_RANK
    o3 = o2 + MLA_ROPE
    c_q, c_kv, k_r, rest = w_in[..., :o1], w_in[..., o1:o2], w_in[..., o2:o3], w_in[..., o3:]
    z = lambda n: jnp.zeros((DEPTH, D_MODEL, n), F32)
    w_in_eff = jnp.concatenate([c_q, k_r, z(32), c_kv, z(64), _rot_cols(k_r), z(32), rest], axis=-1)

    w_uq = p["mla_w_uq"]
    qd = MLA_NOPE + MLA_ROPE
    wq_parts, wqr_parts = [], []
    zq = lambda n: jnp.zeros((DEPTH, MLA_Q_RANK, n), F32)
    for h in range(MLA_HEADS):
        wh = w_uq[..., h * qd:(h + 1) * qd]
        wq_parts += [wh, zq(MLA_SLOT - qd)]
        wqr_parts += [zq(MLA_NOPE), _rot_cols(wh[..., MLA_NOPE:]), zq(MLA_SLOT - qd)]
    pad_rows = lambda w: jnp.pad(w, ((0, 0), (0, 256 - MLA_Q_RANK), (0, 0)))
    w_ukv = p["mla_w_ukv"]
    wk_parts, wv_parts = [], []
    zk = jnp.zeros((DEPTH, MLA_KV_RANK, MLA_SLOT - MLA_NOPE), F32)
    for h in range(MLA_HEADS):
        base = h * (MLA_NOPE + MLA_V)
        wk_parts += [w_ukv[..., base:base + MLA_NOPE], zk]
        wv_parts.append(w_ukv[..., base + MLA_NOPE:base + MLA_NOPE + MLA_V])

    w_r, w_i, b_r, b_i = p["lru_w_r"], p["lru_w_i"], p["lru_b_r"], p["lru_b_i"]
    w_gate = jnp.concatenate([_block_diag(w_r[:, 0]), _block_diag(w_r[:, 1]),
                              _block_diag(w_i[:, 0]), _block_diag(w_i[:, 1])], axis=-1)
    b_gate = jnp.concatenate([b_r[:, 0], b_r[:, 1], b_i[:, 0], b_i[:, 1]], axis=-1)
    w_gu = p["w_gu"]
    row = lambda v: v[:, None, :]
    return {
        "g1": row(p["norm1_g"]),
        "g2": row(p["norm2_g"]),
        "w_in": w_in_eff.astype(BF16),
        "gq": row(jnp.pad(p["mla_q_norm_g"], ((0, 0), (0, 256 - MLA_Q_RANK)))),
        "gkv": row(p["mla_kv_norm_g"]),
        "wq": pad_rows(jnp.concatenate(wq_parts, axis=-1)).astype(BF16),
        "wqr": pad_rows(jnp.concatenate(wqr_parts, axis=-1)).astype(BF16),
        "wk": jnp.concatenate(wk_parts, axis=-1).astype(BF16),
        "wv": jnp.concatenate(wv_parts, axis=-1).astype(BF16),
        "conv_w": p["lru_conv_w"],
        "conv_b": row(p["lru_conv_b"]),
        "w_gate": w_gate.astype(BF16),
        "b_gate": row(b_gate),
        "lru_lambda": p["lru_lambda"],
        "w_pool": _block_diag(p["pool_w"]).astype(BF16),
        "pool_scale": row(p["pool_scale"]),
        "diff_lambda": p["diff_lambda"],
        "diff_g": row(jnp.tile(p["diff_norm_g"], (1, DIFF_HEADS))),
        "w_out": p["w_out"].astype(BF16),
        "w_gate_ff": w_gu[..., :FF_HIDDEN].astype(BF16),
        "w_up_ff": w_gu[..., FF_HIDDEN:].astype(BF16),
        "w_down": p["w_down"].astype(BF16),
    }


def _rope_tables(n, positional):
    quarter = MLA_ROPE // 4
    if positional:
        t = jnp.arange(n)
        row = (t // GRID_W).astype(F32)
        col = (t % GRID_W).astype(F32)
        inv = ROPE_BASE ** (-jnp.arange(quarter, dtype=F32) / quarter)
        ang = jnp.concatenate([row[:, None] * inv, col[:, None] * inv], axis=-1)
        cos, sin = jnp.cos(ang), jnp.sin(ang)
    else:
        cos, sin = jnp.ones((n, 16), F32), jnp.zeros((n, 16), F32)
    one = lambda w: jnp.ones((n, w), F32)
    zero = lambda w: jnp.zeros((n, w), F32)
    scale = LOG2E / math.sqrt(MLA_NOPE + MLA_ROPE)
    return {
        "cosq": jnp.concatenate([one(64), cos, cos, one(32)], axis=1) * scale,
        "sinq": jnp.concatenate([zero(64), sin, sin, zero(32)], axis=1) * scale,
        "cosk": jnp.concatenate([zero(64), cos, cos, zero(32)], axis=1),
        "sink": jnp.concatenate([zero(64), sin, sin, zero(32)], axis=1),
        "cosd": jnp.tile(jnp.concatenate([cos, cos], axis=1), (1, 8)),
        "sina": jnp.tile(jnp.concatenate([-sin, zero(16)], axis=1), (1, 8)),
        "sinb": jnp.tile(jnp.concatenate([zero(16), sin], axis=1), (1, 8)),
    }


def _layer(x, mod, lw, tabs, layer_idx, ctx, gf, *, nb, n, final):
    emit_cache = ctx is None
    tok_nb, tok_n = (1, nb * n) if mod.shared else (nb, n)
    outs = _inproj(x, mod, lw, tabs, nb=tok_nb, n=tok_n, emit_cache=emit_cache)
    q, k, vt, u_lru, u_pool, dq, dk, dvt = outs[:8]
    lam_init = 0.8 - 0.6 * math.exp(-0.3 * layer_idx)
    if ctx is None:
        h0 = jnp.zeros((1, 2, LRU_WIDTH), F32)
        h0_block = lambda b: 0
        mla_ctx = diff_ctx = None
    else:
        ckv, kr_pad, cdk, cdv, h0 = ctx
        p = ckv.shape[0] // (nb * DEPTH)
        h0_block = lambda b: b * DEPTH + layer_idx
        kc, vtc, dkc, dvtc = _ctx_prep(ckv, kr_pad, cdk, cdv, lw, nb=nb, p=p)
        mla_ctx = (kc, vtc)
        diff_ctx = (dkc, dvtc)
    y_mla = _mla_attn(q, k, vt, mla_ctx, nb=nb, n=n)
    y_lru, st = _lru(u_lru, h0, h0_block, lw, nb=nb, n=n)
    y_pool = _pool(u_pool, lw, nb=nb, n=n)
    y_diff = _diff_attn(dq, dk, dvt, diff_ctx, lw, nb=nb, n=n, lam_init=lam_init)
    x2 = _mix_ffn(x, (y_mla, y_lru, y_pool, y_diff), mod, lw, gf, nb=tok_nb, n=tok_n, final=final)
    cache = (outs[8], outs[9][:, 64:96], outs[10], outs[11], st) if emit_cache else None
    return x2, cache


def kernel(x_prompt, x_sample, cache_mla_ckv, cache_mla_krope, cache_diff_k, cache_diff_v, state_lru,
           c, c_ctx, w_ada, b_ada, norm1_g, norm2_g, w_in, mla_q_norm_g, mla_w_uq, mla_kv_norm_g,
           mla_w_ukv, lru_conv_w, lru_conv_b, lru_w_r, lru_b_r, lru_w_i, lru_b_i, lru_lambda, pool_w,
           pool_scale, diff_lambda, diff_norm_g, w_out, w_gu, w_down, final_norm_g):
    p = {
        "norm1_g": norm1_g, "norm2_g": norm2_g, "w_in": w_in, "mla_q_norm_g": mla_q_norm_g,
        "mla_w_uq": mla_w_uq, "mla_kv_norm_g": mla_kv_norm_g, "mla_w_ukv": mla_w_ukv,
        "lru_conv_w": lru_conv_w, "lru_conv_b": lru_conv_b, "lru_w_r": lru_w_r, "lru_b_r": lru_b_r,
        "lru_w_i": lru_w_i, "lru_b_i": lru_b_i, "lru_lambda": lru_lambda, "pool_w": pool_w,
        "pool_scale": pool_scale, "diff_lambda": diff_lambda, "diff_norm_g": diff_norm_g,
        "w_out": w_out, "w_gu": w_gu, "w_down": w_down,
    }
    Bp, Np, _ = x_prompt.shape
    Bs, Ns, _ = x_sample.shape
    P = cache_mla_ckv.shape[2]

    cond_all = jnp.concatenate([c, c_ctx[None, :], jnp.zeros((MOD_ROWS - Bs - 1, D_MODEL), F32)], axis=0)
    mod_table = _ada(cond_all, w_ada, b_ada).reshape(DEPTH * MOD_ROWS, 1, 6 * D_MODEL)
    tabs_p = _rope_tables(Bp * Np, positional=False)
    tabs_s = _rope_tables(Ns, positional=True)
    kr_pad = jnp.pad(cache_mla_krope, ((0, 0), (0, 0), (0, 0), (MLA_NOPE, MLA_SLOT - MLA_NOPE - MLA_ROPE)))
    flat = lambda a, w: a.reshape(Bs * DEPTH * P, w)
    ctx = (flat(cache_mla_ckv, MLA_KV_RANK), flat(kr_pad, MLA_SLOT), flat(cache_diff_k, 256),
           flat(cache_diff_v, 256), state_lru.reshape(Bs * DEPTH, 2, LRU_WIDTH))
    gf = final_norm_g[None, :]
    stacked = _stack_weights(p)

    xp = x_prompt.reshape(Bp * Np, D_MODEL)
    xs = x_sample.reshape(Bs * Ns, D_MODEL)
    caches = []
    for l in range(DEPTH):
        lw = _LayerWeights(stacked, l)
        final = l == DEPTH - 1
        mod_p = _Mod(mod_table, l * MOD_ROWS + Bs, shared=True)
        mod_s = _Mod(mod_table, l * MOD_ROWS, shared=False)
        xp, cache = _layer(xp, mod_p, lw, tabs_p, l, None, gf, nb=Bp, n=Np, final=final)
        caches.append(cache)
        xs, _ = _layer(xs, mod_s, lw, tabs_s, l, ctx, gf, nb=Bs, n=Ns, final=final)

    stack = lambda i, w: jnp.stack([cc[i].reshape(Bp, Np, w) for cc in caches], axis=1)
    new_mla_ckv = stack(0, MLA_KV_RANK)
    new_mla_krope = stack(1, MLA_ROPE)
    new_diff_k = stack(2, 256).reshape(Bp, DEPTH, Np, DIFF_HEADS, 2, DIFF_DIM)
    new_diff_v = stack(3, 256).reshape(Bp, DEPTH, Np, DIFF_HEADS, 2 * DIFF_DIM)
    new_state_lru = jnp.stack([cc[4] for cc in caches], axis=1)
    return (xp.reshape(Bp, Np, D_MODEL), xs.reshape(Bs, Ns, D_MODEL),
            new_mla_ckv, new_mla_krope, new_diff_k, new_diff_v, new_state_lru)
```

```python
import functools
import math

import jax
import jax.numpy as jnp
from jax import lax
from jax.experimental import pallas as pl
from jax.experimental.pallas import tpu as pltpu

F32 = jnp.float32
BF16 = jnp.bfloat16

D_MODEL = 1024
DEPTH = 2
GRID_W = 64
GROUP_WIDTH = 256
MLA_HEADS = 4
MLA_NOPE = 64
MLA_ROPE = 32
MLA_V = 64
MLA_Q_RANK = 192
MLA_KV_RANK = 128
MLA_SLOT = 128
LRU_WIDTH = 256
LRU_C = 8.0
POOL_WINDOWS = (2, 4, 8, 16)
POOL_CH = 64
DIFF_HEADS = 4
DIFF_DIM = 32
HEAD_V = 64
FF_HIDDEN = 2816
FF_CHUNKS = ((0, 1536), (1536, 2816))
ROPE_BASE = 10000.0
EPS = 1e-6
IN_EFF = 2048
HALO = 8
SCAN_RUN = 4
VT_ROWS = 80
ATT_TQ = 256
TOKEN_TILE = 512
TAB_WIDTH = 4 * 128 + 3 * 256
MOD_ROWS = 16
LOG2E = math.log2(math.e)

VMEM_LIMIT_BYTES = 56 * 1024 * 1024

_NT = (((1,), (1,)), ((), ()))


def _params(*sem):
    return pltpu.CompilerParams(dimension_semantics=sem, vmem_limit_bytes=VMEM_LIMIT_BYTES)


def _resident(shape):
    zeros = (0,) * len(shape)
    return pl.BlockSpec(shape, lambda *_: zeros, pipeline_mode=pl.Buffered(1))


def _dot(a, b):
    return jnp.dot(a, b, preferred_element_type=F32)


def _dot_nt(a, b):
    return lax.dot_general(a, b, _NT, preferred_element_type=F32)


def _rms_rows(x, width):
    ms = jnp.sum(x * x, axis=-1, keepdims=True) * (1.0 / width)
    return x * lax.rsqrt(ms + EPS)


def _store_vt(vt_ref, v):
    vt = v.T
    rows = v.shape[0]
    pad = VT_ROWS - HEAD_V
    ones_row = jnp.where(lax.broadcasted_iota(jnp.int32, (pad, rows), 0) == 0, 1.0, 0.0).astype(BF16)
    for hh in range(vt_ref.shape[0]):
        vt_ref[hh, 0:HEAD_V, :] = vt[hh * HEAD_V:(hh + 1) * HEAD_V, :].astype(BF16)
        vt_ref[hh, HEAD_V:VT_ROWS, :] = ones_row


class _Mod:
    def __init__(self, table, row0, shared):
        self.table, self.row0, self.shared = table, row0, shared

    def spec(self, batch_of):
        row0 = self.row0
        if self.shared:
            return pl.BlockSpec((1, 1, 6 * D_MODEL), lambda *g: (row0, 0, 0))
        return pl.BlockSpec((1, 1, 6 * D_MODEL), lambda *g: (row0 + batch_of(*g), 0, 0))


class _LayerWeights:
    def __init__(self, stacked, layer):
        self.stacked, self.layer = stacked, layer

    def __getitem__(self, name):
        return self.stacked[name]

    def spec(self, name):
        layer = self.layer
        _, rows, cols = self.stacked[name].shape
        return pl.BlockSpec((None, rows, cols), lambda *_: (layer, 0, 0), pipeline_mode=pl.Buffered(1))


def _ada_kernel(cond_ref, w_ref, b_ref, out_ref):
    c = cond_ref[...]
    s = c * jax.nn.sigmoid(c)
    out_ref[0] = _dot(s.astype(BF16), w_ref[0].astype(BF16)) + b_ref[0]


def _ada(cond_all, w_ada, b_ada):
    rows = cond_all.shape[0]
    tn = 1536
    return pl.pallas_call(
        _ada_kernel,
        grid=(DEPTH, 6 * D_MODEL // tn),
        in_specs=[
            pl.BlockSpec((rows, D_MODEL), lambda l, j: (0, 0)),
            pl.BlockSpec((1, D_MODEL, tn), lambda l, j: (l, 0, j)),
            pl.BlockSpec((1, 1, tn), lambda l, j: (l, 0, j)),
        ],
        out_specs=pl.BlockSpec((1, rows, tn), lambda l, j: (l, 0, j)),
        out_shape=jax.ShapeDtypeStruct((DEPTH, rows, 6 * D_MODEL), F32),
        compiler_params=_params("arbitrary", "arbitrary"),
        name="ada_mod",
    )(cond_all, w_ada, b_ada.reshape(DEPTH, 1, 6 * D_MODEL))


def _inproj_kernel(x_ref, mod_ref, g1_ref, win_ref, gq_ref, gkv_ref, wq_ref, wqr_ref, wkv_ref, tab_ref,
                   q_out, k_out, vt_out, lru_out, pool_out, dq_out, dk_out, dvt_out, *cache_outs):
    cosq_ref, sinq_ref, cosk_ref, sink_ref = (tab_ref.at[:, i * 128:(i + 1) * 128] for i in range(4))
    cosd_ref, sina_ref, sinb_ref = (tab_ref.at[:, 512 + i * 256:768 + i * 256] for i in range(3))
    x = x_ref[...]
    mod = mod_ref[0]
    sh1 = mod[:, 0:D_MODEL]
    sc1 = mod[:, D_MODEL:2 * D_MODEL]
    h = _rms_rows(x, D_MODEL) * g1_ref[...]
    hb = (h * (1.0 + sc1) + sh1).astype(BF16)

    u_mla = _dot(hb, win_ref[:, 0:512])
    u_pd = _dot(hb, win_ref[:, 1024:1536])
    u_kv = _dot(hb, win_ref[:, 1536:2048])
    t01 = u_mla[:, 0:256]
    lane = lax.broadcasted_iota(jnp.int32, (1, 256), 1)
    cq = jnp.where(lane < MLA_Q_RANK, t01, 0.0)
    cqn = (_rms_rows(cq, MLA_Q_RANK) * gq_ref[...]).astype(BF16)
    qa = _dot(cqn, wq_ref[...])
    qr = _dot(cqn, wqr_ref[...])
    cosq = cosq_ref[...]
    sinq = sinq_ref[...]
    ckv = u_mla[:, 256:384]
    lat = _rms_rows(ckv, MLA_KV_RANK) * gkv_ref[...]
    latb = lat.astype(BF16)
    kkv = _dot(latb, wkv_ref[...])
    kk = kkv[:, 0:MLA_HEADS * MLA_SLOT]
    _store_vt(vt_out, kkv[:, MLA_HEADS * MLA_SLOT:])
    t1 = t01[:, 128:256]
    t3 = u_mla[:, 384:512]
    kro = t1 * cosk_ref[...] + t3 * sink_ref[...]
    for hh in range(MLA_HEADS):
        sl = slice(hh * MLA_SLOT, (hh + 1) * MLA_SLOT)
        q_out[hh] = (qa[:, sl] * cosq + qr[:, sl] * sinq).astype(q_out.dtype)
        k_out[hh] = (kk[:, sl] + kro).astype(k_out.dtype)

    lru_out[...] = _dot(hb, win_ref[:, 512:1024])
    pool_out[...] = u_pd[:, 0:256]

    cosd = cosd_ref[...]
    sina = sina_ref[...]
    sinb = sinb_ref[...]

    def rope(t):
        return t * cosd + pltpu.roll(t, 256 - 16, 1) * sina + pltpu.roll(t, 16, 1) * sinb

    dq = u_pd[:, 256:512]
    dk = u_kv[:, 0:256]
    dv = u_kv[:, 256:512]
    dq_out[...] = (rope(dq) * (LOG2E / math.sqrt(DIFF_DIM))).astype(dq_out.dtype)
    dk_out[...] = rope(dk).astype(dk_out.dtype)
    _store_vt(dvt_out, dv)

    if cache_outs:
        lat_out, kr_out, dk_raw_out, dv_raw_out = cache_outs
        lat_out[...] = lat
        kr_out[...] = t1
        dk_raw_out[...] = dk
        dv_raw_out[...] = dv


def _inproj(x, mod, lw, tabs, *, nb, n, emit_cache):
    T = nb * n
    tm = TOKEN_TILE
    npt = n // tm
    row_blk = lambda j, b: b * npt + j

    def tok(width):
        return pl.BlockSpec((tm, width), lambda j, b: (row_blk(j, b), 0))

    def tab(width):
        return pl.BlockSpec((tm, width), lambda j, b: (j, 0))

    head = pl.BlockSpec((MLA_HEADS, tm, MLA_SLOT), lambda j, b: (0, row_blk(j, b), 0))
    vt_spec = pl.BlockSpec((MLA_HEADS, VT_ROWS, tm), lambda j, b: (0, 0, row_blk(j, b)))
    wnames = ("g1", "w_in", "gq", "gkv", "wq", "wqr", "wkv")
    in_specs = [tok(D_MODEL), mod.spec(lambda j, b: b)] + [lw.spec(nm) for nm in wnames] + [
        tab(TAB_WIDTH)]
    out_specs = [head, head, vt_spec, tok(512), tok(256), tok(256), tok(256), vt_spec]
    vt_shape = jax.ShapeDtypeStruct((MLA_HEADS, VT_ROWS, T), BF16)
    out_shape = [
        jax.ShapeDtypeStruct((MLA_HEADS, T, MLA_SLOT), BF16),
        jax.ShapeDtypeStruct((MLA_HEADS, T, MLA_SLOT), BF16),
        vt_shape,
        jax.ShapeDtypeStruct((T, 512), F32),
        jax.ShapeDtypeStruct((T, 256), F32),
        jax.ShapeDtypeStruct((T, 256), BF16),
        jax.ShapeDtypeStruct((T, 256), BF16),
        vt_shape,
    ]
    if emit_cache:
        out_specs += [tok(128), tok(128), tok(256), tok(256)]
        out_shape += [jax.ShapeDtypeStruct((T, 128), F32), jax.ShapeDtypeStruct((T, 128), F32),
                      jax.ShapeDtypeStruct((T, 256), F32), jax.ShapeDtypeStruct((T, 256), F32)]
    return pl.pallas_call(
        _inproj_kernel,
        grid=(npt, nb),
        in_specs=in_specs,
        out_specs=out_specs,
        out_shape=out_shape,
        compiler_params=_params("arbitrary", "arbitrary"),
        name="inproj_cache" if emit_cache else "inproj",
    )(x, mod.table, *[lw[nm] for nm in wnames], tabs)


def _ctx_prep_kernel(ckv_ref, kr_ref, dk_ref, dv_ref, wkv_ref, k_out, vt_out, dk_out, dvt_out):
    latb = ckv_ref[...].astype(BF16)
    kkv = _dot(latb, wkv_ref[...])
    kk = kkv[:, 0:MLA_HEADS * MLA_SLOT]
    kr = kr_ref[...]
    for hh in range(MLA_HEADS):
        k_out[hh] = (kk[:, hh * MLA_SLOT:(hh + 1) * MLA_SLOT] + kr).astype(k_out.dtype)
    _store_vt(vt_out, kkv[:, MLA_HEADS * MLA_SLOT:])
    dk_out[...] = dk_ref[...].astype(dk_out.dtype)
    _store_vt(dvt_out, dv_ref[...])


def _ctx_prep(ckv, kr_pad, cdk, cdv, lw, *, nb, p):
    T = nb * p
    layer = lw.layer
    cache_row = lambda w: pl.BlockSpec((p, w), lambda b: (b * DEPTH + layer, 0))
    row = lambda w: pl.BlockSpec((p, w), lambda b: (b, 0))
    vt_spec = pl.BlockSpec((MLA_HEADS, VT_ROWS, p), lambda b: (0, 0, b))
    vt_shape = jax.ShapeDtypeStruct((MLA_HEADS, VT_ROWS, T), BF16)
    return pl.pallas_call(
        _ctx_prep_kernel,
        grid=(nb,),
        in_specs=[cache_row(128), cache_row(128), cache_row(256), cache_row(256),
                  lw.spec("wkv")],
        out_specs=[pl.BlockSpec((MLA_HEADS, p, MLA_SLOT), lambda b: (0, b, 0)), vt_spec, row(256), vt_spec],
        out_shape=[jax.ShapeDtypeStruct((MLA_HEADS, T, MLA_SLOT), BF16), vt_shape,
                   jax.ShapeDtypeStruct((T, 256), BF16), vt_shape],
        compiler_params=_params("arbitrary"),
        name="ctx_prep",
    )(ckv, kr_pad, cdk, cdv, lw["wkv"])


SAFE_DENOM = 2.0 ** -60
BOUND_SLACK = 1.02


def _scores(k_new, k_ctx, q):
    sn = _dot_nt(k_new(), q)
    sc = _dot_nt(k_ctx(), q) if k_ctx is not None else None
    return sn, sc


def _exact_shift(k_new, k_ctx, q):
    sn, sc = _scores(k_new, k_ctx, q)
    m = jnp.max(sn, axis=0, keepdims=True)
    if sc is not None:
        m = jnp.maximum(m, jnp.max(sc, axis=0, keepdims=True))
    return m


def _bound_shift(q, key_norm2):
    qf = q.astype(F32)
    ones = jnp.ones((8, q.shape[1]), BF16)
    q_norm2 = _dot_nt(ones, (qf * qf).astype(BF16))[0:1, :]
    return jnp.sqrt(q_norm2 * key_norm2) * BOUND_SLACK


def _max_row_norm2(k_new, k_ctx, col_sum):
    def one(k):
        kf = k.astype(F32)
        return jnp.max(_dot((kf * kf).astype(BF16), col_sum), axis=0, keepdims=True)
    m = one(k_new)
    if k_ctx is not None:
        m = jnp.maximum(m, one(k_ctx))
    return m * BOUND_SLACK


def _exp_stage(e_buf, k_new, k_ctx, q, shift, n_ctx):
    sn, sc = _scores(k_new, k_ctx, q)
    e_buf[n_ctx:, :] = jnp.exp2(sn - shift).astype(BF16)
    if sc is not None:
        e_buf[0:n_ctx, :] = jnp.exp2(sc - shift).astype(BF16)


def _value_stage(e_buf, vt_new, vt_ctx, n_ctx):
    o = _dot(vt_new(), e_buf[n_ctx:, :])
    if vt_ctx is not None:
        o = o + _dot(vt_ctx(), e_buf[0:n_ctx, :])
    return o


def _run_pipeline(n_maps, exp_stage, value_stage):
    exp_stage(0)
    for u in range(n_maps):
        if u + 1 < n_maps:
            exp_stage(u + 1)
        value_stage(u)


def _att_scratch(nk):
    return [pltpu.VMEM((8, 128), F32),
            pltpu.VMEM((MLA_HEADS * HEAD_V, ATT_TQ), F32),
            pltpu.VMEM((nk, ATT_TQ), BF16), pltpu.VMEM((nk, ATT_TQ), BF16)]


def _att_nsub(n):
    return 2 if n % (2 * ATT_TQ) == 0 else 1


def _mla_attn_kernel(*refs, has_ctx, nsub):
    if has_ctx:
        q_ref, k_ref, vt_ref, kc_ref, vtc_ref, o_ref, kn2, ot, e0, e1 = refs
        n_ctx = kc_ref.shape[1]
    else:
        q_ref, k_ref, vt_ref, o_ref, kn2, ot, e0, e1 = refs
        n_ctx = 0
    e_bufs = (e0, e1)

    @pl.when(pl.program_id(1) == 0)
    def _():
        ones = jnp.ones((MLA_SLOT, 128), BF16)
        for hh in range(MLA_HEADS):
            kn2[hh:hh + 1, :] = _max_row_norm2(k_ref[hh], kc_ref[hh] if has_ctx else None, ones)

    def run(exact):
        denoms = []

        def exp_stage(u):
            t, hh = divmod(u, MLA_HEADS)
            q = q_ref[hh, t * ATT_TQ:(t + 1) * ATT_TQ, :]
            k_new = lambda: k_ref[hh]
            k_ctx = (lambda: kc_ref[hh]) if has_ctx else None
            shift = _exact_shift(k_new, k_ctx, q) if exact else _bound_shift(q, kn2[hh:hh + 1, 0:1])
            _exp_stage(e_bufs[u % 2], k_new, k_ctx, q, shift, n_ctx)

        def value_stage(u):
            t, hh = divmod(u, MLA_HEADS)
            o = _value_stage(e_bufs[u % 2], lambda: vt_ref[hh],
                             (lambda: vtc_ref[hh]) if has_ctx else None, n_ctx)
            denom = o[HEAD_V:HEAD_V + 1, :]
            denoms.append(denom)
            ot[hh * HEAD_V:(hh + 1) * HEAD_V, :] = o[0:HEAD_V, :] * (1.0 / denom)
            if hh == MLA_HEADS - 1:
                o_ref[t * ATT_TQ:(t + 1) * ATT_TQ, :] = ot[...].T

        _run_pipeline(nsub * MLA_HEADS, exp_stage, value_stage)
        return jnp.min(functools.reduce(jnp.minimum, denoms))

    denom_min = run(exact=False)

    @pl.when(jnp.logical_not(denom_min >= SAFE_DENOM))
    def _():
        run(exact=True)


def _mla_attn(q, k, vt, ctx, *, nb, n):
    nsub = _att_nsub(n)
    tq = nsub * ATT_TQ
    npt = n // tq
    H, S = MLA_HEADS, MLA_SLOT
    in_specs = [
        pl.BlockSpec((H, tq, S), lambda b, j: (0, b * npt + j, 0)),
        pl.BlockSpec((H, n, S), lambda b, j: (0, b, 0)),
        pl.BlockSpec((H, VT_ROWS, n), lambda b, j: (0, 0, b)),
    ]
    args = [q, k, vt]
    n_ctx = 0
    if ctx is not None:
        n_ctx = ctx[0].shape[1] // nb
        in_specs += [
            pl.BlockSpec((H, n_ctx, S), lambda b, j: (0, b, 0)),
            pl.BlockSpec((H, VT_ROWS, n_ctx), lambda b, j: (0, 0, b)),
        ]
        args += list(ctx)
    return pl.pallas_call(
        functools.partial(_mla_attn_kernel, has_ctx=ctx is not None, nsub=nsub),
        grid=(nb, npt),
        in_specs=in_specs,
        out_specs=pl.BlockSpec((tq, 256), lambda b, j: (b * npt + j, 0)),
        out_shape=jax.ShapeDtypeStruct((nb * n, 256), F32),
        scratch_shapes=_att_scratch(n + n_ctx),
        compiler_params=_params("arbitrary", "arbitrary"),
        name="mla_attn_ctx" if ctx is not None else "mla_attn",
    )(*args)


def _diff_attn_kernel(*refs, has_ctx, nsub, lam_init):
    if has_ctx:
        lv_ref, g_ref, q_ref, k_ref, vt_ref, kc_ref, vtc_ref, o_ref, kn2, ot, e0, e1 = refs
        n_ctx = kc_ref.shape[0]
    else:
        lv_ref, g_ref, q_ref, k_ref, vt_ref, o_ref, kn2, ot, e0, e1 = refs
        n_ctx = 0
    e_bufs = (e0, e1)
    lv = lv_ref[...]
    lam = (jnp.exp(jnp.sum(lv[0:1] * lv[1:2], axis=-1, keepdims=True))
           - jnp.exp(jnp.sum(lv[2:3] * lv[3:4], axis=-1, keepdims=True)) + lam_init)
    lane128 = lax.broadcasted_iota(jnp.int32, (1, 128), 1)
    n_pairs = 2 * DIFF_HEADS

    @pl.when(pl.program_id(1) == 0)
    def _():
        dim = lax.broadcasted_iota(jnp.int32, (256, 128), 0)
        col = lax.broadcasted_iota(jnp.int32, (256, 128), 1)
        indicator = jnp.where(dim // DIFF_DIM == col, 1.0, 0.0).astype(BF16)
        kn2[0:1, :] = _max_row_norm2(k_ref[...], kc_ref[...] if has_ctx else None, indicator)

    def run(exact):
        denoms = []
        outs = {}

        def exp_stage(u):
            t, p = divmod(u, n_pairs)
            tile = slice((p * DIFF_DIM // 128) * 128, (p * DIFF_DIM // 128 + 1) * 128)
            k_new = lambda: k_ref[:, tile]
            k_ctx = (lambda: kc_ref[:, tile]) if has_ctx else None
            q = q_ref[t * ATT_TQ:(t + 1) * ATT_TQ, tile]
            lo = p * DIFF_DIM - tile.start
            in_pair = (lane128 >= lo) & (lane128 < lo + DIFF_DIM)
            qm = jnp.where(in_pair, q, jnp.zeros_like(q))
            shift = _exact_shift(k_new, k_ctx, qm) if exact else _bound_shift(qm, kn2[0:1, p:p + 1])
            _exp_stage(e_bufs[u % 2], k_new, k_ctx, qm, shift, n_ctx)

        def value_stage(u):
            t, p = divmod(u, n_pairs)
            hh = p // 2
            o = _value_stage(e_bufs[u % 2], lambda: vt_ref[hh],
                             (lambda: vtc_ref[hh]) if has_ctx else None, n_ctx)
            denom = o[HEAD_V:HEAD_V + 1, :]
            denoms.append(denom)
            outs[u] = (o[0:HEAD_V, :], denom)
            if p % 2 == 1:
                (o0, l0), (o1, l1) = outs.pop(u - 1), outs.pop(u)
                o = o0 * (1.0 / l0) - o1 * (lam / l1)
                msq = jnp.sum(o * o, axis=0, keepdims=True) * (1.0 / HEAD_V)
                ot[hh * HEAD_V:(hh + 1) * HEAD_V, :] = o * lax.rsqrt(msq + EPS)
            if p == n_pairs - 1:
                o_ref[t * ATT_TQ:(t + 1) * ATT_TQ, :] = (ot[...].T * g_ref[...]) * (1.0 - lam_init)

        _run_pipeline(nsub * n_pairs, exp_stage, value_stage)
        return jnp.min(functools.reduce(jnp.minimum, denoms))

    denom_min = run(exact=False)

    @pl.when(jnp.logical_not(denom_min >= SAFE_DENOM))
    def _():
        run(exact=True)


def _diff_attn(q, k, vt, ctx, lw, *, nb, n, lam_init):
    nsub = 1
    tq = nsub * ATT_TQ
    npt = n // tq
    in_specs = [
        lw.spec("diff_lambda"),
        lw.spec("diff_g"),
        pl.BlockSpec((tq, 256), lambda b, j: (b * npt + j, 0)),
        pl.BlockSpec((n, 256), lambda b, j: (b, 0)),
        pl.BlockSpec((DIFF_HEADS, VT_ROWS, n), lambda b, j: (0, 0, b)),
    ]
    args = [lw["diff_lambda"], lw["diff_g"], q, k, vt]
    n_ctx = 0
    if ctx is not None:
        n_ctx = ctx[0].shape[0] // nb
        in_specs += [pl.BlockSpec((n_ctx, 256), lambda b, j: (b, 0)),
                     pl.BlockSpec((DIFF_HEADS, VT_ROWS, n_ctx), lambda b, j: (0, 0, b))]
        args += list(ctx)
    return pl.pallas_call(
        functools.partial(_diff_attn_kernel, has_ctx=ctx is not None, nsub=nsub, lam_init=lam_init),
        grid=(nb, npt),
        in_specs=in_specs,
        out_specs=pl.BlockSpec((tq, 256), lambda b, j: (b * npt + j, 0)),
        out_shape=jax.ShapeDtypeStruct((nb * n, 256), F32),
        scratch_shapes=_att_scratch(n + n_ctx),
        compiler_params=_params("arbitrary", "arbitrary"),
        name="diff_attn_ctx" if ctx is not None else "diff_attn",
    )(*args)


def _shift_rows(v, k):
    return pltpu.roll(v, (-k) % v.shape[0], 0)


def _scan_strided(a_ref, b_ref, h_ref, row0, carry, n_rows, reverse):
    sub = lax.broadcasted_iota(jnp.int32, (8, 128), 0)
    span = 8 * SCAN_RUN
    order = tuple(range(SCAN_RUN))[::-1] if reverse else tuple(range(SCAN_RUN))
    starts = tuple(range(0, n_rows, span))[::-1] if reverse else tuple(range(0, n_rows, span))
    carries = []
    for lt in range(a_ref.shape[0]):
        c_in = carry[:, lt * 128:(lt + 1) * 128]
        for start in starts:
            tile = lambda ref, g: ref[lt, pl.ds(row0 + start + g, 8, stride=SCAN_RUN), :]
            a = [tile(a_ref, g) for g in range(SCAN_RUN)]
            b = [tile(b_ref, g) for g in range(SCAN_RUN)]
            h = {order[0]: b[order[0]]}
            p = {order[0]: a[order[0]]}
            for prev, g in zip(order, order[1:]):
                h[g] = a[g] * h[prev] + b[g]
                p[g] = a[g] * p[prev]
            pi, hi = p[order[-1]], h[order[-1]]
            for s in (1, 2, 4):
                shift = 8 - s if reverse else s
                valid = (sub < 8 - s) if reverse else (sub >= s)
                pr, hr = pltpu.roll(pi, shift, 0), pltpu.roll(hi, shift, 0)
                hi = jnp.where(valid, pi * hr + hi, hi)
                pi = jnp.where(valid, pi * pr, pi)
            one = 7 if reverse else 1
            first = (sub == 7) if reverse else (sub == 0)
            pe = jnp.where(first, 1.0, pltpu.roll(pi, one, 0))
            he = jnp.where(first, 0.0, pltpu.roll(hi, one, 0))
            c = pe * c_in + he
            for g in range(SCAN_RUN):
                h_ref[lt, pl.ds(start + g, 8, stride=SCAN_RUN), :] = h[g] + p[g] * c
            last = 0 if reverse else 7
            c_in = pi[last:last + 1, :] * c_in + hi[last:last + 1, :]
        carries.append(c_in)
    return jnp.concatenate(carries, axis=1)


def _sigmoid(x):
    return 0.5 * jnp.tanh(0.5 * x) + 0.5


def _gelu_tanh(x):
    return x * (0.5 * (1.0 + jnp.tanh(math.sqrt(2.0 / math.pi) * (x + 0.044715 * (x * x * x)))))


def _lru_kernel(u_ref, h0_ref, cw_ref, cb_ref, wg_ref, bg_ref, lam_ref, y_ref, st_ref,
                xpad, a1s, b1s, a0c, b0c, hc, *, N, T):
    W = LRU_WIDTH
    nc = N // T
    tiles = [slice(lt * 128, (lt + 1) * 128) for lt in range(W // 128)]
    zeros = jnp.zeros((HALO, W), F32)
    xpad[0:HALO, :] = zeros
    xpad[N + HALO:N + 2 * HALO, :] = zeros

    def fill(j, carry):
        r0 = pl.multiple_of(j * T, T)
        xpad[pl.ds(r0 + HALO, T), :] = u_ref[pl.ds(r0, T), 0:W]
        return carry

    lax.fori_loop(0, nc, fill, 0)

    z = -lam_ref[...]
    sp = jnp.maximum(z, 0.0) + jnp.log1p(jnp.exp(-jnp.abs(z)))
    cw = cw_ref[...]
    cb = cb_ref[...]
    bg = bg_ref[...]

    def fwd(j, carry):
        r0 = pl.multiple_of(j * T, T)
        ext = xpad[pl.ds(r0, T + 2 * HALO), :]
        body = slice(HALO, HALO + T)
        xc = cb
        for tap in range(4):
            xc = xc + _shift_rows(ext, tap - 1)[body] * cw[tap:tap + 1]
        g = _sigmoid(_dot(xc.astype(BF16), wg_ref[...]) + bg)
        ab = []
        for d in range(2):
            r = g[:, d * W:(d + 1) * W]
            i = g[:, (2 + d) * W:(3 + d) * W]
            log_a = (-LRU_C * r) * sp[d:d + 1]
            a = jnp.exp(log_a)
            bt = (jnp.sqrt(1.0 - a * a) * i) * xc
            ab.append((a, bt))
        for lt, lanes in enumerate(tiles):
            a0c[lt] = ab[0][0][:, lanes]
            b0c[lt] = ab[0][1][:, lanes]
            a1s[lt, pl.ds(r0, T), :] = ab[1][0][:, lanes]
            b1s[lt, pl.ds(r0, T), :] = ab[1][1][:, lanes]
        carry = _scan_strided(a0c, b0c, hc, 0, carry, T, reverse=False)
        for lt, lanes in enumerate(tiles):
            y_ref[pl.ds(r0, T), lanes] = hc[lt]
        return carry

    cf = lax.fori_loop(0, nc, fwd, h0_ref[0, 0:1, :])

    def bwd(jj, carry):
        r0 = pl.multiple_of((nc - 1 - jj) * T, T)
        carry = _scan_strided(a1s, b1s, hc, r0, carry, T, reverse=True)
        for lt, lanes in enumerate(tiles):
            gb = u_ref[pl.ds(r0, T), W + lt * 128:W + (lt + 1) * 128]
            y_ref[pl.ds(r0, T), lanes] = (y_ref[pl.ds(r0, T), lanes] + hc[lt]) * _gelu_tanh(gb)
        return carry

    cbw = lax.fori_loop(0, nc, bwd, h0_ref[0, 1:2, :])
    st_ref[0, 0:1, :] = cf
    st_ref[0, 1:2, :] = cbw


def _lru(u, h0, h0_block, lw, *, nb, n):
    T = min(n, 256)
    W = LRU_WIDTH
    return pl.pallas_call(
        functools.partial(_lru_kernel, N=n, T=T),
        grid=(nb,),
        in_specs=[
            pl.BlockSpec((n, 2 * W), lambda b: (b, 0)),
            pl.BlockSpec((1, 2, W), lambda b: (h0_block(b), 0, 0)),
            lw.spec("conv_w"), lw.spec("conv_b"), lw.spec("w_gate"), lw.spec("b_gate"),
            lw.spec("lru_lambda"),
        ],
        out_specs=[
            pl.BlockSpec((n, W), lambda b: (b, 0)),
            pl.BlockSpec((1, 2, W), lambda b: (b, 0, 0)),
        ],
        out_shape=[
            jax.ShapeDtypeStruct((nb * n, W), F32),
            jax.ShapeDtypeStruct((nb, 2, W), F32),
        ],
        scratch_shapes=[
            pltpu.VMEM((n + 2 * HALO, W), F32),
            pltpu.VMEM((W // 128, n, 128), F32),
            pltpu.VMEM((W // 128, n, 128), F32),
            pltpu.VMEM((W // 128, T, 128), F32),
            pltpu.VMEM((W // 128, T, 128), F32),
            pltpu.VMEM((W // 128, T, 128), F32),
        ],
        compiler_params=_params("arbitrary"),
        name="rglru",
    )(u, h0, lw["conv_w"], lw["conv_b"], lw["w_gate"], lw["b_gate"], lw["lru_lambda"])


def _pool_kernel(u_ref, wp_ref, sc_ref, y_ref, xpad, *, N, T):
    W = GROUP_WIDTH
    nc = N // T
    zeros = jnp.zeros((HALO, W), F32)
    xpad[0:HALO, :] = zeros
    xpad[N + HALO:N + 2 * HALO, :] = zeros

    def fill(j, carry):
        r0 = pl.multiple_of(j * T, T)
        xpad[pl.ds(r0 + HALO, T), :] = u_ref[pl.ds(r0, T), :]
        return carry

    lax.fori_loop(0, nc, fill, 0)

    grp = lax.broadcasted_iota(jnp.int32, (1, W), 1) // POOL_CH
    half = jnp.where(grp == 0, 1, jnp.where(grp == 1, 2, jnp.where(grp == 2, 4, 8)))
    scale = sc_ref[...]

    def chunk(j, carry):
        r0 = pl.multiple_of(j * T, T)
        ext = xpad[pl.ds(r0, T + 2 * HALO), :]
        w2 = _shift_rows(ext, -1) + ext
        w4 = _shift_rows(w2, -1) + _shift_rows(w2, 1)
        w8 = _shift_rows(w4, -2) + _shift_rows(w4, 2)
        w16 = _shift_rows(w8, -4) + _shift_rows(w8, 4)
        ws = jnp.where(grp == 0, w2, jnp.where(grp == 1, w4, jnp.where(grp == 2, w8, w16)))
        body = slice(HALO, HALO + T)
        t = r0 + lax.broadcasted_iota(jnp.int32, (T, W), 0)
        cnt = (jnp.minimum(t + half, N) - jnp.maximum(t - half, 0)).astype(F32)
        d = ws[body] / cnt - ext[body]
        y_ref[pl.ds(r0, T), :] = _dot(d.astype(BF16), wp_ref[...]) * scale
        return carry

    lax.fori_loop(0, nc, chunk, 0)


def _pool(u, lw, *, nb, n):
    W = GROUP_WIDTH
    T = min(n, 256)
    return pl.pallas_call(
        functools.partial(_pool_kernel, N=n, T=T),
        grid=(nb,),
        in_specs=[pl.BlockSpec((n, W), lambda b: (b, 0)), lw.spec("w_pool"), lw.spec("pool_scale")],
        out_specs=pl.BlockSpec((n, W), lambda b: (b, 0)),
        out_shape=jax.ShapeDtypeStruct((nb * n, W), F32),
        scratch_shapes=[pltpu.VMEM((n + 2 * HALO, W), F32)],
        compiler_params=_params("arbitrary"),
        name="pool_mixer",
    )(u, lw["w_pool"], lw["pool_scale"])


def _mix_ffn_kernel(*refs, final):
    if final:
        (x_ref, ya_ref, yb_ref, yc_ref, yd_ref, mod_ref, g2_ref, wo_ref, wg_ref, wu_ref, wd_ref,
         gf_ref, o_ref) = refs
    else:
        (x_ref, ya_ref, yb_ref, yc_ref, yd_ref, mod_ref, g2_ref, wo_ref, wg_ref, wu_ref, wd_ref,
         o_ref) = refs
    mod = mod_ref[0]
    gate1 = mod[:, 2 * D_MODEL:3 * D_MODEL]
    sh2 = mod[:, 3 * D_MODEL:4 * D_MODEL]
    sc2 = mod[:, 4 * D_MODEL:5 * D_MODEL]
    gate2 = mod[:, 5 * D_MODEL:6 * D_MODEL]
    mix = None
    for i, y_ref in enumerate((ya_ref, yb_ref, yc_ref, yd_ref)):
        part = _dot(y_ref[...].astype(BF16), wo_ref[i * GROUP_WIDTH:(i + 1) * GROUP_WIDTH, :])
        mix = part if mix is None else mix + part
    x1 = x_ref[...] + gate1 * mix
    h = _rms_rows(x1, D_MODEL) * g2_ref[...]
    hb = (h * (1.0 + sc2) + sh2).astype(BF16)
    ff = None
    for lo, hi in FF_CHUNKS:
        g = _dot(hb, wg_ref[:, lo:hi])
        up = _dot(hb, wu_ref[:, lo:hi])
        act = ((g * jax.nn.sigmoid(g)) * up).astype(BF16)
        part = _dot(act, wd_ref[lo:hi, :])
        ff = part if ff is None else ff + part
    x2 = x1 + gate2 * ff
    if final:
        x2 = _rms_rows(x2, D_MODEL) * gf_ref[...]
    o_ref[...] = x2


def _mix_ffn(x, ys, mod, lw, gf, *, nb, n, final):
    T = nb * n
    tm = TOKEN_TILE
    npt = n // tm

    def tok(width):
        return pl.BlockSpec((tm, width), lambda i: (i, 0))

    wnames = ("g2", "w_out", "w_gate_ff", "w_up_ff", "w_down")
    in_specs = [tok(D_MODEL), tok(256), tok(256), tok(256), tok(256),
                mod.spec(lambda i: i // npt)] + [lw.spec(nm) for nm in wnames]
    args = [x, *ys, mod.table] + [lw[nm] for nm in wnames]
    if final:
        in_specs.append(_resident((1, D_MODEL)))
        args.append(gf)
    return pl.pallas_call(
        functools.partial(_mix_ffn_kernel, final=final),
        grid=(T // tm,),
        in_specs=in_specs,
        out_specs=tok(D_MODEL),
        out_shape=jax.ShapeDtypeStruct((T, D_MODEL), F32),
        compiler_params=_params("arbitrary"),
        name="mix_ffn_final" if final else "mix_ffn",
    )(*args)


def _block_diag(w):
    L, G, c, e = w.shape
    return jnp.einsum('lgce,gh->lgche', w, jnp.eye(G, dtype=w.dtype)).reshape(L, G * c, G * e)


def _rot_cols(w):
    return jnp.concatenate([-w[..., 16:32], w[..., 0:16]], axis=-1)


def _stack_weights(p):
    w_in = p["w_in"]
    o1 = MLA_Q_RANK
    o2 = o1 + MLA_KV_RANK
    o3 = o2 + MLA_ROPE
    c_q, c_kv, k_r, rest = w_in[..., :o1], w_in[..., o1:o2], w_in[..., o2:o3], w_in[..., o3:]
    z = lambda n: jnp.zeros((DEPTH, D_MODEL, n), F32)
    w_in_eff = jnp.concatenate([c_q, k_r, z(32), c_kv, z(64), _rot_cols(k_r), z(32), rest], axis=-1)

    w_uq = p["mla_w_uq"]
    qd = MLA_NOPE + MLA_ROPE
    wq_parts, wqr_parts = [], []
    zq = lambda n: jnp.zeros((DEPTH, MLA_Q_RANK, n), F32)
    for h in range(MLA_HEADS):
        wh = w_uq[..., h * qd:(h + 1) * qd]
        wq_parts += [wh, zq(MLA_SLOT - qd)]
        wqr_parts += [zq(MLA_NOPE), _rot_cols(wh[..., MLA_NOPE:]), zq(MLA_SLOT - qd)]
    pad_rows = lambda w: jnp.pad(w, ((0, 0), (0, 256 - MLA_Q_RANK), (0, 0)))
    w_ukv = p["mla_w_ukv"]
    wk_parts, wv_parts = [], []
    zk = jnp.zeros((DEPTH, MLA_KV_RANK, MLA_SLOT - MLA_NOPE), F32)
    for h in range(MLA_HEADS):
        base = h * (MLA_NOPE + MLA_V)
        wk_parts += [w_ukv[..., base:base + MLA_NOPE], zk]
        wv_parts.append(w_ukv[..., base + MLA_NOPE:base + MLA_NOPE + MLA_V])

    w_r, w_i, b_r, b_i = p["lru_w_r"], p["lru_w_i"], p["lru_b_r"], p["lru_b_i"]
    w_gate = jnp.concatenate([_block_diag(w_r[:, 0]), _block_diag(w_r[:, 1]),
                              _block_diag(w_i[:, 0]), _block_diag(w_i[:, 1])], axis=-1)
    b_gate = jnp.concatenate([b_r[:, 0], b_r[:, 1], b_i[:, 0], b_i[:, 1]], axis=-1)
    w_gu = p["w_gu"]
    row = lambda v: v[:, None, :]
    return {
        "g1": row(p["norm1_g"]),
        "g2": row(p["norm2_g"]),
        "w_in": w_in_eff.astype(BF16),
        "gq": row(jnp.pad(p["mla_q_norm_g"], ((0, 0), (0, 256 - MLA_Q_RANK)))),
        "gkv": row(p["mla_kv_norm_g"]),
        "wq": pad_rows(jnp.concatenate(wq_parts, axis=-1)).astype(BF16),
        "wqr": pad_rows(jnp.concatenate(wqr_parts, axis=-1)).astype(BF16),
        "wkv": jnp.concatenate(wk_parts + wv_parts, axis=-1).astype(BF16),
        "conv_w": p["lru_conv_w"],
        "conv_b": row(p["lru_conv_b"]),
        "w_gate": w_gate.astype(BF16),
        "b_gate": row(b_gate),
        "lru_lambda": p["lru_lambda"],
        "w_pool": _block_diag(p["pool_w"]).astype(BF16),
        "pool_scale": row(p["pool_scale"]),
        "diff_lambda": p["diff_lambda"],
        "diff_g": row(jnp.tile(p["diff_norm_g"], (1, DIFF_HEADS))),
        "w_out": p["w_out"].astype(BF16),
        "w_gate_ff": w_gu[..., :FF_HIDDEN].astype(BF16),
        "w_up_ff": w_gu[..., FF_HIDDEN:].astype(BF16),
        "w_down": p["w_down"].astype(BF16),
    }


def _rope_tables(n, positional):
    quarter = MLA_ROPE // 4
    if positional:
        t = jnp.arange(n)
        row = (t // GRID_W).astype(F32)
        col = (t % GRID_W).astype(F32)
        inv = ROPE_BASE ** (-jnp.arange(quarter, dtype=F32) / quarter)
        ang = jnp.concatenate([row[:, None] * inv, col[:, None] * inv], axis=-1)
        cos, sin = jnp.cos(ang), jnp.sin(ang)
    else:
        cos, sin = jnp.ones((n, 16), F32), jnp.zeros((n, 16), F32)
    one = lambda w: jnp.ones((n, w), F32)
    zero = lambda w: jnp.zeros((n, w), F32)
    scale = LOG2E / math.sqrt(MLA_NOPE + MLA_ROPE)
    return jnp.concatenate(
        [one(64) * scale, cos * scale, cos * scale, one(32) * scale,
         zero(64), sin * scale, sin * scale, zero(32),
         zero(64), cos, cos, zero(32),
         zero(64), sin, sin, zero(32)]
        + [cos, cos] * 8 + [-sin, zero(16)] * 8 + [zero(16), sin] * 8, axis=1)


def _layer(x, mod, lw, tabs, layer_idx, ctx, gf, *, nb, n, final):
    emit_cache = ctx is None
    tok_nb, tok_n = (1, nb * n) if mod.shared else (nb, n)
    outs = _inproj(x, mod, lw, tabs, nb=tok_nb, n=tok_n, emit_cache=emit_cache)
    q, k, vt, u_lru, u_pool, dq, dk, dvt = outs[:8]
    lam_init = 0.8 - 0.6 * math.exp(-0.3 * layer_idx)
    if ctx is None:
        h0 = jnp.zeros((1, 2, LRU_WIDTH), F32)
        h0_block = lambda b: 0
        mla_ctx = diff_ctx = None
    else:
        ckv, kr_pad, cdk, cdv, h0 = ctx
        p = ckv.shape[0] // (nb * DEPTH)
        h0_block = lambda b: b * DEPTH + layer_idx
        kc, vtc, dkc, dvtc = _ctx_prep(ckv, kr_pad, cdk, cdv, lw, nb=nb, p=p)
        mla_ctx = (kc, vtc)
        diff_ctx = (dkc, dvtc)
    y_mla = _mla_attn(q, k, vt, mla_ctx, nb=nb, n=n)
    y_lru, st = _lru(u_lru, h0, h0_block, lw, nb=nb, n=n)
    y_pool = _pool(u_pool, lw, nb=nb, n=n)
    y_diff = _diff_attn(dq, dk, dvt, diff_ctx, lw, nb=nb, n=n, lam_init=lam_init)
    x2 = _mix_ffn(x, (y_mla, y_lru, y_pool, y_diff), mod, lw, gf, nb=tok_nb, n=tok_n, final=final)
    cache = (outs[8], outs[9][:, 64:96], outs[10], outs[11], st) if emit_cache else None
    return x2, cache


def kernel(x_prompt, x_sample, cache_mla_ckv, cache_mla_krope, cache_diff_k, cache_diff_v, state_lru,
           c, c_ctx, w_ada, b_ada, norm1_g, norm2_g, w_in, mla_q_norm_g, mla_w_uq, mla_kv_norm_g,
           mla_w_ukv, lru_conv_w, lru_conv_b, lru_w_r, lru_b_r, lru_w_i, lru_b_i, lru_lambda, pool_w,
           pool_scale, diff_lambda, diff_norm_g, w_out, w_gu, w_down, final_norm_g):
    p = {
        "norm1_g": norm1_g, "norm2_g": norm2_g, "w_in": w_in, "mla_q_norm_g": mla_q_norm_g,
        "mla_w_uq": mla_w_uq, "mla_kv_norm_g": mla_kv_norm_g, "mla_w_ukv": mla_w_ukv,
        "lru_conv_w": lru_conv_w, "lru_conv_b": lru_conv_b, "lru_w_r": lru_w_r, "lru_b_r": lru_b_r,
        "lru_w_i": lru_w_i, "lru_b_i": lru_b_i, "lru_lambda": lru_lambda, "pool_w": pool_w,
        "pool_scale": pool_scale, "diff_lambda": diff_lambda, "diff_norm_g": diff_norm_g,
        "w_out": w_out, "w_gu": w_gu, "w_down": w_down,
    }
    Bp, Np, _ = x_prompt.shape
    Bs, Ns, _ = x_sample.shape
    P = cache_mla_ckv.shape[2]

    cond_all = jnp.concatenate([c, c_ctx[None, :], jnp.zeros((MOD_ROWS - Bs - 1, D_MODEL), F32)], axis=0)
    mod_table = _ada(cond_all, w_ada, b_ada).reshape(DEPTH * MOD_ROWS, 1, 6 * D_MODEL)
    tabs_p = _rope_tables(Bp * Np, positional=False)
    tabs_s = _rope_tables(Ns, positional=True)
    kr_pad = jnp.pad(cache_mla_krope, ((0, 0), (0, 0), (0, 0), (MLA_NOPE, MLA_SLOT - MLA_NOPE - MLA_ROPE)))
    flat = lambda a, w: a.reshape(Bs * DEPTH * P, w)
    ctx = (flat(cache_mla_ckv, MLA_KV_RANK), flat(kr_pad, MLA_SLOT), flat(cache_diff_k, 256),
           flat(cache_diff_v, 256), state_lru.reshape(Bs * DEPTH, 2, LRU_WIDTH))
    gf = final_norm_g[None, :]
    stacked = _stack_weights(p)

    xp = x_prompt.reshape(Bp * Np, D_MODEL)
    xs = x_sample.reshape(Bs * Ns, D_MODEL)
    caches = []
    for l in range(DEPTH):
        lw = _LayerWeights(stacked, l)
        final = l == DEPTH - 1
        mod_p = _Mod(mod_table, l * MOD_ROWS + Bs, shared=True)
        mod_s = _Mod(mod_table, l * MOD_ROWS, shared=False)
        xp, cache = _layer(xp, mod_p, lw, tabs_p, l, None, gf, nb=Bp, n=Np, final=final)
        caches.append(cache)
        xs, _ = _layer(xs, mod_s, lw, tabs_s, l, ctx, gf, nb=Bs, n=Ns, final=final)

    stack = lambda i, w: jnp.stack([cc[i].reshape(Bp, Np, w) for cc in caches], axis=1)
    new_mla_ckv = stack(0, MLA_KV_RANK)
    new_mla_krope = stack(1, MLA_ROPE)
    new_diff_k = stack(2, 256).reshape(Bp, DEPTH, Np, DIFF_HEADS, 2, DIFF_DIM)
    new_diff_v = stack(3, 256).reshape(Bp, DEPTH, Np, DIFF_HEADS, 2 * DIFF_DIM)
    new_state_lru = jnp.stack([cc[4] for cc in caches], axis=1)
    return (xp.reshape(Bp, Np, D_MODEL), xs.reshape(Bs, Ns, D_MODEL),
            new_mla_ckv, new_mla_krope, new_diff_k, new_diff_v, new_state_lru)
```

```python
import functools
import math

import jax
import jax.numpy as jnp
import numpy as np
from jax import lax
from jax.experimental import pallas as pl
from jax.experimental.pallas import tpu as pltpu

F32 = jnp.float32
BF16 = jnp.bfloat16

D_MODEL = 1024
DEPTH = 2
GRID_W = 64
GROUP_WIDTH = 256
MLA_HEADS = 4
MLA_NOPE = 64
MLA_ROPE = 32
MLA_V = 64
MLA_Q_RANK = 192
MLA_KV_RANK = 128
MLA_SLOT = 128
LRU_WIDTH = 256
LRU_C = 8.0
POOL_WINDOWS = (2, 4, 8, 16)
POOL_CH = 64
DIFF_HEADS = 4
DIFF_DIM = 32
HEAD_V = 64
FF_HIDDEN = 2816
FF_CHUNKS = ((0, 1536), (1536, 2816))
ROPE_BASE = 10000.0
EPS = 1e-6
IN_EFF = 2048
HALO = 8
SCAN_RUN = 4
VT_ROWS = 80
ATT_TQ = 256
TOKEN_TILE = 512
TAB_WIDTH = 4 * 128 + 3 * 256
MOD_ROWS = 16
LOG2E = math.log2(math.e)

VMEM_LIMIT_BYTES = 56 * 1024 * 1024

_NT = (((1,), (1,)), ((), ()))


def _params(*sem):
    return pltpu.CompilerParams(dimension_semantics=sem, vmem_limit_bytes=VMEM_LIMIT_BYTES)


def _resident(shape):
    zeros = (0,) * len(shape)
    return pl.BlockSpec(shape, lambda *_: zeros, pipeline_mode=pl.Buffered(1))


def _dot(a, b):
    return jnp.dot(a, b, preferred_element_type=F32)


def _dot_nt(a, b):
    return lax.dot_general(a, b, _NT, preferred_element_type=F32)


def _rms_rows(x, width):
    ms = jnp.sum(x * x, axis=-1, keepdims=True) * (1.0 / width)
    return x * lax.rsqrt(ms + EPS)


def _store_vt(vt_ref, v):
    vt = v.T
    rows = v.shape[0]
    pad = VT_ROWS - HEAD_V
    ones_row = jnp.where(lax.broadcasted_iota(jnp.int32, (pad, rows), 0) == 0, 1.0, 0.0).astype(BF16)
    for hh in range(vt_ref.shape[0]):
        vt_ref[hh, 0:HEAD_V, :] = vt[hh * HEAD_V:(hh + 1) * HEAD_V, :].astype(BF16)
        vt_ref[hh, HEAD_V:VT_ROWS, :] = ones_row


class _Mod:
    def __init__(self, table, row0, shared):
        self.table, self.row0, self.shared = table, row0, shared

    def spec(self, batch_of):
        row0 = self.row0
        if self.shared:
            return pl.BlockSpec((1, 1, 6 * D_MODEL), lambda *g: (row0, 0, 0))
        return pl.BlockSpec((1, 1, 6 * D_MODEL), lambda *g: (row0 + batch_of(*g), 0, 0))


class _LayerWeights:
    def __init__(self, stacked, layer):
        self.stacked, self.layer = stacked, layer

    def __getitem__(self, name):
        return self.stacked[name]

    def spec(self, name):
        layer = self.layer
        _, rows, cols = self.stacked[name].shape
        return pl.BlockSpec((None, rows, cols), lambda *_: (layer, 0, 0), pipeline_mode=pl.Buffered(1))


def _ada_kernel(cond_ref, w_ref, b_ref, out_ref):
    c = cond_ref[...]
    s = c * jax.nn.sigmoid(c)
    out_ref[0] = _dot(s.astype(BF16), w_ref[0].astype(BF16)) + b_ref[0]


def _ada(cond_all, w_ada, b_ada):
    rows = cond_all.shape[0]
    tn = 1536
    return pl.pallas_call(
        _ada_kernel,
        grid=(DEPTH, 6 * D_MODEL // tn),
        in_specs=[
            pl.BlockSpec((rows, D_MODEL), lambda l, j: (0, 0)),
            pl.BlockSpec((1, D_MODEL, tn), lambda l, j: (l, 0, j)),
            pl.BlockSpec((1, 1, tn), lambda l, j: (l, 0, j)),
        ],
        out_specs=pl.BlockSpec((1, rows, tn), lambda l, j: (l, 0, j)),
        out_shape=jax.ShapeDtypeStruct((DEPTH, rows, 6 * D_MODEL), F32),
        compiler_params=_params("arbitrary", "arbitrary"),
        name="ada_mod",
    )(cond_all, w_ada, b_ada.reshape(DEPTH, 1, 6 * D_MODEL))


def _inproj_kernel(x_ref, mod_ref, g1_ref, win_ref, gq_ref, gkv_ref, wq_ref, wqr_ref, wkv_ref, tab_ref,
                   q_out, k_out, vt_out, lru_out, pool_out, dq_out, dk_out, dvt_out, *cache_outs):
    cosq_ref, sinq_ref, cosk_ref, sink_ref = (tab_ref.at[:, i * 128:(i + 1) * 128] for i in range(4))
    cosd_ref, sina_ref, sinb_ref = (tab_ref.at[:, 512 + i * 256:768 + i * 256] for i in range(3))
    x = x_ref[...]
    mod = mod_ref[0]
    sh1 = mod[:, 0:D_MODEL]
    sc1 = mod[:, D_MODEL:2 * D_MODEL]
    h = _rms_rows(x, D_MODEL) * g1_ref[...]
    hb = (h * (1.0 + sc1) + sh1).astype(BF16)

    u_mla = _dot(hb, win_ref[:, 0:512])
    u_pd = _dot(hb, win_ref[:, 1024:1536])
    u_kv = _dot(hb, win_ref[:, 1536:2048])
    t01 = u_mla[:, 0:256]
    lane = lax.broadcasted_iota(jnp.int32, (1, 256), 1)
    cq = jnp.where(lane < MLA_Q_RANK, t01, 0.0)
    cqn = (_rms_rows(cq, MLA_Q_RANK) * gq_ref[...]).astype(BF16)
    qa = _dot(cqn, wq_ref[...])
    qr = _dot(cqn, wqr_ref[...])
    cosq = cosq_ref[...]
    sinq = sinq_ref[...]
    ckv = u_mla[:, 256:384]
    lat = _rms_rows(ckv, MLA_KV_RANK) * gkv_ref[...]
    latb = lat.astype(BF16)
    kkv = _dot(latb, wkv_ref[...])
    kk = kkv[:, 0:MLA_HEADS * MLA_SLOT]
    _store_vt(vt_out, kkv[:, MLA_HEADS * MLA_SLOT:])
    t1 = t01[:, 128:256]
    t3 = u_mla[:, 384:512]
    kro = t1 * cosk_ref[...] + t3 * sink_ref[...]
    for hh in range(MLA_HEADS):
        sl = slice(hh * MLA_SLOT, (hh + 1) * MLA_SLOT)
        q_out[hh] = (qa[:, sl] * cosq + qr[:, sl] * sinq).astype(q_out.dtype)
        k_out[hh] = (kk[:, sl] + kro).astype(k_out.dtype)

    lru_out[...] = _dot(hb, win_ref[:, 512:1024])
    pool_out[...] = u_pd[:, 0:256]

    cosd = cosd_ref[...]
    sina = sina_ref[...]
    sinb = sinb_ref[...]

    def rope(t):
        return t * cosd + pltpu.roll(t, 256 - 16, 1) * sina + pltpu.roll(t, 16, 1) * sinb

    dq = u_pd[:, 256:512]
    dk = u_kv[:, 0:256]
    dv = u_kv[:, 256:512]
    dq_out[...] = (rope(dq) * (LOG2E / math.sqrt(DIFF_DIM))).astype(dq_out.dtype)
    dk_out[...] = rope(dk).astype(dk_out.dtype)
    _store_vt(dvt_out, dv)

    if cache_outs:
        lat_out, kr_out, dk_raw_out, dv_raw_out = cache_outs
        lat_out[...] = lat
        kr_out[...] = t1
        dk_raw_out[...] = dk
        dv_raw_out[...] = dv


def _inproj(x, mod, lw, tabs, *, nb, n, emit_cache):
    T = nb * n
    tm = TOKEN_TILE
    npt = n // tm
    row_blk = lambda j, b: b * npt + j

    def tok(width):
        return pl.BlockSpec((tm, width), lambda j, b: (row_blk(j, b), 0))

    def tab(width):
        return pl.BlockSpec((tm, width), lambda j, b: (j, 0))

    head = pl.BlockSpec((MLA_HEADS, tm, MLA_SLOT), lambda j, b: (0, row_blk(j, b), 0))
    vt_spec = pl.BlockSpec((MLA_HEADS, VT_ROWS, tm), lambda j, b: (0, 0, row_blk(j, b)))
    wnames = ("g1", "w_in", "gq", "gkv", "wq", "wqr", "wkv")
    in_specs = [tok(D_MODEL), mod.spec(lambda j, b: b)] + [lw.spec(nm) for nm in wnames] + [
        tab(TAB_WIDTH)]
    out_specs = [head, head, vt_spec, tok(512), tok(256), tok(256), tok(256), vt_spec]
    vt_shape = jax.ShapeDtypeStruct((MLA_HEADS, VT_ROWS, T), BF16)
    out_shape = [
        jax.ShapeDtypeStruct((MLA_HEADS, T, MLA_SLOT), BF16),
        jax.ShapeDtypeStruct((MLA_HEADS, T, MLA_SLOT), BF16),
        vt_shape,
        jax.ShapeDtypeStruct((T, 512), F32),
        jax.ShapeDtypeStruct((T, 256), F32),
        jax.ShapeDtypeStruct((T, 256), BF16),
        jax.ShapeDtypeStruct((T, 256), BF16),
        vt_shape,
    ]
    if emit_cache:
        out_specs += [tok(128), tok(128), tok(256), tok(256)]
        out_shape += [jax.ShapeDtypeStruct((T, 128), F32), jax.ShapeDtypeStruct((T, 128), F32),
                      jax.ShapeDtypeStruct((T, 256), F32), jax.ShapeDtypeStruct((T, 256), F32)]
    return pl.pallas_call(
        _inproj_kernel,
        grid=(npt, nb),
        in_specs=in_specs,
        out_specs=out_specs,
        out_shape=out_shape,
        compiler_params=_params("arbitrary", "arbitrary"),
        name="inproj_cache" if emit_cache else "inproj",
    )(x, mod.table, *[lw[nm] for nm in wnames], tabs)


def _ctx_prep_kernel(ckv_ref, kr_ref, dk_ref, dv_ref, wkv_ref, k_out, vt_out, dk_out, dvt_out):
    latb = ckv_ref[...].astype(BF16)
    kkv = _dot(latb, wkv_ref[...])
    kk = kkv[:, 0:MLA_HEADS * MLA_SLOT]
    kr = kr_ref[...]
    for hh in range(MLA_HEADS):
        k_out[hh] = (kk[:, hh * MLA_SLOT:(hh + 1) * MLA_SLOT] + kr).astype(k_out.dtype)
    _store_vt(vt_out, kkv[:, MLA_HEADS * MLA_SLOT:])
    dk_out[...] = dk_ref[...].astype(dk_out.dtype)
    _store_vt(dvt_out, dv_ref[...])


def _ctx_prep(ckv, kr_pad, cdk, cdv, lw, *, nb, p):
    T = nb * p
    layer = lw.layer
    cache_row = lambda w: pl.BlockSpec((p, w), lambda b: (b * DEPTH + layer, 0))
    row = lambda w: pl.BlockSpec((p, w), lambda b: (b, 0))
    vt_spec = pl.BlockSpec((MLA_HEADS, VT_ROWS, p), lambda b: (0, 0, b))
    vt_shape = jax.ShapeDtypeStruct((MLA_HEADS, VT_ROWS, T), BF16)
    return pl.pallas_call(
        _ctx_prep_kernel,
        grid=(nb,),
        in_specs=[cache_row(128), cache_row(128), cache_row(256), cache_row(256),
                  lw.spec("wkv")],
        out_specs=[pl.BlockSpec((MLA_HEADS, p, MLA_SLOT), lambda b: (0, b, 0)), vt_spec, row(256), vt_spec],
        out_shape=[jax.ShapeDtypeStruct((MLA_HEADS, T, MLA_SLOT), BF16), vt_shape,
                   jax.ShapeDtypeStruct((T, 256), BF16), vt_shape],
        compiler_params=_params("arbitrary"),
        name="ctx_prep",
    )(ckv, kr_pad, cdk, cdv, lw["wkv"])


SAFE_DENOM = 2.0 ** -60
BOUND_SLACK = 1.02


def _scores(k_new, k_ctx, q):
    sn = _dot_nt(k_new(), q)
    sc = _dot_nt(k_ctx(), q) if k_ctx is not None else None
    return sn, sc


def _exact_shift(k_new, k_ctx, q):
    sn, sc = _scores(k_new, k_ctx, q)
    m = jnp.max(sn, axis=0, keepdims=True)
    if sc is not None:
        m = jnp.maximum(m, jnp.max(sc, axis=0, keepdims=True))
    return m


def _bound_shift(q, key_norm2):
    qf = q.astype(F32)
    ones = jnp.ones((8, q.shape[1]), BF16)
    q_norm2 = _dot_nt(ones, (qf * qf).astype(BF16))[0:1, :]
    return jnp.sqrt(q_norm2 * key_norm2) * BOUND_SLACK


def _max_row_norm2(k_new, k_ctx, col_sum):
    def one(k):
        kf = k.astype(F32)
        return jnp.max(_dot((kf * kf).astype(BF16), col_sum), axis=0, keepdims=True)
    m = one(k_new)
    if k_ctx is not None:
        m = jnp.maximum(m, one(k_ctx))
    return m * BOUND_SLACK


def _exp_stage(e_buf, k_new, k_ctx, q, shift, n_ctx):
    sn, sc = _scores(k_new, k_ctx, q)
    e_buf[n_ctx:, :] = jnp.exp2(sn - shift).astype(BF16)
    if sc is not None:
        e_buf[0:n_ctx, :] = jnp.exp2(sc - shift).astype(BF16)


def _value_stage(e_buf, vt_new, vt_ctx, n_ctx):
    o = _dot(vt_new(), e_buf[n_ctx:, :])
    if vt_ctx is not None:
        o = o + _dot(vt_ctx(), e_buf[0:n_ctx, :])
    return o


def _run_pipeline(n_maps, exp_stage, value_stage):
    exp_stage(0)
    for u in range(n_maps):
        if u + 1 < n_maps:
            exp_stage(u + 1)
        value_stage(u)


def _att_scratch(nk):
    return [pltpu.VMEM((8, 128), F32),
            pltpu.VMEM((MLA_HEADS * HEAD_V, ATT_TQ), F32),
            pltpu.VMEM((nk, ATT_TQ), BF16), pltpu.VMEM((nk, ATT_TQ), BF16)]


def _att_nsub(n):
    return 2 if n % (2 * ATT_TQ) == 0 else 1


def _mla_attn_kernel(*refs, has_ctx, nsub):
    if has_ctx:
        q_ref, k_ref, vt_ref, kc_ref, vtc_ref, o_ref, kn2, ot, e0, e1 = refs
        n_ctx = kc_ref.shape[1]
    else:
        q_ref, k_ref, vt_ref, o_ref, kn2, ot, e0, e1 = refs
        n_ctx = 0
    e_bufs = (e0, e1)

    @pl.when(pl.program_id(1) == 0)
    def _():
        ones = jnp.ones((MLA_SLOT, 128), BF16)
        for hh in range(MLA_HEADS):
            kn2[hh:hh + 1, :] = _max_row_norm2(k_ref[hh], kc_ref[hh] if has_ctx else None, ones)

    def run(exact):
        denoms = []

        def exp_stage(u):
            t, hh = divmod(u, MLA_HEADS)
            q = q_ref[hh, t * ATT_TQ:(t + 1) * ATT_TQ, :]
            k_new = lambda: k_ref[hh]
            k_ctx = (lambda: kc_ref[hh]) if has_ctx else None
            shift = _exact_shift(k_new, k_ctx, q) if exact else _bound_shift(q, kn2[hh:hh + 1, 0:1])
            _exp_stage(e_bufs[u % 2], k_new, k_ctx, q, shift, n_ctx)

        def value_stage(u):
            t, hh = divmod(u, MLA_HEADS)
            o = _value_stage(e_bufs[u % 2], lambda: vt_ref[hh],
                             (lambda: vtc_ref[hh]) if has_ctx else None, n_ctx)
            denom = o[HEAD_V:HEAD_V + 1, :]
            denoms.append(denom)
            ot[hh * HEAD_V:(hh + 1) * HEAD_V, :] = o[0:HEAD_V, :] * (1.0 / denom)
            if hh == MLA_HEADS - 1:
                o_ref[t * ATT_TQ:(t + 1) * ATT_TQ, :] = ot[...].T

        _run_pipeline(nsub * MLA_HEADS, exp_stage, value_stage)
        return jnp.min(functools.reduce(jnp.minimum, denoms))

    denom_min = run(exact=False)

    @pl.when(jnp.logical_not(denom_min >= SAFE_DENOM))
    def _():
        run(exact=True)


def _mla_attn(q, k, vt, ctx, *, nb, n):
    nsub = _att_nsub(n)
    tq = nsub * ATT_TQ
    npt = n // tq
    H, S = MLA_HEADS, MLA_SLOT
    in_specs = [
        pl.BlockSpec((H, tq, S), lambda b, j: (0, b * npt + j, 0)),
        pl.BlockSpec((H, n, S), lambda b, j: (0, b, 0)),
        pl.BlockSpec((H, VT_ROWS, n), lambda b, j: (0, 0, b)),
    ]
    args = [q, k, vt]
    n_ctx = 0
    if ctx is not None:
        n_ctx = ctx[0].shape[1] // nb
        in_specs += [
            pl.BlockSpec((H, n_ctx, S), lambda b, j: (0, b, 0)),
            pl.BlockSpec((H, VT_ROWS, n_ctx), lambda b, j: (0, 0, b)),
        ]
        args += list(ctx)
    return pl.pallas_call(
        functools.partial(_mla_attn_kernel, has_ctx=ctx is not None, nsub=nsub),
        grid=(nb, npt),
        in_specs=in_specs,
        out_specs=pl.BlockSpec((tq, 256), lambda b, j: (b * npt + j, 0)),
        out_shape=jax.ShapeDtypeStruct((nb * n, 256), F32),
        scratch_shapes=_att_scratch(n + n_ctx),
        compiler_params=_params("arbitrary", "arbitrary"),
        name="mla_attn_ctx" if ctx is not None else "mla_attn",
    )(*args)


def _diff_attn_kernel(*refs, has_ctx, nsub, lam_init):
    if has_ctx:
        lv_ref, g_ref, q_ref, k_ref, vt_ref, kc_ref, vtc_ref, o_ref, kn2, ot, e0, e1 = refs
        n_ctx = kc_ref.shape[0]
    else:
        lv_ref, g_ref, q_ref, k_ref, vt_ref, o_ref, kn2, ot, e0, e1 = refs
        n_ctx = 0
    e_bufs = (e0, e1)
    lv = lv_ref[...]
    lam = (jnp.exp(jnp.sum(lv[0:1] * lv[1:2], axis=-1, keepdims=True))
           - jnp.exp(jnp.sum(lv[2:3] * lv[3:4], axis=-1, keepdims=True)) + lam_init)
    lane128 = lax.broadcasted_iota(jnp.int32, (1, 128), 1)
    n_pairs = 2 * DIFF_HEADS

    @pl.when(pl.program_id(1) == 0)
    def _():
        dim = lax.broadcasted_iota(jnp.int32, (256, 128), 0)
        col = lax.broadcasted_iota(jnp.int32, (256, 128), 1)
        indicator = jnp.where(dim // DIFF_DIM == col, 1.0, 0.0).astype(BF16)
        kn2[0:1, :] = _max_row_norm2(k_ref[...], kc_ref[...] if has_ctx else None, indicator)

    def run(exact):
        denoms = []
        outs = {}

        def exp_stage(u):
            t, p = divmod(u, n_pairs)
            tile = slice((p * DIFF_DIM // 128) * 128, (p * DIFF_DIM // 128 + 1) * 128)
            k_new = lambda: k_ref[:, tile]
            k_ctx = (lambda: kc_ref[:, tile]) if has_ctx else None
            q = q_ref[t * ATT_TQ:(t + 1) * ATT_TQ, tile]
            lo = p * DIFF_DIM - tile.start
            in_pair = (lane128 >= lo) & (lane128 < lo + DIFF_DIM)
            qm = jnp.where(in_pair, q, jnp.zeros_like(q))
            shift = _exact_shift(k_new, k_ctx, qm) if exact else _bound_shift(qm, kn2[0:1, p:p + 1])
            _exp_stage(e_bufs[u % 2], k_new, k_ctx, qm, shift, n_ctx)

        def value_stage(u):
            t, p = divmod(u, n_pairs)
            hh = p // 2
            o = _value_stage(e_bufs[u % 2], lambda: vt_ref[hh],
                             (lambda: vtc_ref[hh]) if has_ctx else None, n_ctx)
            denom = o[HEAD_V:HEAD_V + 1, :]
            denoms.append(denom)
            outs[u] = (o[0:HEAD_V, :], denom)
            if p % 2 == 1:
                (o0, l0), (o1, l1) = outs.pop(u - 1), outs.pop(u)
                o = o0 * (1.0 / l0) - o1 * (lam / l1)
                msq = jnp.sum(o * o, axis=0, keepdims=True) * (1.0 / HEAD_V)
                ot[hh * HEAD_V:(hh + 1) * HEAD_V, :] = o * lax.rsqrt(msq + EPS)
            if p == n_pairs - 1:
                o_ref[t * ATT_TQ:(t + 1) * ATT_TQ, :] = (ot[...].T * g_ref[...]) * (1.0 - lam_init)

        _run_pipeline(nsub * n_pairs, exp_stage, value_stage)
        return jnp.min(functools.reduce(jnp.minimum, denoms))

    denom_min = run(exact=False)

    @pl.when(jnp.logical_not(denom_min >= SAFE_DENOM))
    def _():
        run(exact=True)


def _diff_attn(q, k, vt, ctx, lw, *, nb, n, lam_init):
    nsub = 1
    tq = nsub * ATT_TQ
    npt = n // tq
    in_specs = [
        lw.spec("diff_lambda"),
        lw.spec("diff_g"),
        pl.BlockSpec((tq, 256), lambda b, j: (b * npt + j, 0)),
        pl.BlockSpec((n, 256), lambda b, j: (b, 0)),
        pl.BlockSpec((DIFF_HEADS, VT_ROWS, n), lambda b, j: (0, 0, b)),
    ]
    args = [lw["diff_lambda"], lw["diff_g"], q, k, vt]
    n_ctx = 0
    if ctx is not None:
        n_ctx = ctx[0].shape[0] // nb
        in_specs += [pl.BlockSpec((n_ctx, 256), lambda b, j: (b, 0)),
                     pl.BlockSpec((DIFF_HEADS, VT_ROWS, n_ctx), lambda b, j: (0, 0, b))]
        args += list(ctx)
    return pl.pallas_call(
        functools.partial(_diff_attn_kernel, has_ctx=ctx is not None, nsub=nsub, lam_init=lam_init),
        grid=(nb, npt),
        in_specs=in_specs,
        out_specs=pl.BlockSpec((tq, 256), lambda b, j: (b * npt + j, 0)),
        out_shape=jax.ShapeDtypeStruct((nb * n, 256), F32),
        scratch_shapes=_att_scratch(n + n_ctx),
        compiler_params=_params("arbitrary", "arbitrary"),
        name="diff_attn_ctx" if ctx is not None else "diff_attn",
    )(*args)


def _shift_rows(v, k):
    return pltpu.roll(v, (-k) % v.shape[0], 0)


def _scan_strided(a_ref, b_ref, h_ref, row0, carry, n_rows, reverse):
    sub = lax.broadcasted_iota(jnp.int32, (8, 128), 0)
    span = 8 * SCAN_RUN
    order = tuple(range(SCAN_RUN))[::-1] if reverse else tuple(range(SCAN_RUN))
    starts = tuple(range(0, n_rows, span))[::-1] if reverse else tuple(range(0, n_rows, span))
    carries = []
    for lt in range(a_ref.shape[0]):
        c_in = carry[:, lt * 128:(lt + 1) * 128]
        for start in starts:
            tile = lambda ref, g: ref[lt, pl.ds(row0 + start + g, 8, stride=SCAN_RUN), :]
            a = [tile(a_ref, g) for g in range(SCAN_RUN)]
            b = [tile(b_ref, g) for g in range(SCAN_RUN)]
            h = {order[0]: b[order[0]]}
            p = {order[0]: a[order[0]]}
            for prev, g in zip(order, order[1:]):
                h[g] = a[g] * h[prev] + b[g]
                p[g] = a[g] * p[prev]
            pi, hi = p[order[-1]], h[order[-1]]
            for s in (1, 2, 4):
                shift = 8 - s if reverse else s
                valid = (sub < 8 - s) if reverse else (sub >= s)
                pr, hr = pltpu.roll(pi, shift, 0), pltpu.roll(hi, shift, 0)
                hi = jnp.where(valid, pi * hr + hi, hi)
                pi = jnp.where(valid, pi * pr, pi)
            one = 7 if reverse else 1
            first = (sub == 7) if reverse else (sub == 0)
            pe = jnp.where(first, 1.0, pltpu.roll(pi, one, 0))
            he = jnp.where(first, 0.0, pltpu.roll(hi, one, 0))
            c = pe * c_in + he
            for g in range(SCAN_RUN):
                h_ref[lt, pl.ds(start + g, 8, stride=SCAN_RUN), :] = h[g] + p[g] * c
            last = 0 if reverse else 7
            c_in = pi[last:last + 1, :] * c_in + hi[last:last + 1, :]
        carries.append(c_in)
    return jnp.concatenate(carries, axis=1)


def _sigmoid(x):
    return 0.5 * jnp.tanh(0.5 * x) + 0.5


def _gelu_tanh(x):
    return x * (0.5 * (1.0 + jnp.tanh(math.sqrt(2.0 / math.pi) * (x + 0.044715 * (x * x * x)))))


def _lru_kernel(u_ref, h0_ref, cw_ref, cb_ref, wg_ref, bg_ref, lam_ref, y_ref, st_ref,
                xpad, a1s, b1s, a0c, b0c, hc, *, N, T):
    W = LRU_WIDTH
    nc = N // T
    tiles = [slice(lt * 128, (lt + 1) * 128) for lt in range(W // 128)]
    zeros = jnp.zeros((HALO, W), F32)
    xpad[0:HALO, :] = zeros
    xpad[N + HALO:N + 2 * HALO, :] = zeros

    def fill(j, carry):
        r0 = pl.multiple_of(j * T, T)
        xpad[pl.ds(r0 + HALO, T), :] = u_ref[pl.ds(r0, T), 0:W]
        return carry

    lax.fori_loop(0, nc, fill, 0)

    z = -lam_ref[...]
    sp = jnp.maximum(z, 0.0) + jnp.log1p(jnp.exp(-jnp.abs(z)))
    cw = cw_ref[...]
    cb = cb_ref[...]
    bg = bg_ref[...]

    def fwd(j, carry):
        r0 = pl.multiple_of(j * T, T)
        ext = xpad[pl.ds(r0, T + 2 * HALO), :]
        body = slice(HALO, HALO + T)
        xc = cb
        for tap in range(4):
            xc = xc + _shift_rows(ext, tap - 1)[body] * cw[tap:tap + 1]
        g = _sigmoid(_dot(xc.astype(BF16), wg_ref[...]) + bg)
        ab = []
        for d in range(2):
            r = g[:, d * W:(d + 1) * W]
            i = g[:, (2 + d) * W:(3 + d) * W]
            log_a = (-LRU_C * r) * sp[d:d + 1]
            a = jnp.exp(log_a)
            bt = (jnp.sqrt(1.0 - a * a) * i) * xc
            ab.append((a, bt))
        for lt, lanes in enumerate(tiles):
            a0c[lt] = ab[0][0][:, lanes]
            b0c[lt] = ab[0][1][:, lanes]
            a1s[lt, pl.ds(r0, T), :] = ab[1][0][:, lanes]
            b1s[lt, pl.ds(r0, T), :] = ab[1][1][:, lanes]
        carry = _scan_strided(a0c, b0c, hc, 0, carry, T, reverse=False)
        for lt, lanes in enumerate(tiles):
            y_ref[pl.ds(r0, T), lanes] = hc[lt]
        return carry

    cf = lax.fori_loop(0, nc, fwd, h0_ref[0, 0:1, :])

    def bwd(jj, carry):
        r0 = pl.multiple_of((nc - 1 - jj) * T, T)
        carry = _scan_strided(a1s, b1s, hc, r0, carry, T, reverse=True)
        for lt, lanes in enumerate(tiles):
            gb = u_ref[pl.ds(r0, T), W + lt * 128:W + (lt + 1) * 128]
            y_ref[pl.ds(r0, T), lanes] = (y_ref[pl.ds(r0, T), lanes] + hc[lt]) * _gelu_tanh(gb)
        return carry

    cbw = lax.fori_loop(0, nc, bwd, h0_ref[0, 1:2, :])
    st_ref[0, 0:1, :] = cf
    st_ref[0, 1:2, :] = cbw


def _lru(u, h0, h0_block, lw, *, nb, n):
    T = min(n, 256)
    W = LRU_WIDTH
    return pl.pallas_call(
        functools.partial(_lru_kernel, N=n, T=T),
        grid=(nb,),
        in_specs=[
            pl.BlockSpec((n, 2 * W), lambda b: (b, 0)),
            pl.BlockSpec((1, 2, W), lambda b: (h0_block(b), 0, 0)),
            lw.spec("conv_w"), lw.spec("conv_b"), lw.spec("w_gate"), lw.spec("b_gate"),
            lw.spec("lru_lambda"),
        ],
        out_specs=[
            pl.BlockSpec((n, W), lambda b: (b, 0)),
            pl.BlockSpec((1, 2, W), lambda b: (b, 0, 0)),
        ],
        out_shape=[
            jax.ShapeDtypeStruct((nb * n, W), F32),
            jax.ShapeDtypeStruct((nb, 2, W), F32),
        ],
        scratch_shapes=[
            pltpu.VMEM((n + 2 * HALO, W), F32),
            pltpu.VMEM((W // 128, n, 128), F32),
            pltpu.VMEM((W // 128, n, 128), F32),
            pltpu.VMEM((W // 128, T, 128), F32),
            pltpu.VMEM((W // 128, T, 128), F32),
            pltpu.VMEM((W // 128, T, 128), F32),
        ],
        compiler_params=_params("arbitrary"),
        name="rglru",
    )(u, h0, lw["conv_w"], lw["conv_b"], lw["w_gate"], lw["b_gate"], lw["lru_lambda"])


def _pool_kernel(u_ref, wp_ref, sc_ref, y_ref, xpad, *, N, T):
    W = GROUP_WIDTH
    nc = N // T
    zeros = jnp.zeros((HALO, W), F32)
    xpad[0:HALO, :] = zeros
    xpad[N + HALO:N + 2 * HALO, :] = zeros

    def fill(j, carry):
        r0 = pl.multiple_of(j * T, T)
        xpad[pl.ds(r0 + HALO, T), :] = u_ref[pl.ds(r0, T), :]
        return carry

    lax.fori_loop(0, nc, fill, 0)

    grp = lax.broadcasted_iota(jnp.int32, (1, W), 1) // POOL_CH
    half = jnp.where(grp == 0, 1, jnp.where(grp == 1, 2, jnp.where(grp == 2, 4, 8)))
    scale = sc_ref[...]

    def chunk(j, carry):
        r0 = pl.multiple_of(j * T, T)
        ext = xpad[pl.ds(r0, T + 2 * HALO), :]
        w2 = _shift_rows(ext, -1) + ext
        w4 = _shift_rows(w2, -1) + _shift_rows(w2, 1)
        w8 = _shift_rows(w4, -2) + _shift_rows(w4, 2)
        w16 = _shift_rows(w8, -4) + _shift_rows(w8, 4)
        ws = jnp.where(grp == 0, w2, jnp.where(grp == 1, w4, jnp.where(grp == 2, w8, w16)))
        body = slice(HALO, HALO + T)
        t = r0 + lax.broadcasted_iota(jnp.int32, (T, W), 0)
        cnt = (jnp.minimum(t + half, N) - jnp.maximum(t - half, 0)).astype(F32)
        d = ws[body] / cnt - ext[body]
        y_ref[pl.ds(r0, T), :] = _dot(d.astype(BF16), wp_ref[...]) * scale
        return carry

    lax.fori_loop(0, nc, chunk, 0)


def _pool(u, lw, *, nb, n):
    W = GROUP_WIDTH
    T = min(n, 256)
    return pl.pallas_call(
        functools.partial(_pool_kernel, N=n, T=T),
        grid=(nb,),
        in_specs=[pl.BlockSpec((n, W), lambda b: (b, 0)), lw.spec("w_pool"), lw.spec("pool_scale")],
        out_specs=pl.BlockSpec((n, W), lambda b: (b, 0)),
        out_shape=jax.ShapeDtypeStruct((nb * n, W), F32),
        scratch_shapes=[pltpu.VMEM((n + 2 * HALO, W), F32)],
        compiler_params=_params("arbitrary"),
        name="pool_mixer",
    )(u, lw["w_pool"], lw["pool_scale"])


def _mix_ffn_kernel(*refs, final):
    if final:
        (x_ref, ya_ref, yb_ref, yc_ref, yd_ref, mod_ref, g2_ref, wo_ref, wg_ref, wu_ref, wd_ref,
         gf_ref, o_ref) = refs
    else:
        (x_ref, ya_ref, yb_ref, yc_ref, yd_ref, mod_ref, g2_ref, wo_ref, wg_ref, wu_ref, wd_ref,
         o_ref) = refs
    mod = mod_ref[0]
    gate1 = mod[:, 2 * D_MODEL:3 * D_MODEL]
    sh2 = mod[:, 3 * D_MODEL:4 * D_MODEL]
    sc2 = mod[:, 4 * D_MODEL:5 * D_MODEL]
    gate2 = mod[:, 5 * D_MODEL:6 * D_MODEL]
    mix = None
    for i, y_ref in enumerate((ya_ref, yb_ref, yc_ref, yd_ref)):
        part = _dot(y_ref[...].astype(BF16), wo_ref[i * GROUP_WIDTH:(i + 1) * GROUP_WIDTH, :])
        mix = part if mix is None else mix + part
    x1 = x_ref[...] + gate1 * mix
    h = _rms_rows(x1, D_MODEL) * g2_ref[...]
    hb = (h * (1.0 + sc2) + sh2).astype(BF16)
    ff = None
    for lo, hi in FF_CHUNKS:
        g = _dot(hb, wg_ref[:, lo:hi])
        up = _dot(hb, wu_ref[:, lo:hi])
        act = ((g * jax.nn.sigmoid(g)) * up).astype(BF16)
        part = _dot(act, wd_ref[lo:hi, :])
        ff = part if ff is None else ff + part
    x2 = x1 + gate2 * ff
    if final:
        x2 = _rms_rows(x2, D_MODEL) * gf_ref[...]
    o_ref[...] = x2


def _mix_ffn(x, ys, mod, lw, gf, *, nb, n, final):
    T = nb * n
    tm = TOKEN_TILE
    npt = n // tm

    def tok(width):
        return pl.BlockSpec((tm, width), lambda i: (i, 0))

    wnames = ("g2", "w_out", "w_gate_ff", "w_up_ff", "w_down")
    in_specs = [tok(D_MODEL), tok(256), tok(256), tok(256), tok(256),
                mod.spec(lambda i: i // npt)] + [lw.spec(nm) for nm in wnames]
    args = [x, *ys, mod.table] + [lw[nm] for nm in wnames]
    if final:
        in_specs.append(_resident((1, D_MODEL)))
        args.append(gf)
    return pl.pallas_call(
        functools.partial(_mix_ffn_kernel, final=final),
        grid=(T // tm,),
        in_specs=in_specs,
        out_specs=tok(D_MODEL),
        out_shape=jax.ShapeDtypeStruct((T, D_MODEL), F32),
        compiler_params=_params("arbitrary"),
        name="mix_ffn_final" if final else "mix_ffn",
    )(*args)


def _block_diag(w):
    L, G, c, e = w.shape
    return jnp.einsum('lgce,gh->lgche', w, jnp.eye(G, dtype=w.dtype)).reshape(L, G * c, G * e)


def _rot_cols(w):
    return jnp.concatenate([-w[..., 16:32], w[..., 0:16]], axis=-1)


def _stack_weights(p):
    w_in = p["w_in"]
    o1 = MLA_Q_RANK
    o2 = o1 + MLA_KV_RANK
    o3 = o2 + MLA_ROPE
    c_q, c_kv, k_r, rest = w_in[..., :o1], w_in[..., o1:o2], w_in[..., o2:o3], w_in[..., o3:]
    z = lambda n: jnp.zeros((DEPTH, D_MODEL, n), F32)
    w_in_eff = jnp.concatenate([c_q, k_r, z(32), c_kv, z(64), _rot_cols(k_r), z(32), rest], axis=-1)

    w_uq = p["mla_w_uq"]
    qd = MLA_NOPE + MLA_ROPE
    wq_parts, wqr_parts = [], []
    zq = lambda n: jnp.zeros((DEPTH, MLA_Q_RANK, n), F32)
    for h in range(MLA_HEADS):
        wh = w_uq[..., h * qd:(h + 1) * qd]
        wq_parts += [wh, zq(MLA_SLOT - qd)]
        wqr_parts += [zq(MLA_NOPE), _rot_cols(wh[..., MLA_NOPE:]), zq(MLA_SLOT - qd)]
    pad_rows = lambda w: jnp.pad(w, ((0, 0), (0, 256 - MLA_Q_RANK), (0, 0)))
    w_ukv = p["mla_w_ukv"]
    wk_parts, wv_parts = [], []
    zk = jnp.zeros((DEPTH, MLA_KV_RANK, MLA_SLOT - MLA_NOPE), F32)
    for h in range(MLA_HEADS):
        base = h * (MLA_NOPE + MLA_V)
        wk_parts += [w_ukv[..., base:base + MLA_NOPE], zk]
        wv_parts.append(w_ukv[..., base + MLA_NOPE:base + MLA_NOPE + MLA_V])

    w_r, w_i, b_r, b_i = p["lru_w_r"], p["lru_w_i"], p["lru_b_r"], p["lru_b_i"]
    w_gate = jnp.concatenate([_block_diag(w_r[:, 0]), _block_diag(w_r[:, 1]),
                              _block_diag(w_i[:, 0]), _block_diag(w_i[:, 1])], axis=-1)
    b_gate = jnp.concatenate([b_r[:, 0], b_r[:, 1], b_i[:, 0], b_i[:, 1]], axis=-1)
    w_gu = p["w_gu"]
    row = lambda v: v[:, None, :]
    return {
        "g1": row(p["norm1_g"]),
        "g2": row(p["norm2_g"]),
        "w_in": w_in_eff.astype(BF16),
        "gq": row(jnp.pad(p["mla_q_norm_g"], ((0, 0), (0, 256 - MLA_Q_RANK)))),
        "gkv": row(p["mla_kv_norm_g"]),
        "wq": pad_rows(jnp.concatenate(wq_parts, axis=-1)).astype(BF16),
        "wqr": pad_rows(jnp.concatenate(wqr_parts, axis=-1)).astype(BF16),
        "wkv": jnp.concatenate(wk_parts + wv_parts, axis=-1).astype(BF16),
        "conv_w": p["lru_conv_w"],
        "conv_b": row(p["lru_conv_b"]),
        "w_gate": w_gate.astype(BF16),
        "b_gate": row(b_gate),
        "lru_lambda": p["lru_lambda"],
        "w_pool": _block_diag(p["pool_w"]).astype(BF16),
        "pool_scale": row(p["pool_scale"]),
        "diff_lambda": p["diff_lambda"],
        "diff_g": row(jnp.tile(p["diff_norm_g"], (1, DIFF_HEADS))),
        "w_out": p["w_out"].astype(BF16),
        "w_gate_ff": w_gu[..., :FF_HIDDEN].astype(BF16),
        "w_up_ff": w_gu[..., FF_HIDDEN:].astype(BF16),
        "w_down": p["w_down"].astype(BF16),
    }


def _rope_tables(n, positional):
    quarter = MLA_ROPE // 4
    if positional:
        t = jnp.arange(n)
        row = (t // GRID_W).astype(F32)
        col = (t % GRID_W).astype(F32)
        inv = ROPE_BASE ** (-jnp.arange(quarter, dtype=F32) / quarter)
        ang = jnp.concatenate([row[:, None] * inv, col[:, None] * inv], axis=-1)
        cos, sin = jnp.cos(ang), jnp.sin(ang)
    else:
        cos, sin = jnp.ones((n, 16), F32), jnp.zeros((n, 16), F32)
    scale = LOG2E / math.sqrt(MLA_NOPE + MLA_ROPE)
    place = np.zeros((32, TAB_WIDTH), np.float32)
    offset = np.zeros((1, TAB_WIDTH), np.float32)
    offset[0, 0:64] = offset[0, 96:128] = scale
    for i in range(16):
        for half in (64, 80):
            place[i, half + i] = scale
            place[16 + i, 128 + half + i] = scale
            place[i, 256 + half + i] = 1.0
            place[16 + i, 384 + half + i] = 1.0
        for grp in range(8):
            place[i, 512 + 32 * grp + i] = place[i, 512 + 32 * grp + 16 + i] = 1.0
            place[16 + i, 768 + 32 * grp + i] = -1.0
            place[16 + i, 1024 + 32 * grp + 16 + i] = 1.0
    return jnp.dot(jnp.concatenate([cos, sin], axis=1), place, precision=lax.Precision.HIGHEST) + offset


def _layer(x, mod, lw, tabs, layer_idx, ctx, gf, *, nb, n, final):
    emit_cache = ctx is None
    tok_nb, tok_n = (1, nb * n) if mod.shared else (nb, n)
    outs = _inproj(x, mod, lw, tabs, nb=tok_nb, n=tok_n, emit_cache=emit_cache)
    q, k, vt, u_lru, u_pool, dq, dk, dvt = outs[:8]
    lam_init = 0.8 - 0.6 * math.exp(-0.3 * layer_idx)
    if ctx is None:
        h0 = jnp.zeros((1, 2, LRU_WIDTH), F32)
        h0_block = lambda b: 0
        mla_ctx = diff_ctx = None
    else:
        ckv, kr_pad, cdk, cdv, h0 = ctx
        p = ckv.shape[0] // (nb * DEPTH)
        h0_block = lambda b: b * DEPTH + layer_idx
        kc, vtc, dkc, dvtc = _ctx_prep(ckv, kr_pad, cdk, cdv, lw, nb=nb, p=p)
        mla_ctx = (kc, vtc)
        diff_ctx = (dkc, dvtc)
    y_mla = _mla_attn(q, k, vt, mla_ctx, nb=nb, n=n)
    y_lru, st = _lru(u_lru, h0, h0_block, lw, nb=nb, n=n)
    y_pool = _pool(u_pool, lw, nb=nb, n=n)
    y_diff = _diff_attn(dq, dk, dvt, diff_ctx, lw, nb=nb, n=n, lam_init=lam_init)
    x2 = _mix_ffn(x, (y_mla, y_lru, y_pool, y_diff), mod, lw, gf, nb=tok_nb, n=tok_n, final=final)
    cache = (outs[8], outs[9][:, 64:96], outs[10], outs[11], st) if emit_cache else None
    return x2, cache


def kernel(x_prompt, x_sample, cache_mla_ckv, cache_mla_krope, cache_diff_k, cache_diff_v, state_lru,
           c, c_ctx, w_ada, b_ada, norm1_g, norm2_g, w_in, mla_q_norm_g, mla_w_uq, mla_kv_norm_g,
           mla_w_ukv, lru_conv_w, lru_conv_b, lru_w_r, lru_b_r, lru_w_i, lru_b_i, lru_lambda, pool_w,
           pool_scale, diff_lambda, diff_norm_g, w_out, w_gu, w_down, final_norm_g):
    p = {
        "norm1_g": norm1_g, "norm2_g": norm2_g, "w_in": w_in, "mla_q_norm_g": mla_q_norm_g,
        "mla_w_uq": mla_w_uq, "mla_kv_norm_g": mla_kv_norm_g, "mla_w_ukv": mla_w_ukv,
        "lru_conv_w": lru_conv_w, "lru_conv_b": lru_conv_b, "lru_w_r": lru_w_r, "lru_b_r": lru_b_r,
        "lru_w_i": lru_w_i, "lru_b_i": lru_b_i, "lru_lambda": lru_lambda, "pool_w": pool_w,
        "pool_scale": pool_scale, "diff_lambda": diff_lambda, "diff_norm_g": diff_norm_g,
        "w_out": w_out, "w_gu": w_gu, "w_down": w_down,
    }
    Bp, Np, _ = x_prompt.shape
    Bs, Ns, _ = x_sample.shape
    P = cache_mla_ckv.shape[2]

    cond_all = jnp.concatenate([c, c_ctx[None, :], jnp.zeros((MOD_ROWS - Bs - 1, D_MODEL), F32)], axis=0)
    mod_table = _ada(cond_all, w_ada, b_ada).reshape(DEPTH * MOD_ROWS, 1, 6 * D_MODEL)
    tabs_p = _rope_tables(Bp * Np, positional=False)
    tabs_s = _rope_tables(Ns, positional=True)
    kr_pad = jnp.pad(cache_mla_krope, ((0, 0), (0, 0), (0, 0), (MLA_NOPE, MLA_SLOT - MLA_NOPE - MLA_ROPE)))
    flat = lambda a, w: a.reshape(Bs * DEPTH * P, w)
    ctx = (flat(cache_mla_ckv, MLA_KV_RANK), flat(kr_pad, MLA_SLOT), flat(cache_diff_k, 256),
           flat(cache_diff_v, 256), state_lru.reshape(Bs * DEPTH, 2, LRU_WIDTH))
    gf = final_norm_g[None, :]
    stacked = _stack_weights(p)

    xp = x_prompt.reshape(Bp * Np, D_MODEL)
    xs = x_sample.reshape(Bs * Ns, D_MODEL)
    caches = []
    for l in range(DEPTH):
        lw = _LayerWeights(stacked, l)
        final = l == DEPTH - 1
        mod_p = _Mod(mod_table, l * MOD_ROWS + Bs, shared=True)
        mod_s = _Mod(mod_table, l * MOD_ROWS, shared=False)
        xp, cache = _layer(xp, mod_p, lw, tabs_p, l, None, gf, nb=Bp, n=Np, final=final)
        caches.append(cache)
        xs, _ = _layer(xs, mod_s, lw, tabs_s, l, ctx, gf, nb=Bs, n=Ns, final=final)

    stack = lambda i, w: jnp.stack([cc[i].reshape(Bp, Np, w) for cc in caches], axis=1)
    new_mla_ckv = stack(0, MLA_KV_RANK)
    new_mla_krope = stack(1, MLA_ROPE)
    new_diff_k = stack(2, 256).reshape(Bp, DEPTH, Np, DIFF_HEADS, 2, DIFF_DIM)
    new_diff_v = stack(3, 256).reshape(Bp, DEPTH, Np, DIFF_HEADS, 2 * DIFF_DIM)
    new_state_lru = jnp.stack([cc[4] for cc in caches], axis=1)
    return (xp.reshape(Bp, Np, D_MODEL), xs.reshape(Bs, Ns, D_MODEL),
            new_mla_ckv, new_mla_krope, new_diff_k, new_diff_v, new_state_lru)
```

```python
import functools
import math

import jax
import jax.numpy as jnp
import numpy as np
from jax import lax
from jax.experimental import pallas as pl
from jax.experimental.pallas import tpu as pltpu

F32 = jnp.float32
BF16 = jnp.bfloat16

D_MODEL = 1024
DEPTH = 2
GRID_W = 64
GROUP_WIDTH = 256
MLA_HEADS = 4
MLA_NOPE = 64
MLA_ROPE = 32
MLA_V = 64
MLA_Q_RANK = 192
MLA_KV_RANK = 128
MLA_SLOT = 128
LRU_WIDTH = 256
LRU_C = 8.0
POOL_WINDOWS = (2, 4, 8, 16)
POOL_CH = 64
DIFF_HEADS = 4
DIFF_DIM = 32
HEAD_V = 64
FF_HIDDEN = 2816
FF_CHUNKS = ((0, 1536), (1536, 2816))
ROPE_BASE = 10000.0
EPS = 1e-6
IN_EFF = 2048
HALO = 8
SCAN_RUN = 4
VT_ROWS = 80
ATT_TQ = 256
TOKEN_TILE = 512
TAB_WIDTH = 4 * 128 + 3 * 256
MOD_ROWS = 16
LOG2E = math.log2(math.e)

VMEM_LIMIT_BYTES = 56 * 1024 * 1024

_NT = (((1,), (1,)), ((), ()))


def _params(*sem):
    return pltpu.CompilerParams(dimension_semantics=sem, vmem_limit_bytes=VMEM_LIMIT_BYTES)


def _resident(shape):
    zeros = (0,) * len(shape)
    return pl.BlockSpec(shape, lambda *_: zeros, pipeline_mode=pl.Buffered(1))


def _dot(a, b):
    return jnp.dot(a, b, preferred_element_type=F32)


def _dot_nt(a, b):
    return lax.dot_general(a, b, _NT, preferred_element_type=F32)


def _rms_rows(x, width):
    ms = jnp.sum(x * x, axis=-1, keepdims=True) * (1.0 / width)
    return x * lax.rsqrt(ms + EPS)


def _store_vt(vt_ref, v):
    vt = v.T
    rows = v.shape[0]
    pad = VT_ROWS - HEAD_V
    ones_row = jnp.where(lax.broadcasted_iota(jnp.int32, (pad, rows), 0) == 0, 1.0, 0.0).astype(BF16)
    for hh in range(vt_ref.shape[0]):
        vt_ref[hh, 0:HEAD_V, :] = vt[hh * HEAD_V:(hh + 1) * HEAD_V, :].astype(BF16)
        vt_ref[hh, HEAD_V:VT_ROWS, :] = ones_row


class _Mod:
    def __init__(self, table, row0, shared):
        self.table, self.row0, self.shared = table, row0, shared

    def spec(self, batch_of):
        row0 = self.row0
        if self.shared:
            return pl.BlockSpec((1, 1, 6 * D_MODEL), lambda *g: (row0, 0, 0))
        return pl.BlockSpec((1, 1, 6 * D_MODEL), lambda *g: (row0 + batch_of(*g), 0, 0))


class _LayerWeights:
    def __init__(self, stacked, layer):
        self.stacked, self.layer = stacked, layer

    def __getitem__(self, name):
        return self.stacked[name]

    def spec(self, name):
        layer = self.layer
        _, rows, cols = self.stacked[name].shape
        return pl.BlockSpec((None, rows, cols), lambda *_: (layer, 0, 0), pipeline_mode=pl.Buffered(1))


def _ada_kernel(cond_ref, w_ref, b_ref, out_ref):
    c = cond_ref[...]
    s = c * jax.nn.sigmoid(c)
    out_ref[0] = _dot(s.astype(BF16), w_ref[0].astype(BF16)) + b_ref[0]


def _ada(cond_all, w_ada, b_ada):
    rows = cond_all.shape[0]
    tn = 1536
    return pl.pallas_call(
        _ada_kernel,
        grid=(DEPTH, 6 * D_MODEL // tn),
        in_specs=[
            pl.BlockSpec((rows, D_MODEL), lambda l, j: (0, 0)),
            pl.BlockSpec((1, D_MODEL, tn), lambda l, j: (l, 0, j)),
            pl.BlockSpec((1, 1, tn), lambda l, j: (l, 0, j)),
        ],
        out_specs=pl.BlockSpec((1, rows, tn), lambda l, j: (l, 0, j)),
        out_shape=jax.ShapeDtypeStruct((DEPTH, rows, 6 * D_MODEL), F32),
        compiler_params=_params("arbitrary", "arbitrary"),
        name="ada_mod",
    )(cond_all, w_ada, b_ada.reshape(DEPTH, 1, 6 * D_MODEL))


def _inproj_kernel(x_ref, mod_ref, g1_ref, win_ref, gq_ref, gkv_ref, wq_ref, wqr_ref, wkv_ref, tab_ref,
                   q_out, k_out, vt_out, lru_out, pool_out, dq_out, dk_out, dvt_out, *cache_outs):
    cosq_ref, sinq_ref, cosk_ref, sink_ref = (tab_ref.at[:, i * 128:(i + 1) * 128] for i in range(4))
    cosd_ref, sina_ref, sinb_ref = (tab_ref.at[:, 512 + i * 256:768 + i * 256] for i in range(3))
    x = x_ref[...]
    mod = mod_ref[0]
    sh1 = mod[:, 0:D_MODEL]
    sc1 = mod[:, D_MODEL:2 * D_MODEL]
    h = _rms_rows(x, D_MODEL) * g1_ref[...]
    hb = (h * (1.0 + sc1) + sh1).astype(BF16)

    u_mla = _dot(hb, win_ref[:, 0:512])
    u_pd = _dot(hb, win_ref[:, 1024:1536])
    u_kv = _dot(hb, win_ref[:, 1536:2048])
    t01 = u_mla[:, 0:256]
    lane = lax.broadcasted_iota(jnp.int32, (1, 256), 1)
    cq = jnp.where(lane < MLA_Q_RANK, t01, 0.0)
    cqn = (_rms_rows(cq, MLA_Q_RANK) * gq_ref[...]).astype(BF16)
    qa = _dot(cqn, wq_ref[...])
    qr = _dot(cqn, wqr_ref[...])
    cosq = cosq_ref[...]
    sinq = sinq_ref[...]
    ckv = u_mla[:, 256:384]
    lat = _rms_rows(ckv, MLA_KV_RANK) * gkv_ref[...]
    latb = lat.astype(BF16)
    kkv = _dot(latb, wkv_ref[...])
    kk = kkv[:, 0:MLA_HEADS * MLA_SLOT]
    _store_vt(vt_out, kkv[:, MLA_HEADS * MLA_SLOT:])
    t1 = t01[:, 128:256]
    t3 = u_mla[:, 384:512]
    kro = t1 * cosk_ref[...] + t3 * sink_ref[...]
    for hh in range(MLA_HEADS):
        sl = slice(hh * MLA_SLOT, (hh + 1) * MLA_SLOT)
        q_out[hh] = (qa[:, sl] * cosq + qr[:, sl] * sinq).astype(q_out.dtype)
        k_out[hh] = (kk[:, sl] + kro).astype(k_out.dtype)

    lru_out[...] = _dot(hb, win_ref[:, 512:1024])
    pool_out[...] = u_pd[:, 0:256]

    cosd = cosd_ref[...]
    sina = sina_ref[...]
    sinb = sinb_ref[...]

    def rope(t):
        return t * cosd + pltpu.roll(t, 256 - 16, 1) * sina + pltpu.roll(t, 16, 1) * sinb

    dq = u_pd[:, 256:512]
    dk = u_kv[:, 0:256]
    dv = u_kv[:, 256:512]
    dq_out[...] = (rope(dq) * (LOG2E / math.sqrt(DIFF_DIM))).astype(dq_out.dtype)
    dk_out[...] = rope(dk).astype(dk_out.dtype)
    _store_vt(dvt_out, dv)

    if cache_outs:
        lat_out, kr_out, dk_raw_out, dv_raw_out = cache_outs
        lat_out[...] = lat
        kr_out[...] = t1
        dk_raw_out[...] = dk
        dv_raw_out[...] = dv


def _inproj(x, mod, lw, tabs, *, nb, n, emit_cache):
    T = nb * n
    tm = TOKEN_TILE
    npt = n // tm
    row_blk = lambda j, b: b * npt + j

    def tok(width):
        return pl.BlockSpec((tm, width), lambda j, b: (row_blk(j, b), 0))

    def tab(width):
        return pl.BlockSpec((tm, width), lambda j, b: (j, 0))

    head = pl.BlockSpec((MLA_HEADS, tm, MLA_SLOT), lambda j, b: (0, row_blk(j, b), 0))
    vt_spec = pl.BlockSpec((MLA_HEADS, VT_ROWS, tm), lambda j, b: (0, 0, row_blk(j, b)))
    wnames = ("g1", "w_in", "gq", "gkv", "wq", "wqr", "wkv")
    in_specs = [tok(D_MODEL), mod.spec(lambda j, b: b)] + [lw.spec(nm) for nm in wnames] + [
        tab(TAB_WIDTH)]
    out_specs = [head, head, vt_spec, tok(512), tok(256), tok(256), tok(256), vt_spec]
    vt_shape = jax.ShapeDtypeStruct((MLA_HEADS, VT_ROWS, T), BF16)
    out_shape = [
        jax.ShapeDtypeStruct((MLA_HEADS, T, MLA_SLOT), BF16),
        jax.ShapeDtypeStruct((MLA_HEADS, T, MLA_SLOT), BF16),
        vt_shape,
        jax.ShapeDtypeStruct((T, 512), F32),
        jax.ShapeDtypeStruct((T, 256), F32),
        jax.ShapeDtypeStruct((T, 256), BF16),
        jax.ShapeDtypeStruct((T, 256), BF16),
        vt_shape,
    ]
    if emit_cache:
        out_specs += [tok(128), tok(128), tok(256), tok(256)]
        out_shape += [jax.ShapeDtypeStruct((T, 128), F32), jax.ShapeDtypeStruct((T, 128), F32),
                      jax.ShapeDtypeStruct((T, 256), F32), jax.ShapeDtypeStruct((T, 256), F32)]
    return pl.pallas_call(
        _inproj_kernel,
        grid=(npt, nb),
        in_specs=in_specs,
        out_specs=out_specs,
        out_shape=out_shape,
        compiler_params=_params("arbitrary", "arbitrary"),
        name="inproj_cache" if emit_cache else "inproj",
    )(x, mod.table, *[lw[nm] for nm in wnames], tabs)


def _ctx_prep_kernel(ckv_ref, kr_ref, dk_ref, dv_ref, wkv_ref, k_out, vt_out, dk_out, dvt_out):
    latb = ckv_ref[...].astype(BF16)
    kkv = _dot(latb, wkv_ref[...])
    kk = kkv[:, 0:MLA_HEADS * MLA_SLOT]
    kr = kr_ref[...]
    for hh in range(MLA_HEADS):
        k_out[hh] = (kk[:, hh * MLA_SLOT:(hh + 1) * MLA_SLOT] + kr).astype(k_out.dtype)
    _store_vt(vt_out, kkv[:, MLA_HEADS * MLA_SLOT:])
    dk_out[...] = dk_ref[...].astype(dk_out.dtype)
    _store_vt(dvt_out, dv_ref[...])


def _ctx_prep(ckv, kr_pad, cdk, cdv, lw, *, nb, p):
    T = nb * p
    layer = lw.layer
    cache_row = lambda w: pl.BlockSpec((p, w), lambda b: (b * DEPTH + layer, 0))
    row = lambda w: pl.BlockSpec((p, w), lambda b: (b, 0))
    vt_spec = pl.BlockSpec((MLA_HEADS, VT_ROWS, p), lambda b: (0, 0, b))
    vt_shape = jax.ShapeDtypeStruct((MLA_HEADS, VT_ROWS, T), BF16)
    return pl.pallas_call(
        _ctx_prep_kernel,
        grid=(nb,),
        in_specs=[cache_row(128), cache_row(128), cache_row(256), cache_row(256),
                  lw.spec("wkv")],
        out_specs=[pl.BlockSpec((MLA_HEADS, p, MLA_SLOT), lambda b: (0, b, 0)), vt_spec, row(256), vt_spec],
        out_shape=[jax.ShapeDtypeStruct((MLA_HEADS, T, MLA_SLOT), BF16), vt_shape,
                   jax.ShapeDtypeStruct((T, 256), BF16), vt_shape],
        compiler_params=_params("arbitrary"),
        name="ctx_prep",
    )(ckv, kr_pad, cdk, cdv, lw["wkv"])


SAFE_DENOM = 2.0 ** -60
BOUND_SLACK = 1.02


def _scores(k_new, k_ctx, q):
    sn = _dot_nt(k_new(), q)
    sc = _dot_nt(k_ctx(), q) if k_ctx is not None else None
    return sn, sc


def _exact_shift(k_new, k_ctx, q):
    sn, sc = _scores(k_new, k_ctx, q)
    m = jnp.max(sn, axis=0, keepdims=True)
    if sc is not None:
        m = jnp.maximum(m, jnp.max(sc, axis=0, keepdims=True))
    return m


def _bound_shift(q, key_norm2):
    qf = q.astype(F32)
    ones = jnp.ones((8, q.shape[1]), BF16)
    q_norm2 = _dot_nt(ones, (qf * qf).astype(BF16))[0:1, :]
    return jnp.sqrt(q_norm2 * key_norm2) * BOUND_SLACK


def _max_row_norm2(k_new, k_ctx, col_sum):
    def one(k):
        kf = k.astype(F32)
        return jnp.max(_dot((kf * kf).astype(BF16), col_sum), axis=0, keepdims=True)
    m = one(k_new)
    if k_ctx is not None:
        m = jnp.maximum(m, one(k_ctx))
    return m * BOUND_SLACK


def _exp_stage(e_buf, k_new, k_ctx, q, shift, n_ctx):
    sn, sc = _scores(k_new, k_ctx, q)
    e_buf[n_ctx:, :] = jnp.exp2(sn - shift).astype(BF16)
    if sc is not None:
        e_buf[0:n_ctx, :] = jnp.exp2(sc - shift).astype(BF16)


def _value_stage(e_buf, vt_new, vt_ctx, n_ctx):
    o = _dot(vt_new(), e_buf[n_ctx:, :])
    if vt_ctx is not None:
        o = o + _dot(vt_ctx(), e_buf[0:n_ctx, :])
    return o


def _run_pipeline(n_maps, exp_stage, value_stage):
    exp_stage(0)
    for u in range(n_maps):
        if u + 1 < n_maps:
            exp_stage(u + 1)
        value_stage(u)


def _att_scratch(nk):
    return [pltpu.VMEM((8, 128), F32),
            pltpu.VMEM((MLA_HEADS * HEAD_V, ATT_TQ), F32),
            pltpu.VMEM((nk, ATT_TQ), BF16), pltpu.VMEM((nk, ATT_TQ), BF16)]


def _att_nsub(n):
    return 2 if n % (2 * ATT_TQ) == 0 else 1


def _mla_attn_kernel(*refs, has_ctx, nsub):
    if has_ctx:
        q_ref, k_ref, vt_ref, kc_ref, vtc_ref, o_ref, kn2, ot, e0, e1 = refs
        n_ctx = kc_ref.shape[1]
    else:
        q_ref, k_ref, vt_ref, o_ref, kn2, ot, e0, e1 = refs
        n_ctx = 0
    e_bufs = (e0, e1)

    @pl.when(pl.program_id(1) == 0)
    def _():
        ones = jnp.ones((MLA_SLOT, 128), BF16)
        for hh in range(MLA_HEADS):
            kn2[hh:hh + 1, :] = _max_row_norm2(k_ref[hh], kc_ref[hh] if has_ctx else None, ones)

    def run(exact):
        denoms = []

        def exp_stage(u):
            t, hh = divmod(u, MLA_HEADS)
            q = q_ref[hh, t * ATT_TQ:(t + 1) * ATT_TQ, :]
            k_new = lambda: k_ref[hh]
            k_ctx = (lambda: kc_ref[hh]) if has_ctx else None
            shift = _exact_shift(k_new, k_ctx, q) if exact else _bound_shift(q, kn2[hh:hh + 1, 0:1])
            _exp_stage(e_bufs[u % 2], k_new, k_ctx, q, shift, n_ctx)

        def value_stage(u):
            t, hh = divmod(u, MLA_HEADS)
            o = _value_stage(e_bufs[u % 2], lambda: vt_ref[hh],
                             (lambda: vtc_ref[hh]) if has_ctx else None, n_ctx)
            denom = o[HEAD_V:HEAD_V + 1, :]
            denoms.append(denom)
            ot[hh * HEAD_V:(hh + 1) * HEAD_V, :] = o[0:HEAD_V, :] * (1.0 / denom)
            if hh == MLA_HEADS - 1:
                o_ref[t * ATT_TQ:(t + 1) * ATT_TQ, :] = ot[...].T

        _run_pipeline(nsub * MLA_HEADS, exp_stage, value_stage)
        return jnp.min(functools.reduce(jnp.minimum, denoms))

    denom_min = run(exact=False)

    @pl.when(jnp.logical_not(denom_min >= SAFE_DENOM))
    def _():
        run(exact=True)


def _mla_attn(q, k, vt, ctx, *, nb, n):
    nsub = _att_nsub(n)
    tq = nsub * ATT_TQ
    npt = n // tq
    H, S = MLA_HEADS, MLA_SLOT
    in_specs = [
        pl.BlockSpec((H, tq, S), lambda b, j: (0, b * npt + j, 0)),
        pl.BlockSpec((H, n, S), lambda b, j: (0, b, 0)),
        pl.BlockSpec((H, VT_ROWS, n), lambda b, j: (0, 0, b)),
    ]
    args = [q, k, vt]
    n_ctx = 0
    if ctx is not None:
        n_ctx = ctx[0].shape[1] // nb
        in_specs += [
            pl.BlockSpec((H, n_ctx, S), lambda b, j: (0, b, 0)),
            pl.BlockSpec((H, VT_ROWS, n_ctx), lambda b, j: (0, 0, b)),
        ]
        args += list(ctx)
    return pl.pallas_call(
        functools.partial(_mla_attn_kernel, has_ctx=ctx is not None, nsub=nsub),
        grid=(nb, npt),
        in_specs=in_specs,
        out_specs=pl.BlockSpec((tq, 256), lambda b, j: (b * npt + j, 0)),
        out_shape=jax.ShapeDtypeStruct((nb * n, 256), F32),
        scratch_shapes=_att_scratch(n + n_ctx),
        compiler_params=_params("arbitrary", "arbitrary"),
        name="mla_attn_ctx" if ctx is not None else "mla_attn",
    )(*args)


def _diff_attn_kernel(*refs, has_ctx, nsub, lam_init):
    if has_ctx:
        lv_ref, g_ref, q_ref, k_ref, vt_ref, kc_ref, vtc_ref, o_ref, kn2, ot, e0, e1 = refs
        n_ctx = kc_ref.shape[0]
    else:
        lv_ref, g_ref, q_ref, k_ref, vt_ref, o_ref, kn2, ot, e0, e1 = refs
        n_ctx = 0
    e_bufs = (e0, e1)
    lv = lv_ref[...]
    lam = (jnp.exp(jnp.sum(lv[0:1] * lv[1:2], axis=-1, keepdims=True))
           - jnp.exp(jnp.sum(lv[2:3] * lv[3:4], axis=-1, keepdims=True)) + lam_init)
    lane128 = lax.broadcasted_iota(jnp.int32, (1, 128), 1)
    n_pairs = 2 * DIFF_HEADS

    @pl.when(pl.program_id(1) == 0)
    def _():
        dim = lax.broadcasted_iota(jnp.int32, (256, 128), 0)
        col = lax.broadcasted_iota(jnp.int32, (256, 128), 1)
        indicator = jnp.where(dim // DIFF_DIM == col, 1.0, 0.0).astype(BF16)
        kn2[0:1, :] = _max_row_norm2(k_ref[...], kc_ref[...] if has_ctx else None, indicator)

    def run(exact):
        denoms = []
        outs = {}

        def exp_stage(u):
            t, p = divmod(u, n_pairs)
            tile = slice((p * DIFF_DIM // 128) * 128, (p * DIFF_DIM // 128 + 1) * 128)
            k_new = lambda: k_ref[:, tile]
            k_ctx = (lambda: kc_ref[:, tile]) if has_ctx else None
            q = q_ref[t * ATT_TQ:(t + 1) * ATT_TQ, tile]
            lo = p * DIFF_DIM - tile.start
            in_pair = (lane128 >= lo) & (lane128 < lo + DIFF_DIM)
            qm = jnp.where(in_pair, q, jnp.zeros_like(q))
            shift = _exact_shift(k_new, k_ctx, qm) if exact else _bound_shift(qm, kn2[0:1, p:p + 1])
            _exp_stage(e_bufs[u % 2], k_new, k_ctx, qm, shift, n_ctx)

        def value_stage(u):
            t, p = divmod(u, n_pairs)
            hh = p // 2
            o = _value_stage(e_bufs[u % 2], lambda: vt_ref[hh],
                             (lambda: vtc_ref[hh]) if has_ctx else None, n_ctx)
            denom = o[HEAD_V:HEAD_V + 1, :]
            denoms.append(denom)
            outs[u] = (o[0:HEAD_V, :], denom)
            if p % 2 == 1:
                (o0, l0), (o1, l1) = outs.pop(u - 1), outs.pop(u)
                o = o0 * (1.0 / l0) - o1 * (lam / l1)
                msq = jnp.sum(o * o, axis=0, keepdims=True) * (1.0 / HEAD_V)
                ot[hh * HEAD_V:(hh + 1) * HEAD_V, :] = o * lax.rsqrt(msq + EPS)
            if p == n_pairs - 1:
                o_ref[t * ATT_TQ:(t + 1) * ATT_TQ, :] = (ot[...].T * g_ref[...]) * (1.0 - lam_init)

        _run_pipeline(nsub * n_pairs, exp_stage, value_stage)
        return jnp.min(functools.reduce(jnp.minimum, denoms))

    denom_min = run(exact=False)

    @pl.when(jnp.logical_not(denom_min >= SAFE_DENOM))
    def _():
        run(exact=True)


def _diff_attn(q, k, vt, ctx, lw, *, nb, n, lam_init):
    nsub = 1
    tq = nsub * ATT_TQ
    npt = n // tq
    in_specs = [
        lw.spec("diff_lambda"),
        lw.spec("diff_g"),
        pl.BlockSpec((tq, 256), lambda b, j: (b * npt + j, 0)),
        pl.BlockSpec((n, 256), lambda b, j: (b, 0)),
        pl.BlockSpec((DIFF_HEADS, VT_ROWS, n), lambda b, j: (0, 0, b)),
    ]
    args = [lw["diff_lambda"], lw["diff_g"], q, k, vt]
    n_ctx = 0
    if ctx is not None:
        n_ctx = ctx[0].shape[0] // nb
        in_specs += [pl.BlockSpec((n_ctx, 256), lambda b, j: (b, 0)),
                     pl.BlockSpec((DIFF_HEADS, VT_ROWS, n_ctx), lambda b, j: (0, 0, b))]
        args += list(ctx)
    return pl.pallas_call(
        functools.partial(_diff_attn_kernel, has_ctx=ctx is not None, nsub=nsub, lam_init=lam_init),
        grid=(nb, npt),
        in_specs=in_specs,
        out_specs=pl.BlockSpec((tq, 256), lambda b, j: (b * npt + j, 0)),
        out_shape=jax.ShapeDtypeStruct((nb * n, 256), F32),
        scratch_shapes=_att_scratch(n + n_ctx),
        compiler_params=_params("arbitrary", "arbitrary"),
        name="diff_attn_ctx" if ctx is not None else "diff_attn",
    )(*args)


def _shift_rows(v, k):
    return pltpu.roll(v, (-k) % v.shape[0], 0)


def _scan_strided(a_ref, b_ref, h_ref, row0, carry, n_rows, reverse):
    sub = lax.broadcasted_iota(jnp.int32, (8, 128), 0)
    span = 8 * SCAN_RUN
    order = tuple(range(SCAN_RUN))[::-1] if reverse else tuple(range(SCAN_RUN))
    starts = tuple(range(0, n_rows, span))[::-1] if reverse else tuple(range(0, n_rows, span))
    carries = []
    for lt in range(a_ref.shape[0]):
        c_in = carry[:, lt * 128:(lt + 1) * 128]
        for start in starts:
            tile = lambda ref, g: ref[lt, pl.ds(row0 + start + g, 8, stride=SCAN_RUN), :]
            a = [tile(a_ref, g) for g in range(SCAN_RUN)]
            b = [tile(b_ref, g) for g in range(SCAN_RUN)]
            h = {order[0]: b[order[0]]}
            p = {order[0]: a[order[0]]}
            for prev, g in zip(order, order[1:]):
                h[g] = a[g] * h[prev] + b[g]
                p[g] = a[g] * p[prev]
            pi, hi = p[order[-1]], h[order[-1]]
            for s in (1, 2, 4):
                shift = 8 - s if reverse else s
                valid = (sub < 8 - s) if reverse else (sub >= s)
                pr, hr = pltpu.roll(pi, shift, 0), pltpu.roll(hi, shift, 0)
                hi = jnp.where(valid, pi * hr + hi, hi)
                pi = jnp.where(valid, pi * pr, pi)
            one = 7 if reverse else 1
            first = (sub == 7) if reverse else (sub == 0)
            pe = jnp.where(first, 1.0, pltpu.roll(pi, one, 0))
            he = jnp.where(first, 0.0, pltpu.roll(hi, one, 0))
            c = pe * c_in + he
            for g in range(SCAN_RUN):
                h_ref[lt, pl.ds(start + g, 8, stride=SCAN_RUN), :] = h[g] + p[g] * c
            last = 0 if reverse else 7
            c_in = pi[last:last + 1, :] * c_in + hi[last:last + 1, :]
        carries.append(c_in)
    return jnp.concatenate(carries, axis=1)


def _sigmoid(x):
    return 0.5 * jnp.tanh(0.5 * x) + 0.5


def _gelu_tanh(x):
    return x * (0.5 * (1.0 + jnp.tanh(math.sqrt(2.0 / math.pi) * (x + 0.044715 * (x * x * x)))))


def _lru_kernel(u_ref, h0_ref, cw_ref, cb_ref, wg_ref, bg_ref, lam_ref, y_ref, st_ref,
                xpad, a1s, b1s, a0c, b0c, hc, *, N, T):
    W = LRU_WIDTH
    nc = N // T
    tiles = [slice(lt * 128, (lt + 1) * 128) for lt in range(W // 128)]
    zeros = jnp.zeros((HALO, W), F32)
    xpad[0:HALO, :] = zeros
    xpad[N + HALO:N + 2 * HALO, :] = zeros

    def fill(j, carry):
        r0 = pl.multiple_of(j * T, T)
        xpad[pl.ds(r0 + HALO, T), :] = u_ref[pl.ds(r0, T), 0:W]
        return carry

    lax.fori_loop(0, nc, fill, 0)

    z = -lam_ref[...]
    sp = jnp.maximum(z, 0.0) + jnp.log1p(jnp.exp(-jnp.abs(z)))
    cw = cw_ref[...]
    cb = cb_ref[...]
    bg = bg_ref[...]

    def fwd(j, carry):
        r0 = pl.multiple_of(j * T, T)
        ext = xpad[pl.ds(r0, T + 2 * HALO), :]
        body = slice(HALO, HALO + T)
        xc = cb
        for tap in range(4):
            xc = xc + _shift_rows(ext, tap - 1)[body] * cw[tap:tap + 1]
        g = _sigmoid(_dot(xc.astype(BF16), wg_ref[...]) + bg)
        ab = []
        for d in range(2):
            r = g[:, d * W:(d + 1) * W]
            i = g[:, (2 + d) * W:(3 + d) * W]
            log_a = (-LRU_C * r) * sp[d:d + 1]
            a = jnp.exp(log_a)
            bt = (jnp.sqrt(1.0 - a * a) * i) * xc
            ab.append((a, bt))
        for lt, lanes in enumerate(tiles):
            a0c[lt] = ab[0][0][:, lanes]
            b0c[lt] = ab[0][1][:, lanes]
            a1s[lt, pl.ds(r0, T), :] = ab[1][0][:, lanes]
            b1s[lt, pl.ds(r0, T), :] = ab[1][1][:, lanes]
        carry = _scan_strided(a0c, b0c, hc, 0, carry, T, reverse=False)
        for lt, lanes in enumerate(tiles):
            y_ref[pl.ds(r0, T), lanes] = hc[lt]
        return carry

    cf = lax.fori_loop(0, nc, fwd, h0_ref[0, 0:1, :])

    def bwd(jj, carry):
        r0 = pl.multiple_of((nc - 1 - jj) * T, T)
        carry = _scan_strided(a1s, b1s, hc, r0, carry, T, reverse=True)
        for lt, lanes in enumerate(tiles):
            gb = u_ref[pl.ds(r0, T), W + lt * 128:W + (lt + 1) * 128]
            y_ref[pl.ds(r0, T), lanes] = (y_ref[pl.ds(r0, T), lanes] + hc[lt]) * _gelu_tanh(gb)
        return carry

    cbw = lax.fori_loop(0, nc, bwd, h0_ref[0, 1:2, :])
    st_ref[0, 0:1, :] = cf
    st_ref[0, 1:2, :] = cbw


def _lru(u, h0, h0_block, lw, *, nb, n):
    T = min(n, 256)
    W = LRU_WIDTH
    return pl.pallas_call(
        functools.partial(_lru_kernel, N=n, T=T),
        grid=(nb,),
        in_specs=[
            pl.BlockSpec((n, 2 * W), lambda b: (b, 0)),
            pl.BlockSpec((1, 2, W), lambda b: (h0_block(b), 0, 0)),
            lw.spec("conv_w"), lw.spec("conv_b"), lw.spec("w_gate"), lw.spec("b_gate"),
            lw.spec("lru_lambda"),
        ],
        out_specs=[
            pl.BlockSpec((n, W), lambda b: (b, 0)),
            pl.BlockSpec((1, 2, W), lambda b: (b, 0, 0)),
        ],
        out_shape=[
            jax.ShapeDtypeStruct((nb * n, W), F32),
            jax.ShapeDtypeStruct((nb, 2, W), F32),
        ],
        scratch_shapes=[
            pltpu.VMEM((n + 2 * HALO, W), F32),
            pltpu.VMEM((W // 128, n, 128), F32),
            pltpu.VMEM((W // 128, n, 128), F32),
            pltpu.VMEM((W // 128, T, 128), F32),
            pltpu.VMEM((W // 128, T, 128), F32),
            pltpu.VMEM((W // 128, T, 128), F32),
        ],
        compiler_params=_params("arbitrary"),
        name="rglru",
    )(u, h0, lw["conv_w"], lw["conv_b"], lw["w_gate"], lw["b_gate"], lw["lru_lambda"])


def _pool_kernel(u_ref, wp_ref, sc_ref, y_ref, xpad, *, N, T):
    W = GROUP_WIDTH
    nc = N // T
    zeros = jnp.zeros((HALO, W), F32)
    xpad[0:HALO, :] = zeros
    xpad[N + HALO:N + 2 * HALO, :] = zeros

    def fill(j, carry):
        r0 = pl.multiple_of(j * T, T)
        xpad[pl.ds(r0 + HALO, T), :] = u_ref[pl.ds(r0, T), :]
        return carry

    lax.fori_loop(0, nc, fill, 0)

    grp = lax.broadcasted_iota(jnp.int32, (1, W), 1) // POOL_CH
    half = jnp.where(grp == 0, 1, jnp.where(grp == 1, 2, jnp.where(grp == 2, 4, 8)))
    scale = sc_ref[...]

    def chunk(j, carry):
        r0 = pl.multiple_of(j * T, T)
        ext = xpad[pl.ds(r0, T + 2 * HALO), :]
        w2 = _shift_rows(ext, -1) + ext
        w4 = _shift_rows(w2, -1) + _shift_rows(w2, 1)
        w8 = _shift_rows(w4, -2) + _shift_rows(w4, 2)
        w16 = _shift_rows(w8, -4) + _shift_rows(w8, 4)
        ws = jnp.where(grp == 0, w2, jnp.where(grp == 1, w4, jnp.where(grp == 2, w8, w16)))
        body = slice(HALO, HALO + T)
        t = r0 + lax.broadcasted_iota(jnp.int32, (T, W), 0)
        cnt = (jnp.minimum(t + half, N) - jnp.maximum(t - half, 0)).astype(F32)
        d = ws[body] / cnt - ext[body]
        y_ref[pl.ds(r0, T), :] = _dot(d.astype(BF16), wp_ref[...]) * scale
        return carry

    lax.fori_loop(0, nc, chunk, 0)


def _pool(u, lw, *, nb, n):
    W = GROUP_WIDTH
    T = min(n, 256)
    return pl.pallas_call(
        functools.partial(_pool_kernel, N=n, T=T),
        grid=(nb,),
        in_specs=[pl.BlockSpec((n, W), lambda b: (b, 0)), lw.spec("w_pool"), lw.spec("pool_scale")],
        out_specs=pl.BlockSpec((n, W), lambda b: (b, 0)),
        out_shape=jax.ShapeDtypeStruct((nb * n, W), F32),
        scratch_shapes=[pltpu.VMEM((n + 2 * HALO, W), F32)],
        compiler_params=_params("arbitrary"),
        name="pool_mixer",
    )(u, lw["w_pool"], lw["pool_scale"])


def _mix_ffn_kernel(*refs, final):
    if final:
        (x_ref, ya_ref, yb_ref, yc_ref, yd_ref, mod_ref, g2_ref, wo_ref, wg_ref, wu_ref, wd_ref,
         gf_ref, o_ref) = refs
    else:
        (x_ref, ya_ref, yb_ref, yc_ref, yd_ref, mod_ref, g2_ref, wo_ref, wg_ref, wu_ref, wd_ref,
         o_ref) = refs
    mod = mod_ref[0]
    gate1 = mod[:, 2 * D_MODEL:3 * D_MODEL]
    sh2 = mod[:, 3 * D_MODEL:4 * D_MODEL]
    sc2 = mod[:, 4 * D_MODEL:5 * D_MODEL]
    gate2 = mod[:, 5 * D_MODEL:6 * D_MODEL]
    mix = None
    for i, y_ref in enumerate((ya_ref, yb_ref, yc_ref, yd_ref)):
        part = _dot(y_ref[...].astype(BF16), wo_ref[i * GROUP_WIDTH:(i + 1) * GROUP_WIDTH, :])
        mix = part if mix is None else mix + part
    x1 = x_ref[...] + gate1 * mix
    h = _rms_rows(x1, D_MODEL) * g2_ref[...]
    hb = (h * (1.0 + sc2) + sh2).astype(BF16)
    ff = None
    for lo, hi in FF_CHUNKS:
        g = _dot(hb, wg_ref[:, lo:hi])
        up = _dot(hb, wu_ref[:, lo:hi])
        act = ((g * jax.nn.sigmoid(g)) * up).astype(BF16)
        part = _dot(act, wd_ref[lo:hi, :])
        ff = part if ff is None else ff + part
    x2 = x1 + gate2 * ff
    if final:
        x2 = _rms_rows(x2, D_MODEL) * gf_ref[...]
    o_ref[...] = x2


def _mix_ffn(x, ys, mod, lw, gf, *, nb, n, final):
    T = nb * n
    tm = 256
    npt = n // tm

    def tok(width):
        return pl.BlockSpec((tm, width), lambda i: (i, 0))

    wnames = ("g2", "w_out", "w_gate_ff", "w_up_ff", "w_down")
    in_specs = [tok(D_MODEL), tok(256), tok(256), tok(256), tok(256),
                mod.spec(lambda i: i // npt)] + [lw.spec(nm) for nm in wnames]
    args = [x, *ys, mod.table] + [lw[nm] for nm in wnames]
    if final:
        in_specs.append(_resident((1, D_MODEL)))
        args.append(gf)
    return pl.pallas_call(
        functools.partial(_mix_ffn_kernel, final=final),
        grid=(T // tm,),
        in_specs=in_specs,
        out_specs=tok(D_MODEL),
        out_shape=jax.ShapeDtypeStruct((T, D_MODEL), F32),
        compiler_params=_params("arbitrary"),
        name="mix_ffn_final" if final else "mix_ffn",
    )(*args)


def _block_diag(w):
    L, G, c, e = w.shape
    return jnp.einsum('lgce,gh->lgche', w, jnp.eye(G, dtype=w.dtype)).reshape(L, G * c, G * e)


def _rot_cols(w):
    return jnp.concatenate([-w[..., 16:32], w[..., 0:16]], axis=-1)


def _stack_weights(p):
    w_in = p["w_in"]
    o1 = MLA_Q_RANK
    o2 = o1 + MLA_KV_RANK
    o3 = o2 + MLA_ROPE
    c_q, c_kv, k_r, rest = w_in[..., :o1], w_in[..., o1:o2], w_in[..., o2:o3], w_in[..., o3:]
    z = lambda n: jnp.zeros((DEPTH, D_MODEL, n), F32)
    w_in_eff = jnp.concatenate([c_q, k_r, z(32), c_kv, z(64), _rot_cols(k_r), z(32), rest], axis=-1)

    w_uq = p["mla_w_uq"]
    qd = MLA_NOPE + MLA_ROPE
    wq_parts, wqr_parts = [], []
    zq = lambda n: jnp.zeros((DEPTH, MLA_Q_RANK, n), F32)
    for h in range(MLA_HEADS):
        wh = w_uq[..., h * qd:(h + 1) * qd]
        wq_parts += [wh, zq(MLA_SLOT - qd)]
        wqr_parts += [zq(MLA_NOPE), _rot_cols(wh[..., MLA_NOPE:]), zq(MLA_SLOT - qd)]
    pad_rows = lambda w: jnp.pad(w, ((0, 0), (0, 256 - MLA_Q_RANK), (0, 0)))
    w_ukv = p["mla_w_ukv"]
    wk_parts, wv_parts = [], []
    zk = jnp.zeros((DEPTH, MLA_KV_RANK, MLA_SLOT - MLA_NOPE), F32)
    for h in range(MLA_HEADS):
        base = h * (MLA_NOPE + MLA_V)
        wk_parts += [w_ukv[..., base:base + MLA_NOPE], zk]
        wv_parts.append(w_ukv[..., base + MLA_NOPE:base + MLA_NOPE + MLA_V])

    w_r, w_i, b_r, b_i = p["lru_w_r"], p["lru_w_i"], p["lru_b_r"], p["lru_b_i"]
    w_gate = jnp.concatenate([_block_diag(w_r[:, 0]), _block_diag(w_r[:, 1]),
                              _block_diag(w_i[:, 0]), _block_diag(w_i[:, 1])], axis=-1)
    b_gate = jnp.concatenate([b_r[:, 0], b_r[:, 1], b_i[:, 0], b_i[:, 1]], axis=-1)
    w_gu = p["w_gu"]
    row = lambda v: v[:, None, :]
    return {
        "g1": row(p["norm1_g"]),
        "g2": row(p["norm2_g"]),
        "w_in": w_in_eff.astype(BF16),
        "gq": row(jnp.pad(p["mla_q_norm_g"], ((0, 0), (0, 256 - MLA_Q_RANK)))),
        "gkv": row(p["mla_kv_norm_g"]),
        "wq": pad_rows(jnp.concatenate(wq_parts, axis=-1)).astype(BF16),
        "wqr": pad_rows(jnp.concatenate(wqr_parts, axis=-1)).astype(BF16),
        "wkv": jnp.concatenate(wk_parts + wv_parts, axis=-1).astype(BF16),
        "conv_w": p["lru_conv_w"],
        "conv_b": row(p["lru_conv_b"]),
        "w_gate": w_gate.astype(BF16),
        "b_gate": row(b_gate),
        "lru_lambda": p["lru_lambda"],
        "w_pool": _block_diag(p["pool_w"]).astype(BF16),
        "pool_scale": row(p["pool_scale"]),
        "diff_lambda": p["diff_lambda"],
        "diff_g": row(jnp.tile(p["diff_norm_g"], (1, DIFF_HEADS))),
        "w_out": p["w_out"].astype(BF16),
        "w_gate_ff": w_gu[..., :FF_HIDDEN].astype(BF16),
        "w_up_ff": w_gu[..., FF_HIDDEN:].astype(BF16),
        "w_down": p["w_down"].astype(BF16),
    }


def _rope_tables(n, positional):
    quarter = MLA_ROPE // 4
    if positional:
        t = jnp.arange(n)
        row = (t // GRID_W).astype(F32)
        col = (t % GRID_W).astype(F32)
        inv = ROPE_BASE ** (-jnp.arange(quarter, dtype=F32) / quarter)
        ang = jnp.concatenate([row[:, None] * inv, col[:, None] * inv], axis=-1)
        cos, sin = jnp.cos(ang), jnp.sin(ang)
    else:
        cos, sin = jnp.ones((n, 16), F32), jnp.zeros((n, 16), F32)
    scale = LOG2E / math.sqrt(MLA_NOPE + MLA_ROPE)
    place = np.zeros((32, TAB_WIDTH), np.float32)
    offset = np.zeros((1, TAB_WIDTH), np.float32)
    offset[0, 0:64] = offset[0, 96:128] = scale
    for i in range(16):
        for half in (64, 80):
            place[i, half + i] = scale
            place[16 + i, 128 + half + i] = scale
            place[i, 256 + half + i] = 1.0
            place[16 + i, 384 + half + i] = 1.0
        for grp in range(8):
            place[i, 512 + 32 * grp + i] = place[i, 512 + 32 * grp + 16 + i] = 1.0
            place[16 + i, 768 + 32 * grp + i] = -1.0
            place[16 + i, 1024 + 32 * grp + 16 + i] = 1.0
    return jnp.dot(jnp.concatenate([cos, sin], axis=1), place, precision=lax.Precision.HIGHEST) + offset


def _layer(x, mod, lw, tabs, layer_idx, ctx, gf, *, nb, n, final):
    emit_cache = ctx is None
    tok_nb, tok_n = (1, nb * n) if mod.shared else (nb, n)
    outs = _inproj(x, mod, lw, tabs, nb=tok_nb, n=tok_n, emit_cache=emit_cache)
    q, k, vt, u_lru, u_pool, dq, dk, dvt = outs[:8]
    lam_init = 0.8 - 0.6 * math.exp(-0.3 * layer_idx)
    if ctx is None:
        h0 = jnp.zeros((1, 2, LRU_WIDTH), F32)
        h0_block = lambda b: 0
        mla_ctx = diff_ctx = None
    else:
        ckv, kr_pad, cdk, cdv, h0 = ctx
        p = ckv.shape[0] // (nb * DEPTH)
        h0_block = lambda b: b * DEPTH + layer_idx
        kc, vtc, dkc, dvtc = _ctx_prep(ckv, kr_pad, cdk, cdv, lw, nb=nb, p=p)
        mla_ctx = (kc, vtc)
        diff_ctx = (dkc, dvtc)
    y_mla = _mla_attn(q, k, vt, mla_ctx, nb=nb, n=n)
    y_lru, st = _lru(u_lru, h0, h0_block, lw, nb=nb, n=n)
    y_pool = _pool(u_pool, lw, nb=nb, n=n)
    y_diff = _diff_attn(dq, dk, dvt, diff_ctx, lw, nb=nb, n=n, lam_init=lam_init)
    x2 = _mix_ffn(x, (y_mla, y_lru, y_pool, y_diff), mod, lw, gf, nb=tok_nb, n=tok_n, final=final)
    cache = (outs[8], outs[9][:, 64:96], outs[10], outs[11], st) if emit_cache else None
    return x2, cache


def kernel(x_prompt, x_sample, cache_mla_ckv, cache_mla_krope, cache_diff_k, cache_diff_v, state_lru,
           c, c_ctx, w_ada, b_ada, norm1_g, norm2_g, w_in, mla_q_norm_g, mla_w_uq, mla_kv_norm_g,
           mla_w_ukv, lru_conv_w, lru_conv_b, lru_w_r, lru_b_r, lru_w_i, lru_b_i, lru_lambda, pool_w,
           pool_scale, diff_lambda, diff_norm_g, w_out, w_gu, w_down, final_norm_g):
    p = {
        "norm1_g": norm1_g, "norm2_g": norm2_g, "w_in": w_in, "mla_q_norm_g": mla_q_norm_g,
        "mla_w_uq": mla_w_uq, "mla_kv_norm_g": mla_kv_norm_g, "mla_w_ukv": mla_w_ukv,
        "lru_conv_w": lru_conv_w, "lru_conv_b": lru_conv_b, "lru_w_r": lru_w_r, "lru_b_r": lru_b_r,
        "lru_w_i": lru_w_i, "lru_b_i": lru_b_i, "lru_lambda": lru_lambda, "pool_w": pool_w,
        "pool_scale": pool_scale, "diff_lambda": diff_lambda, "diff_norm_g": diff_norm_g,
        "w_out": w_out, "w_gu": w_gu, "w_down": w_down,
    }
    Bp, Np, _ = x_prompt.shape
    Bs, Ns, _ = x_sample.shape
    P = cache_mla_ckv.shape[2]

    cond_all = jnp.concatenate([c, c_ctx[None, :], jnp.zeros((MOD_ROWS - Bs - 1, D_MODEL), F32)], axis=0)
    mod_table = _ada(cond_all, w_ada, b_ada).reshape(DEPTH * MOD_ROWS, 1, 6 * D_MODEL)
    tabs_p = _rope_tables(Bp * Np, positional=False)
    tabs_s = _rope_tables(Ns, positional=True)
    kr_pad = jnp.pad(cache_mla_krope, ((0, 0), (0, 0), (0, 0), (MLA_NOPE, MLA_SLOT - MLA_NOPE - MLA_ROPE)))
    flat = lambda a, w: a.reshape(Bs * DEPTH * P, w)
    ctx = (flat(cache_mla_ckv, MLA_KV_RANK), flat(kr_pad, MLA_SLOT), flat(cache_diff_k, 256),
           flat(cache_diff_v, 256), state_lru.reshape(Bs * DEPTH, 2, LRU_WIDTH))
    gf = final_norm_g[None, :]
    stacked = _stack_weights(p)

    xp = x_prompt.reshape(Bp * Np, D_MODEL)
    xs = x_sample.reshape(Bs * Ns, D_MODEL)
    caches = []
    for l in range(DEPTH):
        lw = _LayerWeights(stacked, l)
        final = l == DEPTH - 1
        mod_p = _Mod(mod_table, l * MOD_ROWS + Bs, shared=True)
        mod_s = _Mod(mod_table, l * MOD_ROWS, shared=False)
        xp, cache = _layer(xp, mod_p, lw, tabs_p, l, None, gf, nb=Bp, n=Np, final=final)
        caches.append(cache)
        xs, _ = _layer(xs, mod_s, lw, tabs_s, l, ctx, gf, nb=Bs, n=Ns, final=final)

    stack = lambda i, w: jnp.stack([cc[i].reshape(Bp, Np, w) for cc in caches], axis=1)
    new_mla_ckv = stack(0, MLA_KV_RANK)
    new_mla_krope = stack(1, MLA_ROPE)
    new_diff_k = stack(2, 256).reshape(Bp, DEPTH, Np, DIFF_HEADS, 2, DIFF_DIM)
    new_diff_v = stack(3, 256).reshape(Bp, DEPTH, Np, DIFF_HEADS, 2 * DIFF_DIM)
    new_state_lru = jnp.stack([cc[4] for cc in caches], axis=1)
    return (xp.reshape(Bp, Np, D_MODEL), xs.reshape(Bs, Ns, D_MODEL),
            new_mla_ckv, new_mla_krope, new_diff_k, new_diff_v, new_state_lru)
```

```python
import functools
import math

import jax
import jax.numpy as jnp
import numpy as np
from jax import lax
from jax.experimental import pallas as pl
from jax.experimental.pallas import tpu as pltpu

F32 = jnp.float32
BF16 = jnp.bfloat16

D_MODEL = 1024
DEPTH = 2
GRID_W = 64
GROUP_WIDTH = 256
MLA_HEADS = 4
MLA_NOPE = 64
MLA_ROPE = 32
MLA_V = 64
MLA_Q_RANK = 192
MLA_KV_RANK = 128
MLA_SLOT = 128
LRU_WIDTH = 256
LRU_C = 8.0
POOL_WINDOWS = (2, 4, 8, 16)
POOL_CH = 64
DIFF_HEADS = 4
DIFF_DIM = 32
HEAD_V = 64
FF_HIDDEN = 2816
FF_CHUNKS = ((0, 1536), (1536, 2816))
ROPE_BASE = 10000.0
EPS = 1e-6
IN_EFF = 2048
HALO = 8
SCAN_RUN = 4
VT_ROWS = 80
ATT_TQ = 256
TOKEN_TILE = 512
TAB_WIDTH = 4 * 128 + 3 * 256
MOD_ROWS = 16
LOG2E = math.log2(math.e)

VMEM_LIMIT_BYTES = 56 * 1024 * 1024

_NT = (((1,), (1,)), ((), ()))


def _params(*sem):
    return pltpu.CompilerParams(dimension_semantics=sem, vmem_limit_bytes=VMEM_LIMIT_BYTES)


def _resident(shape):
    zeros = (0,) * len(shape)
    return pl.BlockSpec(shape, lambda *_: zeros, pipeline_mode=pl.Buffered(1))


def _dot(a, b):
    return jnp.dot(a, b, preferred_element_type=F32)


def _dot_nt(a, b):
    return lax.dot_general(a, b, _NT, preferred_element_type=F32)


def _rms_rows(x, width):
    ms = jnp.sum(x * x, axis=-1, keepdims=True) * (1.0 / width)
    return x * lax.rsqrt(ms + EPS)


def _store_vt(vt_ref, v):
    vt = v.T
    rows = v.shape[0]
    pad = VT_ROWS - HEAD_V
    ones_row = jnp.where(lax.broadcasted_iota(jnp.int32, (pad, rows), 0) == 0, 1.0, 0.0).astype(BF16)
    for hh in range(vt_ref.shape[0]):
        vt_ref[hh, 0:HEAD_V, :] = vt[hh * HEAD_V:(hh + 1) * HEAD_V, :].astype(BF16)
        vt_ref[hh, HEAD_V:VT_ROWS, :] = ones_row


class _Mod:
    def __init__(self, table, row0, shared):
        self.table, self.row0, self.shared = table, row0, shared

    def spec(self, batch_of):
        row0 = self.row0
        if self.shared:
            return pl.BlockSpec((1, 1, 6 * D_MODEL), lambda *g: (row0, 0, 0))
        return pl.BlockSpec((1, 1, 6 * D_MODEL), lambda *g: (row0 + batch_of(*g), 0, 0))


class _LayerWeights:
    def __init__(self, stacked, layer):
        self.stacked, self.layer = stacked, layer

    def __getitem__(self, name):
        return self.stacked[name]

    def spec(self, name):
        layer = self.layer
        _, rows, cols = self.stacked[name].shape
        return pl.BlockSpec((None, rows, cols), lambda *_: (layer, 0, 0), pipeline_mode=pl.Buffered(1))


def _ada_kernel(cond_ref, w_ref, b_ref, out_ref):
    c = cond_ref[...]
    s = c * jax.nn.sigmoid(c)
    out_ref[0] = _dot(s.astype(BF16), w_ref[0].astype(BF16)) + b_ref[0]


def _ada(cond_all, w_ada, b_ada):
    rows = cond_all.shape[0]
    tn = 1536
    return pl.pallas_call(
        _ada_kernel,
        grid=(DEPTH, 6 * D_MODEL // tn),
        in_specs=[
            pl.BlockSpec((rows, D_MODEL), lambda l, j: (0, 0)),
            pl.BlockSpec((1, D_MODEL, tn), lambda l, j: (l, 0, j)),
            pl.BlockSpec((1, 1, tn), lambda l, j: (l, 0, j)),
        ],
        out_specs=pl.BlockSpec((1, rows, tn), lambda l, j: (l, 0, j)),
        out_shape=jax.ShapeDtypeStruct((DEPTH, rows, 6 * D_MODEL), F32),
        compiler_params=_params("arbitrary", "arbitrary"),
        name="ada_mod",
    )(cond_all, w_ada, b_ada.reshape(DEPTH, 1, 6 * D_MODEL))


def _inproj_kernel(x_ref, mod_ref, g1_ref, win_ref, gq_ref, gkv_ref, wq_ref, wqr_ref, wkv_ref, tab_ref,
                   q_out, k_out, vt_out, lru_out, pool_out, dq_out, dk_out, dvt_out, *cache_outs):
    cosq_ref, sinq_ref, cosk_ref, sink_ref = (tab_ref.at[:, i * 128:(i + 1) * 128] for i in range(4))
    cosd_ref, sina_ref, sinb_ref = (tab_ref.at[:, 512 + i * 256:768 + i * 256] for i in range(3))
    x = x_ref[...]
    mod = mod_ref[0]
    sh1 = mod[:, 0:D_MODEL]
    sc1 = mod[:, D_MODEL:2 * D_MODEL]
    h = _rms_rows(x, D_MODEL) * g1_ref[...]
    hb = (h * (1.0 + sc1) + sh1).astype(BF16)

    u_mla = _dot(hb, win_ref[:, 0:512])
    u_pd = _dot(hb, win_ref[:, 1024:1536])
    u_kv = _dot(hb, win_ref[:, 1536:2048])
    t01 = u_mla[:, 0:256]
    lane = lax.broadcasted_iota(jnp.int32, (1, 256), 1)
    cq = jnp.where(lane < MLA_Q_RANK, t01, 0.0)
    cqn = (_rms_rows(cq, MLA_Q_RANK) * gq_ref[...]).astype(BF16)
    qa = _dot(cqn, wq_ref[...])
    qr = _dot(cqn, wqr_ref[...])
    cosq = cosq_ref[...]
    sinq = sinq_ref[...]
    ckv = u_mla[:, 256:384]
    lat = _rms_rows(ckv, MLA_KV_RANK) * gkv_ref[...]
    latb = lat.astype(BF16)
    kkv = _dot(latb, wkv_ref[...])
    kk = kkv[:, 0:MLA_HEADS * MLA_SLOT]
    _store_vt(vt_out, kkv[:, MLA_HEADS * MLA_SLOT:])
    t1 = t01[:, 128:256]
    t3 = u_mla[:, 384:512]
    kro = t1 * cosk_ref[...] + t3 * sink_ref[...]
    for hh in range(MLA_HEADS):
        sl = slice(hh * MLA_SLOT, (hh + 1) * MLA_SLOT)
        q_out[hh] = (qa[:, sl] * cosq + qr[:, sl] * sinq).astype(q_out.dtype)
        k_out[hh] = (kk[:, sl] + kro).astype(k_out.dtype)

    lru_out[...] = _dot(hb, win_ref[:, 512:1024])
    pool_out[...] = u_pd[:, 0:256]

    cosd = cosd_ref[...]
    sina = sina_ref[...]
    sinb = sinb_ref[...]

    def rope(t):
        return t * cosd + pltpu.roll(t, 256 - 16, 1) * sina + pltpu.roll(t, 16, 1) * sinb

    dq = u_pd[:, 256:512]
    dk = u_kv[:, 0:256]
    dv = u_kv[:, 256:512]
    dq_out[...] = (rope(dq) * (LOG2E / math.sqrt(DIFF_DIM))).astype(dq_out.dtype)
    dk_out[...] = rope(dk).astype(dk_out.dtype)
    _store_vt(dvt_out, dv)

    if cache_outs:
        lat_out, kr_out, dk_raw_out, dv_raw_out = cache_outs
        lat_out[...] = lat
        kr_out[...] = t1
        dk_raw_out[...] = dk
        dv_raw_out[...] = dv


def _inproj(x, mod, lw, tabs, *, nb, n, emit_cache):
    T = nb * n
    tm = TOKEN_TILE
    npt = n // tm
    row_blk = lambda j, b: b * npt + j

    def tok(width):
        return pl.BlockSpec((tm, width), lambda j, b: (row_blk(j, b), 0))

    def tab(width):
        return pl.BlockSpec((tm, width), lambda j, b: (j, 0))

    head = pl.BlockSpec((MLA_HEADS, tm, MLA_SLOT), lambda j, b: (0, row_blk(j, b), 0))
    vt_spec = pl.BlockSpec((MLA_HEADS, VT_ROWS, tm), lambda j, b: (0, 0, row_blk(j, b)))
    wnames = ("g1", "w_in", "gq", "gkv", "wq", "wqr", "wkv")
    in_specs = [tok(D_MODEL), mod.spec(lambda j, b: b)] + [lw.spec(nm) for nm in wnames] + [
        tab(TAB_WIDTH)]
    out_specs = [head, head, vt_spec, tok(512), tok(256), tok(256), tok(256), vt_spec]
    vt_shape = jax.ShapeDtypeStruct((MLA_HEADS, VT_ROWS, T), BF16)
    out_shape = [
        jax.ShapeDtypeStruct((MLA_HEADS, T, MLA_SLOT), BF16),
        jax.ShapeDtypeStruct((MLA_HEADS, T, MLA_SLOT), BF16),
        vt_shape,
        jax.ShapeDtypeStruct((T, 512), F32),
        jax.ShapeDtypeStruct((T, 256), F32),
        jax.ShapeDtypeStruct((T, 256), BF16),
        jax.ShapeDtypeStruct((T, 256), BF16),
        vt_shape,
    ]
    if emit_cache:
        out_specs += [tok(128), tok(128), tok(256), tok(256)]
        out_shape += [jax.ShapeDtypeStruct((T, 128), F32), jax.ShapeDtypeStruct((T, 128), F32),
                      jax.ShapeDtypeStruct((T, 256), F32), jax.ShapeDtypeStruct((T, 256), F32)]
    return pl.pallas_call(
        _inproj_kernel,
        grid=(npt, nb),
        in_specs=in_specs,
        out_specs=out_specs,
        out_shape=out_shape,
        compiler_params=_params("arbitrary", "arbitrary"),
        name="inproj_cache" if emit_cache else "inproj",
    )(x, mod.table, *[lw[nm] for nm in wnames], tabs)


def _ctx_prep_kernel(ckv_ref, kr_ref, dk_ref, dv_ref, wkv_ref, k_out, vt_out, dk_out, dvt_out):
    latb = ckv_ref[...].astype(BF16)
    kkv = _dot(latb, wkv_ref[...])
    kk = kkv[:, 0:MLA_HEADS * MLA_SLOT]
    kr = kr_ref[...]
    for hh in range(MLA_HEADS):
        k_out[hh] = (kk[:, hh * MLA_SLOT:(hh + 1) * MLA_SLOT] + kr).astype(k_out.dtype)
    _store_vt(vt_out, kkv[:, MLA_HEADS * MLA_SLOT:])
    dk_out[...] = dk_ref[...].astype(dk_out.dtype)
    _store_vt(dvt_out, dv_ref[...])


def _ctx_prep(ckv, kr_pad, cdk, cdv, lw, *, nb, p):
    T = nb * p
    layer = lw.layer
    cache_row = lambda w: pl.BlockSpec((p, w), lambda b: (b * DEPTH + layer, 0))
    row = lambda w: pl.BlockSpec((p, w), lambda b: (b, 0))
    vt_spec = pl.BlockSpec((MLA_HEADS, VT_ROWS, p), lambda b: (0, 0, b))
    vt_shape = jax.ShapeDtypeStruct((MLA_HEADS, VT_ROWS, T), BF16)
    return pl.pallas_call(
        _ctx_prep_kernel,
        grid=(nb,),
        in_specs=[cache_row(128), cache_row(128), cache_row(256), cache_row(256),
                  lw.spec("wkv")],
        out_specs=[pl.BlockSpec((MLA_HEADS, p, MLA_SLOT), lambda b: (0, b, 0)), vt_spec, row(256), vt_spec],
        out_shape=[jax.ShapeDtypeStruct((MLA_HEADS, T, MLA_SLOT), BF16), vt_shape,
                   jax.ShapeDtypeStruct((T, 256), BF16), vt_shape],
        compiler_params=_params("arbitrary"),
        name="ctx_prep",
    )(ckv, kr_pad, cdk, cdv, lw["wkv"])


SAFE_DENOM = 2.0 ** -60
BOUND_SLACK = 1.02


def _scores(k_new, k_ctx, q):
    sn = _dot_nt(k_new(), q)
    sc = _dot_nt(k_ctx(), q) if k_ctx is not None else None
    return sn, sc


def _exact_shift(k_new, k_ctx, q):
    sn, sc = _scores(k_new, k_ctx, q)
    m = jnp.max(sn, axis=0, keepdims=True)
    if sc is not None:
        m = jnp.maximum(m, jnp.max(sc, axis=0, keepdims=True))
    return m


def _bound_shift(q, key_norm2):
    qf = q.astype(F32)
    ones = jnp.ones((8, q.shape[1]), BF16)
    q_norm2 = _dot_nt(ones, (qf * qf).astype(BF16))[0:1, :]
    return jnp.sqrt(q_norm2 * key_norm2) * BOUND_SLACK


def _max_row_norm2(k_new, k_ctx, col_sum):
    def one(k):
        kf = k.astype(F32)
        return jnp.max(_dot((kf * kf).astype(BF16), col_sum), axis=0, keepdims=True)
    m = one(k_new)
    if k_ctx is not None:
        m = jnp.maximum(m, one(k_ctx))
    return m * BOUND_SLACK


def _exp_stage(e_buf, k_new, k_ctx, q, shift, n_ctx):
    sn, sc = _scores(k_new, k_ctx, q)
    e_buf[n_ctx:, :] = jnp.exp2(sn - shift).astype(BF16)
    if sc is not None:
        e_buf[0:n_ctx, :] = jnp.exp2(sc - shift).astype(BF16)


def _value_stage(e_buf, vt_new, vt_ctx, n_ctx):
    o = _dot(vt_new(), e_buf[n_ctx:, :])
    if vt_ctx is not None:
        o = o + _dot(vt_ctx(), e_buf[0:n_ctx, :])
    return o


def _run_pipeline(n_maps, exp_stage, value_stage):
    exp_stage(0)
    for u in range(n_maps):
        if u + 1 < n_maps:
            exp_stage(u + 1)
        value_stage(u)


def _att_scratch(nk, key_shape):
    scratch = [pltpu.VMEM((8, 128), F32),
               pltpu.VMEM((MLA_HEADS * HEAD_V, ATT_TQ), F32),
               pltpu.VMEM((nk, ATT_TQ), BF16), pltpu.VMEM((nk, ATT_TQ), BF16)]
    if key_shape is not None:
        scratch += [pltpu.VMEM(key_shape, BF16), pltpu.VMEM((MLA_HEADS, VT_ROWS, nk), BF16)]
    return scratch


def _att_nsub(n):
    return 2 if n % (2 * ATT_TQ) == 0 else 1


def _mla_attn_kernel(*refs, has_ctx, nsub):
    if has_ctx:
        q_ref, k_ref, vt_ref, kc_ref, vtc_ref, o_ref, kn2, ot, e0, e1, keys, vals = refs
        n_ctx = kc_ref.shape[1]
    else:
        q_ref, k_ref, vt_ref, o_ref, kn2, ot, e0, e1 = refs
        keys, vals = k_ref, vt_ref
    e_bufs = (e0, e1)

    @pl.when(pl.program_id(1) == 0)
    def _():
        ones = jnp.ones((MLA_SLOT, 128), BF16)
        for hh in range(MLA_HEADS):
            kn2[hh:hh + 1, :] = _max_row_norm2(k_ref[hh], kc_ref[hh] if has_ctx else None, ones)
        if has_ctx:
            keys[:, 0:n_ctx, :] = kc_ref[...]
            keys[:, n_ctx:, :] = k_ref[...]
            vals[:, :, 0:n_ctx] = vtc_ref[...]
            vals[:, :, n_ctx:] = vt_ref[...]

    def run(exact):
        denoms = []

        def exp_stage(u):
            t, hh = divmod(u, MLA_HEADS)
            q = q_ref[hh, t * ATT_TQ:(t + 1) * ATT_TQ, :]
            k_all = lambda: keys[hh]
            shift = _exact_shift(k_all, None, q) if exact else _bound_shift(q, kn2[hh:hh + 1, 0:1])
            _exp_stage(e_bufs[u % 2], k_all, None, q, shift, 0)

        def value_stage(u):
            t, hh = divmod(u, MLA_HEADS)
            o = _value_stage(e_bufs[u % 2], lambda: vals[hh], None, 0)
            denom = o[HEAD_V:HEAD_V + 1, :]
            denoms.append(denom)
            ot[hh * HEAD_V:(hh + 1) * HEAD_V, :] = o[0:HEAD_V, :] * (1.0 / denom)
            if hh == MLA_HEADS - 1:
                o_ref[t * ATT_TQ:(t + 1) * ATT_TQ, :] = ot[...].T

        _run_pipeline(nsub * MLA_HEADS, exp_stage, value_stage)
        return jnp.min(functools.reduce(jnp.minimum, denoms))

    denom_min = run(exact=False)

    @pl.when(jnp.logical_not(denom_min >= SAFE_DENOM))
    def _():
        run(exact=True)


def _mla_attn(q, k, vt, ctx, *, nb, n):
    nsub = _att_nsub(n)
    tq = nsub * ATT_TQ
    npt = n // tq
    H, S = MLA_HEADS, MLA_SLOT
    in_specs = [
        pl.BlockSpec((H, tq, S), lambda b, j: (0, b * npt + j, 0)),
        pl.BlockSpec((H, n, S), lambda b, j: (0, b, 0)),
        pl.BlockSpec((H, VT_ROWS, n), lambda b, j: (0, 0, b)),
    ]
    args = [q, k, vt]
    n_ctx = 0
    if ctx is not None:
        n_ctx = ctx[0].shape[1] // nb
        in_specs += [
            pl.BlockSpec((H, n_ctx, S), lambda b, j: (0, b, 0)),
            pl.BlockSpec((H, VT_ROWS, n_ctx), lambda b, j: (0, 0, b)),
        ]
        args += list(ctx)
    return pl.pallas_call(
        functools.partial(_mla_attn_kernel, has_ctx=ctx is not None, nsub=nsub),
        grid=(nb, npt),
        in_specs=in_specs,
        out_specs=pl.BlockSpec((tq, 256), lambda b, j: (b * npt + j, 0)),
        out_shape=jax.ShapeDtypeStruct((nb * n, 256), F32),
        scratch_shapes=_att_scratch(n + n_ctx, (H, n + n_ctx, S) if ctx is not None else None),
        compiler_params=_params("arbitrary", "arbitrary"),
        name="mla_attn_ctx" if ctx is not None else "mla_attn",
    )(*args)


def _diff_attn_kernel(*refs, has_ctx, nsub, lam_init):
    if has_ctx:
        lv_ref, g_ref, q_ref, k_ref, vt_ref, kc_ref, vtc_ref, o_ref, kn2, ot, e0, e1, keys, vals = refs
        n_ctx = kc_ref.shape[0]
    else:
        lv_ref, g_ref, q_ref, k_ref, vt_ref, o_ref, kn2, ot, e0, e1 = refs
        keys, vals = k_ref, vt_ref
    e_bufs = (e0, e1)
    lv = lv_ref[...]
    lam = (jnp.exp(jnp.sum(lv[0:1] * lv[1:2], axis=-1, keepdims=True))
           - jnp.exp(jnp.sum(lv[2:3] * lv[3:4], axis=-1, keepdims=True)) + lam_init)
    lane128 = lax.broadcasted_iota(jnp.int32, (1, 128), 1)
    n_pairs = 2 * DIFF_HEADS

    @pl.when(pl.program_id(1) == 0)
    def _():
        dim = lax.broadcasted_iota(jnp.int32, (256, 128), 0)
        col = lax.broadcasted_iota(jnp.int32, (256, 128), 1)
        indicator = jnp.where(dim // DIFF_DIM == col, 1.0, 0.0).astype(BF16)
        kn2[0:1, :] = _max_row_norm2(k_ref[...], kc_ref[...] if has_ctx else None, indicator)
        if has_ctx:
            keys[0:n_ctx, :] = kc_ref[...]
            keys[n_ctx:, :] = k_ref[...]
            vals[:, :, 0:n_ctx] = vtc_ref[...]
            vals[:, :, n_ctx:] = vt_ref[...]

    def run(exact):
        denoms = []
        outs = {}

        def exp_stage(u):
            t, p = divmod(u, n_pairs)
            tile = slice((p * DIFF_DIM // 128) * 128, (p * DIFF_DIM // 128 + 1) * 128)
            k_new = lambda: keys[:, tile]
            k_ctx = None
            q = q_ref[t * ATT_TQ:(t + 1) * ATT_TQ, tile]
            lo = p * DIFF_DIM - tile.start
            in_pair = (lane128 >= lo) & (lane128 < lo + DIFF_DIM)
            qm = jnp.where(in_pair, q, jnp.zeros_like(q))
            shift = _exact_shift(k_new, k_ctx, qm) if exact else _bound_shift(qm, kn2[0:1, p:p + 1])
            _exp_stage(e_bufs[u % 2], k_new, k_ctx, qm, shift, 0)

        def value_stage(u):
            t, p = divmod(u, n_pairs)
            hh = p // 2
            o = _value_stage(e_bufs[u % 2], lambda: vals[hh], None, 0)
            denom = o[HEAD_V:HEAD_V + 1, :]
            denoms.append(denom)
            outs[u] = (o[0:HEAD_V, :], denom)
            if p % 2 == 1:
                (o0, l0), (o1, l1) = outs.pop(u - 1), outs.pop(u)
                o = o0 * (1.0 / l0) - o1 * (lam / l1)
                msq = jnp.sum(o * o, axis=0, keepdims=True) * (1.0 / HEAD_V)
                ot[hh * HEAD_V:(hh + 1) * HEAD_V, :] = o * lax.rsqrt(msq + EPS)
            if p == n_pairs - 1:
                o_ref[t * ATT_TQ:(t + 1) * ATT_TQ, :] = (ot[...].T * g_ref[...]) * (1.0 - lam_init)

        _run_pipeline(nsub * n_pairs, exp_stage, value_stage)
        return jnp.min(functools.reduce(jnp.minimum, denoms))

    denom_min = run(exact=False)

    @pl.when(jnp.logical_not(denom_min >= SAFE_DENOM))
    def _():
        run(exact=True)


def _diff_attn(q, k, vt, ctx, lw, *, nb, n, lam_init):
    nsub = 1
    tq = nsub * ATT_TQ
    npt = n // tq
    in_specs = [
        lw.spec("diff_lambda"),
        lw.spec("diff_g"),
        pl.BlockSpec((tq, 256), lambda b, j: (b * npt + j, 0)),
        pl.BlockSpec((n, 256), lambda b, j: (b, 0)),
        pl.BlockSpec((DIFF_HEADS, VT_ROWS, n), lambda b, j: (0, 0, b)),
    ]
    args = [lw["diff_lambda"], lw["diff_g"], q, k, vt]
    n_ctx = 0
    if ctx is not None:
        n_ctx = ctx[0].shape[0] // nb
        in_specs += [pl.BlockSpec((n_ctx, 256), lambda b, j: (b, 0)),
                     pl.BlockSpec((DIFF_HEADS, VT_ROWS, n_ctx), lambda b, j: (0, 0, b))]
        args += list(ctx)
    return pl.pallas_call(
        functools.partial(_diff_attn_kernel, has_ctx=ctx is not None, nsub=nsub, lam_init=lam_init),
        grid=(nb, npt),
        in_specs=in_specs,
        out_specs=pl.BlockSpec((tq, 256), lambda b, j: (b * npt + j, 0)),
        out_shape=jax.ShapeDtypeStruct((nb * n, 256), F32),
        scratch_shapes=_att_scratch(n + n_ctx, (n + n_ctx, 256) if ctx is not None else None),
        compiler_params=_params("arbitrary", "arbitrary"),
        name="diff_attn_ctx" if ctx is not None else "diff_attn",
    )(*args)


def _shift_rows(v, k):
    return pltpu.roll(v, (-k) % v.shape[0], 0)


def _scan_strided(a_ref, b_ref, h_ref, row0, carry, n_rows, reverse):
    sub = lax.broadcasted_iota(jnp.int32, (8, 128), 0)
    span = 8 * SCAN_RUN
    order = tuple(range(SCAN_RUN))[::-1] if reverse else tuple(range(SCAN_RUN))
    starts = tuple(range(0, n_rows, span))[::-1] if reverse else tuple(range(0, n_rows, span))
    carries = []
    for lt in range(a_ref.shape[0]):
        c_in = carry[:, lt * 128:(lt + 1) * 128]
        for start in starts:
            tile = lambda ref, g: ref[lt, pl.ds(row0 + start + g, 8, stride=SCAN_RUN), :]
            a = [tile(a_ref, g) for g in range(SCAN_RUN)]
            b = [tile(b_ref, g) for g in range(SCAN_RUN)]
            h = {order[0]: b[order[0]]}
            p = {order[0]: a[order[0]]}
            for prev, g in zip(order, order[1:]):
                h[g] = a[g] * h[prev] + b[g]
                p[g] = a[g] * p[prev]
            pi, hi = p[order[-1]], h[order[-1]]
            for s in (1, 2, 4):
                shift = 8 - s if reverse else s
                valid = (sub < 8 - s) if reverse else (sub >= s)
                pr, hr = pltpu.roll(pi, shift, 0), pltpu.roll(hi, shift, 0)
                hi = jnp.where(valid, pi * hr + hi, hi)
                pi = jnp.where(valid, pi * pr, pi)
            one = 7 if reverse else 1
            first = (sub == 7) if reverse else (sub == 0)
            pe = jnp.where(first, 1.0, pltpu.roll(pi, one, 0))
            he = jnp.where(first, 0.0, pltpu.roll(hi, one, 0))
            c = pe * c_in + he
            for g in range(SCAN_RUN):
                h_ref[lt, pl.ds(start + g, 8, stride=SCAN_RUN), :] = h[g] + p[g] * c
            last = 0 if reverse else 7
            c_in = pi[last:last + 1, :] * c_in + hi[last:last + 1, :]
        carries.append(c_in)
    return jnp.concatenate(carries, axis=1)


def _sigmoid(x):
    return 0.5 * jnp.tanh(0.5 * x) + 0.5


def _gelu_tanh(x):
    return x * (0.5 * (1.0 + jnp.tanh(math.sqrt(2.0 / math.pi) * (x + 0.044715 * (x * x * x)))))


def _lru_kernel(u_ref, h0_ref, cw_ref, cb_ref, wg_ref, bg_ref, lam_ref, y_ref, st_ref,
                xpad, a1s, b1s, a0c, b0c, hc, *, N, T):
    W = LRU_WIDTH
    nc = N // T
    tiles = [slice(lt * 128, (lt + 1) * 128) for lt in range(W // 128)]
    zeros = jnp.zeros((HALO, W), F32)
    xpad[0:HALO, :] = zeros
    xpad[N + HALO:N + 2 * HALO, :] = zeros

    def fill(j, carry):
        r0 = pl.multiple_of(j * T, T)
        xpad[pl.ds(r0 + HALO, T), :] = u_ref[pl.ds(r0, T), 0:W]
        return carry

    lax.fori_loop(0, nc, fill, 0)

    z = -lam_ref[...]
    sp = jnp.maximum(z, 0.0) + jnp.log1p(jnp.exp(-jnp.abs(z)))
    cw = cw_ref[...]
    cb = cb_ref[...]
    bg = bg_ref[...]

    def fwd(j, carry):
        r0 = pl.multiple_of(j * T, T)
        ext = xpad[pl.ds(r0, T + 2 * HALO), :]
        body = slice(HALO, HALO + T)
        xc = cb
        for tap in range(4):
            xc = xc + _shift_rows(ext, tap - 1)[body] * cw[tap:tap + 1]
        g = _sigmoid(_dot(xc.astype(BF16), wg_ref[...]) + bg)
        ab = []
        for d in range(2):
            r = g[:, d * W:(d + 1) * W]
            i = g[:, (2 + d) * W:(3 + d) * W]
            log_a = (-LRU_C * r) * sp[d:d + 1]
            a = jnp.exp(log_a)
            bt = (jnp.sqrt(1.0 - a * a) * i) * xc
            ab.append((a, bt))
        for lt, lanes in enumerate(tiles):
            a0c[lt] = ab[0][0][:, lanes]
            b0c[lt] = ab[0][1][:, lanes]
            a1s[lt, pl.ds(r0, T), :] = ab[1][0][:, lanes]
            b1s[lt, pl.ds(r0, T), :] = ab[1][1][:, lanes]
        carry = _scan_strided(a0c, b0c, hc, 0, carry, T, reverse=False)
        for lt, lanes in enumerate(tiles):
            y_ref[pl.ds(r0, T), lanes] = hc[lt]
        return carry

    cf = lax.fori_loop(0, nc, fwd, h0_ref[0, 0:1, :])

    def bwd(jj, carry):
        r0 = pl.multiple_of((nc - 1 - jj) * T, T)
        carry = _scan_strided(a1s, b1s, hc, r0, carry, T, reverse=True)
        for lt, lanes in enumerate(tiles):
            gb = u_ref[pl.ds(r0, T), W + lt * 128:W + (lt + 1) * 128]
            y_ref[pl.ds(r0, T), lanes] = (y_ref[pl.ds(r0, T), lanes] + hc[lt]) * _gelu_tanh(gb)
        return carry

    cbw = lax.fori_loop(0, nc, bwd, h0_ref[0, 1:2, :])
    st_ref[0, 0:1, :] = cf
    st_ref[0, 1:2, :] = cbw


def _lru(u, h0, h0_block, lw, *, nb, n):
    T = min(n, 256)
    W = LRU_WIDTH
    return pl.pallas_call(
        functools.partial(_lru_kernel, N=n, T=T),
        grid=(nb,),
        in_specs=[
            pl.BlockSpec((n, 2 * W), lambda b: (b, 0)),
            pl.BlockSpec((1, 2, W), lambda b: (h0_block(b), 0, 0)),
            lw.spec("conv_w"), lw.spec("conv_b"), lw.spec("w_gate"), lw.spec("b_gate"),
            lw.spec("lru_lambda"),
        ],
        out_specs=[
            pl.BlockSpec((n, W), lambda b: (b, 0)),
            pl.BlockSpec((1, 2, W), lambda b: (b, 0, 0)),
        ],
        out_shape=[
            jax.ShapeDtypeStruct((nb * n, W), F32),
            jax.ShapeDtypeStruct((nb, 2, W), F32),
        ],
        scratch_shapes=[
            pltpu.VMEM((n + 2 * HALO, W), F32),
            pltpu.VMEM((W // 128, n, 128), F32),
            pltpu.VMEM((W // 128, n, 128), F32),
            pltpu.VMEM((W // 128, T, 128), F32),
            pltpu.VMEM((W // 128, T, 128), F32),
            pltpu.VMEM((W // 128, T, 128), F32),
        ],
        compiler_params=_params("arbitrary"),
        name="rglru",
    )(u, h0, lw["conv_w"], lw["conv_b"], lw["w_gate"], lw["b_gate"], lw["lru_lambda"])


def _pool_kernel(u_ref, wp_ref, sc_ref, y_ref, xpad, *, N, T):
    W = GROUP_WIDTH
    nc = N // T
    zeros = jnp.zeros((HALO, W), F32)
    xpad[0:HALO, :] = zeros
    xpad[N + HALO:N + 2 * HALO, :] = zeros

    def fill(j, carry):
        r0 = pl.multiple_of(j * T, T)
        xpad[pl.ds(r0 + HALO, T), :] = u_ref[pl.ds(r0, T), :]
        return carry

    lax.fori_loop(0, nc, fill, 0)

    grp = lax.broadcasted_iota(jnp.int32, (1, W), 1) // POOL_CH
    half = jnp.where(grp == 0, 1, jnp.where(grp == 1, 2, jnp.where(grp == 2, 4, 8)))
    scale = sc_ref[...]

    def chunk(j, carry):
        r0 = pl.multiple_of(j * T, T)
        ext = xpad[pl.ds(r0, T + 2 * HALO), :]
        w2 = _shift_rows(ext, -1) + ext
        w4 = _shift_rows(w2, -1) + _shift_rows(w2, 1)
        w8 = _shift_rows(w4, -2) + _shift_rows(w4, 2)
        w16 = _shift_rows(w8, -4) + _shift_rows(w8, 4)
        ws = jnp.where(grp == 0, w2, jnp.where(grp == 1, w4, jnp.where(grp == 2, w8, w16)))
        body = slice(HALO, HALO + T)
        t = r0 + lax.broadcasted_iota(jnp.int32, (T, W), 0)
        cnt = (jnp.minimum(t + half, N) - jnp.maximum(t - half, 0)).astype(F32)
        d = ws[body] / cnt - ext[body]
        y_ref[pl.ds(r0, T), :] = _dot(d.astype(BF16), wp_ref[...]) * scale
        return carry

    lax.fori_loop(0, nc, chunk, 0)


def _pool(u, lw, *, nb, n):
    W = GROUP_WIDTH
    T = min(n, 256)
    return pl.pallas_call(
        functools.partial(_pool_kernel, N=n, T=T),
        grid=(nb,),
        in_specs=[pl.BlockSpec((n, W), lambda b: (b, 0)), lw.spec("w_pool"), lw.spec("pool_scale")],
        out_specs=pl.BlockSpec((n, W), lambda b: (b, 0)),
        out_shape=jax.ShapeDtypeStruct((nb * n, W), F32),
        scratch_shapes=[pltpu.VMEM((n + 2 * HALO, W), F32)],
        compiler_params=_params("arbitrary"),
        name="pool_mixer",
    )(u, lw["w_pool"], lw["pool_scale"])


def _mix_ffn_kernel(*refs, final):
    if final:
        (x_ref, ya_ref, yb_ref, yc_ref, yd_ref, mod_ref, g2_ref, wo_ref, wg_ref, wu_ref, wd_ref,
         gf_ref, o_ref) = refs
    else:
        (x_ref, ya_ref, yb_ref, yc_ref, yd_ref, mod_ref, g2_ref, wo_ref, wg_ref, wu_ref, wd_ref,
         o_ref) = refs
    mod = mod_ref[0]
    gate1 = mod[:, 2 * D_MODEL:3 * D_MODEL]
    sh2 = mod[:, 3 * D_MODEL:4 * D_MODEL]
    sc2 = mod[:, 4 * D_MODEL:5 * D_MODEL]
    gate2 = mod[:, 5 * D_MODEL:6 * D_MODEL]
    mix = None
    for i, y_ref in enumerate((ya_ref, yb_ref, yc_ref, yd_ref)):
        part = _dot(y_ref[...].astype(BF16), wo_ref[i * GROUP_WIDTH:(i + 1) * GROUP_WIDTH, :])
        mix = part if mix is None else mix + part
    x1 = x_ref[...] + gate1 * mix
    h = _rms_rows(x1, D_MODEL) * g2_ref[...]
    hb = (h * (1.0 + sc2) + sh2).astype(BF16)
    ff = None
    for lo, hi in FF_CHUNKS:
        g = _dot(hb, wg_ref[:, lo:hi])
        up = _dot(hb, wu_ref[:, lo:hi])
        act = ((g * jax.nn.sigmoid(g)) * up).astype(BF16)
        part = _dot(act, wd_ref[lo:hi, :])
        ff = part if ff is None else ff + part
    x2 = x1 + gate2 * ff
    if final:
        x2 = _rms_rows(x2, D_MODEL) * gf_ref[...]
    o_ref[...] = x2


def _mix_ffn(x, ys, mod, lw, gf, *, nb, n, final):
    T = nb * n
    tm = TOKEN_TILE
    npt = n // tm

    def tok(width):
        return pl.BlockSpec((tm, width), lambda i: (i, 0))

    wnames = ("g2", "w_out", "w_gate_ff", "w_up_ff", "w_down")
    in_specs = [tok(D_MODEL), tok(256), tok(256), tok(256), tok(256),
                mod.spec(lambda i: i // npt)] + [lw.spec(nm) for nm in wnames]
    args = [x, *ys, mod.table] + [lw[nm] for nm in wnames]
    if final:
        in_specs.append(_resident((1, D_MODEL)))
        args.append(gf)
    return pl.pallas_call(
        functools.partial(_mix_ffn_kernel, final=final),
        grid=(T // tm,),
        in_specs=in_specs,
        out_specs=tok(D_MODEL),
        out_shape=jax.ShapeDtypeStruct((T, D_MODEL), F32),
        compiler_params=_params("arbitrary"),
        name="mix_ffn_final" if final else "mix_ffn",
    )(*args)


def _block_diag(w):
    L, G, c, e = w.shape
    return jnp.einsum('lgce,gh->lgche', w, jnp.eye(G, dtype=w.dtype)).reshape(L, G * c, G * e)


def _rot_cols(w):
    return jnp.concatenate([-w[..., 16:32], w[..., 0:16]], axis=-1)


def _stack_weights(p):
    w_in = p["w_in"]
    o1 = MLA_Q_RANK
    o2 = o1 + MLA_KV_RANK
    o3 = o2 + MLA_ROPE
    c_q, c_kv, k_r, rest = w_in[..., :o1], w_in[..., o1:o2], w_in[..., o2:o3], w_in[..., o3:]
    z = lambda n: jnp.zeros((DEPTH, D_MODEL, n), F32)
    w_in_eff = jnp.concatenate([c_q, k_r, z(32), c_kv, z(64), _rot_cols(k_r), z(32), rest], axis=-1)

    w_uq = p["mla_w_uq"]
    qd = MLA_NOPE + MLA_ROPE
    wq_parts, wqr_parts = [], []
    zq = lambda n: jnp.zeros((DEPTH, MLA_Q_RANK, n), F32)
    for h in range(MLA_HEADS):
        wh = w_uq[..., h * qd:(h + 1) * qd]
        wq_parts += [wh, zq(MLA_SLOT - qd)]
        wqr_parts += [zq(MLA_NOPE), _rot_cols(wh[..., MLA_NOPE:]), zq(MLA_SLOT - qd)]
    pad_rows = lambda w: jnp.pad(w, ((0, 0), (0, 256 - MLA_Q_RANK), (0, 0)))
    w_ukv = p["mla_w_ukv"]
    wk_parts, wv_parts = [], []
    zk = jnp.zeros((DEPTH, MLA_KV_RANK, MLA_SLOT - MLA_NOPE), F32)
    for h in range(MLA_HEADS):
        base = h * (MLA_NOPE + MLA_V)
        wk_parts += [w_ukv[..., base:base + MLA_NOPE], zk]
        wv_parts.append(w_ukv[..., base + MLA_NOPE:base + MLA_NOPE + MLA_V])

    w_r, w_i, b_r, b_i = p["lru_w_r"], p["lru_w_i"], p["lru_b_r"], p["lru_b_i"]
    w_gate = jnp.concatenate([_block_diag(w_r[:, 0]), _block_diag(w_r[:, 1]),
                              _block_diag(w_i[:, 0]), _block_diag(w_i[:, 1])], axis=-1)
    b_gate = jnp.concatenate([b_r[:, 0], b_r[:, 1], b_i[:, 0], b_i[:, 1]], axis=-1)
    w_gu = p["w_gu"]
    row = lambda v: v[:, None, :]
    return {
        "g1": row(p["norm1_g"]),
        "g2": row(p["norm2_g"]),
        "w_in": w_in_eff.astype(BF16),
        "gq": row(jnp.pad(p["mla_q_norm_g"], ((0, 0), (0, 256 - MLA_Q_RANK)))),
        "gkv": row(p["mla_kv_norm_g"]),
        "wq": pad_rows(jnp.concatenate(wq_parts, axis=-1)).astype(BF16),
        "wqr": pad_rows(jnp.concatenate(wqr_parts, axis=-1)).astype(BF16),
        "wkv": jnp.concatenate(wk_parts + wv_parts, axis=-1).astype(BF16),
        "conv_w": p["lru_conv_w"],
        "conv_b": row(p["lru_conv_b"]),
        "w_gate": w_gate.astype(BF16),
        "b_gate": row(b_gate),
        "lru_lambda": p["lru_lambda"],
        "w_pool": _block_diag(p["pool_w"]).astype(BF16),
        "pool_scale": row(p["pool_scale"]),
        "diff_lambda": p["diff_lambda"],
        "diff_g": row(jnp.tile(p["diff_norm_g"], (1, DIFF_HEADS))),
        "w_out": p["w_out"].astype(BF16),
        "w_gate_ff": w_gu[..., :FF_HIDDEN].astype(BF16),
        "w_up_ff": w_gu[..., FF_HIDDEN:].astype(BF16),
        "w_down": p["w_down"].astype(BF16),
    }


def _rope_tables(n, positional):
    quarter = MLA_ROPE // 4
    if positional:
        t = jnp.arange(n)
        row = (t // GRID_W).astype(F32)
        col = (t % GRID_W).astype(F32)
        inv = ROPE_BASE ** (-jnp.arange(quarter, dtype=F32) / quarter)
        ang = jnp.concatenate([row[:, None] * inv, col[:, None] * inv], axis=-1)
        cos, sin = jnp.cos(ang), jnp.sin(ang)
    else:
        cos, sin = jnp.ones((n, 16), F32), jnp.zeros((n, 16), F32)
    scale = LOG2E / math.sqrt(MLA_NOPE + MLA_ROPE)
    place = np.zeros((32, TAB_WIDTH), np.float32)
    offset = np.zeros((1, TAB_WIDTH), np.float32)
    offset[0, 0:64] = offset[0, 96:128] = scale
    for i in range(16):
        for half in (64, 80):
            place[i, half + i] = scale
            place[16 + i, 128 + half + i] = scale
            place[i, 256 + half + i] = 1.0
            place[16 + i, 384 + half + i] = 1.0
        for grp in range(8):
            place[i, 512 + 32 * grp + i] = place[i, 512 + 32 * grp + 16 + i] = 1.0
            place[16 + i, 768 + 32 * grp + i] = -1.0
            place[16 + i, 1024 + 32 * grp + 16 + i] = 1.0
    return jnp.dot(jnp.concatenate([cos, sin], axis=1), place, precision=lax.Precision.HIGHEST) + offset


def _layer(x, mod, lw, tabs, layer_idx, ctx, gf, *, nb, n, final):
    emit_cache = ctx is None
    tok_nb, tok_n = (1, nb * n) if mod.shared else (nb, n)
    outs = _inproj(x, mod, lw, tabs, nb=tok_nb, n=tok_n, emit_cache=emit_cache)
    q, k, vt, u_lru, u_pool, dq, dk, dvt = outs[:8]
    lam_init = 0.8 - 0.6 * math.exp(-0.3 * layer_idx)
    if ctx is None:
        h0 = jnp.zeros((1, 2, LRU_WIDTH), F32)
        h0_block = lambda b: 0
        mla_ctx = diff_ctx = None
    else:
        ckv, kr_pad, cdk, cdv, h0 = ctx
        p = ckv.shape[0] // (nb * DEPTH)
        h0_block = lambda b: b * DEPTH + layer_idx
        kc, vtc, dkc, dvtc = _ctx_prep(ckv, kr_pad, cdk, cdv, lw, nb=nb, p=p)
        mla_ctx = (kc, vtc)
        diff_ctx = (dkc, dvtc)
    y_mla = _mla_attn(q, k, vt, mla_ctx, nb=nb, n=n)
    y_lru, st = _lru(u_lru, h0, h0_block, lw, nb=nb, n=n)
    y_pool = _pool(u_pool, lw, nb=nb, n=n)
    y_diff = _diff_attn(dq, dk, dvt, diff_ctx, lw, nb=nb, n=n, lam_init=lam_init)
    x2 = _mix_ffn(x, (y_mla, y_lru, y_pool, y_diff), mod, lw, gf, nb=tok_nb, n=tok_n, final=final)
    cache = (outs[8], outs[9][:, 64:96], outs[10], outs[11], st) if emit_cache else None
    return x2, cache


def kernel(x_prompt, x_sample, cache_mla_ckv, cache_mla_krope, cache_diff_k, cache_diff_v, state_lru,
           c, c_ctx, w_ada, b_ada, norm1_g, norm2_g, w_in, mla_q_norm_g, mla_w_uq, mla_kv_norm_g,
           mla_w_ukv, lru_conv_w, lru_conv_b, lru_w_r, lru_b_r, lru_w_i, lru_b_i, lru_lambda, pool_w,
           pool_scale, diff_lambda, diff_norm_g, w_out, w_gu, w_down, final_norm_g):
    p = {
        "norm1_g": norm1_g, "norm2_g": norm2_g, "w_in": w_in, "mla_q_norm_g": mla_q_norm_g,
        "mla_w_uq": mla_w_uq, "mla_kv_norm_g": mla_kv_norm_g, "mla_w_ukv": mla_w_ukv,
        "lru_conv_w": lru_conv_w, "lru_conv_b": lru_conv_b, "lru_w_r": lru_w_r, "lru_b_r": lru_b_r,
        "lru_w_i": lru_w_i, "lru_b_i": lru_b_i, "lru_lambda": lru_lambda, "pool_w": pool_w,
        "pool_scale": pool_scale, "diff_lambda": diff_lambda, "diff_norm_g": diff_norm_g,
        "w_out": w_out, "w_gu": w_gu, "w_down": w_down,
    }
    Bp, Np, _ = x_prompt.shape
    Bs, Ns, _ = x_sample.shape
    P = cache_mla_ckv.shape[2]

    cond_all = jnp.concatenate([c, c_ctx[None, :], jnp.zeros((MOD_ROWS - Bs - 1, D_MODEL), F32)], axis=0)
    mod_table = _ada(cond_all, w_ada, b_ada).reshape(DEPTH * MOD_ROWS, 1, 6 * D_MODEL)
    tabs_p = _rope_tables(Bp * Np, positional=False)
    tabs_s = _rope_tables(Ns, positional=True)
    kr_pad = jnp.pad(cache_mla_krope, ((0, 0), (0, 0), (0, 0), (MLA_NOPE, MLA_SLOT - MLA_NOPE - MLA_ROPE)))
    flat = lambda a, w: a.reshape(Bs * DEPTH * P, w)
    ctx = (flat(cache_mla_ckv, MLA_KV_RANK), flat(kr_pad, MLA_SLOT), flat(cache_diff_k, 256),
           flat(cache_diff_v, 256), state_lru.reshape(Bs * DEPTH, 2, LRU_WIDTH))
    gf = final_norm_g[None, :]
    stacked = _stack_weights(p)

    xp = x_prompt.reshape(Bp * Np, D_MODEL)
    xs = x_sample.reshape(Bs * Ns, D_MODEL)
    caches = []
    for l in range(DEPTH):
        lw = _LayerWeights(stacked, l)
        final = l == DEPTH - 1
        mod_p = _Mod(mod_table, l * MOD_ROWS + Bs, shared=True)
        mod_s = _Mod(mod_table, l * MOD_ROWS, shared=False)
        xp, cache = _layer(xp, mod_p, lw, tabs_p, l, None, gf, nb=Bp, n=Np, final=final)
        caches.append(cache)
        xs, _ = _layer(xs, mod_s, lw, tabs_s, l, ctx, gf, nb=Bs, n=Ns, final=final)

    stack = lambda i, w: jnp.stack([cc[i].reshape(Bp, Np, w) for cc in caches], axis=1)
    new_mla_ckv = stack(0, MLA_KV_RANK)
    new_mla_krope = stack(1, MLA_ROPE)
    new_diff_k = stack(2, 256).reshape(Bp, DEPTH, Np, DIFF_HEADS, 2, DIFF_DIM)
    new_diff_v = stack(3, 256).reshape(Bp, DEPTH, Np, DIFF_HEADS, 2 * DIFF_DIM)
    new_state_lru = jnp.stack([cc[4] for cc in caches], axis=1)
    return (xp.reshape(Bp, Np, D_MODEL), xs.reshape(Bs, Ns, D_MODEL),
            new_mla_ckv, new_mla_krope, new_diff_k, new_diff_v, new_state_lru)
```

```python
import functools
import math

import jax
import jax.numpy as jnp
import numpy as np
from jax import lax
from jax.experimental import pallas as pl
from jax.experimental.pallas import tpu as pltpu

F32 = jnp.float32
BF16 = jnp.bfloat16

D_MODEL = 1024
DEPTH = 2
GRID_W = 64
GROUP_WIDTH = 256
MLA_HEADS = 4
MLA_NOPE = 64
MLA_ROPE = 32
MLA_V = 64
MLA_Q_RANK = 192
MLA_KV_RANK = 128
MLA_SLOT = 128
LRU_WIDTH = 256
LRU_C = 8.0
POOL_WINDOWS = (2, 4, 8, 16)
POOL_CH = 64
DIFF_HEADS = 4
DIFF_DIM = 32
HEAD_V = 64
FF_HIDDEN = 2816
FF_CHUNKS = ((0, 1536), (1536, 2816))
ROPE_BASE = 10000.0
EPS = 1e-6
IN_EFF = 2048
HALO = 8
SCAN_RUN = 4
VT_ROWS = 80
ATT_TQ = 256
TOKEN_TILE = 512
TAB_WIDTH = 4 * 128 + 3 * 256
MOD_ROWS = 16
LOG2E = math.log2(math.e)

VMEM_LIMIT_BYTES = 56 * 1024 * 1024

_NT = (((1,), (1,)), ((), ()))


def _params(*sem):
    return pltpu.CompilerParams(dimension_semantics=sem, vmem_limit_bytes=VMEM_LIMIT_BYTES)


def _resident(shape):
    zeros = (0,) * len(shape)
    return pl.BlockSpec(shape, lambda *_: zeros, pipeline_mode=pl.Buffered(1))


def _dot(a, b):
    return jnp.dot(a, b, preferred_element_type=F32)


def _dot_nt(a, b):
    return lax.dot_general(a, b, _NT, preferred_element_type=F32)


def _rms_rows(x, width):
    ms = jnp.sum(x * x, axis=-1, keepdims=True) * (1.0 / width)
    return x * lax.rsqrt(ms + EPS)


def _store_vt(vt_ref, v):
    vt = v.T
    rows = v.shape[0]
    pad = VT_ROWS - HEAD_V
    ones_row = jnp.where(lax.broadcasted_iota(jnp.int32, (pad, rows), 0) == 0, 1.0, 0.0).astype(BF16)
    for hh in range(vt_ref.shape[0]):
        vt_ref[hh, 0:HEAD_V, :] = vt[hh * HEAD_V:(hh + 1) * HEAD_V, :].astype(BF16)
        vt_ref[hh, HEAD_V:VT_ROWS, :] = ones_row


class _Mod:
    def __init__(self, table, row0, shared):
        self.table, self.row0, self.shared = table, row0, shared

    def spec(self, batch_of):
        row0 = self.row0
        if self.shared:
            return pl.BlockSpec((1, 1, 6 * D_MODEL), lambda *g: (row0, 0, 0))
        return pl.BlockSpec((1, 1, 6 * D_MODEL), lambda *g: (row0 + batch_of(*g), 0, 0))


class _LayerWeights:
    def __init__(self, stacked, layer):
        self.stacked, self.layer = stacked, layer

    def __getitem__(self, name):
        return self.stacked[name]

    def spec(self, name, col_blocks=1, col_block=0):
        layer = self.layer
        _, rows, cols = self.stacked[name].shape
        return pl.BlockSpec((None, rows, cols // col_blocks), lambda *_: (layer, 0, col_block),
                            pipeline_mode=pl.Buffered(1))


def _ada_kernel(cond_ref, w_ref, b_ref, out_ref):
    c = cond_ref[...]
    s = c * jax.nn.sigmoid(c)
    out_ref[0] = _dot(s.astype(BF16), w_ref[0].astype(BF16)) + b_ref[0]


def _ada(cond_all, w_ada, b_ada):
    rows = cond_all.shape[0]
    tn = 1536
    return pl.pallas_call(
        _ada_kernel,
        grid=(DEPTH, 6 * D_MODEL // tn),
        in_specs=[
            pl.BlockSpec((rows, D_MODEL), lambda l, j: (0, 0)),
            pl.BlockSpec((1, D_MODEL, tn), lambda l, j: (l, 0, j)),
            pl.BlockSpec((1, 1, tn), lambda l, j: (l, 0, j)),
        ],
        out_specs=pl.BlockSpec((1, rows, tn), lambda l, j: (l, 0, j)),
        out_shape=jax.ShapeDtypeStruct((DEPTH, rows, 6 * D_MODEL), F32),
        compiler_params=_params("arbitrary", "arbitrary"),
        name="ada_mod",
    )(cond_all, w_ada, b_ada.reshape(DEPTH, 1, 6 * D_MODEL))


def _inproj_kernel(x_ref, mod_ref, g1_ref, win_ref, gq_ref, gkv_ref, wq_ref, wqr_ref, wkv_ref, tab_ref,
                   q_out, k_out, vt_out, lru_out, pool_out, dq_out, dk_out, dvt_out, *cache_outs):
    cosq_ref, sinq_ref, cosk_ref, sink_ref = (tab_ref.at[:, i * 128:(i + 1) * 128] for i in range(4))
    cosd_ref, sina_ref, sinb_ref = (tab_ref.at[:, 512 + i * 256:768 + i * 256] for i in range(3))
    x = x_ref[...]
    mod = mod_ref[0]
    sh1 = mod[:, 0:D_MODEL]
    sc1 = mod[:, D_MODEL:2 * D_MODEL]
    h = _rms_rows(x, D_MODEL) * g1_ref[...]
    hb = (h * (1.0 + sc1) + sh1).astype(BF16)

    u_mla = _dot(hb, win_ref[:, 0:512])
    u_pd = _dot(hb, win_ref[:, 1024:1536])
    u_kv = _dot(hb, win_ref[:, 1536:2048])
    t01 = u_mla[:, 0:256]
    lane = lax.broadcasted_iota(jnp.int32, (1, 256), 1)
    cq = jnp.where(lane < MLA_Q_RANK, t01, 0.0)
    cqn = (_rms_rows(cq, MLA_Q_RANK) * gq_ref[...]).astype(BF16)
    qa = _dot(cqn, wq_ref[...])
    qr = _dot(cqn, wqr_ref[...])
    cosq = cosq_ref[...]
    sinq = sinq_ref[...]
    ckv = u_mla[:, 256:384]
    lat = _rms_rows(ckv, MLA_KV_RANK) * gkv_ref[...]
    latb = lat.astype(BF16)
    kkv = _dot(latb, wkv_ref[...])
    kk = kkv[:, 0:MLA_HEADS * MLA_SLOT]
    _store_vt(vt_out, kkv[:, MLA_HEADS * MLA_SLOT:])
    t1 = t01[:, 128:256]
    t3 = u_mla[:, 384:512]
    kro = t1 * cosk_ref[...] + t3 * sink_ref[...]
    for hh in range(MLA_HEADS):
        sl = slice(hh * MLA_SLOT, (hh + 1) * MLA_SLOT)
        q_out[hh] = (qa[:, sl] * cosq + qr[:, sl] * sinq).astype(q_out.dtype)
        k_out[hh] = (kk[:, sl] + kro).astype(k_out.dtype)

    lru_out[...] = _dot(hb, win_ref[:, 512:1024])
    pool_out[...] = u_pd[:, 0:256]

    cosd = cosd_ref[...]
    sina = sina_ref[...]
    sinb = sinb_ref[...]

    def rope(t):
        return t * cosd + pltpu.roll(t, 256 - 16, 1) * sina + pltpu.roll(t, 16, 1) * sinb

    dq = u_pd[:, 256:512]
    dk = u_kv[:, 0:256]
    dv = u_kv[:, 256:512]
    dq_out[...] = (rope(dq) * (LOG2E / math.sqrt(DIFF_DIM))).astype(dq_out.dtype)
    dk_out[...] = rope(dk).astype(dk_out.dtype)
    _store_vt(dvt_out, dv)

    if cache_outs:
        lat_out, kr_out, dk_raw_out, dv_raw_out = cache_outs
        lat_out[...] = lat
        kr_out[...] = t1
        dk_raw_out[...] = dk
        dv_raw_out[...] = dv


def _inproj(x, mod, lw, tabs, *, nb, n, emit_cache):
    T = nb * n
    tm = TOKEN_TILE
    npt = n // tm
    row_blk = lambda j, b: b * npt + j

    def tok(width):
        return pl.BlockSpec((tm, width), lambda j, b: (row_blk(j, b), 0))

    def tab(width):
        return pl.BlockSpec((tm, width), lambda j, b: (j, 0))

    head = pl.BlockSpec((MLA_HEADS, tm, MLA_SLOT), lambda j, b: (0, row_blk(j, b), 0))
    vt_spec = pl.BlockSpec((MLA_HEADS, VT_ROWS, tm), lambda j, b: (0, 0, row_blk(j, b)))
    wnames = ("g1", "w_in", "gq", "gkv", "wq", "wqr", "wkv")
    in_specs = [tok(D_MODEL), mod.spec(lambda j, b: b)] + [lw.spec(nm) for nm in wnames] + [
        tab(TAB_WIDTH)]
    out_specs = [head, head, vt_spec, tok(512), tok(256), tok(256), tok(256), vt_spec]
    vt_shape = jax.ShapeDtypeStruct((MLA_HEADS, VT_ROWS, T), BF16)
    out_shape = [
        jax.ShapeDtypeStruct((MLA_HEADS, T, MLA_SLOT), BF16),
        jax.ShapeDtypeStruct((MLA_HEADS, T, MLA_SLOT), BF16),
        vt_shape,
        jax.ShapeDtypeStruct((T, 512), F32),
        jax.ShapeDtypeStruct((T, 256), F32),
        jax.ShapeDtypeStruct((T, 256), BF16),
        jax.ShapeDtypeStruct((T, 256), BF16),
        vt_shape,
    ]
    if emit_cache:
        out_specs += [tok(128), tok(128), tok(256), tok(256)]
        out_shape += [jax.ShapeDtypeStruct((T, 128), F32), jax.ShapeDtypeStruct((T, 128), F32),
                      jax.ShapeDtypeStruct((T, 256), F32), jax.ShapeDtypeStruct((T, 256), F32)]
    return pl.pallas_call(
        _inproj_kernel,
        grid=(npt, nb),
        in_specs=in_specs,
        out_specs=out_specs,
        out_shape=out_shape,
        compiler_params=_params("arbitrary", "arbitrary"),
        name="inproj_cache" if emit_cache else "inproj",
    )(x, mod.table, *[lw[nm] for nm in wnames], tabs)


def _ctx_prep_kernel(ckv_ref, kr_ref, dk_ref, dv_ref, wkv_ref, k_out, vt_out, dk_out, dvt_out):
    latb = ckv_ref[...].astype(BF16)
    kkv = _dot(latb, wkv_ref[...])
    kk = kkv[:, 0:MLA_HEADS * MLA_SLOT]
    kr = kr_ref[...]
    for hh in range(MLA_HEADS):
        k_out[hh] = (kk[:, hh * MLA_SLOT:(hh + 1) * MLA_SLOT] + kr).astype(k_out.dtype)
    _store_vt(vt_out, kkv[:, MLA_HEADS * MLA_SLOT:])
    dk_out[...] = dk_ref[...].astype(dk_out.dtype)
    _store_vt(dvt_out, dv_ref[...])


def _ctx_prep(ckv, kr_pad, cdk, cdv, lw, *, nb, p):
    T = nb * p
    layer = lw.layer
    cache_row = lambda w: pl.BlockSpec((p, w), lambda b: (b * DEPTH + layer, 0))
    row = lambda w: pl.BlockSpec((p, w), lambda b: (b, 0))
    vt_spec = pl.BlockSpec((MLA_HEADS, VT_ROWS, p), lambda b: (0, 0, b))
    vt_shape = jax.ShapeDtypeStruct((MLA_HEADS, VT_ROWS, T), BF16)
    return pl.pallas_call(
        _ctx_prep_kernel,
        grid=(nb,),
        in_specs=[cache_row(128), cache_row(128), cache_row(256), cache_row(256),
                  lw.spec("wkv")],
        out_specs=[pl.BlockSpec((MLA_HEADS, p, MLA_SLOT), lambda b: (0, b, 0)), vt_spec, row(256), vt_spec],
        out_shape=[jax.ShapeDtypeStruct((MLA_HEADS, T, MLA_SLOT), BF16), vt_shape,
                   jax.ShapeDtypeStruct((T, 256), BF16), vt_shape],
        compiler_params=_params("arbitrary"),
        name="ctx_prep",
    )(ckv, kr_pad, cdk, cdv, lw["wkv"])


SAFE_DENOM = 2.0 ** -60
BOUND_SLACK = 1.02


def _scores(k_new, k_ctx, q):
    sn = _dot_nt(k_new(), q)
    sc = _dot_nt(k_ctx(), q) if k_ctx is not None else None
    return sn, sc


def _exact_shift(k_new, k_ctx, q):
    sn, sc = _scores(k_new, k_ctx, q)
    m = jnp.max(sn, axis=0, keepdims=True)
    if sc is not None:
        m = jnp.maximum(m, jnp.max(sc, axis=0, keepdims=True))
    return m


def _bound_shift(q, key_norm2):
    qf = q.astype(F32)
    ones = jnp.ones((8, q.shape[1]), BF16)
    q_norm2 = _dot_nt(ones, (qf * qf).astype(BF16))[0:1, :]
    return jnp.sqrt(q_norm2 * key_norm2) * BOUND_SLACK


def _max_row_norm2(k_new, k_ctx, col_sum):
    def one(k):
        kf = k.astype(F32)
        return jnp.max(_dot((kf * kf).astype(BF16), col_sum), axis=0, keepdims=True)
    m = one(k_new)
    if k_ctx is not None:
        m = jnp.maximum(m, one(k_ctx))
    return m * BOUND_SLACK


def _exp_stage(e_buf, k_new, k_ctx, q, shift, n_ctx):
    sn, sc = _scores(k_new, k_ctx, q)
    e_buf[n_ctx:, :] = jnp.exp2(sn - shift).astype(BF16)
    if sc is not None:
        e_buf[0:n_ctx, :] = jnp.exp2(sc - shift).astype(BF16)


def _value_stage(e_buf, vt_new, vt_ctx, n_ctx):
    o = _dot(vt_new(), e_buf[n_ctx:, :])
    if vt_ctx is not None:
        o = o + _dot(vt_ctx(), e_buf[0:n_ctx, :])
    return o


def _run_pipeline(n_maps, exp_stage, value_stage):
    exp_stage(0)
    for u in range(n_maps):
        if u + 1 < n_maps:
            exp_stage(u + 1)
        value_stage(u)


def _att_scratch(nk, key_shape):
    scratch = [pltpu.VMEM((8, 128), F32),
               pltpu.VMEM((MLA_HEADS * HEAD_V, ATT_TQ), F32),
               pltpu.VMEM((nk, ATT_TQ), BF16), pltpu.VMEM((nk, ATT_TQ), BF16)]
    if key_shape is not None:
        scratch += [pltpu.VMEM(key_shape, BF16), pltpu.VMEM((MLA_HEADS, VT_ROWS, nk), BF16)]
    return scratch


def _att_nsub(n):
    return 2 if n % (2 * ATT_TQ) == 0 else 1


def _mla_attn_kernel(*refs, has_ctx, nsub):
    if has_ctx:
        q_ref, k_ref, vt_ref, kc_ref, vtc_ref, o_ref, kn2, ot, e0, e1, keys, vals = refs
        n_ctx = kc_ref.shape[1]
    else:
        q_ref, k_ref, vt_ref, o_ref, kn2, ot, e0, e1 = refs
        keys, vals = k_ref, vt_ref
    e_bufs = (e0, e1)

    @pl.when(pl.program_id(1) == 0)
    def _():
        ones = jnp.ones((MLA_SLOT, 128), BF16)
        for hh in range(MLA_HEADS):
            kn2[hh:hh + 1, :] = _max_row_norm2(k_ref[hh], kc_ref[hh] if has_ctx else None, ones)
        if has_ctx:
            keys[:, 0:n_ctx, :] = kc_ref[...]
            keys[:, n_ctx:, :] = k_ref[...]
            vals[:, :, 0:n_ctx] = vtc_ref[...]
            vals[:, :, n_ctx:] = vt_ref[...]

    def run(exact):
        denoms = []

        def exp_stage(u):
            t, hh = divmod(u, MLA_HEADS)
            q = q_ref[hh, t * ATT_TQ:(t + 1) * ATT_TQ, :]
            k_all = lambda: keys[hh]
            shift = _exact_shift(k_all, None, q) if exact else _bound_shift(q, kn2[hh:hh + 1, 0:1])
            _exp_stage(e_bufs[u % 2], k_all, None, q, shift, 0)

        def value_stage(u):
            t, hh = divmod(u, MLA_HEADS)
            o = _value_stage(e_bufs[u % 2], lambda: vals[hh], None, 0)
            denom = o[HEAD_V:HEAD_V + 1, :]
            denoms.append(denom)
            ot[hh * HEAD_V:(hh + 1) * HEAD_V, :] = o[0:HEAD_V, :] * (1.0 / denom)
            if hh == MLA_HEADS - 1:
                o_ref[t * ATT_TQ:(t + 1) * ATT_TQ, :] = ot[...].T

        _run_pipeline(nsub * MLA_HEADS, exp_stage, value_stage)
        return jnp.min(functools.reduce(jnp.minimum, denoms))

    denom_min = run(exact=False)

    @pl.when(jnp.logical_not(denom_min >= SAFE_DENOM))
    def _():
        run(exact=True)


def _mla_attn(q, k, vt, ctx, *, nb, n):
    nsub = _att_nsub(n)
    tq = nsub * ATT_TQ
    npt = n // tq
    H, S = MLA_HEADS, MLA_SLOT
    in_specs = [
        pl.BlockSpec((H, tq, S), lambda b, j: (0, b * npt + j, 0)),
        pl.BlockSpec((H, n, S), lambda b, j: (0, b, 0)),
        pl.BlockSpec((H, VT_ROWS, n), lambda b, j: (0, 0, b)),
    ]
    args = [q, k, vt]
    n_ctx = 0
    if ctx is not None:
        n_ctx = ctx[0].shape[1] // nb
        in_specs += [
            pl.BlockSpec((H, n_ctx, S), lambda b, j: (0, b, 0)),
            pl.BlockSpec((H, VT_ROWS, n_ctx), lambda b, j: (0, 0, b)),
        ]
        args += list(ctx)
    return pl.pallas_call(
        functools.partial(_mla_attn_kernel, has_ctx=ctx is not None, nsub=nsub),
        grid=(nb, npt),
        in_specs=in_specs,
        out_specs=pl.BlockSpec((tq, 256), lambda b, j: (b * npt + j, 0)),
        out_shape=jax.ShapeDtypeStruct((nb * n, 256), F32),
        scratch_shapes=_att_scratch(n + n_ctx, (H, n + n_ctx, S) if ctx is not None else None),
        compiler_params=_params("arbitrary", "arbitrary"),
        name="mla_attn_ctx" if ctx is not None else "mla_attn",
    )(*args)


def _diff_attn_kernel(*refs, has_ctx, nsub, lam_init):
    if has_ctx:
        lv_ref, g_ref, q_ref, k_ref, vt_ref, kc_ref, vtc_ref, o_ref, kn2, ot, e0, e1, keys, vals = refs
        n_ctx = kc_ref.shape[0]
    else:
        lv_ref, g_ref, q_ref, k_ref, vt_ref, o_ref, kn2, ot, e0, e1 = refs
        keys, vals = k_ref, vt_ref
    e_bufs = (e0, e1)
    lv = lv_ref[...]
    lam = (jnp.exp(jnp.sum(lv[0:1] * lv[1:2], axis=-1, keepdims=True))
           - jnp.exp(jnp.sum(lv[2:3] * lv[3:4], axis=-1, keepdims=True)) + lam_init)
    lane128 = lax.broadcasted_iota(jnp.int32, (1, 128), 1)
    n_pairs = 2 * DIFF_HEADS

    @pl.when(pl.program_id(1) == 0)
    def _():
        dim = lax.broadcasted_iota(jnp.int32, (256, 128), 0)
        col = lax.broadcasted_iota(jnp.int32, (256, 128), 1)
        indicator = jnp.where(dim // DIFF_DIM == col, 1.0, 0.0).astype(BF16)
        kn2[0:1, :] = _max_row_norm2(k_ref[...], kc_ref[...] if has_ctx else None, indicator)
        if has_ctx:
            keys[0:n_ctx, :] = kc_ref[...]
            keys[n_ctx:, :] = k_ref[...]
            vals[:, :, 0:n_ctx] = vtc_ref[...]
            vals[:, :, n_ctx:] = vt_ref[...]

    def run(exact):
        denoms = []
        outs = {}

        def exp_stage(u):
            t, p = divmod(u, n_pairs)
            tile = slice((p * DIFF_DIM // 128) * 128, (p * DIFF_DIM // 128 + 1) * 128)
            k_new = lambda: keys[:, tile]
            k_ctx = None
            q = q_ref[t * ATT_TQ:(t + 1) * ATT_TQ, tile]
            lo = p * DIFF_DIM - tile.start
            in_pair = (lane128 >= lo) & (lane128 < lo + DIFF_DIM)
            qm = jnp.where(in_pair, q, jnp.zeros_like(q))
            shift = _exact_shift(k_new, k_ctx, qm) if exact else _bound_shift(qm, kn2[0:1, p:p + 1])
            _exp_stage(e_bufs[u % 2], k_new, k_ctx, qm, shift, 0)

        def value_stage(u):
            t, p = divmod(u, n_pairs)
            hh = p // 2
            o = _value_stage(e_bufs[u % 2], lambda: vals[hh], None, 0)
            denom = o[HEAD_V:HEAD_V + 1, :]
            denoms.append(denom)
            outs[u] = (o[0:HEAD_V, :], denom)
            if p % 2 == 1:
                (o0, l0), (o1, l1) = outs.pop(u - 1), outs.pop(u)
                o = o0 * (1.0 / l0) - o1 * (lam / l1)
                msq = jnp.sum(o * o, axis=0, keepdims=True) * (1.0 / HEAD_V)
                ot[hh * HEAD_V:(hh + 1) * HEAD_V, :] = o * lax.rsqrt(msq + EPS)
            if p == n_pairs - 1:
                o_ref[t * ATT_TQ:(t + 1) * ATT_TQ, :] = (ot[...].T * g_ref[...]) * (1.0 - lam_init)

        _run_pipeline(nsub * n_pairs, exp_stage, value_stage)
        return jnp.min(functools.reduce(jnp.minimum, denoms))

    denom_min = run(exact=False)

    @pl.when(jnp.logical_not(denom_min >= SAFE_DENOM))
    def _():
        run(exact=True)


def _diff_attn(q, k, vt, ctx, lw, *, nb, n, lam_init):
    nsub = 1
    tq = nsub * ATT_TQ
    npt = n // tq
    in_specs = [
        lw.spec("diff_lambda"),
        lw.spec("diff_g"),
        pl.BlockSpec((tq, 256), lambda b, j: (b * npt + j, 0)),
        pl.BlockSpec((n, 256), lambda b, j: (b, 0)),
        pl.BlockSpec((DIFF_HEADS, VT_ROWS, n), lambda b, j: (0, 0, b)),
    ]
    args = [lw["diff_lambda"], lw["diff_g"], q, k, vt]
    n_ctx = 0
    if ctx is not None:
        n_ctx = ctx[0].shape[0] // nb
        in_specs += [pl.BlockSpec((n_ctx, 256), lambda b, j: (b, 0)),
                     pl.BlockSpec((DIFF_HEADS, VT_ROWS, n_ctx), lambda b, j: (0, 0, b))]
        args += list(ctx)
    return pl.pallas_call(
        functools.partial(_diff_attn_kernel, has_ctx=ctx is not None, nsub=nsub, lam_init=lam_init),
        grid=(nb, npt),
        in_specs=in_specs,
        out_specs=pl.BlockSpec((tq, 256), lambda b, j: (b * npt + j, 0)),
        out_shape=jax.ShapeDtypeStruct((nb * n, 256), F32),
        scratch_shapes=_att_scratch(n + n_ctx, (n + n_ctx, 256) if ctx is not None else None),
        compiler_params=_params("arbitrary", "arbitrary"),
        name="diff_attn_ctx" if ctx is not None else "diff_attn",
    )(*args)


def _shift_rows(v, k):
    return pltpu.roll(v, (-k) % v.shape[0], 0)


def _scan_strided(a_ref, b_ref, h_ref, row0, carry, n_rows, reverse):
    sub = lax.broadcasted_iota(jnp.int32, (8, 128), 0)
    span = 8 * SCAN_RUN
    order = tuple(range(SCAN_RUN))[::-1] if reverse else tuple(range(SCAN_RUN))
    starts = tuple(range(0, n_rows, span))[::-1] if reverse else tuple(range(0, n_rows, span))
    carries = []
    for lt in range(a_ref.shape[0]):
        c_in = carry[:, lt * 128:(lt + 1) * 128]
        for start in starts:
            tile = lambda ref, g: ref[lt, pl.ds(row0 + start + g, 8, stride=SCAN_RUN), :]
            a = [tile(a_ref, g) for g in range(SCAN_RUN)]
            b = [tile(b_ref, g) for g in range(SCAN_RUN)]
            h = {order[0]: b[order[0]]}
            p = {order[0]: a[order[0]]}
            for prev, g in zip(order, order[1:]):
                h[g] = a[g] * h[prev] + b[g]
                p[g] = a[g] * p[prev]
            pi, hi = p[order[-1]], h[order[-1]]
            for s in (1, 2, 4):
                shift = 8 - s if reverse else s
                valid = (sub < 8 - s) if reverse else (sub >= s)
                pr, hr = pltpu.roll(pi, shift, 0), pltpu.roll(hi, shift, 0)
                hi = jnp.where(valid, pi * hr + hi, hi)
                pi = jnp.where(valid, pi * pr, pi)
            one = 7 if reverse else 1
            first = (sub == 7) if reverse else (sub == 0)
            pe = jnp.where(first, 1.0, pltpu.roll(pi, one, 0))
            he = jnp.where(first, 0.0, pltpu.roll(hi, one, 0))
            c = pe * c_in + he
            for g in range(SCAN_RUN):
                h_ref[lt, pl.ds(start + g, 8, stride=SCAN_RUN), :] = h[g] + p[g] * c
            last = 0 if reverse else 7
            c_in = pi[last:last + 1, :] * c_in + hi[last:last + 1, :]
        carries.append(c_in)
    return jnp.concatenate(carries, axis=1)


def _sigmoid(x):
    return 0.5 * jnp.tanh(0.5 * x) + 0.5


def _gelu_tanh(x):
    return x * (0.5 * (1.0 + jnp.tanh(math.sqrt(2.0 / math.pi) * (x + 0.044715 * (x * x * x)))))


def _lru_kernel(u_ref, h0_ref, cw_ref, cb_ref, wg_ref, bg_ref, lam_ref, y_ref, st_ref,
                xpad, a1s, b1s, a0c, b0c, hc, *, N, T):
    W = LRU_WIDTH
    nc = N // T
    tiles = [slice(lt * 128, (lt + 1) * 128) for lt in range(W // 128)]
    zeros = jnp.zeros((HALO, W), F32)
    xpad[0:HALO, :] = zeros
    xpad[N + HALO:N + 2 * HALO, :] = zeros

    def fill(j, carry):
        r0 = pl.multiple_of(j * T, T)
        xpad[pl.ds(r0 + HALO, T), :] = u_ref[pl.ds(r0, T), 0:W]
        return carry

    lax.fori_loop(0, nc, fill, 0)

    z = -lam_ref[...]
    sp = jnp.maximum(z, 0.0) + jnp.log1p(jnp.exp(-jnp.abs(z)))
    cw = cw_ref[...]
    cb = cb_ref[...]
    bg = bg_ref[...]

    def fwd(j, carry):
        r0 = pl.multiple_of(j * T, T)
        ext = xpad[pl.ds(r0, T + 2 * HALO), :]
        body = slice(HALO, HALO + T)
        xc = cb
        for tap in range(4):
            xc = xc + _shift_rows(ext, tap - 1)[body] * cw[tap:tap + 1]
        g = _sigmoid(_dot(xc.astype(BF16), wg_ref[...]) + bg)
        ab = []
        for d in range(2):
            r = g[:, d * W:(d + 1) * W]
            i = g[:, (2 + d) * W:(3 + d) * W]
            log_a = (-LRU_C * r) * sp[d:d + 1]
            a = jnp.exp(log_a)
            bt = (jnp.sqrt(1.0 - a * a) * i) * xc
            ab.append((a, bt))
        for lt, lanes in enumerate(tiles):
            a0c[lt] = ab[0][0][:, lanes]
            b0c[lt] = ab[0][1][:, lanes]
            a1s[lt, pl.ds(r0, T), :] = ab[1][0][:, lanes]
            b1s[lt, pl.ds(r0, T), :] = ab[1][1][:, lanes]
        carry = _scan_strided(a0c, b0c, hc, 0, carry, T, reverse=False)
        for lt, lanes in enumerate(tiles):
            y_ref[pl.ds(r0, T), lanes] = hc[lt]
        return carry

    cf = lax.fori_loop(0, nc, fwd, h0_ref[0, 0:1, :])

    def bwd(jj, carry):
        r0 = pl.multiple_of((nc - 1 - jj) * T, T)
        carry = _scan_strided(a1s, b1s, hc, r0, carry, T, reverse=True)
        for lt, lanes in enumerate(tiles):
            gb = u_ref[pl.ds(r0, T), W + lt * 128:W + (lt + 1) * 128]
            y_ref[pl.ds(r0, T), lanes] = (y_ref[pl.ds(r0, T), lanes] + hc[lt]) * _gelu_tanh(gb)
        return carry

    cbw = lax.fori_loop(0, nc, bwd, h0_ref[0, 1:2, :])
    st_ref[0, 0:1, :] = cf
    st_ref[0, 1:2, :] = cbw


def _lru(u, h0, h0_block, lw, *, nb, n):
    T = min(n, 256)
    W = LRU_WIDTH
    return pl.pallas_call(
        functools.partial(_lru_kernel, N=n, T=T),
        grid=(nb,),
        in_specs=[
            pl.BlockSpec((n, 2 * W), lambda b: (b, 0)),
            pl.BlockSpec((1, 2, W), lambda b: (h0_block(b), 0, 0)),
            lw.spec("conv_w"), lw.spec("conv_b"), lw.spec("w_gate"), lw.spec("b_gate"),
            lw.spec("lru_lambda"),
        ],
        out_specs=[
            pl.BlockSpec((n, W), lambda b: (b, 0)),
            pl.BlockSpec((1, 2, W), lambda b: (b, 0, 0)),
        ],
        out_shape=[
            jax.ShapeDtypeStruct((nb * n, W), F32),
            jax.ShapeDtypeStruct((nb, 2, W), F32),
        ],
        scratch_shapes=[
            pltpu.VMEM((n + 2 * HALO, W), F32),
            pltpu.VMEM((W // 128, n, 128), F32),
            pltpu.VMEM((W // 128, n, 128), F32),
            pltpu.VMEM((W // 128, T, 128), F32),
            pltpu.VMEM((W // 128, T, 128), F32),
            pltpu.VMEM((W // 128, T, 128), F32),
        ],
        compiler_params=_params("arbitrary"),
        name="rglru",
    )(u, h0, lw["conv_w"], lw["conv_b"], lw["w_gate"], lw["b_gate"], lw["lru_lambda"])


def _pool_kernel(u_ref, wp_ref, sc_ref, y_ref, xpad, *, N, T):
    W = GROUP_WIDTH
    nc = N // T
    zeros = jnp.zeros((HALO, W), F32)
    xpad[0:HALO, :] = zeros
    xpad[N + HALO:N + 2 * HALO, :] = zeros

    def fill(j, carry):
        r0 = pl.multiple_of(j * T, T)
        xpad[pl.ds(r0 + HALO, T), :] = u_ref[pl.ds(r0, T), :]
        return carry

    lax.fori_loop(0, nc, fill, 0)

    grp = lax.broadcasted_iota(jnp.int32, (1, W), 1) // POOL_CH
    half = jnp.where(grp == 0, 1, jnp.where(grp == 1, 2, jnp.where(grp == 2, 4, 8)))
    scale = sc_ref[...]

    def chunk(j, carry):
        r0 = pl.multiple_of(j * T, T)
        ext = xpad[pl.ds(r0, T + 2 * HALO), :]
        w2 = _shift_rows(ext, -1) + ext
        w4 = _shift_rows(w2, -1) + _shift_rows(w2, 1)
        w8 = _shift_rows(w4, -2) + _shift_rows(w4, 2)
        w16 = _shift_rows(w8, -4) + _shift_rows(w8, 4)
        ws = jnp.where(grp == 0, w2, jnp.where(grp == 1, w4, jnp.where(grp == 2, w8, w16)))
        body = slice(HALO, HALO + T)
        t = r0 + lax.broadcasted_iota(jnp.int32, (T, W), 0)
        cnt = (jnp.minimum(t + half, N) - jnp.maximum(t - half, 0)).astype(F32)
        d = ws[body] / cnt - ext[body]
        y_ref[pl.ds(r0, T), :] = _dot(d.astype(BF16), wp_ref[...]) * scale
        return carry

    lax.fori_loop(0, nc, chunk, 0)


def _pool(u, lw, *, nb, n):
    W = GROUP_WIDTH
    T = min(n, 256)
    return pl.pallas_call(
        functools.partial(_pool_kernel, N=n, T=T),
        grid=(nb,),
        in_specs=[pl.BlockSpec((n, W), lambda b: (b, 0)), lw.spec("w_pool"), lw.spec("pool_scale")],
        out_specs=pl.BlockSpec((n, W), lambda b: (b, 0)),
        out_shape=jax.ShapeDtypeStruct((nb * n, W), F32),
        scratch_shapes=[pltpu.VMEM((n + 2 * HALO, W), F32)],
        compiler_params=_params("arbitrary"),
        name="pool_mixer",
    )(u, lw["w_pool"], lw["pool_scale"])


def _mix_ffn_kernel(*refs, final):
    if final:
        (x_ref, ya_ref, yb_ref, yc_ref, yd_ref, mod_ref, g2_ref, wo_ref, wg_ref, wu_ref, wd_ref,
         gf_ref, o_ref) = refs
    else:
        (x_ref, ya_ref, yb_ref, yc_ref, yd_ref, mod_ref, g2_ref, wo_ref, wg_ref, wu_ref, wd_ref,
         o_ref) = refs
    mod = mod_ref[0]
    gate1 = mod[:, 2 * D_MODEL:3 * D_MODEL]
    sh2 = mod[:, 3 * D_MODEL:4 * D_MODEL]
    sc2 = mod[:, 4 * D_MODEL:5 * D_MODEL]
    gate2 = mod[:, 5 * D_MODEL:6 * D_MODEL]
    mix = None
    for i, y_ref in enumerate((ya_ref, yb_ref, yc_ref, yd_ref)):
        part = _dot(y_ref[...].astype(BF16), wo_ref[i * GROUP_WIDTH:(i + 1) * GROUP_WIDTH, :])
        mix = part if mix is None else mix + part
    x1 = x_ref[...] + gate1 * mix
    h = _rms_rows(x1, D_MODEL) * g2_ref[...]
    hb = (h * (1.0 + sc2) + sh2).astype(BF16)
    ff = None
    for lo, hi in FF_CHUNKS:
        g = _dot(hb, wg_ref[:, lo:hi])
        up = _dot(hb, wu_ref[:, lo:hi])
        act = ((g * jax.nn.sigmoid(g)) * up).astype(BF16)
        part = _dot(act, wd_ref[lo:hi, :])
        ff = part if ff is None else ff + part
    x2 = x1 + gate2 * ff
    if final:
        x2 = _rms_rows(x2, D_MODEL) * gf_ref[...]
    o_ref[...] = x2


def _mix_ffn(x, ys, mod, lw, gf, *, nb, n, final):
    T = nb * n
    tm = TOKEN_TILE
    npt = n // tm

    def tok(width):
        return pl.BlockSpec((tm, width), lambda i: (i, 0))

    in_specs = [tok(D_MODEL), tok(256), tok(256), tok(256), tok(256), mod.spec(lambda i: i // npt),
                lw.spec("g2"), lw.spec("w_out"), lw.spec("w_gu", col_blocks=2, col_block=0),
                lw.spec("w_gu", col_blocks=2, col_block=1), lw.spec("w_down")]
    args = [x, *ys, mod.table, lw["g2"], lw["w_out"], lw["w_gu"], lw["w_gu"], lw["w_down"]]
    if final:
        in_specs.append(_resident((1, D_MODEL)))
        args.append(gf)
    return pl.pallas_call(
        functools.partial(_mix_ffn_kernel, final=final),
        grid=(T // tm,),
        in_specs=in_specs,
        out_specs=tok(D_MODEL),
        out_shape=jax.ShapeDtypeStruct((T, D_MODEL), F32),
        compiler_params=_params("arbitrary"),
        name="mix_ffn_final" if final else "mix_ffn",
    )(*args)


def _block_diag(w):
    L, G, c, e = w.shape
    return jnp.einsum('lgce,gh->lgche', w, jnp.eye(G, dtype=w.dtype)).reshape(L, G * c, G * e)


def _rot_cols(w):
    return jnp.concatenate([-w[..., 16:32], w[..., 0:16]], axis=-1)


def _stack_weights(p):
    w_in = p["w_in"]
    o1 = MLA_Q_RANK
    o2 = o1 + MLA_KV_RANK
    o3 = o2 + MLA_ROPE
    c_q, c_kv, k_r, rest = w_in[..., :o1], w_in[..., o1:o2], w_in[..., o2:o3], w_in[..., o3:]
    z = lambda n: jnp.zeros((DEPTH, D_MODEL, n), F32)
    w_in_eff = jnp.concatenate([c_q, k_r, z(32), c_kv, z(64), _rot_cols(k_r), z(32), rest], axis=-1)

    w_uq = p["mla_w_uq"]
    qd = MLA_NOPE + MLA_ROPE
    wq_parts, wqr_parts = [], []
    zq = lambda n: jnp.zeros((DEPTH, MLA_Q_RANK, n), F32)
    for h in range(MLA_HEADS):
        wh = w_uq[..., h * qd:(h + 1) * qd]
        wq_parts += [wh, zq(MLA_SLOT - qd)]
        wqr_parts += [zq(MLA_NOPE), _rot_cols(wh[..., MLA_NOPE:]), zq(MLA_SLOT - qd)]
    pad_rows = lambda w: jnp.pad(w, ((0, 0), (0, 256 - MLA_Q_RANK), (0, 0)))
    w_ukv = p["mla_w_ukv"]
    wk_parts, wv_parts = [], []
    zk = jnp.zeros((DEPTH, MLA_KV_RANK, MLA_SLOT - MLA_NOPE), F32)
    for h in range(MLA_HEADS):
        base = h * (MLA_NOPE + MLA_V)
        wk_parts += [w_ukv[..., base:base + MLA_NOPE], zk]
        wv_parts.append(w_ukv[..., base + MLA_NOPE:base + MLA_NOPE + MLA_V])

    w_r, w_i, b_r, b_i = p["lru_w_r"], p["lru_w_i"], p["lru_b_r"], p["lru_b_i"]
    w_gate = jnp.concatenate([_block_diag(w_r[:, 0]), _block_diag(w_r[:, 1]),
                              _block_diag(w_i[:, 0]), _block_diag(w_i[:, 1])], axis=-1)
    b_gate = jnp.concatenate([b_r[:, 0], b_r[:, 1], b_i[:, 0], b_i[:, 1]], axis=-1)
    row = lambda v: v[:, None, :]
    return {
        "g1": row(p["norm1_g"]),
        "g2": row(p["norm2_g"]),
        "w_in": w_in_eff.astype(BF16),
        "gq": row(jnp.pad(p["mla_q_norm_g"], ((0, 0), (0, 256 - MLA_Q_RANK)))),
        "gkv": row(p["mla_kv_norm_g"]),
        "wq": pad_rows(jnp.concatenate(wq_parts, axis=-1)).astype(BF16),
        "wqr": pad_rows(jnp.concatenate(wqr_parts, axis=-1)).astype(BF16),
        "wkv": jnp.concatenate(wk_parts + wv_parts, axis=-1).astype(BF16),
        "conv_w": p["lru_conv_w"],
        "conv_b": row(p["lru_conv_b"]),
        "w_gate": w_gate.astype(BF16),
        "b_gate": row(b_gate),
        "lru_lambda": p["lru_lambda"],
        "w_pool": _block_diag(p["pool_w"]).astype(BF16),
        "pool_scale": row(p["pool_scale"]),
        "diff_lambda": p["diff_lambda"],
        "diff_g": row(jnp.tile(p["diff_norm_g"], (1, DIFF_HEADS))),
        "w_out": p["w_out"].astype(BF16),
        "w_gu": p["w_gu"].astype(BF16),
        "w_down": p["w_down"].astype(BF16),
    }


def _rope_tables(n, positional):
    quarter = MLA_ROPE // 4
    if positional:
        t = jnp.arange(n)
        row = (t // GRID_W).astype(F32)
        col = (t % GRID_W).astype(F32)
        inv = ROPE_BASE ** (-jnp.arange(quarter, dtype=F32) / quarter)
        ang = jnp.concatenate([row[:, None] * inv, col[:, None] * inv], axis=-1)
        cos, sin = jnp.cos(ang), jnp.sin(ang)
    else:
        cos, sin = jnp.ones((n, 16), F32), jnp.zeros((n, 16), F32)
    scale = LOG2E / math.sqrt(MLA_NOPE + MLA_ROPE)
    place = np.zeros((32, TAB_WIDTH), np.float32)
    offset = np.zeros((1, TAB_WIDTH), np.float32)
    offset[0, 0:64] = offset[0, 96:128] = scale
    for i in range(16):
        for half in (64, 80):
            place[i, half + i] = scale
            place[16 + i, 128 + half + i] = scale
            place[i, 256 + half + i] = 1.0
            place[16 + i, 384 + half + i] = 1.0
        for grp in range(8):
            place[i, 512 + 32 * grp + i] = place[i, 512 + 32 * grp + 16 + i] = 1.0
            place[16 + i, 768 + 32 * grp + i] = -1.0
            place[16 + i, 1024 + 32 * grp + 16 + i] = 1.0
    return jnp.dot(jnp.concatenate([cos, sin], axis=1), place, precision=lax.Precision.HIGHEST) + offset


def _layer(x, mod, lw, tabs, layer_idx, ctx, gf, *, nb, n, final):
    emit_cache = ctx is None
    tok_nb, tok_n = (1, nb * n) if mod.shared else (nb, n)
    outs = _inproj(x, mod, lw, tabs, nb=tok_nb, n=tok_n, emit_cache=emit_cache)
    q, k, vt, u_lru, u_pool, dq, dk, dvt = outs[:8]
    lam_init = 0.8 - 0.6 * math.exp(-0.3 * layer_idx)
    if ctx is None:
        h0 = jnp.zeros((1, 2, LRU_WIDTH), F32)
        h0_block = lambda b: 0
        mla_ctx = diff_ctx = None
    else:
        ckv, kr_pad, cdk, cdv, h0 = ctx
        p = ckv.shape[0] // (nb * DEPTH)
        h0_block = lambda b: b * DEPTH + layer_idx
        kc, vtc, dkc, dvtc = _ctx_prep(ckv, kr_pad, cdk, cdv, lw, nb=nb, p=p)
        mla_ctx = (kc, vtc)
        diff_ctx = (dkc, dvtc)
    y_mla = _mla_attn(q, k, vt, mla_ctx, nb=nb, n=n)
    y_lru, st = _lru(u_lru, h0, h0_block, lw, nb=nb, n=n)
    y_pool = _pool(u_pool, lw, nb=nb, n=n)
    y_diff = _diff_attn(dq, dk, dvt, diff_ctx, lw, nb=nb, n=n, lam_init=lam_init)
    x2 = _mix_ffn(x, (y_mla, y_lru, y_pool, y_diff), mod, lw, gf, nb=tok_nb, n=tok_n, final=final)
    cache = (outs[8], outs[9][:, 64:96], outs[10], outs[11], st) if emit_cache else None
    return x2, cache


def kernel(x_prompt, x_sample, cache_mla_ckv, cache_mla_krope, cache_diff_k, cache_diff_v, state_lru,
           c, c_ctx, w_ada, b_ada, norm1_g, norm2_g, w_in, mla_q_norm_g, mla_w_uq, mla_kv_norm_g,
           mla_w_ukv, lru_conv_w, lru_conv_b, lru_w_r, lru_b_r, lru_w_i, lru_b_i, lru_lambda, pool_w,
           pool_scale, diff_lambda, diff_norm_g, w_out, w_gu, w_down, final_norm_g):
    p = {
        "norm1_g": norm1_g, "norm2_g": norm2_g, "w_in": w_in, "mla_q_norm_g": mla_q_norm_g,
        "mla_w_uq": mla_w_uq, "mla_kv_norm_g": mla_kv_norm_g, "mla_w_ukv": mla_w_ukv,
        "lru_conv_w": lru_conv_w, "lru_conv_b": lru_conv_b, "lru_w_r": lru_w_r, "lru_b_r": lru_b_r,
        "lru_w_i": lru_w_i, "lru_b_i": lru_b_i, "lru_lambda": lru_lambda, "pool_w": pool_w,
        "pool_scale": pool_scale, "diff_lambda": diff_lambda, "diff_norm_g": diff_norm_g,
        "w_out": w_out, "w_gu": w_gu, "w_down": w_down,
    }
    Bp, Np, _ = x_prompt.shape
    Bs, Ns, _ = x_sample.shape
    P = cache_mla_ckv.shape[2]

    cond_all = jnp.concatenate([c, c_ctx[None, :], jnp.zeros((MOD_ROWS - Bs - 1, D_MODEL), F32)], axis=0)
    mod_table = _ada(cond_all, w_ada, b_ada).reshape(DEPTH * MOD_ROWS, 1, 6 * D_MODEL)
    tabs_p = _rope_tables(Bp * Np, positional=False)
    tabs_s = _rope_tables(Ns, positional=True)
    kr_pad = jnp.pad(cache_mla_krope, ((0, 0), (0, 0), (0, 0), (MLA_NOPE, MLA_SLOT - MLA_NOPE - MLA_ROPE)))
    flat = lambda a, w: a.reshape(Bs * DEPTH * P, w)
    ctx = (flat(cache_mla_ckv, MLA_KV_RANK), flat(kr_pad, MLA_SLOT), flat(cache_diff_k, 256),
           flat(cache_diff_v, 256), state_lru.reshape(Bs * DEPTH, 2, LRU_WIDTH))
    gf = final_norm_g[None, :]
    stacked = _stack_weights(p)

    xp = x_prompt.reshape(Bp * Np, D_MODEL)
    xs = x_sample.reshape(Bs * Ns, D_MODEL)
    caches = []
    for l in range(DEPTH):
        lw = _LayerWeights(stacked, l)
        final = l == DEPTH - 1
        mod_p = _Mod(mod_table, l * MOD_ROWS + Bs, shared=True)
        mod_s = _Mod(mod_table, l * MOD_ROWS, shared=False)
        xp, cache = _layer(xp, mod_p, lw, tabs_p, l, None, gf, nb=Bp, n=Np, final=final)
        caches.append(cache)
        xs, _ = _layer(xs, mod_s, lw, tabs_s, l, ctx, gf, nb=Bs, n=Ns, final=final)

    stack = lambda i, w: jnp.stack([cc[i].reshape(Bp, Np, w) for cc in caches], axis=1)
    new_mla_ckv = stack(0, MLA_KV_RANK)
    new_mla_krope = stack(1, MLA_ROPE)
    new_diff_k = stack(2, 256).reshape(Bp, DEPTH, Np, DIFF_HEADS, 2, DIFF_DIM)
    new_diff_v = stack(3, 256).reshape(Bp, DEPTH, Np, DIFF_HEADS, 2 * DIFF_DIM)
    new_state_lru = jnp.stack([cc[4] for cc in caches], axis=1)
    return (xp.reshape(Bp, Np, D_MODEL), xs.reshape(Bs, Ns, D_MODEL),
            new_mla_ckv, new_mla_krope, new_diff_k, new_diff_v, new_state_lru)
```

```python
import functools
import math

import jax
import jax.numpy as jnp
import numpy as np
from jax import lax
from jax.experimental import pallas as pl
from jax.experimental.pallas import tpu as pltpu

F32 = jnp.float32
BF16 = jnp.bfloat16

D_MODEL = 1024
DEPTH = 2
GRID_W = 64
GROUP_WIDTH = 256
MLA_HEADS = 4
MLA_NOPE = 64
MLA_ROPE = 32
MLA_V = 64
MLA_Q_RANK = 192
MLA_KV_RANK = 128
MLA_SLOT = 128
LRU_WIDTH = 256
LRU_C = 8.0
POOL_WINDOWS = (2, 4, 8, 16)
POOL_CH = 64
DIFF_HEADS = 4
DIFF_DIM = 32
HEAD_V = 64
FF_HIDDEN = 2816
FF_CHUNKS = ((0, 1536), (1536, 2816))
ROPE_BASE = 10000.0
EPS = 1e-6
IN_EFF = 2048
HALO = 8
SCAN_RUN = 4
VT_ROWS = 80
ATT_TQ = 256
TOKEN_TILE = 512
TAB_WIDTH = 4 * 128 + 3 * 256
MOD_ROWS = 16
LOG2E = math.log2(math.e)

VMEM_LIMIT_BYTES = 56 * 1024 * 1024

_NT = (((1,), (1,)), ((), ()))


def _params(*sem):
    return pltpu.CompilerParams(dimension_semantics=sem, vmem_limit_bytes=VMEM_LIMIT_BYTES)


def _resident(shape):
    zeros = (0,) * len(shape)
    return pl.BlockSpec(shape, lambda *_: zeros, pipeline_mode=pl.Buffered(1))


def _dot(a, b):
    return jnp.dot(a, b, preferred_element_type=F32)


def _dot_nt(a, b):
    return lax.dot_general(a, b, _NT, preferred_element_type=F32)


def _rms_rows(x, width):
    ms = jnp.sum(x * x, axis=-1, keepdims=True) * (1.0 / width)
    return x * lax.rsqrt(ms + EPS)


def _store_vt(vt_ref, v):
    vt = v.T
    rows = v.shape[0]
    pad = VT_ROWS - HEAD_V
    ones_row = jnp.where(lax.broadcasted_iota(jnp.int32, (pad, rows), 0) == 0, 1.0, 0.0).astype(BF16)
    for hh in range(vt_ref.shape[0]):
        vt_ref[hh, 0:HEAD_V, :] = vt[hh * HEAD_V:(hh + 1) * HEAD_V, :].astype(BF16)
        vt_ref[hh, HEAD_V:VT_ROWS, :] = ones_row


class _Mod:
    def __init__(self, table, row0, shared):
        self.table, self.row0, self.shared = table, row0, shared

    def spec(self, batch_of):
        row0 = self.row0
        if self.shared:
            return pl.BlockSpec((1, 1, 6 * D_MODEL), lambda *g: (row0, 0, 0))
        return pl.BlockSpec((1, 1, 6 * D_MODEL), lambda *g: (row0 + batch_of(*g), 0, 0))


class _LayerWeights:
    def __init__(self, stacked, layer):
        self.stacked, self.layer = stacked, layer

    def __getitem__(self, name):
        return self.stacked[name]

    def spec(self, name, col_blocks=1, col_block=0):
        layer = self.layer
        _, rows, cols = self.stacked[name].shape
        return pl.BlockSpec((None, rows, cols // col_blocks), lambda *_: (layer, 0, col_block),
                            pipeline_mode=pl.Buffered(1))


def _ada_kernel(cond_ref, w_ref, b_ref, out_ref):
    c = cond_ref[...]
    s = c * jax.nn.sigmoid(c)
    out_ref[0] = _dot(s.astype(BF16), w_ref[0].astype(BF16)) + b_ref[0]


def _ada(cond_all, w_ada, b_ada):
    rows = cond_all.shape[0]
    tn = 1536
    return pl.pallas_call(
        _ada_kernel,
        grid=(DEPTH, 6 * D_MODEL // tn),
        in_specs=[
            pl.BlockSpec((rows, D_MODEL), lambda l, j: (0, 0)),
            pl.BlockSpec((1, D_MODEL, tn), lambda l, j: (l, 0, j)),
            pl.BlockSpec((1, 1, tn), lambda l, j: (l, 0, j)),
        ],
        out_specs=pl.BlockSpec((1, rows, tn), lambda l, j: (l, 0, j)),
        out_shape=jax.ShapeDtypeStruct((DEPTH, rows, 6 * D_MODEL), F32),
        compiler_params=_params("arbitrary", "arbitrary"),
        name="ada_mod",
    )(cond_all, w_ada, b_ada.reshape(DEPTH, 1, 6 * D_MODEL))


def _inproj_kernel(x_ref, mod_ref, g1_ref, win_ref, gq_ref, gkv_ref, wq_ref, wqr_ref, wkv_ref, tab_ref,
                   q_out, k_out, vt_out, lru_out, pool_out, dq_out, dk_out, dvt_out, *cache_outs):
    cosq_ref, sinq_ref, cosk_ref, sink_ref = (tab_ref.at[:, i * 128:(i + 1) * 128] for i in range(4))
    cosd_ref, sina_ref, sinb_ref = (tab_ref.at[:, 512 + i * 256:768 + i * 256] for i in range(3))
    x = x_ref[...]
    mod = mod_ref[0]
    sh1 = mod[:, 0:D_MODEL]
    sc1 = mod[:, D_MODEL:2 * D_MODEL]
    h = _rms_rows(x, D_MODEL) * g1_ref[...]
    hb = (h * (1.0 + sc1) + sh1).astype(BF16)

    u_mla = _dot(hb, win_ref[:, 0:512])
    u_pd = _dot(hb, win_ref[:, 1024:1536])
    u_kv = _dot(hb, win_ref[:, 1536:2048])
    t01 = u_mla[:, 0:256]
    lane = lax.broadcasted_iota(jnp.int32, (1, 256), 1)
    cq = jnp.where(lane < MLA_Q_RANK, t01, 0.0)
    cqn = (_rms_rows(cq, MLA_Q_RANK) * gq_ref[...]).astype(BF16)
    qa = _dot(cqn, wq_ref[...])
    qr = _dot(cqn, wqr_ref[...])
    cosq = cosq_ref[...]
    sinq = sinq_ref[...]
    ckv = u_mla[:, 256:384]
    lat = _rms_rows(ckv, MLA_KV_RANK) * gkv_ref[...]
    latb = lat.astype(BF16)
    kkv = _dot(latb, wkv_ref[...])
    kk = kkv[:, 0:MLA_HEADS * MLA_SLOT]
    _store_vt(vt_out, kkv[:, MLA_HEADS * MLA_SLOT:])
    t1 = t01[:, 128:256]
    t3 = u_mla[:, 384:512]
    kro = t1 * cosk_ref[...] + t3 * sink_ref[...]
    for hh in range(MLA_HEADS):
        sl = slice(hh * MLA_SLOT, (hh + 1) * MLA_SLOT)
        q_out[hh] = (qa[:, sl] * cosq + qr[:, sl] * sinq).astype(q_out.dtype)
        k_out[hh] = (kk[:, sl] + kro).astype(k_out.dtype)

    lru_out[...] = _dot(hb, win_ref[:, 512:1024])
    pool_out[...] = u_pd[:, 0:256]

    cosd = cosd_ref[...]
    sina = sina_ref[...]
    sinb = sinb_ref[...]

    def rope(t):
        return t * cosd + pltpu.roll(t, 256 - 16, 1) * sina + pltpu.roll(t, 16, 1) * sinb

    dq = u_pd[:, 256:512]
    dk = u_kv[:, 0:256]
    dv = u_kv[:, 256:512]
    dq_out[...] = (rope(dq) * (LOG2E / math.sqrt(DIFF_DIM))).astype(dq_out.dtype)
    dk_out[...] = rope(dk).astype(dk_out.dtype)
    _store_vt(dvt_out, dv)

    if cache_outs:
        lat_out, kr_out, dk_raw_out, dv_raw_out = cache_outs
        lat_out[...] = lat
        kr_out[...] = t1
        dk_raw_out[...] = dk
        dv_raw_out[...] = dv


def _inproj(x, mod, lw, tabs, *, nb, n, emit_cache):
    T = nb * n
    tm = TOKEN_TILE
    npt = n // tm
    row_blk = lambda j, b: b * npt + j

    def tok(width):
        return pl.BlockSpec((tm, width), lambda j, b: (row_blk(j, b), 0))

    def tab(width):
        return pl.BlockSpec((tm, width), lambda j, b: (j, 0))

    head = pl.BlockSpec((MLA_HEADS, tm, MLA_SLOT), lambda j, b: (0, row_blk(j, b), 0))
    vt_spec = pl.BlockSpec((MLA_HEADS, VT_ROWS, tm), lambda j, b: (0, 0, row_blk(j, b)))
    wnames = ("g1", "w_in", "gq", "gkv", "wq", "wqr", "wkv")
    in_specs = [tok(D_MODEL), mod.spec(lambda j, b: b)] + [lw.spec(nm) for nm in wnames] + [
        tab(TAB_WIDTH)]
    out_specs = [head, head, vt_spec, tok(512), tok(256), tok(256), tok(256), vt_spec]
    vt_shape = jax.ShapeDtypeStruct((MLA_HEADS, VT_ROWS, T), BF16)
    out_shape = [
        jax.ShapeDtypeStruct((MLA_HEADS, T, MLA_SLOT), BF16),
        jax.ShapeDtypeStruct((MLA_HEADS, T, MLA_SLOT), BF16),
        vt_shape,
        jax.ShapeDtypeStruct((T, 512), F32),
        jax.ShapeDtypeStruct((T, 256), F32),
        jax.ShapeDtypeStruct((T, 256), BF16),
        jax.ShapeDtypeStruct((T, 256), BF16),
        vt_shape,
    ]
    if emit_cache:
        out_specs += [tok(128), tok(128), tok(256), tok(256)]
        out_shape += [jax.ShapeDtypeStruct((T, 128), F32), jax.ShapeDtypeStruct((T, 128), F32),
                      jax.ShapeDtypeStruct((T, 256), F32), jax.ShapeDtypeStruct((T, 256), F32)]
    return pl.pallas_call(
        _inproj_kernel,
        grid=(npt, nb),
        in_specs=in_specs,
        out_specs=out_specs,
        out_shape=out_shape,
        compiler_params=_params("arbitrary", "arbitrary"),
        name="inproj_cache" if emit_cache else "inproj",
    )(x, mod.table, *[lw[nm] for nm in wnames], tabs)


def _ctx_prep_kernel(ckv_ref, kr_ref, dk_ref, dv_ref, wkv_ref, k_out, vt_out, dk_out, dvt_out):
    latb = ckv_ref[...].astype(BF16)
    kkv = _dot(latb, wkv_ref[...])
    kk = kkv[:, 0:MLA_HEADS * MLA_SLOT]
    kr = kr_ref[...]
    for hh in range(MLA_HEADS):
        k_out[hh] = (kk[:, hh * MLA_SLOT:(hh + 1) * MLA_SLOT] + kr).astype(k_out.dtype)
    _store_vt(vt_out, kkv[:, MLA_HEADS * MLA_SLOT:])
    dk_out[...] = dk_ref[...].astype(dk_out.dtype)
    _store_vt(dvt_out, dv_ref[...])


def _ctx_prep(ckv, kr_pad, cdk, cdv, lw, *, nb, p):
    T = nb * p
    layer = lw.layer
    cache_row = lambda w: pl.BlockSpec((p, w), lambda b: (b * DEPTH + layer, 0))
    row = lambda w: pl.BlockSpec((p, w), lambda b: (b, 0))
    vt_spec = pl.BlockSpec((MLA_HEADS, VT_ROWS, p), lambda b: (0, 0, b))
    vt_shape = jax.ShapeDtypeStruct((MLA_HEADS, VT_ROWS, T), BF16)
    return pl.pallas_call(
        _ctx_prep_kernel,
        grid=(nb,),
        in_specs=[cache_row(128), cache_row(128), cache_row(256), cache_row(256),
                  lw.spec("wkv")],
        out_specs=[pl.BlockSpec((MLA_HEADS, p, MLA_SLOT), lambda b: (0, b, 0)), vt_spec, row(256), vt_spec],
        out_shape=[jax.ShapeDtypeStruct((MLA_HEADS, T, MLA_SLOT), BF16), vt_shape,
                   jax.ShapeDtypeStruct((T, 256), BF16), vt_shape],
        compiler_params=_params("arbitrary"),
        name="ctx_prep",
    )(ckv, kr_pad, cdk, cdv, lw["wkv"])


SAFE_DENOM = 2.0 ** -60
BOUND_SLACK = 1.02


def _scores(k_new, k_ctx, q):
    sn = _dot_nt(k_new(), q)
    sc = _dot_nt(k_ctx(), q) if k_ctx is not None else None
    return sn, sc


def _exact_shift(k_new, k_ctx, q):
    sn, sc = _scores(k_new, k_ctx, q)
    m = jnp.max(sn, axis=0, keepdims=True)
    if sc is not None:
        m = jnp.maximum(m, jnp.max(sc, axis=0, keepdims=True))
    return m


def _bound_shift(q, key_norm2):
    qf = q.astype(F32)
    ones = jnp.ones((8, q.shape[1]), BF16)
    q_norm2 = _dot_nt(ones, (qf * qf).astype(BF16))[0:1, :]
    return jnp.sqrt(q_norm2 * key_norm2) * BOUND_SLACK


def _max_row_norm2(k_new, k_ctx, col_sum):
    def one(k):
        kf = k.astype(F32)
        return jnp.max(_dot((kf * kf).astype(BF16), col_sum), axis=0, keepdims=True)
    m = one(k_new)
    if k_ctx is not None:
        m = jnp.maximum(m, one(k_ctx))
    return m * BOUND_SLACK


def _exp_stage(e_buf, k_new, k_ctx, q, shift, n_ctx):
    sn, sc = _scores(k_new, k_ctx, q)
    e_buf[n_ctx:, :] = jnp.exp2(sn - shift).astype(BF16)
    if sc is not None:
        e_buf[0:n_ctx, :] = jnp.exp2(sc - shift).astype(BF16)


def _value_stage(e_buf, vt_new, vt_ctx, n_ctx):
    vt = vt_new()
    nk = vt.shape[1]
    half = (nk // 512) * 256 if nk >= 1024 else nk
    o = _dot(vt[:, 0:half], e_buf[n_ctx:n_ctx + half, :])
    if half < nk:
        o = o + _dot(vt[:, half:nk], e_buf[n_ctx + half:n_ctx + nk, :])
    if vt_ctx is not None:
        o = o + _dot(vt_ctx(), e_buf[0:n_ctx, :])
    return o


def _run_pipeline(n_maps, exp_stage, value_stage):
    exp_stage(0)
    for u in range(n_maps):
        if u + 1 < n_maps:
            exp_stage(u + 1)
        value_stage(u)


def _att_scratch(nk, key_shape):
    scratch = [pltpu.VMEM((8, 128), F32),
               pltpu.VMEM((MLA_HEADS * HEAD_V, ATT_TQ), F32),
               pltpu.VMEM((nk, ATT_TQ), BF16), pltpu.VMEM((nk, ATT_TQ), BF16)]
    if key_shape is not None:
        scratch += [pltpu.VMEM(key_shape, BF16), pltpu.VMEM((MLA_HEADS, VT_ROWS, nk), BF16)]
    return scratch


def _att_nsub(n):
    return 2 if n % (2 * ATT_TQ) == 0 else 1


def _mla_attn_kernel(*refs, has_ctx, nsub):
    if has_ctx:
        q_ref, k_ref, vt_ref, kc_ref, vtc_ref, o_ref, kn2, ot, e0, e1, keys, vals = refs
        n_ctx = kc_ref.shape[1]
    else:
        q_ref, k_ref, vt_ref, o_ref, kn2, ot, e0, e1 = refs
        keys, vals = k_ref, vt_ref
    e_bufs = (e0, e1)

    @pl.when(pl.program_id(1) == 0)
    def _():
        ones = jnp.ones((MLA_SLOT, 128), BF16)
        for hh in range(MLA_HEADS):
            kn2[hh:hh + 1, :] = _max_row_norm2(k_ref[hh], kc_ref[hh] if has_ctx else None, ones)
        if has_ctx:
            keys[:, 0:n_ctx, :] = kc_ref[...]
            keys[:, n_ctx:, :] = k_ref[...]
            vals[:, :, 0:n_ctx] = vtc_ref[...]
            vals[:, :, n_ctx:] = vt_ref[...]

    def run(exact):
        denoms = []

        def exp_stage(u):
            t, hh = divmod(u, MLA_HEADS)
            q = q_ref[hh, t * ATT_TQ:(t + 1) * ATT_TQ, :]
            k_all = lambda: keys[hh]
            shift = _exact_shift(k_all, None, q) if exact else _bound_shift(q, kn2[hh:hh + 1, 0:1])
            _exp_stage(e_bufs[u % 2], k_all, None, q, shift, 0)

        def value_stage(u):
            t, hh = divmod(u, MLA_HEADS)
            o = _value_stage(e_bufs[u % 2], lambda: vals[hh], None, 0)
            denom = o[HEAD_V:HEAD_V + 1, :]
            denoms.append(denom)
            ot[hh * HEAD_V:(hh + 1) * HEAD_V, :] = o[0:HEAD_V, :] * (1.0 / denom)
            if hh == MLA_HEADS - 1:
                o_ref[t * ATT_TQ:(t + 1) * ATT_TQ, :] = ot[...].T

        _run_pipeline(nsub * MLA_HEADS, exp_stage, value_stage)
        return jnp.min(functools.reduce(jnp.minimum, denoms))

    denom_min = run(exact=False)

    @pl.when(jnp.logical_not(denom_min >= SAFE_DENOM))
    def _():
        run(exact=True)


def _mla_attn(q, k, vt, ctx, *, nb, n):
    nsub = _att_nsub(n)
    tq = nsub * ATT_TQ
    npt = n // tq
    H, S = MLA_HEADS, MLA_SLOT
    in_specs = [
        pl.BlockSpec((H, tq, S), lambda b, j: (0, b * npt + j, 0)),
        pl.BlockSpec((H, n, S), lambda b, j: (0, b, 0)),
        pl.BlockSpec((H, VT_ROWS, n), lambda b, j: (0, 0, b)),
    ]
    args = [q, k, vt]
    n_ctx = 0
    if ctx is not None:
        n_ctx = ctx[0].shape[1] // nb
        in_specs += [
            pl.BlockSpec((H, n_ctx, S), lambda b, j: (0, b, 0)),
            pl.BlockSpec((H, VT_ROWS, n_ctx), lambda b, j: (0, 0, b)),
        ]
        args += list(ctx)
    return pl.pallas_call(
        functools.partial(_mla_attn_kernel, has_ctx=ctx is not None, nsub=nsub),
        grid=(nb, npt),
        in_specs=in_specs,
        out_specs=pl.BlockSpec((tq, 256), lambda b, j: (b * npt + j, 0)),
        out_shape=jax.ShapeDtypeStruct((nb * n, 256), F32),
        scratch_shapes=_att_scratch(n + n_ctx, (H, n + n_ctx, S) if ctx is not None else None),
        compiler_params=_params("arbitrary", "arbitrary"),
        name="mla_attn_ctx" if ctx is not None else "mla_attn",
    )(*args)


def _diff_attn_kernel(*refs, has_ctx, nsub, lam_init):
    if has_ctx:
        lv_ref, g_ref, q_ref, k_ref, vt_ref, kc_ref, vtc_ref, o_ref, kn2, ot, e0, e1, keys, vals = refs
        n_ctx = kc_ref.shape[0]
    else:
        lv_ref, g_ref, q_ref, k_ref, vt_ref, o_ref, kn2, ot, e0, e1 = refs
        keys, vals = k_ref, vt_ref
    e_bufs = (e0, e1)
    lv = lv_ref[...]
    lam = (jnp.exp(jnp.sum(lv[0:1] * lv[1:2], axis=-1, keepdims=True))
           - jnp.exp(jnp.sum(lv[2:3] * lv[3:4], axis=-1, keepdims=True)) + lam_init)
    lane128 = lax.broadcasted_iota(jnp.int32, (1, 128), 1)
    n_pairs = 2 * DIFF_HEADS

    @pl.when(pl.program_id(1) == 0)
    def _():
        dim = lax.broadcasted_iota(jnp.int32, (256, 128), 0)
        col = lax.broadcasted_iota(jnp.int32, (256, 128), 1)
        indicator = jnp.where(dim // DIFF_DIM == col, 1.0, 0.0).astype(BF16)
        kn2[0:1, :] = _max_row_norm2(k_ref[...], kc_ref[...] if has_ctx else None, indicator)
        if has_ctx:
            keys[0:n_ctx, :] = kc_ref[...]
            keys[n_ctx:, :] = k_ref[...]
            vals[:, :, 0:n_ctx] = vtc_ref[...]
            vals[:, :, n_ctx:] = vt_ref[...]

    def run(exact):
        denoms = []
        outs = {}

        def exp_stage(u):
            t, p = divmod(u, n_pairs)
            tile = slice((p * DIFF_DIM // 128) * 128, (p * DIFF_DIM // 128 + 1) * 128)
            k_new = lambda: keys[:, tile]
            k_ctx = None
            q = q_ref[t * ATT_TQ:(t + 1) * ATT_TQ, tile]
            lo = p * DIFF_DIM - tile.start
            in_pair = (lane128 >= lo) & (lane128 < lo + DIFF_DIM)
            qm = jnp.where(in_pair, q, jnp.zeros_like(q))
            shift = _exact_shift(k_new, k_ctx, qm) if exact else _bound_shift(qm, kn2[0:1, p:p + 1])
            _exp_stage(e_bufs[u % 2], k_new, k_ctx, qm, shift, 0)

        def value_stage(u):
            t, p = divmod(u, n_pairs)
            hh = p // 2
            o = _value_stage(e_bufs[u % 2], lambda: vals[hh], None, 0)
            denom = o[HEAD_V:HEAD_V + 1, :]
            denoms.append(denom)
            outs[u] = (o[0:HEAD_V, :], denom)
            if p % 2 == 1:
                (o0, l0), (o1, l1) = outs.pop(u - 1), outs.pop(u)
                o = o0 * (1.0 / l0) - o1 * (lam / l1)
                msq = jnp.sum(o * o, axis=0, keepdims=True) * (1.0 / HEAD_V)
                ot[hh * HEAD_V:(hh + 1) * HEAD_V, :] = o * lax.rsqrt(msq + EPS)
            if p == n_pairs - 1:
                o_ref[t * ATT_TQ:(t + 1) * ATT_TQ, :] = (ot[...].T * g_ref[...]) * (1.0 - lam_init)

        _run_pipeline(nsub * n_pairs, exp_stage, value_stage)
        return jnp.min(functools.reduce(jnp.minimum, denoms))

    denom_min = run(exact=False)

    @pl.when(jnp.logical_not(denom_min >= SAFE_DENOM))
    def _():
        run(exact=True)


def _diff_attn(q, k, vt, ctx, lw, *, nb, n, lam_init):
    nsub = 1
    tq = nsub * ATT_TQ
    npt = n // tq
    in_specs = [
        lw.spec("diff_lambda"),
        lw.spec("diff_g"),
        pl.BlockSpec((tq, 256), lambda b, j: (b * npt + j, 0)),
        pl.BlockSpec((n, 256), lambda b, j: (b, 0)),
        pl.BlockSpec((DIFF_HEADS, VT_ROWS, n), lambda b, j: (0, 0, b)),
    ]
    args = [lw["diff_lambda"], lw["diff_g"], q, k, vt]
    n_ctx = 0
    if ctx is not None:
        n_ctx = ctx[0].shape[0] // nb
        in_specs += [pl.BlockSpec((n_ctx, 256), lambda b, j: (b, 0)),
                     pl.BlockSpec((DIFF_HEADS, VT_ROWS, n_ctx), lambda b, j: (0, 0, b))]
        args += list(ctx)
    return pl.pallas_call(
        functools.partial(_diff_attn_kernel, has_ctx=ctx is not None, nsub=nsub, lam_init=lam_init),
        grid=(nb, npt),
        in_specs=in_specs,
        out_specs=pl.BlockSpec((tq, 256), lambda b, j: (b * npt + j, 0)),
        out_shape=jax.ShapeDtypeStruct((nb * n, 256), F32),
        scratch_shapes=_att_scratch(n + n_ctx, (n + n_ctx, 256) if ctx is not None else None),
        compiler_params=_params("arbitrary", "arbitrary"),
        name="diff_attn_ctx" if ctx is not None else "diff_attn",
    )(*args)


def _shift_rows(v, k):
    return pltpu.roll(v, (-k) % v.shape[0], 0)


def _scan_strided(a_ref, b_ref, h_ref, row0, carry, n_rows, reverse):
    sub = lax.broadcasted_iota(jnp.int32, (8, 128), 0)
    span = 8 * SCAN_RUN
    order = tuple(range(SCAN_RUN))[::-1] if reverse else tuple(range(SCAN_RUN))
    starts = tuple(range(0, n_rows, span))[::-1] if reverse else tuple(range(0, n_rows, span))
    carries = []
    for lt in range(a_ref.shape[0]):
        c_in = carry[:, lt * 128:(lt + 1) * 128]
        for start in starts:
            tile = lambda ref, g: ref[lt, pl.ds(row0 + start + g, 8, stride=SCAN_RUN), :]
            a = [tile(a_ref, g) for g in range(SCAN_RUN)]
            b = [tile(b_ref, g) for g in range(SCAN_RUN)]
            h = {order[0]: b[order[0]]}
            p = {order[0]: a[order[0]]}
            for prev, g in zip(order, order[1:]):
                h[g] = a[g] * h[prev] + b[g]
                p[g] = a[g] * p[prev]
            pi, hi = p[order[-1]], h[order[-1]]
            for s in (1, 2, 4):
                shift = 8 - s if reverse else s
                valid = (sub < 8 - s) if reverse else (sub >= s)
                pr, hr = pltpu.roll(pi, shift, 0), pltpu.roll(hi, shift, 0)
                hi = jnp.where(valid, pi * hr + hi, hi)
                pi = jnp.where(valid, pi * pr, pi)
            one = 7 if reverse else 1
            first = (sub == 7) if reverse else (sub == 0)
            pe = jnp.where(first, 1.0, pltpu.roll(pi, one, 0))
            he = jnp.where(first, 0.0, pltpu.roll(hi, one, 0))
            c = pe * c_in + he
            for g in range(SCAN_RUN):
                h_ref[lt, pl.ds(start + g, 8, stride=SCAN_RUN), :] = h[g] + p[g] * c
            last = 0 if reverse else 7
            c_in = pi[last:last + 1, :] * c_in + hi[last:last + 1, :]
        carries.append(c_in)
    return jnp.concatenate(carries, axis=1)


def _sigmoid(x):
    return 0.5 * jnp.tanh(0.5 * x) + 0.5


def _gelu_tanh(x):
    return x * (0.5 * (1.0 + jnp.tanh(math.sqrt(2.0 / math.pi) * (x + 0.044715 * (x * x * x)))))


def _lru_kernel(u_ref, h0_ref, cw_ref, cb_ref, wg_ref, bg_ref, lam_ref, y_ref, st_ref,
                xpad, a1s, b1s, a0c, b0c, hc, *, N, T):
    W = LRU_WIDTH
    nc = N // T
    tiles = [slice(lt * 128, (lt + 1) * 128) for lt in range(W // 128)]
    zeros = jnp.zeros((HALO, W), F32)
    xpad[0:HALO, :] = zeros
    xpad[N + HALO:N + 2 * HALO, :] = zeros

    def fill(j, carry):
        r0 = pl.multiple_of(j * T, T)
        xpad[pl.ds(r0 + HALO, T), :] = u_ref[pl.ds(r0, T), 0:W]
        return carry

    lax.fori_loop(0, nc, fill, 0)

    z = -lam_ref[...]
    sp = jnp.maximum(z, 0.0) + jnp.log1p(jnp.exp(-jnp.abs(z)))
    cw = cw_ref[...]
    cb = cb_ref[...]
    bg = bg_ref[...]

    def fwd(j, carry):
        r0 = pl.multiple_of(j * T, T)
        ext = xpad[pl.ds(r0, T + 2 * HALO), :]
        body = slice(HALO, HALO + T)
        xc = cb
        for tap in range(4):
            xc = xc + _shift_rows(ext, tap - 1)[body] * cw[tap:tap + 1]
        g = _sigmoid(_dot(xc.astype(BF16), wg_ref[...]) + bg)
        ab = []
        for d in range(2):
            r = g[:, d * W:(d + 1) * W]
            i = g[:, (2 + d) * W:(3 + d) * W]
            log_a = (-LRU_C * r) * sp[d:d + 1]
            a = jnp.exp(log_a)
            bt = (jnp.sqrt(1.0 - a * a) * i) * xc
            ab.append((a, bt))
        for lt, lanes in enumerate(tiles):
            a0c[lt] = ab[0][0][:, lanes]
            b0c[lt] = ab[0][1][:, lanes]
            a1s[lt, pl.ds(r0, T), :] = ab[1][0][:, lanes]
            b1s[lt, pl.ds(r0, T), :] = ab[1][1][:, lanes]
        carry = _scan_strided(a0c, b0c, hc, 0, carry, T, reverse=False)
        for lt, lanes in enumerate(tiles):
            y_ref[pl.ds(r0, T), lanes] = hc[lt]
        return carry

    cf = lax.fori_loop(0, nc, fwd, h0_ref[0, 0:1, :])

    def bwd(jj, carry):
        r0 = pl.multiple_of((nc - 1 - jj) * T, T)
        carry = _scan_strided(a1s, b1s, hc, r0, carry, T, reverse=True)
        for lt, lanes in enumerate(tiles):
            gb = u_ref[pl.ds(r0, T), W + lt * 128:W + (lt + 1) * 128]
            y_ref[pl.ds(r0, T), lanes] = (y_ref[pl.ds(r0, T), lanes] + hc[lt]) * _gelu_tanh(gb)
        return carry

    cbw = lax.fori_loop(0, nc, bwd, h0_ref[0, 1:2, :])
    st_ref[0, 0:1, :] = cf
    st_ref[0, 1:2, :] = cbw


def _lru(u, h0, h0_block, lw, *, nb, n):
    T = min(n, 256)
    W = LRU_WIDTH
    return pl.pallas_call(
        functools.partial(_lru_kernel, N=n, T=T),
        grid=(nb,),
        in_specs=[
            pl.BlockSpec((n, 2 * W), lambda b: (b, 0)),
            pl.BlockSpec((1, 2, W), lambda b: (h0_block(b), 0, 0)),
            lw.spec("conv_w"), lw.spec("conv_b"), lw.spec("w_gate"), lw.spec("b_gate"),
            lw.spec("lru_lambda"),
        ],
        out_specs=[
            pl.BlockSpec((n, W), lambda b: (b, 0)),
            pl.BlockSpec((1, 2, W), lambda b: (b, 0, 0)),
        ],
        out_shape=[
            jax.ShapeDtypeStruct((nb * n, W), F32),
            jax.ShapeDtypeStruct((nb, 2, W), F32),
        ],
        scratch_shapes=[
            pltpu.VMEM((n + 2 * HALO, W), F32),
            pltpu.VMEM((W // 128, n, 128), F32),
            pltpu.VMEM((W // 128, n, 128), F32),
            pltpu.VMEM((W // 128, T, 128), F32),
            pltpu.VMEM((W // 128, T, 128), F32),
            pltpu.VMEM((W // 128, T, 128), F32),
        ],
        compiler_params=_params("arbitrary"),
        name="rglru",
    )(u, h0, lw["conv_w"], lw["conv_b"], lw["w_gate"], lw["b_gate"], lw["lru_lambda"])


def _pool_kernel(u_ref, wp_ref, sc_ref, y_ref, xpad, *, N, T):
    W = GROUP_WIDTH
    nc = N // T
    zeros = jnp.zeros((HALO, W), F32)
    xpad[0:HALO, :] = zeros
    xpad[N + HALO:N + 2 * HALO, :] = zeros

    def fill(j, carry):
        r0 = pl.multiple_of(j * T, T)
        xpad[pl.ds(r0 + HALO, T), :] = u_ref[pl.ds(r0, T), :]
        return carry

    lax.fori_loop(0, nc, fill, 0)

    grp = lax.broadcasted_iota(jnp.int32, (1, W), 1) // POOL_CH
    half = jnp.where(grp == 0, 1, jnp.where(grp == 1, 2, jnp.where(grp == 2, 4, 8)))
    scale = sc_ref[...]

    def chunk(j, carry):
        r0 = pl.multiple_of(j * T, T)
        ext = xpad[pl.ds(r0, T + 2 * HALO), :]
        w2 = _shift_rows(ext, -1) + ext
        w4 = _shift_rows(w2, -1) + _shift_rows(w2, 1)
        w8 = _shift_rows(w4, -2) + _shift_rows(w4, 2)
        w16 = _shift_rows(w8, -4) + _shift_rows(w8, 4)
        ws = jnp.where(grp == 0, w2, jnp.where(grp == 1, w4, jnp.where(grp == 2, w8, w16)))
        body = slice(HALO, HALO + T)
        t = r0 + lax.broadcasted_iota(jnp.int32, (T, W), 0)
        cnt = (jnp.minimum(t + half, N) - jnp.maximum(t - half, 0)).astype(F32)
        d = ws[body] / cnt - ext[body]
        y_ref[pl.ds(r0, T), :] = _dot(d.astype(BF16), wp_ref[...]) * scale
        return carry

    lax.fori_loop(0, nc, chunk, 0)


def _pool(u, lw, *, nb, n):
    W = GROUP_WIDTH
    T = min(n, 256)
    return pl.pallas_call(
        functools.partial(_pool_kernel, N=n, T=T),
        grid=(nb,),
        in_specs=[pl.BlockSpec((n, W), lambda b: (b, 0)), lw.spec("w_pool"), lw.spec("pool_scale")],
        out_specs=pl.BlockSpec((n, W), lambda b: (b, 0)),
        out_shape=jax.ShapeDtypeStruct((nb * n, W), F32),
        scratch_shapes=[pltpu.VMEM((n + 2 * HALO, W), F32)],
        compiler_params=_params("arbitrary"),
        name="pool_mixer",
    )(u, lw["w_pool"], lw["pool_scale"])


def _mix_ffn_kernel(*refs, final):
    if final:
        (x_ref, ya_ref, yb_ref, yc_ref, yd_ref, mod_ref, g2_ref, wo_ref, wg_ref, wu_ref, wd_ref,
         gf_ref, o_ref) = refs
    else:
        (x_ref, ya_ref, yb_ref, yc_ref, yd_ref, mod_ref, g2_ref, wo_ref, wg_ref, wu_ref, wd_ref,
         o_ref) = refs
    mod = mod_ref[0]
    gate1 = mod[:, 2 * D_MODEL:3 * D_MODEL]
    sh2 = mod[:, 3 * D_MODEL:4 * D_MODEL]
    sc2 = mod[:, 4 * D_MODEL:5 * D_MODEL]
    gate2 = mod[:, 5 * D_MODEL:6 * D_MODEL]
    mix = None
    for i, y_ref in enumerate((ya_ref, yb_ref, yc_ref, yd_ref)):
        part = _dot(y_ref[...].astype(BF16), wo_ref[i * GROUP_WIDTH:(i + 1) * GROUP_WIDTH, :])
        mix = part if mix is None else mix + part
    x1 = x_ref[...] + gate1 * mix
    h = _rms_rows(x1, D_MODEL) * g2_ref[...]
    hb = (h * (1.0 + sc2) + sh2).astype(BF16)
    ff = None
    for lo, hi in FF_CHUNKS:
        g = _dot(hb, wg_ref[:, lo:hi])
        up = _dot(hb, wu_ref[:, lo:hi])
        act = ((g * jax.nn.sigmoid(g)) * up).astype(BF16)
        part = _dot(act, wd_ref[lo:hi, :])
        ff = part if ff is None else ff + part
    x2 = x1 + gate2 * ff
    if final:
        x2 = _rms_rows(x2, D_MODEL) * gf_ref[...]
    o_ref[...] = x2


def _mix_ffn(x, ys, mod, lw, gf, *, nb, n, final):
    T = nb * n
    tm = TOKEN_TILE
    npt = n // tm

    def tok(width):
        return pl.BlockSpec((tm, width), lambda i: (i, 0))

    in_specs = [tok(D_MODEL), tok(256), tok(256), tok(256), tok(256), mod.spec(lambda i: i // npt),
                lw.spec("g2"), lw.spec("w_out"), lw.spec("w_gu", col_blocks=2, col_block=0),
                lw.spec("w_gu", col_blocks=2, col_block=1), lw.spec("w_down")]
    args = [x, *ys, mod.table, lw["g2"], lw["w_out"], lw["w_gu"], lw["w_gu"], lw["w_down"]]
    if final:
        in_specs.append(_resident((1, D_MODEL)))
        args.append(gf)
    return pl.pallas_call(
        functools.partial(_mix_ffn_kernel, final=final),
        grid=(T // tm,),
        in_specs=in_specs,
        out_specs=tok(D_MODEL),
        out_shape=jax.ShapeDtypeStruct((T, D_MODEL), F32),
        compiler_params=_params("arbitrary"),
        name="mix_ffn_final" if final else "mix_ffn",
    )(*args)


def _block_diag(w):
    L, G, c, e = w.shape
    return jnp.einsum('lgce,gh->lgche', w, jnp.eye(G, dtype=w.dtype)).reshape(L, G * c, G * e)


def _rot_cols(w):
    return jnp.concatenate([-w[..., 16:32], w[..., 0:16]], axis=-1)


def _stack_weights(p):
    w_in = p["w_in"]
    o1 = MLA_Q_RANK
    o2 = o1 + MLA_KV_RANK
    o3 = o2 + MLA_ROPE
    c_q, c_kv, k_r, rest = w_in[..., :o1], w_in[..., o1:o2], w_in[..., o2:o3], w_in[..., o3:]
    z = lambda n: jnp.zeros((DEPTH, D_MODEL, n), F32)
    w_in_eff = jnp.concatenate([c_q, k_r, z(32), c_kv, z(64), _rot_cols(k_r), z(32), rest], axis=-1)

    w_uq = p["mla_w_uq"]
    qd = MLA_NOPE + MLA_ROPE
    wq_parts, wqr_parts = [], []
    zq = lambda n: jnp.zeros((DEPTH, MLA_Q_RANK, n), F32)
    for h in range(MLA_HEADS):
        wh = w_uq[..., h * qd:(h + 1) * qd]
        wq_parts += [wh, zq(MLA_SLOT - qd)]
        wqr_parts += [zq(MLA_NOPE), _rot_cols(wh[..., MLA_NOPE:]), zq(MLA_SLOT - qd)]
    pad_rows = lambda w: jnp.pad(w, ((0, 0), (0, 256 - MLA_Q_RANK), (0, 0)))
    w_ukv = p["mla_w_ukv"]
    wk_parts, wv_parts = [], []
    zk = jnp.zeros((DEPTH, MLA_KV_RANK, MLA_SLOT - MLA_NOPE), F32)
    for h in range(MLA_HEADS):
        base = h * (MLA_NOPE + MLA_V)
        wk_parts += [w_ukv[..., base:base + MLA_NOPE], zk]
        wv_parts.append(w_ukv[..., base + MLA_NOPE:base + MLA_NOPE + MLA_V])

    w_r, w_i, b_r, b_i = p["lru_w_r"], p["lru_w_i"], p["lru_b_r"], p["lru_b_i"]
    w_gate = jnp.concatenate([_block_diag(w_r[:, 0]), _block_diag(w_r[:, 1]),
                              _block_diag(w_i[:, 0]), _block_diag(w_i[:, 1])], axis=-1)
    b_gate = jnp.concatenate([b_r[:, 0], b_r[:, 1], b_i[:, 0], b_i[:, 1]], axis=-1)
    row = lambda v: v[:, None, :]
    return {
        "g1": row(p["norm1_g"]),
        "g2": row(p["norm2_g"]),
        "w_in": w_in_eff.astype(BF16),
        "gq": row(jnp.pad(p["mla_q_norm_g"], ((0, 0), (0, 256 - MLA_Q_RANK)))),
        "gkv": row(p["mla_kv_norm_g"]),
        "wq": pad_rows(jnp.concatenate(wq_parts, axis=-1)).astype(BF16),
        "wqr": pad_rows(jnp.concatenate(wqr_parts, axis=-1)).astype(BF16),
        "wkv": jnp.concatenate(wk_parts + wv_parts, axis=-1).astype(BF16),
        "conv_w": p["lru_conv_w"],
        "conv_b": row(p["lru_conv_b"]),
        "w_gate": w_gate.astype(BF16),
        "b_gate": row(b_gate),
        "lru_lambda": p["lru_lambda"],
        "w_pool": _block_diag(p["pool_w"]).astype(BF16),
        "pool_scale": row(p["pool_scale"]),
        "diff_lambda": p["diff_lambda"],
        "diff_g": row(jnp.tile(p["diff_norm_g"], (1, DIFF_HEADS))),
        "w_out": p["w_out"].astype(BF16),
        "w_gu": p["w_gu"].astype(BF16),
        "w_down": p["w_down"].astype(BF16),
    }


def _rope_tables(n, positional):
    quarter = MLA_ROPE // 4
    if positional:
        t = jnp.arange(n)
        row = (t // GRID_W).astype(F32)
        col = (t % GRID_W).astype(F32)
        inv = ROPE_BASE ** (-jnp.arange(quarter, dtype=F32) / quarter)
        ang = jnp.concatenate([row[:, None] * inv, col[:, None] * inv], axis=-1)
        cos, sin = jnp.cos(ang), jnp.sin(ang)
    else:
        cos, sin = jnp.ones((n, 16), F32), jnp.zeros((n, 16), F32)
    scale = LOG2E / math.sqrt(MLA_NOPE + MLA_ROPE)
    place = np.zeros((32, TAB_WIDTH), np.float32)
    offset = np.zeros((1, TAB_WIDTH), np.float32)
    offset[0, 0:64] = offset[0, 96:128] = scale
    for i in range(16):
        for half in (64, 80):
            place[i, half + i] = scale
            place[16 + i, 128 + half + i] = scale
            place[i, 256 + half + i] = 1.0
            place[16 + i, 384 + half + i] = 1.0
        for grp in range(8):
            place[i, 512 + 32 * grp + i] = place[i, 512 + 32 * grp + 16 + i] = 1.0
            place[16 + i, 768 + 32 * grp + i] = -1.0
            place[16 + i, 1024 + 32 * grp + 16 + i] = 1.0
    return jnp.dot(jnp.concatenate([cos, sin], axis=1), place, precision=lax.Precision.HIGHEST) + offset


def _layer(x, mod, lw, tabs, layer_idx, ctx, gf, *, nb, n, final):
    emit_cache = ctx is None
    tok_nb, tok_n = (1, nb * n) if mod.shared else (nb, n)
    outs = _inproj(x, mod, lw, tabs, nb=tok_nb, n=tok_n, emit_cache=emit_cache)
    q, k, vt, u_lru, u_pool, dq, dk, dvt = outs[:8]
    lam_init = 0.8 - 0.6 * math.exp(-0.3 * layer_idx)
    if ctx is None:
        h0 = jnp.zeros((1, 2, LRU_WIDTH), F32)
        h0_block = lambda b: 0
        mla_ctx = diff_ctx = None
    else:
        ckv, kr_pad, cdk, cdv, h0 = ctx
        p = ckv.shape[0] // (nb * DEPTH)
        h0_block = lambda b: b * DEPTH + layer_idx
        kc, vtc, dkc, dvtc = _ctx_prep(ckv, kr_pad, cdk, cdv, lw, nb=nb, p=p)
        mla_ctx = (kc, vtc)
        diff_ctx = (dkc, dvtc)
    y_mla = _mla_attn(q, k, vt, mla_ctx, nb=nb, n=n)
    y_lru, st = _lru(u_lru, h0, h0_block, lw, nb=nb, n=n)
    y_pool = _pool(u_pool, lw, nb=nb, n=n)
    y_diff = _diff_attn(dq, dk, dvt, diff_ctx, lw, nb=nb, n=n, lam_init=lam_init)
    x2 = _mix_ffn(x, (y_mla, y_lru, y_pool, y_diff), mod, lw, gf, nb=tok_nb, n=tok_n, final=final)
    cache = (outs[8], outs[9][:, 64:96], outs[10], outs[11], st) if emit_cache else None
    return x2, cache


def kernel(x_prompt, x_sample, cache_mla_ckv, cache_mla_krope, cache_diff_k, cache_diff_v, state_lru,
           c, c_ctx, w_ada, b_ada, norm1_g, norm2_g, w_in, mla_q_norm_g, mla_w_uq, mla_kv_norm_g,
           mla_w_ukv, lru_conv_w, lru_conv_b, lru_w_r, lru_b_r, lru_w_i, lru_b_i, lru_lambda, pool_w,
           pool_scale, diff_lambda, diff_norm_g, w_out, w_gu, w_down, final_norm_g):
    p = {
        "norm1_g": norm1_g, "norm2_g": norm2_g, "w_in": w_in, "mla_q_norm_g": mla_q_norm_g,
        "mla_w_uq": mla_w_uq, "mla_kv_norm_g": mla_kv_norm_g, "mla_w_ukv": mla_w_ukv,
        "lru_conv_w": lru_conv_w, "lru_conv_b": lru_conv_b, "lru_w_r": lru_w_r, "lru_b_r": lru_b_r,
        "lru_w_i": lru_w_i, "lru_b_i": lru_b_i, "lru_lambda": lru_lambda, "pool_w": pool_w,
        "pool_scale": pool_scale, "diff_lambda": diff_lambda, "diff_norm_g": diff_norm_g,
        "w_out": w_out, "w_gu": w_gu, "w_down": w_down,
    }
    Bp, Np, _ = x_prompt.shape
    Bs, Ns, _ = x_sample.shape
    P = cache_mla_ckv.shape[2]

    cond_all = jnp.concatenate([c, c_ctx[None, :], jnp.zeros((MOD_ROWS - Bs - 1, D_MODEL), F32)], axis=0)
    mod_table = _ada(cond_all, w_ada, b_ada).reshape(DEPTH * MOD_ROWS, 1, 6 * D_MODEL)
    tabs_p = _rope_tables(Bp * Np, positional=False)
    tabs_s = _rope_tables(Ns, positional=True)
    kr_pad = jnp.pad(cache_mla_krope, ((0, 0), (0, 0), (0, 0), (MLA_NOPE, MLA_SLOT - MLA_NOPE - MLA_ROPE)))
    flat = lambda a, w: a.reshape(Bs * DEPTH * P, w)
    ctx = (flat(cache_mla_ckv, MLA_KV_RANK), flat(kr_pad, MLA_SLOT), flat(cache_diff_k, 256),
           flat(cache_diff_v, 256), state_lru.reshape(Bs * DEPTH, 2, LRU_WIDTH))
    gf = final_norm_g[None, :]
    stacked = _stack_weights(p)

    xp = x_prompt.reshape(Bp * Np, D_MODEL)
    xs = x_sample.reshape(Bs * Ns, D_MODEL)
    caches = []
    for l in range(DEPTH):
        lw = _LayerWeights(stacked, l)
        final = l == DEPTH - 1
        mod_p = _Mod(mod_table, l * MOD_ROWS + Bs, shared=True)
        mod_s = _Mod(mod_table, l * MOD_ROWS, shared=False)
        xp, cache = _layer(xp, mod_p, lw, tabs_p, l, None, gf, nb=Bp, n=Np, final=final)
        caches.append(cache)
        xs, _ = _layer(xs, mod_s, lw, tabs_s, l, ctx, gf, nb=Bs, n=Ns, final=final)

    stack = lambda i, w: jnp.stack([cc[i].reshape(Bp, Np, w) for cc in caches], axis=1)
    new_mla_ckv = stack(0, MLA_KV_RANK)
    new_mla_krope = stack(1, MLA_ROPE)
    new_diff_k = stack(2, 256).reshape(Bp, DEPTH, Np, DIFF_HEADS, 2, DIFF_DIM)
    new_diff_v = stack(3, 256).reshape(Bp, DEPTH, Np, DIFF_HEADS, 2 * DIFF_DIM)
    new_state_lru = jnp.stack([cc[4] for cc in caches], axis=1)
    return (xp.reshape(Bp, Np, D_MODEL), xs.reshape(Bs, Ns, D_MODEL),
            new_mla_ckv, new_mla_krope, new_diff_k, new_diff_v, new_state_lru)
```

```python
import functools
import math

import jax
import jax.numpy as jnp
import numpy as np
from jax import lax
from jax.experimental import pallas as pl
from jax.experimental.pallas import tpu as pltpu

F32 = jnp.float32
BF16 = jnp.bfloat16

D_MODEL = 1024
DEPTH = 2
GRID_W = 64
GROUP_WIDTH = 256
MLA_HEADS = 4
MLA_NOPE = 64
MLA_ROPE = 32
MLA_V = 64
MLA_Q_RANK = 192
MLA_KV_RANK = 128
MLA_SLOT = 128
LRU_WIDTH = 256
LRU_C = 8.0
POOL_WINDOWS = (2, 4, 8, 16)
POOL_CH = 64
DIFF_HEADS = 4
DIFF_DIM = 32
HEAD_V = 64
FF_HIDDEN = 2816
FF_CHUNKS = ((0, 1536), (1536, 2816))
ROPE_BASE = 10000.0
EPS = 1e-6
IN_EFF = 2048
HALO = 8
SCAN_RUN = 4
VT_ROWS = 80
ATT_TQ = 256
TOKEN_TILE = 512
TAB_WIDTH = 4 * 128 + 3 * 256
MOD_ROWS = 16
LOG2E = math.log2(math.e)

VMEM_LIMIT_BYTES = 56 * 1024 * 1024

_NT = (((1,), (1,)), ((), ()))


def _params(*sem):
    return pltpu.CompilerParams(dimension_semantics=sem, vmem_limit_bytes=VMEM_LIMIT_BYTES)


def _resident(shape):
    zeros = (0,) * len(shape)
    return pl.BlockSpec(shape, lambda *_: zeros, pipeline_mode=pl.Buffered(1))


def _dot(a, b):
    return jnp.dot(a, b, preferred_element_type=F32)


def _dot_nt(a, b):
    return lax.dot_general(a, b, _NT, preferred_element_type=F32)


def _rms_rows(x, width):
    ms = jnp.sum(x * x, axis=-1, keepdims=True) * (1.0 / width)
    return x * lax.rsqrt(ms + EPS)


def _store_vt(vt_ref, v):
    vt = v.T
    rows = v.shape[0]
    pad = VT_ROWS - HEAD_V
    ones_row = jnp.where(lax.broadcasted_iota(jnp.int32, (pad, rows), 0) == 0, 1.0, 0.0).astype(BF16)
    for hh in range(vt_ref.shape[0]):
        vt_ref[hh, 0:HEAD_V, :] = vt[hh * HEAD_V:(hh + 1) * HEAD_V, :].astype(BF16)
        vt_ref[hh, HEAD_V:VT_ROWS, :] = ones_row


class _Mod:
    def __init__(self, table, row0, shared):
        self.table, self.row0, self.shared = table, row0, shared

    def spec(self, batch_of):
        row0 = self.row0
        if self.shared:
            return pl.BlockSpec((1, 1, 6 * D_MODEL), lambda *g: (row0, 0, 0))
        return pl.BlockSpec((1, 1, 6 * D_MODEL), lambda *g: (row0 + batch_of(*g), 0, 0))


class _LayerWeights:
    def __init__(self, stacked, layer):
        self.stacked, self.layer = stacked, layer

    def __getitem__(self, name):
        return self.stacked[name]

    def spec(self, name, col_blocks=1, col_block=0):
        layer = self.layer
        _, rows, cols = self.stacked[name].shape
        return pl.BlockSpec((None, rows, cols // col_blocks), lambda *_: (layer, 0, col_block),
                            pipeline_mode=pl.Buffered(1))


def _ada_kernel(cond_ref, w_ref, b_ref, out_ref):
    c = cond_ref[...]
    s = c * jax.nn.sigmoid(c)
    out_ref[0] = _dot(s.astype(BF16), w_ref[0].astype(BF16)) + b_ref[0]


def _ada(cond_all, w_ada, b_ada):
    rows = cond_all.shape[0]
    tn = 1536
    return pl.pallas_call(
        _ada_kernel,
        grid=(DEPTH, 6 * D_MODEL // tn),
        in_specs=[
            pl.BlockSpec((rows, D_MODEL), lambda l, j: (0, 0)),
            pl.BlockSpec((1, D_MODEL, tn), lambda l, j: (l, 0, j)),
            pl.BlockSpec((1, 1, tn), lambda l, j: (l, 0, j)),
        ],
        out_specs=pl.BlockSpec((1, rows, tn), lambda l, j: (l, 0, j)),
        out_shape=jax.ShapeDtypeStruct((DEPTH, rows, 6 * D_MODEL), F32),
        compiler_params=_params("arbitrary", "arbitrary"),
        name="ada_mod",
    )(cond_all, w_ada, b_ada.reshape(DEPTH, 1, 6 * D_MODEL))


def _inproj_kernel(x_ref, mod_ref, g1_ref, win_ref, gq_ref, gkv_ref, wq_ref, wqr_ref, wkv_ref, tab_ref,
                   q_out, k_out, vt_out, lru_out, pool_out, dq_out, dk_out, dvt_out, *cache_outs):
    cosq_ref, sinq_ref, cosk_ref, sink_ref = (tab_ref.at[:, i * 128:(i + 1) * 128] for i in range(4))
    cosd_ref, sina_ref, sinb_ref = (tab_ref.at[:, 512 + i * 256:768 + i * 256] for i in range(3))
    x = x_ref[...]
    mod = mod_ref[0]
    sh1 = mod[:, 0:D_MODEL]
    sc1 = mod[:, D_MODEL:2 * D_MODEL]
    h = _rms_rows(x, D_MODEL) * g1_ref[...]
    hb = (h * (1.0 + sc1) + sh1).astype(BF16)

    u_mla = _dot(hb, win_ref[:, 0:512])
    u_pd = _dot(hb, win_ref[:, 1024:1536])
    u_kv = _dot(hb, win_ref[:, 1536:2048])
    t01 = u_mla[:, 0:256]
    lane = lax.broadcasted_iota(jnp.int32, (1, 256), 1)
    cq = jnp.where(lane < MLA_Q_RANK, t01, 0.0)
    cqn = (_rms_rows(cq, MLA_Q_RANK) * gq_ref[...]).astype(BF16)
    qa = _dot(cqn, wq_ref[...])
    qr = _dot(cqn, wqr_ref[...])
    cosq = cosq_ref[...]
    sinq = sinq_ref[...]
    ckv = u_mla[:, 256:384]
    lat = _rms_rows(ckv, MLA_KV_RANK) * gkv_ref[...]
    latb = lat.astype(BF16)
    kkv = _dot(latb, wkv_ref[...])
    kk = kkv[:, 0:MLA_HEADS * MLA_SLOT]
    _store_vt(vt_out, kkv[:, MLA_HEADS * MLA_SLOT:])
    t1 = t01[:, 128:256]
    t3 = u_mla[:, 384:512]
    kro = t1 * cosk_ref[...] + t3 * sink_ref[...]
    for hh in range(MLA_HEADS):
        sl = slice(hh * MLA_SLOT, (hh + 1) * MLA_SLOT)
        q_out[hh] = (qa[:, sl] * cosq + qr[:, sl] * sinq).astype(q_out.dtype)
        k_out[hh] = (kk[:, sl] + kro).astype(k_out.dtype)

    lru_out[...] = _dot(hb, win_ref[:, 512:1024])
    pool_out[...] = u_pd[:, 0:256]

    cosd = cosd_ref[...]
    sina = sina_ref[...]
    sinb = sinb_ref[...]

    def rope(t):
        return t * cosd + pltpu.roll(t, 256 - 16, 1) * sina + pltpu.roll(t, 16, 1) * sinb

    dq = u_pd[:, 256:512]
    dk = u_kv[:, 0:256]
    dv = u_kv[:, 256:512]
    dq_out[...] = (rope(dq) * (LOG2E / math.sqrt(DIFF_DIM))).astype(dq_out.dtype)
    dk_out[...] = rope(dk).astype(dk_out.dtype)
    _store_vt(dvt_out, dv)

    if cache_outs:
        lat_out, kr_out, dk_raw_out, dv_raw_out = cache_outs
        lat_out[...] = lat
        kr_out[...] = t1
        dk_raw_out[...] = dk
        dv_raw_out[...] = dv


def _inproj(x, mod, lw, tabs, *, nb, n, emit_cache):
    T = nb * n
    tm = TOKEN_TILE
    npt = n // tm
    row_blk = lambda j, b: b * npt + j

    def tok(width):
        return pl.BlockSpec((tm, width), lambda j, b: (row_blk(j, b), 0))

    def tab(width):
        return pl.BlockSpec((tm, width), lambda j, b: (j, 0))

    head = pl.BlockSpec((MLA_HEADS, tm, MLA_SLOT), lambda j, b: (0, row_blk(j, b), 0))
    vt_spec = pl.BlockSpec((MLA_HEADS, VT_ROWS, tm), lambda j, b: (0, 0, row_blk(j, b)))
    wnames = ("g1", "w_in", "gq", "gkv", "wq", "wqr", "wkv")
    in_specs = [tok(D_MODEL), mod.spec(lambda j, b: b)] + [lw.spec(nm) for nm in wnames] + [
        tab(TAB_WIDTH)]
    out_specs = [head, head, vt_spec, tok(512), tok(256), tok(256), tok(256), vt_spec]
    vt_shape = jax.ShapeDtypeStruct((MLA_HEADS, VT_ROWS, T), BF16)
    out_shape = [
        jax.ShapeDtypeStruct((MLA_HEADS, T, MLA_SLOT), BF16),
        jax.ShapeDtypeStruct((MLA_HEADS, T, MLA_SLOT), BF16),
        vt_shape,
        jax.ShapeDtypeStruct((T, 512), F32),
        jax.ShapeDtypeStruct((T, 256), F32),
        jax.ShapeDtypeStruct((T, 256), BF16),
        jax.ShapeDtypeStruct((T, 256), BF16),
        vt_shape,
    ]
    if emit_cache:
        out_specs += [tok(128), tok(128), tok(256), tok(256)]
        out_shape += [jax.ShapeDtypeStruct((T, 128), F32), jax.ShapeDtypeStruct((T, 128), F32),
                      jax.ShapeDtypeStruct((T, 256), F32), jax.ShapeDtypeStruct((T, 256), F32)]
    return pl.pallas_call(
        _inproj_kernel,
        grid=(npt, nb),
        in_specs=in_specs,
        out_specs=out_specs,
        out_shape=out_shape,
        compiler_params=_params("arbitrary", "arbitrary"),
        name="inproj_cache" if emit_cache else "inproj",
    )(x, mod.table, *[lw[nm] for nm in wnames], tabs)


def _ctx_prep_kernel(ckv_ref, kr_ref, dk_ref, dv_ref, wkv_ref, k_out, vt_out, dk_out, dvt_out):
    latb = ckv_ref[...].astype(BF16)
    kkv = _dot(latb, wkv_ref[...])
    kk = kkv[:, 0:MLA_HEADS * MLA_SLOT]
    kr = kr_ref[...]
    for hh in range(MLA_HEADS):
        k_out[hh] = (kk[:, hh * MLA_SLOT:(hh + 1) * MLA_SLOT] + kr).astype(k_out.dtype)
    _store_vt(vt_out, kkv[:, MLA_HEADS * MLA_SLOT:])
    dk_out[...] = dk_ref[...].astype(dk_out.dtype)
    _store_vt(dvt_out, dv_ref[...])


def _ctx_prep(ckv, kr_pad, cdk, cdv, lw, *, nb, p):
    T = nb * p
    layer = lw.layer
    cache_row = lambda w: pl.BlockSpec((p, w), lambda b: (b * DEPTH + layer, 0))
    row = lambda w: pl.BlockSpec((p, w), lambda b: (b, 0))
    vt_spec = pl.BlockSpec((MLA_HEADS, VT_ROWS, p), lambda b: (0, 0, b))
    vt_shape = jax.ShapeDtypeStruct((MLA_HEADS, VT_ROWS, T), BF16)
    return pl.pallas_call(
        _ctx_prep_kernel,
        grid=(nb,),
        in_specs=[cache_row(128), cache_row(128), cache_row(256), cache_row(256),
                  lw.spec("wkv")],
        out_specs=[pl.BlockSpec((MLA_HEADS, p, MLA_SLOT), lambda b: (0, b, 0)), vt_spec, row(256), vt_spec],
        out_shape=[jax.ShapeDtypeStruct((MLA_HEADS, T, MLA_SLOT), BF16), vt_shape,
                   jax.ShapeDtypeStruct((T, 256), BF16), vt_shape],
        compiler_params=_params("arbitrary"),
        name="ctx_prep",
    )(ckv, kr_pad, cdk, cdv, lw["wkv"])


SAFE_DENOM = 2.0 ** -60
E_BUFS = 3
BOUND_SLACK = 1.02


def _scores(k_new, k_ctx, q):
    sn = _dot_nt(k_new(), q)
    sc = _dot_nt(k_ctx(), q) if k_ctx is not None else None
    return sn, sc


def _exact_shift(k_new, k_ctx, q):
    sn, sc = _scores(k_new, k_ctx, q)
    m = jnp.max(sn, axis=0, keepdims=True)
    if sc is not None:
        m = jnp.maximum(m, jnp.max(sc, axis=0, keepdims=True))
    return m


def _bound_shift(q, key_norm2):
    qf = q.astype(F32)
    ones = jnp.ones((8, q.shape[1]), BF16)
    q_norm2 = _dot_nt(ones, (qf * qf).astype(BF16))[0:1, :]
    return jnp.sqrt(q_norm2 * key_norm2) * BOUND_SLACK


def _max_row_norm2(k_new, k_ctx, col_sum):
    def one(k):
        kf = k.astype(F32)
        return jnp.max(_dot((kf * kf).astype(BF16), col_sum), axis=0, keepdims=True)
    m = one(k_new)
    if k_ctx is not None:
        m = jnp.maximum(m, one(k_ctx))
    return m * BOUND_SLACK


def _exp_stage(e_buf, k_new, k_ctx, q, shift, n_ctx):
    sn, sc = _scores(k_new, k_ctx, q)
    e_buf[n_ctx:, :] = jnp.exp2(sn - shift).astype(BF16)
    if sc is not None:
        e_buf[0:n_ctx, :] = jnp.exp2(sc - shift).astype(BF16)


def _value_stage(e_buf, vt_new, vt_ctx, n_ctx):
    o = _dot(vt_new(), e_buf[n_ctx:, :])
    if vt_ctx is not None:
        o = o + _dot(vt_ctx(), e_buf[0:n_ctx, :])
    return o


def _run_pipeline(n_maps, exp_stage, value_stage):
    ahead = E_BUFS - 1
    for u in range(min(ahead, n_maps)):
        exp_stage(u)
    for u in range(n_maps):
        if u + ahead < n_maps:
            exp_stage(u + ahead)
        value_stage(u)


def _att_scratch(nk, key_shape):
    scratch = [pltpu.VMEM((8, 128), F32),
               pltpu.VMEM((MLA_HEADS * HEAD_V, ATT_TQ), F32)]
    if key_shape is not None:
        scratch += [pltpu.VMEM(key_shape, BF16), pltpu.VMEM((MLA_HEADS, VT_ROWS, nk), BF16)]
    return scratch + [pltpu.VMEM((nk, ATT_TQ), BF16)] * E_BUFS


def _att_nsub(n):
    return 2 if n % (2 * ATT_TQ) == 0 else 1


def _mla_attn_kernel(*refs, has_ctx, nsub):
    if has_ctx:
        q_ref, k_ref, vt_ref, kc_ref, vtc_ref, o_ref, kn2, ot, keys, vals, *e_bufs = refs
        n_ctx = kc_ref.shape[1]
    else:
        q_ref, k_ref, vt_ref, o_ref, kn2, ot, *e_bufs = refs
        keys, vals = k_ref, vt_ref

    @pl.when(pl.program_id(1) == 0)
    def _():
        ones = jnp.ones((MLA_SLOT, 128), BF16)
        for hh in range(MLA_HEADS):
            kn2[hh:hh + 1, :] = _max_row_norm2(k_ref[hh], kc_ref[hh] if has_ctx else None, ones)
        if has_ctx:
            keys[:, 0:n_ctx, :] = kc_ref[...]
            keys[:, n_ctx:, :] = k_ref[...]
            vals[:, :, 0:n_ctx] = vtc_ref[...]
            vals[:, :, n_ctx:] = vt_ref[...]

    def run(exact):
        denoms = []

        def exp_stage(u):
            t, hh = divmod(u, MLA_HEADS)
            q = q_ref[hh, t * ATT_TQ:(t + 1) * ATT_TQ, :]
            k_all = lambda: keys[hh]
            shift = _exact_shift(k_all, None, q) if exact else _bound_shift(q, kn2[hh:hh + 1, 0:1])
            _exp_stage(e_bufs[u % E_BUFS], k_all, None, q, shift, 0)

        def value_stage(u):
            t, hh = divmod(u, MLA_HEADS)
            o = _value_stage(e_bufs[u % E_BUFS], lambda: vals[hh], None, 0)
            denom = o[HEAD_V:HEAD_V + 1, :]
            denoms.append(denom)
            ot[hh * HEAD_V:(hh + 1) * HEAD_V, :] = o[0:HEAD_V, :] * (1.0 / denom)
            if hh == MLA_HEADS - 1:
                o_ref[t * ATT_TQ:(t + 1) * ATT_TQ, :] = ot[...].T

        _run_pipeline(nsub * MLA_HEADS, exp_stage, value_stage)
        return jnp.min(functools.reduce(jnp.minimum, denoms))

    denom_min = run(exact=False)

    @pl.when(jnp.logical_not(denom_min >= SAFE_DENOM))
    def _():
        run(exact=True)


def _mla_attn(q, k, vt, ctx, *, nb, n):
    nsub = _att_nsub(n)
    tq = nsub * ATT_TQ
    npt = n // tq
    H, S = MLA_HEADS, MLA_SLOT
    in_specs = [
        pl.BlockSpec((H, tq, S), lambda b, j: (0, b * npt + j, 0)),
        pl.BlockSpec((H, n, S), lambda b, j: (0, b, 0)),
        pl.BlockSpec((H, VT_ROWS, n), lambda b, j: (0, 0, b)),
    ]
    args = [q, k, vt]
    n_ctx = 0
    if ctx is not None:
        n_ctx = ctx[0].shape[1] // nb
        in_specs += [
            pl.BlockSpec((H, n_ctx, S), lambda b, j: (0, b, 0)),
            pl.BlockSpec((H, VT_ROWS, n_ctx), lambda b, j: (0, 0, b)),
        ]
        args += list(ctx)
    return pl.pallas_call(
        functools.partial(_mla_attn_kernel, has_ctx=ctx is not None, nsub=nsub),
        grid=(nb, npt),
        in_specs=in_specs,
        out_specs=pl.BlockSpec((tq, 256), lambda b, j: (b * npt + j, 0)),
        out_shape=jax.ShapeDtypeStruct((nb * n, 256), F32),
        scratch_shapes=_att_scratch(n + n_ctx, (H, n + n_ctx, S) if ctx is not None else None),
        compiler_params=_params("arbitrary", "arbitrary"),
        name="mla_attn_ctx" if ctx is not None else "mla_attn",
    )(*args)


def _diff_attn_kernel(*refs, has_ctx, nsub, lam_init):
    if has_ctx:
        lv_ref, g_ref, q_ref, k_ref, vt_ref, kc_ref, vtc_ref, o_ref, kn2, ot, keys, vals, *e_bufs = refs
        n_ctx = kc_ref.shape[0]
    else:
        lv_ref, g_ref, q_ref, k_ref, vt_ref, o_ref, kn2, ot, *e_bufs = refs
        keys, vals = k_ref, vt_ref
    lv = lv_ref[...]
    lam = (jnp.exp(jnp.sum(lv[0:1] * lv[1:2], axis=-1, keepdims=True))
           - jnp.exp(jnp.sum(lv[2:3] * lv[3:4], axis=-1, keepdims=True)) + lam_init)
    lane128 = lax.broadcasted_iota(jnp.int32, (1, 128), 1)
    n_pairs = 2 * DIFF_HEADS

    @pl.when(pl.program_id(1) == 0)
    def _():
        dim = lax.broadcasted_iota(jnp.int32, (256, 128), 0)
        col = lax.broadcasted_iota(jnp.int32, (256, 128), 1)
        indicator = jnp.where(dim // DIFF_DIM == col, 1.0, 0.0).astype(BF16)
        kn2[0:1, :] = _max_row_norm2(k_ref[...], kc_ref[...] if has_ctx else None, indicator)
        if has_ctx:
            keys[0:n_ctx, :] = kc_ref[...]
            keys[n_ctx:, :] = k_ref[...]
            vals[:, :, 0:n_ctx] = vtc_ref[...]
            vals[:, :, n_ctx:] = vt_ref[...]

    def run(exact):
        denoms = []
        outs = {}

        def exp_stage(u):
            t, p = divmod(u, n_pairs)
            tile = slice((p * DIFF_DIM // 128) * 128, (p * DIFF_DIM // 128 + 1) * 128)
            k_new = lambda: keys[:, tile]
            k_ctx = None
            q = q_ref[t * ATT_TQ:(t + 1) * ATT_TQ, tile]
            lo = p * DIFF_DIM - tile.start
            in_pair = (lane128 >= lo) & (lane128 < lo + DIFF_DIM)
            qm = jnp.where(in_pair, q, jnp.zeros_like(q))
            shift = _exact_shift(k_new, k_ctx, qm) if exact else _bound_shift(qm, kn2[0:1, p:p + 1])
            _exp_stage(e_bufs[u % E_BUFS], k_new, k_ctx, qm, shift, 0)

        def value_stage(u):
            t, p = divmod(u, n_pairs)
            hh = p // 2
            o = _value_stage(e_bufs[u % E_BUFS], lambda: vals[hh], None, 0)
            denom = o[HEAD_V:HEAD_V + 1, :]
            denoms.append(denom)
            outs[u] = (o[0:HEAD_V, :], denom)
            if p % 2 == 1:
                (o0, l0), (o1, l1) = outs.pop(u - 1), outs.pop(u)
                o = o0 * (1.0 / l0) - o1 * (lam / l1)
                msq = jnp.sum(o * o, axis=0, keepdims=True) * (1.0 / HEAD_V)
                ot[hh * HEAD_V:(hh + 1) * HEAD_V, :] = o * lax.rsqrt(msq + EPS)
            if p == n_pairs - 1:
                o_ref[t * ATT_TQ:(t + 1) * ATT_TQ, :] = (ot[...].T * g_ref[...]) * (1.0 - lam_init)

        _run_pipeline(nsub * n_pairs, exp_stage, value_stage)
        return jnp.min(functools.reduce(jnp.minimum, denoms))

    denom_min = run(exact=False)

    @pl.when(jnp.logical_not(denom_min >= SAFE_DENOM))
    def _():
        run(exact=True)


def _diff_attn(q, k, vt, ctx, lw, *, nb, n, lam_init):
    nsub = 1
    tq = nsub * ATT_TQ
    npt = n // tq
    in_specs = [
        lw.spec("diff_lambda"),
        lw.spec("diff_g"),
        pl.BlockSpec((tq, 256), lambda b, j: (b * npt + j, 0)),
        pl.BlockSpec((n, 256), lambda b, j: (b, 0)),
        pl.BlockSpec((DIFF_HEADS, VT_ROWS, n), lambda b, j: (0, 0, b)),
    ]
    args = [lw["diff_lambda"], lw["diff_g"], q, k, vt]
    n_ctx = 0
    if ctx is not None:
        n_ctx = ctx[0].shape[0] // nb
        in_specs += [pl.BlockSpec((n_ctx, 256), lambda b, j: (b, 0)),
                     pl.BlockSpec((DIFF_HEADS, VT_ROWS, n_ctx), lambda b, j: (0, 0, b))]
        args += list(ctx)
    return pl.pallas_call(
        functools.partial(_diff_attn_kernel, has_ctx=ctx is not None, nsub=nsub, lam_init=lam_init),
        grid=(nb, npt),
        in_specs=in_specs,
        out_specs=pl.BlockSpec((tq, 256), lambda b, j: (b * npt + j, 0)),
        out_shape=jax.ShapeDtypeStruct((nb * n, 256), F32),
        scratch_shapes=_att_scratch(n + n_ctx, (n + n_ctx, 256) if ctx is not None else None),
        compiler_params=_params("arbitrary", "arbitrary"),
        name="diff_attn_ctx" if ctx is not None else "diff_attn",
    )(*args)


def _shift_rows(v, k):
    return pltpu.roll(v, (-k) % v.shape[0], 0)


def _scan_strided(a_ref, b_ref, h_ref, row0, carry, n_rows, reverse):
    sub = lax.broadcasted_iota(jnp.int32, (8, 128), 0)
    span = 8 * SCAN_RUN
    order = tuple(range(SCAN_RUN))[::-1] if reverse else tuple(range(SCAN_RUN))
    starts = tuple(range(0, n_rows, span))[::-1] if reverse else tuple(range(0, n_rows, span))
    carries = []
    for lt in range(a_ref.shape[0]):
        c_in = carry[:, lt * 128:(lt + 1) * 128]
        for start in starts:
            tile = lambda ref, g: ref[lt, pl.ds(row0 + start + g, 8, stride=SCAN_RUN), :]
            a = [tile(a_ref, g) for g in range(SCAN_RUN)]
            b = [tile(b_ref, g) for g in range(SCAN_RUN)]
            h = {order[0]: b[order[0]]}
            p = {order[0]: a[order[0]]}
            for prev, g in zip(order, order[1:]):
                h[g] = a[g] * h[prev] + b[g]
                p[g] = a[g] * p[prev]
            pi, hi = p[order[-1]], h[order[-1]]
            for s in (1, 2, 4):
                shift = 8 - s if reverse else s
                valid = (sub < 8 - s) if reverse else (sub >= s)
                pr, hr = pltpu.roll(pi, shift, 0), pltpu.roll(hi, shift, 0)
                hi = jnp.where(valid, pi * hr + hi, hi)
                pi = jnp.where(valid, pi * pr, pi)
            one = 7 if reverse else 1
            first = (sub == 7) if reverse else (sub == 0)
            pe = jnp.where(first, 1.0, pltpu.roll(pi, one, 0))
            he = jnp.where(first, 0.0, pltpu.roll(hi, one, 0))
            c = pe * c_in + he
            for g in range(SCAN_RUN):
                h_ref[lt, pl.ds(start + g, 8, stride=SCAN_RUN), :] = h[g] + p[g] * c
            last = 0 if reverse else 7
            c_in = pi[last:last + 1, :] * c_in + hi[last:last + 1, :]
        carries.append(c_in)
    return jnp.concatenate(carries, axis=1)


def _sigmoid(x):
    return 0.5 * jnp.tanh(0.5 * x) + 0.5


def _gelu_tanh(x):
    return x * (0.5 * (1.0 + jnp.tanh(math.sqrt(2.0 / math.pi) * (x + 0.044715 * (x * x * x)))))


def _lru_kernel(u_ref, h0_ref, cw_ref, cb_ref, wg_ref, bg_ref, lam_ref, y_ref, st_ref,
                xpad, a1s, b1s, a0c, b0c, hc, *, N, T):
    W = LRU_WIDTH
    nc = N // T
    tiles = [slice(lt * 128, (lt + 1) * 128) for lt in range(W // 128)]
    zeros = jnp.zeros((HALO, W), F32)
    xpad[0:HALO, :] = zeros
    xpad[N + HALO:N + 2 * HALO, :] = zeros

    def fill(j, carry):
        r0 = pl.multiple_of(j * T, T)
        xpad[pl.ds(r0 + HALO, T), :] = u_ref[pl.ds(r0, T), 0:W]
        return carry

    lax.fori_loop(0, nc, fill, 0)

    z = -lam_ref[...]
    sp = jnp.maximum(z, 0.0) + jnp.log1p(jnp.exp(-jnp.abs(z)))
    cw = cw_ref[...]
    cb = cb_ref[...]
    bg = bg_ref[...]

    def fwd(j, carry):
        r0 = pl.multiple_of(j * T, T)
        ext = xpad[pl.ds(r0, T + 2 * HALO), :]
        body = slice(HALO, HALO + T)
        xc = cb
        for tap in range(4):
            xc = xc + _shift_rows(ext, tap - 1)[body] * cw[tap:tap + 1]
        g = _sigmoid(_dot(xc.astype(BF16), wg_ref[...]) + bg)
        ab = []
        for d in range(2):
            r = g[:, d * W:(d + 1) * W]
            i = g[:, (2 + d) * W:(3 + d) * W]
            log_a = (-LRU_C * r) * sp[d:d + 1]
            a = jnp.exp(log_a)
            bt = (jnp.sqrt(1.0 - a * a) * i) * xc
            ab.append((a, bt))
        for lt, lanes in enumerate(tiles):
            a0c[lt] = ab[0][0][:, lanes]
            b0c[lt] = ab[0][1][:, lanes]
            a1s[lt, pl.ds(r0, T), :] = ab[1][0][:, lanes]
            b1s[lt, pl.ds(r0, T), :] = ab[1][1][:, lanes]
        carry = _scan_strided(a0c, b0c, hc, 0, carry, T, reverse=False)
        for lt, lanes in enumerate(tiles):
            y_ref[pl.ds(r0, T), lanes] = hc[lt]
        return carry

    cf = lax.fori_loop(0, nc, fwd, h0_ref[0, 0:1, :])

    def bwd(jj, carry):
        r0 = pl.multiple_of((nc - 1 - jj) * T, T)
        carry = _scan_strided(a1s, b1s, hc, r0, carry, T, reverse=True)
        for lt, lanes in enumerate(tiles):
            gb = u_ref[pl.ds(r0, T), W + lt * 128:W + (lt + 1) * 128]
            y_ref[pl.ds(r0, T), lanes] = (y_ref[pl.ds(r0, T), lanes] + hc[lt]) * _gelu_tanh(gb)
        return carry

    cbw = lax.fori_loop(0, nc, bwd, h0_ref[0, 1:2, :])
    st_ref[0, 0:1, :] = cf
    st_ref[0, 1:2, :] = cbw


def _lru(u, h0, h0_block, lw, *, nb, n):
    T = min(n, 256)
    W = LRU_WIDTH
    return pl.pallas_call(
        functools.partial(_lru_kernel, N=n, T=T),
        grid=(nb,),
        in_specs=[
            pl.BlockSpec((n, 2 * W), lambda b: (b, 0)),
            pl.BlockSpec((1, 2, W), lambda b: (h0_block(b), 0, 0)),
            lw.spec("conv_w"), lw.spec("conv_b"), lw.spec("w_gate"), lw.spec("b_gate"),
            lw.spec("lru_lambda"),
        ],
        out_specs=[
            pl.BlockSpec((n, W), lambda b: (b, 0)),
            pl.BlockSpec((1, 2, W), lambda b: (b, 0, 0)),
        ],
        out_shape=[
            jax.ShapeDtypeStruct((nb * n, W), F32),
            jax.ShapeDtypeStruct((nb, 2, W), F32),
        ],
        scratch_shapes=[
            pltpu.VMEM((n + 2 * HALO, W), F32),
            pltpu.VMEM((W // 128, n, 128), F32),
            pltpu.VMEM((W // 128, n, 128), F32),
            pltpu.VMEM((W // 128, T, 128), F32),
            pltpu.VMEM((W // 128, T, 128), F32),
            pltpu.VMEM((W // 128, T, 128), F32),
        ],
        compiler_params=_params("arbitrary"),
        name="rglru",
    )(u, h0, lw["conv_w"], lw["conv_b"], lw["w_gate"], lw["b_gate"], lw["lru_lambda"])


def _pool_kernel(u_ref, wp_ref, sc_ref, y_ref, xpad, *, N, T):
    W = GROUP_WIDTH
    nc = N // T
    zeros = jnp.zeros((HALO, W), F32)
    xpad[0:HALO, :] = zeros
    xpad[N + HALO:N + 2 * HALO, :] = zeros

    def fill(j, carry):
        r0 = pl.multiple_of(j * T, T)
        xpad[pl.ds(r0 + HALO, T), :] = u_ref[pl.ds(r0, T), :]
        return carry

    lax.fori_loop(0, nc, fill, 0)

    grp = lax.broadcasted_iota(jnp.int32, (1, W), 1) // POOL_CH
    half = jnp.where(grp == 0, 1, jnp.where(grp == 1, 2, jnp.where(grp == 2, 4, 8)))
    scale = sc_ref[...]

    def chunk(j, carry):
        r0 = pl.multiple_of(j * T, T)
        ext = xpad[pl.ds(r0, T + 2 * HALO), :]
        w2 = _shift_rows(ext, -1) + ext
        w4 = _shift_rows(w2, -1) + _shift_rows(w2, 1)
        w8 = _shift_rows(w4, -2) + _shift_rows(w4, 2)
        w16 = _shift_rows(w8, -4) + _shift_rows(w8, 4)
        ws = jnp.where(grp == 0, w2, jnp.where(grp == 1, w4, jnp.where(grp == 2, w8, w16)))
        body = slice(HALO, HALO + T)
        t = r0 + lax.broadcasted_iota(jnp.int32, (T, W), 0)
        cnt = (jnp.minimum(t + half, N) - jnp.maximum(t - half, 0)).astype(F32)
        d = ws[body] / cnt - ext[body]
        y_ref[pl.ds(r0, T), :] = _dot(d.astype(BF16), wp_ref[...]) * scale
        return carry

    lax.fori_loop(0, nc, chunk, 0)


def _pool(u, lw, *, nb, n):
    W = GROUP_WIDTH
    T = min(n, 256)
    return pl.pallas_call(
        functools.partial(_pool_kernel, N=n, T=T),
        grid=(nb,),
        in_specs=[pl.BlockSpec((n, W), lambda b: (b, 0)), lw.spec("w_pool"), lw.spec("pool_scale")],
        out_specs=pl.BlockSpec((n, W), lambda b: (b, 0)),
        out_shape=jax.ShapeDtypeStruct((nb * n, W), F32),
        scratch_shapes=[pltpu.VMEM((n + 2 * HALO, W), F32)],
        compiler_params=_params("arbitrary"),
        name="pool_mixer",
    )(u, lw["w_pool"], lw["pool_scale"])


def _mix_ffn_kernel(*refs, final):
    if final:
        (x_ref, ya_ref, yb_ref, yc_ref, yd_ref, mod_ref, g2_ref, wo_ref, wg_ref, wu_ref, wd_ref,
         gf_ref, o_ref) = refs
    else:
        (x_ref, ya_ref, yb_ref, yc_ref, yd_ref, mod_ref, g2_ref, wo_ref, wg_ref, wu_ref, wd_ref,
         o_ref) = refs
    mod = mod_ref[0]
    gate1 = mod[:, 2 * D_MODEL:3 * D_MODEL]
    sh2 = mod[:, 3 * D_MODEL:4 * D_MODEL]
    sc2 = mod[:, 4 * D_MODEL:5 * D_MODEL]
    gate2 = mod[:, 5 * D_MODEL:6 * D_MODEL]
    mix = None
    for i, y_ref in enumerate((ya_ref, yb_ref, yc_ref, yd_ref)):
        part = _dot(y_ref[...].astype(BF16), wo_ref[i * GROUP_WIDTH:(i + 1) * GROUP_WIDTH, :])
        mix = part if mix is None else mix + part
    x1 = x_ref[...] + gate1 * mix
    h = _rms_rows(x1, D_MODEL) * g2_ref[...]
    hb = (h * (1.0 + sc2) + sh2).astype(BF16)
    ff = None
    for lo, hi in FF_CHUNKS:
        g = _dot(hb, wg_ref[:, lo:hi])
        up = _dot(hb, wu_ref[:, lo:hi])
        act = ((g * jax.nn.sigmoid(g)) * up).astype(BF16)
        part = _dot(act, wd_ref[lo:hi, :])
        ff = part if ff is None else ff + part
    x2 = x1 + gate2 * ff
    if final:
        x2 = _rms_rows(x2, D_MODEL) * gf_ref[...]
    o_ref[...] = x2


def _mix_ffn(x, ys, mod, lw, gf, *, nb, n, final):
    T = nb * n
    tm = TOKEN_TILE
    npt = n // tm

    def tok(width):
        return pl.BlockSpec((tm, width), lambda i: (i, 0))

    in_specs = [tok(D_MODEL), tok(256), tok(256), tok(256), tok(256), mod.spec(lambda i: i // npt),
                lw.spec("g2"), lw.spec("w_out"), lw.spec("w_gu", col_blocks=2, col_block=0),
                lw.spec("w_gu", col_blocks=2, col_block=1), lw.spec("w_down")]
    args = [x, *ys, mod.table, lw["g2"], lw["w_out"], lw["w_gu"], lw["w_gu"], lw["w_down"]]
    if final:
        in_specs.append(_resident((1, D_MODEL)))
        args.append(gf)
    return pl.pallas_call(
        functools.partial(_mix_ffn_kernel, final=final),
        grid=(T // tm,),
        in_specs=in_specs,
        out_specs=tok(D_MODEL),
        out_shape=jax.ShapeDtypeStruct((T, D_MODEL), F32),
        compiler_params=_params("arbitrary"),
        name="mix_ffn_final" if final else "mix_ffn",
    )(*args)


def _block_diag(w):
    L, G, c, e = w.shape
    return jnp.einsum('lgce,gh->lgche', w, jnp.eye(G, dtype=w.dtype)).reshape(L, G * c, G * e)


def _rot_cols(w):
    return jnp.concatenate([-w[..., 16:32], w[..., 0:16]], axis=-1)


def _stack_weights(p):
    w_in = p["w_in"]
    o1 = MLA_Q_RANK
    o2 = o1 + MLA_KV_RANK
    o3 = o2 + MLA_ROPE
    c_q, c_kv, k_r, rest = w_in[..., :o1], w_in[..., o1:o2], w_in[..., o2:o3], w_in[..., o3:]
    z = lambda n: jnp.zeros((DEPTH, D_MODEL, n), F32)
    w_in_eff = jnp.concatenate([c_q, k_r, z(32), c_kv, z(64), _rot_cols(k_r), z(32), rest], axis=-1)

    w_uq = p["mla_w_uq"]
    qd = MLA_NOPE + MLA_ROPE
    wq_parts, wqr_parts = [], []
    zq = lambda n: jnp.zeros((DEPTH, MLA_Q_RANK, n), F32)
    for h in range(MLA_HEADS):
        wh = w_uq[..., h * qd:(h + 1) * qd]
        wq_parts += [wh, zq(MLA_SLOT - qd)]
        wqr_parts += [zq(MLA_NOPE), _rot_cols(wh[..., MLA_NOPE:]), zq(MLA_SLOT - qd)]
    pad_rows = lambda w: jnp.pad(w, ((0, 0), (0, 256 - MLA_Q_RANK), (0, 0)))
    w_ukv = p["mla_w_ukv"]
    wk_parts, wv_parts = [], []
    zk = jnp.zeros((DEPTH, MLA_KV_RANK, MLA_SLOT - MLA_NOPE), F32)
    for h in range(MLA_HEADS):
        base = h * (MLA_NOPE + MLA_V)
        wk_parts += [w_ukv[..., base:base + MLA_NOPE], zk]
        wv_parts.append(w_ukv[..., base + MLA_NOPE:base + MLA_NOPE + MLA_V])

    w_r, w_i, b_r, b_i = p["lru_w_r"], p["lru_w_i"], p["lru_b_r"], p["lru_b_i"]
    w_gate = jnp.concatenate([_block_diag(w_r[:, 0]), _block_diag(w_r[:, 1]),
                              _block_diag(w_i[:, 0]), _block_diag(w_i[:, 1])], axis=-1)
    b_gate = jnp.concatenate([b_r[:, 0], b_r[:, 1], b_i[:, 0], b_i[:, 1]], axis=-1)
    row = lambda v: v[:, None, :]
    return {
        "g1": row(p["norm1_g"]),
        "g2": row(p["norm2_g"]),
        "w_in": w_in_eff.astype(BF16),
        "gq": row(jnp.pad(p["mla_q_norm_g"], ((0, 0), (0, 256 - MLA_Q_RANK)))),
        "gkv": row(p["mla_kv_norm_g"]),
        "wq": pad_rows(jnp.concatenate(wq_parts, axis=-1)).astype(BF16),
        "wqr": pad_rows(jnp.concatenate(wqr_parts, axis=-1)).astype(BF16),
        "wkv": jnp.concatenate(wk_parts + wv_parts, axis=-1).astype(BF16),
        "conv_w": p["lru_conv_w"],
        "conv_b": row(p["lru_conv_b"]),
        "w_gate": w_gate.astype(BF16),
        "b_gate": row(b_gate),
        "lru_lambda": p["lru_lambda"],
        "w_pool": _block_diag(p["pool_w"]).astype(BF16),
        "pool_scale": row(p["pool_scale"]),
        "diff_lambda": p["diff_lambda"],
        "diff_g": row(jnp.tile(p["diff_norm_g"], (1, DIFF_HEADS))),
        "w_out": p["w_out"].astype(BF16),
        "w_gu": p["w_gu"].astype(BF16),
        "w_down": p["w_down"].astype(BF16),
    }


def _rope_tables(n, positional):
    quarter = MLA_ROPE // 4
    if positional:
        t = jnp.arange(n)
        row = (t // GRID_W).astype(F32)
        col = (t % GRID_W).astype(F32)
        inv = ROPE_BASE ** (-jnp.arange(quarter, dtype=F32) / quarter)
        ang = jnp.concatenate([row[:, None] * inv, col[:, None] * inv], axis=-1)
        cos, sin = jnp.cos(ang), jnp.sin(ang)
    else:
        cos, sin = jnp.ones((n, 16), F32), jnp.zeros((n, 16), F32)
    scale = LOG2E / math.sqrt(MLA_NOPE + MLA_ROPE)
    place = np.zeros((32, TAB_WIDTH), np.float32)
    offset = np.zeros((1, TAB_WIDTH), np.float32)
    offset[0, 0:64] = offset[0, 96:128] = scale
    for i in range(16):
        for half in (64, 80):
            place[i, half + i] = scale
            place[16 + i, 128 + half + i] = scale
            place[i, 256 + half + i] = 1.0
            place[16 + i, 384 + half + i] = 1.0
        for grp in range(8):
            place[i, 512 + 32 * grp + i] = place[i, 512 + 32 * grp + 16 + i] = 1.0
            place[16 + i, 768 + 32 * grp + i] = -1.0
            place[16 + i, 1024 + 32 * grp + 16 + i] = 1.0
    return jnp.dot(jnp.concatenate([cos, sin], axis=1), place, precision=lax.Precision.HIGHEST) + offset


def _layer(x, mod, lw, tabs, layer_idx, ctx, gf, *, nb, n, final):
    emit_cache = ctx is None
    tok_nb, tok_n = (1, nb * n) if mod.shared else (nb, n)
    outs = _inproj(x, mod, lw, tabs, nb=tok_nb, n=tok_n, emit_cache=emit_cache)
    q, k, vt, u_lru, u_pool, dq, dk, dvt = outs[:8]
    lam_init = 0.8 - 0.6 * math.exp(-0.3 * layer_idx)
    if ctx is None:
        h0 = jnp.zeros((1, 2, LRU_WIDTH), F32)
        h0_block = lambda b: 0
        mla_ctx = diff_ctx = None
    else:
        ckv, kr_pad, cdk, cdv, h0 = ctx
        p = ckv.shape[0] // (nb * DEPTH)
        h0_block = lambda b: b * DEPTH + layer_idx
        kc, vtc, dkc, dvtc = _ctx_prep(ckv, kr_pad, cdk, cdv, lw, nb=nb, p=p)
        mla_ctx = (kc, vtc)
        diff_ctx = (dkc, dvtc)
    y_mla = _mla_attn(q, k, vt, mla_ctx, nb=nb, n=n)
    y_lru, st = _lru(u_lru, h0, h0_block, lw, nb=nb, n=n)
    y_pool = _pool(u_pool, lw, nb=nb, n=n)
    y_diff = _diff_attn(dq, dk, dvt, diff_ctx, lw, nb=nb, n=n, lam_init=lam_init)
    x2 = _mix_ffn(x, (y_mla, y_lru, y_pool, y_diff), mod, lw, gf, nb=tok_nb, n=tok_n, final=final)
    cache = (outs[8], outs[9][:, 64:96], outs[10], outs[11], st) if emit_cache else None
    return x2, cache


def kernel(x_prompt, x_sample, cache_mla_ckv, cache_mla_krope, cache_diff_k, cache_diff_v, state_lru,
           c, c_ctx, w_ada, b_ada, norm1_g, norm2_g, w_in, mla_q_norm_g, mla_w_uq, mla_kv_norm_g,
           mla_w_ukv, lru_conv_w, lru_conv_b, lru_w_r, lru_b_r, lru_w_i, lru_b_i, lru_lambda, pool_w,
           pool_scale, diff_lambda, diff_norm_g, w_out, w_gu, w_down, final_norm_g):
    p = {
        "norm1_g": norm1_g, "norm2_g": norm2_g, "w_in": w_in, "mla_q_norm_g": mla_q_norm_g,
        "mla_w_uq": mla_w_uq, "mla_kv_norm_g": mla_kv_norm_g, "mla_w_ukv": mla_w_ukv,
        "lru_conv_w": lru_conv_w, "lru_conv_b": lru_conv_b, "lru_w_r": lru_w_r, "lru_b_r": lru_b_r,
        "lru_w_i": lru_w_i, "lru_b_i": lru_b_i, "lru_lambda": lru_lambda, "pool_w": pool_w,
        "pool_scale": pool_scale, "diff_lambda": diff_lambda, "diff_norm_g": diff_norm_g,
        "w_out": w_out, "w_gu": w_gu, "w_down": w_down,
    }
    Bp, Np, _ = x_prompt.shape
    Bs, Ns, _ = x_sample.shape
    P = cache_mla_ckv.shape[2]

    cond_all = jnp.concatenate([c, c_ctx[None, :], jnp.zeros((MOD_ROWS - Bs - 1, D_MODEL), F32)], axis=0)
    mod_table = _ada(cond_all, w_ada, b_ada).reshape(DEPTH * MOD_ROWS, 1, 6 * D_MODEL)
    tabs_p = _rope_tables(Bp * Np, positional=False)
    tabs_s = _rope_tables(Ns, positional=True)
    kr_pad = jnp.pad(cache_mla_krope, ((0, 0), (0, 0), (0, 0), (MLA_NOPE, MLA_SLOT - MLA_NOPE - MLA_ROPE)))
    flat = lambda a, w: a.reshape(Bs * DEPTH * P, w)
    ctx = (flat(cache_mla_ckv, MLA_KV_RANK), flat(kr_pad, MLA_SLOT), flat(cache_diff_k, 256),
           flat(cache_diff_v, 256), state_lru.reshape(Bs * DEPTH, 2, LRU_WIDTH))
    gf = final_norm_g[None, :]
    stacked = _stack_weights(p)

    xp = x_prompt.reshape(Bp * Np, D_MODEL)
    xs = x_sample.reshape(Bs * Ns, D_MODEL)
    caches = []
    for l in range(DEPTH):
        lw = _LayerWeights(stacked, l)
        final = l == DEPTH - 1
        mod_p = _Mod(mod_table, l * MOD_ROWS + Bs, shared=True)
        mod_s = _Mod(mod_table, l * MOD_ROWS, shared=False)
        xp, cache = _layer(xp, mod_p, lw, tabs_p, l, None, gf, nb=Bp, n=Np, final=final)
        caches.append(cache)
        xs, _ = _layer(xs, mod_s, lw, tabs_s, l, ctx, gf, nb=Bs, n=Ns, final=final)

    stack = lambda i, w: jnp.stack([cc[i].reshape(Bp, Np, w) for cc in caches], axis=1)
    new_mla_ckv = stack(0, MLA_KV_RANK)
    new_mla_krope = stack(1, MLA_ROPE)
    new_diff_k = stack(2, 256).reshape(Bp, DEPTH, Np, DIFF_HEADS, 2, DIFF_DIM)
    new_diff_v = stack(3, 256).reshape(Bp, DEPTH, Np, DIFF_HEADS, 2 * DIFF_DIM)
    new_state_lru = jnp.stack([cc[4] for cc in caches], axis=1)
    return (xp.reshape(Bp, Np, D_MODEL), xs.reshape(Bs, Ns, D_MODEL),
            new_mla_ckv, new_mla_krope, new_diff_k, new_diff_v, new_state_lru)
```

```python
import functools
import math

import jax
import jax.numpy as jnp
import numpy as np
from jax import lax
from jax.experimental import pallas as pl
from jax.experimental.pallas import tpu as pltpu

F32 = jnp.float32
BF16 = jnp.bfloat16

D_MODEL = 1024
DEPTH = 2
GRID_W = 64
GROUP_WIDTH = 256
MLA_HEADS = 4
MLA_NOPE = 64
MLA_ROPE = 32
MLA_V = 64
MLA_Q_RANK = 192
MLA_KV_RANK = 128
MLA_SLOT = 128
LRU_WIDTH = 256
LRU_C = 8.0
POOL_WINDOWS = (2, 4, 8, 16)
POOL_CH = 64
DIFF_HEADS = 4
DIFF_DIM = 32
HEAD_V = 64
FF_HIDDEN = 2816
FF_CHUNKS = ((0, 1536), (1536, 2816))
ROPE_BASE = 10000.0
EPS = 1e-6
IN_EFF = 2048
HALO = 8
SCAN_RUN = 4
VT_ROWS = 80
ATT_TQ = 256
TOKEN_TILE = 512
TAB_WIDTH = 4 * 128 + 3 * 256
MOD_ROWS = 16
LOG2E = math.log2(math.e)

VMEM_LIMIT_BYTES = 56 * 1024 * 1024

_NT = (((1,), (1,)), ((), ()))


def _params(*sem):
    return pltpu.CompilerParams(dimension_semantics=sem, vmem_limit_bytes=VMEM_LIMIT_BYTES)


def _resident(shape):
    zeros = (0,) * len(shape)
    return pl.BlockSpec(shape, lambda *_: zeros, pipeline_mode=pl.Buffered(1))


def _dot(a, b):
    return jnp.dot(a, b, preferred_element_type=F32)


def _dot_nt(a, b):
    return lax.dot_general(a, b, _NT, preferred_element_type=F32)


def _rms_rows(x, width):
    ms = jnp.sum(x * x, axis=-1, keepdims=True) * (1.0 / width)
    return x * lax.rsqrt(ms + EPS)


def _store_vt(vt_ref, v):
    vt = v.T
    rows = v.shape[0]
    pad = VT_ROWS - HEAD_V
    ones_row = jnp.where(lax.broadcasted_iota(jnp.int32, (pad, rows), 0) == 0, 1.0, 0.0).astype(BF16)
    for hh in range(vt_ref.shape[0]):
        vt_ref[hh, 0:HEAD_V, :] = vt[hh * HEAD_V:(hh + 1) * HEAD_V, :].astype(BF16)
        vt_ref[hh, HEAD_V:VT_ROWS, :] = ones_row


class _Mod:
    def __init__(self, table, row0, shared):
        self.table, self.row0, self.shared = table, row0, shared

    def spec(self, batch_of):
        row0 = self.row0
        if self.shared:
            return pl.BlockSpec((1, 1, 6 * D_MODEL), lambda *g: (row0, 0, 0))
        return pl.BlockSpec((1, 1, 6 * D_MODEL), lambda *g: (row0 + batch_of(*g), 0, 0))


class _LayerWeights:
    def __init__(self, stacked, layer):
        self.stacked, self.layer = stacked, layer

    def __getitem__(self, name):
        return self.stacked[name]

    def spec(self, name, col_blocks=1, col_block=0):
        layer = self.layer
        _, rows, cols = self.stacked[name].shape
        return pl.BlockSpec((None, rows, cols // col_blocks), lambda *_: (layer, 0, col_block),
                            pipeline_mode=pl.Buffered(1))


def _ada_kernel(cond_ref, w_ref, b_ref, out_ref):
    c = cond_ref[...]
    s = c * jax.nn.sigmoid(c)
    out_ref[0] = _dot(s.astype(BF16), w_ref[0].astype(BF16)) + b_ref[0]


def _ada(cond_all, w_ada, b_ada):
    rows = cond_all.shape[0]
    tn = 1536
    return pl.pallas_call(
        _ada_kernel,
        grid=(DEPTH, 6 * D_MODEL // tn),
        in_specs=[
            pl.BlockSpec((rows, D_MODEL), lambda l, j: (0, 0)),
            pl.BlockSpec((1, D_MODEL, tn), lambda l, j: (l, 0, j)),
            pl.BlockSpec((1, 1, tn), lambda l, j: (l, 0, j)),
        ],
        out_specs=pl.BlockSpec((1, rows, tn), lambda l, j: (l, 0, j)),
        out_shape=jax.ShapeDtypeStruct((DEPTH, rows, 6 * D_MODEL), F32),
        compiler_params=_params("arbitrary", "arbitrary"),
        name="ada_mod",
    )(cond_all, w_ada, b_ada.reshape(DEPTH, 1, 6 * D_MODEL))


def _inproj_kernel(x_ref, mod_ref, g1_ref, win_ref, gq_ref, gkv_ref, wq_ref, wqr_ref, wkv_ref, tab_ref,
                   q_out, k_out, vt_out, lru_out, pool_out, dq_out, dk_out, dvt_out, *cache_outs):
    cosq_ref, sinq_ref, cosk_ref, sink_ref = (tab_ref.at[:, i * 128:(i + 1) * 128] for i in range(4))
    cosd_ref, sina_ref, sinb_ref = (tab_ref.at[:, 512 + i * 256:768 + i * 256] for i in range(3))
    x = x_ref[...]
    mod = mod_ref[0]
    sh1 = mod[:, 0:D_MODEL]
    sc1 = mod[:, D_MODEL:2 * D_MODEL]
    h = _rms_rows(x, D_MODEL) * g1_ref[...]
    hb = (h * (1.0 + sc1) + sh1).astype(BF16)

    u_mla = _dot(hb, win_ref[:, 0:512])
    u_pd = _dot(hb, win_ref[:, 1024:1536])
    u_kv = _dot(hb, win_ref[:, 1536:2048])
    t01 = u_mla[:, 0:256]
    lane = lax.broadcasted_iota(jnp.int32, (1, 256), 1)
    cq = jnp.where(lane < MLA_Q_RANK, t01, 0.0)
    cqn = (_rms_rows(cq, MLA_Q_RANK) * gq_ref[...]).astype(BF16)
    qa = _dot(cqn, wq_ref[...])
    qr = _dot(cqn, wqr_ref[...])
    cosq = cosq_ref[...]
    sinq = sinq_ref[...]
    ckv = u_mla[:, 256:384]
    lat = _rms_rows(ckv, MLA_KV_RANK) * gkv_ref[...]
    latb = lat.astype(BF16)
    kkv = _dot(latb, wkv_ref[...])
    kk = kkv[:, 0:MLA_HEADS * MLA_SLOT]
    _store_vt(vt_out, kkv[:, MLA_HEADS * MLA_SLOT:])
    t1 = t01[:, 128:256]
    t3 = u_mla[:, 384:512]
    kro = t1 * cosk_ref[...] + t3 * sink_ref[...]
    for hh in range(MLA_HEADS):
        sl = slice(hh * MLA_SLOT, (hh + 1) * MLA_SLOT)
        q_out[hh] = (qa[:, sl] * cosq + qr[:, sl] * sinq).astype(q_out.dtype)
        k_out[hh] = (kk[:, sl] + kro).astype(k_out.dtype)

    lru_out[...] = _dot(hb, win_ref[:, 512:1024])
    pool_out[...] = u_pd[:, 0:256]

    cosd = cosd_ref[...]
    sina = sina_ref[...]
    sinb = sinb_ref[...]

    def rope(t):
        return t * cosd + pltpu.roll(t, 256 - 16, 1) * sina + pltpu.roll(t, 16, 1) * sinb

    dq = u_pd[:, 256:512]
    dk = u_kv[:, 0:256]
    dv = u_kv[:, 256:512]
    dq_out[...] = (rope(dq) * (LOG2E / math.sqrt(DIFF_DIM))).astype(dq_out.dtype)
    dk_out[...] = rope(dk).astype(dk_out.dtype)
    _store_vt(dvt_out, dv)

    if cache_outs:
        lat_out, kr_out, dk_raw_out, dv_raw_out = cache_outs
        lat_out[...] = lat
        kr_out[...] = t1
        dk_raw_out[...] = dk
        dv_raw_out[...] = dv


def _inproj(x, mod, lw, tabs, *, nb, n, emit_cache):
    T = nb * n
    tm = TOKEN_TILE
    npt = n // tm
    row_blk = lambda j, b: b * npt + j

    def tok(width):
        return pl.BlockSpec((tm, width), lambda j, b: (row_blk(j, b), 0))

    def tab(width):
        return pl.BlockSpec((tm, width), lambda j, b: (j, 0))

    head = pl.BlockSpec((MLA_HEADS, tm, MLA_SLOT), lambda j, b: (0, row_blk(j, b), 0))
    vt_spec = pl.BlockSpec((MLA_HEADS, VT_ROWS, tm), lambda j, b: (0, 0, row_blk(j, b)))
    wnames = ("g1", "w_in", "gq", "gkv", "wq", "wqr", "wkv")
    in_specs = [tok(D_MODEL), mod.spec(lambda j, b: b)] + [lw.spec(nm) for nm in wnames] + [
        tab(TAB_WIDTH)]
    out_specs = [head, head, vt_spec, tok(512), tok(256), tok(256), tok(256), vt_spec]
    vt_shape = jax.ShapeDtypeStruct((MLA_HEADS, VT_ROWS, T), BF16)
    out_shape = [
        jax.ShapeDtypeStruct((MLA_HEADS, T, MLA_SLOT), BF16),
        jax.ShapeDtypeStruct((MLA_HEADS, T, MLA_SLOT), BF16),
        vt_shape,
        jax.ShapeDtypeStruct((T, 512), F32),
        jax.ShapeDtypeStruct((T, 256), F32),
        jax.ShapeDtypeStruct((T, 256), BF16),
        jax.ShapeDtypeStruct((T, 256), BF16),
        vt_shape,
    ]
    if emit_cache:
        out_specs += [tok(128), tok(128), tok(256), tok(256)]
        out_shape += [jax.ShapeDtypeStruct((T, 128), F32), jax.ShapeDtypeStruct((T, 128), F32),
                      jax.ShapeDtypeStruct((T, 256), F32), jax.ShapeDtypeStruct((T, 256), F32)]
    return pl.pallas_call(
        _inproj_kernel,
        grid=(npt, nb),
        in_specs=in_specs,
        out_specs=out_specs,
        out_shape=out_shape,
        compiler_params=_params("arbitrary", "arbitrary"),
        name="inproj_cache" if emit_cache else "inproj",
    )(x, mod.table, *[lw[nm] for nm in wnames], tabs)


def _ctx_prep_kernel(ckv_ref, kr_ref, dk_ref, dv_ref, wkv_ref, k_out, vt_out, dk_out, dvt_out):
    latb = ckv_ref[...].astype(BF16)
    kkv = _dot(latb, wkv_ref[...])
    kk = kkv[:, 0:MLA_HEADS * MLA_SLOT]
    kr = kr_ref[...]
    for hh in range(MLA_HEADS):
        k_out[hh] = (kk[:, hh * MLA_SLOT:(hh + 1) * MLA_SLOT] + kr).astype(k_out.dtype)
    _store_vt(vt_out, kkv[:, MLA_HEADS * MLA_SLOT:])
    dk_out[...] = dk_ref[...].astype(dk_out.dtype)
    _store_vt(dvt_out, dv_ref[...])


def _ctx_prep(ckv, kr_pad, cdk, cdv, lw, *, nb, p):
    T = nb * p
    layer = lw.layer
    cache_row = lambda w: pl.BlockSpec((p, w), lambda b: (b * DEPTH + layer, 0))
    row = lambda w: pl.BlockSpec((p, w), lambda b: (b, 0))
    vt_spec = pl.BlockSpec((MLA_HEADS, VT_ROWS, p), lambda b: (0, 0, b))
    vt_shape = jax.ShapeDtypeStruct((MLA_HEADS, VT_ROWS, T), BF16)
    return pl.pallas_call(
        _ctx_prep_kernel,
        grid=(nb,),
        in_specs=[cache_row(128), cache_row(128), cache_row(256), cache_row(256),
                  lw.spec("wkv")],
        out_specs=[pl.BlockSpec((MLA_HEADS, p, MLA_SLOT), lambda b: (0, b, 0)), vt_spec, row(256), vt_spec],
        out_shape=[jax.ShapeDtypeStruct((MLA_HEADS, T, MLA_SLOT), BF16), vt_shape,
                   jax.ShapeDtypeStruct((T, 256), BF16), vt_shape],
        compiler_params=_params("arbitrary"),
        name="ctx_prep",
    )(ckv, kr_pad, cdk, cdv, lw["wkv"])


SAFE_DENOM = 2.0 ** -60
E_BUFS = 4
BOUND_SLACK = 1.02


def _scores(k_new, k_ctx, q):
    sn = _dot_nt(k_new(), q)
    sc = _dot_nt(k_ctx(), q) if k_ctx is not None else None
    return sn, sc


def _exact_shift(k_new, k_ctx, q):
    sn, sc = _scores(k_new, k_ctx, q)
    m = jnp.max(sn, axis=0, keepdims=True)
    if sc is not None:
        m = jnp.maximum(m, jnp.max(sc, axis=0, keepdims=True))
    return m


def _bound_shift(q, key_norm2):
    qf = q.astype(F32)
    ones = jnp.ones((8, q.shape[1]), BF16)
    q_norm2 = _dot_nt(ones, (qf * qf).astype(BF16))[0:1, :]
    return jnp.sqrt(q_norm2 * key_norm2) * BOUND_SLACK


def _max_row_norm2(k_new, k_ctx, col_sum):
    def one(k):
        kf = k.astype(F32)
        return jnp.max(_dot((kf * kf).astype(BF16), col_sum), axis=0, keepdims=True)
    m = one(k_new)
    if k_ctx is not None:
        m = jnp.maximum(m, one(k_ctx))
    return m * BOUND_SLACK


def _exp_stage(e_buf, k_new, k_ctx, q, shift, n_ctx):
    sn, sc = _scores(k_new, k_ctx, q)
    e_buf[n_ctx:, :] = jnp.exp2(sn - shift).astype(BF16)
    if sc is not None:
        e_buf[0:n_ctx, :] = jnp.exp2(sc - shift).astype(BF16)


def _value_stage(e_buf, vt_new, vt_ctx, n_ctx):
    o = _dot(vt_new(), e_buf[n_ctx:, :])
    if vt_ctx is not None:
        o = o + _dot(vt_ctx(), e_buf[0:n_ctx, :])
    return o


def _run_pipeline(n_maps, exp_stage, value_stage):
    ahead = E_BUFS - 1
    for u in range(min(ahead, n_maps)):
        exp_stage(u)
    for u in range(n_maps):
        if u + ahead < n_maps:
            exp_stage(u + ahead)
        value_stage(u)


def _att_scratch(nk, key_shape):
    scratch = [pltpu.VMEM((8, 128), F32),
               pltpu.VMEM((MLA_HEADS * HEAD_V, ATT_TQ), F32)]
    if key_shape is not None:
        scratch += [pltpu.VMEM(key_shape, BF16), pltpu.VMEM((MLA_HEADS, VT_ROWS, nk), BF16)]
    return scratch + [pltpu.VMEM((nk, ATT_TQ), BF16)] * E_BUFS


def _att_nsub(n):
    return 2 if n % (2 * ATT_TQ) == 0 else 1


def _mla_attn_kernel(*refs, has_ctx, nsub):
    if has_ctx:
        q_ref, k_ref, vt_ref, kc_ref, vtc_ref, o_ref, kn2, ot, keys, vals, *e_bufs = refs
        n_ctx = kc_ref.shape[1]
    else:
        q_ref, k_ref, vt_ref, o_ref, kn2, ot, *e_bufs = refs
        keys, vals = k_ref, vt_ref

    @pl.when(pl.program_id(1) == 0)
    def _():
        ones = jnp.ones((MLA_SLOT, 128), BF16)
        for hh in range(MLA_HEADS):
            kn2[hh:hh + 1, :] = _max_row_norm2(k_ref[hh], kc_ref[hh] if has_ctx else None, ones)
        if has_ctx:
            keys[:, 0:n_ctx, :] = kc_ref[...]
            keys[:, n_ctx:, :] = k_ref[...]
            vals[:, :, 0:n_ctx] = vtc_ref[...]
            vals[:, :, n_ctx:] = vt_ref[...]

    def run(exact):
        denoms = []

        def exp_stage(u):
            t, hh = divmod(u, MLA_HEADS)
            q = q_ref[hh, t * ATT_TQ:(t + 1) * ATT_TQ, :]
            k_all = lambda: keys[hh]
            shift = _exact_shift(k_all, None, q) if exact else _bound_shift(q, kn2[hh:hh + 1, 0:1])
            _exp_stage(e_bufs[u % E_BUFS], k_all, None, q, shift, 0)

        def value_stage(u):
            t, hh = divmod(u, MLA_HEADS)
            o = _value_stage(e_bufs[u % E_BUFS], lambda: vals[hh], None, 0)
            denom = o[HEAD_V:HEAD_V + 1, :]
            denoms.append(denom)
            ot[hh * HEAD_V:(hh + 1) * HEAD_V, :] = o[0:HEAD_V, :] * (1.0 / denom)
            if hh == MLA_HEADS - 1:
                o_ref[t * ATT_TQ:(t + 1) * ATT_TQ, :] = ot[...].T

        _run_pipeline(nsub * MLA_HEADS, exp_stage, value_stage)
        return jnp.min(functools.reduce(jnp.minimum, denoms))

    denom_min = run(exact=False)

    @pl.when(jnp.logical_not(denom_min >= SAFE_DENOM))
    def _():
        run(exact=True)


def _mla_attn(q, k, vt, ctx, *, nb, n):
    nsub = _att_nsub(n)
    tq = nsub * ATT_TQ
    npt = n // tq
    H, S = MLA_HEADS, MLA_SLOT
    in_specs = [
        pl.BlockSpec((H, tq, S), lambda b, j: (0, b * npt + j, 0)),
        pl.BlockSpec((H, n, S), lambda b, j: (0, b, 0)),
        pl.BlockSpec((H, VT_ROWS, n), lambda b, j: (0, 0, b)),
    ]
    args = [q, k, vt]
    n_ctx = 0
    if ctx is not None:
        n_ctx = ctx[0].shape[1] // nb
        in_specs += [
            pl.BlockSpec((H, n_ctx, S), lambda b, j: (0, b, 0)),
            pl.BlockSpec((H, VT_ROWS, n_ctx), lambda b, j: (0, 0, b)),
        ]
        args += list(ctx)
    return pl.pallas_call(
        functools.partial(_mla_attn_kernel, has_ctx=ctx is not None, nsub=nsub),
        grid=(nb, npt),
        in_specs=in_specs,
        out_specs=pl.BlockSpec((tq, 256), lambda b, j: (b * npt + j, 0)),
        out_shape=jax.ShapeDtypeStruct((nb * n, 256), F32),
        scratch_shapes=_att_scratch(n + n_ctx, (H, n + n_ctx, S) if ctx is not None else None),
        compiler_params=_params("arbitrary", "arbitrary"),
        name="mla_attn_ctx" if ctx is not None else "mla_attn",
    )(*args)


def _diff_attn_kernel(*refs, has_ctx, nsub, lam_init):
    if has_ctx:
        lv_ref, g_ref, q_ref, k_ref, vt_ref, kc_ref, vtc_ref, o_ref, kn2, ot, keys, vals, *e_bufs = refs
        n_ctx = kc_ref.shape[0]
    else:
        lv_ref, g_ref, q_ref, k_ref, vt_ref, o_ref, kn2, ot, *e_bufs = refs
        keys, vals = k_ref, vt_ref
    lv = lv_ref[...]
    lam = (jnp.exp(jnp.sum(lv[0:1] * lv[1:2], axis=-1, keepdims=True))
           - jnp.exp(jnp.sum(lv[2:3] * lv[3:4], axis=-1, keepdims=True)) + lam_init)
    lane128 = lax.broadcasted_iota(jnp.int32, (1, 128), 1)
    n_pairs = 2 * DIFF_HEADS

    @pl.when(pl.program_id(1) == 0)
    def _():
        dim = lax.broadcasted_iota(jnp.int32, (256, 128), 0)
        col = lax.broadcasted_iota(jnp.int32, (256, 128), 1)
        indicator = jnp.where(dim // DIFF_DIM == col, 1.0, 0.0).astype(BF16)
        kn2[0:1, :] = _max_row_norm2(k_ref[...], kc_ref[...] if has_ctx else None, indicator)
        if has_ctx:
            keys[0:n_ctx, :] = kc_ref[...]
            keys[n_ctx:, :] = k_ref[...]
            vals[:, :, 0:n_ctx] = vtc_ref[...]
            vals[:, :, n_ctx:] = vt_ref[...]

    def run(exact):
        denoms = []
        outs = {}

        def exp_stage(u):
            t, p = divmod(u, n_pairs)
            tile = slice((p * DIFF_DIM // 128) * 128, (p * DIFF_DIM // 128 + 1) * 128)
            k_new = lambda: keys[:, tile]
            k_ctx = None
            q = q_ref[t * ATT_TQ:(t + 1) * ATT_TQ, tile]
            lo = p * DIFF_DIM - tile.start
            in_pair = (lane128 >= lo) & (lane128 < lo + DIFF_DIM)
            qm = jnp.where(in_pair, q, jnp.zeros_like(q))
            shift = _exact_shift(k_new, k_ctx, qm) if exact else _bound_shift(qm, kn2[0:1, p:p + 1])
            _exp_stage(e_bufs[u % E_BUFS], k_new, k_ctx, qm, shift, 0)

        def value_stage(u):
            t, p = divmod(u, n_pairs)
            hh = p // 2
            o = _value_stage(e_bufs[u % E_BUFS], lambda: vals[hh], None, 0)
            denom = o[HEAD_V:HEAD_V + 1, :]
            denoms.append(denom)
            outs[u] = (o[0:HEAD_V, :], denom)
            if p % 2 == 1:
                (o0, l0), (o1, l1) = outs.pop(u - 1), outs.pop(u)
                o = o0 * (1.0 / l0) - o1 * (lam / l1)
                msq = jnp.sum(o * o, axis=0, keepdims=True) * (1.0 / HEAD_V)
                ot[hh * HEAD_V:(hh + 1) * HEAD_V, :] = o * lax.rsqrt(msq + EPS)
            if p == n_pairs - 1:
                o_ref[t * ATT_TQ:(t + 1) * ATT_TQ, :] = (ot[...].T * g_ref[...]) * (1.0 - lam_init)

        _run_pipeline(nsub * n_pairs, exp_stage, value_stage)
        return jnp.min(functools.reduce(jnp.minimum, denoms))

    denom_min = run(exact=False)

    @pl.when(jnp.logical_not(denom_min >= SAFE_DENOM))
    def _():
        run(exact=True)


def _diff_attn(q, k, vt, ctx, lw, *, nb, n, lam_init):
    nsub = 1
    tq = nsub * ATT_TQ
    npt = n // tq
    in_specs = [
        lw.spec("diff_lambda"),
        lw.spec("diff_g"),
        pl.BlockSpec((tq, 256), lambda b, j: (b * npt + j, 0)),
        pl.BlockSpec((n, 256), lambda b, j: (b, 0)),
        pl.BlockSpec((DIFF_HEADS, VT_ROWS, n), lambda b, j: (0, 0, b)),
    ]
    args = [lw["diff_lambda"], lw["diff_g"], q, k, vt]
    n_ctx = 0
    if ctx is not None:
        n_ctx = ctx[0].shape[0] // nb
        in_specs += [pl.BlockSpec((n_ctx, 256), lambda b, j: (b, 0)),
                     pl.BlockSpec((DIFF_HEADS, VT_ROWS, n_ctx), lambda b, j: (0, 0, b))]
        args += list(ctx)
    return pl.pallas_call(
        functools.partial(_diff_attn_kernel, has_ctx=ctx is not None, nsub=nsub, lam_init=lam_init),
        grid=(nb, npt),
        in_specs=in_specs,
        out_specs=pl.BlockSpec((tq, 256), lambda b, j: (b * npt + j, 0)),
        out_shape=jax.ShapeDtypeStruct((nb * n, 256), F32),
        scratch_shapes=_att_scratch(n + n_ctx, (n + n_ctx, 256) if ctx is not None else None),
        compiler_params=_params("arbitrary", "arbitrary"),
        name="diff_attn_ctx" if ctx is not None else "diff_attn",
    )(*args)


def _shift_rows(v, k):
    return pltpu.roll(v, (-k) % v.shape[0], 0)


def _scan_strided(a_ref, b_ref, h_ref, row0, carry, n_rows, reverse):
    sub = lax.broadcasted_iota(jnp.int32, (8, 128), 0)
    span = 8 * SCAN_RUN
    order = tuple(range(SCAN_RUN))[::-1] if reverse else tuple(range(SCAN_RUN))
    starts = tuple(range(0, n_rows, span))[::-1] if reverse else tuple(range(0, n_rows, span))
    carries = []
    for lt in range(a_ref.shape[0]):
        c_in = carry[:, lt * 128:(lt + 1) * 128]
        for start in starts:
            tile = lambda ref, g: ref[lt, pl.ds(row0 + start + g, 8, stride=SCAN_RUN), :]
            a = [tile(a_ref, g) for g in range(SCAN_RUN)]
            b = [tile(b_ref, g) for g in range(SCAN_RUN)]
            h = {order[0]: b[order[0]]}
            p = {order[0]: a[order[0]]}
            for prev, g in zip(order, order[1:]):
                h[g] = a[g] * h[prev] + b[g]
                p[g] = a[g] * p[prev]
            pi, hi = p[order[-1]], h[order[-1]]
            for s in (1, 2, 4):
                shift = 8 - s if reverse else s
                valid = (sub < 8 - s) if reverse else (sub >= s)
                pr, hr = pltpu.roll(pi, shift, 0), pltpu.roll(hi, shift, 0)
                hi = jnp.where(valid, pi * hr + hi, hi)
                pi = jnp.where(valid, pi * pr, pi)
            one = 7 if reverse else 1
            first = (sub == 7) if reverse else (sub == 0)
            pe = jnp.where(first, 1.0, pltpu.roll(pi, one, 0))
            he = jnp.where(first, 0.0, pltpu.roll(hi, one, 0))
            c = pe * c_in + he
            for g in range(SCAN_RUN):
                h_ref[lt, pl.ds(start + g, 8, stride=SCAN_RUN), :] = h[g] + p[g] * c
            last = 0 if reverse else 7
            c_in = pi[last:last + 1, :] * c_in + hi[last:last + 1, :]
        carries.append(c_in)
    return jnp.concatenate(carries, axis=1)


def _sigmoid(x):
    return 0.5 * jnp.tanh(0.5 * x) + 0.5


def _gelu_tanh(x):
    return x * (0.5 * (1.0 + jnp.tanh(math.sqrt(2.0 / math.pi) * (x + 0.044715 * (x * x * x)))))


def _lru_kernel(u_ref, h0_ref, cw_ref, cb_ref, wg_ref, bg_ref, lam_ref, y_ref, st_ref,
                xpad, a1s, b1s, a0c, b0c, hc, *, N, T):
    W = LRU_WIDTH
    nc = N // T
    tiles = [slice(lt * 128, (lt + 1) * 128) for lt in range(W // 128)]
    zeros = jnp.zeros((HALO, W), F32)
    xpad[0:HALO, :] = zeros
    xpad[N + HALO:N + 2 * HALO, :] = zeros

    def fill(j, carry):
        r0 = pl.multiple_of(j * T, T)
        xpad[pl.ds(r0 + HALO, T), :] = u_ref[pl.ds(r0, T), 0:W]
        return carry

    lax.fori_loop(0, nc, fill, 0)

    z = -lam_ref[...]
    sp = jnp.maximum(z, 0.0) + jnp.log1p(jnp.exp(-jnp.abs(z)))
    cw = cw_ref[...]
    cb = cb_ref[...]
    bg = bg_ref[...]

    def fwd(j, carry):
        r0 = pl.multiple_of(j * T, T)
        ext = xpad[pl.ds(r0, T + 2 * HALO), :]
        body = slice(HALO, HALO + T)
        xc = cb
        for tap in range(4):
            xc = xc + _shift_rows(ext, tap - 1)[body] * cw[tap:tap + 1]
        g = _sigmoid(_dot(xc.astype(BF16), wg_ref[...]) + bg)
        ab = []
        for d in range(2):
            r = g[:, d * W:(d + 1) * W]
            i = g[:, (2 + d) * W:(3 + d) * W]
            log_a = (-LRU_C * r) * sp[d:d + 1]
            a = jnp.exp(log_a)
            bt = (jnp.sqrt(1.0 - a * a) * i) * xc
            ab.append((a, bt))
        for lt, lanes in enumerate(tiles):
            a0c[lt] = ab[0][0][:, lanes]
            b0c[lt] = ab[0][1][:, lanes]
            a1s[lt, pl.ds(r0, T), :] = ab[1][0][:, lanes]
            b1s[lt, pl.ds(r0, T), :] = ab[1][1][:, lanes]
        carry = _scan_strided(a0c, b0c, hc, 0, carry, T, reverse=False)
        for lt, lanes in enumerate(tiles):
            y_ref[pl.ds(r0, T), lanes] = hc[lt]
        return carry

    cf = lax.fori_loop(0, nc, fwd, h0_ref[0, 0:1, :])

    def bwd(jj, carry):
        r0 = pl.multiple_of((nc - 1 - jj) * T, T)
        carry = _scan_strided(a1s, b1s, hc, r0, carry, T, reverse=True)
        for lt, lanes in enumerate(tiles):
            gb = u_ref[pl.ds(r0, T), W + lt * 128:W + (lt + 1) * 128]
            y_ref[pl.ds(r0, T), lanes] = (y_ref[pl.ds(r0, T), lanes] + hc[lt]) * _gelu_tanh(gb)
        return carry

    cbw = lax.fori_loop(0, nc, bwd, h0_ref[0, 1:2, :])
    st_ref[0, 0:1, :] = cf
    st_ref[0, 1:2, :] = cbw


def _lru(u, h0, h0_block, lw, *, nb, n):
    T = min(n, 256)
    W = LRU_WIDTH
    return pl.pallas_call(
        functools.partial(_lru_kernel, N=n, T=T),
        grid=(nb,),
        in_specs=[
            pl.BlockSpec((n, 2 * W), lambda b: (b, 0)),
            pl.BlockSpec((1, 2, W), lambda b: (h0_block(b), 0, 0)),
            lw.spec("conv_w"), lw.spec("conv_b"), lw.spec("w_gate"), lw.spec("b_gate"),
            lw.spec("lru_lambda"),
        ],
        out_specs=[
            pl.BlockSpec((n, W), lambda b: (b, 0)),
            pl.BlockSpec((1, 2, W), lambda b: (b, 0, 0)),
        ],
        out_shape=[
            jax.ShapeDtypeStruct((nb * n, W), F32),
            jax.ShapeDtypeStruct((nb, 2, W), F32),
        ],
        scratch_shapes=[
            pltpu.VMEM((n + 2 * HALO, W), F32),
            pltpu.VMEM((W // 128, n, 128), F32),
            pltpu.VMEM((W // 128, n, 128), F32),
            pltpu.VMEM((W // 128, T, 128), F32),
            pltpu.VMEM((W // 128, T, 128), F32),
            pltpu.VMEM((W // 128, T, 128), F32),
        ],
        compiler_params=_params("arbitrary"),
        name="rglru",
    )(u, h0, lw["conv_w"], lw["conv_b"], lw["w_gate"], lw["b_gate"], lw["lru_lambda"])


def _pool_kernel(u_ref, wp_ref, sc_ref, y_ref, xpad, *, N, T):
    W = GROUP_WIDTH
    nc = N // T
    zeros = jnp.zeros((HALO, W), F32)
    xpad[0:HALO, :] = zeros
    xpad[N + HALO:N + 2 * HALO, :] = zeros

    def fill(j, carry):
        r0 = pl.multiple_of(j * T, T)
        xpad[pl.ds(r0 + HALO, T), :] = u_ref[pl.ds(r0, T), :]
        return carry

    lax.fori_loop(0, nc, fill, 0)

    grp = lax.broadcasted_iota(jnp.int32, (1, W), 1) // POOL_CH
    half = jnp.where(grp == 0, 1, jnp.where(grp == 1, 2, jnp.where(grp == 2, 4, 8)))
    scale = sc_ref[...]

    def chunk(j, carry):
        r0 = pl.multiple_of(j * T, T)
        ext = xpad[pl.ds(r0, T + 2 * HALO), :]
        w2 = _shift_rows(ext, -1) + ext
        w4 = _shift_rows(w2, -1) + _shift_rows(w2, 1)
        w8 = _shift_rows(w4, -2) + _shift_rows(w4, 2)
        w16 = _shift_rows(w8, -4) + _shift_rows(w8, 4)
        ws = jnp.where(grp == 0, w2, jnp.where(grp == 1, w4, jnp.where(grp == 2, w8, w16)))
        body = slice(HALO, HALO + T)
        t = r0 + lax.broadcasted_iota(jnp.int32, (T, W), 0)
        cnt = (jnp.minimum(t + half, N) - jnp.maximum(t - half, 0)).astype(F32)
        d = ws[body] / cnt - ext[body]
        y_ref[pl.ds(r0, T), :] = _dot(d.astype(BF16), wp_ref[...]) * scale
        return carry

    lax.fori_loop(0, nc, chunk, 0)


def _pool(u, lw, *, nb, n):
    W = GROUP_WIDTH
    T = min(n, 256)
    return pl.pallas_call(
        functools.partial(_pool_kernel, N=n, T=T),
        grid=(nb,),
        in_specs=[pl.BlockSpec((n, W), lambda b: (b, 0)), lw.spec("w_pool"), lw.spec("pool_scale")],
        out_specs=pl.BlockSpec((n, W), lambda b: (b, 0)),
        out_shape=jax.ShapeDtypeStruct((nb * n, W), F32),
        scratch_shapes=[pltpu.VMEM((n + 2 * HALO, W), F32)],
        compiler_params=_params("arbitrary"),
        name="pool_mixer",
    )(u, lw["w_pool"], lw["pool_scale"])


def _mix_ffn_kernel(*refs, final):
    if final:
        (x_ref, ya_ref, yb_ref, yc_ref, yd_ref, mod_ref, g2_ref, wo_ref, wg_ref, wu_ref, wd_ref,
         gf_ref, o_ref) = refs
    else:
        (x_ref, ya_ref, yb_ref, yc_ref, yd_ref, mod_ref, g2_ref, wo_ref, wg_ref, wu_ref, wd_ref,
         o_ref) = refs
    mod = mod_ref[0]
    gate1 = mod[:, 2 * D_MODEL:3 * D_MODEL]
    sh2 = mod[:, 3 * D_MODEL:4 * D_MODEL]
    sc2 = mod[:, 4 * D_MODEL:5 * D_MODEL]
    gate2 = mod[:, 5 * D_MODEL:6 * D_MODEL]
    mix = None
    for i, y_ref in enumerate((ya_ref, yb_ref, yc_ref, yd_ref)):
        part = _dot(y_ref[...].astype(BF16), wo_ref[i * GROUP_WIDTH:(i + 1) * GROUP_WIDTH, :])
        mix = part if mix is None else mix + part
    x1 = x_ref[...] + gate1 * mix
    h = _rms_rows(x1, D_MODEL) * g2_ref[...]
    hb = (h * (1.0 + sc2) + sh2).astype(BF16)
    ff = None
    for lo, hi in FF_CHUNKS:
        g = _dot(hb, wg_ref[:, lo:hi])
        up = _dot(hb, wu_ref[:, lo:hi])
        act = ((g * jax.nn.sigmoid(g)) * up).astype(BF16)
        part = _dot(act, wd_ref[lo:hi, :])
        ff = part if ff is None else ff + part
    x2 = x1 + gate2 * ff
    if final:
        x2 = _rms_rows(x2, D_MODEL) * gf_ref[...]
    o_ref[...] = x2


def _mix_ffn(x, ys, mod, lw, gf, *, nb, n, final):
    T = nb * n
    tm = TOKEN_TILE
    npt = n // tm

    def tok(width):
        return pl.BlockSpec((tm, width), lambda i: (i, 0))

    in_specs = [tok(D_MODEL), tok(256), tok(256), tok(256), tok(256), mod.spec(lambda i: i // npt),
                lw.spec("g2"), lw.spec("w_out"), lw.spec("w_gu", col_blocks=2, col_block=0),
                lw.spec("w_gu", col_blocks=2, col_block=1), lw.spec("w_down")]
    args = [x, *ys, mod.table, lw["g2"], lw["w_out"], lw["w_gu"], lw["w_gu"], lw["w_down"]]
    if final:
        in_specs.append(_resident((1, D_MODEL)))
        args.append(gf)
    return pl.pallas_call(
        functools.partial(_mix_ffn_kernel, final=final),
        grid=(T // tm,),
        in_specs=in_specs,
        out_specs=tok(D_MODEL),
        out_shape=jax.ShapeDtypeStruct((T, D_MODEL), F32),
        compiler_params=_params("arbitrary"),
        name="mix_ffn_final" if final else "mix_ffn",
    )(*args)


def _block_diag(w):
    L, G, c, e = w.shape
    return jnp.einsum('lgce,gh->lgche', w, jnp.eye(G, dtype=w.dtype)).reshape(L, G * c, G * e)


def _rot_cols(w):
    return jnp.concatenate([-w[..., 16:32], w[..., 0:16]], axis=-1)


def _stack_weights(p):
    w_in = p["w_in"]
    o1 = MLA_Q_RANK
    o2 = o1 + MLA_KV_RANK
    o3 = o2 + MLA_ROPE
    c_q, c_kv, k_r, rest = w_in[..., :o1], w_in[..., o1:o2], w_in[..., o2:o3], w_in[..., o3:]
    z = lambda n: jnp.zeros((DEPTH, D_MODEL, n), F32)
    w_in_eff = jnp.concatenate([c_q, k_r, z(32), c_kv, z(64), _rot_cols(k_r), z(32), rest], axis=-1)

    w_uq = p["mla_w_uq"]
    qd = MLA_NOPE + MLA_ROPE
    wq_parts, wqr_parts = [], []
    zq = lambda n: jnp.zeros((DEPTH, MLA_Q_RANK, n), F32)
    for h in range(MLA_HEADS):
        wh = w_uq[..., h * qd:(h + 1) * qd]
        wq_parts += [wh, zq(MLA_SLOT - qd)]
        wqr_parts += [zq(MLA_NOPE), _rot_cols(wh[..., MLA_NOPE:]), zq(MLA_SLOT - qd)]
    pad_rows = lambda w: jnp.pad(w, ((0, 0), (0, 256 - MLA_Q_RANK), (0, 0)))
    w_ukv = p["mla_w_ukv"]
    wk_parts, wv_parts = [], []
    zk = jnp.zeros((DEPTH, MLA_KV_RANK, MLA_SLOT - MLA_NOPE), F32)
    for h in range(MLA_HEADS):
        base = h * (MLA_NOPE + MLA_V)
        wk_parts += [w_ukv[..., base:base + MLA_NOPE], zk]
        wv_parts.append(w_ukv[..., base + MLA_NOPE:base + MLA_NOPE + MLA_V])

    w_r, w_i, b_r, b_i = p["lru_w_r"], p["lru_w_i"], p["lru_b_r"], p["lru_b_i"]
    w_gate = jnp.concatenate([_block_diag(w_r[:, 0]), _block_diag(w_r[:, 1]),
                              _block_diag(w_i[:, 0]), _block_diag(w_i[:, 1])], axis=-1)
    b_gate = jnp.concatenate([b_r[:, 0], b_r[:, 1], b_i[:, 0], b_i[:, 1]], axis=-1)
    row = lambda v: v[:, None, :]
    return {
        "g1": row(p["norm1_g"]),
        "g2": row(p["norm2_g"]),
        "w_in": w_in_eff.astype(BF16),
        "gq": row(jnp.pad(p["mla_q_norm_g"], ((0, 0), (0, 256 - MLA_Q_RANK)))),
        "gkv": row(p["mla_kv_norm_g"]),
        "wq": pad_rows(jnp.concatenate(wq_parts, axis=-1)).astype(BF16),
        "wqr": pad_rows(jnp.concatenate(wqr_parts, axis=-1)).astype(BF16),
        "wkv": jnp.concatenate(wk_parts + wv_parts, axis=-1).astype(BF16),
        "conv_w": p["lru_conv_w"],
        "conv_b": row(p["lru_conv_b"]),
        "w_gate": w_gate.astype(BF16),
        "b_gate": row(b_gate),
        "lru_lambda": p["lru_lambda"],
        "w_pool": _block_diag(p["pool_w"]).astype(BF16),
        "pool_scale": row(p["pool_scale"]),
        "diff_lambda": p["diff_lambda"],
        "diff_g": row(jnp.tile(p["diff_norm_g"], (1, DIFF_HEADS))),
        "w_out": p["w_out"].astype(BF16),
        "w_gu": p["w_gu"].astype(BF16),
        "w_down": p["w_down"].astype(BF16),
    }


def _rope_tables(n, positional):
    quarter = MLA_ROPE // 4
    if positional:
        t = jnp.arange(n)
        row = (t // GRID_W).astype(F32)
        col = (t % GRID_W).astype(F32)
        inv = ROPE_BASE ** (-jnp.arange(quarter, dtype=F32) / quarter)
        ang = jnp.concatenate([row[:, None] * inv, col[:, None] * inv], axis=-1)
        cos, sin = jnp.cos(ang), jnp.sin(ang)
    else:
        cos, sin = jnp.ones((n, 16), F32), jnp.zeros((n, 16), F32)
    scale = LOG2E / math.sqrt(MLA_NOPE + MLA_ROPE)
    place = np.zeros((32, TAB_WIDTH), np.float32)
    offset = np.zeros((1, TAB_WIDTH), np.float32)
    offset[0, 0:64] = offset[0, 96:128] = scale
    for i in range(16):
        for half in (64, 80):
            place[i, half + i] = scale
            place[16 + i, 128 + half + i] = scale
            place[i, 256 + half + i] = 1.0
            place[16 + i, 384 + half + i] = 1.0
        for grp in range(8):
            place[i, 512 + 32 * grp + i] = place[i, 512 + 32 * grp + 16 + i] = 1.0
            place[16 + i, 768 + 32 * grp + i] = -1.0
            place[16 + i, 1024 + 32 * grp + 16 + i] = 1.0
    return jnp.dot(jnp.concatenate([cos, sin], axis=1), place, precision=lax.Precision.HIGHEST) + offset


def _layer(x, mod, lw, tabs, layer_idx, ctx, gf, *, nb, n, final):
    emit_cache = ctx is None
    tok_nb, tok_n = (1, nb * n) if mod.shared else (nb, n)
    outs = _inproj(x, mod, lw, tabs, nb=tok_nb, n=tok_n, emit_cache=emit_cache)
    q, k, vt, u_lru, u_pool, dq, dk, dvt = outs[:8]
    lam_init = 0.8 - 0.6 * math.exp(-0.3 * layer_idx)
    if ctx is None:
        h0 = jnp.zeros((1, 2, LRU_WIDTH), F32)
        h0_block = lambda b: 0
        mla_ctx = diff_ctx = None
    else:
        ckv, kr_pad, cdk, cdv, h0 = ctx
        p = ckv.shape[0] // (nb * DEPTH)
        h0_block = lambda b: b * DEPTH + layer_idx
        kc, vtc, dkc, dvtc = _ctx_prep(ckv, kr_pad, cdk, cdv, lw, nb=nb, p=p)
        mla_ctx = (kc, vtc)
        diff_ctx = (dkc, dvtc)
    y_mla = _mla_attn(q, k, vt, mla_ctx, nb=nb, n=n)
    y_lru, st = _lru(u_lru, h0, h0_block, lw, nb=nb, n=n)
    y_pool = _pool(u_pool, lw, nb=nb, n=n)
    y_diff = _diff_attn(dq, dk, dvt, diff_ctx, lw, nb=nb, n=n, lam_init=lam_init)
    x2 = _mix_ffn(x, (y_mla, y_lru, y_pool, y_diff), mod, lw, gf, nb=tok_nb, n=tok_n, final=final)
    cache = (outs[8], outs[9][:, 64:96], outs[10], outs[11], st) if emit_cache else None
    return x2, cache


def kernel(x_prompt, x_sample, cache_mla_ckv, cache_mla_krope, cache_diff_k, cache_diff_v, state_lru,
           c, c_ctx, w_ada, b_ada, norm1_g, norm2_g, w_in, mla_q_norm_g, mla_w_uq, mla_kv_norm_g,
           mla_w_ukv, lru_conv_w, lru_conv_b, lru_w_r, lru_b_r, lru_w_i, lru_b_i, lru_lambda, pool_w,
           pool_scale, diff_lambda, diff_norm_g, w_out, w_gu, w_down, final_norm_g):
    p = {
        "norm1_g": norm1_g, "norm2_g": norm2_g, "w_in": w_in, "mla_q_norm_g": mla_q_norm_g,
        "mla_w_uq": mla_w_uq, "mla_kv_norm_g": mla_kv_norm_g, "mla_w_ukv": mla_w_ukv,
        "lru_conv_w": lru_conv_w, "lru_conv_b": lru_conv_b, "lru_w_r": lru_w_r, "lru_b_r": lru_b_r,
        "lru_w_i": lru_w_i, "lru_b_i": lru_b_i, "lru_lambda": lru_lambda, "pool_w": pool_w,
        "pool_scale": pool_scale, "diff_lambda": diff_lambda, "diff_norm_g": diff_norm_g,
        "w_out": w_out, "w_gu": w_gu, "w_down": w_down,
    }
    Bp, Np, _ = x_prompt.shape
    Bs, Ns, _ = x_sample.shape
    P = cache_mla_ckv.shape[2]

    cond_all = jnp.concatenate([c, c_ctx[None, :], jnp.zeros((MOD_ROWS - Bs - 1, D_MODEL), F32)], axis=0)
    mod_table = _ada(cond_all, w_ada, b_ada).reshape(DEPTH * MOD_ROWS, 1, 6 * D_MODEL)
    tabs_p = _rope_tables(Bp * Np, positional=False)
    tabs_s = _rope_tables(Ns, positional=True)
    kr_pad = jnp.pad(cache_mla_krope, ((0, 0), (0, 0), (0, 0), (MLA_NOPE, MLA_SLOT - MLA_NOPE - MLA_ROPE)))
    flat = lambda a, w: a.reshape(Bs * DEPTH * P, w)
    ctx = (flat(cache_mla_ckv, MLA_KV_RANK), flat(kr_pad, MLA_SLOT), flat(cache_diff_k, 256),
           flat(cache_diff_v, 256), state_lru.reshape(Bs * DEPTH, 2, LRU_WIDTH))
    gf = final_norm_g[None, :]
    stacked = _stack_weights(p)

    xp = x_prompt.reshape(Bp * Np, D_MODEL)
    xs = x_sample.reshape(Bs * Ns, D_MODEL)
    caches = []
    for l in range(DEPTH):
        lw = _LayerWeights(stacked, l)
        final = l == DEPTH - 1
        mod_p = _Mod(mod_table, l * MOD_ROWS + Bs, shared=True)
        mod_s = _Mod(mod_table, l * MOD_ROWS, shared=False)
        xp, cache = _layer(xp, mod_p, lw, tabs_p, l, None, gf, nb=Bp, n=Np, final=final)
        caches.append(cache)
        xs, _ = _layer(xs, mod_s, lw, tabs_s, l, ctx, gf, nb=Bs, n=Ns, final=final)

    stack = lambda i, w: jnp.stack([cc[i].reshape(Bp, Np, w) for cc in caches], axis=1)
    new_mla_ckv = stack(0, MLA_KV_RANK)
    new_mla_krope = stack(1, MLA_ROPE)
    new_diff_k = stack(2, 256).reshape(Bp, DEPTH, Np, DIFF_HEADS, 2, DIFF_DIM)
    new_diff_v = stack(3, 256).reshape(Bp, DEPTH, Np, DIFF_HEADS, 2 * DIFF_DIM)
    new_state_lru = jnp.stack([cc[4] for cc in caches], axis=1)
    return (xp.reshape(Bp, Np, D_MODEL), xs.reshape(Bs, Ns, D_MODEL),
            new_mla_ckv, new_mla_krope, new_diff_k, new_diff_v, new_state_lru)
```

```python
import functools
import math

import jax
import jax.numpy as jnp
import numpy as np
from jax import lax
from jax.experimental import pallas as pl
from jax.experimental.pallas import tpu as pltpu

F32 = jnp.float32
BF16 = jnp.bfloat16

D_MODEL = 1024
DEPTH = 2
GRID_W = 64
GROUP_WIDTH = 256
MLA_HEADS = 4
MLA_NOPE = 64
MLA_ROPE = 32
MLA_V = 64
MLA_Q_RANK = 192
MLA_KV_RANK = 128
MLA_SLOT = 128
LRU_WIDTH = 256
LRU_C = 8.0
POOL_WINDOWS = (2, 4, 8, 16)
POOL_CH = 64
DIFF_HEADS = 4
DIFF_DIM = 32
HEAD_V = 64
FF_HIDDEN = 2816
FF_CHUNKS = ((0, 1536), (1536, 2816))
ROPE_BASE = 10000.0
EPS = 1e-6
IN_EFF = 2048
HALO = 8
SCAN_RUN = 4
VT_ROWS = 80
ATT_TQ = 256
TOKEN_TILE = 512
TAB_WIDTH = 4 * 128 + 3 * 256
MOD_ROWS = 16
LOG2E = math.log2(math.e)

VMEM_LIMIT_BYTES = 56 * 1024 * 1024

_NT = (((1,), (1,)), ((), ()))


def _params(*sem):
    return pltpu.CompilerParams(dimension_semantics=sem, vmem_limit_bytes=VMEM_LIMIT_BYTES)


def _resident(shape):
    zeros = (0,) * len(shape)
    return pl.BlockSpec(shape, lambda *_: zeros, pipeline_mode=pl.Buffered(1))


def _dot(a, b):
    return jnp.dot(a, b, preferred_element_type=F32)


def _dot_nt(a, b):
    return lax.dot_general(a, b, _NT, preferred_element_type=F32)


def _rms_rows(x, width):
    ms = jnp.sum(x * x, axis=-1, keepdims=True) * (1.0 / width)
    return x * lax.rsqrt(ms + EPS)


def _store_vt(vt_ref, v):
    vt = v.T
    rows = v.shape[0]
    pad = VT_ROWS - HEAD_V
    ones_row = jnp.where(lax.broadcasted_iota(jnp.int32, (pad, rows), 0) == 0, 1.0, 0.0).astype(BF16)
    for hh in range(vt_ref.shape[0]):
        vt_ref[hh, 0:HEAD_V, :] = vt[hh * HEAD_V:(hh + 1) * HEAD_V, :].astype(BF16)
        vt_ref[hh, HEAD_V:VT_ROWS, :] = ones_row


class _Mod:
    def __init__(self, table, row0, shared):
        self.table, self.row0, self.shared = table, row0, shared

    def spec(self, batch_of):
        row0 = self.row0
        if self.shared:
            return pl.BlockSpec((1, 1, 6 * D_MODEL), lambda *g: (row0, 0, 0))
        return pl.BlockSpec((1, 1, 6 * D_MODEL), lambda *g: (row0 + batch_of(*g), 0, 0))


class _LayerWeights:
    def __init__(self, stacked, layer):
        self.stacked, self.layer = stacked, layer

    def __getitem__(self, name):
        return self.stacked[name]

    def spec(self, name, col_blocks=1, col_block=0):
        layer = self.layer
        _, rows, cols = self.stacked[name].shape
        return pl.BlockSpec((None, rows, cols // col_blocks), lambda *_: (layer, 0, col_block),
                            pipeline_mode=pl.Buffered(1))


def _ada_kernel(cond_ref, w_ref, b_ref, out_ref):
    c = cond_ref[...]
    s = c * jax.nn.sigmoid(c)
    out_ref[0] = _dot(s.astype(BF16), w_ref[0].astype(BF16)) + b_ref[0]


def _ada(cond_all, w_ada, b_ada):
    rows = cond_all.shape[0]
    tn = 1536
    return pl.pallas_call(
        _ada_kernel,
        grid=(DEPTH, 6 * D_MODEL // tn),
        in_specs=[
            pl.BlockSpec((rows, D_MODEL), lambda l, j: (0, 0)),
            pl.BlockSpec((1, D_MODEL, tn), lambda l, j: (l, 0, j)),
            pl.BlockSpec((1, 1, tn), lambda l, j: (l, 0, j)),
        ],
        out_specs=pl.BlockSpec((1, rows, tn), lambda l, j: (l, 0, j)),
        out_shape=jax.ShapeDtypeStruct((DEPTH, rows, 6 * D_MODEL), F32),
        compiler_params=_params("arbitrary", "arbitrary"),
        name="ada_mod",
    )(cond_all, w_ada, b_ada.reshape(DEPTH, 1, 6 * D_MODEL))


def _inproj_kernel(x_ref, mod_ref, g1_ref, win_ref, gq_ref, gkv_ref, wq_ref, wqr_ref, wkv_ref, tab_ref,
                   q_out, k_out, vt_out, lru_out, pool_out, dq_out, dk_out, dvt_out, *cache_outs):
    cosq_ref, sinq_ref, cosk_ref, sink_ref = (tab_ref.at[:, i * 128:(i + 1) * 128] for i in range(4))
    cosd_ref, sina_ref, sinb_ref = (tab_ref.at[:, 512 + i * 256:768 + i * 256] for i in range(3))
    x = x_ref[...]
    mod = mod_ref[0]
    sh1 = mod[:, 0:D_MODEL]
    sc1 = mod[:, D_MODEL:2 * D_MODEL]
    h = _rms_rows(x, D_MODEL) * g1_ref[...]
    hb = (h * (1.0 + sc1) + sh1).astype(BF16)

    u_mla = _dot(hb, win_ref[:, 0:512])
    u_pd = _dot(hb, win_ref[:, 1024:1536])
    u_kv = _dot(hb, win_ref[:, 1536:2048])
    t01 = u_mla[:, 0:256]
    lane = lax.broadcasted_iota(jnp.int32, (1, 256), 1)
    cq = jnp.where(lane < MLA_Q_RANK, t01, 0.0)
    cqn = (_rms_rows(cq, MLA_Q_RANK) * gq_ref[...]).astype(BF16)
    qa = _dot(cqn, wq_ref[...])
    qr = _dot(cqn, wqr_ref[...])
    cosq = cosq_ref[...]
    sinq = sinq_ref[...]
    ckv = u_mla[:, 256:384]
    lat = _rms_rows(ckv, MLA_KV_RANK) * gkv_ref[...]
    latb = lat.astype(BF16)
    kkv = _dot(latb, wkv_ref[...])
    kk = kkv[:, 0:MLA_HEADS * MLA_SLOT]
    _store_vt(vt_out, kkv[:, MLA_HEADS * MLA_SLOT:])
    t1 = t01[:, 128:256]
    t3 = u_mla[:, 384:512]
    kro = t1 * cosk_ref[...] + t3 * sink_ref[...]
    for hh in range(MLA_HEADS):
        sl = slice(hh * MLA_SLOT, (hh + 1) * MLA_SLOT)
        q_out[hh] = (qa[:, sl] * cosq + qr[:, sl] * sinq).astype(q_out.dtype)
        k_out[hh] = (kk[:, sl] + kro).astype(k_out.dtype)

    lru_out[...] = _dot(hb, win_ref[:, 512:1024])
    pool_out[...] = u_pd[:, 0:256]

    cosd = cosd_ref[...]
    sina = sina_ref[...]
    sinb = sinb_ref[...]

    def rope(t):
        return t * cosd + pltpu.roll(t, 256 - 16, 1) * sina + pltpu.roll(t, 16, 1) * sinb

    dq = u_pd[:, 256:512]
    dk = u_kv[:, 0:256]
    dv = u_kv[:, 256:512]
    dq_out[...] = (rope(dq) * (LOG2E / math.sqrt(DIFF_DIM))).astype(dq_out.dtype)
    dk_out[...] = rope(dk).astype(dk_out.dtype)
    _store_vt(dvt_out, dv)

    if cache_outs:
        lat_out, kr_out, dk_raw_out, dv_raw_out = cache_outs
        lat_out[...] = lat
        kr_out[...] = t1
        dk_raw_out[...] = dk
        dv_raw_out[...] = dv


def _inproj(x, mod, lw, tabs, *, nb, n, emit_cache):
    T = nb * n
    tm = TOKEN_TILE
    npt = n // tm
    row_blk = lambda j, b: b * npt + j

    def tok(width):
        return pl.BlockSpec((tm, width), lambda j, b: (row_blk(j, b), 0))

    def tab(width):
        return pl.BlockSpec((tm, width), lambda j, b: (j, 0))

    head = pl.BlockSpec((MLA_HEADS, tm, MLA_SLOT), lambda j, b: (0, row_blk(j, b), 0))
    vt_spec = pl.BlockSpec((MLA_HEADS, VT_ROWS, tm), lambda j, b: (0, 0, row_blk(j, b)))
    wnames = ("g1", "w_in", "gq", "gkv", "wq", "wqr", "wkv")
    in_specs = [tok(D_MODEL), mod.spec(lambda j, b: b)] + [lw.spec(nm) for nm in wnames] + [
        tab(TAB_WIDTH)]
    out_specs = [head, head, vt_spec, tok(512), tok(256), tok(256), tok(256), vt_spec]
    vt_shape = jax.ShapeDtypeStruct((MLA_HEADS, VT_ROWS, T), BF16)
    out_shape = [
        jax.ShapeDtypeStruct((MLA_HEADS, T, MLA_SLOT), BF16),
        jax.ShapeDtypeStruct((MLA_HEADS, T, MLA_SLOT), BF16),
        vt_shape,
        jax.ShapeDtypeStruct((T, 512), F32),
        jax.ShapeDtypeStruct((T, 256), F32),
        jax.ShapeDtypeStruct((T, 256), BF16),
        jax.ShapeDtypeStruct((T, 256), BF16),
        vt_shape,
    ]
    if emit_cache:
        out_specs += [tok(128), tok(128), tok(256), tok(256)]
        out_shape += [jax.ShapeDtypeStruct((T, 128), F32), jax.ShapeDtypeStruct((T, 128), F32),
                      jax.ShapeDtypeStruct((T, 256), F32), jax.ShapeDtypeStruct((T, 256), F32)]
    return pl.pallas_call(
        _inproj_kernel,
        grid=(npt, nb),
        in_specs=in_specs,
        out_specs=out_specs,
        out_shape=out_shape,
        compiler_params=_params("arbitrary", "arbitrary"),
        name="inproj_cache" if emit_cache else "inproj",
    )(x, mod.table, *[lw[nm] for nm in wnames], tabs)


def _ctx_prep_kernel(ckv_ref, kr_ref, dk_ref, dv_ref, wkv_ref, k_out, vt_out, dk_out, dvt_out):
    latb = ckv_ref[...].astype(BF16)
    kkv = _dot(latb, wkv_ref[...])
    kk = kkv[:, 0:MLA_HEADS * MLA_SLOT]
    kr = kr_ref[...]
    for hh in range(MLA_HEADS):
        k_out[hh] = (kk[:, hh * MLA_SLOT:(hh + 1) * MLA_SLOT] + kr).astype(k_out.dtype)
    _store_vt(vt_out, kkv[:, MLA_HEADS * MLA_SLOT:])
    dk_out[...] = dk_ref[...].astype(dk_out.dtype)
    _store_vt(dvt_out, dv_ref[...])


def _ctx_prep(ckv, kr_pad, cdk, cdv, lw, *, nb, p):
    T = nb * p
    layer = lw.layer
    cache_row = lambda w: pl.BlockSpec((p, w), lambda b: (b * DEPTH + layer, 0))
    row = lambda w: pl.BlockSpec((p, w), lambda b: (b, 0))
    vt_spec = pl.BlockSpec((MLA_HEADS, VT_ROWS, p), lambda b: (0, 0, b))
    vt_shape = jax.ShapeDtypeStruct((MLA_HEADS, VT_ROWS, T), BF16)
    return pl.pallas_call(
        _ctx_prep_kernel,
        grid=(nb,),
        in_specs=[cache_row(128), cache_row(128), cache_row(256), cache_row(256),
                  lw.spec("wkv")],
        out_specs=[pl.BlockSpec((MLA_HEADS, p, MLA_SLOT), lambda b: (0, b, 0)), vt_spec, row(256), vt_spec],
        out_shape=[jax.ShapeDtypeStruct((MLA_HEADS, T, MLA_SLOT), BF16), vt_shape,
                   jax.ShapeDtypeStruct((T, 256), BF16), vt_shape],
        compiler_params=_params("arbitrary"),
        name="ctx_prep",
    )(ckv, kr_pad, cdk, cdv, lw["wkv"])


SAFE_DENOM = 2.0 ** -60
E_BUFS = 3
BOUND_SLACK = 1.02


def _scores(k_new, k_ctx, q):
    sn = _dot_nt(k_new(), q)
    sc = _dot_nt(k_ctx(), q) if k_ctx is not None else None
    return sn, sc


def _exact_shift(k_new, k_ctx, q):
    sn, sc = _scores(k_new, k_ctx, q)
    m = jnp.max(sn, axis=0, keepdims=True)
    if sc is not None:
        m = jnp.maximum(m, jnp.max(sc, axis=0, keepdims=True))
    return m


def _bound_shift(q, key_norm2):
    qf = q.astype(F32)
    ones = jnp.ones((8, q.shape[1]), BF16)
    q_norm2 = _dot_nt(ones, (qf * qf).astype(BF16))[0:1, :]
    return jnp.sqrt(q_norm2 * key_norm2) * BOUND_SLACK


def _max_row_norm2(k_new, k_ctx, col_sum):
    def one(k):
        kf = k.astype(F32)
        return jnp.max(_dot((kf * kf).astype(BF16), col_sum), axis=0, keepdims=True)
    m = one(k_new)
    if k_ctx is not None:
        m = jnp.maximum(m, one(k_ctx))
    return m * BOUND_SLACK


def _exp_stage(e_buf, k_new, k_ctx, q, shift, n_ctx):
    sn, sc = _scores(k_new, k_ctx, q)
    e_buf[n_ctx:, :] = jnp.exp2(sn - shift).astype(BF16)
    if sc is not None:
        e_buf[0:n_ctx, :] = jnp.exp2(sc - shift).astype(BF16)


def _value_stage(e_buf, vt_new, vt_ctx, n_ctx):
    o = _dot(vt_new(), e_buf[n_ctx:, :])
    if vt_ctx is not None:
        o = o + _dot(vt_ctx(), e_buf[0:n_ctx, :])
    return o


def _run_pipeline(n_maps, exp_stage, value_stage):
    ahead = E_BUFS - 1
    for u in range(min(ahead, n_maps)):
        exp_stage(u)
    for u in range(n_maps):
        if u + ahead < n_maps:
            exp_stage(u + ahead)
        value_stage(u)


def _att_scratch(nk, key_shape):
    scratch = [pltpu.VMEM((8, 128), F32),
               pltpu.VMEM((MLA_HEADS * HEAD_V, ATT_TQ), F32)]
    if key_shape is not None:
        scratch += [pltpu.VMEM(key_shape, BF16), pltpu.VMEM((MLA_HEADS, VT_ROWS, nk), BF16)]
    return scratch + [pltpu.VMEM((nk, ATT_TQ), BF16)] * E_BUFS


def _att_nsub(n):
    return 2 if n % (2 * ATT_TQ) == 0 else 1


def _mla_attn_kernel(*refs, has_ctx, nsub):
    if has_ctx:
        q_ref, k_ref, vt_ref, kc_ref, vtc_ref, o_ref, kn2, ot, keys, vals, *e_bufs = refs
        n_ctx = kc_ref.shape[1]
    else:
        q_ref, k_ref, vt_ref, o_ref, kn2, ot, *e_bufs = refs
        keys, vals = k_ref, vt_ref

    @pl.when(pl.program_id(1) == 0)
    def _():
        ones = jnp.ones((MLA_SLOT, 128), BF16)
        for hh in range(MLA_HEADS):
            kn2[hh:hh + 1, :] = _max_row_norm2(k_ref[hh], kc_ref[hh] if has_ctx else None, ones)
        if has_ctx:
            keys[:, 0:n_ctx, :] = kc_ref[...]
            keys[:, n_ctx:, :] = k_ref[...]
            vals[:, :, 0:n_ctx] = vtc_ref[...]
            vals[:, :, n_ctx:] = vt_ref[...]

    def run(exact):
        denoms = []

        def exp_stage(u):
            t, hh = divmod(u, MLA_HEADS)
            q = q_ref[hh, t * ATT_TQ:(t + 1) * ATT_TQ, :]
            k_all = lambda: keys[hh]
            shift = _exact_shift(k_all, None, q) if exact else _bound_shift(q, kn2[hh:hh + 1, 0:1])
            _exp_stage(e_bufs[u % E_BUFS], k_all, None, q, shift, 0)

        def value_stage(u):
            t, hh = divmod(u, MLA_HEADS)
            o = _value_stage(e_bufs[u % E_BUFS], lambda: vals[hh], None, 0)
            denom = o[HEAD_V:HEAD_V + 1, :]
            denoms.append(denom)
            ot[hh * HEAD_V:(hh + 1) * HEAD_V, :] = o[0:HEAD_V, :] * (1.0 / denom)
            if hh == MLA_HEADS - 1:
                o_ref[t * ATT_TQ:(t + 1) * ATT_TQ, :] = ot[...].T

        _run_pipeline(nsub * MLA_HEADS, exp_stage, value_stage)
        return jnp.min(functools.reduce(jnp.minimum, denoms))

    denom_min = run(exact=False)

    @pl.when(jnp.logical_not(denom_min >= SAFE_DENOM))
    def _():
        run(exact=True)


def _mla_attn(q, k, vt, ctx, *, nb, n):
    nsub = _att_nsub(n)
    tq = nsub * ATT_TQ
    npt = n // tq
    H, S = MLA_HEADS, MLA_SLOT
    in_specs = [
        pl.BlockSpec((H, tq, S), lambda b, j: (0, b * npt + j, 0)),
        pl.BlockSpec((H, n, S), lambda b, j: (0, b, 0)),
        pl.BlockSpec((H, VT_ROWS, n), lambda b, j: (0, 0, b)),
    ]
    args = [q, k, vt]
    n_ctx = 0
    if ctx is not None:
        n_ctx = ctx[0].shape[1] // nb
        in_specs += [
            pl.BlockSpec((H, n_ctx, S), lambda b, j: (0, b, 0)),
            pl.BlockSpec((H, VT_ROWS, n_ctx), lambda b, j: (0, 0, b)),
        ]
        args += list(ctx)
    return pl.pallas_call(
        functools.partial(_mla_attn_kernel, has_ctx=ctx is not None, nsub=nsub),
        grid=(nb, npt),
        in_specs=in_specs,
        out_specs=pl.BlockSpec((tq, 256), lambda b, j: (b * npt + j, 0)),
        out_shape=jax.ShapeDtypeStruct((nb * n, 256), F32),
        scratch_shapes=_att_scratch(n + n_ctx, (H, n + n_ctx, S) if ctx is not None else None),
        compiler_params=_params("arbitrary", "arbitrary"),
        name="mla_attn_ctx" if ctx is not None else "mla_attn",
    )(*args)


def _diff_attn_kernel(*refs, has_ctx, nsub, lam_init):
    if has_ctx:
        lv_ref, g_ref, q_ref, k_ref, vt_ref, kc_ref, vtc_ref, o_ref, kn2, ot, keys, vals, *e_bufs = refs
        n_ctx = kc_ref.shape[0]
    else:
        lv_ref, g_ref, q_ref, k_ref, vt_ref, o_ref, kn2, ot, *e_bufs = refs
        keys, vals = k_ref, vt_ref
    lv = lv_ref[...]
    lam = (jnp.exp(jnp.sum(lv[0:1] * lv[1:2], axis=-1, keepdims=True))
           - jnp.exp(jnp.sum(lv[2:3] * lv[3:4], axis=-1, keepdims=True)) + lam_init)
    lane128 = lax.broadcasted_iota(jnp.int32, (1, 128), 1)
    n_pairs = 2 * DIFF_HEADS

    @pl.when(pl.program_id(1) == 0)
    def _():
        dim = lax.broadcasted_iota(jnp.int32, (256, 128), 0)
        col = lax.broadcasted_iota(jnp.int32, (256, 128), 1)
        indicator = jnp.where(dim // DIFF_DIM == col, 1.0, 0.0).astype(BF16)
        kn2[0:1, :] = _max_row_norm2(k_ref[...], kc_ref[...] if has_ctx else None, indicator)
        if has_ctx:
            keys[0:n_ctx, :] = kc_ref[...]
            keys[n_ctx:, :] = k_ref[...]
            vals[:, :, 0:n_ctx] = vtc_ref[...]
            vals[:, :, n_ctx:] = vt_ref[...]

    def run(exact):
        denoms = []
        outs = {}

        def exp_stage(u):
            t, p = divmod(u, n_pairs)
            tile = slice((p * DIFF_DIM // 128) * 128, (p * DIFF_DIM // 128 + 1) * 128)
            k_new = lambda: keys[:, tile]
            k_ctx = None
            q = q_ref[t * ATT_TQ:(t + 1) * ATT_TQ, tile]
            lo = p * DIFF_DIM - tile.start
            in_pair = (lane128 >= lo) & (lane128 < lo + DIFF_DIM)
            qm = jnp.where(in_pair, q, jnp.zeros_like(q))
            shift = _exact_shift(k_new, k_ctx, qm) if exact else _bound_shift(qm, kn2[0:1, p:p + 1])
            _exp_stage(e_bufs[u % E_BUFS], k_new, k_ctx, qm, shift, 0)

        def value_stage(u):
            t, p = divmod(u, n_pairs)
            hh = p // 2
            o = _value_stage(e_bufs[u % E_BUFS], lambda: vals[hh], None, 0)
            denom = o[HEAD_V:HEAD_V + 1, :]
            denoms.append(denom)
            outs[u] = (o[0:HEAD_V, :], denom)
            if p % 2 == 1:
                (o0, l0), (o1, l1) = outs.pop(u - 1), outs.pop(u)
                o = o0 * (1.0 / l0) - o1 * (lam / l1)
                msq = jnp.sum(o * o, axis=0, keepdims=True) * (1.0 / HEAD_V)
                ot[hh * HEAD_V:(hh + 1) * HEAD_V, :] = o * lax.rsqrt(msq + EPS)
            if p == n_pairs - 1:
                o_ref[t * ATT_TQ:(t + 1) * ATT_TQ, :] = (ot[...].T * g_ref[...]) * (1.0 - lam_init)

        _run_pipeline(nsub * n_pairs, exp_stage, value_stage)
        return jnp.min(functools.reduce(jnp.minimum, denoms))

    denom_min = run(exact=False)

    @pl.when(jnp.logical_not(denom_min >= SAFE_DENOM))
    def _():
        run(exact=True)


def _diff_attn(q, k, vt, ctx, lw, *, nb, n, lam_init):
    nsub = _att_nsub(n)
    tq = nsub * ATT_TQ
    npt = n // tq
    in_specs = [
        lw.spec("diff_lambda"),
        lw.spec("diff_g"),
        pl.BlockSpec((tq, 256), lambda b, j: (b * npt + j, 0)),
        pl.BlockSpec((n, 256), lambda b, j: (b, 0)),
        pl.BlockSpec((DIFF_HEADS, VT_ROWS, n), lambda b, j: (0, 0, b)),
    ]
    args = [lw["diff_lambda"], lw["diff_g"], q, k, vt]
    n_ctx = 0
    if ctx is not None:
        n_ctx = ctx[0].shape[0] // nb
        in_specs += [pl.BlockSpec((n_ctx, 256), lambda b, j: (b, 0)),
                     pl.BlockSpec((DIFF_HEADS, VT_ROWS, n_ctx), lambda b, j: (0, 0, b))]
        args += list(ctx)
    return pl.pallas_call(
        functools.partial(_diff_attn_kernel, has_ctx=ctx is not None, nsub=nsub, lam_init=lam_init),
        grid=(nb, npt),
        in_specs=in_specs,
        out_specs=pl.BlockSpec((tq, 256), lambda b, j: (b * npt + j, 0)),
        out_shape=jax.ShapeDtypeStruct((nb * n, 256), F32),
        scratch_shapes=_att_scratch(n + n_ctx, (n + n_ctx, 256) if ctx is not None else None),
        compiler_params=_params("arbitrary", "arbitrary"),
        name="diff_attn_ctx" if ctx is not None else "diff_attn",
    )(*args)


def _shift_rows(v, k):
    return pltpu.roll(v, (-k) % v.shape[0], 0)


def _scan_strided(a_ref, b_ref, h_ref, row0, carry, n_rows, reverse):
    sub = lax.broadcasted_iota(jnp.int32, (8, 128), 0)
    span = 8 * SCAN_RUN
    order = tuple(range(SCAN_RUN))[::-1] if reverse else tuple(range(SCAN_RUN))
    starts = tuple(range(0, n_rows, span))[::-1] if reverse else tuple(range(0, n_rows, span))
    carries = []
    for lt in range(a_ref.shape[0]):
        c_in = carry[:, lt * 128:(lt + 1) * 128]
        for start in starts:
            tile = lambda ref, g: ref[lt, pl.ds(row0 + start + g, 8, stride=SCAN_RUN), :]
            a = [tile(a_ref, g) for g in range(SCAN_RUN)]
            b = [tile(b_ref, g) for g in range(SCAN_RUN)]
            h = {order[0]: b[order[0]]}
            p = {order[0]: a[order[0]]}
            for prev, g in zip(order, order[1:]):
                h[g] = a[g] * h[prev] + b[g]
                p[g] = a[g] * p[prev]
            pi, hi = p[order[-1]], h[order[-1]]
            for s in (1, 2, 4):
                shift = 8 - s if reverse else s
                valid = (sub < 8 - s) if reverse else (sub >= s)
                pr, hr = pltpu.roll(pi, shift, 0), pltpu.roll(hi, shift, 0)
                hi = jnp.where(valid, pi * hr + hi, hi)
                pi = jnp.where(valid, pi * pr, pi)
            one = 7 if reverse else 1
            first = (sub == 7) if reverse else (sub == 0)
            pe = jnp.where(first, 1.0, pltpu.roll(pi, one, 0))
            he = jnp.where(first, 0.0, pltpu.roll(hi, one, 0))
            c = pe * c_in + he
            for g in range(SCAN_RUN):
                h_ref[lt, pl.ds(start + g, 8, stride=SCAN_RUN), :] = h[g] + p[g] * c
            last = 0 if reverse else 7
            c_in = pi[last:last + 1, :] * c_in + hi[last:last + 1, :]
        carries.append(c_in)
    return jnp.concatenate(carries, axis=1)


def _sigmoid(x):
    return 0.5 * jnp.tanh(0.5 * x) + 0.5


def _gelu_tanh(x):
    return x * (0.5 * (1.0 + jnp.tanh(math.sqrt(2.0 / math.pi) * (x + 0.044715 * (x * x * x)))))


def _lru_kernel(u_ref, h0_ref, cw_ref, cb_ref, wg_ref, bg_ref, lam_ref, y_ref, st_ref,
                xpad, a1s, b1s, a0c, b0c, hc, *, N, T):
    W = LRU_WIDTH
    nc = N // T
    tiles = [slice(lt * 128, (lt + 1) * 128) for lt in range(W // 128)]
    zeros = jnp.zeros((HALO, W), F32)
    xpad[0:HALO, :] = zeros
    xpad[N + HALO:N + 2 * HALO, :] = zeros

    def fill(j, carry):
        r0 = pl.multiple_of(j * T, T)
        xpad[pl.ds(r0 + HALO, T), :] = u_ref[pl.ds(r0, T), 0:W]
        return carry

    lax.fori_loop(0, nc, fill, 0)

    z = -lam_ref[...]
    sp = jnp.maximum(z, 0.0) + jnp.log1p(jnp.exp(-jnp.abs(z)))
    cw = cw_ref[...]
    cb = cb_ref[...]
    bg = bg_ref[...]

    def fwd(j, carry):
        r0 = pl.multiple_of(j * T, T)
        ext = xpad[pl.ds(r0, T + 2 * HALO), :]
        body = slice(HALO, HALO + T)
        xc = cb
        for tap in range(4):
            xc = xc + _shift_rows(ext, tap - 1)[body] * cw[tap:tap + 1]
        g = _sigmoid(_dot(xc.astype(BF16), wg_ref[...]) + bg)
        ab = []
        for d in range(2):
            r = g[:, d * W:(d + 1) * W]
            i = g[:, (2 + d) * W:(3 + d) * W]
            log_a = (-LRU_C * r) * sp[d:d + 1]
            a = jnp.exp(log_a)
            bt = (jnp.sqrt(1.0 - a * a) * i) * xc
            ab.append((a, bt))
        for lt, lanes in enumerate(tiles):
            a0c[lt] = ab[0][0][:, lanes]
            b0c[lt] = ab[0][1][:, lanes]
            a1s[lt, pl.ds(r0, T), :] = ab[1][0][:, lanes]
            b1s[lt, pl.ds(r0, T), :] = ab[1][1][:, lanes]
        carry = _scan_strided(a0c, b0c, hc, 0, carry, T, reverse=False)
        for lt, lanes in enumerate(tiles):
            y_ref[pl.ds(r0, T), lanes] = hc[lt]
        return carry

    cf = lax.fori_loop(0, nc, fwd, h0_ref[0, 0:1, :])

    def bwd(jj, carry):
        r0 = pl.multiple_of((nc - 1 - jj) * T, T)
        carry = _scan_strided(a1s, b1s, hc, r0, carry, T, reverse=True)
        for lt, lanes in enumerate(tiles):
            gb = u_ref[pl.ds(r0, T), W + lt * 128:W + (lt + 1) * 128]
            y_ref[pl.ds(r0, T), lanes] = (y_ref[pl.ds(r0, T), lanes] + hc[lt]) * _gelu_tanh(gb)
        return carry

    cbw = lax.fori_loop(0, nc, bwd, h0_ref[0, 1:2, :])
    st_ref[0, 0:1, :] = cf
    st_ref[0, 1:2, :] = cbw


def _lru(u, h0, h0_block, lw, *, nb, n):
    T = min(n, 256)
    W = LRU_WIDTH
    return pl.pallas_call(
        functools.partial(_lru_kernel, N=n, T=T),
        grid=(nb,),
        in_specs=[
            pl.BlockSpec((n, 2 * W), lambda b: (b, 0)),
            pl.BlockSpec((1, 2, W), lambda b: (h0_block(b), 0, 0)),
            lw.spec("conv_w"), lw.spec("conv_b"), lw.spec("w_gate"), lw.spec("b_gate"),
            lw.spec("lru_lambda"),
        ],
        out_specs=[
            pl.BlockSpec((n, W), lambda b: (b, 0)),
            pl.BlockSpec((1, 2, W), lambda b: (b, 0, 0)),
        ],
        out_shape=[
            jax.ShapeDtypeStruct((nb * n, W), F32),
            jax.ShapeDtypeStruct((nb, 2, W), F32),
        ],
        scratch_shapes=[
            pltpu.VMEM((n + 2 * HALO, W), F32),
            pltpu.VMEM((W // 128, n, 128), F32),
            pltpu.VMEM((W // 128, n, 128), F32),
            pltpu.VMEM((W // 128, T, 128), F32),
            pltpu.VMEM((W // 128, T, 128), F32),
            pltpu.VMEM((W // 128, T, 128), F32),
        ],
        compiler_params=_params("arbitrary"),
        name="rglru",
    )(u, h0, lw["conv_w"], lw["conv_b"], lw["w_gate"], lw["b_gate"], lw["lru_lambda"])


def _pool_kernel(u_ref, wp_ref, sc_ref, y_ref, xpad, *, N, T):
    W = GROUP_WIDTH
    nc = N // T
    zeros = jnp.zeros((HALO, W), F32)
    xpad[0:HALO, :] = zeros
    xpad[N + HALO:N + 2 * HALO, :] = zeros

    def fill(j, carry):
        r0 = pl.multiple_of(j * T, T)
        xpad[pl.ds(r0 + HALO, T), :] = u_ref[pl.ds(r0, T), :]
        return carry

    lax.fori_loop(0, nc, fill, 0)

    grp = lax.broadcasted_iota(jnp.int32, (1, W), 1) // POOL_CH
    half = jnp.where(grp == 0, 1, jnp.where(grp == 1, 2, jnp.where(grp == 2, 4, 8)))
    scale = sc_ref[...]

    def chunk(j, carry):
        r0 = pl.multiple_of(j * T, T)
        ext = xpad[pl.ds(r0, T + 2 * HALO), :]
        w2 = _shift_rows(ext, -1) + ext
        w4 = _shift_rows(w2, -1) + _shift_rows(w2, 1)
        w8 = _shift_rows(w4, -2) + _shift_rows(w4, 2)
        w16 = _shift_rows(w8, -4) + _shift_rows(w8, 4)
        ws = jnp.where(grp == 0, w2, jnp.where(grp == 1, w4, jnp.where(grp == 2, w8, w16)))
        body = slice(HALO, HALO + T)
        t = r0 + lax.broadcasted_iota(jnp.int32, (T, W), 0)
        cnt = (jnp.minimum(t + half, N) - jnp.maximum(t - half, 0)).astype(F32)
        d = ws[body] / cnt - ext[body]
        y_ref[pl.ds(r0, T), :] = _dot(d.astype(BF16), wp_ref[...]) * scale
        return carry

    lax.fori_loop(0, nc, chunk, 0)


def _pool(u, lw, *, nb, n):
    W = GROUP_WIDTH
    T = min(n, 256)
    return pl.pallas_call(
        functools.partial(_pool_kernel, N=n, T=T),
        grid=(nb,),
        in_specs=[pl.BlockSpec((n, W), lambda b: (b, 0)), lw.spec("w_pool"), lw.spec("pool_scale")],
        out_specs=pl.BlockSpec((n, W), lambda b: (b, 0)),
        out_shape=jax.ShapeDtypeStruct((nb * n, W), F32),
        scratch_shapes=[pltpu.VMEM((n + 2 * HALO, W), F32)],
        compiler_params=_params("arbitrary"),
        name="pool_mixer",
    )(u, lw["w_pool"], lw["pool_scale"])


def _mix_ffn_kernel(*refs, final):
    if final:
        (x_ref, ya_ref, yb_ref, yc_ref, yd_ref, mod_ref, g2_ref, wo_ref, wg_ref, wu_ref, wd_ref,
         gf_ref, o_ref) = refs
    else:
        (x_ref, ya_ref, yb_ref, yc_ref, yd_ref, mod_ref, g2_ref, wo_ref, wg_ref, wu_ref, wd_ref,
         o_ref) = refs
    mod = mod_ref[0]
    gate1 = mod[:, 2 * D_MODEL:3 * D_MODEL]
    sh2 = mod[:, 3 * D_MODEL:4 * D_MODEL]
    sc2 = mod[:, 4 * D_MODEL:5 * D_MODEL]
    gate2 = mod[:, 5 * D_MODEL:6 * D_MODEL]
    mix = None
    for i, y_ref in enumerate((ya_ref, yb_ref, yc_ref, yd_ref)):
        part = _dot(y_ref[...].astype(BF16), wo_ref[i * GROUP_WIDTH:(i + 1) * GROUP_WIDTH, :])
        mix = part if mix is None else mix + part
    x1 = x_ref[...] + gate1 * mix
    h = _rms_rows(x1, D_MODEL) * g2_ref[...]
    hb = (h * (1.0 + sc2) + sh2).astype(BF16)
    ff = None
    for lo, hi in FF_CHUNKS:
        g = _dot(hb, wg_ref[:, lo:hi])
        up = _dot(hb, wu_ref[:, lo:hi])
        act = ((g * jax.nn.sigmoid(g)) * up).astype(BF16)
        part = _dot(act, wd_ref[lo:hi, :])
        ff = part if ff is None else ff + part
    x2 = x1 + gate2 * ff
    if final:
        x2 = _rms_rows(x2, D_MODEL) * gf_ref[...]
    o_ref[...] = x2


def _mix_ffn(x, ys, mod, lw, gf, *, nb, n, final):
    T = nb * n
    tm = TOKEN_TILE
    npt = n // tm

    def tok(width):
        return pl.BlockSpec((tm, width), lambda i: (i, 0))

    in_specs = [tok(D_MODEL), tok(256), tok(256), tok(256), tok(256), mod.spec(lambda i: i // npt),
                lw.spec("g2"), lw.spec("w_out"), lw.spec("w_gu", col_blocks=2, col_block=0),
                lw.spec("w_gu", col_blocks=2, col_block=1), lw.spec("w_down")]
    args = [x, *ys, mod.table, lw["g2"], lw["w_out"], lw["w_gu"], lw["w_gu"], lw["w_down"]]
    if final:
        in_specs.append(_resident((1, D_MODEL)))
        args.append(gf)
    return pl.pallas_call(
        functools.partial(_mix_ffn_kernel, final=final),
        grid=(T // tm,),
        in_specs=in_specs,
        out_specs=tok(D_MODEL),
        out_shape=jax.ShapeDtypeStruct((T, D_MODEL), F32),
        compiler_params=_params("arbitrary"),
        name="mix_ffn_final" if final else "mix_ffn",
    )(*args)


def _block_diag(w):
    L, G, c, e = w.shape
    return jnp.einsum('lgce,gh->lgche', w, jnp.eye(G, dtype=w.dtype)).reshape(L, G * c, G * e)


def _rot_cols(w):
    return jnp.concatenate([-w[..., 16:32], w[..., 0:16]], axis=-1)


def _stack_weights(p):
    w_in = p["w_in"]
    o1 = MLA_Q_RANK
    o2 = o1 + MLA_KV_RANK
    o3 = o2 + MLA_ROPE
    c_q, c_kv, k_r, rest = w_in[..., :o1], w_in[..., o1:o2], w_in[..., o2:o3], w_in[..., o3:]
    z = lambda n: jnp.zeros((DEPTH, D_MODEL, n), F32)
    w_in_eff = jnp.concatenate([c_q, k_r, z(32), c_kv, z(64), _rot_cols(k_r), z(32), rest], axis=-1)

    w_uq = p["mla_w_uq"]
    qd = MLA_NOPE + MLA_ROPE
    wq_parts, wqr_parts = [], []
    zq = lambda n: jnp.zeros((DEPTH, MLA_Q_RANK, n), F32)
    for h in range(MLA_HEADS):
        wh = w_uq[..., h * qd:(h + 1) * qd]
        wq_parts += [wh, zq(MLA_SLOT - qd)]
        wqr_parts += [zq(MLA_NOPE), _rot_cols(wh[..., MLA_NOPE:]), zq(MLA_SLOT - qd)]
    pad_rows = lambda w: jnp.pad(w, ((0, 0), (0, 256 - MLA_Q_RANK), (0, 0)))
    w_ukv = p["mla_w_ukv"]
    wk_parts, wv_parts = [], []
    zk = jnp.zeros((DEPTH, MLA_KV_RANK, MLA_SLOT - MLA_NOPE), F32)
    for h in range(MLA_HEADS):
        base = h * (MLA_NOPE + MLA_V)
        wk_parts += [w_ukv[..., base:base + MLA_NOPE], zk]
        wv_parts.append(w_ukv[..., base + MLA_NOPE:base + MLA_NOPE + MLA_V])

    w_r, w_i, b_r, b_i = p["lru_w_r"], p["lru_w_i"], p["lru_b_r"], p["lru_b_i"]
    w_gate = jnp.concatenate([_block_diag(w_r[:, 0]), _block_diag(w_r[:, 1]),
                              _block_diag(w_i[:, 0]), _block_diag(w_i[:, 1])], axis=-1)
    b_gate = jnp.concatenate([b_r[:, 0], b_r[:, 1], b_i[:, 0], b_i[:, 1]], axis=-1)
    row = lambda v: v[:, None, :]
    return {
        "g1": row(p["norm1_g"]),
        "g2": row(p["norm2_g"]),
        "w_in": w_in_eff.astype(BF16),
        "gq": row(jnp.pad(p["mla_q_norm_g"], ((0, 0), (0, 256 - MLA_Q_RANK)))),
        "gkv": row(p["mla_kv_norm_g"]),
        "wq": pad_rows(jnp.concatenate(wq_parts, axis=-1)).astype(BF16),
        "wqr": pad_rows(jnp.concatenate(wqr_parts, axis=-1)).astype(BF16),
        "wkv": jnp.concatenate(wk_parts + wv_parts, axis=-1).astype(BF16),
        "conv_w": p["lru_conv_w"],
        "conv_b": row(p["lru_conv_b"]),
        "w_gate": w_gate.astype(BF16),
        "b_gate": row(b_gate),
        "lru_lambda": p["lru_lambda"],
        "w_pool": _block_diag(p["pool_w"]).astype(BF16),
        "pool_scale": row(p["pool_scale"]),
        "diff_lambda": p["diff_lambda"],
        "diff_g": row(jnp.tile(p["diff_norm_g"], (1, DIFF_HEADS))),
        "w_out": p["w_out"].astype(BF16),
        "w_gu": p["w_gu"].astype(BF16),
        "w_down": p["w_down"].astype(BF16),
    }


def _rope_tables(n, positional):
    quarter = MLA_ROPE // 4
    if positional:
        t = jnp.arange(n)
        row = (t // GRID_W).astype(F32)
        col = (t % GRID_W).astype(F32)
        inv = ROPE_BASE ** (-jnp.arange(quarter, dtype=F32) / quarter)
        ang = jnp.concatenate([row[:, None] * inv, col[:, None] * inv], axis=-1)
        cos, sin = jnp.cos(ang), jnp.sin(ang)
    else:
        cos, sin = jnp.ones((n, 16), F32), jnp.zeros((n, 16), F32)
    scale = LOG2E / math.sqrt(MLA_NOPE + MLA_ROPE)
    place = np.zeros((32, TAB_WIDTH), np.float32)
    offset = np.zeros((1, TAB_WIDTH), np.float32)
    offset[0, 0:64] = offset[0, 96:128] = scale
    for i in range(16):
        for half in (64, 80):
            place[i, half + i] = scale
            place[16 + i, 128 + half + i] = scale
            place[i, 256 + half + i] = 1.0
            place[16 + i, 384 + half + i] = 1.0
        for grp in range(8):
            place[i, 512 + 32 * grp + i] = place[i, 512 + 32 * grp + 16 + i] = 1.0
            place[16 + i, 768 + 32 * grp + i] = -1.0
            place[16 + i, 1024 + 32 * grp + 16 + i] = 1.0
    return jnp.dot(jnp.concatenate([cos, sin], axis=1), place, precision=lax.Precision.HIGHEST) + offset


def _layer(x, mod, lw, tabs, layer_idx, ctx, gf, *, nb, n, final):
    emit_cache = ctx is None
    tok_nb, tok_n = (1, nb * n) if mod.shared else (nb, n)
    outs = _inproj(x, mod, lw, tabs, nb=tok_nb, n=tok_n, emit_cache=emit_cache)
    q, k, vt, u_lru, u_pool, dq, dk, dvt = outs[:8]
    lam_init = 0.8 - 0.6 * math.exp(-0.3 * layer_idx)
    if ctx is None:
        h0 = jnp.zeros((1, 2, LRU_WIDTH), F32)
        h0_block = lambda b: 0
        mla_ctx = diff_ctx = None
    else:
        ckv, kr_pad, cdk, cdv, h0 = ctx
        p = ckv.shape[0] // (nb * DEPTH)
        h0_block = lambda b: b * DEPTH + layer_idx
        kc, vtc, dkc, dvtc = _ctx_prep(ckv, kr_pad, cdk, cdv, lw, nb=nb, p=p)
        mla_ctx = (kc, vtc)
        diff_ctx = (dkc, dvtc)
    y_mla = _mla_attn(q, k, vt, mla_ctx, nb=nb, n=n)
    y_lru, st = _lru(u_lru, h0, h0_block, lw, nb=nb, n=n)
    y_pool = _pool(u_pool, lw, nb=nb, n=n)
    y_diff = _diff_attn(dq, dk, dvt, diff_ctx, lw, nb=nb, n=n, lam_init=lam_init)
    x2 = _mix_ffn(x, (y_mla, y_lru, y_pool, y_diff), mod, lw, gf, nb=tok_nb, n=tok_n, final=final)
    cache = (outs[8], outs[9][:, 64:96], outs[10], outs[11], st) if emit_cache else None
    return x2, cache


def kernel(x_prompt, x_sample, cache_mla_ckv, cache_mla_krope, cache_diff_k, cache_diff_v, state_lru,
           c, c_ctx, w_ada, b_ada, norm1_g, norm2_g, w_in, mla_q_norm_g, mla_w_uq, mla_kv_norm_g,
           mla_w_ukv, lru_conv_w, lru_conv_b, lru_w_r, lru_b_r, lru_w_i, lru_b_i, lru_lambda, pool_w,
           pool_scale, diff_lambda, diff_norm_g, w_out, w_gu, w_down, final_norm_g):
    p = {
        "norm1_g": norm1_g, "norm2_g": norm2_g, "w_in": w_in, "mla_q_norm_g": mla_q_norm_g,
        "mla_w_uq": mla_w_uq, "mla_kv_norm_g": mla_kv_norm_g, "mla_w_ukv": mla_w_ukv,
        "lru_conv_w": lru_conv_w, "lru_conv_b": lru_conv_b, "lru_w_r": lru_w_r, "lru_b_r": lru_b_r,
        "lru_w_i": lru_w_i, "lru_b_i": lru_b_i, "lru_lambda": lru_lambda, "pool_w": pool_w,
        "pool_scale": pool_scale, "diff_lambda": diff_lambda, "diff_norm_g": diff_norm_g,
        "w_out": w_out, "w_gu": w_gu, "w_down": w_down,
    }
    Bp, Np, _ = x_prompt.shape
    Bs, Ns, _ = x_sample.shape
    P = cache_mla_ckv.shape[2]

    cond_all = jnp.concatenate([c, c_ctx[None, :], jnp.zeros((MOD_ROWS - Bs - 1, D_MODEL), F32)], axis=0)
    mod_table = _ada(cond_all, w_ada, b_ada).reshape(DEPTH * MOD_ROWS, 1, 6 * D_MODEL)
    tabs_p = _rope_tables(Bp * Np, positional=False)
    tabs_s = _rope_tables(Ns, positional=True)
    kr_pad = jnp.pad(cache_mla_krope, ((0, 0), (0, 0), (0, 0), (MLA_NOPE, MLA_SLOT - MLA_NOPE - MLA_ROPE)))
    flat = lambda a, w: a.reshape(Bs * DEPTH * P, w)
    ctx = (flat(cache_mla_ckv, MLA_KV_RANK), flat(kr_pad, MLA_SLOT), flat(cache_diff_k, 256),
           flat(cache_diff_v, 256), state_lru.reshape(Bs * DEPTH, 2, LRU_WIDTH))
    gf = final_norm_g[None, :]
    stacked = _stack_weights(p)

    xp = x_prompt.reshape(Bp * Np, D_MODEL)
    xs = x_sample.reshape(Bs * Ns, D_MODEL)
    caches = []
    for l in range(DEPTH):
        lw = _LayerWeights(stacked, l)
        final = l == DEPTH - 1
        mod_p = _Mod(mod_table, l * MOD_ROWS + Bs, shared=True)
        mod_s = _Mod(mod_table, l * MOD_ROWS, shared=False)
        xp, cache = _layer(xp, mod_p, lw, tabs_p, l, None, gf, nb=Bp, n=Np, final=final)
        caches.append(cache)
        xs, _ = _layer(xs, mod_s, lw, tabs_s, l, ctx, gf, nb=Bs, n=Ns, final=final)

    stack = lambda i, w: jnp.stack([cc[i].reshape(Bp, Np, w) for cc in caches], axis=1)
    new_mla_ckv = stack(0, MLA_KV_RANK)
    new_mla_krope = stack(1, MLA_ROPE)
    new_diff_k = stack(2, 256).reshape(Bp, DEPTH, Np, DIFF_HEADS, 2, DIFF_DIM)
    new_diff_v = stack(3, 256).reshape(Bp, DEPTH, Np, DIFF_HEADS, 2 * DIFF_DIM)
    new_state_lru = jnp.stack([cc[4] for cc in caches], axis=1)
    return (xp.reshape(Bp, Np, D_MODEL), xs.reshape(Bs, Ns, D_MODEL),
            new_mla_ckv, new_mla_krope, new_diff_k, new_diff_v, new_state_lru)
```

```python
import functools
import math

import jax
import jax.numpy as jnp
import numpy as np
from jax import lax
from jax.experimental import pallas as pl
from jax.experimental.pallas import tpu as pltpu

F32 = jnp.float32
BF16 = jnp.bfloat16

D_MODEL = 1024
DEPTH = 2
GRID_W = 64
GROUP_WIDTH = 256
MLA_HEADS = 4
MLA_NOPE = 64
MLA_ROPE = 32
MLA_V = 64
MLA_Q_RANK = 192
MLA_KV_RANK = 128
MLA_SLOT = 128
LRU_WIDTH = 256
LRU_C = 8.0
POOL_WINDOWS = (2, 4, 8, 16)
POOL_CH = 64
DIFF_HEADS = 4
DIFF_DIM = 32
HEAD_V = 64
FF_HIDDEN = 2816
FF_CHUNKS = ((0, 1536), (1536, 2816))
ROPE_BASE = 10000.0
EPS = 1e-6
IN_EFF = 2048
HALO = 8
SCAN_RUN = 4
VT_ROWS = 80
ATT_TQ = 256
TOKEN_TILE = 512
TAB_WIDTH = 4 * 128 + 3 * 256
MOD_ROWS = 16
LOG2E = math.log2(math.e)

VMEM_LIMIT_BYTES = 56 * 1024 * 1024

_NT = (((1,), (1,)), ((), ()))


def _params(*sem):
    return pltpu.CompilerParams(dimension_semantics=sem, vmem_limit_bytes=VMEM_LIMIT_BYTES)


def _resident(shape):
    zeros = (0,) * len(shape)
    return pl.BlockSpec(shape, lambda *_: zeros, pipeline_mode=pl.Buffered(1))


def _dot(a, b):
    return jnp.dot(a, b, preferred_element_type=F32)


def _dot_nt(a, b):
    return lax.dot_general(a, b, _NT, preferred_element_type=F32)


def _rms_rows(x, width):
    ms = jnp.sum(x * x, axis=-1, keepdims=True) * (1.0 / width)
    return x * lax.rsqrt(ms + EPS)


def _store_vt(vt_ref, v):
    vt = v.T
    rows = v.shape[0]
    pad = VT_ROWS - HEAD_V
    ones_row = jnp.where(lax.broadcasted_iota(jnp.int32, (pad, rows), 0) == 0, 1.0, 0.0).astype(BF16)
    for hh in range(vt_ref.shape[0]):
        vt_ref[hh, 0:HEAD_V, :] = vt[hh * HEAD_V:(hh + 1) * HEAD_V, :].astype(BF16)
        vt_ref[hh, HEAD_V:VT_ROWS, :] = ones_row


class _Mod:
    def __init__(self, table, row0, shared):
        self.table, self.row0, self.shared = table, row0, shared

    def spec(self, batch_of):
        row0 = self.row0
        if self.shared:
            return pl.BlockSpec((1, 1, 6 * D_MODEL), lambda *g: (row0, 0, 0))
        return pl.BlockSpec((1, 1, 6 * D_MODEL), lambda *g: (row0 + batch_of(*g), 0, 0))


class _LayerWeights:
    def __init__(self, stacked, layer):
        self.stacked, self.layer = stacked, layer

    def __getitem__(self, name):
        return self.stacked[name]

    def spec(self, name, col_blocks=1, col_block=0):
        layer = self.layer
        _, rows, cols = self.stacked[name].shape
        return pl.BlockSpec((None, rows, cols // col_blocks), lambda *_: (layer, 0, col_block),
                            pipeline_mode=pl.Buffered(1))


def _ada_kernel(cond_ref, w_ref, b_ref, out_ref):
    c = cond_ref[...]
    s = c * jax.nn.sigmoid(c)
    out_ref[0] = _dot(s.astype(BF16), w_ref[0].astype(BF16)) + b_ref[0]


def _ada(cond_all, w_ada, b_ada):
    rows = cond_all.shape[0]
    tn = 1536
    return pl.pallas_call(
        _ada_kernel,
        grid=(DEPTH, 6 * D_MODEL // tn),
        in_specs=[
            pl.BlockSpec((rows, D_MODEL), lambda l, j: (0, 0)),
            pl.BlockSpec((1, D_MODEL, tn), lambda l, j: (l, 0, j)),
            pl.BlockSpec((1, 1, tn), lambda l, j: (l, 0, j)),
        ],
        out_specs=pl.BlockSpec((1, rows, tn), lambda l, j: (l, 0, j)),
        out_shape=jax.ShapeDtypeStruct((DEPTH, rows, 6 * D_MODEL), F32),
        compiler_params=_params("arbitrary", "arbitrary"),
        name="ada_mod",
    )(cond_all, w_ada, b_ada.reshape(DEPTH, 1, 6 * D_MODEL))


def _inproj_kernel(x_ref, mod_ref, g1_ref, win_ref, gq_ref, gkv_ref, wq_ref, wqr_ref, wkv_ref, tab_ref,
                   q_out, k_out, vt_out, lru_out, pool_out, dq_out, dk_out, dvt_out, *cache_outs):
    cosq_ref, sinq_ref, cosk_ref, sink_ref = (tab_ref.at[:, i * 128:(i + 1) * 128] for i in range(4))
    cosd_ref, sina_ref, sinb_ref = (tab_ref.at[:, 512 + i * 256:768 + i * 256] for i in range(3))
    x = x_ref[...]
    mod = mod_ref[0]
    sh1 = mod[:, 0:D_MODEL]
    sc1 = mod[:, D_MODEL:2 * D_MODEL]
    h = _rms_rows(x, D_MODEL) * g1_ref[...]
    hb = (h * (1.0 + sc1) + sh1).astype(BF16)

    u_mla = _dot(hb, win_ref[:, 0:512])
    u_pd = _dot(hb, win_ref[:, 1024:1536])
    u_kv = _dot(hb, win_ref[:, 1536:2048])
    t01 = u_mla[:, 0:256]
    lane = lax.broadcasted_iota(jnp.int32, (1, 256), 1)
    cq = jnp.where(lane < MLA_Q_RANK, t01, 0.0)
    cqn = (_rms_rows(cq, MLA_Q_RANK) * gq_ref[...]).astype(BF16)
    qa = _dot(cqn, wq_ref[...])
    qr = _dot(cqn, wqr_ref[...])
    cosq = cosq_ref[...]
    sinq = sinq_ref[...]
    ckv = u_mla[:, 256:384]
    lat = _rms_rows(ckv, MLA_KV_RANK) * gkv_ref[...]
    latb = lat.astype(BF16)
    kkv = _dot(latb, wkv_ref[...])
    kk = kkv[:, 0:MLA_HEADS * MLA_SLOT]
    _store_vt(vt_out, kkv[:, MLA_HEADS * MLA_SLOT:])
    t1 = t01[:, 128:256]
    t3 = u_mla[:, 384:512]
    kro = t1 * cosk_ref[...] + t3 * sink_ref[...]
    for hh in range(MLA_HEADS):
        sl = slice(hh * MLA_SLOT, (hh + 1) * MLA_SLOT)
        q_out[hh] = (qa[:, sl] * cosq + qr[:, sl] * sinq).astype(q_out.dtype)
        k_out[hh] = (kk[:, sl] + kro).astype(k_out.dtype)

    lru_out[...] = _dot(hb, win_ref[:, 512:1024])
    pool_out[...] = u_pd[:, 0:256]

    cosd = cosd_ref[...]
    sina = sina_ref[...]
    sinb = sinb_ref[...]

    def rope(t):
        return t * cosd + pltpu.roll(t, 256 - 16, 1) * sina + pltpu.roll(t, 16, 1) * sinb

    dq = u_pd[:, 256:512]
    dk = u_kv[:, 0:256]
    dv = u_kv[:, 256:512]
    dq_out[...] = (rope(dq) * (LOG2E / math.sqrt(DIFF_DIM))).astype(dq_out.dtype)
    dk_out[...] = rope(dk).astype(dk_out.dtype)
    _store_vt(dvt_out, dv)

    if cache_outs:
        lat_out, kr_out, dk_raw_out, dv_raw_out = cache_outs
        lat_out[...] = lat
        kr_out[...] = t1
        dk_raw_out[...] = dk
        dv_raw_out[...] = dv


def _inproj(x, mod, lw, tabs, *, nb, n, emit_cache):
    T = nb * n
    tm = TOKEN_TILE
    npt = n // tm
    row_blk = lambda j, b: b * npt + j

    def tok(width):
        return pl.BlockSpec((tm, width), lambda j, b: (row_blk(j, b), 0))

    def tab(width):
        return pl.BlockSpec((tm, width), lambda j, b: (j, 0))

    head = pl.BlockSpec((MLA_HEADS, tm, MLA_SLOT), lambda j, b: (0, row_blk(j, b), 0))
    vt_spec = pl.BlockSpec((MLA_HEADS, VT_ROWS, tm), lambda j, b: (0, 0, row_blk(j, b)))
    wnames = ("g1", "w_in", "gq", "gkv", "wq", "wqr", "wkv")
    in_specs = [tok(D_MODEL), mod.spec(lambda j, b: b)] + [lw.spec(nm) for nm in wnames] + [
        tab(TAB_WIDTH)]
    out_specs = [head, head, vt_spec, tok(512), tok(256), tok(256), tok(256), vt_spec]
    vt_shape = jax.ShapeDtypeStruct((MLA_HEADS, VT_ROWS, T), BF16)
    out_shape = [
        jax.ShapeDtypeStruct((MLA_HEADS, T, MLA_SLOT), BF16),
        jax.ShapeDtypeStruct((MLA_HEADS, T, MLA_SLOT), BF16),
        vt_shape,
        jax.ShapeDtypeStruct((T, 512), F32),
        jax.ShapeDtypeStruct((T, 256), F32),
        jax.ShapeDtypeStruct((T, 256), BF16),
        jax.ShapeDtypeStruct((T, 256), BF16),
        vt_shape,
    ]
    if emit_cache:
        out_specs += [tok(128), tok(128), tok(256), tok(256)]
        out_shape += [jax.ShapeDtypeStruct((T, 128), F32), jax.ShapeDtypeStruct((T, 128), F32),
                      jax.ShapeDtypeStruct((T, 256), F32), jax.ShapeDtypeStruct((T, 256), F32)]
    return pl.pallas_call(
        _inproj_kernel,
        grid=(npt, nb),
        in_specs=in_specs,
        out_specs=out_specs,
        out_shape=out_shape,
        compiler_params=_params("arbitrary", "arbitrary"),
        name="inproj_cache" if emit_cache else "inproj",
    )(x, mod.table, *[lw[nm] for nm in wnames], tabs)


def _ctx_prep_kernel(ckv_ref, kr_ref, dk_ref, dv_ref, wkv_ref, k_out, vt_out, dk_out, dvt_out):
    latb = ckv_ref[...].astype(BF16)
    kkv = _dot(latb, wkv_ref[...])
    kk = kkv[:, 0:MLA_HEADS * MLA_SLOT]
    kr = kr_ref[...]
    for hh in range(MLA_HEADS):
        k_out[hh] = (kk[:, hh * MLA_SLOT:(hh + 1) * MLA_SLOT] + kr).astype(k_out.dtype)
    _store_vt(vt_out, kkv[:, MLA_HEADS * MLA_SLOT:])
    dk_out[...] = dk_ref[...].astype(dk_out.dtype)
    _store_vt(dvt_out, dv_ref[...])


def _ctx_prep(ckv, kr_pad, cdk, cdv, lw, *, nb, p):
    T = nb * p
    layer = lw.layer
    cache_row = lambda w: pl.BlockSpec((p, w), lambda b: (b * DEPTH + layer, 0))
    row = lambda w: pl.BlockSpec((p, w), lambda b: (b, 0))
    vt_spec = pl.BlockSpec((MLA_HEADS, VT_ROWS, p), lambda b: (0, 0, b))
    vt_shape = jax.ShapeDtypeStruct((MLA_HEADS, VT_ROWS, T), BF16)
    return pl.pallas_call(
        _ctx_prep_kernel,
        grid=(nb,),
        in_specs=[cache_row(128), cache_row(128), cache_row(256), cache_row(256),
                  lw.spec("wkv")],
        out_specs=[pl.BlockSpec((MLA_HEADS, p, MLA_SLOT), lambda b: (0, b, 0)), vt_spec, row(256), vt_spec],
        out_shape=[jax.ShapeDtypeStruct((MLA_HEADS, T, MLA_SLOT), BF16), vt_shape,
                   jax.ShapeDtypeStruct((T, 256), BF16), vt_shape],
        compiler_params=_params("arbitrary"),
        name="ctx_prep",
    )(ckv, kr_pad, cdk, cdv, lw["wkv"])


SAFE_DENOM = 2.0 ** -60
E_BUFS = 3
BOUND_SLACK = 1.02


def _scores(k_new, k_ctx, q):
    sn = _dot_nt(k_new(), q)
    sc = _dot_nt(k_ctx(), q) if k_ctx is not None else None
    return sn, sc


def _exact_shift(k_new, k_ctx, q):
    sn, sc = _scores(k_new, k_ctx, q)
    m = jnp.max(sn, axis=0, keepdims=True)
    if sc is not None:
        m = jnp.maximum(m, jnp.max(sc, axis=0, keepdims=True))
    return m


def _bound_shift(q, key_norm2):
    qf = q.astype(F32)
    ones = jnp.ones((8, q.shape[1]), BF16)
    q_norm2 = _dot_nt(ones, (qf * qf).astype(BF16))[0:1, :]
    return jnp.sqrt(q_norm2 * key_norm2) * BOUND_SLACK


def _max_row_norm2(k_new, k_ctx, col_sum):
    def one(k):
        kf = k.astype(F32)
        return jnp.max(_dot((kf * kf).astype(BF16), col_sum), axis=0, keepdims=True)
    m = one(k_new)
    if k_ctx is not None:
        m = jnp.maximum(m, one(k_ctx))
    return m * BOUND_SLACK


def _exp_stage(e_buf, k_new, k_ctx, q, shift, n_ctx):
    sn, sc = _scores(k_new, k_ctx, q)
    e_buf[n_ctx:, :] = jnp.exp2(sn - shift).astype(BF16)
    if sc is not None:
        e_buf[0:n_ctx, :] = jnp.exp2(sc - shift).astype(BF16)


def _value_stage(e_buf, vt_new, vt_ctx, n_ctx):
    o = _dot(vt_new(), e_buf[n_ctx:, :])
    if vt_ctx is not None:
        o = o + _dot(vt_ctx(), e_buf[0:n_ctx, :])
    return o


def _run_pipeline(n_maps, exp_stage, value_stage):
    ahead = E_BUFS - 1
    for u in range(min(ahead, n_maps)):
        exp_stage(u)
    for u in range(n_maps):
        value_stage(u)
        if u + ahead < n_maps:
            exp_stage(u + ahead)


def _att_scratch(nk, key_shape):
    scratch = [pltpu.VMEM((8, 128), F32),
               pltpu.VMEM((MLA_HEADS * HEAD_V, ATT_TQ), F32)]
    if key_shape is not None:
        scratch += [pltpu.VMEM(key_shape, BF16), pltpu.VMEM((MLA_HEADS, VT_ROWS, nk), BF16)]
    return scratch + [pltpu.VMEM((nk, ATT_TQ), BF16)] * E_BUFS


def _att_nsub(n):
    return 2 if n % (2 * ATT_TQ) == 0 else 1


def _mla_attn_kernel(*refs, has_ctx, nsub):
    if has_ctx:
        q_ref, k_ref, vt_ref, kc_ref, vtc_ref, o_ref, kn2, ot, keys, vals, *e_bufs = refs
        n_ctx = kc_ref.shape[1]
    else:
        q_ref, k_ref, vt_ref, o_ref, kn2, ot, *e_bufs = refs
        keys, vals = k_ref, vt_ref

    @pl.when(pl.program_id(1) == 0)
    def _():
        ones = jnp.ones((MLA_SLOT, 128), BF16)
        for hh in range(MLA_HEADS):
            kn2[hh:hh + 1, :] = _max_row_norm2(k_ref[hh], kc_ref[hh] if has_ctx else None, ones)
        if has_ctx:
            keys[:, 0:n_ctx, :] = kc_ref[...]
            keys[:, n_ctx:, :] = k_ref[...]
            vals[:, :, 0:n_ctx] = vtc_ref[...]
            vals[:, :, n_ctx:] = vt_ref[...]

    def run(exact):
        denoms = []

        def exp_stage(u):
            t, hh = divmod(u, MLA_HEADS)
            q = q_ref[hh, t * ATT_TQ:(t + 1) * ATT_TQ, :]
            k_all = lambda: keys[hh]
            shift = _exact_shift(k_all, None, q) if exact else _bound_shift(q, kn2[hh:hh + 1, 0:1])
            _exp_stage(e_bufs[u % E_BUFS], k_all, None, q, shift, 0)

        def value_stage(u):
            t, hh = divmod(u, MLA_HEADS)
            o = _value_stage(e_bufs[u % E_BUFS], lambda: vals[hh], None, 0)
            denom = o[HEAD_V:HEAD_V + 1, :]
            denoms.append(denom)
            ot[hh * HEAD_V:(hh + 1) * HEAD_V, :] = o[0:HEAD_V, :] * (1.0 / denom)
            if hh == MLA_HEADS - 1:
                o_ref[t * ATT_TQ:(t + 1) * ATT_TQ, :] = ot[...].T

        _run_pipeline(nsub * MLA_HEADS, exp_stage, value_stage)
        return jnp.min(functools.reduce(jnp.minimum, denoms))

    denom_min = run(exact=False)

    @pl.when(jnp.logical_not(denom_min >= SAFE_DENOM))
    def _():
        run(exact=True)


def _mla_attn(q, k, vt, ctx, *, nb, n):
    nsub = _att_nsub(n)
    tq = nsub * ATT_TQ
    npt = n // tq
    H, S = MLA_HEADS, MLA_SLOT
    in_specs = [
        pl.BlockSpec((H, tq, S), lambda b, j: (0, b * npt + j, 0)),
        pl.BlockSpec((H, n, S), lambda b, j: (0, b, 0)),
        pl.BlockSpec((H, VT_ROWS, n), lambda b, j: (0, 0, b)),
    ]
    args = [q, k, vt]
    n_ctx = 0
    if ctx is not None:
        n_ctx = ctx[0].shape[1] // nb
        in_specs += [
            pl.BlockSpec((H, n_ctx, S), lambda b, j: (0, b, 0)),
            pl.BlockSpec((H, VT_ROWS, n_ctx), lambda b, j: (0, 0, b)),
        ]
        args += list(ctx)
    return pl.pallas_call(
        functools.partial(_mla_attn_kernel, has_ctx=ctx is not None, nsub=nsub),
        grid=(nb, npt),
        in_specs=in_specs,
        out_specs=pl.BlockSpec((tq, 256), lambda b, j: (b * npt + j, 0)),
        out_shape=jax.ShapeDtypeStruct((nb * n, 256), F32),
        scratch_shapes=_att_scratch(n + n_ctx, (H, n + n_ctx, S) if ctx is not None else None),
        compiler_params=_params("arbitrary", "arbitrary"),
        name="mla_attn_ctx" if ctx is not None else "mla_attn",
    )(*args)


def _diff_attn_kernel(*refs, has_ctx, nsub, lam_init):
    if has_ctx:
        lv_ref, g_ref, q_ref, k_ref, vt_ref, kc_ref, vtc_ref, o_ref, kn2, ot, keys, vals, *e_bufs = refs
        n_ctx = kc_ref.shape[0]
    else:
        lv_ref, g_ref, q_ref, k_ref, vt_ref, o_ref, kn2, ot, *e_bufs = refs
        keys, vals = k_ref, vt_ref
    lv = lv_ref[...]
    lam = (jnp.exp(jnp.sum(lv[0:1] * lv[1:2], axis=-1, keepdims=True))
           - jnp.exp(jnp.sum(lv[2:3] * lv[3:4], axis=-1, keepdims=True)) + lam_init)
    lane128 = lax.broadcasted_iota(jnp.int32, (1, 128), 1)
    n_pairs = 2 * DIFF_HEADS

    @pl.when(pl.program_id(1) == 0)
    def _():
        dim = lax.broadcasted_iota(jnp.int32, (256, 128), 0)
        col = lax.broadcasted_iota(jnp.int32, (256, 128), 1)
        indicator = jnp.where(dim // DIFF_DIM == col, 1.0, 0.0).astype(BF16)
        kn2[0:1, :] = _max_row_norm2(k_ref[...], kc_ref[...] if has_ctx else None, indicator)
        if has_ctx:
            keys[0:n_ctx, :] = kc_ref[...]
            keys[n_ctx:, :] = k_ref[...]
            vals[:, :, 0:n_ctx] = vtc_ref[...]
            vals[:, :, n_ctx:] = vt_ref[...]

    def run(exact):
        denoms = []
        outs = {}

        def exp_stage(u):
            t, p = divmod(u, n_pairs)
            tile = slice((p * DIFF_DIM // 128) * 128, (p * DIFF_DIM // 128 + 1) * 128)
            k_new = lambda: keys[:, tile]
            k_ctx = None
            q = q_ref[t * ATT_TQ:(t + 1) * ATT_TQ, tile]
            lo = p * DIFF_DIM - tile.start
            in_pair = (lane128 >= lo) & (lane128 < lo + DIFF_DIM)
            qm = jnp.where(in_pair, q, jnp.zeros_like(q))
            shift = _exact_shift(k_new, k_ctx, qm) if exact else _bound_shift(qm, kn2[0:1, p:p + 1])
            _exp_stage(e_bufs[u % E_BUFS], k_new, k_ctx, qm, shift, 0)

        def value_stage(u):
            t, p = divmod(u, n_pairs)
            hh = p // 2
            o = _value_stage(e_bufs[u % E_BUFS], lambda: vals[hh], None, 0)
            denom = o[HEAD_V:HEAD_V + 1, :]
            denoms.append(denom)
            outs[u] = (o[0:HEAD_V, :], denom)
            if p % 2 == 1:
                (o0, l0), (o1, l1) = outs.pop(u - 1), outs.pop(u)
                o = o0 * (1.0 / l0) - o1 * (lam / l1)
                msq = jnp.sum(o * o, axis=0, keepdims=True) * (1.0 / HEAD_V)
                ot[hh * HEAD_V:(hh + 1) * HEAD_V, :] = o * lax.rsqrt(msq + EPS)
            if p == n_pairs - 1:
                o_ref[t * ATT_TQ:(t + 1) * ATT_TQ, :] = (ot[...].T * g_ref[...]) * (1.0 - lam_init)

        _run_pipeline(nsub * n_pairs, exp_stage, value_stage)
        return jnp.min(functools.reduce(jnp.minimum, denoms))

    denom_min = run(exact=False)

    @pl.when(jnp.logical_not(denom_min >= SAFE_DENOM))
    def _():
        run(exact=True)


def _diff_attn(q, k, vt, ctx, lw, *, nb, n, lam_init):
    nsub = 1
    tq = nsub * ATT_TQ
    npt = n // tq
    in_specs = [
        lw.spec("diff_lambda"),
        lw.spec("diff_g"),
        pl.BlockSpec((tq, 256), lambda b, j: (b * npt + j, 0)),
        pl.BlockSpec((n, 256), lambda b, j: (b, 0)),
        pl.BlockSpec((DIFF_HEADS, VT_ROWS, n), lambda b, j: (0, 0, b)),
    ]
    args = [lw["diff_lambda"], lw["diff_g"], q, k, vt]
    n_ctx = 0
    if ctx is not None:
        n_ctx = ctx[0].shape[0] // nb
        in_specs += [pl.BlockSpec((n_ctx, 256), lambda b, j: (b, 0)),
                     pl.BlockSpec((DIFF_HEADS, VT_ROWS, n_ctx), lambda b, j: (0, 0, b))]
        args += list(ctx)
    return pl.pallas_call(
        functools.partial(_diff_attn_kernel, has_ctx=ctx is not None, nsub=nsub, lam_init=lam_init),
        grid=(nb, npt),
        in_specs=in_specs,
        out_specs=pl.BlockSpec((tq, 256), lambda b, j: (b * npt + j, 0)),
        out_shape=jax.ShapeDtypeStruct((nb * n, 256), F32),
        scratch_shapes=_att_scratch(n + n_ctx, (n + n_ctx, 256) if ctx is not None else None),
        compiler_params=_params("arbitrary", "arbitrary"),
        name="diff_attn_ctx" if ctx is not None else "diff_attn",
    )(*args)


def _shift_rows(v, k):
    return pltpu.roll(v, (-k) % v.shape[0], 0)


def _scan_strided(a_ref, b_ref, h_ref, row0, carry, n_rows, reverse):
    sub = lax.broadcasted_iota(jnp.int32, (8, 128), 0)
    span = 8 * SCAN_RUN
    order = tuple(range(SCAN_RUN))[::-1] if reverse else tuple(range(SCAN_RUN))
    starts = tuple(range(0, n_rows, span))[::-1] if reverse else tuple(range(0, n_rows, span))
    carries = []
    for lt in range(a_ref.shape[0]):
        c_in = carry[:, lt * 128:(lt + 1) * 128]
        for start in starts:
            tile = lambda ref, g: ref[lt, pl.ds(row0 + start + g, 8, stride=SCAN_RUN), :]
            a = [tile(a_ref, g) for g in range(SCAN_RUN)]
            b = [tile(b_ref, g) for g in range(SCAN_RUN)]
            h = {order[0]: b[order[0]]}
            p = {order[0]: a[order[0]]}
            for prev, g in zip(order, order[1:]):
                h[g] = a[g] * h[prev] + b[g]
                p[g] = a[g] * p[prev]
            pi, hi = p[order[-1]], h[order[-1]]
            for s in (1, 2, 4):
                shift = 8 - s if reverse else s
                valid = (sub < 8 - s) if reverse else (sub >= s)
                pr, hr = pltpu.roll(pi, shift, 0), pltpu.roll(hi, shift, 0)
                hi = jnp.where(valid, pi * hr + hi, hi)
                pi = jnp.where(valid, pi * pr, pi)
            one = 7 if reverse else 1
            first = (sub == 7) if reverse else (sub == 0)
            pe = jnp.where(first, 1.0, pltpu.roll(pi, one, 0))
            he = jnp.where(first, 0.0, pltpu.roll(hi, one, 0))
            c = pe * c_in + he
            for g in range(SCAN_RUN):
                h_ref[lt, pl.ds(start + g, 8, stride=SCAN_RUN), :] = h[g] + p[g] * c
            last = 0 if reverse else 7
            c_in = pi[last:last + 1, :] * c_in + hi[last:last + 1, :]
        carries.append(c_in)
    return jnp.concatenate(carries, axis=1)


def _sigmoid(x):
    return 0.5 * jnp.tanh(0.5 * x) + 0.5


def _gelu_tanh(x):
    return x * (0.5 * (1.0 + jnp.tanh(math.sqrt(2.0 / math.pi) * (x + 0.044715 * (x * x * x)))))


def _lru_kernel(u_ref, h0_ref, cw_ref, cb_ref, wg_ref, bg_ref, lam_ref, y_ref, st_ref,
                xpad, a1s, b1s, a0c, b0c, hc, *, N, T):
    W = LRU_WIDTH
    nc = N // T
    tiles = [slice(lt * 128, (lt + 1) * 128) for lt in range(W // 128)]
    zeros = jnp.zeros((HALO, W), F32)
    xpad[0:HALO, :] = zeros
    xpad[N + HALO:N + 2 * HALO, :] = zeros

    def fill(j, carry):
        r0 = pl.multiple_of(j * T, T)
        xpad[pl.ds(r0 + HALO, T), :] = u_ref[pl.ds(r0, T), 0:W]
        return carry

    lax.fori_loop(0, nc, fill, 0)

    z = -lam_ref[...]
    sp = jnp.maximum(z, 0.0) + jnp.log1p(jnp.exp(-jnp.abs(z)))
    cw = cw_ref[...]
    cb = cb_ref[...]
    bg = bg_ref[...]

    def fwd(j, carry):
        r0 = pl.multiple_of(j * T, T)
        ext = xpad[pl.ds(r0, T + 2 * HALO), :]
        body = slice(HALO, HALO + T)
        xc = cb
        for tap in range(4):
            xc = xc + _shift_rows(ext, tap - 1)[body] * cw[tap:tap + 1]
        g = _sigmoid(_dot(xc.astype(BF16), wg_ref[...]) + bg)
        ab = []
        for d in range(2):
            r = g[:, d * W:(d + 1) * W]
            i = g[:, (2 + d) * W:(3 + d) * W]
            log_a = (-LRU_C * r) * sp[d:d + 1]
            a = jnp.exp(log_a)
            bt = (jnp.sqrt(1.0 - a * a) * i) * xc
            ab.append((a, bt))
        for lt, lanes in enumerate(tiles):
            a0c[lt] = ab[0][0][:, lanes]
            b0c[lt] = ab[0][1][:, lanes]
            a1s[lt, pl.ds(r0, T), :] = ab[1][0][:, lanes]
            b1s[lt, pl.ds(r0, T), :] = ab[1][1][:, lanes]
        carry = _scan_strided(a0c, b0c, hc, 0, carry, T, reverse=False)
        for lt, lanes in enumerate(tiles):
            y_ref[pl.ds(r0, T), lanes] = hc[lt]
        return carry

    cf = lax.fori_loop(0, nc, fwd, h0_ref[0, 0:1, :])

    def bwd(jj, carry):
        r0 = pl.multiple_of((nc - 1 - jj) * T, T)
        carry = _scan_strided(a1s, b1s, hc, r0, carry, T, reverse=True)
        for lt, lanes in enumerate(tiles):
            gb = u_ref[pl.ds(r0, T), W + lt * 128:W + (lt + 1) * 128]
            y_ref[pl.ds(r0, T), lanes] = (y_ref[pl.ds(r0, T), lanes] + hc[lt]) * _gelu_tanh(gb)
        return carry

    cbw = lax.fori_loop(0, nc, bwd, h0_ref[0, 1:2, :])
    st_ref[0, 0:1, :] = cf
    st_ref[0, 1:2, :] = cbw


def _lru(u, h0, h0_block, lw, *, nb, n):
    T = min(n, 256)
    W = LRU_WIDTH
    return pl.pallas_call(
        functools.partial(_lru_kernel, N=n, T=T),
        grid=(nb,),
        in_specs=[
            pl.BlockSpec((n, 2 * W), lambda b: (b, 0)),
            pl.BlockSpec((1, 2, W), lambda b: (h0_block(b), 0, 0)),
            lw.spec("conv_w"), lw.spec("conv_b"), lw.spec("w_gate"), lw.spec("b_gate"),
            lw.spec("lru_lambda"),
        ],
        out_specs=[
            pl.BlockSpec((n, W), lambda b: (b, 0)),
            pl.BlockSpec((1, 2, W), lambda b: (b, 0, 0)),
        ],
        out_shape=[
            jax.ShapeDtypeStruct((nb * n, W), F32),
            jax.ShapeDtypeStruct((nb, 2, W), F32),
        ],
        scratch_shapes=[
            pltpu.VMEM((n + 2 * HALO, W), F32),
            pltpu.VMEM((W // 128, n, 128), F32),
            pltpu.VMEM((W // 128, n, 128), F32),
            pltpu.VMEM((W // 128, T, 128), F32),
            pltpu.VMEM((W // 128, T, 128), F32),
            pltpu.VMEM((W // 128, T, 128), F32),
        ],
        compiler_params=_params("arbitrary"),
        name="rglru",
    )(u, h0, lw["conv_w"], lw["conv_b"], lw["w_gate"], lw["b_gate"], lw["lru_lambda"])


def _pool_kernel(u_ref, wp_ref, sc_ref, y_ref, xpad, *, N, T):
    W = GROUP_WIDTH
    nc = N // T
    zeros = jnp.zeros((HALO, W), F32)
    xpad[0:HALO, :] = zeros
    xpad[N + HALO:N + 2 * HALO, :] = zeros

    def fill(j, carry):
        r0 = pl.multiple_of(j * T, T)
        xpad[pl.ds(r0 + HALO, T), :] = u_ref[pl.ds(r0, T), :]
        return carry

    lax.fori_loop(0, nc, fill, 0)

    grp = lax.broadcasted_iota(jnp.int32, (1, W), 1) // POOL_CH
    half = jnp.where(grp == 0, 1, jnp.where(grp == 1, 2, jnp.where(grp == 2, 4, 8)))
    scale = sc_ref[...]

    def chunk(j, carry):
        r0 = pl.multiple_of(j * T, T)
        ext = xpad[pl.ds(r0, T + 2 * HALO), :]
        w2 = _shift_rows(ext, -1) + ext
        w4 = _shift_rows(w2, -1) + _shift_rows(w2, 1)
        w8 = _shift_rows(w4, -2) + _shift_rows(w4, 2)
        w16 = _shift_rows(w8, -4) + _shift_rows(w8, 4)
        ws = jnp.where(grp == 0, w2, jnp.where(grp == 1, w4, jnp.where(grp == 2, w8, w16)))
        body = slice(HALO, HALO + T)
        t = r0 + lax.broadcasted_iota(jnp.int32, (T, W), 0)
        cnt = (jnp.minimum(t + half, N) - jnp.maximum(t - half, 0)).astype(F32)
        d = ws[body] / cnt - ext[body]
        y_ref[pl.ds(r0, T), :] = _dot(d.astype(BF16), wp_ref[...]) * scale
        return carry

    lax.fori_loop(0, nc, chunk, 0)


def _pool(u, lw, *, nb, n):
    W = GROUP_WIDTH
    T = min(n, 256)
    return pl.pallas_call(
        functools.partial(_pool_kernel, N=n, T=T),
        grid=(nb,),
        in_specs=[pl.BlockSpec((n, W), lambda b: (b, 0)), lw.spec("w_pool"), lw.spec("pool_scale")],
        out_specs=pl.BlockSpec((n, W), lambda b: (b, 0)),
        out_shape=jax.ShapeDtypeStruct((nb * n, W), F32),
        scratch_shapes=[pltpu.VMEM((n + 2 * HALO, W), F32)],
        compiler_params=_params("arbitrary"),
        name="pool_mixer",
    )(u, lw["w_pool"], lw["pool_scale"])


def _mix_ffn_kernel(*refs, final):
    if final:
        (x_ref, ya_ref, yb_ref, yc_ref, yd_ref, mod_ref, g2_ref, wo_ref, wg_ref, wu_ref, wd_ref,
         gf_ref, o_ref) = refs
    else:
        (x_ref, ya_ref, yb_ref, yc_ref, yd_ref, mod_ref, g2_ref, wo_ref, wg_ref, wu_ref, wd_ref,
         o_ref) = refs
    mod = mod_ref[0]
    gate1 = mod[:, 2 * D_MODEL:3 * D_MODEL]
    sh2 = mod[:, 3 * D_MODEL:4 * D_MODEL]
    sc2 = mod[:, 4 * D_MODEL:5 * D_MODEL]
    gate2 = mod[:, 5 * D_MODEL:6 * D_MODEL]
    mix = None
    for i, y_ref in enumerate((ya_ref, yb_ref, yc_ref, yd_ref)):
        part = _dot(y_ref[...].astype(BF16), wo_ref[i * GROUP_WIDTH:(i + 1) * GROUP_WIDTH, :])
        mix = part if mix is None else mix + part
    x1 = x_ref[...] + gate1 * mix
    h = _rms_rows(x1, D_MODEL) * g2_ref[...]
    hb = (h * (1.0 + sc2) + sh2).astype(BF16)
    ff = None
    for lo, hi in FF_CHUNKS:
        g = _dot(hb, wg_ref[:, lo:hi])
        up = _dot(hb, wu_ref[:, lo:hi])
        act = ((g * jax.nn.sigmoid(g)) * up).astype(BF16)
        part = _dot(act, wd_ref[lo:hi, :])
        ff = part if ff is None else ff + part
    x2 = x1 + gate2 * ff
    if final:
        x2 = _rms_rows(x2, D_MODEL) * gf_ref[...]
    o_ref[...] = x2


def _mix_ffn(x, ys, mod, lw, gf, *, nb, n, final):
    T = nb * n
    tm = TOKEN_TILE
    npt = n // tm

    def tok(width):
        return pl.BlockSpec((tm, width), lambda i: (i, 0))

    in_specs = [tok(D_MODEL), tok(256), tok(256), tok(256), tok(256), mod.spec(lambda i: i // npt),
                lw.spec("g2"), lw.spec("w_out"), lw.spec("w_gu", col_blocks=2, col_block=0),
                lw.spec("w_gu", col_blocks=2, col_block=1), lw.spec("w_down")]
    args = [x, *ys, mod.table, lw["g2"], lw["w_out"], lw["w_gu"], lw["w_gu"], lw["w_down"]]
    if final:
        in_specs.append(_resident((1, D_MODEL)))
        args.append(gf)
    return pl.pallas_call(
        functools.partial(_mix_ffn_kernel, final=final),
        grid=(T // tm,),
        in_specs=in_specs,
        out_specs=tok(D_MODEL),
        out_shape=jax.ShapeDtypeStruct((T, D_MODEL), F32),
        compiler_params=_params("arbitrary"),
        name="mix_ffn_final" if final else "mix_ffn",
    )(*args)


def _block_diag(w):
    L, G, c, e = w.shape
    return jnp.einsum('lgce,gh->lgche', w, jnp.eye(G, dtype=w.dtype)).reshape(L, G * c, G * e)


def _rot_cols(w):
    return jnp.concatenate([-w[..., 16:32], w[..., 0:16]], axis=-1)


def _stack_weights(p):
    w_in = p["w_in"]
    o1 = MLA_Q_RANK
    o2 = o1 + MLA_KV_RANK
    o3 = o2 + MLA_ROPE
    c_q, c_kv, k_r, rest = w_in[..., :o1], w_in[..., o1:o2], w_in[..., o2:o3], w_in[..., o3:]
    z = lambda n: jnp.zeros((DEPTH, D_MODEL, n), F32)
    w_in_eff = jnp.concatenate([c_q, k_r, z(32), c_kv, z(64), _rot_cols(k_r), z(32), rest], axis=-1)

    w_uq = p["mla_w_uq"]
    qd = MLA_NOPE + MLA_ROPE
    wq_parts, wqr_parts = [], []
    zq = lambda n: jnp.zeros((DEPTH, MLA_Q_RANK, n), F32)
    for h in range(MLA_HEADS):
        wh = w_uq[..., h * qd:(h + 1) * qd]
        wq_parts += [wh, zq(MLA_SLOT - qd)]
        wqr_parts += [zq(MLA_NOPE), _rot_cols(wh[..., MLA_NOPE:]), zq(MLA_SLOT - qd)]
    pad_rows = lambda w: jnp.pad(w, ((0, 0), (0, 256 - MLA_Q_RANK), (0, 0)))
    w_ukv = p["mla_w_ukv"]
    wk_parts, wv_parts = [], []
    zk = jnp.zeros((DEPTH, MLA_KV_RANK, MLA_SLOT - MLA_NOPE), F32)
    for h in range(MLA_HEADS):
        base = h * (MLA_NOPE + MLA_V)
        wk_parts += [w_ukv[..., base:base + MLA_NOPE], zk]
        wv_parts.append(w_ukv[..., base + MLA_NOPE:base + MLA_NOPE + MLA_V])

    w_r, w_i, b_r, b_i = p["lru_w_r"], p["lru_w_i"], p["lru_b_r"], p["lru_b_i"]
    w_gate = jnp.concatenate([_block_diag(w_r[:, 0]), _block_diag(w_r[:, 1]),
                              _block_diag(w_i[:, 0]), _block_diag(w_i[:, 1])], axis=-1)
    b_gate = jnp.concatenate([b_r[:, 0], b_r[:, 1], b_i[:, 0], b_i[:, 1]], axis=-1)
    row = lambda v: v[:, None, :]
    return {
        "g1": row(p["norm1_g"]),
        "g2": row(p["norm2_g"]),
        "w_in": w_in_eff.astype(BF16),
        "gq": row(jnp.pad(p["mla_q_norm_g"], ((0, 0), (0, 256 - MLA_Q_RANK)))),
        "gkv": row(p["mla_kv_norm_g"]),
        "wq": pad_rows(jnp.concatenate(wq_parts, axis=-1)).astype(BF16),
        "wqr": pad_rows(jnp.concatenate(wqr_parts, axis=-1)).astype(BF16),
        "wkv": jnp.concatenate(wk_parts + wv_parts, axis=-1).astype(BF16),
        "conv_w": p["lru_conv_w"],
        "conv_b": row(p["lru_conv_b"]),
        "w_gate": w_gate.astype(BF16),
        "b_gate": row(b_gate),
        "lru_lambda": p["lru_lambda"],
        "w_pool": _block_diag(p["pool_w"]).astype(BF16),
        "pool_scale": row(p["pool_scale"]),
        "diff_lambda": p["diff_lambda"],
        "diff_g": row(jnp.tile(p["diff_norm_g"], (1, DIFF_HEADS))),
        "w_out": p["w_out"].astype(BF16),
        "w_gu": p["w_gu"].astype(BF16),
        "w_down": p["w_down"].astype(BF16),
    }


def _rope_tables(n, positional):
    quarter = MLA_ROPE // 4
    if positional:
        t = jnp.arange(n)
        row = (t // GRID_W).astype(F32)
        col = (t % GRID_W).astype(F32)
        inv = ROPE_BASE ** (-jnp.arange(quarter, dtype=F32) / quarter)
        ang = jnp.concatenate([row[:, None] * inv, col[:, None] * inv], axis=-1)
        cos, sin = jnp.cos(ang), jnp.sin(ang)
    else:
        cos, sin = jnp.ones((n, 16), F32), jnp.zeros((n, 16), F32)
    scale = LOG2E / math.sqrt(MLA_NOPE + MLA_ROPE)
    place = np.zeros((32, TAB_WIDTH), np.float32)
    offset = np.zeros((1, TAB_WIDTH), np.float32)
    offset[0, 0:64] = offset[0, 96:128] = scale
    for i in range(16):
        for half in (64, 80):
            place[i, half + i] = scale
            place[16 + i, 128 + half + i] = scale
            place[i, 256 + half + i] = 1.0
            place[16 + i, 384 + half + i] = 1.0
        for grp in range(8):
            place[i, 512 + 32 * grp + i] = place[i, 512 + 32 * grp + 16 + i] = 1.0
            place[16 + i, 768 + 32 * grp + i] = -1.0
            place[16 + i, 1024 + 32 * grp + 16 + i] = 1.0
    return jnp.dot(jnp.concatenate([cos, sin], axis=1), place, precision=lax.Precision.HIGHEST) + offset


def _layer(x, mod, lw, tabs, layer_idx, ctx, gf, *, nb, n, final):
    emit_cache = ctx is None
    tok_nb, tok_n = (1, nb * n) if mod.shared else (nb, n)
    outs = _inproj(x, mod, lw, tabs, nb=tok_nb, n=tok_n, emit_cache=emit_cache)
    q, k, vt, u_lru, u_pool, dq, dk, dvt = outs[:8]
    lam_init = 0.8 - 0.6 * math.exp(-0.3 * layer_idx)
    if ctx is None:
        h0 = jnp.zeros((1, 2, LRU_WIDTH), F32)
        h0_block = lambda b: 0
        mla_ctx = diff_ctx = None
    else:
        ckv, kr_pad, cdk, cdv, h0 = ctx
        p = ckv.shape[0] // (nb * DEPTH)
        h0_block = lambda b: b * DEPTH + layer_idx
        kc, vtc, dkc, dvtc = _ctx_prep(ckv, kr_pad, cdk, cdv, lw, nb=nb, p=p)
        mla_ctx = (kc, vtc)
        diff_ctx = (dkc, dvtc)
    y_mla = _mla_attn(q, k, vt, mla_ctx, nb=nb, n=n)
    y_lru, st = _lru(u_lru, h0, h0_block, lw, nb=nb, n=n)
    y_pool = _pool(u_pool, lw, nb=nb, n=n)
    y_diff = _diff_attn(dq, dk, dvt, diff_ctx, lw, nb=nb, n=n, lam_init=lam_init)
    x2 = _mix_ffn(x, (y_mla, y_lru, y_pool, y_diff), mod, lw, gf, nb=tok_nb, n=tok_n, final=final)
    cache = (outs[8], outs[9][:, 64:96], outs[10], outs[11], st) if emit_cache else None
    return x2, cache


def kernel(x_prompt, x_sample, cache_mla_ckv, cache_mla_krope, cache_diff_k, cache_diff_v, state_lru,
           c, c_ctx, w_ada, b_ada, norm1_g, norm2_g, w_in, mla_q_norm_g, mla_w_uq, mla_kv_norm_g,
           mla_w_ukv, lru_conv_w, lru_conv_b, lru_w_r, lru_b_r, lru_w_i, lru_b_i, lru_lambda, pool_w,
           pool_scale, diff_lambda, diff_norm_g, w_out, w_gu, w_down, final_norm_g):
    p = {
        "norm1_g": norm1_g, "norm2_g": norm2_g, "w_in": w_in, "mla_q_norm_g": mla_q_norm_g,
        "mla_w_uq": mla_w_uq, "mla_kv_norm_g": mla_kv_norm_g, "mla_w_ukv": mla_w_ukv,
        "lru_conv_w": lru_conv_w, "lru_conv_b": lru_conv_b, "lru_w_r": lru_w_r, "lru_b_r": lru_b_r,
        "lru_w_i": lru_w_i, "lru_b_i": lru_b_i, "lru_lambda": lru_lambda, "pool_w": pool_w,
        "pool_scale": pool_scale, "diff_lambda": diff_lambda, "diff_norm_g": diff_norm_g,
        "w_out": w_out, "w_gu": w_gu, "w_down": w_down,
    }
    Bp, Np, _ = x_prompt.shape
    Bs, Ns, _ = x_sample.shape
    P = cache_mla_ckv.shape[2]

    cond_all = jnp.concatenate([c, c_ctx[None, :], jnp.zeros((MOD_ROWS - Bs - 1, D_MODEL), F32)], axis=0)
    mod_table = _ada(cond_all, w_ada, b_ada).reshape(DEPTH * MOD_ROWS, 1, 6 * D_MODEL)
    tabs_p = _rope_tables(Bp * Np, positional=False)
    tabs_s = _rope_tables(Ns, positional=True)
    kr_pad = jnp.pad(cache_mla_krope, ((0, 0), (0, 0), (0, 0), (MLA_NOPE, MLA_SLOT - MLA_NOPE - MLA_ROPE)))
    flat = lambda a, w: a.reshape(Bs * DEPTH * P, w)
    ctx = (flat(cache_mla_ckv, MLA_KV_RANK), flat(kr_pad, MLA_SLOT), flat(cache_diff_k, 256),
           flat(cache_diff_v, 256), state_lru.reshape(Bs * DEPTH, 2, LRU_WIDTH))
    gf = final_norm_g[None, :]
    stacked = _stack_weights(p)

    xp = x_prompt.reshape(Bp * Np, D_MODEL)
    xs = x_sample.reshape(Bs * Ns, D_MODEL)
    caches = []
    for l in range(DEPTH):
        lw = _LayerWeights(stacked, l)
        final = l == DEPTH - 1
        mod_p = _Mod(mod_table, l * MOD_ROWS + Bs, shared=True)
        mod_s = _Mod(mod_table, l * MOD_ROWS, shared=False)
        xp, cache = _layer(xp, mod_p, lw, tabs_p, l, None, gf, nb=Bp, n=Np, final=final)
        caches.append(cache)
        xs, _ = _layer(xs, mod_s, lw, tabs_s, l, ctx, gf, nb=Bs, n=Ns, final=final)

    stack = lambda i, w: jnp.stack([cc[i].reshape(Bp, Np, w) for cc in caches], axis=1)
    new_mla_ckv = stack(0, MLA_KV_RANK)
    new_mla_krope = stack(1, MLA_ROPE)
    new_diff_k = stack(2, 256).reshape(Bp, DEPTH, Np, DIFF_HEADS, 2, DIFF_DIM)
    new_diff_v = stack(3, 256).reshape(Bp, DEPTH, Np, DIFF_HEADS, 2 * DIFF_DIM)
    new_state_lru = jnp.stack([cc[4] for cc in caches], axis=1)
    return (xp.reshape(Bp, Np, D_MODEL), xs.reshape(Bs, Ns, D_MODEL),
            new_mla_ckv, new_mla_krope, new_diff_k, new_diff_v, new_state_lru)
```

```python
import functools
import math

import jax
import jax.numpy as jnp
import numpy as np
from jax import lax
from jax.experimental import pallas as pl
from jax.experimental.pallas import tpu as pltpu

F32 = jnp.float32
BF16 = jnp.bfloat16

D_MODEL = 1024
DEPTH = 2
GRID_W = 64
GROUP_WIDTH = 256
MLA_HEADS = 4
MLA_NOPE = 64
MLA_ROPE = 32
MLA_V = 64
MLA_Q_RANK = 192
MLA_KV_RANK = 128
MLA_SLOT = 128
LRU_WIDTH = 256
LRU_C = 8.0
POOL_WINDOWS = (2, 4, 8, 16)
POOL_CH = 64
DIFF_HEADS = 4
DIFF_DIM = 32
HEAD_V = 64
FF_HIDDEN = 2816
FF_CHUNKS = ((0, 1536), (1536, 2816))
ROPE_BASE = 10000.0
EPS = 1e-6
IN_EFF = 2048
HALO = 8
SCAN_RUN = 4
VT_ROWS = 80
ATT_TQ = 256
TOKEN_TILE = 512
TAB_WIDTH = 4 * 128 + 3 * 256
MOD_ROWS = 16
LOG2E = math.log2(math.e)

VMEM_LIMIT_BYTES = 56 * 1024 * 1024

_NT = (((1,), (1,)), ((), ()))


def _params(*sem):
    return pltpu.CompilerParams(dimension_semantics=sem, vmem_limit_bytes=VMEM_LIMIT_BYTES)


def _resident(shape):
    zeros = (0,) * len(shape)
    return pl.BlockSpec(shape, lambda *_: zeros, pipeline_mode=pl.Buffered(1))


def _dot(a, b):
    return jnp.dot(a, b, preferred_element_type=F32)


def _dot_nt(a, b):
    return lax.dot_general(a, b, _NT, preferred_element_type=F32)


def _rms_rows(x, width):
    ms = jnp.sum(x * x, axis=-1, keepdims=True) * (1.0 / width)
    return x * lax.rsqrt(ms + EPS)


def _store_vt(vt_ref, v):
    vt = v.T
    rows = v.shape[0]
    pad = VT_ROWS - HEAD_V
    ones_row = jnp.where(lax.broadcasted_iota(jnp.int32, (pad, rows), 0) == 0, 1.0, 0.0).astype(BF16)
    for hh in range(vt_ref.shape[0]):
        vt_ref[hh, 0:HEAD_V, :] = vt[hh * HEAD_V:(hh + 1) * HEAD_V, :].astype(BF16)
        vt_ref[hh, HEAD_V:VT_ROWS, :] = ones_row


class _Mod:
    def __init__(self, table, row0, shared):
        self.table, self.row0, self.shared = table, row0, shared

    def spec(self, batch_of):
        row0 = self.row0
        if self.shared:
            return pl.BlockSpec((1, 1, 6 * D_MODEL), lambda *g: (row0, 0, 0))
        return pl.BlockSpec((1, 1, 6 * D_MODEL), lambda *g: (row0 + batch_of(*g), 0, 0))


class _LayerWeights:
    def __init__(self, stacked, layer):
        self.stacked, self.layer = stacked, layer

    def __getitem__(self, name):
        return self.stacked[name]

    def spec(self, name, col_blocks=1, col_block=0):
        layer = self.layer
        _, rows, cols = self.stacked[name].shape
        return pl.BlockSpec((None, rows, cols // col_blocks), lambda *_: (layer, 0, col_block),
                            pipeline_mode=pl.Buffered(1))


def _ada_kernel(cond_ref, w_ref, b_ref, out_ref):
    c = cond_ref[...]
    s = c * jax.nn.sigmoid(c)
    out_ref[0] = _dot(s.astype(BF16), w_ref[0].astype(BF16)) + b_ref[0]


def _ada(cond_all, w_ada, b_ada):
    rows = cond_all.shape[0]
    tn = 1536
    return pl.pallas_call(
        _ada_kernel,
        grid=(DEPTH, 6 * D_MODEL // tn),
        in_specs=[
            pl.BlockSpec((rows, D_MODEL), lambda l, j: (0, 0)),
            pl.BlockSpec((1, D_MODEL, tn), lambda l, j: (l, 0, j)),
            pl.BlockSpec((1, 1, tn), lambda l, j: (l, 0, j)),
        ],
        out_specs=pl.BlockSpec((1, rows, tn), lambda l, j: (l, 0, j)),
        out_shape=jax.ShapeDtypeStruct((DEPTH, rows, 6 * D_MODEL), F32),
        compiler_params=_params("arbitrary", "arbitrary"),
        name="ada_mod",
    )(cond_all, w_ada, b_ada.reshape(DEPTH, 1, 6 * D_MODEL))


def _inproj_kernel(x_ref, mod_ref, g1_ref, win_ref, gq_ref, gkv_ref, wq_ref, wqr_ref, wkv_ref, tab_ref,
                   q_out, k_out, vt_out, lru_out, pool_out, dq_out, dk_out, dvt_out, *cache_outs):
    cosq_ref, sinq_ref, cosk_ref, sink_ref = (tab_ref.at[:, i * 128:(i + 1) * 128] for i in range(4))
    cosd_ref, sina_ref, sinb_ref = (tab_ref.at[:, 512 + i * 256:768 + i * 256] for i in range(3))
    x = x_ref[...]
    mod = mod_ref[0]
    sh1 = mod[:, 0:D_MODEL]
    sc1 = mod[:, D_MODEL:2 * D_MODEL]
    h = _rms_rows(x, D_MODEL) * g1_ref[...]
    hb = (h * (1.0 + sc1) + sh1).astype(BF16)

    u_mla = _dot(hb, win_ref[:, 0:512])
    u_pd = _dot(hb, win_ref[:, 1024:1536])
    u_kv = _dot(hb, win_ref[:, 1536:2048])
    t01 = u_mla[:, 0:256]
    lane = lax.broadcasted_iota(jnp.int32, (1, 256), 1)
    cq = jnp.where(lane < MLA_Q_RANK, t01, 0.0)
    cqn = (_rms_rows(cq, MLA_Q_RANK) * gq_ref[...]).astype(BF16)
    qa = _dot(cqn, wq_ref[...])
    qr = _dot(cqn, wqr_ref[...])
    cosq = cosq_ref[...]
    sinq = sinq_ref[...]
    ckv = u_mla[:, 256:384]
    lat = _rms_rows(ckv, MLA_KV_RANK) * gkv_ref[...]
    latb = lat.astype(BF16)
    kkv = _dot(latb, wkv_ref[...])
    kk = kkv[:, 0:MLA_HEADS * MLA_SLOT]
    _store_vt(vt_out, kkv[:, MLA_HEADS * MLA_SLOT:])
    t1 = t01[:, 128:256]
    t3 = u_mla[:, 384:512]
    kro = t1 * cosk_ref[...] + t3 * sink_ref[...]
    for hh in range(MLA_HEADS):
        sl = slice(hh * MLA_SLOT, (hh + 1) * MLA_SLOT)
        q_out[hh] = (qa[:, sl] * cosq + qr[:, sl] * sinq).astype(q_out.dtype)
        k_out[hh] = (kk[:, sl] + kro).astype(k_out.dtype)

    lru_out[...] = _dot(hb, win_ref[:, 512:1024])
    pool_out[...] = u_pd[:, 0:256]

    cosd = cosd_ref[...]
    sina = sina_ref[...]
    sinb = sinb_ref[...]

    def rope(t):
        return t * cosd + pltpu.roll(t, 256 - 16, 1) * sina + pltpu.roll(t, 16, 1) * sinb

    dq = u_pd[:, 256:512]
    dk = u_kv[:, 0:256]
    dv = u_kv[:, 256:512]
    dq_out[...] = (rope(dq) * (LOG2E / math.sqrt(DIFF_DIM))).astype(dq_out.dtype)
    dk_out[...] = rope(dk).astype(dk_out.dtype)
    _store_vt(dvt_out, dv)

    if cache_outs:
        lat_out, kr_out, dk_raw_out, dv_raw_out = cache_outs
        lat_out[...] = lat
        kr_out[...] = t1
        dk_raw_out[...] = dk
        dv_raw_out[...] = dv


def _inproj(x, mod, lw, tabs, *, nb, n, emit_cache):
    T = nb * n
    tm = TOKEN_TILE
    npt = n // tm
    row_blk = lambda j, b: b * npt + j

    def tok(width):
        return pl.BlockSpec((tm, width), lambda j, b: (row_blk(j, b), 0))

    def tab(width):
        return pl.BlockSpec((tm, width), lambda j, b: (j, 0))

    head = pl.BlockSpec((MLA_HEADS, tm, MLA_SLOT), lambda j, b: (0, row_blk(j, b), 0))
    vt_spec = pl.BlockSpec((MLA_HEADS, VT_ROWS, tm), lambda j, b: (0, 0, row_blk(j, b)))
    wnames = ("g1", "w_in", "gq", "gkv", "wq", "wqr", "wkv")
    in_specs = [tok(D_MODEL), mod.spec(lambda j, b: b)] + [lw.spec(nm) for nm in wnames] + [
        tab(TAB_WIDTH)]
    out_specs = [head, head, vt_spec, tok(512), tok(256), tok(256), tok(256), vt_spec]
    vt_shape = jax.ShapeDtypeStruct((MLA_HEADS, VT_ROWS, T), BF16)
    out_shape = [
        jax.ShapeDtypeStruct((MLA_HEADS, T, MLA_SLOT), BF16),
        jax.ShapeDtypeStruct((MLA_HEADS, T, MLA_SLOT), BF16),
        vt_shape,
        jax.ShapeDtypeStruct((T, 512), F32),
        jax.ShapeDtypeStruct((T, 256), F32),
        jax.ShapeDtypeStruct((T, 256), BF16),
        jax.ShapeDtypeStruct((T, 256), BF16),
        vt_shape,
    ]
    if emit_cache:
        out_specs += [tok(128), tok(128), tok(256), tok(256)]
        out_shape += [jax.ShapeDtypeStruct((T, 128), F32), jax.ShapeDtypeStruct((T, 128), F32),
                      jax.ShapeDtypeStruct((T, 256), F32), jax.ShapeDtypeStruct((T, 256), F32)]
    return pl.pallas_call(
        _inproj_kernel,
        grid=(npt, nb),
        in_specs=in_specs,
        out_specs=out_specs,
        out_shape=out_shape,
        compiler_params=_params("arbitrary", "arbitrary"),
        name="inproj_cache" if emit_cache else "inproj",
    )(x, mod.table, *[lw[nm] for nm in wnames], tabs)


def _ctx_prep_kernel(ckv_ref, kr_ref, dk_ref, dv_ref, wkv_ref, k_out, vt_out, dk_out, dvt_out):
    latb = ckv_ref[...].astype(BF16)
    kkv = _dot(latb, wkv_ref[...])
    kk = kkv[:, 0:MLA_HEADS * MLA_SLOT]
    kr = kr_ref[...]
    for hh in range(MLA_HEADS):
        k_out[hh] = (kk[:, hh * MLA_SLOT:(hh + 1) * MLA_SLOT] + kr).astype(k_out.dtype)
    _store_vt(vt_out, kkv[:, MLA_HEADS * MLA_SLOT:])
    dk_out[...] = dk_ref[...].astype(dk_out.dtype)
    _store_vt(dvt_out, dv_ref[...])


def _ctx_prep(ckv, kr_pad, cdk, cdv, lw, *, nb, p):
    T = nb * p
    layer = lw.layer
    cache_row = lambda w: pl.BlockSpec((p, w), lambda b: (b * DEPTH + layer, 0))
    row = lambda w: pl.BlockSpec((p, w), lambda b: (b, 0))
    vt_spec = pl.BlockSpec((MLA_HEADS, VT_ROWS, p), lambda b: (0, 0, b))
    vt_shape = jax.ShapeDtypeStruct((MLA_HEADS, VT_ROWS, T), BF16)
    return pl.pallas_call(
        _ctx_prep_kernel,
        grid=(nb,),
        in_specs=[cache_row(128), cache_row(128), cache_row(256), cache_row(256),
                  lw.spec("wkv")],
        out_specs=[pl.BlockSpec((MLA_HEADS, p, MLA_SLOT), lambda b: (0, b, 0)), vt_spec, row(256), vt_spec],
        out_shape=[jax.ShapeDtypeStruct((MLA_HEADS, T, MLA_SLOT), BF16), vt_shape,
                   jax.ShapeDtypeStruct((T, 256), BF16), vt_shape],
        compiler_params=_params("arbitrary"),
        name="ctx_prep",
    )(ckv, kr_pad, cdk, cdv, lw["wkv"])


SAFE_DENOM = 2.0 ** -60
E_BUFS = 3
BOUND_SLACK = 1.02


def _scores(k_new, k_ctx, q):
    sn = _dot_nt(k_new(), q)
    sc = _dot_nt(k_ctx(), q) if k_ctx is not None else None
    return sn, sc


def _exact_shift(k_new, k_ctx, q):
    sn, sc = _scores(k_new, k_ctx, q)
    m = jnp.max(sn, axis=0, keepdims=True)
    if sc is not None:
        m = jnp.maximum(m, jnp.max(sc, axis=0, keepdims=True))
    return m


def _bound_shift(q, key_norm2):
    qf = q.astype(F32)
    ones = jnp.ones((8, q.shape[1]), BF16)
    q_norm2 = _dot_nt(ones, (qf * qf).astype(BF16))[0:1, :]
    return jnp.sqrt(q_norm2 * key_norm2) * BOUND_SLACK


def _max_row_norm2(k_new, k_ctx, col_sum):
    def one(k):
        kf = k.astype(F32)
        return jnp.max(_dot((kf * kf).astype(BF16), col_sum), axis=0, keepdims=True)
    m = one(k_new)
    if k_ctx is not None:
        m = jnp.maximum(m, one(k_ctx))
    return m * BOUND_SLACK


def _exp_stage(e_buf, k_new, k_ctx, q, shift, n_ctx):
    sn, sc = _scores(k_new, k_ctx, q)
    e_buf[n_ctx:, :] = jnp.exp2(sn - shift).astype(BF16)
    if sc is not None:
        e_buf[0:n_ctx, :] = jnp.exp2(sc - shift).astype(BF16)


def _value_stage(e_buf, vt_new, vt_ctx, n_ctx):
    o = _dot(vt_new(), e_buf[n_ctx:, :])
    if vt_ctx is not None:
        o = o + _dot(vt_ctx(), e_buf[0:n_ctx, :])
    return o


def _run_pipeline(n_maps, exp_stage, value_stage):
    ahead = E_BUFS - 1
    for u in range(min(ahead, n_maps)):
        exp_stage(u)
    for u in range(n_maps):
        if u + ahead < n_maps:
            exp_stage(u + ahead)
        value_stage(u)


def _att_scratch(nk, key_shape):
    scratch = [pltpu.VMEM((8, 128), F32),
               pltpu.VMEM((MLA_HEADS * HEAD_V, ATT_TQ), F32)]
    if key_shape is not None:
        scratch += [pltpu.VMEM(key_shape, BF16), pltpu.VMEM((MLA_HEADS, VT_ROWS, nk), BF16)]
    return scratch + [pltpu.VMEM((nk, ATT_TQ), BF16)] * E_BUFS


def _att_nsub(n):
    return 2 if n % (2 * ATT_TQ) == 0 else 1


def _mla_attn_kernel(*refs, has_ctx, nsub):
    if has_ctx:
        q_ref, k_ref, vt_ref, kc_ref, vtc_ref, o_ref, kn2, ot, keys, vals, *e_bufs = refs
        n_ctx = kc_ref.shape[1]
    else:
        q_ref, k_ref, vt_ref, o_ref, kn2, ot, *e_bufs = refs
        keys, vals = k_ref, vt_ref

    @pl.when(pl.program_id(1) == 0)
    def _():
        ones = jnp.ones((MLA_SLOT, 128), BF16)
        for hh in range(MLA_HEADS):
            kn2[hh:hh + 1, :] = _max_row_norm2(k_ref[hh], kc_ref[hh] if has_ctx else None, ones)
        if has_ctx:
            keys[:, 0:n_ctx, :] = kc_ref[...]
            keys[:, n_ctx:, :] = k_ref[...]
            vals[:, :, 0:n_ctx] = vtc_ref[...]
            vals[:, :, n_ctx:] = vt_ref[...]

    def run(exact):
        denoms = []

        def exp_stage(u):
            t, hh = divmod(u, MLA_HEADS)
            q = q_ref[hh, t * ATT_TQ:(t + 1) * ATT_TQ, :]
            k_all = lambda: keys[hh]
            shift = _exact_shift(k_all, None, q) if exact else _bound_shift(q, kn2[hh:hh + 1, 0:1])
            _exp_stage(e_bufs[u % E_BUFS], k_all, None, q, shift, 0)

        def value_stage(u):
            t, hh = divmod(u, MLA_HEADS)
            o = _value_stage(e_bufs[u % E_BUFS], lambda: vals[hh], None, 0)
            denom = o[HEAD_V:HEAD_V + 1, :]
            denoms.append(denom)
            ot[hh * HEAD_V:(hh + 1) * HEAD_V, :] = o[0:HEAD_V, :] * (1.0 / denom)
            if hh == MLA_HEADS - 1:
                o_ref[t * ATT_TQ:(t + 1) * ATT_TQ, :] = ot[...].T

        _run_pipeline(nsub * MLA_HEADS, exp_stage, value_stage)
        return jnp.min(functools.reduce(jnp.minimum, denoms))

    denom_min = run(exact=False)

    @pl.when(jnp.logical_not(denom_min >= SAFE_DENOM))
    def _():
        run(exact=True)


def _mla_attn(q, k, vt, ctx, *, nb, n):
    nsub = _att_nsub(n)
    tq = nsub * ATT_TQ
    npt = n // tq
    H, S = MLA_HEADS, MLA_SLOT
    in_specs = [
        pl.BlockSpec((H, tq, S), lambda b, j: (0, b * npt + j, 0)),
        pl.BlockSpec((H, n, S), lambda b, j: (0, b, 0)),
        pl.BlockSpec((H, VT_ROWS, n), lambda b, j: (0, 0, b)),
    ]
    args = [q, k, vt]
    n_ctx = 0
    if ctx is not None:
        n_ctx = ctx[0].shape[1] // nb
        in_specs += [
            pl.BlockSpec((H, n_ctx, S), lambda b, j: (0, b, 0)),
            pl.BlockSpec((H, VT_ROWS, n_ctx), lambda b, j: (0, 0, b)),
        ]
        args += list(ctx)
    return pl.pallas_call(
        functools.partial(_mla_attn_kernel, has_ctx=ctx is not None, nsub=nsub),
        grid=(nb, npt),
        in_specs=in_specs,
        out_specs=pl.BlockSpec((tq, 256), lambda b, j: (b * npt + j, 0)),
        out_shape=jax.ShapeDtypeStruct((nb * n, 256), F32),
        scratch_shapes=_att_scratch(n + n_ctx, (H, n + n_ctx, S) if ctx is not None else None),
        compiler_params=_params("arbitrary", "arbitrary"),
        name="mla_attn_ctx" if ctx is not None else "mla_attn",
    )(*args)


def _diff_attn_kernel(*refs, has_ctx, nsub, lam_init):
    if has_ctx:
        lv_ref, g_ref, q_ref, k_ref, vt_ref, kc_ref, vtc_ref, o_ref, kn2, ot, keys, vals, *e_bufs = refs
        n_ctx = kc_ref.shape[0]
    else:
        lv_ref, g_ref, q_ref, k_ref, vt_ref, o_ref, kn2, ot, *e_bufs = refs
        keys, vals = k_ref, vt_ref
    lv = lv_ref[...]
    lam = (jnp.exp(jnp.sum(lv[0:1] * lv[1:2], axis=-1, keepdims=True))
           - jnp.exp(jnp.sum(lv[2:3] * lv[3:4], axis=-1, keepdims=True)) + lam_init)
    lane128 = lax.broadcasted_iota(jnp.int32, (1, 128), 1)
    n_pairs = 2 * DIFF_HEADS

    @pl.when(pl.program_id(1) == 0)
    def _():
        dim = lax.broadcasted_iota(jnp.int32, (256, 128), 0)
        col = lax.broadcasted_iota(jnp.int32, (256, 128), 1)
        indicator = jnp.where(dim // DIFF_DIM == col, 1.0, 0.0).astype(BF16)
        kn2[0:1, :] = _max_row_norm2(k_ref[...], kc_ref[...] if has_ctx else None, indicator)
        if has_ctx:
            keys[0:n_ctx, :] = kc_ref[...]
            keys[n_ctx:, :] = k_ref[...]
            vals[:, :, 0:n_ctx] = vtc_ref[...]
            vals[:, :, n_ctx:] = vt_ref[...]

    def run(exact):
        denoms = []
        outs = {}

        def exp_stage(u):
            t, p = divmod(u, n_pairs)
            tile = slice((p * DIFF_DIM // 128) * 128, (p * DIFF_DIM // 128 + 1) * 128)
            k_new = lambda: keys[:, tile]
            k_ctx = None
            q = q_ref[t * ATT_TQ:(t + 1) * ATT_TQ, tile]
            lo = p * DIFF_DIM - tile.start
            in_pair = (lane128 >= lo) & (lane128 < lo + DIFF_DIM)
            qm = jnp.where(in_pair, q, jnp.zeros_like(q))
            shift = _exact_shift(k_new, k_ctx, qm) if exact else _bound_shift(qm, kn2[0:1, p:p + 1])
            _exp_stage(e_bufs[u % E_BUFS], k_new, k_ctx, qm, shift, 0)

        def value_stage(u):
            t, p = divmod(u, n_pairs)
            hh = p // 2
            o = _value_stage(e_bufs[u % E_BUFS], lambda: vals[hh], None, 0)
            denom = o[HEAD_V:HEAD_V + 1, :]
            denoms.append(denom)
            outs[u] = (o[0:HEAD_V, :], denom)
            if p % 2 == 1:
                (o0, l0), (o1, l1) = outs.pop(u - 1), outs.pop(u)
                o = o0 * (1.0 / l0) - o1 * (lam / l1)
                msq = jnp.sum(o * o, axis=0, keepdims=True) * (1.0 / HEAD_V)
                ot[hh * HEAD_V:(hh + 1) * HEAD_V, :] = o * lax.rsqrt(msq + EPS)
            if p == n_pairs - 1:
                o_ref[t * ATT_TQ:(t + 1) * ATT_TQ, :] = (ot[...].T * g_ref[...]) * (1.0 - lam_init)

        _run_pipeline(nsub * n_pairs, exp_stage, value_stage)
        return jnp.min(functools.reduce(jnp.minimum, denoms))

    denom_min = run(exact=False)

    @pl.when(jnp.logical_not(denom_min >= SAFE_DENOM))
    def _():
        run(exact=True)


def _diff_attn(q, k, vt, ctx, lw, *, nb, n, lam_init):
    nsub = 1
    tq = nsub * ATT_TQ
    npt = n // tq
    in_specs = [
        lw.spec("diff_lambda"),
        lw.spec("diff_g"),
        pl.BlockSpec((tq, 256), lambda b, j: (b * npt + j, 0)),
        pl.BlockSpec((n, 256), lambda b, j: (b, 0)),
        pl.BlockSpec((DIFF_HEADS, VT_ROWS, n), lambda b, j: (0, 0, b)),
    ]
    args = [lw["diff_lambda"], lw["diff_g"], q, k, vt]
    n_ctx = 0
    if ctx is not None:
        n_ctx = ctx[0].shape[0] // nb
        in_specs += [pl.BlockSpec((n_ctx, 256), lambda b, j: (b, 0)),
                     pl.BlockSpec((DIFF_HEADS, VT_ROWS, n_ctx), lambda b, j: (0, 0, b))]
        args += list(ctx)
    return pl.pallas_call(
        functools.partial(_diff_attn_kernel, has_ctx=ctx is not None, nsub=nsub, lam_init=lam_init),
        grid=(nb, npt),
        in_specs=in_specs,
        out_specs=pl.BlockSpec((tq, 256), lambda b, j: (b * npt + j, 0)),
        out_shape=jax.ShapeDtypeStruct((nb * n, 256), F32),
        scratch_shapes=_att_scratch(n + n_ctx, (n + n_ctx, 256) if ctx is not None else None),
        compiler_params=_params("arbitrary", "arbitrary"),
        name="diff_attn_ctx" if ctx is not None else "diff_attn",
    )(*args)


def _shift_rows(v, k):
    return pltpu.roll(v, (-k) % v.shape[0], 0)


def _scan_strided(a_ref, b_ref, h_ref, row0, carry, n_rows, reverse):
    sub = lax.broadcasted_iota(jnp.int32, (8, 128), 0)
    span = 8 * SCAN_RUN
    order = tuple(range(SCAN_RUN))[::-1] if reverse else tuple(range(SCAN_RUN))
    starts = tuple(range(0, n_rows, span))[::-1] if reverse else tuple(range(0, n_rows, span))
    carries = []
    for lt in range(a_ref.shape[0]):
        c_in = carry[:, lt * 128:(lt + 1) * 128]
        for start in starts:
            tile = lambda ref, g: ref[lt, pl.ds(row0 + start + g, 8, stride=SCAN_RUN), :]
            a = [tile(a_ref, g) for g in range(SCAN_RUN)]
            b = [tile(b_ref, g) for g in range(SCAN_RUN)]
            h = {order[0]: b[order[0]]}
            p = {order[0]: a[order[0]]}
            for prev, g in zip(order, order[1:]):
                h[g] = a[g] * h[prev] + b[g]
                p[g] = a[g] * p[prev]
            pi, hi = p[order[-1]], h[order[-1]]
            for s in (1, 2, 4):
                shift = 8 - s if reverse else s
                valid = (sub < 8 - s) if reverse else (sub >= s)
                pr, hr = pltpu.roll(pi, shift, 0), pltpu.roll(hi, shift, 0)
                hi = jnp.where(valid, pi * hr + hi, hi)
                pi = jnp.where(valid, pi * pr, pi)
            one = 7 if reverse else 1
            first = (sub == 7) if reverse else (sub == 0)
            pe = jnp.where(first, 1.0, pltpu.roll(pi, one, 0))
            he = jnp.where(first, 0.0, pltpu.roll(hi, one, 0))
            c = pe * c_in + he
            for g in range(SCAN_RUN):
                h_ref[lt, pl.ds(start + g, 8, stride=SCAN_RUN), :] = h[g] + p[g] * c
            last = 0 if reverse else 7
            c_in = pi[last:last + 1, :] * c_in + hi[last:last + 1, :]
        carries.append(c_in)
    return jnp.concatenate(carries, axis=1)


def _sigmoid(x):
    return 0.5 * jnp.tanh(0.5 * x) + 0.5


def _gelu_tanh(x):
    return x * (0.5 * (1.0 + jnp.tanh(math.sqrt(2.0 / math.pi) * (x + 0.044715 * (x * x * x)))))


def _lru_kernel(u_ref, h0_ref, cw_ref, cb_ref, wg_ref, bg_ref, lam_ref, y_ref, st_ref,
                xpad, a1s, b1s, a0c, b0c, hc, *, N, T):
    W = LRU_WIDTH
    nc = N // T
    tiles = [slice(lt * 128, (lt + 1) * 128) for lt in range(W // 128)]
    zeros = jnp.zeros((HALO, W), F32)
    xpad[0:HALO, :] = zeros
    xpad[N + HALO:N + 2 * HALO, :] = zeros

    def fill(j, carry):
        r0 = pl.multiple_of(j * T, T)
        xpad[pl.ds(r0 + HALO, T), :] = u_ref[pl.ds(r0, T), 0:W]
        return carry

    lax.fori_loop(0, nc, fill, 0)

    z = -lam_ref[...]
    sp = jnp.maximum(z, 0.0) + jnp.log1p(jnp.exp(-jnp.abs(z)))
    cw = cw_ref[...]
    cb = cb_ref[...]
    bg = bg_ref[...]

    def fwd(j, carry):
        r0 = pl.multiple_of(j * T, T)
        ext = xpad[pl.ds(r0, T + 2 * HALO), :]
        body = slice(HALO, HALO + T)
        xc = cb
        for tap in range(4):
            xc = xc + _shift_rows(ext, tap - 1)[body] * cw[tap:tap + 1]
        g = _sigmoid(_dot(xc.astype(BF16), wg_ref[...]) + bg)
        ab = []
        for d in range(2):
            r = g[:, d * W:(d + 1) * W]
            i = g[:, (2 + d) * W:(3 + d) * W]
            log_a = (-LRU_C * r) * sp[d:d + 1]
            a = jnp.exp(log_a)
            bt = (jnp.sqrt(1.0 - a * a) * i) * xc
            ab.append((a, bt))
        for lt, lanes in enumerate(tiles):
            a0c[lt] = ab[0][0][:, lanes]
            b0c[lt] = ab[0][1][:, lanes]
            a1s[lt, pl.ds(r0, T), :] = ab[1][0][:, lanes]
            b1s[lt, pl.ds(r0, T), :] = ab[1][1][:, lanes]
        carry = _scan_strided(a0c, b0c, hc, 0, carry, T, reverse=False)
        for lt, lanes in enumerate(tiles):
            y_ref[pl.ds(r0, T), lanes] = hc[lt]
        return carry

    cf = lax.fori_loop(0, nc, fwd, h0_ref[0, 0:1, :], unroll=min(2, nc))

    def bwd(jj, carry):
        r0 = pl.multiple_of((nc - 1 - jj) * T, T)
        carry = _scan_strided(a1s, b1s, hc, r0, carry, T, reverse=True)
        for lt, lanes in enumerate(tiles):
            gb = u_ref[pl.ds(r0, T), W + lt * 128:W + (lt + 1) * 128]
            y_ref[pl.ds(r0, T), lanes] = (y_ref[pl.ds(r0, T), lanes] + hc[lt]) * _gelu_tanh(gb)
        return carry

    cbw = lax.fori_loop(0, nc, bwd, h0_ref[0, 1:2, :], unroll=min(2, nc))
    st_ref[0, 0:1, :] = cf
    st_ref[0, 1:2, :] = cbw


def _lru(u, h0, h0_block, lw, *, nb, n):
    T = min(n, 256)
    W = LRU_WIDTH
    return pl.pallas_call(
        functools.partial(_lru_kernel, N=n, T=T),
        grid=(nb,),
        in_specs=[
            pl.BlockSpec((n, 2 * W), lambda b: (b, 0)),
            pl.BlockSpec((1, 2, W), lambda b: (h0_block(b), 0, 0)),
            lw.spec("conv_w"), lw.spec("conv_b"), lw.spec("w_gate"), lw.spec("b_gate"),
            lw.spec("lru_lambda"),
        ],
        out_specs=[
            pl.BlockSpec((n, W), lambda b: (b, 0)),
            pl.BlockSpec((1, 2, W), lambda b: (b, 0, 0)),
        ],
        out_shape=[
            jax.ShapeDtypeStruct((nb * n, W), F32),
            jax.ShapeDtypeStruct((nb, 2, W), F32),
        ],
        scratch_shapes=[
            pltpu.VMEM((n + 2 * HALO, W), F32),
            pltpu.VMEM((W // 128, n, 128), F32),
            pltpu.VMEM((W // 128, n, 128), F32),
            pltpu.VMEM((W // 128, T, 128), F32),
            pltpu.VMEM((W // 128, T, 128), F32),
            pltpu.VMEM((W // 128, T, 128), F32),
        ],
        compiler_params=_params("arbitrary"),
        name="rglru",
    )(u, h0, lw["conv_w"], lw["conv_b"], lw["w_gate"], lw["b_gate"], lw["lru_lambda"])


def _pool_kernel(u_ref, wp_ref, sc_ref, y_ref, xpad, *, N, T):
    W = GROUP_WIDTH
    nc = N // T
    zeros = jnp.zeros((HALO, W), F32)
    xpad[0:HALO, :] = zeros
    xpad[N + HALO:N + 2 * HALO, :] = zeros

    def fill(j, carry):
        r0 = pl.multiple_of(j * T, T)
        xpad[pl.ds(r0 + HALO, T), :] = u_ref[pl.ds(r0, T), :]
        return carry

    lax.fori_loop(0, nc, fill, 0)

    grp = lax.broadcasted_iota(jnp.int32, (1, W), 1) // POOL_CH
    half = jnp.where(grp == 0, 1, jnp.where(grp == 1, 2, jnp.where(grp == 2, 4, 8)))
    scale = sc_ref[...]

    def chunk(j, carry):
        r0 = pl.multiple_of(j * T, T)
        ext = xpad[pl.ds(r0, T + 2 * HALO), :]
        w2 = _shift_rows(ext, -1) + ext
        w4 = _shift_rows(w2, -1) + _shift_rows(w2, 1)
        w8 = _shift_rows(w4, -2) + _shift_rows(w4, 2)
        w16 = _shift_rows(w8, -4) + _shift_rows(w8, 4)
        ws = jnp.where(grp == 0, w2, jnp.where(grp == 1, w4, jnp.where(grp == 2, w8, w16)))
        body = slice(HALO, HALO + T)
        t = r0 + lax.broadcasted_iota(jnp.int32, (T, W), 0)
        cnt = (jnp.minimum(t + half, N) - jnp.maximum(t - half, 0)).astype(F32)
        d = ws[body] / cnt - ext[body]
        y_ref[pl.ds(r0, T), :] = _dot(d.astype(BF16), wp_ref[...]) * scale
        return carry

    lax.fori_loop(0, nc, chunk, 0, unroll=min(2, nc))


def _pool(u, lw, *, nb, n):
    W = GROUP_WIDTH
    T = min(n, 256)
    return pl.pallas_call(
        functools.partial(_pool_kernel, N=n, T=T),
        grid=(nb,),
        in_specs=[pl.BlockSpec((n, W), lambda b: (b, 0)), lw.spec("w_pool"), lw.spec("pool_scale")],
        out_specs=pl.BlockSpec((n, W), lambda b: (b, 0)),
        out_shape=jax.ShapeDtypeStruct((nb * n, W), F32),
        scratch_shapes=[pltpu.VMEM((n + 2 * HALO, W), F32)],
        compiler_params=_params("arbitrary"),
        name="pool_mixer",
    )(u, lw["w_pool"], lw["pool_scale"])


def _mix_ffn_kernel(*refs, final):
    if final:
        (x_ref, ya_ref, yb_ref, yc_ref, yd_ref, mod_ref, g2_ref, wo_ref, wg_ref, wu_ref, wd_ref,
         gf_ref, o_ref) = refs
    else:
        (x_ref, ya_ref, yb_ref, yc_ref, yd_ref, mod_ref, g2_ref, wo_ref, wg_ref, wu_ref, wd_ref,
         o_ref) = refs
    mod = mod_ref[0]
    gate1 = mod[:, 2 * D_MODEL:3 * D_MODEL]
    sh2 = mod[:, 3 * D_MODEL:4 * D_MODEL]
    sc2 = mod[:, 4 * D_MODEL:5 * D_MODEL]
    gate2 = mod[:, 5 * D_MODEL:6 * D_MODEL]
    mix = None
    for i, y_ref in enumerate((ya_ref, yb_ref, yc_ref, yd_ref)):
        part = _dot(y_ref[...].astype(BF16), wo_ref[i * GROUP_WIDTH:(i + 1) * GROUP_WIDTH, :])
        mix = part if mix is None else mix + part
    x1 = x_ref[...] + gate1 * mix
    h = _rms_rows(x1, D_MODEL) * g2_ref[...]
    hb = (h * (1.0 + sc2) + sh2).astype(BF16)
    ff = None
    for lo, hi in FF_CHUNKS:
        g = _dot(hb, wg_ref[:, lo:hi])
        up = _dot(hb, wu_ref[:, lo:hi])
        act = ((g * jax.nn.sigmoid(g)) * up).astype(BF16)
        part = _dot(act, wd_ref[lo:hi, :])
        ff = part if ff is None else ff + part
    x2 = x1 + gate2 * ff
    if final:
        x2 = _rms_rows(x2, D_MODEL) * gf_ref[...]
    o_ref[...] = x2


def _mix_ffn(x, ys, mod, lw, gf, *, nb, n, final):
    T = nb * n
    tm = TOKEN_TILE
    npt = n // tm

    def tok(width):
        return pl.BlockSpec((tm, width), lambda i: (i, 0))

    in_specs = [tok(D_MODEL), tok(256), tok(256), tok(256), tok(256), mod.spec(lambda i: i // npt),
                lw.spec("g2"), lw.spec("w_out"), lw.spec("w_gu", col_blocks=2, col_block=0),
                lw.spec("w_gu", col_blocks=2, col_block=1), lw.spec("w_down")]
    args = [x, *ys, mod.table, lw["g2"], lw["w_out"], lw["w_gu"], lw["w_gu"], lw["w_down"]]
    if final:
        in_specs.append(_resident((1, D_MODEL)))
        args.append(gf)
    return pl.pallas_call(
        functools.partial(_mix_ffn_kernel, final=final),
        grid=(T // tm,),
        in_specs=in_specs,
        out_specs=tok(D_MODEL),
        out_shape=jax.ShapeDtypeStruct((T, D_MODEL), F32),
        compiler_params=_params("arbitrary"),
        name="mix_ffn_final" if final else "mix_ffn",
    )(*args)


def _block_diag(w):
    L, G, c, e = w.shape
    return jnp.einsum('lgce,gh->lgche', w, jnp.eye(G, dtype=w.dtype)).reshape(L, G * c, G * e)


def _rot_cols(w):
    return jnp.concatenate([-w[..., 16:32], w[..., 0:16]], axis=-1)


def _stack_weights(p):
    w_in = p["w_in"]
    o1 = MLA_Q_RANK
    o2 = o1 + MLA_KV_RANK
    o3 = o2 + MLA_ROPE
    c_q, c_kv, k_r, rest = w_in[..., :o1], w_in[..., o1:o2], w_in[..., o2:o3], w_in[..., o3:]
    z = lambda n: jnp.zeros((DEPTH, D_MODEL, n), F32)
    w_in_eff = jnp.concatenate([c_q, k_r, z(32), c_kv, z(64), _rot_cols(k_r), z(32), rest], axis=-1)

    w_uq = p["mla_w_uq"]
    qd = MLA_NOPE + MLA_ROPE
    wq_parts, wqr_parts = [], []
    zq = lambda n: jnp.zeros((DEPTH, MLA_Q_RANK, n), F32)
    for h in range(MLA_HEADS):
        wh = w_uq[..., h * qd:(h + 1) * qd]
        wq_parts += [wh, zq(MLA_SLOT - qd)]
        wqr_parts += [zq(MLA_NOPE), _rot_cols(wh[..., MLA_NOPE:]), zq(MLA_SLOT - qd)]
    pad_rows = lambda w: jnp.pad(w, ((0, 0), (0, 256 - MLA_Q_RANK), (0, 0)))
    w_ukv = p["mla_w_ukv"]
    wk_parts, wv_parts = [], []
    zk = jnp.zeros((DEPTH, MLA_KV_RANK, MLA_SLOT - MLA_NOPE), F32)
    for h in range(MLA_HEADS):
        base = h * (MLA_NOPE + MLA_V)
        wk_parts += [w_ukv[..., base:base + MLA_NOPE], zk]
        wv_parts.append(w_ukv[..., base + MLA_NOPE:base + MLA_NOPE + MLA_V])

    w_r, w_i, b_r, b_i = p["lru_w_r"], p["lru_w_i"], p["lru_b_r"], p["lru_b_i"]
    w_gate = jnp.concatenate([_block_diag(w_r[:, 0]), _block_diag(w_r[:, 1]),
                              _block_diag(w_i[:, 0]), _block_diag(w_i[:, 1])], axis=-1)
    b_gate = jnp.concatenate([b_r[:, 0], b_r[:, 1], b_i[:, 0], b_i[:, 1]], axis=-1)
    row = lambda v: v[:, None, :]
    return {
        "g1": row(p["norm1_g"]),
        "g2": row(p["norm2_g"]),
        "w_in": w_in_eff.astype(BF16),
        "gq": row(jnp.pad(p["mla_q_norm_g"], ((0, 0), (0, 256 - MLA_Q_RANK)))),
        "gkv": row(p["mla_kv_norm_g"]),
        "wq": pad_rows(jnp.concatenate(wq_parts, axis=-1)).astype(BF16),
        "wqr": pad_rows(jnp.concatenate(wqr_parts, axis=-1)).astype(BF16),
        "wkv": jnp.concatenate(wk_parts + wv_parts, axis=-1).astype(BF16),
        "conv_w": p["lru_conv_w"],
        "conv_b": row(p["lru_conv_b"]),
        "w_gate": w_gate.astype(BF16),
        "b_gate": row(b_gate),
        "lru_lambda": p["lru_lambda"],
        "w_pool": _block_diag(p["pool_w"]).astype(BF16),
        "pool_scale": row(p["pool_scale"]),
        "diff_lambda": p["diff_lambda"],
        "diff_g": row(jnp.tile(p["diff_norm_g"], (1, DIFF_HEADS))),
        "w_out": p["w_out"].astype(BF16),
        "w_gu": p["w_gu"].astype(BF16),
        "w_down": p["w_down"].astype(BF16),
    }


def _rope_tables(n, positional):
    quarter = MLA_ROPE // 4
    if positional:
        t = jnp.arange(n)
        row = (t // GRID_W).astype(F32)
        col = (t % GRID_W).astype(F32)
        inv = ROPE_BASE ** (-jnp.arange(quarter, dtype=F32) / quarter)
        ang = jnp.concatenate([row[:, None] * inv, col[:, None] * inv], axis=-1)
        cos, sin = jnp.cos(ang), jnp.sin(ang)
    else:
        cos, sin = jnp.ones((n, 16), F32), jnp.zeros((n, 16), F32)
    scale = LOG2E / math.sqrt(MLA_NOPE + MLA_ROPE)
    place = np.zeros((32, TAB_WIDTH), np.float32)
    offset = np.zeros((1, TAB_WIDTH), np.float32)
    offset[0, 0:64] = offset[0, 96:128] = scale
    for i in range(16):
        for half in (64, 80):
            place[i, half + i] = scale
            place[16 + i, 128 + half + i] = scale
            place[i, 256 + half + i] = 1.0
            place[16 + i, 384 + half + i] = 1.0
        for grp in range(8):
            place[i, 512 + 32 * grp + i] = place[i, 512 + 32 * grp + 16 + i] = 1.0
            place[16 + i, 768 + 32 * grp + i] = -1.0
            place[16 + i, 1024 + 32 * grp + 16 + i] = 1.0
    return jnp.dot(jnp.concatenate([cos, sin], axis=1), place, precision=lax.Precision.HIGHEST) + offset


def _layer(x, mod, lw, tabs, layer_idx, ctx, gf, *, nb, n, final):
    emit_cache = ctx is None
    tok_nb, tok_n = (1, nb * n) if mod.shared else (nb, n)
    outs = _inproj(x, mod, lw, tabs, nb=tok_nb, n=tok_n, emit_cache=emit_cache)
    q, k, vt, u_lru, u_pool, dq, dk, dvt = outs[:8]
    lam_init = 0.8 - 0.6 * math.exp(-0.3 * layer_idx)
    if ctx is None:
        h0 = jnp.zeros((1, 2, LRU_WIDTH), F32)
        h0_block = lambda b: 0
        mla_ctx = diff_ctx = None
    else:
        ckv, kr_pad, cdk, cdv, h0 = ctx
        p = ckv.shape[0] // (nb * DEPTH)
        h0_block = lambda b: b * DEPTH + layer_idx
        kc, vtc, dkc, dvtc = _ctx_prep(ckv, kr_pad, cdk, cdv, lw, nb=nb, p=p)
        mla_ctx = (kc, vtc)
        diff_ctx = (dkc, dvtc)
    y_mla = _mla_attn(q, k, vt, mla_ctx, nb=nb, n=n)
    y_lru, st = _lru(u_lru, h0, h0_block, lw, nb=nb, n=n)
    y_pool = _pool(u_pool, lw, nb=nb, n=n)
    y_diff = _diff_attn(dq, dk, dvt, diff_ctx, lw, nb=nb, n=n, lam_init=lam_init)
    x2 = _mix_ffn(x, (y_mla, y_lru, y_pool, y_diff), mod, lw, gf, nb=tok_nb, n=tok_n, final=final)
    cache = (outs[8], outs[9][:, 64:96], outs[10], outs[11], st) if emit_cache else None
    return x2, cache


def kernel(x_prompt, x_sample, cache_mla_ckv, cache_mla_krope, cache_diff_k, cache_diff_v, state_lru,
           c, c_ctx, w_ada, b_ada, norm1_g, norm2_g, w_in, mla_q_norm_g, mla_w_uq, mla_kv_norm_g,
           mla_w_ukv, lru_conv_w, lru_conv_b, lru_w_r, lru_b_r, lru_w_i, lru_b_i, lru_lambda, pool_w,
           pool_scale, diff_lambda, diff_norm_g, w_out, w_gu, w_down, final_norm_g):
    p = {
        "norm1_g": norm1_g, "norm2_g": norm2_g, "w_in": w_in, "mla_q_norm_g": mla_q_norm_g,
        "mla_w_uq": mla_w_uq, "mla_kv_norm_g": mla_kv_norm_g, "mla_w_ukv": mla_w_ukv,
        "lru_conv_w": lru_conv_w, "lru_conv_b": lru_conv_b, "lru_w_r": lru_w_r, "lru_b_r": lru_b_r,
        "lru_w_i": lru_w_i, "lru_b_i": lru_b_i, "lru_lambda": lru_lambda, "pool_w": pool_w,
        "pool_scale": pool_scale, "diff_lambda": diff_lambda, "diff_norm_g": diff_norm_g,
        "w_out": w_out, "w_gu": w_gu, "w_down": w_down,
    }
    Bp, Np, _ = x_prompt.shape
    Bs, Ns, _ = x_sample.shape
    P = cache_mla_ckv.shape[2]

    cond_all = jnp.concatenate([c, c_ctx[None, :], jnp.zeros((MOD_ROWS - Bs - 1, D_MODEL), F32)], axis=0)
    mod_table = _ada(cond_all, w_ada, b_ada).reshape(DEPTH * MOD_ROWS, 1, 6 * D_MODEL)
    tabs_p = _rope_tables(Bp * Np, positional=False)
    tabs_s = _rope_tables(Ns, positional=True)
    kr_pad = jnp.pad(cache_mla_krope, ((0, 0), (0, 0), (0, 0), (MLA_NOPE, MLA_SLOT - MLA_NOPE - MLA_ROPE)))
    flat = lambda a, w: a.reshape(Bs * DEPTH * P, w)
    ctx = (flat(cache_mla_ckv, MLA_KV_RANK), flat(kr_pad, MLA_SLOT), flat(cache_diff_k, 256),
           flat(cache_diff_v, 256), state_lru.reshape(Bs * DEPTH, 2, LRU_WIDTH))
    gf = final_norm_g[None, :]
    stacked = _stack_weights(p)

    xp = x_prompt.reshape(Bp * Np, D_MODEL)
    xs = x_sample.reshape(Bs * Ns, D_MODEL)
    caches = []
    for l in range(DEPTH):
        lw = _LayerWeights(stacked, l)
        final = l == DEPTH - 1
        mod_p = _Mod(mod_table, l * MOD_ROWS + Bs, shared=True)
        mod_s = _Mod(mod_table, l * MOD_ROWS, shared=False)
        xp, cache = _layer(xp, mod_p, lw, tabs_p, l, None, gf, nb=Bp, n=Np, final=final)
        caches.append(cache)
        xs, _ = _layer(xs, mod_s, lw, tabs_s, l, ctx, gf, nb=Bs, n=Ns, final=final)

    stack = lambda i, w: jnp.stack([cc[i].reshape(Bp, Np, w) for cc in caches], axis=1)
    new_mla_ckv = stack(0, MLA_KV_RANK)
    new_mla_krope = stack(1, MLA_ROPE)
    new_diff_k = stack(2, 256).reshape(Bp, DEPTH, Np, DIFF_HEADS, 2, DIFF_DIM)
    new_diff_v = stack(3, 256).reshape(Bp, DEPTH, Np, DIFF_HEADS, 2 * DIFF_DIM)
    new_state_lru = jnp.stack([cc[4] for cc in caches], axis=1)
    return (xp.reshape(Bp, Np, D_MODEL), xs.reshape(Bs, Ns, D_MODEL),
            new_mla_ckv, new_mla_krope, new_diff_k, new_diff_v, new_state_lru)
```

```python
import functools
import math

import jax
import jax.numpy as jnp
import numpy as np
from jax import lax
from jax.experimental import pallas as pl
from jax.experimental.pallas import tpu as pltpu

F32 = jnp.float32
BF16 = jnp.bfloat16

D_MODEL = 1024
DEPTH = 2
GRID_W = 64
GROUP_WIDTH = 256
MLA_HEADS = 4
MLA_NOPE = 64
MLA_ROPE = 32
MLA_V = 64
MLA_Q_RANK = 192
MLA_KV_RANK = 128
MLA_SLOT = 128
LRU_WIDTH = 256
LRU_C = 8.0
POOL_WINDOWS = (2, 4, 8, 16)
POOL_CH = 64
DIFF_HEADS = 4
DIFF_DIM = 32
HEAD_V = 64
FF_HIDDEN = 2816
FF_CHUNKS = ((0, 1536), (1536, 2816))
ROPE_BASE = 10000.0
EPS = 1e-6
IN_EFF = 2048
HALO = 8
SCAN_RUN = 4
VT_ROWS = 80
ATT_TQ = 256
TOKEN_TILE = 512
TAB_WIDTH = 4 * 128 + 3 * 256
MOD_ROWS = 16
LOG2E = math.log2(math.e)

VMEM_LIMIT_BYTES = 56 * 1024 * 1024

_NT = (((1,), (1,)), ((), ()))


def _params(*sem):
    return pltpu.CompilerParams(dimension_semantics=sem, vmem_limit_bytes=VMEM_LIMIT_BYTES)


def _resident(shape):
    zeros = (0,) * len(shape)
    return pl.BlockSpec(shape, lambda *_: zeros, pipeline_mode=pl.Buffered(1))


def _dot(a, b):
    return jnp.dot(a, b, preferred_element_type=F32)


def _dot_nt(a, b):
    return lax.dot_general(a, b, _NT, preferred_element_type=F32)


def _rms_rows(x, width):
    ms = jnp.sum(x * x, axis=-1, keepdims=True) * (1.0 / width)
    return x * lax.rsqrt(ms + EPS)


def _store_vt(vt_ref, v):
    vt = v.T
    rows = v.shape[0]
    pad = VT_ROWS - HEAD_V
    ones_row = jnp.where(lax.broadcasted_iota(jnp.int32, (pad, rows), 0) == 0, 1.0, 0.0).astype(BF16)
    for hh in range(vt_ref.shape[0]):
        vt_ref[hh, 0:HEAD_V, :] = vt[hh * HEAD_V:(hh + 1) * HEAD_V, :].astype(BF16)
        vt_ref[hh, HEAD_V:VT_ROWS, :] = ones_row


class _Mod:
    def __init__(self, table, row0, shared):
        self.table, self.row0, self.shared = table, row0, shared

    def spec(self, batch_of):
        row0 = self.row0
        if self.shared:
            return pl.BlockSpec((1, 1, 6 * D_MODEL), lambda *g: (row0, 0, 0))
        return pl.BlockSpec((1, 1, 6 * D_MODEL), lambda *g: (row0 + batch_of(*g), 0, 0))


class _LayerWeights:
    def __init__(self, stacked, layer):
        self.stacked, self.layer = stacked, layer

    def __getitem__(self, name):
        return self.stacked[name]

    def spec(self, name, col_blocks=1, col_block=0):
        layer = self.layer
        _, rows, cols = self.stacked[name].shape
        return pl.BlockSpec((None, rows, cols // col_blocks), lambda *_: (layer, 0, col_block),
                            pipeline_mode=pl.Buffered(1))


def _ada_kernel(cond_ref, w_ref, b_ref, out_ref):
    c = cond_ref[...]
    s = c * jax.nn.sigmoid(c)
    out_ref[0] = _dot(s.astype(BF16), w_ref[0].astype(BF16)) + b_ref[0]


def _ada(cond_all, w_ada, b_ada):
    rows = cond_all.shape[0]
    tn = 1536
    return pl.pallas_call(
        _ada_kernel,
        grid=(DEPTH, 6 * D_MODEL // tn),
        in_specs=[
            pl.BlockSpec((rows, D_MODEL), lambda l, j: (0, 0)),
            pl.BlockSpec((1, D_MODEL, tn), lambda l, j: (l, 0, j)),
            pl.BlockSpec((1, 1, tn), lambda l, j: (l, 0, j)),
        ],
        out_specs=pl.BlockSpec((1, rows, tn), lambda l, j: (l, 0, j)),
        out_shape=jax.ShapeDtypeStruct((DEPTH, rows, 6 * D_MODEL), F32),
        compiler_params=_params("arbitrary", "arbitrary"),
        name="ada_mod",
    )(cond_all, w_ada, b_ada.reshape(DEPTH, 1, 6 * D_MODEL))


def _inproj_kernel(x_ref, mod_ref, g1_ref, win_ref, gq_ref, gkv_ref, wq_ref, wqr_ref, wkv_ref, tab_ref,
                   q_out, k_out, vt_out, lru_out, pool_out, dq_out, dk_out, dvt_out, *cache_outs):
    cosq_ref, sinq_ref, cosk_ref, sink_ref = (tab_ref.at[:, i * 128:(i + 1) * 128] for i in range(4))
    cosd_ref, sina_ref, sinb_ref = (tab_ref.at[:, 512 + i * 256:768 + i * 256] for i in range(3))
    x = x_ref[...]
    mod = mod_ref[0]
    sh1 = mod[:, 0:D_MODEL]
    sc1 = mod[:, D_MODEL:2 * D_MODEL]
    h = _rms_rows(x, D_MODEL) * g1_ref[...]
    hb = (h * (1.0 + sc1) + sh1).astype(BF16)

    u_mla = _dot(hb, win_ref[:, 0:512])
    u_pd = _dot(hb, win_ref[:, 1024:1536])
    u_kv = _dot(hb, win_ref[:, 1536:2048])
    t01 = u_mla[:, 0:256]
    lane = lax.broadcasted_iota(jnp.int32, (1, 256), 1)
    cq = jnp.where(lane < MLA_Q_RANK, t01, 0.0)
    cqn = (_rms_rows(cq, MLA_Q_RANK) * gq_ref[...]).astype(BF16)
    qa = _dot(cqn, wq_ref[...])
    qr = _dot(cqn, wqr_ref[...])
    cosq = cosq_ref[...]
    sinq = sinq_ref[...]
    ckv = u_mla[:, 256:384]
    lat = _rms_rows(ckv, MLA_KV_RANK) * gkv_ref[...]
    latb = lat.astype(BF16)
    kkv = _dot(latb, wkv_ref[...])
    kk = kkv[:, 0:MLA_HEADS * MLA_SLOT]
    _store_vt(vt_out, kkv[:, MLA_HEADS * MLA_SLOT:])
    t1 = t01[:, 128:256]
    t3 = u_mla[:, 384:512]
    kro = t1 * cosk_ref[...] + t3 * sink_ref[...]
    for hh in range(MLA_HEADS):
        sl = slice(hh * MLA_SLOT, (hh + 1) * MLA_SLOT)
        q_out[hh] = (qa[:, sl] * cosq + qr[:, sl] * sinq).astype(q_out.dtype)
        k_out[hh] = (kk[:, sl] + kro).astype(k_out.dtype)

    lru_out[...] = _dot(hb, win_ref[:, 512:1024])
    pool_out[...] = u_pd[:, 0:256]

    cosd = cosd_ref[...]
    sina = sina_ref[...]
    sinb = sinb_ref[...]

    def rope(t):
        return t * cosd + pltpu.roll(t, 256 - 16, 1) * sina + pltpu.roll(t, 16, 1) * sinb

    dq = u_pd[:, 256:512]
    dk = u_kv[:, 0:256]
    dv = u_kv[:, 256:512]
    dq_out[...] = (rope(dq) * (LOG2E / math.sqrt(DIFF_DIM))).astype(dq_out.dtype)
    dk_out[...] = rope(dk).astype(dk_out.dtype)
    _store_vt(dvt_out, dv)

    if cache_outs:
        lat_out, kr_out, dk_raw_out, dv_raw_out = cache_outs
        lat_out[...] = lat
        kr_out[...] = t1
        dk_raw_out[...] = dk
        dv_raw_out[...] = dv


def _inproj(x, mod, lw, tabs, *, nb, n, emit_cache):
    T = nb * n
    tm = TOKEN_TILE
    npt = n // tm
    row_blk = lambda j, b: b * npt + j

    def tok(width):
        return pl.BlockSpec((tm, width), lambda j, b: (row_blk(j, b), 0))

    def tab(width):
        return pl.BlockSpec((tm, width), lambda j, b: (j, 0))

    head = pl.BlockSpec((MLA_HEADS, tm, MLA_SLOT), lambda j, b: (0, row_blk(j, b), 0))
    vt_spec = pl.BlockSpec((MLA_HEADS, VT_ROWS, tm), lambda j, b: (0, 0, row_blk(j, b)))
    wnames = ("g1", "w_in", "gq", "gkv", "wq", "wqr", "wkv")
    in_specs = [tok(D_MODEL), mod.spec(lambda j, b: b)] + [lw.spec(nm) for nm in wnames] + [
        tab(TAB_WIDTH)]
    out_specs = [head, head, vt_spec, tok(512), tok(256), tok(256), tok(256), vt_spec]
    vt_shape = jax.ShapeDtypeStruct((MLA_HEADS, VT_ROWS, T), BF16)
    out_shape = [
        jax.ShapeDtypeStruct((MLA_HEADS, T, MLA_SLOT), BF16),
        jax.ShapeDtypeStruct((MLA_HEADS, T, MLA_SLOT), BF16),
        vt_shape,
        jax.ShapeDtypeStruct((T, 512), F32),
        jax.ShapeDtypeStruct((T, 256), F32),
        jax.ShapeDtypeStruct((T, 256), BF16),
        jax.ShapeDtypeStruct((T, 256), BF16),
        vt_shape,
    ]
    if emit_cache:
        out_specs += [tok(128), tok(128), tok(256), tok(256)]
        out_shape += [jax.ShapeDtypeStruct((T, 128), F32), jax.ShapeDtypeStruct((T, 128), F32),
                      jax.ShapeDtypeStruct((T, 256), F32), jax.ShapeDtypeStruct((T, 256), F32)]
    return pl.pallas_call(
        _inproj_kernel,
        grid=(npt, nb),
        in_specs=in_specs,
        out_specs=out_specs,
        out_shape=out_shape,
        compiler_params=_params("arbitrary", "arbitrary"),
        name="inproj_cache" if emit_cache else "inproj",
    )(x, mod.table, *[lw[nm] for nm in wnames], tabs)


def _ctx_prep_kernel(ckv_ref, kr_ref, dk_ref, dv_ref, wkv_ref, k_out, vt_out, dk_out, dvt_out):
    latb = ckv_ref[...].astype(BF16)
    kkv = _dot(latb, wkv_ref[...])
    kk = kkv[:, 0:MLA_HEADS * MLA_SLOT]
    kr = kr_ref[...]
    for hh in range(MLA_HEADS):
        k_out[hh] = (kk[:, hh * MLA_SLOT:(hh + 1) * MLA_SLOT] + kr).astype(k_out.dtype)
    _store_vt(vt_out, kkv[:, MLA_HEADS * MLA_SLOT:])
    dk_out[...] = dk_ref[...].astype(dk_out.dtype)
    _store_vt(dvt_out, dv_ref[...])


def _ctx_prep(ckv, kr_pad, cdk, cdv, lw, *, nb, p):
    T = nb * p
    layer = lw.layer
    cache_row = lambda w: pl.BlockSpec((p, w), lambda b: (b * DEPTH + layer, 0))
    row = lambda w: pl.BlockSpec((p, w), lambda b: (b, 0))
    vt_spec = pl.BlockSpec((MLA_HEADS, VT_ROWS, p), lambda b: (0, 0, b))
    vt_shape = jax.ShapeDtypeStruct((MLA_HEADS, VT_ROWS, T), BF16)
    return pl.pallas_call(
        _ctx_prep_kernel,
        grid=(nb,),
        in_specs=[cache_row(128), cache_row(128), cache_row(256), cache_row(256),
                  lw.spec("wkv")],
        out_specs=[pl.BlockSpec((MLA_HEADS, p, MLA_SLOT), lambda b: (0, b, 0)), vt_spec, row(256), vt_spec],
        out_shape=[jax.ShapeDtypeStruct((MLA_HEADS, T, MLA_SLOT), BF16), vt_shape,
                   jax.ShapeDtypeStruct((T, 256), BF16), vt_shape],
        compiler_params=_params("arbitrary"),
        name="ctx_prep",
    )(ckv, kr_pad, cdk, cdv, lw["wkv"])


SAFE_DENOM = 2.0 ** -60
E_BUFS = 3
BOUND_SLACK = 1.02


def _scores(k_new, k_ctx, q):
    sn = _dot_nt(k_new(), q)
    sc = _dot_nt(k_ctx(), q) if k_ctx is not None else None
    return sn, sc


def _exact_shift(k_new, k_ctx, q):
    sn, sc = _scores(k_new, k_ctx, q)
    m = jnp.max(sn, axis=0, keepdims=True)
    if sc is not None:
        m = jnp.maximum(m, jnp.max(sc, axis=0, keepdims=True))
    return m


def _bound_shift(q, key_norm2):
    qf = q.astype(F32)
    ones = jnp.ones((8, q.shape[1]), BF16)
    q_norm2 = _dot_nt(ones, (qf * qf).astype(BF16))[0:1, :]
    return jnp.sqrt(q_norm2 * key_norm2) * BOUND_SLACK


def _max_row_norm2(k_new, k_ctx, col_sum):
    def one(k):
        kf = k.astype(F32)
        return jnp.max(_dot((kf * kf).astype(BF16), col_sum), axis=0, keepdims=True)
    m = one(k_new)
    if k_ctx is not None:
        m = jnp.maximum(m, one(k_ctx))
    return m * BOUND_SLACK


def _exp_stage(e_buf, k_new, k_ctx, q, shift, n_ctx):
    sn, sc = _scores(k_new, k_ctx, q)
    e_buf[n_ctx:, :] = jnp.exp2(sn - shift).astype(BF16)
    if sc is not None:
        e_buf[0:n_ctx, :] = jnp.exp2(sc - shift).astype(BF16)


def _value_stage(e_buf, vt_new, vt_ctx, n_ctx):
    o = _dot(vt_new(), e_buf[n_ctx:, :])
    if vt_ctx is not None:
        o = o + _dot(vt_ctx(), e_buf[0:n_ctx, :])
    return o


def _run_pipeline(n_maps, exp_stage, value_stage):
    ahead = E_BUFS - 1
    for u in range(min(ahead, n_maps)):
        exp_stage(u)
    for u in range(n_maps):
        if u + ahead < n_maps:
            exp_stage(u + ahead)
        value_stage(u)


def _att_scratch(nk, key_shape):
    scratch = [pltpu.VMEM((8, 128), F32),
               pltpu.VMEM((MLA_HEADS * HEAD_V, ATT_TQ), F32)]
    if key_shape is not None:
        scratch += [pltpu.VMEM(key_shape, BF16), pltpu.VMEM((MLA_HEADS, VT_ROWS, nk), BF16)]
    return scratch + [pltpu.VMEM((nk, ATT_TQ), BF16)] * E_BUFS


def _att_nsub(n):
    return 2 if n % (2 * ATT_TQ) == 0 else 1


def _mla_attn_kernel(*refs, has_ctx, nsub):
    if has_ctx:
        q_ref, k_ref, vt_ref, kc_ref, vtc_ref, o_ref, kn2, ot, keys, vals, *e_bufs = refs
        n_ctx = kc_ref.shape[1]
    else:
        q_ref, k_ref, vt_ref, o_ref, kn2, ot, *e_bufs = refs
        keys, vals = k_ref, vt_ref

    @pl.when(pl.program_id(1) == 0)
    def _():
        ones = jnp.ones((MLA_SLOT, 128), BF16)
        for hh in range(MLA_HEADS):
            kn2[hh:hh + 1, :] = _max_row_norm2(k_ref[hh], kc_ref[hh] if has_ctx else None, ones)
        if has_ctx:
            keys[:, 0:n_ctx, :] = kc_ref[...]
            keys[:, n_ctx:, :] = k_ref[...]
            vals[:, :, 0:n_ctx] = vtc_ref[...]
            vals[:, :, n_ctx:] = vt_ref[...]

    def run(exact):
        denoms = []

        def exp_stage(u):
            t, hh = divmod(u, MLA_HEADS)
            q = q_ref[hh, t * ATT_TQ:(t + 1) * ATT_TQ, :]
            k_all = lambda: keys[hh]
            shift = _exact_shift(k_all, None, q) if exact else _bound_shift(q, kn2[hh:hh + 1, 0:1])
            _exp_stage(e_bufs[u % E_BUFS], k_all, None, q, shift, 0)

        def value_stage(u):
            t, hh = divmod(u, MLA_HEADS)
            o = _value_stage(e_bufs[u % E_BUFS], lambda: vals[hh], None, 0)
            denom = o[HEAD_V:HEAD_V + 1, :]
            denoms.append(denom)
            ot[hh * HEAD_V:(hh + 1) * HEAD_V, :] = o[0:HEAD_V, :] * (1.0 / denom)
            if hh == MLA_HEADS - 1:
                o_ref[t * ATT_TQ:(t + 1) * ATT_TQ, :] = ot[...].T

        _run_pipeline(nsub * MLA_HEADS, exp_stage, value_stage)
        return jnp.min(functools.reduce(jnp.minimum, denoms))

    denom_min = run(exact=False)

    @pl.when(jnp.logical_not(denom_min >= SAFE_DENOM))
    def _():
        run(exact=True)


def _mla_attn(q, k, vt, ctx, *, nb, n):
    nsub = _att_nsub(n)
    tq = nsub * ATT_TQ
    npt = n // tq
    H, S = MLA_HEADS, MLA_SLOT
    in_specs = [
        pl.BlockSpec((H, tq, S), lambda b, j: (0, b * npt + j, 0)),
        pl.BlockSpec((H, n, S), lambda b, j: (0, b, 0)),
        pl.BlockSpec((H, VT_ROWS, n), lambda b, j: (0, 0, b)),
    ]
    args = [q, k, vt]
    n_ctx = 0
    if ctx is not None:
        n_ctx = ctx[0].shape[1] // nb
        in_specs += [
            pl.BlockSpec((H, n_ctx, S), lambda b, j: (0, b, 0)),
            pl.BlockSpec((H, VT_ROWS, n_ctx), lambda b, j: (0, 0, b)),
        ]
        args += list(ctx)
    return pl.pallas_call(
        functools.partial(_mla_attn_kernel, has_ctx=ctx is not None, nsub=nsub),
        grid=(nb, npt),
        in_specs=in_specs,
        out_specs=pl.BlockSpec((tq, 256), lambda b, j: (b * npt + j, 0)),
        out_shape=jax.ShapeDtypeStruct((nb * n, 256), F32),
        scratch_shapes=_att_scratch(n + n_ctx, (H, n + n_ctx, S) if ctx is not None else None),
        compiler_params=_params("arbitrary", "arbitrary"),
        name="mla_attn_ctx" if ctx is not None else "mla_attn",
    )(*args)


def _diff_attn_kernel(*refs, has_ctx, nsub, lam_init):
    if has_ctx:
        lv_ref, g_ref, q_ref, k_ref, vt_ref, kc_ref, vtc_ref, o_ref, kn2, ot, keys, vals, *e_bufs = refs
        n_ctx = kc_ref.shape[0]
    else:
        lv_ref, g_ref, q_ref, k_ref, vt_ref, o_ref, kn2, ot, *e_bufs = refs
        keys, vals = k_ref, vt_ref
    lv = lv_ref[...]
    lam = (jnp.exp(jnp.sum(lv[0:1] * lv[1:2], axis=-1, keepdims=True))
           - jnp.exp(jnp.sum(lv[2:3] * lv[3:4], axis=-1, keepdims=True)) + lam_init)
    lane128 = lax.broadcasted_iota(jnp.int32, (1, 128), 1)
    n_pairs = 2 * DIFF_HEADS

    @pl.when(pl.program_id(1) == 0)
    def _():
        dim = lax.broadcasted_iota(jnp.int32, (256, 128), 0)
        col = lax.broadcasted_iota(jnp.int32, (256, 128), 1)
        indicator = jnp.where(dim // DIFF_DIM == col, 1.0, 0.0).astype(BF16)
        kn2[0:1, :] = _max_row_norm2(k_ref[...], kc_ref[...] if has_ctx else None, indicator)
        if has_ctx:
            keys[0:n_ctx, :] = kc_ref[...]
            keys[n_ctx:, :] = k_ref[...]
            vals[:, :, 0:n_ctx] = vtc_ref[...]
            vals[:, :, n_ctx:] = vt_ref[...]

    def run(exact):
        denoms = []
        outs = {}

        def exp_stage(u):
            t, p = divmod(u, n_pairs)
            tile = slice((p * DIFF_DIM // 128) * 128, (p * DIFF_DIM // 128 + 1) * 128)
            k_new = lambda: keys[:, tile]
            k_ctx = None
            q = q_ref[t * ATT_TQ:(t + 1) * ATT_TQ, tile]
            lo = p * DIFF_DIM - tile.start
            in_pair = (lane128 >= lo) & (lane128 < lo + DIFF_DIM)
            qm = jnp.where(in_pair, q, jnp.zeros_like(q))
            shift = _exact_shift(k_new, k_ctx, qm) if exact else _bound_shift(qm, kn2[0:1, p:p + 1])
            _exp_stage(e_bufs[u % E_BUFS], k_new, k_ctx, qm, shift, 0)

        def value_stage(u):
            t, p = divmod(u, n_pairs)
            hh = p // 2
            o = _value_stage(e_bufs[u % E_BUFS], lambda: vals[hh], None, 0)
            denom = o[HEAD_V:HEAD_V + 1, :]
            denoms.append(denom)
            outs[u] = (o[0:HEAD_V, :], denom)
            if p % 2 == 1:
                (o0, l0), (o1, l1) = outs.pop(u - 1), outs.pop(u)
                o = o0 * (1.0 / l0) - o1 * (lam / l1)
                msq = jnp.sum(o * o, axis=0, keepdims=True) * (1.0 / HEAD_V)
                ot[hh * HEAD_V:(hh + 1) * HEAD_V, :] = o * lax.rsqrt(msq + EPS)
            if p == n_pairs - 1:
                o_ref[t * ATT_TQ:(t + 1) * ATT_TQ, :] = (ot[...].T * g_ref[...]) * (1.0 - lam_init)

        _run_pipeline(nsub * n_pairs, exp_stage, value_stage)
        return jnp.min(functools.reduce(jnp.minimum, denoms))

    denom_min = run(exact=False)

    @pl.when(jnp.logical_not(denom_min >= SAFE_DENOM))
    def _():
        run(exact=True)


def _diff_attn(q, k, vt, ctx, lw, *, nb, n, lam_init):
    nsub = 1
    tq = nsub * ATT_TQ
    npt = n // tq
    in_specs = [
        lw.spec("diff_lambda"),
        lw.spec("diff_g"),
        pl.BlockSpec((tq, 256), lambda b, j: (b * npt + j, 0)),
        pl.BlockSpec((n, 256), lambda b, j: (b, 0)),
        pl.BlockSpec((DIFF_HEADS, VT_ROWS, n), lambda b, j: (0, 0, b)),
    ]
    args = [lw["diff_lambda"], lw["diff_g"], q, k, vt]
    n_ctx = 0
    if ctx is not None:
        n_ctx = ctx[0].shape[0] // nb
        in_specs += [pl.BlockSpec((n_ctx, 256), lambda b, j: (b, 0)),
                     pl.BlockSpec((DIFF_HEADS, VT_ROWS, n_ctx), lambda b, j: (0, 0, b))]
        args += list(ctx)
    return pl.pallas_call(
        functools.partial(_diff_attn_kernel, has_ctx=ctx is not None, nsub=nsub, lam_init=lam_init),
        grid=(nb, npt),
        in_specs=in_specs,
        out_specs=pl.BlockSpec((tq, 256), lambda b, j: (b * npt + j, 0)),
        out_shape=jax.ShapeDtypeStruct((nb * n, 256), F32),
        scratch_shapes=_att_scratch(n + n_ctx, (n + n_ctx, 256) if ctx is not None else None),
        compiler_params=_params("arbitrary", "arbitrary"),
        name="diff_attn_ctx" if ctx is not None else "diff_attn",
    )(*args)


def _shift_rows(v, k):
    return pltpu.roll(v, (-k) % v.shape[0], 0)


def _scan_strided(a_ref, b_ref, h_ref, row0, carry, n_rows, reverse):
    sub = lax.broadcasted_iota(jnp.int32, (8, 128), 0)
    span = 8 * SCAN_RUN
    order = tuple(range(SCAN_RUN))[::-1] if reverse else tuple(range(SCAN_RUN))
    starts = tuple(range(0, n_rows, span))[::-1] if reverse else tuple(range(0, n_rows, span))
    carries = []
    for lt in range(a_ref.shape[0]):
        c_in = carry[:, lt * 128:(lt + 1) * 128]
        for start in starts:
            tile = lambda ref, g: ref[lt, pl.ds(row0 + start + g, 8, stride=SCAN_RUN), :]
            a = [tile(a_ref, g) for g in range(SCAN_RUN)]
            b = [tile(b_ref, g) for g in range(SCAN_RUN)]
            h = {order[0]: b[order[0]]}
            p = {order[0]: a[order[0]]}
            for prev, g in zip(order, order[1:]):
                h[g] = a[g] * h[prev] + b[g]
                p[g] = a[g] * p[prev]
            pi, hi = p[order[-1]], h[order[-1]]
            for s in (1, 2, 4):
                shift = 8 - s if reverse else s
                valid = (sub < 8 - s) if reverse else (sub >= s)
                pr, hr = pltpu.roll(pi, shift, 0), pltpu.roll(hi, shift, 0)
                hi = jnp.where(valid, pi * hr + hi, hi)
                pi = jnp.where(valid, pi * pr, pi)
            one = 7 if reverse else 1
            first = (sub == 7) if reverse else (sub == 0)
            pe = jnp.where(first, 1.0, pltpu.roll(pi, one, 0))
            he = jnp.where(first, 0.0, pltpu.roll(hi, one, 0))
            c = pe * c_in + he
            for g in range(SCAN_RUN):
                h_ref[lt, pl.ds(start + g, 8, stride=SCAN_RUN), :] = h[g] + p[g] * c
            last = 0 if reverse else 7
            c_in = pi[last:last + 1, :] * c_in + hi[last:last + 1, :]
        carries.append(c_in)
    return jnp.concatenate(carries, axis=1)


def _sigmoid(x):
    return 0.5 * jnp.tanh(0.5 * x) + 0.5


def _gelu_tanh(x):
    return x * (0.5 * (1.0 + jnp.tanh(math.sqrt(2.0 / math.pi) * (x + 0.044715 * (x * x * x)))))


def _lru_kernel(u_ref, h0_ref, cw_ref, cb_ref, wg_ref, bg_ref, lam_ref, y_ref, st_ref,
                xpad, a1s, b1s, a0c, b0c, hc, *, N, T):
    W = LRU_WIDTH
    nc = N // T
    tiles = [slice(lt * 128, (lt + 1) * 128) for lt in range(W // 128)]
    zeros = jnp.zeros((HALO, W), F32)
    xpad[0:HALO, :] = zeros
    xpad[N + HALO:N + 2 * HALO, :] = zeros

    def fill(j, carry):
        r0 = pl.multiple_of(j * T, T)
        xpad[pl.ds(r0 + HALO, T), :] = u_ref[pl.ds(r0, T), 0:W]
        return carry

    lax.fori_loop(0, nc, fill, 0)

    z = -lam_ref[...]
    sp = jnp.maximum(z, 0.0) + jnp.log1p(jnp.exp(-jnp.abs(z)))
    cw = cw_ref[...]
    cb = cb_ref[...]
    bg = bg_ref[...]

    def fwd(j, carry):
        r0 = pl.multiple_of(j * T, T)
        ext = xpad[pl.ds(r0, T + 2 * HALO), :]
        body = slice(HALO, HALO + T)
        xc = cb
        for tap in range(4):
            xc = xc + _shift_rows(ext, tap - 1)[body] * cw[tap:tap + 1]
        g = _sigmoid(_dot(xc.astype(BF16), wg_ref[...]) + bg)
        ab = []
        for d in range(2):
            r = g[:, d * W:(d + 1) * W]
            i = g[:, (2 + d) * W:(3 + d) * W]
            log_a = (-LRU_C * r) * sp[d:d + 1]
            a = jnp.exp(log_a)
            bt = (jnp.sqrt(1.0 - a * a) * i) * xc
            ab.append((a, bt))
        for lt, lanes in enumerate(tiles):
            a0c[lt] = ab[0][0][:, lanes]
            b0c[lt] = ab[0][1][:, lanes]
            a1s[lt, pl.ds(r0, T), :] = ab[1][0][:, lanes]
            b1s[lt, pl.ds(r0, T), :] = ab[1][1][:, lanes]
        carry = _scan_strided(a0c, b0c, hc, 0, carry, T, reverse=False)
        for lt, lanes in enumerate(tiles):
            y_ref[pl.ds(r0, T), lanes] = hc[lt]
        return carry

    cf = lax.fori_loop(0, nc, fwd, h0_ref[0, 0:1, :], unroll=min(4, nc))

    def bwd(jj, carry):
        r0 = pl.multiple_of((nc - 1 - jj) * T, T)
        carry = _scan_strided(a1s, b1s, hc, r0, carry, T, reverse=True)
        for lt, lanes in enumerate(tiles):
            gb = u_ref[pl.ds(r0, T), W + lt * 128:W + (lt + 1) * 128]
            y_ref[pl.ds(r0, T), lanes] = (y_ref[pl.ds(r0, T), lanes] + hc[lt]) * _gelu_tanh(gb)
        return carry

    cbw = lax.fori_loop(0, nc, bwd, h0_ref[0, 1:2, :], unroll=min(4, nc))
    st_ref[0, 0:1, :] = cf
    st_ref[0, 1:2, :] = cbw


def _lru(u, h0, h0_block, lw, *, nb, n):
    T = min(n, 256)
    W = LRU_WIDTH
    return pl.pallas_call(
        functools.partial(_lru_kernel, N=n, T=T),
        grid=(nb,),
        in_specs=[
            pl.BlockSpec((n, 2 * W), lambda b: (b, 0)),
            pl.BlockSpec((1, 2, W), lambda b: (h0_block(b), 0, 0)),
            lw.spec("conv_w"), lw.spec("conv_b"), lw.spec("w_gate"), lw.spec("b_gate"),
            lw.spec("lru_lambda"),
        ],
        out_specs=[
            pl.BlockSpec((n, W), lambda b: (b, 0)),
            pl.BlockSpec((1, 2, W), lambda b: (b, 0, 0)),
        ],
        out_shape=[
            jax.ShapeDtypeStruct((nb * n, W), F32),
            jax.ShapeDtypeStruct((nb, 2, W), F32),
        ],
        scratch_shapes=[
            pltpu.VMEM((n + 2 * HALO, W), F32),
            pltpu.VMEM((W // 128, n, 128), F32),
            pltpu.VMEM((W // 128, n, 128), F32),
            pltpu.VMEM((W // 128, T, 128), F32),
            pltpu.VMEM((W // 128, T, 128), F32),
            pltpu.VMEM((W // 128, T, 128), F32),
        ],
        compiler_params=_params("arbitrary"),
        name="rglru",
    )(u, h0, lw["conv_w"], lw["conv_b"], lw["w_gate"], lw["b_gate"], lw["lru_lambda"])


def _pool_kernel(u_ref, wp_ref, sc_ref, y_ref, xpad, *, N, T):
    W = GROUP_WIDTH
    nc = N // T
    zeros = jnp.zeros((HALO, W), F32)
    xpad[0:HALO, :] = zeros
    xpad[N + HALO:N + 2 * HALO, :] = zeros

    def fill(j, carry):
        r0 = pl.multiple_of(j * T, T)
        xpad[pl.ds(r0 + HALO, T), :] = u_ref[pl.ds(r0, T), :]
        return carry

    lax.fori_loop(0, nc, fill, 0)

    grp = lax.broadcasted_iota(jnp.int32, (1, W), 1) // POOL_CH
    half = jnp.where(grp == 0, 1, jnp.where(grp == 1, 2, jnp.where(grp == 2, 4, 8)))
    scale = sc_ref[...]

    def chunk(j, carry):
        r0 = pl.multiple_of(j * T, T)
        ext = xpad[pl.ds(r0, T + 2 * HALO), :]
        w2 = _shift_rows(ext, -1) + ext
        w4 = _shift_rows(w2, -1) + _shift_rows(w2, 1)
        w8 = _shift_rows(w4, -2) + _shift_rows(w4, 2)
        w16 = _shift_rows(w8, -4) + _shift_rows(w8, 4)
        ws = jnp.where(grp == 0, w2, jnp.where(grp == 1, w4, jnp.where(grp == 2, w8, w16)))
        body = slice(HALO, HALO + T)
        t = r0 + lax.broadcasted_iota(jnp.int32, (T, W), 0)
        cnt = (jnp.minimum(t + half, N) - jnp.maximum(t - half, 0)).astype(F32)
        d = ws[body] / cnt - ext[body]
        y_ref[pl.ds(r0, T), :] = _dot(d.astype(BF16), wp_ref[...]) * scale
        return carry

    lax.fori_loop(0, nc, chunk, 0, unroll=min(4, nc))


def _pool(u, lw, *, nb, n):
    W = GROUP_WIDTH
    T = min(n, 256)
    return pl.pallas_call(
        functools.partial(_pool_kernel, N=n, T=T),
        grid=(nb,),
        in_specs=[pl.BlockSpec((n, W), lambda b: (b, 0)), lw.spec("w_pool"), lw.spec("pool_scale")],
        out_specs=pl.BlockSpec((n, W), lambda b: (b, 0)),
        out_shape=jax.ShapeDtypeStruct((nb * n, W), F32),
        scratch_shapes=[pltpu.VMEM((n + 2 * HALO, W), F32)],
        compiler_params=_params("arbitrary"),
        name="pool_mixer",
    )(u, lw["w_pool"], lw["pool_scale"])


def _mix_ffn_kernel(*refs, final):
    if final:
        (x_ref, ya_ref, yb_ref, yc_ref, yd_ref, mod_ref, g2_ref, wo_ref, wg_ref, wu_ref, wd_ref,
         gf_ref, o_ref) = refs
    else:
        (x_ref, ya_ref, yb_ref, yc_ref, yd_ref, mod_ref, g2_ref, wo_ref, wg_ref, wu_ref, wd_ref,
         o_ref) = refs
    mod = mod_ref[0]
    gate1 = mod[:, 2 * D_MODEL:3 * D_MODEL]
    sh2 = mod[:, 3 * D_MODEL:4 * D_MODEL]
    sc2 = mod[:, 4 * D_MODEL:5 * D_MODEL]
    gate2 = mod[:, 5 * D_MODEL:6 * D_MODEL]
    mix = None
    for i, y_ref in enumerate((ya_ref, yb_ref, yc_ref, yd_ref)):
        part = _dot(y_ref[...].astype(BF16), wo_ref[i * GROUP_WIDTH:(i + 1) * GROUP_WIDTH, :])
        mix = part if mix is None else mix + part
    x1 = x_ref[...] + gate1 * mix
    h = _rms_rows(x1, D_MODEL) * g2_ref[...]
    hb = (h * (1.0 + sc2) + sh2).astype(BF16)
    ff = None
    for lo, hi in FF_CHUNKS:
        g = _dot(hb, wg_ref[:, lo:hi])
        up = _dot(hb, wu_ref[:, lo:hi])
        act = ((g * jax.nn.sigmoid(g)) * up).astype(BF16)
        part = _dot(act, wd_ref[lo:hi, :])
        ff = part if ff is None else ff + part
    x2 = x1 + gate2 * ff
    if final:
        x2 = _rms_rows(x2, D_MODEL) * gf_ref[...]
    o_ref[...] = x2


def _mix_ffn(x, ys, mod, lw, gf, *, nb, n, final):
    T = nb * n
    tm = TOKEN_TILE
    npt = n // tm

    def tok(width):
        return pl.BlockSpec((tm, width), lambda i: (i, 0))

    in_specs = [tok(D_MODEL), tok(256), tok(256), tok(256), tok(256), mod.spec(lambda i: i // npt),
                lw.spec("g2"), lw.spec("w_out"), lw.spec("w_gu", col_blocks=2, col_block=0),
                lw.spec("w_gu", col_blocks=2, col_block=1), lw.spec("w_down")]
    args = [x, *ys, mod.table, lw["g2"], lw["w_out"], lw["w_gu"], lw["w_gu"], lw["w_down"]]
    if final:
        in_specs.append(_resident((1, D_MODEL)))
        args.append(gf)
    return pl.pallas_call(
        functools.partial(_mix_ffn_kernel, final=final),
        grid=(T // tm,),
        in_specs=in_specs,
        out_specs=tok(D_MODEL),
        out_shape=jax.ShapeDtypeStruct((T, D_MODEL), F32),
        compiler_params=_params("arbitrary"),
        name="mix_ffn_final" if final else "mix_ffn",
    )(*args)


def _block_diag(w):
    L, G, c, e = w.shape
    return jnp.einsum('lgce,gh->lgche', w, jnp.eye(G, dtype=w.dtype)).reshape(L, G * c, G * e)


def _rot_cols(w):
    return jnp.concatenate([-w[..., 16:32], w[..., 0:16]], axis=-1)


def _stack_weights(p):
    w_in = p["w_in"]
    o1 = MLA_Q_RANK
    o2 = o1 + MLA_KV_RANK
    o3 = o2 + MLA_ROPE
    c_q, c_kv, k_r, rest = w_in[..., :o1], w_in[..., o1:o2], w_in[..., o2:o3], w_in[..., o3:]
    z = lambda n: jnp.zeros((DEPTH, D_MODEL, n), F32)
    w_in_eff = jnp.concatenate([c_q, k_r, z(32), c_kv, z(64), _rot_cols(k_r), z(32), rest], axis=-1)

    w_uq = p["mla_w_uq"]
    qd = MLA_NOPE + MLA_ROPE
    wq_parts, wqr_parts = [], []
    zq = lambda n: jnp.zeros((DEPTH, MLA_Q_RANK, n), F32)
    for h in range(MLA_HEADS):
        wh = w_uq[..., h * qd:(h + 1) * qd]
        wq_parts += [wh, zq(MLA_SLOT - qd)]
        wqr_parts += [zq(MLA_NOPE), _rot_cols(wh[..., MLA_NOPE:]), zq(MLA_SLOT - qd)]
    pad_rows = lambda w: jnp.pad(w, ((0, 0), (0, 256 - MLA_Q_RANK), (0, 0)))
    w_ukv = p["mla_w_ukv"]
    wk_parts, wv_parts = [], []
    zk = jnp.zeros((DEPTH, MLA_KV_RANK, MLA_SLOT - MLA_NOPE), F32)
    for h in range(MLA_HEADS):
        base = h * (MLA_NOPE + MLA_V)
        wk_parts += [w_ukv[..., base:base + MLA_NOPE], zk]
        wv_parts.append(w_ukv[..., base + MLA_NOPE:base + MLA_NOPE + MLA_V])

    w_r, w_i, b_r, b_i = p["lru_w_r"], p["lru_w_i"], p["lru_b_r"], p["lru_b_i"]
    w_gate = jnp.concatenate([_block_diag(w_r[:, 0]), _block_diag(w_r[:, 1]),
                              _block_diag(w_i[:, 0]), _block_diag(w_i[:, 1])], axis=-1)
    b_gate = jnp.concatenate([b_r[:, 0], b_r[:, 1], b_i[:, 0], b_i[:, 1]], axis=-1)
    row = lambda v: v[:, None, :]
    return {
        "g1": row(p["norm1_g"]),
        "g2": row(p["norm2_g"]),
        "w_in": w_in_eff.astype(BF16),
        "gq": row(jnp.pad(p["mla_q_norm_g"], ((0, 0), (0, 256 - MLA_Q_RANK)))),
        "gkv": row(p["mla_kv_norm_g"]),
        "wq": pad_rows(jnp.concatenate(wq_parts, axis=-1)).astype(BF16),
        "wqr": pad_rows(jnp.concatenate(wqr_parts, axis=-1)).astype(BF16),
        "wkv": jnp.concatenate(wk_parts + wv_parts, axis=-1).astype(BF16),
        "conv_w": p["lru_conv_w"],
        "conv_b": row(p["lru_conv_b"]),
        "w_gate": w_gate.astype(BF16),
        "b_gate": row(b_gate),
        "lru_lambda": p["lru_lambda"],
        "w_pool": _block_diag(p["pool_w"]).astype(BF16),
        "pool_scale": row(p["pool_scale"]),
        "diff_lambda": p["diff_lambda"],
        "diff_g": row(jnp.tile(p["diff_norm_g"], (1, DIFF_HEADS))),
        "w_out": p["w_out"].astype(BF16),
        "w_gu": p["w_gu"].astype(BF16),
        "w_down": p["w_down"].astype(BF16),
    }


def _rope_tables(n, positional):
    quarter = MLA_ROPE // 4
    if positional:
        t = jnp.arange(n)
        row = (t // GRID_W).astype(F32)
        col = (t % GRID_W).astype(F32)
        inv = ROPE_BASE ** (-jnp.arange(quarter, dtype=F32) / quarter)
        ang = jnp.concatenate([row[:, None] * inv, col[:, None] * inv], axis=-1)
        cos, sin = jnp.cos(ang), jnp.sin(ang)
    else:
        cos, sin = jnp.ones((n, 16), F32), jnp.zeros((n, 16), F32)
    scale = LOG2E / math.sqrt(MLA_NOPE + MLA_ROPE)
    place = np.zeros((32, TAB_WIDTH), np.float32)
    offset = np.zeros((1, TAB_WIDTH), np.float32)
    offset[0, 0:64] = offset[0, 96:128] = scale
    for i in range(16):
        for half in (64, 80):
            place[i, half + i] = scale
            place[16 + i, 128 + half + i] = scale
            place[i, 256 + half + i] = 1.0
            place[16 + i, 384 + half + i] = 1.0
        for grp in range(8):
            place[i, 512 + 32 * grp + i] = place[i, 512 + 32 * grp + 16 + i] = 1.0
            place[16 + i, 768 + 32 * grp + i] = -1.0
            place[16 + i, 1024 + 32 * grp + 16 + i] = 1.0
    return jnp.dot(jnp.concatenate([cos, sin], axis=1), place, precision=lax.Precision.HIGHEST) + offset


def _layer(x, mod, lw, tabs, layer_idx, ctx, gf, *, nb, n, final):
    emit_cache = ctx is None
    tok_nb, tok_n = (1, nb * n) if mod.shared else (nb, n)
    outs = _inproj(x, mod, lw, tabs, nb=tok_nb, n=tok_n, emit_cache=emit_cache)
    q, k, vt, u_lru, u_pool, dq, dk, dvt = outs[:8]
    lam_init = 0.8 - 0.6 * math.exp(-0.3 * layer_idx)
    if ctx is None:
        h0 = jnp.zeros((1, 2, LRU_WIDTH), F32)
        h0_block = lambda b: 0
        mla_ctx = diff_ctx = None
    else:
        ckv, kr_pad, cdk, cdv, h0 = ctx
        p = ckv.shape[0] // (nb * DEPTH)
        h0_block = lambda b: b * DEPTH + layer_idx
        kc, vtc, dkc, dvtc = _ctx_prep(ckv, kr_pad, cdk, cdv, lw, nb=nb, p=p)
        mla_ctx = (kc, vtc)
        diff_ctx = (dkc, dvtc)
    y_mla = _mla_attn(q, k, vt, mla_ctx, nb=nb, n=n)
    y_lru, st = _lru(u_lru, h0, h0_block, lw, nb=nb, n=n)
    y_pool = _pool(u_pool, lw, nb=nb, n=n)
    y_diff = _diff_attn(dq, dk, dvt, diff_ctx, lw, nb=nb, n=n, lam_init=lam_init)
    x2 = _mix_ffn(x, (y_mla, y_lru, y_pool, y_diff), mod, lw, gf, nb=tok_nb, n=tok_n, final=final)
    cache = (outs[8], outs[9][:, 64:96], outs[10], outs[11], st) if emit_cache else None
    return x2, cache


def kernel(x_prompt, x_sample, cache_mla_ckv, cache_mla_krope, cache_diff_k, cache_diff_v, state_lru,
           c, c_ctx, w_ada, b_ada, norm1_g, norm2_g, w_in, mla_q_norm_g, mla_w_uq, mla_kv_norm_g,
           mla_w_ukv, lru_conv_w, lru_conv_b, lru_w_r, lru_b_r, lru_w_i, lru_b_i, lru_lambda, pool_w,
           pool_scale, diff_lambda, diff_norm_g, w_out, w_gu, w_down, final_norm_g):
    p = {
        "norm1_g": norm1_g, "norm2_g": norm2_g, "w_in": w_in, "mla_q_norm_g": mla_q_norm_g,
        "mla_w_uq": mla_w_uq, "mla_kv_norm_g": mla_kv_norm_g, "mla_w_ukv": mla_w_ukv,
        "lru_conv_w": lru_conv_w, "lru_conv_b": lru_conv_b, "lru_w_r": lru_w_r, "lru_b_r": lru_b_r,
        "lru_w_i": lru_w_i, "lru_b_i": lru_b_i, "lru_lambda": lru_lambda, "pool_w": pool_w,
        "pool_scale": pool_scale, "diff_lambda": diff_lambda, "diff_norm_g": diff_norm_g,
        "w_out": w_out, "w_gu": w_gu, "w_down": w_down,
    }
    Bp, Np, _ = x_prompt.shape
    Bs, Ns, _ = x_sample.shape
    P = cache_mla_ckv.shape[2]

    cond_all = jnp.concatenate([c, c_ctx[None, :], jnp.zeros((MOD_ROWS - Bs - 1, D_MODEL), F32)], axis=0)
    mod_table = _ada(cond_all, w_ada, b_ada).reshape(DEPTH * MOD_ROWS, 1, 6 * D_MODEL)
    tabs_p = _rope_tables(Bp * Np, positional=False)
    tabs_s = _rope_tables(Ns, positional=True)
    kr_pad = jnp.pad(cache_mla_krope, ((0, 0), (0, 0), (0, 0), (MLA_NOPE, MLA_SLOT - MLA_NOPE - MLA_ROPE)))
    flat = lambda a, w: a.reshape(Bs * DEPTH * P, w)
    ctx = (flat(cache_mla_ckv, MLA_KV_RANK), flat(kr_pad, MLA_SLOT), flat(cache_diff_k, 256),
           flat(cache_diff_v, 256), state_lru.reshape(Bs * DEPTH, 2, LRU_WIDTH))
    gf = final_norm_g[None, :]
    stacked = _stack_weights(p)

    xp = x_prompt.reshape(Bp * Np, D_MODEL)
    xs = x_sample.reshape(Bs * Ns, D_MODEL)
    caches = []
    for l in range(DEPTH):
        lw = _LayerWeights(stacked, l)
        final = l == DEPTH - 1
        mod_p = _Mod(mod_table, l * MOD_ROWS + Bs, shared=True)
        mod_s = _Mod(mod_table, l * MOD_ROWS, shared=False)
        xp, cache = _layer(xp, mod_p, lw, tabs_p, l, None, gf, nb=Bp, n=Np, final=final)
        caches.append(cache)
        xs, _ = _layer(xs, mod_s, lw, tabs_s, l, ctx, gf, nb=Bs, n=Ns, final=final)

    stack = lambda i, w: jnp.stack([cc[i].reshape(Bp, Np, w) for cc in caches], axis=1)
    new_mla_ckv = stack(0, MLA_KV_RANK)
    new_mla_krope = stack(1, MLA_ROPE)
    new_diff_k = stack(2, 256).reshape(Bp, DEPTH, Np, DIFF_HEADS, 2, DIFF_DIM)
    new_diff_v = stack(3, 256).reshape(Bp, DEPTH, Np, DIFF_HEADS, 2 * DIFF_DIM)
    new_state_lru = jnp.stack([cc[4] for cc in caches], axis=1)
    return (xp.reshape(Bp, Np, D_MODEL), xs.reshape(Bs, Ns, D_MODEL),
            new_mla_ckv, new_mla_krope, new_diff_k, new_diff_v, new_state_lru)
```
